```python
import math
import jax
import jax.numpy as jnp
from jax import lax
import numpy as np

D_MODEL = 1024
BATCH = 8
SEQ = 2048
DEPTH = 4

N_HEADS = 16
HEAD_DIM = D_MODEL // N_HEADS
Q_BLOCK = 128
N_MIXERS = 3
DILATED_PAIRS = ((128, 1), (512, 4), (2048, 16))
N_DIL_GROUPS = len(DILATED_PAIRS)
N_REL_BUCKETS = 32
REL_MAX_DISTANCE = 2048
D_FF = 7 * D_MODEL // 2
N_EXPERTS = 8
TOP_K = 2
RMS_EPS = 1e-6
NEG_INF = -1e30
N_SB_LAYERS = (DEPTH + 2) // 3
N_DIL_LAYERS = (DEPTH + 1) // 3
N_FOX_LAYERS = DEPTH // 3
N_DENSE_LAYERS = (DEPTH + 1) // 2
N_MOE_LAYERS = DEPTH // 2

kernel_name = 'hybrid_stickbreak_dilated_fox_moe_trunk'


def rmsnorm(x, g):
    xf = x.astype(jnp.float32)
    inv = lax.rsqrt(jnp.mean(xf * xf, axis=-1, keepdims=True) + RMS_EPS)
    return (xf * inv).astype(x.dtype) * g


def split_heads(t):
    return t.reshape(t.shape[:-1] + (N_HEADS, HEAD_DIM))


def merge_heads(t):
    return t.reshape(t.shape[:-2] + (N_HEADS * HEAD_DIM,))


def query_blocks(t):
    b, s = t.shape[:2]
    t = t.reshape((b, s // Q_BLOCK, Q_BLOCK) + t.shape[2:])
    return jnp.moveaxis(t, 1, 0)


def unblock(t):
    t = jnp.moveaxis(t, 0, 1)
    return t.reshape((t.shape[0], t.shape[1] * t.shape[2]) + t.shape[3:])


def stick_breaking_attention(q, k, v):
    s = q.shape[1]
    scale = 1.0 / math.sqrt(HEAD_DIM)
    kpos = jnp.arange(s, dtype=jnp.int32)
    starts = jnp.arange(s // Q_BLOCK, dtype=jnp.int32) * Q_BLOCK

    def one_block(args):
        q_blk, start = args
        z = jnp.einsum('bqhd,bkhd->bhqk', q_blk, k, preferred_element_type=jnp.float32) * scale
        qpos = start + jnp.arange(Q_BLOCK, dtype=jnp.int32)
        strict = kpos[None, :] < qpos[:, None]
        log_keep = jnp.where(strict, jax.nn.log_sigmoid(-z), 0.0)
        log_remain = lax.cumsum(log_keep, axis=3, reverse=True) - log_keep
        w = jnp.where(strict, jnp.exp(jax.nn.log_sigmoid(z) + log_remain), 0.0)
        return jnp.einsum('bhqk,bkhd->bqhd', w.astype(v.dtype), v)

    return unblock(lax.map(one_block, (query_blocks(q), starts)))


def stick_breaking_mixer(h, w_qkv, w_o):
    q, k, v = jnp.split(h @ w_qkv, 3, axis=-1)
    o = stick_breaking_attention(split_heads(q), split_heads(k), split_heads(v))
    return merge_heads(o) @ w_o


def t5_causal_bucket(distance):
    max_exact = N_REL_BUCKETS // 2
    d = jnp.maximum(distance, 1).astype(jnp.float32)
    log_b = max_exact + (jnp.log(d / max_exact) / math.log(REL_MAX_DISTANCE / max_exact)
                         * (N_REL_BUCKETS - max_exact)).astype(jnp.int32)
    log_b = jnp.minimum(log_b, N_REL_BUCKETS - 1)
    return jnp.where(distance < max_exact, distance, log_b)


def to_residue_classes(t, dil, l_pad):
    b, s = t.shape[:2]
    t = jnp.moveaxis(t.reshape((b, s // dil, dil) + t.shape[2:]), 2, 1)
    return jnp.pad(t, ((0, 0), (0, 0), (0, l_pad - s // dil)) + ((0, 0),) * (t.ndim - 3))


def from_residue_classes(t, seq):
    b, dil = t.shape[:2]
    rest = t.shape[4:]
    t = t.reshape((b, dil, -1) + rest)[:, :, : seq // dil]
    return jnp.moveaxis(t, 1, 2).reshape((b, seq) + rest)


def dilated_branch(q, k, v, rel_bias, window, dil):
    b, s = q.shape[:2]
    span = window // dil
    l_sub = s // dil
    nb = -(-l_sub // Q_BLOCK)
    l_pad = nb * Q_BLOCK
    scale = 1.0 / math.sqrt(HEAD_DIM)

    def blk(t):
        return to_residue_classes(t, dil, l_pad).reshape((b, dil, nb, Q_BLOCK, N_HEADS, HEAD_DIM))

    def band(t):
        t = blk(t)
        prev = jnp.pad(t, ((0, 0), (0, 0), (1, 0), (0, 0), (0, 0), (0, 0)))[:, :, :-1]
        return jnp.concatenate([prev, t], axis=3)

    qb, kb, vb = blk(q), band(k), band(v)
    qi = jnp.arange(Q_BLOCK, dtype=jnp.int32)
    kj = jnp.arange(2 * Q_BLOCK, dtype=jnp.int32)
    delta = qi[:, None] + Q_BLOCK - kj[None, :]
    blocks = jnp.arange(nb, dtype=jnp.int32)
    key_exists = (blocks[:, None] - 1) * Q_BLOCK + kj[None, :] >= 0
    mask = ((delta >= 0) & (delta <= span))[None] & key_exists[:, None, :]
    bias = jnp.moveaxis(rel_bias[t5_causal_bucket(jnp.maximum(delta, 0) * dil)], -1, 0)
    logits = (jnp.einsum('brnqhd,brnkhd->brnhqk', qb, kb, preferred_element_type=jnp.float32) * scale
              + bias.astype(jnp.float32))
    logits = jnp.where(mask[None, None, :, None], logits, NEG_INF)
    m = jnp.max(logits, axis=-1)
    p = jnp.exp(logits - m[..., None])
    l = jnp.sum(p, axis=-1)
    acc = jnp.einsum('brnhqk,brnkhd->brnqhd', p, vb, preferred_element_type=jnp.float32)
    m = jnp.moveaxis(m, 3, 4)
    l = jnp.moveaxis(l, 3, 4)
    return (from_residue_classes(acc, s), from_residue_classes(m, s), from_residue_classes(l, s))


def dilated_mixer(h, w_in, q_norm, k_norm, rel_bias, w_o):
    proj = h @ w_in
    v = split_heads(proj[..., 2 * N_DIL_GROUPS * D_MODEL:])
    accs, ms, ls = [], [], []
    for g, (window, dil) in enumerate(DILATED_PAIRS):
        q = rmsnorm(split_heads(proj[..., (2 * g) * D_MODEL:(2 * g + 1) * D_MODEL]), q_norm[g])
        k = rmsnorm(split_heads(proj[..., (2 * g + 1) * D_MODEL:(2 * g + 2) * D_MODEL]), k_norm[g])
        acc, m, l = dilated_branch(q, k, v, rel_bias, window, dil)
        accs.append(acc)
        ms.append(m)
        ls.append(l)
    m_all = jnp.stack(ms)
    wts = jnp.exp(m_all - jnp.max(m_all, axis=0, keepdims=True))
    num = jnp.sum(wts[..., None] * jnp.stack(accs), axis=0)
    den = jnp.sum(wts * jnp.stack(ls), axis=0)
    o = (num / den[..., None]).astype(h.dtype)
    return merge_heads(o) @ w_o


def forgetting_attention(q, k, v, log_f):
    s = q.shape[1]
    scale = 1.0 / math.sqrt(HEAD_DIM)
    kpos = jnp.arange(s, dtype=jnp.int32)
    starts = jnp.arange(s // Q_BLOCK, dtype=jnp.int32) * Q_BLOCK
    cum = lax.cumsum(log_f, axis=1)
    cum_keys = jnp.moveaxis(cum, 1, 2)

    def one_block(args):
        q_blk, cum_q, start = args
        logits = jnp.einsum('bqhd,bkhd->bhqk', q_blk, k, preferred_element_type=jnp.float32) * scale
        decay = jnp.moveaxis(cum_q, 1, 2)[..., :, None] - cum_keys[:, :, None, :]
        qpos = start + jnp.arange(Q_BLOCK, dtype=jnp.int32)
        causal = kpos[None, :] <= qpos[:, None]
        p = jax.nn.softmax(jnp.where(causal, logits + decay, NEG_INF), axis=-1)
        return jnp.einsum('bhqk,bkhd->bqhd', p.astype(v.dtype), v)

    return unblock(lax.map(one_block, (query_blocks(q), query_blocks(cum), starts)))


def forgetting_mixer(h, w_in, b_f, q_norm, k_norm, w_o):
    proj = h @ w_in
    q = rmsnorm(split_heads(proj[..., :D_MODEL]), q_norm)
    k = rmsnorm(split_heads(proj[..., D_MODEL:2 * D_MODEL]), k_norm)
    v = split_heads(proj[..., 2 * D_MODEL:3 * D_MODEL])
    log_f = jax.nn.log_sigmoid((proj[..., 3 * D_MODEL:] + b_f).astype(jnp.float32))
    o = forgetting_attention(q, k, v, log_f)
    return merge_heads(o) @ w_o


def swiglu(h, w_gate, w_up, w_down):
    return (jax.nn.silu(h @ w_gate) * (h @ w_up)) @ w_down


def moe_swiglu(h, router, w_gate, w_up, w_down):
    logits = (h @ router).astype(jnp.float32)
    top_logits, top_idx = lax.top_k(logits, TOP_K)
    gates = jax.nn.softmax(top_logits, axis=-1)
    combine = jnp.einsum('bske,bsk->bse', jax.nn.one_hot(top_idx, N_EXPERTS, dtype=jnp.float32), gates)
    y = jnp.zeros_like(h)
    for e in range(N_EXPERTS):
        y = y + combine[..., e:e + 1].astype(h.dtype) * swiglu(h, w_gate[e], w_up[e], w_down[e])
    return y


def setup_inputs(seed: int = 0) -> dict:
    key = jax.random.key(seed)
    ks = jax.random.split(key, 24)
    f32 = jnp.float32

    def w(k, shape, fan_in):
        return jax.random.normal(k, shape, f32) * fan_in ** -0.5

    def gain(k, shape):
        return 1.0 + 0.1 * jax.random.normal(k, shape, f32)

    return {
        'x': jax.random.normal(ks[0], (BATCH, SEQ, D_MODEL), f32),
        'sb_w_qkv': w(ks[1], (N_SB_LAYERS, D_MODEL, 3 * D_MODEL), D_MODEL),
        'sb_w_o': w(ks[2], (N_SB_LAYERS, D_MODEL, D_MODEL), D_MODEL),
        'dil_w_in': w(ks[3], (N_DIL_LAYERS, D_MODEL, (2 * N_DIL_GROUPS + 1) * D_MODEL), D_MODEL),
        'dil_q_norm': gain(ks[4], (N_DIL_LAYERS, N_DIL_GROUPS, HEAD_DIM)),
        'dil_k_norm': gain(ks[5], (N_DIL_LAYERS, N_DIL_GROUPS, HEAD_DIM)),
        'dil_w_o': w(ks[6], (N_DIL_LAYERS, D_MODEL, D_MODEL), D_MODEL),
        'fox_w_in': w(ks[7], (N_FOX_LAYERS, D_MODEL, 3 * D_MODEL + N_HEADS), D_MODEL),
        'fox_b_f': 2.0 + 0.5 * jax.random.normal(ks[8], (N_FOX_LAYERS, N_HEADS), f32),
        'fox_q_norm': gain(ks[9], (N_FOX_LAYERS, HEAD_DIM)),
        'fox_k_norm': gain(ks[10], (N_FOX_LAYERS, HEAD_DIM)),
        'fox_w_o': w(ks[11], (N_FOX_LAYERS, D_MODEL, D_MODEL), D_MODEL),
        'rel_bias': 0.5 * jax.random.normal(ks[12], (N_REL_BUCKETS, N_HEADS), f32),
        'attn_norm': gain(ks[13], (DEPTH, D_MODEL)),
        'ffn_norm': gain(ks[14], (DEPTH, D_MODEL)),
        'mlp_w_gate': w(ks[15], (N_DENSE_LAYERS, D_MODEL, D_FF), D_MODEL),
        'mlp_w_up': w(ks[16], (N_DENSE_LAYERS, D_MODEL, D_FF), D_MODEL),
        'mlp_w_down': w(ks[17], (N_DENSE_LAYERS, D_FF, D_MODEL), D_FF),
        'moe_router': w(ks[18], (N_MOE_LAYERS, D_MODEL, N_EXPERTS), D_MODEL),
        'moe_w_gate': w(ks[19], (N_MOE_LAYERS, N_EXPERTS, D_MODEL, D_FF), D_MODEL),
        'moe_w_up': w(ks[20], (N_MOE_LAYERS, N_EXPERTS, D_MODEL, D_FF), D_MODEL),
        'moe_w_down': w(ks[21], (N_MOE_LAYERS, N_EXPERTS, D_FF, D_MODEL), D_FF),
    }


def reference(x, sb_w_qkv, sb_w_o, dil_w_in, dil_q_norm, dil_k_norm, dil_w_o,
              fox_w_in, fox_b_f, fox_q_norm, fox_k_norm, fox_w_o, rel_bias,
              attn_norm, ffn_norm, mlp_w_gate, mlp_w_up, mlp_w_down,
              moe_router, moe_w_gate, moe_w_up, moe_w_down):
    for i in range(DEPTH):
        h = rmsnorm(x, attn_norm[i])
        kind = i % N_MIXERS
        j = i // N_MIXERS
        if kind == 0:
            mix = stick_breaking_mixer(h, sb_w_qkv[j], sb_w_o[j])
        elif kind == 1:
            mix = dilated_mixer(h, dil_w_in[j], dil_q_norm[j], dil_k_norm[j], rel_bias, dil_w_o[j])
        else:
            mix = forgetting_mixer(h, fox_w_in[j], fox_b_f[j], fox_q_norm[j], fox_k_norm[j], fox_w_o[j])
        x = x + mix
        h = rmsnorm(x, ffn_norm[i])
        f = i // 2
        if i % 2 == 0:
            x = x + swiglu(h, mlp_w_gate[f], mlp_w_up[f], mlp_w_down[f])
        else:
            x = x + moe_swiglu(h, moe_router[f], moe_w_gate[f], moe_w_up[f], moe_w_down[f])
    return x
```

```python
import functools
import math

import jax
import jax.numpy as jnp
from jax import lax
from jax.experimental import pallas as pl
from jax.experimental.pallas import tpu as pltpu

D_MODEL = 1024
N_HEADS = 16
HEAD_DIM = 64
LANES = 128
HEAD_PAIRS = D_MODEL // LANES
D_FF = 3584
N_EXPERTS = 8
N_REL_BUCKETS = 32
REL_MAX_DISTANCE = 2048
DILATED_PAIRS = ((128, 1), (512, 4), (2048, 16))
DIL_SPAN = 128
RMS_EPS = 1e-6
NEG_INF = -1e30
QK_SCALE = 1.0 / math.sqrt(HEAD_DIM)

ROW_TILE = 1024
COL_TILE = 1024
FF_TILE = 512
NORM_CHUNK = 256
ATT_BLOCK = 256
DIL_BLOCK = 128
VMEM_LIMIT = 56 * 1024 * 1024

F32 = jnp.float32
BF16 = jnp.bfloat16


def _params(*semantics):
    return pltpu.CompilerParams(dimension_semantics=semantics, vmem_limit_bytes=VMEM_LIMIT)


def _rms_normalize(x, gain):
    inv = lax.rsqrt(jnp.mean(x * x, axis=-1, keepdims=True) + RMS_EPS)
    return x * inv * gain


def _softplus(z):
    return jnp.maximum(z, 0.0) + jnp.log(1.0 + jnp.exp(-jnp.abs(z)))


def _norm_proj_body(x_ref, g_ref, w_ref, cs_ref, bd_ref, o_ref, h_scr, *, n_norm):
    j = pl.program_id(1)

    @pl.when(j == 0)
    def _():
        h_scr[...] = _rms_normalize(x_ref[...], g_ref[...]).astype(BF16)

    acc = jnp.dot(h_scr[...], w_ref[...], preferred_element_type=F32)

    def plain():
        o_ref[...] = (acc * cs_ref[...]).astype(o_ref.dtype)

    def head_normed():
        for c in range(COL_TILE // NORM_CHUNK):
            sl = slice(c * NORM_CHUNK, (c + 1) * NORM_CHUNK)
            a = acc[:, sl]
            ss = jnp.dot((a * a).astype(BF16), bd_ref[...], preferred_element_type=F32)
            inv = lax.rsqrt(ss * (1.0 / HEAD_DIM) + RMS_EPS)
            o_ref[:, sl] = (a * inv * cs_ref[:, sl]).astype(o_ref.dtype)

    if n_norm == 0:
        plain()
    else:
        pl.when(j < n_norm)(head_normed)
        pl.when(j >= n_norm)(plain)


def _norm_proj(x, gain, w, col_scale, n_norm):
    t, d = x.shape
    n = w.shape[1]
    head_id = jnp.arange(NORM_CHUNK) // HEAD_DIM
    block_diag = (head_id[:, None] == head_id[None, :]).astype(BF16)
    return pl.pallas_call(
        functools.partial(_norm_proj_body, n_norm=n_norm),
        out_shape=jax.ShapeDtypeStruct((t, n), BF16),
        grid=(t // ROW_TILE, n // COL_TILE),
        in_specs=[
            pl.BlockSpec((ROW_TILE, d), lambda i, j: (i, 0)),
            pl.BlockSpec((1, d), lambda i, j: (0, 0)),
            pl.BlockSpec((d, COL_TILE), lambda i, j: (0, j)),
            pl.BlockSpec((1, COL_TILE), lambda i, j: (0, j)),
            pl.BlockSpec((NORM_CHUNK, NORM_CHUNK), lambda i, j: (0, 0)),
        ],
        out_specs=pl.BlockSpec((ROW_TILE, COL_TILE), lambda i, j: (i, j)),
        scratch_shapes=[pltpu.VMEM((ROW_TILE, d), BF16)],
        compiler_params=_params("parallel", "arbitrary"),
        name="norm_proj",
    )(x, gain.reshape(1, d), w.astype(BF16), col_scale.reshape(1, n).astype(F32), block_diag)


def _out_proj_body(x_ref, o_ref, w_ref, y_ref):
    y_ref[...] = x_ref[...] + jnp.dot(o_ref[...], w_ref[...], preferred_element_type=F32)


def _out_proj_residual(x, o, w):
    t, d = x.shape
    return pl.pallas_call(
        _out_proj_body,
        out_shape=jax.ShapeDtypeStruct((t, d), F32),
        grid=(t // ROW_TILE,),
        in_specs=[
            pl.BlockSpec((ROW_TILE, d), lambda i: (i, 0)),
            pl.BlockSpec((ROW_TILE, d), lambda i: (i, 0)),
            pl.BlockSpec((d, d), lambda i: (0, 0)),
        ],
        out_specs=pl.BlockSpec((ROW_TILE, d), lambda i: (i, 0)),
        compiler_params=_params("parallel"),
        name="out_proj",
    )(x, o, w.astype(BF16))


def _swiglu_hidden(h, wg, wu):
    g = jnp.dot(h, wg, preferred_element_type=F32)
    u = jnp.dot(h, wu, preferred_element_type=F32)
    return g * (1.0 / (1.0 + jnp.exp(-g))) * u


def _ffn_body(x_ref, g_ref, wg_ref, wu_ref, wd_ref, y_ref, h_scr):
    f = pl.program_id(1)

    @pl.when(f == 0)
    def _():
        x = x_ref[...]
        h_scr[...] = _rms_normalize(x, g_ref[...]).astype(BF16)
        y_ref[...] = x

    a = _swiglu_hidden(h_scr[...], wg_ref[...], wu_ref[...]).astype(BF16)
    y_ref[...] += jnp.dot(a, wd_ref[...], preferred_element_type=F32)


def _ffn_residual(x, gain, w_gate, w_up, w_down):
    t, d = x.shape
    ff = w_gate.shape[1]
    return pl.pallas_call(
        _ffn_body,
        out_shape=jax.ShapeDtypeStruct((t, d), F32),
        grid=(t // ROW_TILE, ff // FF_TILE),
        in_specs=[
            pl.BlockSpec((ROW_TILE, d), lambda i, f: (i, 0)),
            pl.BlockSpec((1, d), lambda i, f: (0, 0)),
            pl.BlockSpec((d, FF_TILE), lambda i, f: (0, f)),
            pl.BlockSpec((d, FF_TILE), lambda i, f: (0, f)),
            pl.BlockSpec((FF_TILE, d), lambda i, f: (f, 0)),
        ],
        out_specs=pl.BlockSpec((ROW_TILE, d), lambda i, f: (i, 0)),
        scratch_shapes=[pltpu.VMEM((ROW_TILE, d), BF16)],
        compiler_params=_params("parallel", "arbitrary"),
        name="ffn",
    )(x, gain.reshape(1, d), w_gate.astype(BF16), w_up.astype(BF16), w_down.astype(BF16))


def _router_body(x_ref, g_ref, r_ref, c_ref):
    h = _rms_normalize(x_ref[...], g_ref[...])
    logits = jnp.dot(h, r_ref[...], preferred_element_type=F32, precision=lax.Precision.HIGHEST)
    lane = lax.broadcasted_iota(jnp.int32, logits.shape, 1).astype(F32)
    logits = jnp.where(lane < N_EXPERTS, logits, -jnp.inf)
    m1 = jnp.max(logits, axis=-1, keepdims=True)
    i1 = jnp.min(jnp.where(logits == m1, lane, float(LANES)), axis=-1, keepdims=True)
    rest = jnp.where(lane == i1, -jnp.inf, logits)
    m2 = jnp.max(rest, axis=-1, keepdims=True)
    i2 = jnp.min(jnp.where(rest == m2, lane, float(LANES)), axis=-1, keepdims=True)
    e = jnp.exp(m2 - m1)
    g1 = 1.0 / (1.0 + e)
    c_ref[...] = jnp.where(lane == i1, g1, 0.0) + jnp.where(lane == i2, e * g1, 0.0)


def _router_combine(x, gain, router):
    t, d = x.shape
    r = jnp.zeros((d, LANES), F32).at[:, :N_EXPERTS].set(router)
    return pl.pallas_call(
        _router_body,
        out_shape=jax.ShapeDtypeStruct((t, LANES), F32),
        grid=(t // ROW_TILE,),
        in_specs=[
            pl.BlockSpec((ROW_TILE, d), lambda i: (i, 0)),
            pl.BlockSpec((1, d), lambda i: (0, 0)),
            pl.BlockSpec((d, LANES), lambda i: (0, 0)),
        ],
        out_specs=pl.BlockSpec((ROW_TILE, LANES), lambda i: (i, 0)),
        compiler_params=_params("parallel"),
        name="router",
    )(x, gain.reshape(1, d), r)


def _moe_body(x_ref, g_ref, c_ref, wg_ref, wu_ref, wd_ref, y_ref, h_scr):
    e = pl.program_id(1)
    f = pl.program_id(2)

    @pl.when((e == 0) & (f == 0))
    def _():
        x = x_ref[...]
        h_scr[...] = _rms_normalize(x, g_ref[...]).astype(BF16)
        y_ref[...] = x

    comb = c_ref[...]
    lane = lax.broadcasted_iota(jnp.int32, comb.shape, 1)
    gate = jnp.sum(jnp.where(lane == e, comb, 0.0), axis=-1, keepdims=True)
    a = (_swiglu_hidden(h_scr[...], wg_ref[...], wu_ref[...]) * gate).astype(BF16)
    y_ref[...] += jnp.dot(a, wd_ref[...], preferred_element_type=F32)


def _moe_residual(x, gain, combine, w_gate, w_up, w_down):
    t, d = x.shape
    n_e, _, ff = w_gate.shape
    return pl.pallas_call(
        _moe_body,
        out_shape=jax.ShapeDtypeStruct((t, d), F32),
        grid=(t // ROW_TILE, n_e, ff // FF_TILE),
        in_specs=[
            pl.BlockSpec((ROW_TILE, d), lambda i, e, f: (i, 0)),
            pl.BlockSpec((1, d), lambda i, e, f: (0, 0)),
            pl.BlockSpec((ROW_TILE, LANES), lambda i, e, f: (i, 0)),
            pl.BlockSpec((None, d, FF_TILE), lambda i, e, f: (e, 0, f)),
            pl.BlockSpec((None, d, FF_TILE), lambda i, e, f: (e, 0, f)),
            pl.BlockSpec((None, FF_TILE, d), lambda i, e, f: (e, f, 0)),
        ],
        out_specs=pl.BlockSpec((ROW_TILE, d), lambda i, e, f: (i, 0)),
        scratch_shapes=[pltpu.VMEM((ROW_TILE, d), BF16)],
        compiler_params=_params("parallel", "arbitrary", "arbitrary"),
        name="moe",
    )(x, gain.reshape(1, d), combine, w_gate.astype(BF16), w_up.astype(BF16), w_down.astype(BF16))


def _split_head_pair(q):
    is_first = lax.broadcasted_iota(jnp.int32, (1, LANES), 1) < HEAD_DIM
    zero = jnp.zeros_like(q)
    return jnp.where(is_first, q, zero), jnp.where(is_first, zero, q)


def _merge_head_pair(first, second):
    is_first = lax.broadcasted_iota(jnp.int32, (1, LANES), 1) < HEAD_DIM
    return jnp.where(is_first, first, second)


def _qk(q, k):
    return lax.dot_general(q, k, (((1,), (1,)), ((), ())), preferred_element_type=F32)


def _sb_body(q_ref, k_ref, v_ref, u_ref, o_ref, acc_scr, carry_scr):
    qi = pl.program_id(2)
    qs = _split_head_pair(q_ref[...])
    upper = u_ref[...]
    acc_scr[...] = jnp.zeros_like(acc_scr)
    carry_scr[...] = jnp.zeros_like(carry_scr)
    row = lax.broadcasted_iota(jnp.int32, (ATT_BLOCK, ATT_BLOCK), 0)
    col = lax.broadcasted_iota(jnp.int32, (ATT_BLOCK, ATT_BLOCK), 1)
    strict = col < row

    def process(kb, diagonal):
        start = pl.multiple_of(kb * ATT_BLOCK, ATT_BLOCK)
        k = k_ref[pl.ds(start, ATT_BLOCK), :]
        v = v_ref[pl.ds(start, ATT_BLOCK), :]
        for hh in range(2):
            z = _qk(qs[hh], k)
            log_beta = z - _softplus(z)
            log_keep = log_beta - z
            if diagonal:
                log_keep = jnp.where(strict, log_keep, 0.0)
            remain = jnp.dot(log_keep.astype(BF16), upper, preferred_element_type=F32)
            w = jnp.exp(log_beta + remain + carry_scr[hh])
            if diagonal:
                w = jnp.where(strict, w, 0.0)
            carry_scr[hh] += jnp.sum(log_keep, axis=-1, keepdims=True)
            acc_scr[hh] += jnp.dot(w.astype(BF16), v, preferred_element_type=F32)

    process(qi, True)

    def step(it, c):
        process(qi - 1 - it, False)
        return c

    lax.fori_loop(0, qi, step, 0)
    o_ref[...] = _merge_head_pair(acc_scr[0], acc_scr[1]).astype(o_ref.dtype)


def _sb_attention(proj, batch, seq):
    t = proj.shape[0]
    nq = seq // ATT_BLOCK
    idx = jnp.arange(ATT_BLOCK)
    upper = (idx[:, None] > idx[None, :]).astype(BF16)
    return pl.pallas_call(
        _sb_body,
        out_shape=jax.ShapeDtypeStruct((t, D_MODEL), BF16),
        grid=(batch, HEAD_PAIRS, nq),
        in_specs=[
            pl.BlockSpec((ATT_BLOCK, LANES), lambda b, p, i: (b * nq + i, p)),
            pl.BlockSpec((seq, LANES), lambda b, p, i: (b, HEAD_PAIRS + p)),
            pl.BlockSpec((seq, LANES), lambda b, p, i: (b, 2 * HEAD_PAIRS + p)),
            pl.BlockSpec((ATT_BLOCK, ATT_BLOCK), lambda b, p, i: (0, 0)),
        ],
        out_specs=pl.BlockSpec((ATT_BLOCK, LANES), lambda b, p, i: (b * nq + i, p)),
        scratch_shapes=[
            pltpu.VMEM((2, ATT_BLOCK, LANES), F32),
            pltpu.VMEM((2, ATT_BLOCK, 1), F32),
        ],
        compiler_params=_params("parallel", "parallel", "arbitrary"),
        name="sb_attention",
    )(proj, proj, proj, upper)


def _fox_gate_body(x_ref, g_ref, w_ref, b_ref, tri_ref, c_ref):
    h = _rms_normalize(x_ref[...], g_ref[...])
    logits = jnp.dot(h, w_ref[...], preferred_element_type=F32, precision=lax.Precision.HIGHEST)
    log_f = -_softplus(-(logits + b_ref[...]))
    seq = log_f.shape[0]
    carry = jnp.zeros((1, LANES), F32)
    for blk in range(seq // ATT_BLOCK):
        rows = slice(blk * ATT_BLOCK, (blk + 1) * ATT_BLOCK)
        c = jnp.dot(tri_ref[...], log_f[rows], preferred_element_type=F32,
                    precision=lax.Precision.HIGHEST) + carry
        c_ref[rows, :] = c
        carry = c[ATT_BLOCK - 1:ATT_BLOCK, :]


def _fox_cum_log_forget(x, gain, w_gate, b_gate, batch, seq):
    t, d = x.shape
    w = jnp.zeros((d, LANES), F32).at[:, :N_HEADS].set(w_gate)
    b = jnp.zeros((1, LANES), F32).at[0, :N_HEADS].set(b_gate)
    idx = jnp.arange(ATT_BLOCK)
    tri = (idx[:, None] >= idx[None, :]).astype(F32)
    return pl.pallas_call(
        _fox_gate_body,
        out_shape=jax.ShapeDtypeStruct((t, LANES), F32),
        grid=(batch,),
        in_specs=[
            pl.BlockSpec((seq, d), lambda i: (i, 0)),
            pl.BlockSpec((1, d), lambda i: (0, 0)),
            pl.BlockSpec((d, LANES), lambda i: (0, 0)),
            pl.BlockSpec((1, LANES), lambda i: (0, 0)),
            pl.BlockSpec((ATT_BLOCK, ATT_BLOCK), lambda i: (0, 0)),
        ],
        out_specs=pl.BlockSpec((seq, LANES), lambda i: (i, 0)),
        compiler_params=_params("parallel"),
        name="fox_gate",
    )(x, gain.reshape(1, d), w, b, tri)


def _fox_body(q_ref, k_ref, v_ref, cq_ref, ck_ref, o_ref, acc_scr, m_scr, l_scr):
    qi = pl.program_id(2)
    qs = _split_head_pair(q_ref[...])
    acc_scr[...] = jnp.zeros_like(acc_scr)
    m_scr[...] = jnp.full_like(m_scr, NEG_INF)
    l_scr[...] = jnp.zeros_like(l_scr)
    row = lax.broadcasted_iota(jnp.int32, (ATT_BLOCK, ATT_BLOCK), 0)
    col = lax.broadcasted_iota(jnp.int32, (ATT_BLOCK, ATT_BLOCK), 1)
    causal = col <= row

    def process(kb, diagonal):
        start = pl.multiple_of(kb * ATT_BLOCK, ATT_BLOCK)
        k = k_ref[pl.ds(start, ATT_BLOCK), :]
        v = v_ref[pl.ds(start, ATT_BLOCK), :]
        for hh in range(2):
            logits = _qk(qs[hh], k) + (cq_ref[0, hh] - ck_ref[0, hh, kb])
            if diagonal:
                logits = jnp.where(causal, logits, NEG_INF)
            m_old = m_scr[hh]
            m_new = jnp.maximum(m_old, jnp.max(logits, axis=-1, keepdims=True))
            alpha = jnp.exp(m_old - m_new)
            p = jnp.exp(logits - m_new)
            l_scr[hh] = alpha * l_scr[hh] + jnp.sum(p, axis=-1, keepdims=True)
            acc_scr[hh] = alpha * acc_scr[hh] + jnp.dot(p.astype(BF16), v, preferred_element_type=F32)
            m_scr[hh] = m_new

    def step(kb, c):
        process(kb, False)
        return c

    lax.fori_loop(0, qi, step, 0)
    process(qi, True)
    out = _merge_head_pair(acc_scr[0] / l_scr[0], acc_scr[1] / l_scr[1])
    o_ref[...] = out.astype(o_ref.dtype)


def _fox_attention(proj, cum, batch, seq):
    t = proj.shape[0]
    nq = seq // ATT_BLOCK
    cum_h = cum[:, :N_HEADS].reshape(batch, seq, N_HEADS).transpose(0, 2, 1)
    cum_q = cum_h.reshape(batch, N_HEADS, seq, 1)
    cum_k = cum_h.reshape(batch, N_HEADS, nq, 1, ATT_BLOCK)
    return pl.pallas_call(
        _fox_body,
        out_shape=jax.ShapeDtypeStruct((t, D_MODEL), BF16),
        grid=(batch, HEAD_PAIRS, nq),
        in_specs=[
            pl.BlockSpec((ATT_BLOCK, LANES), lambda b, p, i: (b * nq + i, p)),
            pl.BlockSpec((seq, LANES), lambda b, p, i: (b, HEAD_PAIRS + p)),
            pl.BlockSpec((seq, LANES), lambda b, p, i: (b, 2 * HEAD_PAIRS + p)),
            pl.BlockSpec((1, 2, ATT_BLOCK, 1), lambda b, p, i: (b, p, i, 0)),
            pl.BlockSpec((1, 2, nq, 1, ATT_BLOCK), lambda b, p, i: (b, p, 0, 0, 0)),
        ],
        out_specs=pl.BlockSpec((ATT_BLOCK, LANES), lambda b, p, i: (b * nq + i, p)),
        scratch_shapes=[
            pltpu.VMEM((2, ATT_BLOCK, LANES), F32),
            pltpu.VMEM((2, ATT_BLOCK, 1), F32),
            pltpu.VMEM((2, ATT_BLOCK, 1), F32),
        ],
        compiler_params=_params("parallel", "parallel", "arbitrary"),
        name="fox_attention",
    )(proj, proj, proj, cum_q, cum_k)


def _dil_body(q1_ref, k1_ref, q2_ref, k2_ref, q3_ref, k3_ref, v_ref, bias_ref, o_ref,
              q_scr, k_scr, v_scr, acc_scr, m_scr, l_scr):
    seq = v_ref.shape[0]
    for g, (q_ref, k_ref) in enumerate(((q1_ref, k1_ref), (q2_ref, k2_ref), (q3_ref, k3_ref))):
        q_scr[g] = q_ref[...].astype(F32)
        k_scr[g] = k_ref[...].astype(F32)
    v_scr[...] = v_ref[...].astype(F32)

    def block(g, q_start, k_start, n_keys, stride):
        q_rows = pl.ds(q_start, DIL_BLOCK, stride=stride) if stride > 1 else pl.ds(q_start, DIL_BLOCK)
        k_rows = pl.ds(k_start, n_keys, stride=stride) if stride > 1 else pl.ds(k_start, n_keys)
        qs = _split_head_pair(q_scr[g, q_rows, :].astype(BF16))
        k = k_scr[g, k_rows, :].astype(BF16)
        v = v_scr[k_rows, :].astype(BF16)
        outs, maxes, sums = [], [], []
        for hh in range(2):
            if n_keys == 2 * DIL_BLOCK:
                bias = bias_ref[g, hh]
            else:
                bias = bias_ref[g, hh, :, DIL_BLOCK:]
            logits = _qk(qs[hh], k) + bias
            m = jnp.max(logits, axis=-1, keepdims=True)
            p = jnp.exp(logits - m)
            sums.append(jnp.sum(p, axis=-1, keepdims=True))
            maxes.append(m)
            outs.append(jnp.dot(p.astype(BF16), v, preferred_element_type=F32))
        shape = (DIL_BLOCK, LANES)
        acc_scr[g, q_rows, :] = _merge_head_pair(outs[0], outs[1])
        m_scr[g, q_rows, :] = _merge_head_pair(jnp.broadcast_to(maxes[0], shape),
                                               jnp.broadcast_to(maxes[1], shape))
        l_scr[g, q_rows, :] = _merge_head_pair(jnp.broadcast_to(sums[0], shape),
                                               jnp.broadcast_to(sums[1], shape))

    block(0, 0, 0, DIL_BLOCK, 1)

    def g0_step(n, c):
        q_start = pl.multiple_of(n * DIL_BLOCK, DIL_BLOCK)
        k_start = pl.multiple_of((n - 1) * DIL_BLOCK, DIL_BLOCK)
        block(0, q_start, k_start, 2 * DIL_BLOCK, 1)
        return c

    lax.fori_loop(1, seq // DIL_BLOCK, g0_step, 0)

    for g in (1, 2):
        dil = DILATED_PAIRS[g][1]
        class_rows = DIL_BLOCK * dil
        for r in range(dil):
            block(g, r, r, DIL_BLOCK, dil)
            for n in range(1, seq // class_rows):
                block(g, r + n * class_rows, r + (n - 1) * class_rows, 2 * DIL_BLOCK, dil)

    def merge_step(c, carry):
        rows = pl.ds(pl.multiple_of(c * ATT_BLOCK, ATT_BLOCK), ATT_BLOCK)
        m0, m1, m2 = m_scr[0, rows, :], m_scr[1, rows, :], m_scr[2, rows, :]
        m = jnp.maximum(jnp.maximum(m0, m1), m2)
        w0, w1, w2 = jnp.exp(m0 - m), jnp.exp(m1 - m), jnp.exp(m2 - m)
        num = w0 * acc_scr[0, rows, :] + w1 * acc_scr[1, rows, :] + w2 * acc_scr[2, rows, :]
        den = w0 * l_scr[0, rows, :] + w1 * l_scr[1, rows, :] + w2 * l_scr[2, rows, :]
        o_ref[rows, :] = (num / den).astype(o_ref.dtype)
        return carry

    lax.fori_loop(0, seq // ATT_BLOCK, merge_step, 0)


def _t5_causal_bucket(distance):
    max_exact = N_REL_BUCKETS // 2
    d = jnp.maximum(distance, 1).astype(F32)
    log_b = max_exact + (jnp.log(d / max_exact) / math.log(REL_MAX_DISTANCE / max_exact)
                         * (N_REL_BUCKETS - max_exact)).astype(jnp.int32)
    log_b = jnp.minimum(log_b, N_REL_BUCKETS - 1)
    return jnp.where(distance < max_exact, distance, log_b)


def _dilated_bias(rel_bias):
    qi = jnp.arange(DIL_BLOCK, dtype=jnp.int32)
    kj = jnp.arange(2 * DIL_BLOCK, dtype=jnp.int32)
    delta = qi[:, None] + DIL_BLOCK - kj[None, :]
    in_band = (delta >= 0) & (delta <= DIL_SPAN)
    tables = []
    for _, dil in DILATED_PAIRS:
        bias = rel_bias[_t5_causal_bucket(jnp.maximum(delta, 0) * dil)]
        tables.append(jnp.where(in_band[..., None], bias, NEG_INF))
    return jnp.stack(tables).transpose(0, 3, 1, 2).astype(F32)


def _dilated_attention(proj, rel_bias, batch, seq):
    t = proj.shape[0]
    n_groups = len(DILATED_PAIRS)

    def section(s):
        return pl.BlockSpec((seq, LANES), lambda b, p: (b, s * HEAD_PAIRS + p))

    group_scratch = pltpu.VMEM((n_groups, seq, LANES), F32)
    return pl.pallas_call(
        _dil_body,
        out_shape=jax.ShapeDtypeStruct((t, D_MODEL), BF16),
        grid=(batch, HEAD_PAIRS),
        in_specs=[section(s) for s in range(2 * n_groups + 1)] + [
            pl.BlockSpec((n_groups, 2, DIL_BLOCK, 2 * DIL_BLOCK), lambda b, p: (0, p, 0, 0)),
        ],
        out_specs=pl.BlockSpec((seq, LANES), lambda b, p: (b, p)),
        scratch_shapes=[group_scratch, group_scratch, pltpu.VMEM((seq, LANES), F32),
                        group_scratch, group_scratch, group_scratch],
        compiler_params=_params("parallel", "parallel"),
        name="dilated_attention",
    )(*([proj] * (2 * n_groups + 1)), _dilated_bias(rel_bias))


def _tile_heads(v):
    return jnp.tile(v.astype(F32), N_HEADS)


def _sb_mixer(x, gain, w_qkv, w_o, batch, seq):
    ones = jnp.ones((D_MODEL,), F32)
    col_scale = jnp.concatenate([ones * QK_SCALE, ones, ones])
    proj = _norm_proj(x, gain, w_qkv, col_scale, 0)
    return _out_proj_residual(x, _sb_attention(proj, batch, seq), w_o)


def _dilated_mixer(x, gain, w_in, q_norm, k_norm, rel_bias, w_o, batch, seq):
    scales = []
    for g in range(len(DILATED_PAIRS)):
        scales += [_tile_heads(q_norm[g]) * QK_SCALE, _tile_heads(k_norm[g])]
    scales.append(jnp.ones((D_MODEL,), F32))
    proj = _norm_proj(x, gain, w_in, jnp.concatenate(scales), 2 * len(DILATED_PAIRS))
    return _out_proj_residual(x, _dilated_attention(proj, rel_bias, batch, seq), w_o)


def _fox_mixer(x, gain, w_in, b_f, q_norm, k_norm, w_o, batch, seq):
    col_scale = jnp.concatenate([_tile_heads(q_norm) * QK_SCALE, _tile_heads(k_norm),
                                 jnp.ones((D_MODEL,), F32)])
    proj = _norm_proj(x, gain, w_in[:, :3 * D_MODEL], col_scale, 2)
    cum = _fox_cum_log_forget(x, gain, w_in[:, 3 * D_MODEL:], b_f, batch, seq)
    return _out_proj_residual(x, _fox_attention(proj, cum, batch, seq), w_o)


def kernel(x, sb_w_qkv, sb_w_o, dil_w_in, dil_q_norm, dil_k_norm, dil_w_o, fox_w_in, fox_b_f,
           fox_q_norm, fox_k_norm, fox_w_o, rel_bias, attn_norm, ffn_norm, mlp_w_gate, mlp_w_up,
           mlp_w_down, moe_router, moe_w_gate, moe_w_up, moe_w_down):
    batch, seq, d = x.shape
    depth = attn_norm.shape[0]
    h = x.reshape(batch * seq, d)
    for i in range(depth):
        kind, j = i % 3, i // 3
        if kind == 0:
            h = _sb_mixer(h, attn_norm[i], sb_w_qkv[j], sb_w_o[j], batch, seq)
        elif kind == 1:
            h = _dilated_mixer(h, attn_norm[i], dil_w_in[j], dil_q_norm[j], dil_k_norm[j], rel_bias,
                               dil_w_o[j], batch, seq)
        else:
            h = _fox_mixer(h, attn_norm[i], fox_w_in[j], fox_b_f[j], fox_q_norm[j], fox_k_norm[j],
                           fox_w_o[j], batch, seq)
        f = i // 2
        if i % 2 == 0:
            h = _ffn_residual(h, ffn_norm[i], mlp_w_gate[f], mlp_w_up[f], mlp_w_down[f])
        else:
            combine = _router_combine(h, ffn_norm[i], moe_router[f])
            h = _moe_residual(h, ffn_norm[i], combine, moe_w_gate[f], moe_w_up[f], moe_w_down[f])
    return h.reshape(batch, seq, d)
```

```python
import functools
import math

import jax
import jax.numpy as jnp
from jax import lax
from jax.experimental import pallas as pl
from jax.experimental.pallas import tpu as pltpu

D_MODEL = 1024
N_HEADS = 16
HEAD_DIM = 64
LANES = 128
HEAD_PAIRS = D_MODEL // LANES
D_FF = 3584
N_EXPERTS = 8
N_REL_BUCKETS = 32
REL_MAX_DISTANCE = 2048
DILATED_PAIRS = ((128, 1), (512, 4), (2048, 16))
DIL_SPAN = 128
RMS_EPS = 1e-6
NEG_INF = -1e30
QK_SCALE = 1.0 / math.sqrt(HEAD_DIM)

ROW_TILE = 1024
COL_TILE = 1024
FF_TILE = 512
NORM_CHUNK = 256
ATT_BLOCK = 256
ATT_Q_BLOCK = 512
ATT_K_PER_Q = ATT_Q_BLOCK // ATT_BLOCK
DIL_BLOCK = 128
VMEM_LIMIT = 56 * 1024 * 1024

F32 = jnp.float32
BF16 = jnp.bfloat16


def _params(*semantics):
    return pltpu.CompilerParams(dimension_semantics=semantics, vmem_limit_bytes=VMEM_LIMIT)


def _rms_normalize(x, gain):
    inv = lax.rsqrt(jnp.mean(x * x, axis=-1, keepdims=True) + RMS_EPS)
    return x * inv * gain


def _softplus(z):
    return jnp.maximum(z, 0.0) + jnp.log(1.0 + jnp.exp(-jnp.abs(z)))


def _norm_proj_body(x_ref, g_ref, w_ref, cs_ref, bd_ref, o_ref, h_scr, *, n_norm):
    j = pl.program_id(1)

    @pl.when(j == 0)
    def _():
        h_scr[...] = _rms_normalize(x_ref[...], g_ref[...]).astype(BF16)

    acc = jnp.dot(h_scr[...], w_ref[...], preferred_element_type=F32)

    def plain():
        o_ref[...] = (acc * cs_ref[...]).astype(o_ref.dtype)

    def head_normed():
        for c in range(COL_TILE // NORM_CHUNK):
            sl = slice(c * NORM_CHUNK, (c + 1) * NORM_CHUNK)
            a = acc[:, sl]
            ss = jnp.dot((a * a).astype(BF16), bd_ref[...], preferred_element_type=F32)
            inv = lax.rsqrt(ss * (1.0 / HEAD_DIM) + RMS_EPS)
            o_ref[:, sl] = (a * inv * cs_ref[:, sl]).astype(o_ref.dtype)

    if n_norm == 0:
        plain()
    else:
        pl.when(j < n_norm)(head_normed)
        pl.when(j >= n_norm)(plain)


def _norm_proj(x, gain, w, col_scale, n_norm):
    t, d = x.shape
    n = w.shape[1]
    head_id = jnp.arange(NORM_CHUNK) // HEAD_DIM
    block_diag = (head_id[:, None] == head_id[None, :]).astype(BF16)
    return pl.pallas_call(
        functools.partial(_norm_proj_body, n_norm=n_norm),
        out_shape=jax.ShapeDtypeStruct((t, n), BF16),
        grid=(t // ROW_TILE, n // COL_TILE),
        in_specs=[
            pl.BlockSpec((ROW_TILE, d), lambda i, j: (i, 0)),
            pl.BlockSpec((1, d), lambda i, j: (0, 0)),
            pl.BlockSpec((d, COL_TILE), lambda i, j: (0, j)),
            pl.BlockSpec((1, COL_TILE), lambda i, j: (0, j)),
            pl.BlockSpec((NORM_CHUNK, NORM_CHUNK), lambda i, j: (0, 0)),
        ],
        out_specs=pl.BlockSpec((ROW_TILE, COL_TILE), lambda i, j: (i, j)),
        scratch_shapes=[pltpu.VMEM((ROW_TILE, d), BF16)],
        compiler_params=_params("parallel", "arbitrary"),
        name="norm_proj",
    )(x, gain.reshape(1, d), w.astype(BF16), col_scale.reshape(1, n).astype(F32), block_diag)


def _out_proj_body(x_ref, o_ref, w_ref, y_ref):
    y_ref[...] = x_ref[...] + jnp.dot(o_ref[...], w_ref[...], preferred_element_type=F32)


def _out_proj_residual(x, o, w):
    t, d = x.shape
    return pl.pallas_call(
        _out_proj_body,
        out_shape=jax.ShapeDtypeStruct((t, d), F32),
        grid=(t // ROW_TILE,),
        in_specs=[
            pl.BlockSpec((ROW_TILE, d), lambda i: (i, 0)),
            pl.BlockSpec((ROW_TILE, d), lambda i: (i, 0)),
            pl.BlockSpec((d, d), lambda i: (0, 0)),
        ],
        out_specs=pl.BlockSpec((ROW_TILE, d), lambda i: (i, 0)),
        compiler_params=_params("parallel"),
        name="out_proj",
    )(x, o, w.astype(BF16))


def _swiglu_hidden(h, wg, wu):
    g = jnp.dot(h, wg, preferred_element_type=F32)
    u = jnp.dot(h, wu, preferred_element_type=F32)
    return g * (1.0 / (1.0 + jnp.exp(-g))) * u


def _ffn_body(x_ref, g_ref, wg_ref, wu_ref, wd_ref, y_ref, h_scr):
    f = pl.program_id(1)

    @pl.when(f == 0)
    def _():
        x = x_ref[...]
        h_scr[...] = _rms_normalize(x, g_ref[...]).astype(BF16)
        y_ref[...] = x

    a = _swiglu_hidden(h_scr[...], wg_ref[...], wu_ref[...]).astype(BF16)
    y_ref[...] += jnp.dot(a, wd_ref[...], preferred_element_type=F32)


def _ffn_residual(x, gain, w_gate, w_up, w_down):
    t, d = x.shape
    ff = w_gate.shape[1]
    return pl.pallas_call(
        _ffn_body,
        out_shape=jax.ShapeDtypeStruct((t, d), F32),
        grid=(t // ROW_TILE, ff // FF_TILE),
        in_specs=[
            pl.BlockSpec((ROW_TILE, d), lambda i, f: (i, 0)),
            pl.BlockSpec((1, d), lambda i, f: (0, 0)),
            pl.BlockSpec((d, FF_TILE), lambda i, f: (0, f)),
            pl.BlockSpec((d, FF_TILE), lambda i, f: (0, f)),
            pl.BlockSpec((FF_TILE, d), lambda i, f: (f, 0)),
        ],
        out_specs=pl.BlockSpec((ROW_TILE, d), lambda i, f: (i, 0)),
        scratch_shapes=[pltpu.VMEM((ROW_TILE, d), BF16)],
        compiler_params=_params("parallel", "arbitrary"),
        name="ffn",
    )(x, gain.reshape(1, d), w_gate.astype(BF16), w_up.astype(BF16), w_down.astype(BF16))


def _router_body(x_ref, g_ref, r_ref, c_ref):
    h = _rms_normalize(x_ref[...], g_ref[...])
    logits = jnp.dot(h, r_ref[...], preferred_element_type=F32, precision=lax.Precision.HIGHEST)
    lane = lax.broadcasted_iota(jnp.int32, logits.shape, 1).astype(F32)
    logits = jnp.where(lane < N_EXPERTS, logits, -jnp.inf)
    m1 = jnp.max(logits, axis=-1, keepdims=True)
    i1 = jnp.min(jnp.where(logits == m1, lane, float(LANES)), axis=-1, keepdims=True)
    rest = jnp.where(lane == i1, -jnp.inf, logits)
    m2 = jnp.max(rest, axis=-1, keepdims=True)
    i2 = jnp.min(jnp.where(rest == m2, lane, float(LANES)), axis=-1, keepdims=True)
    e = jnp.exp(m2 - m1)
    g1 = 1.0 / (1.0 + e)
    c_ref[...] = jnp.where(lane == i1, g1, 0.0) + jnp.where(lane == i2, e * g1, 0.0)


def _router_combine(x, gain, router):
    t, d = x.shape
    r = jnp.zeros((d, LANES), F32).at[:, :N_EXPERTS].set(router)
    return pl.pallas_call(
        _router_body,
        out_shape=jax.ShapeDtypeStruct((t, LANES), F32),
        grid=(t // ROW_TILE,),
        in_specs=[
            pl.BlockSpec((ROW_TILE, d), lambda i: (i, 0)),
            pl.BlockSpec((1, d), lambda i: (0, 0)),
            pl.BlockSpec((d, LANES), lambda i: (0, 0)),
        ],
        out_specs=pl.BlockSpec((ROW_TILE, LANES), lambda i: (i, 0)),
        compiler_params=_params("parallel"),
        name="router",
    )(x, gain.reshape(1, d), r)


def _moe_body(x_ref, g_ref, c_ref, wg_ref, wu_ref, wd_ref, y_ref, h_scr):
    e = pl.program_id(1)
    f = pl.program_id(2)

    @pl.when((e == 0) & (f == 0))
    def _():
        x = x_ref[...]
        h_scr[...] = _rms_normalize(x, g_ref[...]).astype(BF16)
        y_ref[...] = x

    comb = c_ref[...]
    lane = lax.broadcasted_iota(jnp.int32, comb.shape, 1)
    gate = jnp.sum(jnp.where(lane == e, comb, 0.0), axis=-1, keepdims=True)
    a = (_swiglu_hidden(h_scr[...], wg_ref[...], wu_ref[...]) * gate).astype(BF16)
    y_ref[...] += jnp.dot(a, wd_ref[...], preferred_element_type=F32)


def _moe_residual(x, gain, combine, w_gate, w_up, w_down):
    t, d = x.shape
    n_e, _, ff = w_gate.shape
    return pl.pallas_call(
        _moe_body,
        out_shape=jax.ShapeDtypeStruct((t, d), F32),
        grid=(t // ROW_TILE, n_e, ff // FF_TILE),
        in_specs=[
            pl.BlockSpec((ROW_TILE, d), lambda i, e, f: (i, 0)),
            pl.BlockSpec((1, d), lambda i, e, f: (0, 0)),
            pl.BlockSpec((ROW_TILE, LANES), lambda i, e, f: (i, 0)),
            pl.BlockSpec((None, d, FF_TILE), lambda i, e, f: (e, 0, f)),
            pl.BlockSpec((None, d, FF_TILE), lambda i, e, f: (e, 0, f)),
            pl.BlockSpec((None, FF_TILE, d), lambda i, e, f: (e, f, 0)),
        ],
        out_specs=pl.BlockSpec((ROW_TILE, d), lambda i, e, f: (i, 0)),
        scratch_shapes=[pltpu.VMEM((ROW_TILE, d), BF16)],
        compiler_params=_params("parallel", "arbitrary", "arbitrary"),
        name="moe",
    )(x, gain.reshape(1, d), combine, w_gate.astype(BF16), w_up.astype(BF16), w_down.astype(BF16))


def _split_head_pair(q):
    is_first = lax.broadcasted_iota(jnp.int32, (1, LANES), 1) < HEAD_DIM
    zero = jnp.zeros_like(q)
    return jnp.where(is_first, q, zero), jnp.where(is_first, zero, q)


def _merge_head_pair(first, second):
    is_first = lax.broadcasted_iota(jnp.int32, (1, LANES), 1) < HEAD_DIM
    return jnp.where(is_first, first, second)


def _qk(q, k):
    return lax.dot_general(q, k, (((1,), (1,)), ((), ())), preferred_element_type=F32)


def _transpose_values(v_ref, vt_scr):
    for c in range(v_ref.shape[0] // ATT_BLOCK):
        rows = slice(c * ATT_BLOCK, (c + 1) * ATT_BLOCK)
        vt_scr[:, rows] = v_ref[rows, :].astype(F32).T.astype(vt_scr.dtype)


def _sb_body(q_ref, k_ref, v_ref, u_ref, o_ref, vt_scr, acc_scr, carry_scr):
    qi = pl.program_id(2)

    @pl.when(qi == 0)
    def _():
        _transpose_values(v_ref, vt_scr)

    qs = _split_head_pair(q_ref[...])
    upper = u_ref[...]
    acc_scr[...] = jnp.zeros_like(acc_scr)
    carry_scr[...] = jnp.zeros_like(carry_scr)
    key = lax.broadcasted_iota(jnp.int32, (ATT_BLOCK, ATT_Q_BLOCK), 0)
    query = lax.broadcasted_iota(jnp.int32, (ATT_BLOCK, ATT_Q_BLOCK), 1)

    def process(kb, diagonal):
        start = pl.multiple_of(kb * ATT_BLOCK, ATT_BLOCK)
        k = k_ref[pl.ds(start, ATT_BLOCK), :]
        if diagonal:
            strict = key + (kb * ATT_BLOCK - qi * ATT_Q_BLOCK) < query
        for hh in range(2):
            z = _qk(k, qs[hh])
            log_beta = z - _softplus(z)
            log_keep = log_beta - z
            if diagonal:
                log_keep = jnp.where(strict, log_keep, 0.0)
            remain = jnp.dot(upper, log_keep.astype(BF16), preferred_element_type=F32)
            w = jnp.exp(log_beta + remain + carry_scr[hh])
            if diagonal:
                w = jnp.where(strict, w, 0.0)
            carry_scr[hh] += jnp.sum(log_keep, axis=0, keepdims=True)
            vt = vt_scr[pl.ds(hh * HEAD_DIM, HEAD_DIM), pl.ds(start, ATT_BLOCK)]
            acc_scr[hh] += jnp.dot(vt, w.astype(BF16), preferred_element_type=F32)

    for r in reversed(range(ATT_K_PER_Q)):
        process(qi * ATT_K_PER_Q + r, True)

    def step(it, c):
        for r in range(ATT_K_PER_Q):
            process((qi - it) * ATT_K_PER_Q - 1 - r, False)
        return c

    lax.fori_loop(0, qi, step, 0)
    out_t = jnp.concatenate([acc_scr[0], acc_scr[1]], axis=0)
    o_ref[...] = out_t.T.astype(o_ref.dtype)


def _sb_attention(proj, batch, seq):
    t = proj.shape[0]
    nq = seq // ATT_Q_BLOCK
    idx = jnp.arange(ATT_BLOCK)
    upper = (idx[None, :] > idx[:, None]).astype(BF16)
    return pl.pallas_call(
        _sb_body,
        out_shape=jax.ShapeDtypeStruct((t, D_MODEL), BF16),
        grid=(batch, HEAD_PAIRS, nq),
        in_specs=[
            pl.BlockSpec((ATT_Q_BLOCK, LANES), lambda b, p, i: (b * nq + i, p)),
            pl.BlockSpec((seq, LANES), lambda b, p, i: (b, HEAD_PAIRS + p)),
            pl.BlockSpec((seq, LANES), lambda b, p, i: (b, 2 * HEAD_PAIRS + p)),
            pl.BlockSpec((ATT_BLOCK, ATT_BLOCK), lambda b, p, i: (0, 0)),
        ],
        out_specs=pl.BlockSpec((ATT_Q_BLOCK, LANES), lambda b, p, i: (b * nq + i, p)),
        scratch_shapes=[
            pltpu.VMEM((LANES, seq), BF16),
            pltpu.VMEM((2, HEAD_DIM, ATT_Q_BLOCK), F32),
            pltpu.VMEM((2, 1, ATT_Q_BLOCK), F32),
        ],
        compiler_params=_params("parallel", "parallel", "arbitrary"),
        name="sb_attention",
    )(proj, proj, proj, upper)


def _fox_gate_body(x_ref, g_ref, w_ref, b_ref, tri_ref, c_ref):
    h = _rms_normalize(x_ref[...], g_ref[...])
    logits = jnp.dot(h, w_ref[...], preferred_element_type=F32, precision=lax.Precision.HIGHEST)
    log_f = -_softplus(-(logits + b_ref[...]))
    seq = log_f.shape[0]
    carry = jnp.zeros((1, LANES), F32)
    for blk in range(seq // ATT_BLOCK):
        rows = slice(blk * ATT_BLOCK, (blk + 1) * ATT_BLOCK)
        c = jnp.dot(tri_ref[...], log_f[rows], preferred_element_type=F32,
                    precision=lax.Precision.HIGHEST) + carry
        c_ref[rows, :] = c
        carry = c[ATT_BLOCK - 1:ATT_BLOCK, :]


def _fox_cum_log_forget(x, gain, w_gate, b_gate, batch, seq):
    t, d = x.shape
    w = jnp.zeros((d, LANES), F32).at[:, :N_HEADS].set(w_gate)
    b = jnp.zeros((1, LANES), F32).at[0, :N_HEADS].set(b_gate)
    idx = jnp.arange(ATT_BLOCK)
    tri = (idx[:, None] >= idx[None, :]).astype(F32)
    return pl.pallas_call(
        _fox_gate_body,
        out_shape=jax.ShapeDtypeStruct((t, LANES), F32),
        grid=(batch,),
        in_specs=[
            pl.BlockSpec((seq, d), lambda i: (i, 0)),
            pl.BlockSpec((1, d), lambda i: (0, 0)),
            pl.BlockSpec((d, LANES), lambda i: (0, 0)),
            pl.BlockSpec((1, LANES), lambda i: (0, 0)),
            pl.BlockSpec((ATT_BLOCK, ATT_BLOCK), lambda i: (0, 0)),
        ],
        out_specs=pl.BlockSpec((seq, LANES), lambda i: (i, 0)),
        compiler_params=_params("parallel"),
        name="fox_gate",
    )(x, gain.reshape(1, d), w, b, tri)


def _fox_body(q_ref, k_ref, v_ref, cq_ref, ck_ref, o_ref, vt_scr, ckb_scr, acc_scr, m_scr, l_scr):
    qi = pl.program_id(2)

    @pl.when(qi == 0)
    def _():
        _transpose_values(v_ref, vt_scr)
        for hh in range(2):
            ckb_scr[hh] = jnp.broadcast_to(ck_ref[0, hh], ckb_scr.shape[1:])

    qs = _split_head_pair(q_ref[...])
    acc_scr[...] = jnp.zeros_like(acc_scr)
    m_scr[...] = jnp.full_like(m_scr, NEG_INF)
    l_scr[...] = jnp.zeros_like(l_scr)
    key = lax.broadcasted_iota(jnp.int32, (ATT_BLOCK, ATT_Q_BLOCK), 0)
    query = lax.broadcasted_iota(jnp.int32, (ATT_BLOCK, ATT_Q_BLOCK), 1)

    def process(kb, diagonal):
        start = pl.multiple_of(kb * ATT_BLOCK, ATT_BLOCK)
        k = k_ref[pl.ds(start, ATT_BLOCK), :]
        if diagonal:
            causal = key + (kb * ATT_BLOCK - qi * ATT_Q_BLOCK) <= query
        for hh in range(2):
            ck = ckb_scr[hh, pl.ds(start, ATT_BLOCK), :]
            a = _qk(k, qs[hh]) - jnp.concatenate([ck] * (ATT_Q_BLOCK // LANES), axis=1)
            if diagonal:
                a = jnp.where(causal, a, NEG_INF)
            cq = cq_ref[0, hh]
            m_old = m_scr[hh]
            m_new = jnp.maximum(m_old, jnp.max(a, axis=0, keepdims=True) + cq)
            alpha = jnp.exp(m_old - m_new)
            p = jnp.exp(a + (cq - m_new))
            l_scr[hh] = alpha * l_scr[hh] + jnp.sum(p, axis=0, keepdims=True)
            vt = vt_scr[pl.ds(hh * HEAD_DIM, HEAD_DIM), pl.ds(start, ATT_BLOCK)]
            acc_scr[hh] = alpha * acc_scr[hh] + jnp.dot(vt, p.astype(BF16), preferred_element_type=F32)
            m_scr[hh] = m_new

    def step(it, c):
        for r in range(ATT_K_PER_Q):
            process(it * ATT_K_PER_Q + r, False)
        return c

    lax.fori_loop(0, qi, step, 0)
    for r in range(ATT_K_PER_Q):
        process(qi * ATT_K_PER_Q + r, True)
    out_t = jnp.concatenate([acc_scr[0] / l_scr[0], acc_scr[1] / l_scr[1]], axis=0)
    o_ref[...] = out_t.T.astype(o_ref.dtype)


def _fox_attention(proj, cum, batch, seq):
    t = proj.shape[0]
    nq = seq // ATT_Q_BLOCK
    cum_h = cum[:, :N_HEADS].reshape(batch, seq, N_HEADS).transpose(0, 2, 1)
    cum_q = cum_h.reshape(batch, N_HEADS, 1, seq)
    cum_k = cum_h.reshape(batch, N_HEADS, seq, 1)
    return pl.pallas_call(
        _fox_body,
        out_shape=jax.ShapeDtypeStruct((t, D_MODEL), BF16),
        grid=(batch, HEAD_PAIRS, nq),
        in_specs=[
            pl.BlockSpec((ATT_Q_BLOCK, LANES), lambda b, p, i: (b * nq + i, p)),
            pl.BlockSpec((seq, LANES), lambda b, p, i: (b, HEAD_PAIRS + p)),
            pl.BlockSpec((seq, LANES), lambda b, p, i: (b, 2 * HEAD_PAIRS + p)),
            pl.BlockSpec((1, 2, 1, ATT_Q_BLOCK), lambda b, p, i: (b, p, 0, i)),
            pl.BlockSpec((1, 2, seq, 1), lambda b, p, i: (b, p, 0, 0)),
        ],
        out_specs=pl.BlockSpec((ATT_Q_BLOCK, LANES), lambda b, p, i: (b * nq + i, p)),
        scratch_shapes=[
            pltpu.VMEM((LANES, seq), BF16),
            pltpu.VMEM((2, seq, LANES), F32),
            pltpu.VMEM((2, HEAD_DIM, ATT_Q_BLOCK), F32),
            pltpu.VMEM((2, 1, ATT_Q_BLOCK), F32),
            pltpu.VMEM((2, 1, ATT_Q_BLOCK), F32),
        ],
        compiler_params=_params("parallel", "parallel", "arbitrary"),
        name="fox_attention",
    )(proj, proj, proj, cum_q, cum_k)


def _dil_body(q1_ref, k1_ref, q2_ref, k2_ref, q3_ref, k3_ref, v_ref, bias_ref, o_ref,
              q_scr, k_scr, v_scr, acc_scr, m_scr, l_scr):
    seq = v_ref.shape[0]
    for g, (q_ref, k_ref) in enumerate(((q1_ref, k1_ref), (q2_ref, k2_ref), (q3_ref, k3_ref))):
        q_scr[g] = q_ref[...].astype(F32)
        k_scr[g] = k_ref[...].astype(F32)
    v_scr[...] = v_ref[...].astype(F32)

    def block(g, q_start, k_start, n_keys, stride):
        q_rows = pl.ds(q_start, DIL_BLOCK, stride=stride) if stride > 1 else pl.ds(q_start, DIL_BLOCK)
        k_rows = pl.ds(k_start, n_keys, stride=stride) if stride > 1 else pl.ds(k_start, n_keys)
        qs = _split_head_pair(q_scr[g, q_rows, :].astype(BF16))
        k = k_scr[g, k_rows, :].astype(BF16)
        v = v_scr[k_rows, :].astype(BF16)
        outs, maxes, sums = [], [], []
        for hh in range(2):
            if n_keys == 2 * DIL_BLOCK:
                bias = bias_ref[g, hh]
            else:
                bias = bias_ref[g, hh, :, DIL_BLOCK:]
            logits = _qk(qs[hh], k) + bias
            m = jnp.max(logits, axis=-1, keepdims=True)
            p = jnp.exp(logits - m)
            sums.append(jnp.sum(p, axis=-1, keepdims=True))
            maxes.append(m)
            outs.append(jnp.dot(p.astype(BF16), v, preferred_element_type=F32))
        shape = (DIL_BLOCK, LANES)
        acc_scr[g, q_rows, :] = _merge_head_pair(outs[0], outs[1])
        m_scr[g, q_rows, :] = _merge_head_pair(jnp.broadcast_to(maxes[0], shape),
                                               jnp.broadcast_to(maxes[1], shape))
        l_scr[g, q_rows, :] = _merge_head_pair(jnp.broadcast_to(sums[0], shape),
                                               jnp.broadcast_to(sums[1], shape))

    block(0, 0, 0, DIL_BLOCK, 1)

    def g0_step(n, c):
        q_start = pl.multiple_of(n * DIL_BLOCK, DIL_BLOCK)
        k_start = pl.multiple_of((n - 1) * DIL_BLOCK, DIL_BLOCK)
        block(0, q_start, k_start, 2 * DIL_BLOCK, 1)
        return c

    lax.fori_loop(1, seq // DIL_BLOCK, g0_step, 0)

    for g in (1, 2):
        dil = DILATED_PAIRS[g][1]
        class_rows = DIL_BLOCK * dil
        for r in range(dil):
            block(g, r, r, DIL_BLOCK, dil)
            for n in range(1, seq // class_rows):
                block(g, r + n * class_rows, r + (n - 1) * class_rows, 2 * DIL_BLOCK, dil)

    def merge_step(c, carry):
        rows = pl.ds(pl.multiple_of(c * ATT_BLOCK, ATT_BLOCK), ATT_BLOCK)
        m0, m1, m2 = m_scr[0, rows, :], m_scr[1, rows, :], m_scr[2, rows, :]
        m = jnp.maximum(jnp.maximum(m0, m1), m2)
        w0, w1, w2 = jnp.exp(m0 - m), jnp.exp(m1 - m), jnp.exp(m2 - m)
        num = w0 * acc_scr[0, rows, :] + w1 * acc_scr[1, rows, :] + w2 * acc_scr[2, rows, :]
        den = w0 * l_scr[0, rows, :] + w1 * l_scr[1, rows, :] + w2 * l_scr[2, rows, :]
        o_ref[rows, :] = (num / den).astype(o_ref.dtype)
        return carry

    lax.fori_loop(0, seq // ATT_BLOCK, merge_step, 0)


def _t5_causal_bucket(distance):
    max_exact = N_REL_BUCKETS // 2
    d = jnp.maximum(distance, 1).astype(F32)
    log_b = max_exact + (jnp.log(d / max_exact) / math.log(REL_MAX_DISTANCE / max_exact)
                         * (N_REL_BUCKETS - max_exact)).astype(jnp.int32)
    log_b = jnp.minimum(log_b, N_REL_BUCKETS - 1)
    return jnp.where(distance < max_exact, distance, log_b)


def _dilated_bias(rel_bias):
    qi = jnp.arange(DIL_BLOCK, dtype=jnp.int32)
    kj = jnp.arange(2 * DIL_BLOCK, dtype=jnp.int32)
    delta = qi[:, None] + DIL_BLOCK - kj[None, :]
    in_band = (delta >= 0) & (delta <= DIL_SPAN)
    tables = []
    for _, dil in DILATED_PAIRS:
        bias = rel_bias[_t5_causal_bucket(jnp.maximum(delta, 0) * dil)]
        tables.append(jnp.where(in_band[..., None], bias, NEG_INF))
    return jnp.stack(tables).transpose(0, 3, 1, 2).astype(F32)


def _dilated_attention(proj, rel_bias, batch, seq):
    t = proj.shape[0]
    n_groups = len(DILATED_PAIRS)

    def section(s):
        return pl.BlockSpec((seq, LANES), lambda b, p: (b, s * HEAD_PAIRS + p))

    group_scratch = pltpu.VMEM((n_groups, seq, LANES), F32)
    return pl.pallas_call(
        _dil_body,
        out_shape=jax.ShapeDtypeStruct((t, D_MODEL), BF16),
        grid=(batch, HEAD_PAIRS),
        in_specs=[section(s) for s in range(2 * n_groups + 1)] + [
            pl.BlockSpec((n_groups, 2, DIL_BLOCK, 2 * DIL_BLOCK), lambda b, p: (0, p, 0, 0)),
        ],
        out_specs=pl.BlockSpec((seq, LANES), lambda b, p: (b, p)),
        scratch_shapes=[group_scratch, group_scratch, pltpu.VMEM((seq, LANES), F32),
                        group_scratch, group_scratch, group_scratch],
        compiler_params=_params("parallel", "parallel"),
        name="dilated_attention",
    )(*([proj] * (2 * n_groups + 1)), _dilated_bias(rel_bias))


def _tile_heads(v):
    return jnp.tile(v.astype(F32), N_HEADS)


def _sb_mixer(x, gain, w_qkv, w_o, batch, seq):
    ones = jnp.ones((D_MODEL,), F32)
    col_scale = jnp.concatenate([ones * QK_SCALE, ones, ones])
    proj = _norm_proj(x, gain, w_qkv, col_scale, 0)
    return _out_proj_residual(x, _sb_attention(proj, batch, seq), w_o)


def _dilated_mixer(x, gain, w_in, q_norm, k_norm, rel_bias, w_o, batch, seq):
    scales = []
    for g in range(len(DILATED_PAIRS)):
        scales += [_tile_heads(q_norm[g]) * QK_SCALE, _tile_heads(k_norm[g])]
    scales.append(jnp.ones((D_MODEL,), F32))
    proj = _norm_proj(x, gain, w_in, jnp.concatenate(scales), 2 * len(DILATED_PAIRS))
    return _out_proj_residual(x, _dilated_attention(proj, rel_bias, batch, seq), w_o)


def _fox_mixer(x, gain, w_in, b_f, q_norm, k_norm, w_o, batch, seq):
    col_scale = jnp.concatenate([_tile_heads(q_norm) * QK_SCALE, _tile_heads(k_norm),
                                 jnp.ones((D_MODEL,), F32)])
    proj = _norm_proj(x, gain, w_in[:, :3 * D_MODEL], col_scale, 2)
    cum = _fox_cum_log_forget(x, gain, w_in[:, 3 * D_MODEL:], b_f, batch, seq)
    return _out_proj_residual(x, _fox_attention(proj, cum, batch, seq), w_o)


def kernel(x, sb_w_qkv, sb_w_o, dil_w_in, dil_q_norm, dil_k_norm, dil_w_o, fox_w_in, fox_b_f,
           fox_q_norm, fox_k_norm, fox_w_o, rel_bias, attn_norm, ffn_norm, mlp_w_gate, mlp_w_up,
           mlp_w_down, moe_router, moe_w_gate, moe_w_up, moe_w_down):
    batch, seq, d = x.shape
    depth = attn_norm.shape[0]
    h = x.reshape(batch * seq, d)
    for i in range(depth):
        kind, j = i % 3, i // 3
        if kind == 0:
            h = _sb_mixer(h, attn_norm[i], sb_w_qkv[j], sb_w_o[j], batch, seq)
        elif kind == 1:
            h = _dilated_mixer(h, attn_norm[i], dil_w_in[j], dil_q_norm[j], dil_k_norm[j], rel_bias,
                               dil_w_o[j], batch, seq)
        else:
            h = _fox_mixer(h, attn_norm[i], fox_w_in[j], fox_b_f[j], fox_q_norm[j], fox_k_norm[j],
                           fox_w_o[j], batch, seq)
        f = i // 2
        if i % 2 == 0:
            h = _ffn_residual(h, ffn_norm[i], mlp_w_gate[f], mlp_w_up[f], mlp_w_down[f])
        else:
            combine = _router_combine(h, ffn_norm[i], moe_router[f])
            h = _moe_residual(h, ffn_norm[i], combine, moe_w_gate[f], moe_w_up[f], moe_w_down[f])
    return h.reshape(batch, seq, d)
```

```python
import functools
import math

import jax
import jax.numpy as jnp
from jax import lax
from jax.experimental import pallas as pl
from jax.experimental.pallas import tpu as pltpu
from jax.experimental.pallas import tpu_sc as plsc

D_MODEL = 1024
N_HEADS = 16
HEAD_DIM = 64
LANES = 128
HEAD_PAIRS = D_MODEL // LANES
D_FF = 3584
N_EXPERTS = 8
N_REL_BUCKETS = 32
REL_MAX_DISTANCE = 2048
DILATED_PAIRS = ((128, 1), (512, 4), (2048, 16))
DIL_SPAN = 128
RMS_EPS = 1e-6
NEG_INF = -1e30
QK_SCALE = 1.0 / math.sqrt(HEAD_DIM)

ROW_TILE = 1024
COL_TILE = 1024
FF_TILE = 512
MOE_ROW_TILE = 512
MOE_FF_TILE = 1792
SC_CORES = 2
SC_SUBCORES = 16
SC_GATHER_ROWS = 128
NORM_CHUNK = 256
ATT_BLOCK = 256
ATT_Q_BLOCK = 512
ATT_K_PER_Q = ATT_Q_BLOCK // ATT_BLOCK
DIL_BLOCK = 128
VMEM_LIMIT = 56 * 1024 * 1024

F32 = jnp.float32
BF16 = jnp.bfloat16


def _params(*semantics):
    return pltpu.CompilerParams(dimension_semantics=semantics, vmem_limit_bytes=VMEM_LIMIT)


def _rms_normalize(x, gain):
    inv = lax.rsqrt(jnp.mean(x * x, axis=-1, keepdims=True) + RMS_EPS)
    return x * inv * gain


def _softplus(z):
    return jnp.maximum(z, 0.0) + jnp.log(1.0 + jnp.exp(-jnp.abs(z)))


def _norm_proj_body(x_ref, g_ref, w_ref, cs_ref, bd_ref, o_ref, h_scr, *, n_norm):
    j = pl.program_id(1)

    @pl.when(j == 0)
    def _():
        h_scr[...] = _rms_normalize(x_ref[...], g_ref[...]).astype(BF16)

    acc = jnp.dot(h_scr[...], w_ref[...], preferred_element_type=F32)

    def plain():
        o_ref[...] = (acc * cs_ref[...]).astype(o_ref.dtype)

    def head_normed():
        for c in range(COL_TILE // NORM_CHUNK):
            sl = slice(c * NORM_CHUNK, (c + 1) * NORM_CHUNK)
            a = acc[:, sl]
            ss = jnp.dot((a * a).astype(BF16), bd_ref[...], preferred_element_type=F32)
            inv = lax.rsqrt(ss * (1.0 / HEAD_DIM) + RMS_EPS)
            o_ref[:, sl] = (a * inv * cs_ref[:, sl]).astype(o_ref.dtype)

    if n_norm == 0:
        plain()
    else:
        pl.when(j < n_norm)(head_normed)
        pl.when(j >= n_norm)(plain)


def _norm_proj(x, gain, w, col_scale, n_norm):
    t, d = x.shape
    n = w.shape[1]
    head_id = jnp.arange(NORM_CHUNK) // HEAD_DIM
    block_diag = (head_id[:, None] == head_id[None, :]).astype(BF16)
    return pl.pallas_call(
        functools.partial(_norm_proj_body, n_norm=n_norm),
        out_shape=jax.ShapeDtypeStruct((t, n), BF16),
        grid=(t // ROW_TILE, n // COL_TILE),
        in_specs=[
            pl.BlockSpec((ROW_TILE, d), lambda i, j: (i, 0)),
            pl.BlockSpec((1, d), lambda i, j: (0, 0)),
            pl.BlockSpec((d, COL_TILE), lambda i, j: (0, j)),
            pl.BlockSpec((1, COL_TILE), lambda i, j: (0, j)),
            pl.BlockSpec((NORM_CHUNK, NORM_CHUNK), lambda i, j: (0, 0)),
        ],
        out_specs=pl.BlockSpec((ROW_TILE, COL_TILE), lambda i, j: (i, j)),
        scratch_shapes=[pltpu.VMEM((ROW_TILE, d), BF16)],
        compiler_params=_params("parallel", "arbitrary"),
        name="norm_proj",
    )(x, gain.reshape(1, d), w.astype(BF16), col_scale.reshape(1, n).astype(F32), block_diag)


def _out_proj_body(x_ref, o_ref, w_ref, y_ref):
    y_ref[...] = x_ref[...] + jnp.dot(o_ref[...], w_ref[...], preferred_element_type=F32)


def _out_proj_residual(x, o, w):
    t, d = x.shape
    return pl.pallas_call(
        _out_proj_body,
        out_shape=jax.ShapeDtypeStruct((t, d), F32),
        grid=(t // ROW_TILE,),
        in_specs=[
            pl.BlockSpec((ROW_TILE, d), lambda i: (i, 0)),
            pl.BlockSpec((ROW_TILE, d), lambda i: (i, 0)),
            pl.BlockSpec((d, d), lambda i: (0, 0)),
        ],
        out_specs=pl.BlockSpec((ROW_TILE, d), lambda i: (i, 0)),
        compiler_params=_params("parallel"),
        name="out_proj",
    )(x, o, w.astype(BF16))


def _swiglu_hidden(h, wg, wu):
    g = jnp.dot(h, wg, preferred_element_type=F32)
    u = jnp.dot(h, wu, preferred_element_type=F32)
    return g * (1.0 / (1.0 + jnp.exp(-g))) * u


def _ffn_body(x_ref, g_ref, wg_ref, wu_ref, wd_ref, y_ref, h_scr):
    f = pl.program_id(1)

    @pl.when(f == 0)
    def _():
        x = x_ref[...]
        h_scr[...] = _rms_normalize(x, g_ref[...]).astype(BF16)
        y_ref[...] = x

    a = _swiglu_hidden(h_scr[...], wg_ref[...], wu_ref[...]).astype(BF16)
    y_ref[...] += jnp.dot(a, wd_ref[...], preferred_element_type=F32)


def _ffn_residual(x, gain, w_gate, w_up, w_down):
    t, d = x.shape
    ff = w_gate.shape[1]
    return pl.pallas_call(
        _ffn_body,
        out_shape=jax.ShapeDtypeStruct((t, d), F32),
        grid=(t // ROW_TILE, ff // FF_TILE),
        in_specs=[
            pl.BlockSpec((ROW_TILE, d), lambda i, f: (i, 0)),
            pl.BlockSpec((1, d), lambda i, f: (0, 0)),
            pl.BlockSpec((d, FF_TILE), lambda i, f: (0, f)),
            pl.BlockSpec((d, FF_TILE), lambda i, f: (0, f)),
            pl.BlockSpec((FF_TILE, d), lambda i, f: (f, 0)),
        ],
        out_specs=pl.BlockSpec((ROW_TILE, d), lambda i, f: (i, 0)),
        scratch_shapes=[pltpu.VMEM((ROW_TILE, d), BF16)],
        compiler_params=_params("parallel", "arbitrary"),
        name="ffn",
    )(x, gain.reshape(1, d), w_gate.astype(BF16), w_up.astype(BF16), w_down.astype(BF16))


def _pack_bf16_pairs(x):
    half = x.shape[1] // 2
    bits = pltpu.bitcast(x.astype(BF16).astype(F32), jnp.int32)
    return bits[:, :half] | lax.shift_right_logical(bits[:, half:], jnp.int32(16))


def _unpack_bf16_pairs(p):
    left = pltpu.bitcast(p & jnp.int32(-65536), F32)
    right = pltpu.bitcast(lax.shift_left(p, jnp.int32(16)), F32)
    return jnp.concatenate([left, right], axis=1).astype(BF16)


def _router_body(x_ref, g_ref, r_ref, tri_ref, gates_ref, rank_ref, h_ref, count_scr):
    @pl.when(pl.program_id(0) == 0)
    def _():
        count_scr[...] = jnp.zeros_like(count_scr)

    h = _rms_normalize(x_ref[...], g_ref[...])
    h_ref[...] = _pack_bf16_pairs(h)
    logits = jnp.dot(h, r_ref[...], preferred_element_type=F32, precision=lax.Precision.HIGHEST)
    lane = lax.broadcasted_iota(jnp.int32, logits.shape, 1).astype(F32)
    logits = jnp.where(lane < N_EXPERTS, logits, -jnp.inf)
    m1 = jnp.max(logits, axis=-1, keepdims=True)
    i1 = jnp.min(jnp.where(logits == m1, lane, float(LANES)), axis=-1, keepdims=True)
    rest = jnp.where(lane == i1, -jnp.inf, logits)
    m2 = jnp.max(rest, axis=-1, keepdims=True)
    i2 = jnp.min(jnp.where(rest == m2, lane, float(LANES)), axis=-1, keepdims=True)
    e = jnp.exp(m2 - m1)
    g1 = 1.0 / (1.0 + e)
    gates_ref[...] = jnp.where(lane == i1, g1, 0.0) + jnp.where(lane == i2, e * g1, 0.0)
    chosen = jnp.where((lane == i1) | (lane == i2), 1.0, 0.0)
    inclusive = jnp.dot(tri_ref[...], chosen.astype(BF16), preferred_element_type=F32)
    rank_ref[...] = jnp.where(chosen > 0.0, inclusive - 1.0 + count_scr[...], -1.0)
    count_scr[...] += inclusive[ROW_TILE - 1:ROW_TILE, :]


def _router(x, gain, router):
    t, d = x.shape
    r = jnp.zeros((d, LANES), F32).at[:, :N_EXPERTS].set(router)
    idx = jnp.arange(ROW_TILE)
    tri = (idx[:, None] >= idx[None, :]).astype(BF16)
    return pl.pallas_call(
        _router_body,
        out_shape=(jax.ShapeDtypeStruct((t, LANES), F32), jax.ShapeDtypeStruct((t, LANES), F32),
                   jax.ShapeDtypeStruct((t, d // 2), jnp.int32)),
        grid=(t // ROW_TILE,),
        in_specs=[
            pl.BlockSpec((ROW_TILE, d), lambda i: (i, 0)),
            pl.BlockSpec((1, d), lambda i: (0, 0)),
            pl.BlockSpec((d, LANES), lambda i: (0, 0)),
            pl.BlockSpec((ROW_TILE, ROW_TILE), lambda i: (0, 0)),
        ],
        out_specs=(pl.BlockSpec((ROW_TILE, LANES), lambda i: (i, 0)),
                   pl.BlockSpec((ROW_TILE, LANES), lambda i: (i, 0)),
                   pl.BlockSpec((ROW_TILE, d // 2), lambda i: (i, 0))),
        scratch_shapes=[pltpu.VMEM((1, LANES), F32)],
        compiler_params=_params("arbitrary"),
        name="router",
    )(x, gain.reshape(1, d), r, tri)


def _sc_row_gather(table, idx):
    width = table.shape[1]
    n = idx.shape[0]
    workers = SC_CORES * SC_SUBCORES
    per_worker = n // workers
    assert n % (workers * SC_GATHER_ROWS) == 0
    mesh = plsc.VectorSubcoreMesh(core_axis_name="core", subcore_axis_name="subcore",
                                  num_cores=SC_CORES, num_subcores=SC_SUBCORES)

    @functools.partial(
        pl.kernel, mesh=mesh, out_type=jax.ShapeDtypeStruct((n, width), table.dtype),
        scratch_types=[pltpu.VMEM((SC_GATHER_ROWS,), jnp.int32),
                       pltpu.VMEM((SC_GATHER_ROWS, width), table.dtype),
                       pltpu.SemaphoreType.DMA],
        name="sc_row_gather")
    def gather(table_hbm, idx_hbm, out_hbm, idx_v, rows_v, sem):
        worker = lax.axis_index("subcore") * SC_CORES + lax.axis_index("core")

        @pl.loop(0, per_worker // SC_GATHER_ROWS)
        def _(c):
            off = pl.multiple_of(worker * per_worker + c * SC_GATHER_ROWS, SC_GATHER_ROWS)
            pltpu.sync_copy(idx_hbm.at[pl.ds(off, SC_GATHER_ROWS)], idx_v)
            pltpu.async_copy(table_hbm.at[idx_v], rows_v, sem).wait()
            pltpu.sync_copy(rows_v, out_hbm.at[pl.ds(off, SC_GATHER_ROWS)])

    return gather(table, idx)


def _expert_ffn_body(te_ref, nu_ref, h_ref, wg_ref, wu_ref, wd_ref, y_ref, acc_scr):
    i = pl.program_id(0)
    f = pl.program_id(1)

    @pl.when(i < nu_ref[0])
    def _():
        a = _swiglu_hidden(_unpack_bf16_pairs(h_ref[...]), wg_ref[...], wu_ref[...]).astype(BF16)
        y = jnp.dot(a, wd_ref[...], preferred_element_type=F32)

        @pl.when(f == 0)
        def _():
            acc_scr[...] = y

        @pl.when(f > 0)
        def _():
            acc_scr[...] += y

        @pl.when(f == pl.num_programs(1) - 1)
        def _():
            y_ref[...] = _pack_bf16_pairs(acc_scr[...])


def _expert_ffn(h_sorted, tile_expert, n_used, w_gate, w_up, w_down):
    rows, half = h_sorted.shape
    d = 2 * half
    ff = w_gate.shape[2]
    n_f = ff // MOE_FF_TILE

    def row_map(i, f, te, nu):
        return (jnp.minimum(i, nu[0] - 1), 0)

    def col_step(i, f, nu):
        return jnp.where(i < nu[0], f, n_f - 1)

    grid_spec = pltpu.PrefetchScalarGridSpec(
        num_scalar_prefetch=2,
        grid=(rows // MOE_ROW_TILE, n_f),
        in_specs=[
            pl.BlockSpec((MOE_ROW_TILE, half), row_map),
            pl.BlockSpec((None, d, MOE_FF_TILE), lambda i, f, te, nu: (te[i], 0, col_step(i, f, nu))),
            pl.BlockSpec((None, d, MOE_FF_TILE), lambda i, f, te, nu: (te[i], 0, col_step(i, f, nu))),
            pl.BlockSpec((None, MOE_FF_TILE, d), lambda i, f, te, nu: (te[i], col_step(i, f, nu), 0)),
        ],
        out_specs=pl.BlockSpec((MOE_ROW_TILE, half), row_map),
        scratch_shapes=[pltpu.VMEM((MOE_ROW_TILE, d), F32)],
    )
    return pl.pallas_call(
        _expert_ffn_body,
        out_shape=jax.ShapeDtypeStruct((rows, half), jnp.int32),
        grid_spec=grid_spec,
        compiler_params=_params("arbitrary", "arbitrary"),
        name="expert_ffn",
    )(tile_expert, n_used, h_sorted, w_gate.astype(BF16), w_up.astype(BF16), w_down.astype(BF16))


def _combine_body(x_ref, y_ref, g_ref, o_ref):
    g = g_ref[...]
    first = _unpack_bf16_pairs(y_ref[0]).astype(F32) * g[:, 0:1]
    second = _unpack_bf16_pairs(y_ref[1]).astype(F32) * g[:, 1:2]
    o_ref[...] = x_ref[...] + first + second


def _combine_residual(x, y_pairs, gates2):
    t, d = x.shape
    return pl.pallas_call(
        _combine_body,
        out_shape=jax.ShapeDtypeStruct((t, d), F32),
        grid=(t // ROW_TILE,),
        in_specs=[
            pl.BlockSpec((ROW_TILE, d), lambda i: (i, 0)),
            pl.BlockSpec((2, ROW_TILE, d // 2), lambda i: (0, i, 0)),
            pl.BlockSpec((ROW_TILE, 2), lambda i: (i, 0)),
        ],
        out_specs=pl.BlockSpec((ROW_TILE, d), lambda i: (i, 0)),
        compiler_params=_params("parallel"),
        name="moe_combine",
    )(x, y_pairs, gates2)


def _moe_residual(x, gain, router, w_gate, w_up, w_down):
    t, d = x.shape
    gates, rank, h_packed = _router(x, gain, router)
    rank8 = rank[:, :N_EXPERTS].astype(jnp.int32)
    chosen = rank8 >= 0
    counts = jnp.sum(chosen, axis=0, dtype=jnp.int32)
    padded = (counts + MOE_ROW_TILE - 1) // MOE_ROW_TILE * MOE_ROW_TILE
    ends = jnp.cumsum(padded)
    pos = (ends - padded)[None, :] + rank8
    max_rows = 2 * t + N_EXPERTS * MOE_ROW_TILE
    pos_lo = jnp.min(jnp.where(chosen, pos, max_rows), axis=1)
    pos_hi = jnp.max(jnp.where(chosen, pos, -1), axis=1)
    exp_lo = jnp.argmax(chosen, axis=1)
    exp_hi = N_EXPERTS - 1 - jnp.argmax(chosen[:, ::-1], axis=1)
    gates8 = gates[:, :N_EXPERTS]
    gates2 = jnp.stack([jnp.take_along_axis(gates8, exp_lo[:, None], axis=1)[:, 0],
                        jnp.take_along_axis(gates8, exp_hi[:, None], axis=1)[:, 0]], axis=1)
    slots = jnp.concatenate([pos_lo, pos_hi])
    token = jnp.arange(t, dtype=jnp.int32)
    source = jnp.zeros((max_rows,), jnp.int32).at[slots].set(jnp.concatenate([token, token]),
                                                             unique_indices=True)
    n_tiles = max_rows // MOE_ROW_TILE
    n_used = (ends[-1] // MOE_ROW_TILE).astype(jnp.int32)
    tile_start = jnp.minimum(jnp.arange(n_tiles, dtype=jnp.int32), n_used - 1) * MOE_ROW_TILE
    tile_expert = jnp.sum(tile_start[:, None] >= ends[None, :], axis=1, dtype=jnp.int32)

    h_sorted = _sc_row_gather(h_packed, source)
    y_sorted = _expert_ffn(h_sorted, tile_expert, n_used.reshape(1), w_gate, w_up, w_down)
    y_pairs = _sc_row_gather(y_sorted, slots.astype(jnp.int32)).reshape(2, t, d // 2)
    return _combine_residual(x, y_pairs, gates2)


def _split_head_pair(q):
    is_first = lax.broadcasted_iota(jnp.int32, (1, LANES), 1) < HEAD_DIM
    zero = jnp.zeros_like(q)
    return jnp.where(is_first, q, zero), jnp.where(is_first, zero, q)


def _merge_head_pair(first, second):
    is_first = lax.broadcasted_iota(jnp.int32, (1, LANES), 1) < HEAD_DIM
    return jnp.where(is_first, first, second)


def _qk(q, k):
    return lax.dot_general(q, k, (((1,), (1,)), ((), ())), preferred_element_type=F32)


def _transpose_values(v_ref, vt_scr):
    for c in range(v_ref.shape[0] // ATT_BLOCK):
        rows = slice(c * ATT_BLOCK, (c + 1) * ATT_BLOCK)
        vt_scr[:, rows] = v_ref[rows, :].astype(F32).T.astype(vt_scr.dtype)


def _sb_body(q_ref, k_ref, v_ref, u_ref, o_ref, vt_scr, acc_scr, carry_scr):
    qi = pl.program_id(2)

    @pl.when(qi == 0)
    def _():
        _transpose_values(v_ref, vt_scr)

    qs = _split_head_pair(q_ref[...])
    upper = u_ref[...]
    acc_scr[...] = jnp.zeros_like(acc_scr)
    carry_scr[...] = jnp.zeros_like(carry_scr)
    key = lax.broadcasted_iota(jnp.int32, (ATT_BLOCK, ATT_Q_BLOCK), 0)
    query = lax.broadcasted_iota(jnp.int32, (ATT_BLOCK, ATT_Q_BLOCK), 1)

    def process(kb, diagonal):
        start = pl.multiple_of(kb * ATT_BLOCK, ATT_BLOCK)
        k = k_ref[pl.ds(start, ATT_BLOCK), :]
        if diagonal:
            strict = key + (kb * ATT_BLOCK - qi * ATT_Q_BLOCK) < query
        for hh in range(2):
            z = _qk(k, qs[hh])
            log_beta = z - _softplus(z)
            log_keep = log_beta - z
            if diagonal:
                log_keep = jnp.where(strict, log_keep, 0.0)
            remain = jnp.dot(upper, log_keep.astype(BF16), preferred_element_type=F32)
            w = jnp.exp(log_beta + remain + carry_scr[hh])
            if diagonal:
                w = jnp.where(strict, w, 0.0)
            carry_scr[hh] += jnp.sum(log_keep, axis=0, keepdims=True)
            vt = vt_scr[pl.ds(hh * HEAD_DIM, HEAD_DIM), pl.ds(start, ATT_BLOCK)]
            acc_scr[hh] += jnp.dot(vt, w.astype(BF16), preferred_element_type=F32)

    for r in reversed(range(ATT_K_PER_Q)):
        process(qi * ATT_K_PER_Q + r, True)

    def step(it, c):
        for r in range(ATT_K_PER_Q):
            process((qi - it) * ATT_K_PER_Q - 1 - r, False)
        return c

    lax.fori_loop(0, qi, step, 0)
    out_t = jnp.concatenate([acc_scr[0], acc_scr[1]], axis=0)
    o_ref[...] = out_t.T.astype(o_ref.dtype)


def _sb_attention(proj, batch, seq):
    t = proj.shape[0]
    nq = seq // ATT_Q_BLOCK
    idx = jnp.arange(ATT_BLOCK)
    upper = (idx[None, :] > idx[:, None]).astype(BF16)
    return pl.pallas_call(
        _sb_body,
        out_shape=jax.ShapeDtypeStruct((t, D_MODEL), BF16),
        grid=(batch, HEAD_PAIRS, nq),
        in_specs=[
            pl.BlockSpec((ATT_Q_BLOCK, LANES), lambda b, p, i: (b * nq + i, p)),
            pl.BlockSpec((seq, LANES), lambda b, p, i: (b, HEAD_PAIRS + p)),
            pl.BlockSpec((seq, LANES), lambda b, p, i: (b, 2 * HEAD_PAIRS + p)),
            pl.BlockSpec((ATT_BLOCK, ATT_BLOCK), lambda b, p, i: (0, 0)),
        ],
        out_specs=pl.BlockSpec((ATT_Q_BLOCK, LANES), lambda b, p, i: (b * nq + i, p)),
        scratch_shapes=[
            pltpu.VMEM((LANES, seq), BF16),
            pltpu.VMEM((2, HEAD_DIM, ATT_Q_BLOCK), F32),
            pltpu.VMEM((2, 1, ATT_Q_BLOCK), F32),
        ],
        compiler_params=_params("parallel", "parallel", "arbitrary"),
        name="sb_attention",
    )(proj, proj, proj, upper)


def _fox_gate_body(x_ref, g_ref, w_ref, b_ref, tri_ref, c_ref):
    h = _rms_normalize(x_ref[...], g_ref[...])
    logits = jnp.dot(h, w_ref[...], preferred_element_type=F32, precision=lax.Precision.HIGHEST)
    log_f = -_softplus(-(logits + b_ref[...]))
    seq = log_f.shape[0]
    carry = jnp.zeros((1, LANES), F32)
    for blk in range(seq // ATT_BLOCK):
        rows = slice(blk * ATT_BLOCK, (blk + 1) * ATT_BLOCK)
        c = jnp.dot(tri_ref[...], log_f[rows], preferred_element_type=F32,
                    precision=lax.Precision.HIGHEST) + carry
        c_ref[rows, :] = c
        carry = c[ATT_BLOCK - 1:ATT_BLOCK, :]


def _fox_cum_log_forget(x, gain, w_gate, b_gate, batch, seq):
    t, d = x.shape
    w = jnp.zeros((d, LANES), F32).at[:, :N_HEADS].set(w_gate)
    b = jnp.zeros((1, LANES), F32).at[0, :N_HEADS].set(b_gate)
    idx = jnp.arange(ATT_BLOCK)
    tri = (idx[:, None] >= idx[None, :]).astype(F32)
    return pl.pallas_call(
        _fox_gate_body,
        out_shape=jax.ShapeDtypeStruct((t, LANES), F32),
        grid=(batch,),
        in_specs=[
            pl.BlockSpec((seq, d), lambda i: (i, 0)),
            pl.BlockSpec((1, d), lambda i: (0, 0)),
            pl.BlockSpec((d, LANES), lambda i: (0, 0)),
            pl.BlockSpec((1, LANES), lambda i: (0, 0)),
            pl.BlockSpec((ATT_BLOCK, ATT_BLOCK), lambda i: (0, 0)),
        ],
        out_specs=pl.BlockSpec((seq, LANES), lambda i: (i, 0)),
        compiler_params=_params("parallel"),
        name="fox_gate",
    )(x, gain.reshape(1, d), w, b, tri)


def _fox_body(q_ref, k_ref, v_ref, cq_ref, ck_ref, o_ref, vt_scr, ckb_scr, acc_scr, m_scr, l_scr):
    qi = pl.program_id(2)

    @pl.when(qi == 0)
    def _():
        _transpose_values(v_ref, vt_scr)
        for hh in range(2):
            ckb_scr[hh] = jnp.broadcast_to(ck_ref[0, hh], ckb_scr.shape[1:])

    qs = _split_head_pair(q_ref[...])
    acc_scr[...] = jnp.zeros_like(acc_scr)
    m_scr[...] = jnp.full_like(m_scr, NEG_INF)
    l_scr[...] = jnp.zeros_like(l_scr)
    key = lax.broadcasted_iota(jnp.int32, (ATT_BLOCK, ATT_Q_BLOCK), 0)
    query = lax.broadcasted_iota(jnp.int32, (ATT_BLOCK, ATT_Q_BLOCK), 1)

    def process(kb, diagonal):
        start = pl.multiple_of(kb * ATT_BLOCK, ATT_BLOCK)
        k = k_ref[pl.ds(start, ATT_BLOCK), :]
        if diagonal:
            causal = key + (kb * ATT_BLOCK - qi * ATT_Q_BLOCK) <= query
        for hh in range(2):
            ck = ckb_scr[hh, pl.ds(start, ATT_BLOCK), :]
            a = _qk(k, qs[hh]) - jnp.concatenate([ck] * (ATT_Q_BLOCK // LANES), axis=1)
            if diagonal:
                a = jnp.where(causal, a, NEG_INF)
            cq = cq_ref[0, hh]
            m_old = m_scr[hh]
            m_new = jnp.maximum(m_old, jnp.max(a, axis=0, keepdims=True) + cq)
            alpha = jnp.exp(m_old - m_new)
            p = jnp.exp(a + (cq - m_new))
            l_scr[hh] = alpha * l_scr[hh] + jnp.sum(p, axis=0, keepdims=True)
            vt = vt_scr[pl.ds(hh * HEAD_DIM, HEAD_DIM), pl.ds(start, ATT_BLOCK)]
            acc_scr[hh] = alpha * acc_scr[hh] + jnp.dot(vt, p.astype(BF16), preferred_element_type=F32)
            m_scr[hh] = m_new

    def step(it, c):
        for r in range(ATT_K_PER_Q):
            process(it * ATT_K_PER_Q + r, False)
        return c

    lax.fori_loop(0, qi, step, 0)
    for r in range(ATT_K_PER_Q):
        process(qi * ATT_K_PER_Q + r, True)
    out_t = jnp.concatenate([acc_scr[0] / l_scr[0], acc_scr[1] / l_scr[1]], axis=0)
    o_ref[...] = out_t.T.astype(o_ref.dtype)


def _fox_attention(proj, cum, batch, seq):
    t = proj.shape[0]
    nq = seq // ATT_Q_BLOCK
    cum_h = cum[:, :N_HEADS].reshape(batch, seq, N_HEADS).transpose(0, 2, 1)
    cum_q = cum_h.reshape(batch, N_HEADS, 1, seq)
    cum_k = cum_h.reshape(batch, N_HEADS, seq, 1)
    return pl.pallas_call(
        _fox_body,
        out_shape=jax.ShapeDtypeStruct((t, D_MODEL), BF16),
        grid=(batch, HEAD_PAIRS, nq),
        in_specs=[
            pl.BlockSpec((ATT_Q_BLOCK, LANES), lambda b, p, i: (b * nq + i, p)),
            pl.BlockSpec((seq, LANES), lambda b, p, i: (b, HEAD_PAIRS + p)),
            pl.BlockSpec((seq, LANES), lambda b, p, i: (b, 2 * HEAD_PAIRS + p)),
            pl.BlockSpec((1, 2, 1, ATT_Q_BLOCK), lambda b, p, i: (b, p, 0, i)),
            pl.BlockSpec((1, 2, seq, 1), lambda b, p, i: (b, p, 0, 0)),
        ],
        out_specs=pl.BlockSpec((ATT_Q_BLOCK, LANES), lambda b, p, i: (b * nq + i, p)),
        scratch_shapes=[
            pltpu.VMEM((LANES, seq), BF16),
            pltpu.VMEM((2, seq, LANES), F32),
            pltpu.VMEM((2, HEAD_DIM, ATT_Q_BLOCK), F32),
            pltpu.VMEM((2, 1, ATT_Q_BLOCK), F32),
            pltpu.VMEM((2, 1, ATT_Q_BLOCK), F32),
        ],
        compiler_params=_params("parallel", "parallel", "arbitrary"),
        name="fox_attention",
    )(proj, proj, proj, cum_q, cum_k)


def _dil_body(q1_ref, k1_ref, q2_ref, k2_ref, q3_ref, k3_ref, v_ref, bias_ref, o_ref,
              q_scr, k_scr, v_scr, acc_scr, m_scr, l_scr):
    seq = v_ref.shape[0]
    for g, (q_ref, k_ref) in enumerate(((q1_ref, k1_ref), (q2_ref, k2_ref), (q3_ref, k3_ref))):
        q_scr[g] = q_ref[...].astype(F32)
        k_scr[g] = k_ref[...].astype(F32)
    v_scr[...] = v_ref[...].astype(F32)

    def block(g, q_start, k_start, n_keys, stride):
        q_rows = pl.ds(q_start, DIL_BLOCK, stride=stride) if stride > 1 else pl.ds(q_start, DIL_BLOCK)
        k_rows = pl.ds(k_start, n_keys, stride=stride) if stride > 1 else pl.ds(k_start, n_keys)
        qs = _split_head_pair(q_scr[g, q_rows, :].astype(BF16))
        k = k_scr[g, k_rows, :].astype(BF16)
        v = v_scr[k_rows, :].astype(BF16)
        outs, maxes, sums = [], [], []
        for hh in range(2):
            if n_keys == 2 * DIL_BLOCK:
                bias = bias_ref[g, hh]
            else:
                bias = bias_ref[g, hh, :, DIL_BLOCK:]
            logits = _qk(qs[hh], k) + bias
            m = jnp.max(logits, axis=-1, keepdims=True)
            p = jnp.exp(logits - m)
            sums.append(jnp.sum(p, axis=-1, keepdims=True))
            maxes.append(m)
            outs.append(jnp.dot(p.astype(BF16), v, preferred_element_type=F32))
        shape = (DIL_BLOCK, LANES)
        acc_scr[g, q_rows, :] = _merge_head_pair(outs[0], outs[1])
        m_scr[g, q_rows, :] = _merge_head_pair(jnp.broadcast_to(maxes[0], shape),
                                               jnp.broadcast_to(maxes[1], shape))
        l_scr[g, q_rows, :] = _merge_head_pair(jnp.broadcast_to(sums[0], shape),
                                               jnp.broadcast_to(sums[1], shape))

    block(0, 0, 0, DIL_BLOCK, 1)

    def g0_step(n, c):
        q_start = pl.multiple_of(n * DIL_BLOCK, DIL_BLOCK)
        k_start = pl.multiple_of((n - 1) * DIL_BLOCK, DIL_BLOCK)
        block(0, q_start, k_start, 2 * DIL_BLOCK, 1)
        return c

    lax.fori_loop(1, seq // DIL_BLOCK, g0_step, 0)

    for g in (1, 2):
        dil = DILATED_PAIRS[g][1]
        class_rows = DIL_BLOCK * dil
        for r in range(dil):
            block(g, r, r, DIL_BLOCK, dil)
            for n in range(1, seq // class_rows):
                block(g, r + n * class_rows, r + (n - 1) * class_rows, 2 * DIL_BLOCK, dil)

    def merge_step(c, carry):
        rows = pl.ds(pl.multiple_of(c * ATT_BLOCK, ATT_BLOCK), ATT_BLOCK)
        m0, m1, m2 = m_scr[0, rows, :], m_scr[1, rows, :], m_scr[2, rows, :]
        m = jnp.maximum(jnp.maximum(m0, m1), m2)
        w0, w1, w2 = jnp.exp(m0 - m), jnp.exp(m1 - m), jnp.exp(m2 - m)
        num = w0 * acc_scr[0, rows, :] + w1 * acc_scr[1, rows, :] + w2 * acc_scr[2, rows, :]
        den = w0 * l_scr[0, rows, :] + w1 * l_scr[1, rows, :] + w2 * l_scr[2, rows, :]
        o_ref[rows, :] = (num / den).astype(o_ref.dtype)
        return carry

    lax.fori_loop(0, seq // ATT_BLOCK, merge_step, 0)


def _t5_causal_bucket(distance):
    max_exact = N_REL_BUCKETS // 2
    d = jnp.maximum(distance, 1).astype(F32)
    log_b = max_exact + (jnp.log(d / max_exact) / math.log(REL_MAX_DISTANCE / max_exact)
                         * (N_REL_BUCKETS - max_exact)).astype(jnp.int32)
    log_b = jnp.minimum(log_b, N_REL_BUCKETS - 1)
    return jnp.where(distance < max_exact, distance, log_b)


def _dilated_bias(rel_bias):
    qi = jnp.arange(DIL_BLOCK, dtype=jnp.int32)
    kj = jnp.arange(2 * DIL_BLOCK, dtype=jnp.int32)
    delta = qi[:, None] + DIL_BLOCK - kj[None, :]
    in_band = (delta >= 0) & (delta <= DIL_SPAN)
    tables = []
    for _, dil in DILATED_PAIRS:
        bias = rel_bias[_t5_causal_bucket(jnp.maximum(delta, 0) * dil)]
        tables.append(jnp.where(in_band[..., None], bias, NEG_INF))
    return jnp.stack(tables).transpose(0, 3, 1, 2).astype(F32)


def _dilated_attention(proj, rel_bias, batch, seq):
    t = proj.shape[0]
    n_groups = len(DILATED_PAIRS)

    def section(s):
        return pl.BlockSpec((seq, LANES), lambda b, p: (b, s * HEAD_PAIRS + p))

    group_scratch = pltpu.VMEM((n_groups, seq, LANES), F32)
    return pl.pallas_call(
        _dil_body,
        out_shape=jax.ShapeDtypeStruct((t, D_MODEL), BF16),
        grid=(batch, HEAD_PAIRS),
        in_specs=[section(s) for s in range(2 * n_groups + 1)] + [
            pl.BlockSpec((n_groups, 2, DIL_BLOCK, 2 * DIL_BLOCK), lambda b, p: (0, p, 0, 0)),
        ],
        out_specs=pl.BlockSpec((seq, LANES), lambda b, p: (b, p)),
        scratch_shapes=[group_scratch, group_scratch, pltpu.VMEM((seq, LANES), F32),
                        group_scratch, group_scratch, group_scratch],
        compiler_params=_params("parallel", "parallel"),
        name="dilated_attention",
    )(*([proj] * (2 * n_groups + 1)), _dilated_bias(rel_bias))


def _tile_heads(v):
    return jnp.tile(v.astype(F32), N_HEADS)


def _sb_mixer(x, gain, w_qkv, w_o, batch, seq):
    ones = jnp.ones((D_MODEL,), F32)
    col_scale = jnp.concatenate([ones * QK_SCALE, ones, ones])
    proj = _norm_proj(x, gain, w_qkv, col_scale, 0)
    return _out_proj_residual(x, _sb_attention(proj, batch, seq), w_o)


def _dilated_mixer(x, gain, w_in, q_norm, k_norm, rel_bias, w_o, batch, seq):
    scales = []
    for g in range(len(DILATED_PAIRS)):
        scales += [_tile_heads(q_norm[g]) * QK_SCALE, _tile_heads(k_norm[g])]
    scales.append(jnp.ones((D_MODEL,), F32))
    proj = _norm_proj(x, gain, w_in, jnp.concatenate(scales), 2 * len(DILATED_PAIRS))
    return _out_proj_residual(x, _dilated_attention(proj, rel_bias, batch, seq), w_o)


def _fox_mixer(x, gain, w_in, b_f, q_norm, k_norm, w_o, batch, seq):
    col_scale = jnp.concatenate([_tile_heads(q_norm) * QK_SCALE, _tile_heads(k_norm),
                                 jnp.ones((D_MODEL,), F32)])
    proj = _norm_proj(x, gain, w_in[:, :3 * D_MODEL], col_scale, 2)
    cum = _fox_cum_log_forget(x, gain, w_in[:, 3 * D_MODEL:], b_f, batch, seq)
    return _out_proj_residual(x, _fox_attention(proj, cum, batch, seq), w_o)


def kernel(x, sb_w_qkv, sb_w_o, dil_w_in, dil_q_norm, dil_k_norm, dil_w_o, fox_w_in, fox_b_f,
           fox_q_norm, fox_k_norm, fox_w_o, rel_bias, attn_norm, ffn_norm, mlp_w_gate, mlp_w_up,
           mlp_w_down, moe_router, moe_w_gate, moe_w_up, moe_w_down):
    batch, seq, d = x.shape
    depth = attn_norm.shape[0]
    h = x.reshape(batch * seq, d)
    for i in range(depth):
        kind, j = i % 3, i // 3
        if kind == 0:
            h = _sb_mixer(h, attn_norm[i], sb_w_qkv[j], sb_w_o[j], batch, seq)
        elif kind == 1:
            h = _dilated_mixer(h, attn_norm[i], dil_w_in[j], dil_q_norm[j], dil_k_norm[j], rel_bias,
                               dil_w_o[j], batch, seq)
        else:
            h = _fox_mixer(h, attn_norm[i], fox_w_in[j], fox_b_f[j], fox_q_norm[j], fox_k_norm[j],
                           fox_w_o[j], batch, seq)
        f = i // 2
        if i % 2 == 0:
            h = _ffn_residual(h, ffn_norm[i], mlp_w_gate[f], mlp_w_up[f], mlp_w_down[f])
        else:
            h = _moe_residual(h, ffn_norm[i], moe_router[f], moe_w_gate[f], moe_w_up[f], moe_w_down[f])
    return h.reshape(batch, seq, d)
```

```python
import functools
import math

import jax
import jax.numpy as jnp
from jax import lax
from jax.experimental import pallas as pl
from jax.experimental.pallas import tpu as pltpu
from jax.experimental.pallas import tpu_sc as plsc

D_MODEL = 1024
N_HEADS = 16
HEAD_DIM = 64
LANES = 128
HEAD_PAIRS = D_MODEL // LANES
D_FF = 3584
N_EXPERTS = 8
N_REL_BUCKETS = 32
REL_MAX_DISTANCE = 2048
DILATED_PAIRS = ((128, 1), (512, 4), (2048, 16))
DIL_SPAN = 128
RMS_EPS = 1e-6
NEG_INF = -1e30
SB_UNDERFLOW_LOG = -104.0
QK_SCALE = 1.0 / math.sqrt(HEAD_DIM)

ROW_TILE = 1024
COL_TILE = 1024
FF_TILE = 512
MOE_ROW_TILE = 512
MOE_FF_TILE = 1792
SC_CORES = 2
SC_SUBCORES = 16
SC_GATHER_ROWS = 128
NORM_CHUNK = 256
ATT_BLOCK = 256
ATT_Q_BLOCK = 512
ATT_K_PER_Q = ATT_Q_BLOCK // ATT_BLOCK
DIL_BLOCK = 128
VMEM_LIMIT = 56 * 1024 * 1024

F32 = jnp.float32
BF16 = jnp.bfloat16


def _params(*semantics):
    return pltpu.CompilerParams(dimension_semantics=semantics, vmem_limit_bytes=VMEM_LIMIT)


def _rms_normalize(x, gain):
    inv = lax.rsqrt(jnp.mean(x * x, axis=-1, keepdims=True) + RMS_EPS)
    return x * inv * gain


def _softplus(z):
    return jnp.maximum(z, 0.0) + jnp.log(1.0 + jnp.exp(-jnp.abs(z)))


def _norm_proj_body(x_ref, g_ref, w_ref, cs_ref, bd_ref, o_ref, h_scr, *, n_norm):
    j = pl.program_id(1)

    @pl.when(j == 0)
    def _():
        h_scr[...] = _rms_normalize(x_ref[...], g_ref[...]).astype(BF16)

    acc = jnp.dot(h_scr[...], w_ref[...], preferred_element_type=F32)

    def plain():
        o_ref[...] = (acc * cs_ref[...]).astype(o_ref.dtype)

    def head_normed():
        for c in range(COL_TILE // NORM_CHUNK):
            sl = slice(c * NORM_CHUNK, (c + 1) * NORM_CHUNK)
            a = acc[:, sl]
            ss = jnp.dot((a * a).astype(BF16), bd_ref[...], preferred_element_type=F32)
            inv = lax.rsqrt(ss * (1.0 / HEAD_DIM) + RMS_EPS)
            o_ref[:, sl] = (a * inv * cs_ref[:, sl]).astype(o_ref.dtype)

    if n_norm == 0:
        plain()
    else:
        pl.when(j < n_norm)(head_normed)
        pl.when(j >= n_norm)(plain)


def _norm_proj(x, gain, w, col_scale, n_norm):
    t, d = x.shape
    n = w.shape[1]
    head_id = jnp.arange(NORM_CHUNK) // HEAD_DIM
    block_diag = (head_id[:, None] == head_id[None, :]).astype(BF16)
    return pl.pallas_call(
        functools.partial(_norm_proj_body, n_norm=n_norm),
        out_shape=jax.ShapeDtypeStruct((t, n), BF16),
        grid=(t // ROW_TILE, n // COL_TILE),
        in_specs=[
            pl.BlockSpec((ROW_TILE, d), lambda i, j: (i, 0)),
            pl.BlockSpec((1, d), lambda i, j: (0, 0)),
            pl.BlockSpec((d, COL_TILE), lambda i, j: (0, j)),
            pl.BlockSpec((1, COL_TILE), lambda i, j: (0, j)),
            pl.BlockSpec((NORM_CHUNK, NORM_CHUNK), lambda i, j: (0, 0)),
        ],
        out_specs=pl.BlockSpec((ROW_TILE, COL_TILE), lambda i, j: (i, j)),
        scratch_shapes=[pltpu.VMEM((ROW_TILE, d), BF16)],
        compiler_params=_params("parallel", "arbitrary"),
        name="norm_proj",
    )(x, gain.reshape(1, d), w.astype(BF16), col_scale.reshape(1, n).astype(F32), block_diag)


def _out_proj_body(x_ref, o_ref, w_ref, y_ref):
    y_ref[...] = x_ref[...] + jnp.dot(o_ref[...], w_ref[...], preferred_element_type=F32)


def _out_proj_residual(x, o, w):
    t, d = x.shape
    return pl.pallas_call(
        _out_proj_body,
        out_shape=jax.ShapeDtypeStruct((t, d), F32),
        grid=(t // ROW_TILE,),
        in_specs=[
            pl.BlockSpec((ROW_TILE, d), lambda i: (i, 0)),
            pl.BlockSpec((ROW_TILE, d), lambda i: (i, 0)),
            pl.BlockSpec((d, d), lambda i: (0, 0)),
        ],
        out_specs=pl.BlockSpec((ROW_TILE, d), lambda i: (i, 0)),
        compiler_params=_params("parallel"),
        name="out_proj",
    )(x, o, w.astype(BF16))


def _swiglu_hidden(h, wg, wu):
    g = jnp.dot(h, wg, preferred_element_type=F32)
    u = jnp.dot(h, wu, preferred_element_type=F32)
    return g * (1.0 / (1.0 + jnp.exp(-g))) * u


def _ffn_body(x_ref, g_ref, wg_ref, wu_ref, wd_ref, y_ref, h_scr):
    f = pl.program_id(1)

    @pl.when(f == 0)
    def _():
        x = x_ref[...]
        h_scr[...] = _rms_normalize(x, g_ref[...]).astype(BF16)
        y_ref[...] = x

    a = _swiglu_hidden(h_scr[...], wg_ref[...], wu_ref[...]).astype(BF16)
    y_ref[...] += jnp.dot(a, wd_ref[...], preferred_element_type=F32)


def _ffn_residual(x, gain, w_gate, w_up, w_down):
    t, d = x.shape
    ff = w_gate.shape[1]
    return pl.pallas_call(
        _ffn_body,
        out_shape=jax.ShapeDtypeStruct((t, d), F32),
        grid=(t // ROW_TILE, ff // FF_TILE),
        in_specs=[
            pl.BlockSpec((ROW_TILE, d), lambda i, f: (i, 0)),
            pl.BlockSpec((1, d), lambda i, f: (0, 0)),
            pl.BlockSpec((d, FF_TILE), lambda i, f: (0, f)),
            pl.BlockSpec((d, FF_TILE), lambda i, f: (0, f)),
            pl.BlockSpec((FF_TILE, d), lambda i, f: (f, 0)),
        ],
        out_specs=pl.BlockSpec((ROW_TILE, d), lambda i, f: (i, 0)),
        scratch_shapes=[pltpu.VMEM((ROW_TILE, d), BF16)],
        compiler_params=_params("parallel", "arbitrary"),
        name="ffn",
    )(x, gain.reshape(1, d), w_gate.astype(BF16), w_up.astype(BF16), w_down.astype(BF16))


def _pack_bf16_pairs(x):
    half = x.shape[1] // 2
    bits = pltpu.bitcast(x.astype(BF16).astype(F32), jnp.int32)
    return bits[:, :half] | lax.shift_right_logical(bits[:, half:], jnp.int32(16))


def _unpack_bf16_pairs(p):
    left = pltpu.bitcast(p & jnp.int32(-65536), F32)
    right = pltpu.bitcast(lax.shift_left(p, jnp.int32(16)), F32)
    return jnp.concatenate([left, right], axis=1).astype(BF16)


def _router_body(x_ref, g_ref, r_ref, tri_ref, gates_ref, rank_ref, h_ref, count_scr):
    @pl.when(pl.program_id(0) == 0)
    def _():
        count_scr[...] = jnp.zeros_like(count_scr)

    h = _rms_normalize(x_ref[...], g_ref[...])
    h_ref[...] = _pack_bf16_pairs(h)
    logits = jnp.dot(h, r_ref[...], preferred_element_type=F32, precision=lax.Precision.HIGHEST)
    lane = lax.broadcasted_iota(jnp.int32, logits.shape, 1).astype(F32)
    logits = jnp.where(lane < N_EXPERTS, logits, -jnp.inf)
    m1 = jnp.max(logits, axis=-1, keepdims=True)
    i1 = jnp.min(jnp.where(logits == m1, lane, float(LANES)), axis=-1, keepdims=True)
    rest = jnp.where(lane == i1, -jnp.inf, logits)
    m2 = jnp.max(rest, axis=-1, keepdims=True)
    i2 = jnp.min(jnp.where(rest == m2, lane, float(LANES)), axis=-1, keepdims=True)
    e = jnp.exp(m2 - m1)
    g1 = 1.0 / (1.0 + e)
    gates_ref[...] = jnp.where(lane == i1, g1, 0.0) + jnp.where(lane == i2, e * g1, 0.0)
    chosen = jnp.where((lane == i1) | (lane == i2), 1.0, 0.0)
    inclusive = jnp.dot(tri_ref[...], chosen.astype(BF16), preferred_element_type=F32)
    rank_ref[...] = jnp.where(chosen > 0.0, inclusive - 1.0 + count_scr[...], -1.0)
    count_scr[...] += inclusive[ROW_TILE - 1:ROW_TILE, :]


def _router(x, gain, router):
    t, d = x.shape
    r = jnp.zeros((d, LANES), F32).at[:, :N_EXPERTS].set(router)
    idx = jnp.arange(ROW_TILE)
    tri = (idx[:, None] >= idx[None, :]).astype(BF16)
    return pl.pallas_call(
        _router_body,
        out_shape=(jax.ShapeDtypeStruct((t, LANES), F32), jax.ShapeDtypeStruct((t, LANES), F32),
                   jax.ShapeDtypeStruct((t, d // 2), jnp.int32)),
        grid=(t // ROW_TILE,),
        in_specs=[
            pl.BlockSpec((ROW_TILE, d), lambda i: (i, 0)),
            pl.BlockSpec((1, d), lambda i: (0, 0)),
            pl.BlockSpec((d, LANES), lambda i: (0, 0)),
            pl.BlockSpec((ROW_TILE, ROW_TILE), lambda i: (0, 0)),
        ],
        out_specs=(pl.BlockSpec((ROW_TILE, LANES), lambda i: (i, 0)),
                   pl.BlockSpec((ROW_TILE, LANES), lambda i: (i, 0)),
                   pl.BlockSpec((ROW_TILE, d // 2), lambda i: (i, 0))),
        scratch_shapes=[pltpu.VMEM((1, LANES), F32)],
        compiler_params=_params("arbitrary"),
        name="router",
    )(x, gain.reshape(1, d), r, tri)


def _sc_row_gather(table, idx):
    width = table.shape[1]
    n = idx.shape[0]
    workers = SC_CORES * SC_SUBCORES
    per_worker = n // workers
    assert n % (workers * SC_GATHER_ROWS) == 0
    mesh = plsc.VectorSubcoreMesh(core_axis_name="core", subcore_axis_name="subcore",
                                  num_cores=SC_CORES, num_subcores=SC_SUBCORES)

    @functools.partial(
        pl.kernel, mesh=mesh, out_type=jax.ShapeDtypeStruct((n, width), table.dtype),
        scratch_types=[pltpu.VMEM((SC_GATHER_ROWS,), jnp.int32),
                       pltpu.VMEM((SC_GATHER_ROWS, width), table.dtype),
                       pltpu.SemaphoreType.DMA],
        name="sc_row_gather")
    def gather(table_hbm, idx_hbm, out_hbm, idx_v, rows_v, sem):
        worker = lax.axis_index("subcore") * SC_CORES + lax.axis_index("core")

        @pl.loop(0, per_worker // SC_GATHER_ROWS)
        def _(c):
            off = pl.multiple_of(worker * per_worker + c * SC_GATHER_ROWS, SC_GATHER_ROWS)
            pltpu.sync_copy(idx_hbm.at[pl.ds(off, SC_GATHER_ROWS)], idx_v)
            pltpu.async_copy(table_hbm.at[idx_v], rows_v, sem).wait()
            pltpu.sync_copy(rows_v, out_hbm.at[pl.ds(off, SC_GATHER_ROWS)])

    return gather(table, idx)


def _expert_ffn_body(te_ref, nu_ref, h_ref, wg_ref, wu_ref, wd_ref, y_ref, acc_scr):
    i = pl.program_id(0)
    f = pl.program_id(1)

    @pl.when(i < nu_ref[0])
    def _():
        a = _swiglu_hidden(_unpack_bf16_pairs(h_ref[...]), wg_ref[...], wu_ref[...]).astype(BF16)
        y = jnp.dot(a, wd_ref[...], preferred_element_type=F32)

        @pl.when(f == 0)
        def _():
            acc_scr[...] = y

        @pl.when(f > 0)
        def _():
            acc_scr[...] += y

        @pl.when(f == pl.num_programs(1) - 1)
        def _():
            y_ref[...] = _pack_bf16_pairs(acc_scr[...])


def _expert_ffn(h_sorted, tile_expert, n_used, w_gate, w_up, w_down):
    rows, half = h_sorted.shape
    d = 2 * half
    ff = w_gate.shape[2]
    n_f = ff // MOE_FF_TILE

    def row_map(i, f, te, nu):
        return (jnp.minimum(i, nu[0] - 1), 0)

    def col_step(i, f, nu):
        return jnp.where(i < nu[0], f, n_f - 1)

    grid_spec = pltpu.PrefetchScalarGridSpec(
        num_scalar_prefetch=2,
        grid=(rows // MOE_ROW_TILE, n_f),
        in_specs=[
            pl.BlockSpec((MOE_ROW_TILE, half), row_map),
            pl.BlockSpec((None, d, MOE_FF_TILE), lambda i, f, te, nu: (te[i], 0, col_step(i, f, nu))),
            pl.BlockSpec((None, d, MOE_FF_TILE), lambda i, f, te, nu: (te[i], 0, col_step(i, f, nu))),
            pl.BlockSpec((None, MOE_FF_TILE, d), lambda i, f, te, nu: (te[i], col_step(i, f, nu), 0)),
        ],
        out_specs=pl.BlockSpec((MOE_ROW_TILE, half), row_map),
        scratch_shapes=[pltpu.VMEM((MOE_ROW_TILE, d), F32)],
    )
    return pl.pallas_call(
        _expert_ffn_body,
        out_shape=jax.ShapeDtypeStruct((rows, half), jnp.int32),
        grid_spec=grid_spec,
        compiler_params=_params("arbitrary", "arbitrary"),
        name="expert_ffn",
    )(tile_expert, n_used, h_sorted, w_gate.astype(BF16), w_up.astype(BF16), w_down.astype(BF16))


def _combine_body(x_ref, y_ref, g_ref, o_ref):
    g = g_ref[...]
    first = _unpack_bf16_pairs(y_ref[0]).astype(F32) * g[:, 0:1]
    second = _unpack_bf16_pairs(y_ref[1]).astype(F32) * g[:, 1:2]
    o_ref[...] = x_ref[...] + first + second


def _combine_residual(x, y_pairs, gates2):
    t, d = x.shape
    return pl.pallas_call(
        _combine_body,
        out_shape=jax.ShapeDtypeStruct((t, d), F32),
        grid=(t // ROW_TILE,),
        in_specs=[
            pl.BlockSpec((ROW_TILE, d), lambda i: (i, 0)),
            pl.BlockSpec((2, ROW_TILE, d // 2), lambda i: (0, i, 0)),
            pl.BlockSpec((ROW_TILE, 2), lambda i: (i, 0)),
        ],
        out_specs=pl.BlockSpec((ROW_TILE, d), lambda i: (i, 0)),
        compiler_params=_params("parallel"),
        name="moe_combine",
    )(x, y_pairs, gates2)


def _moe_residual(x, gain, router, w_gate, w_up, w_down):
    t, d = x.shape
    gates, rank, h_packed = _router(x, gain, router)
    rank8 = rank[:, :N_EXPERTS].astype(jnp.int32)
    chosen = rank8 >= 0
    counts = jnp.sum(chosen, axis=0, dtype=jnp.int32)
    padded = (counts + MOE_ROW_TILE - 1) // MOE_ROW_TILE * MOE_ROW_TILE
    ends = jnp.cumsum(padded)
    pos = (ends - padded)[None, :] + rank8
    max_rows = 2 * t + N_EXPERTS * MOE_ROW_TILE
    pos_lo = jnp.min(jnp.where(chosen, pos, max_rows), axis=1)
    pos_hi = jnp.max(jnp.where(chosen, pos, -1), axis=1)
    exp_lo = jnp.argmax(chosen, axis=1)
    exp_hi = N_EXPERTS - 1 - jnp.argmax(chosen[:, ::-1], axis=1)
    gates8 = gates[:, :N_EXPERTS]
    gates2 = jnp.stack([jnp.take_along_axis(gates8, exp_lo[:, None], axis=1)[:, 0],
                        jnp.take_along_axis(gates8, exp_hi[:, None], axis=1)[:, 0]], axis=1)
    slots = jnp.concatenate([pos_lo, pos_hi])
    token = jnp.arange(t, dtype=jnp.int32)
    source = jnp.zeros((max_rows,), jnp.int32).at[slots].set(jnp.concatenate([token, token]),
                                                             unique_indices=True)
    n_tiles = max_rows // MOE_ROW_TILE
    n_used = (ends[-1] // MOE_ROW_TILE).astype(jnp.int32)
    tile_start = jnp.minimum(jnp.arange(n_tiles, dtype=jnp.int32), n_used - 1) * MOE_ROW_TILE
    tile_expert = jnp.sum(tile_start[:, None] >= ends[None, :], axis=1, dtype=jnp.int32)

    h_sorted = _sc_row_gather(h_packed, source)
    y_sorted = _expert_ffn(h_sorted, tile_expert, n_used.reshape(1), w_gate, w_up, w_down)
    y_pairs = _sc_row_gather(y_sorted, slots.astype(jnp.int32)).reshape(2, t, d // 2)
    return _combine_residual(x, y_pairs, gates2)


def _split_head_pair(q):
    is_first = lax.broadcasted_iota(jnp.int32, (1, LANES), 1) < HEAD_DIM
    zero = jnp.zeros_like(q)
    return jnp.where(is_first, q, zero), jnp.where(is_first, zero, q)


def _merge_head_pair(first, second):
    is_first = lax.broadcasted_iota(jnp.int32, (1, LANES), 1) < HEAD_DIM
    return jnp.where(is_first, first, second)


def _qk(q, k):
    return lax.dot_general(q, k, (((1,), (1,)), ((), ())), preferred_element_type=F32)


def _transpose_values(v_ref, vt_scr):
    for c in range(v_ref.shape[0] // ATT_BLOCK):
        rows = slice(c * ATT_BLOCK, (c + 1) * ATT_BLOCK)
        vt_scr[:, rows] = v_ref[rows, :].astype(F32).T.astype(vt_scr.dtype)


def _sb_body(q_ref, k_ref, v_ref, u_ref, o_ref, vt_scr, acc_scr, carry_scr):
    qi = pl.program_id(2)

    @pl.when(qi == 0)
    def _():
        _transpose_values(v_ref, vt_scr)

    upper = u_ref[...]
    acc_scr[...] = jnp.zeros_like(acc_scr)
    carry_scr[...] = jnp.zeros_like(carry_scr)
    key = lax.broadcasted_iota(jnp.int32, (ATT_BLOCK, ATT_BLOCK), 0)
    query = lax.broadcasted_iota(jnp.int32, (ATT_BLOCK, ATT_BLOCK), 1)
    strict = key < query
    qs = [_split_head_pair(q_ref[j * ATT_BLOCK:(j + 1) * ATT_BLOCK, :]) for j in range(ATT_K_PER_Q)]

    def process(j, kb, keep):
        start = pl.multiple_of(kb * ATT_BLOCK, ATT_BLOCK)
        k = k_ref[pl.ds(start, ATT_BLOCK), :]
        for hh in range(2):
            z = _qk(k, qs[j][hh])
            log_beta = z - _softplus(z)
            log_keep = log_beta - z
            if keep is not None:
                log_keep = jnp.where(keep, log_keep, 0.0)
            remain = jnp.dot(upper, log_keep.astype(BF16), preferred_element_type=F32)
            w = jnp.exp(log_beta + remain + carry_scr[j, hh])
            if keep is not None:
                w = jnp.where(keep, w, 0.0)
            carry_scr[j, hh] += jnp.sum(log_keep, axis=0, keepdims=True)
            vt = vt_scr[pl.ds(hh * HEAD_DIM, HEAD_DIM), pl.ds(start, ATT_BLOCK)]
            acc_scr[j, hh] += jnp.dot(vt, w.astype(BF16), preferred_element_type=F32)

    for j in range(ATT_K_PER_Q):
        g = qi * ATT_K_PER_Q + j
        process(j, g, strict)
        if j == 0:
            has_previous = jnp.broadcast_to(g > 0, strict.shape)
            process(j, jnp.maximum(g - 1, 0), has_previous)
        else:
            process(j, g - 1, None)

    for j in range(ATT_K_PER_Q):
        def more(kb, j=j):
            return (kb >= 0) & (jnp.max(carry_scr[j]) > SB_UNDERFLOW_LOG)

        def step(kb, j=j):
            process(j, kb, None)
            return kb - 1

        lax.while_loop(more, step, qi * ATT_K_PER_Q + j - 2)

    for j in range(ATT_K_PER_Q):
        out_t = jnp.concatenate([acc_scr[j, 0], acc_scr[j, 1]], axis=0)
        o_ref[j * ATT_BLOCK:(j + 1) * ATT_BLOCK, :] = out_t.T.astype(o_ref.dtype)


def _sb_attention(proj, batch, seq):
    t = proj.shape[0]
    nq = seq // ATT_Q_BLOCK
    idx = jnp.arange(ATT_BLOCK)
    upper = (idx[None, :] > idx[:, None]).astype(BF16)
    return pl.pallas_call(
        _sb_body,
        out_shape=jax.ShapeDtypeStruct((t, D_MODEL), BF16),
        grid=(batch, HEAD_PAIRS, nq),
        in_specs=[
            pl.BlockSpec((ATT_Q_BLOCK, LANES), lambda b, p, i: (b * nq + i, p)),
            pl.BlockSpec((seq, LANES), lambda b, p, i: (b, HEAD_PAIRS + p)),
            pl.BlockSpec((seq, LANES), lambda b, p, i: (b, 2 * HEAD_PAIRS + p)),
            pl.BlockSpec((ATT_BLOCK, ATT_BLOCK), lambda b, p, i: (0, 0)),
        ],
        out_specs=pl.BlockSpec((ATT_Q_BLOCK, LANES), lambda b, p, i: (b * nq + i, p)),
        scratch_shapes=[
            pltpu.VMEM((LANES, seq), BF16),
            pltpu.VMEM((ATT_K_PER_Q, 2, HEAD_DIM, ATT_BLOCK), F32),
            pltpu.VMEM((ATT_K_PER_Q, 2, 1, ATT_BLOCK), F32),
        ],
        compiler_params=_params("parallel", "parallel", "arbitrary"),
        name="sb_attention",
    )(proj, proj, proj, upper)


def _fox_gate_body(x_ref, g_ref, w_ref, b_ref, tri_ref, c_ref):
    h = _rms_normalize(x_ref[...], g_ref[...])
    logits = jnp.dot(h, w_ref[...], preferred_element_type=F32, precision=lax.Precision.HIGHEST)
    log_f = -_softplus(-(logits + b_ref[...]))
    seq = log_f.shape[0]
    carry = jnp.zeros((1, LANES), F32)
    for blk in range(seq // ATT_BLOCK):
        rows = slice(blk * ATT_BLOCK, (blk + 1) * ATT_BLOCK)
        c = jnp.dot(tri_ref[...], log_f[rows], preferred_element_type=F32,
                    precision=lax.Precision.HIGHEST) + carry
        c_ref[rows, :] = c
        carry = c[ATT_BLOCK - 1:ATT_BLOCK, :]


def _fox_cum_log_forget(x, gain, w_gate, b_gate, batch, seq):
    t, d = x.shape
    w = jnp.zeros((d, LANES), F32).at[:, :N_HEADS].set(w_gate)
    b = jnp.zeros((1, LANES), F32).at[0, :N_HEADS].set(b_gate)
    idx = jnp.arange(ATT_BLOCK)
    tri = (idx[:, None] >= idx[None, :]).astype(F32)
    return pl.pallas_call(
        _fox_gate_body,
        out_shape=jax.ShapeDtypeStruct((t, LANES), F32),
        grid=(batch,),
        in_specs=[
            pl.BlockSpec((seq, d), lambda i: (i, 0)),
            pl.BlockSpec((1, d), lambda i: (0, 0)),
            pl.BlockSpec((d, LANES), lambda i: (0, 0)),
            pl.BlockSpec((1, LANES), lambda i: (0, 0)),
            pl.BlockSpec((ATT_BLOCK, ATT_BLOCK), lambda i: (0, 0)),
        ],
        out_specs=pl.BlockSpec((seq, LANES), lambda i: (i, 0)),
        compiler_params=_params("parallel"),
        name="fox_gate",
    )(x, gain.reshape(1, d), w, b, tri)


def _fox_body(q_ref, k_ref, v_ref, cq_ref, ck_ref, o_ref, vt_scr, ckb_scr, acc_scr, m_scr, l_scr):
    qi = pl.program_id(2)

    @pl.when(qi == 0)
    def _():
        _transpose_values(v_ref, vt_scr)
        for hh in range(2):
            ckb_scr[hh] = jnp.broadcast_to(ck_ref[0, hh], ckb_scr.shape[1:])

    qs = _split_head_pair(q_ref[...])
    acc_scr[...] = jnp.zeros_like(acc_scr)
    m_scr[...] = jnp.full_like(m_scr, NEG_INF)
    l_scr[...] = jnp.zeros_like(l_scr)
    key = lax.broadcasted_iota(jnp.int32, (ATT_BLOCK, ATT_Q_BLOCK), 0)
    query = lax.broadcasted_iota(jnp.int32, (ATT_BLOCK, ATT_Q_BLOCK), 1)

    def process(kb, diagonal):
        start = pl.multiple_of(kb * ATT_BLOCK, ATT_BLOCK)
        k = k_ref[pl.ds(start, ATT_BLOCK), :]
        if diagonal:
            causal = key + (kb * ATT_BLOCK - qi * ATT_Q_BLOCK) <= query
        for hh in range(2):
            ck = ckb_scr[hh, pl.ds(start, ATT_BLOCK), :]
            a = _qk(k, qs[hh]) - jnp.concatenate([ck] * (ATT_Q_BLOCK // LANES), axis=1)
            if diagonal:
                a = jnp.where(causal, a, NEG_INF)
            cq = cq_ref[0, hh]
            m_old = m_scr[hh]
            m_new = jnp.maximum(m_old, jnp.max(a, axis=0, keepdims=True) + cq)
            alpha = jnp.exp(m_old - m_new)
            p = jnp.exp(a + (cq - m_new))
            l_scr[hh] = alpha * l_scr[hh] + jnp.sum(p, axis=0, keepdims=True)
            vt = vt_scr[pl.ds(hh * HEAD_DIM, HEAD_DIM), pl.ds(start, ATT_BLOCK)]
            acc_scr[hh] = alpha * acc_scr[hh] + jnp.dot(vt, p.astype(BF16), preferred_element_type=F32)
            m_scr[hh] = m_new

    def step(it, c):
        for r in range(ATT_K_PER_Q):
            process(it * ATT_K_PER_Q + r, False)
        return c

    lax.fori_loop(0, qi, step, 0)
    for r in range(ATT_K_PER_Q):
        process(qi * ATT_K_PER_Q + r, True)
    out_t = jnp.concatenate([acc_scr[0] / l_scr[0], acc_scr[1] / l_scr[1]], axis=0)
    o_ref[...] = out_t.T.astype(o_ref.dtype)


def _fox_attention(proj, cum, batch, seq):
    t = proj.shape[0]
    nq = seq // ATT_Q_BLOCK
    cum_h = cum[:, :N_HEADS].reshape(batch, seq, N_HEADS).transpose(0, 2, 1)
    cum_q = cum_h.reshape(batch, N_HEADS, 1, seq)
    cum_k = cum_h.reshape(batch, N_HEADS, seq, 1)
    return pl.pallas_call(
        _fox_body,
        out_shape=jax.ShapeDtypeStruct((t, D_MODEL), BF16),
        grid=(batch, HEAD_PAIRS, nq),
        in_specs=[
            pl.BlockSpec((ATT_Q_BLOCK, LANES), lambda b, p, i: (b * nq + i, p)),
            pl.BlockSpec((seq, LANES), lambda b, p, i: (b, HEAD_PAIRS + p)),
            pl.BlockSpec((seq, LANES), lambda b, p, i: (b, 2 * HEAD_PAIRS + p)),
            pl.BlockSpec((1, 2, 1, ATT_Q_BLOCK), lambda b, p, i: (b, p, 0, i)),
            pl.BlockSpec((1, 2, seq, 1), lambda b, p, i: (b, p, 0, 0)),
        ],
        out_specs=pl.BlockSpec((ATT_Q_BLOCK, LANES), lambda b, p, i: (b * nq + i, p)),
        scratch_shapes=[
            pltpu.VMEM((LANES, seq), BF16),
            pltpu.VMEM((2, seq, LANES), F32),
            pltpu.VMEM((2, HEAD_DIM, ATT_Q_BLOCK), F32),
            pltpu.VMEM((2, 1, ATT_Q_BLOCK), F32),
            pltpu.VMEM((2, 1, ATT_Q_BLOCK), F32),
        ],
        compiler_params=_params("parallel", "parallel", "arbitrary"),
        name="fox_attention",
    )(proj, proj, proj, cum_q, cum_k)


def _dil_body(q1_ref, k1_ref, q2_ref, k2_ref, q3_ref, k3_ref, v_ref, bias_ref, o_ref,
              q_scr, k_scr, v_scr, acc_scr, m_scr, l_scr):
    seq = v_ref.shape[0]
    for g, (q_ref, k_ref) in enumerate(((q1_ref, k1_ref), (q2_ref, k2_ref), (q3_ref, k3_ref))):
        q_scr[g] = q_ref[...].astype(F32)
        k_scr[g] = k_ref[...].astype(F32)
    v_scr[...] = v_ref[...].astype(F32)

    def block(g, q_start, k_start, n_keys, stride):
        q_rows = pl.ds(q_start, DIL_BLOCK, stride=stride) if stride > 1 else pl.ds(q_start, DIL_BLOCK)
        k_rows = pl.ds(k_start, n_keys, stride=stride) if stride > 1 else pl.ds(k_start, n_keys)
        qs = _split_head_pair(q_scr[g, q_rows, :].astype(BF16))
        k = k_scr[g, k_rows, :].astype(BF16)
        v = v_scr[k_rows, :].astype(BF16)
        outs, maxes, sums = [], [], []
        for hh in range(2):
            if n_keys == 2 * DIL_BLOCK:
                bias = bias_ref[g, hh]
            else:
                bias = bias_ref[g, hh, :, DIL_BLOCK:]
            logits = _qk(qs[hh], k) + bias
            m = jnp.max(logits, axis=-1, keepdims=True)
            p = jnp.exp(logits - m)
            sums.append(jnp.sum(p, axis=-1, keepdims=True))
            maxes.append(m)
            outs.append(jnp.dot(p.astype(BF16), v, preferred_element_type=F32))
        shape = (DIL_BLOCK, LANES)
        acc_scr[g, q_rows, :] = _merge_head_pair(outs[0], outs[1])
        m_scr[g, q_rows, :] = _merge_head_pair(jnp.broadcast_to(maxes[0], shape),
                                               jnp.broadcast_to(maxes[1], shape))
        l_scr[g, q_rows, :] = _merge_head_pair(jnp.broadcast_to(sums[0], shape),
                                               jnp.broadcast_to(sums[1], shape))

    block(0, 0, 0, DIL_BLOCK, 1)

    def g0_step(n, c):
        q_start = pl.multiple_of(n * DIL_BLOCK, DIL_BLOCK)
        k_start = pl.multiple_of((n - 1) * DIL_BLOCK, DIL_BLOCK)
        block(0, q_start, k_start, 2 * DIL_BLOCK, 1)
        return c

    lax.fori_loop(1, seq // DIL_BLOCK, g0_step, 0)

    for g in (1, 2):
        dil = DILATED_PAIRS[g][1]
        class_rows = DIL_BLOCK * dil
        for r in range(dil):
            block(g, r, r, DIL_BLOCK, dil)
            for n in range(1, seq // class_rows):
                block(g, r + n * class_rows, r + (n - 1) * class_rows, 2 * DIL_BLOCK, dil)

    def merge_step(c, carry):
        rows = pl.ds(pl.multiple_of(c * ATT_BLOCK, ATT_BLOCK), ATT_BLOCK)
        m0, m1, m2 = m_scr[0, rows, :], m_scr[1, rows, :], m_scr[2, rows, :]
        m = jnp.maximum(jnp.maximum(m0, m1), m2)
        w0, w1, w2 = jnp.exp(m0 - m), jnp.exp(m1 - m), jnp.exp(m2 - m)
        num = w0 * acc_scr[0, rows, :] + w1 * acc_scr[1, rows, :] + w2 * acc_scr[2, rows, :]
        den = w0 * l_scr[0, rows, :] + w1 * l_scr[1, rows, :] + w2 * l_scr[2, rows, :]
        o_ref[rows, :] = (num / den).astype(o_ref.dtype)
        return carry

    lax.fori_loop(0, seq // ATT_BLOCK, merge_step, 0)


def _t5_causal_bucket(distance):
    max_exact = N_REL_BUCKETS // 2
    d = jnp.maximum(distance, 1).astype(F32)
    log_b = max_exact + (jnp.log(d / max_exact) / math.log(REL_MAX_DISTANCE / max_exact)
                         * (N_REL_BUCKETS - max_exact)).astype(jnp.int32)
    log_b = jnp.minimum(log_b, N_REL_BUCKETS - 1)
    return jnp.where(distance < max_exact, distance, log_b)


def _dilated_bias(rel_bias):
    qi = jnp.arange(DIL_BLOCK, dtype=jnp.int32)
    kj = jnp.arange(2 * DIL_BLOCK, dtype=jnp.int32)
    delta = qi[:, None] + DIL_BLOCK - kj[None, :]
    in_band = (delta >= 0) & (delta <= DIL_SPAN)
    tables = []
    for _, dil in DILATED_PAIRS:
        bias = rel_bias[_t5_causal_bucket(jnp.maximum(delta, 0) * dil)]
        tables.append(jnp.where(in_band[..., None], bias, NEG_INF))
    return jnp.stack(tables).transpose(0, 3, 1, 2).astype(F32)


def _dilated_attention(proj, rel_bias, batch, seq):
    t = proj.shape[0]
    n_groups = len(DILATED_PAIRS)

    def section(s):
        return pl.BlockSpec((seq, LANES), lambda b, p: (b, s * HEAD_PAIRS + p))

    group_scratch = pltpu.VMEM((n_groups, seq, LANES), F32)
    return pl.pallas_call(
        _dil_body,
        out_shape=jax.ShapeDtypeStruct((t, D_MODEL), BF16),
        grid=(batch, HEAD_PAIRS),
        in_specs=[section(s) for s in range(2 * n_groups + 1)] + [
            pl.BlockSpec((n_groups, 2, DIL_BLOCK, 2 * DIL_BLOCK), lambda b, p: (0, p, 0, 0)),
        ],
        out_specs=pl.BlockSpec((seq, LANES), lambda b, p: (b, p)),
        scratch_shapes=[group_scratch, group_scratch, pltpu.VMEM((seq, LANES), F32),
                        group_scratch, group_scratch, group_scratch],
        compiler_params=_params("parallel", "parallel"),
        name="dilated_attention",
    )(*([proj] * (2 * n_groups + 1)), _dilated_bias(rel_bias))


def _tile_heads(v):
    return jnp.tile(v.astype(F32), N_HEADS)


def _sb_mixer(x, gain, w_qkv, w_o, batch, seq):
    ones = jnp.ones((D_MODEL,), F32)
    col_scale = jnp.concatenate([ones * QK_SCALE, ones, ones])
    proj = _norm_proj(x, gain, w_qkv, col_scale, 0)
    return _out_proj_residual(x, _sb_attention(proj, batch, seq), w_o)


def _dilated_mixer(x, gain, w_in, q_norm, k_norm, rel_bias, w_o, batch, seq):
    scales = []
    for g in range(len(DILATED_PAIRS)):
        scales += [_tile_heads(q_norm[g]) * QK_SCALE, _tile_heads(k_norm[g])]
    scales.append(jnp.ones((D_MODEL,), F32))
    proj = _norm_proj(x, gain, w_in, jnp.concatenate(scales), 2 * len(DILATED_PAIRS))
    return _out_proj_residual(x, _dilated_attention(proj, rel_bias, batch, seq), w_o)


def _fox_mixer(x, gain, w_in, b_f, q_norm, k_norm, w_o, batch, seq):
    col_scale = jnp.concatenate([_tile_heads(q_norm) * QK_SCALE, _tile_heads(k_norm),
                                 jnp.ones((D_MODEL,), F32)])
    proj = _norm_proj(x, gain, w_in[:, :3 * D_MODEL], col_scale, 2)
    cum = _fox_cum_log_forget(x, gain, w_in[:, 3 * D_MODEL:], b_f, batch, seq)
    return _out_proj_residual(x, _fox_attention(proj, cum, batch, seq), w_o)


def kernel(x, sb_w_qkv, sb_w_o, dil_w_in, dil_q_norm, dil_k_norm, dil_w_o, fox_w_in, fox_b_f,
           fox_q_norm, fox_k_norm, fox_w_o, rel_bias, attn_norm, ffn_norm, mlp_w_gate, mlp_w_up,
           mlp_w_down, moe_router, moe_w_gate, moe_w_up, moe_w_down):
    batch, seq, d = x.shape
    depth = attn_norm.shape[0]
    h = x.reshape(batch * seq, d)
    for i in range(depth):
        kind, j = i % 3, i // 3
        if kind == 0:
            h = _sb_mixer(h, attn_norm[i], sb_w_qkv[j], sb_w_o[j], batch, seq)
        elif kind == 1:
            h = _dilated_mixer(h, attn_norm[i], dil_w_in[j], dil_q_norm[j], dil_k_norm[j], rel_bias,
                               dil_w_o[j], batch, seq)
        else:
            h = _fox_mixer(h, attn_norm[i], fox_w_in[j], fox_b_f[j], fox_q_norm[j], fox_k_norm[j],
                           fox_w_o[j], batch, seq)
        f = i // 2
        if i % 2 == 0:
            h = _ffn_residual(h, ffn_norm[i], mlp_w_gate[f], mlp_w_up[f], mlp_w_down[f])
        else:
            h = _moe_residual(h, ffn_norm[i], moe_router[f], moe_w_gate[f], moe_w_up[f], moe_w_down[f])
    return h.reshape(batch, seq, d)
```

```python
import functools
import math

import jax
import jax.numpy as jnp
from jax import lax
from jax.experimental import pallas as pl
from jax.experimental.pallas import tpu as pltpu
from jax.experimental.pallas import tpu_sc as plsc

D_MODEL = 1024
N_HEADS = 16
HEAD_DIM = 64
LANES = 128
HEAD_PAIRS = D_MODEL // LANES
D_FF = 3584
N_EXPERTS = 8
N_REL_BUCKETS = 32
REL_MAX_DISTANCE = 2048
DILATED_PAIRS = ((128, 1), (512, 4), (2048, 16))
DIL_SPAN = 128
RMS_EPS = 1e-6
NEG_INF = -1e30
SB_UNDERFLOW_LOG = -104.0
QK_SCALE = 1.0 / math.sqrt(HEAD_DIM)

ROW_TILE = 1024
COL_TILE = 1024
FF_TILE = 512
MOE_ROW_TILE = 512
MOE_FF_TILE = 1792
SC_CORES = 2
SC_SUBCORES = 16
SC_CHUNK_ROWS = 64
NORM_CHUNK = 256
ATT_BLOCK = 256
ATT_Q_BLOCK = 512
ATT_K_PER_Q = ATT_Q_BLOCK // ATT_BLOCK
DIL_BLOCK = 128
VMEM_LIMIT = 56 * 1024 * 1024

F32 = jnp.float32
BF16 = jnp.bfloat16


def _params(*semantics):
    return pltpu.CompilerParams(dimension_semantics=semantics, vmem_limit_bytes=VMEM_LIMIT)


def _rms_normalize(x, gain):
    inv = lax.rsqrt(jnp.mean(x * x, axis=-1, keepdims=True) + RMS_EPS)
    return x * inv * gain


def _softplus(z):
    return jnp.maximum(z, 0.0) + jnp.log(1.0 + jnp.exp(-jnp.abs(z)))


def _norm_proj_body(x_ref, g_ref, w_ref, cs_ref, bd_ref, o_ref, h_scr, *, n_norm):
    j = pl.program_id(1)

    @pl.when(j == 0)
    def _():
        h_scr[...] = _rms_normalize(x_ref[...], g_ref[...]).astype(BF16)

    acc = jnp.dot(h_scr[...], w_ref[...], preferred_element_type=F32)

    def plain():
        o_ref[...] = (acc * cs_ref[...]).astype(o_ref.dtype)

    def head_normed():
        for c in range(COL_TILE // NORM_CHUNK):
            sl = slice(c * NORM_CHUNK, (c + 1) * NORM_CHUNK)
            a = acc[:, sl]
            ss = jnp.dot((a * a).astype(BF16), bd_ref[...], preferred_element_type=F32)
            inv = lax.rsqrt(ss * (1.0 / HEAD_DIM) + RMS_EPS)
            o_ref[:, sl] = (a * inv * cs_ref[:, sl]).astype(o_ref.dtype)

    if n_norm == 0:
        plain()
    else:
        pl.when(j < n_norm)(head_normed)
        pl.when(j >= n_norm)(plain)


def _norm_proj(x, gain, w, layer, col_scale, n_norm):
    t, d = x.shape
    n = w.shape[2]
    head_id = jnp.arange(NORM_CHUNK) // HEAD_DIM
    block_diag = (head_id[:, None] == head_id[None, :]).astype(BF16)
    return pl.pallas_call(
        functools.partial(_norm_proj_body, n_norm=n_norm),
        out_shape=jax.ShapeDtypeStruct((t, n), BF16),
        grid=(t // ROW_TILE, n // COL_TILE),
        in_specs=[
            pl.BlockSpec((ROW_TILE, d), lambda i, j: (i, 0)),
            pl.BlockSpec((1, d), lambda i, j: (0, 0)),
            pl.BlockSpec((None, d, COL_TILE), lambda i, j: (layer, 0, j)),
            pl.BlockSpec((1, COL_TILE), lambda i, j: (0, j)),
            pl.BlockSpec((NORM_CHUNK, NORM_CHUNK), lambda i, j: (0, 0)),
        ],
        out_specs=pl.BlockSpec((ROW_TILE, COL_TILE), lambda i, j: (i, j)),
        scratch_shapes=[pltpu.VMEM((ROW_TILE, d), BF16)],
        compiler_params=_params("parallel", "arbitrary"),
        name="norm_proj",
    )(x, gain.reshape(1, d), w, col_scale.reshape(1, n).astype(F32), block_diag)


def _out_proj_body(x_ref, o_ref, w_ref, y_ref):
    y_ref[...] = x_ref[...] + jnp.dot(o_ref[...], w_ref[...], preferred_element_type=F32)


def _out_proj_residual(x, o, w, layer):
    t, d = x.shape
    return pl.pallas_call(
        _out_proj_body,
        out_shape=jax.ShapeDtypeStruct((t, d), F32),
        grid=(t // ROW_TILE,),
        in_specs=[
            pl.BlockSpec((ROW_TILE, d), lambda i: (i, 0)),
            pl.BlockSpec((ROW_TILE, d), lambda i: (i, 0)),
            pl.BlockSpec((None, d, d), lambda i: (layer, 0, 0)),
        ],
        out_specs=pl.BlockSpec((ROW_TILE, d), lambda i: (i, 0)),
        compiler_params=_params("parallel"),
        name="out_proj",
    )(x, o, w)


def _swiglu_hidden(h, wg, wu):
    g = jnp.dot(h, wg, preferred_element_type=F32)
    u = jnp.dot(h, wu, preferred_element_type=F32)
    return g * (1.0 / (1.0 + jnp.exp(-g))) * u


def _ffn_body(x_ref, g_ref, wg_ref, wu_ref, wd_ref, y_ref, h_scr):
    f = pl.program_id(1)

    @pl.when(f == 0)
    def _():
        x = x_ref[...]
        h_scr[...] = _rms_normalize(x, g_ref[...]).astype(BF16)
        y_ref[...] = x

    a = _swiglu_hidden(h_scr[...], wg_ref[...], wu_ref[...]).astype(BF16)
    y_ref[...] += jnp.dot(a, wd_ref[...], preferred_element_type=F32)


def _ffn_residual(x, gain, w_gate, w_up, w_down, layer):
    t, d = x.shape
    ff = w_gate.shape[2]
    return pl.pallas_call(
        _ffn_body,
        out_shape=jax.ShapeDtypeStruct((t, d), F32),
        grid=(t // ROW_TILE, ff // FF_TILE),
        in_specs=[
            pl.BlockSpec((ROW_TILE, d), lambda i, f: (i, 0)),
            pl.BlockSpec((1, d), lambda i, f: (0, 0)),
            pl.BlockSpec((None, d, FF_TILE), lambda i, f: (layer, 0, f)),
            pl.BlockSpec((None, d, FF_TILE), lambda i, f: (layer, 0, f)),
            pl.BlockSpec((None, FF_TILE, d), lambda i, f: (layer, f, 0)),
        ],
        out_specs=pl.BlockSpec((ROW_TILE, d), lambda i, f: (i, 0)),
        scratch_shapes=[pltpu.VMEM((ROW_TILE, d), BF16)],
        compiler_params=_params("parallel", "arbitrary"),
        name="ffn",
    )(x, gain.reshape(1, d), w_gate, w_up, w_down)


def _pack_bf16_pairs(x):
    half = x.shape[1] // 2
    bits = pltpu.bitcast(x.astype(BF16).astype(F32), jnp.int32)
    return bits[:, :half] | lax.shift_right_logical(bits[:, half:], jnp.int32(16))


def _unpack_bf16_pairs(p):
    left = pltpu.bitcast(p & jnp.int32(-65536), F32)
    right = pltpu.bitcast(lax.shift_left(p, jnp.int32(16)), F32)
    return jnp.concatenate([left, right], axis=1).astype(BF16)


def _router_body(x_ref, g_ref, r_ref, tri_ref, gates_ref, rank_ref, h_ref, count_scr):
    @pl.when(pl.program_id(0) == 0)
    def _():
        count_scr[...] = jnp.zeros_like(count_scr)

    h = _rms_normalize(x_ref[...], g_ref[...])
    h_ref[...] = _pack_bf16_pairs(h)
    logits = jnp.dot(h, r_ref[...], preferred_element_type=F32, precision=lax.Precision.HIGHEST)
    lane = lax.broadcasted_iota(jnp.int32, logits.shape, 1).astype(F32)
    logits = jnp.where(lane < N_EXPERTS, logits, -jnp.inf)
    m1 = jnp.max(logits, axis=-1, keepdims=True)
    i1 = jnp.min(jnp.where(logits == m1, lane, float(LANES)), axis=-1, keepdims=True)
    rest = jnp.where(lane == i1, -jnp.inf, logits)
    m2 = jnp.max(rest, axis=-1, keepdims=True)
    i2 = jnp.min(jnp.where(rest == m2, lane, float(LANES)), axis=-1, keepdims=True)
    e = jnp.exp(m2 - m1)
    g1 = 1.0 / (1.0 + e)
    gates_ref[...] = jnp.where(lane == i1, g1, 0.0) + jnp.where(lane == i2, e * g1, 0.0)
    chosen = jnp.where((lane == i1) | (lane == i2), 1.0, 0.0)
    inclusive = jnp.dot(tri_ref[...], chosen.astype(BF16), preferred_element_type=F32)
    rank_ref[...] = jnp.where(chosen > 0.0, inclusive - 1.0 + count_scr[...], -1.0)
    count_scr[...] += inclusive[ROW_TILE - 1:ROW_TILE, :]


def _router(x, gain, router):
    t, d = x.shape
    r = jnp.zeros((d, LANES), F32).at[:, :N_EXPERTS].set(router)
    idx = jnp.arange(ROW_TILE)
    tri = (idx[:, None] >= idx[None, :]).astype(BF16)
    return pl.pallas_call(
        _router_body,
        out_shape=(jax.ShapeDtypeStruct((t, LANES), F32), jax.ShapeDtypeStruct((t, LANES), F32),
                   jax.ShapeDtypeStruct((t, d // 2), jnp.int32)),
        grid=(t // ROW_TILE,),
        in_specs=[
            pl.BlockSpec((ROW_TILE, d), lambda i: (i, 0)),
            pl.BlockSpec((1, d), lambda i: (0, 0)),
            pl.BlockSpec((d, LANES), lambda i: (0, 0)),
            pl.BlockSpec((ROW_TILE, ROW_TILE), lambda i: (0, 0)),
        ],
        out_specs=(pl.BlockSpec((ROW_TILE, LANES), lambda i: (i, 0)),
                   pl.BlockSpec((ROW_TILE, LANES), lambda i: (i, 0)),
                   pl.BlockSpec((ROW_TILE, d // 2), lambda i: (i, 0))),
        scratch_shapes=[pltpu.VMEM((1, LANES), F32)],
        compiler_params=_params("arbitrary"),
        name="router",
    )(x, gain.reshape(1, d), r, tri)


def _sc_mesh():
    return plsc.VectorSubcoreMesh(core_axis_name="core", subcore_axis_name="subcore",
                                  num_cores=SC_CORES, num_subcores=SC_SUBCORES)


def _sc_worker_base(per_worker):
    return (lax.axis_index("subcore") * SC_CORES + lax.axis_index("core")) * per_worker


def _sc_row_gather(table, idx):
    width = table.shape[1]
    n = idx.shape[0]
    per_worker = n // (SC_CORES * SC_SUBCORES)
    n_chunks = per_worker // SC_CHUNK_ROWS
    assert n == n_chunks * SC_CHUNK_ROWS * SC_CORES * SC_SUBCORES and n_chunks % 2 == 0

    @functools.partial(
        pl.kernel, mesh=_sc_mesh(), out_type=jax.ShapeDtypeStruct((n, width), table.dtype),
        scratch_types=[pltpu.VMEM((SC_CHUNK_ROWS,), jnp.int32), pltpu.VMEM((SC_CHUNK_ROWS,), jnp.int32),
                       pltpu.VMEM((SC_CHUNK_ROWS, width), table.dtype),
                       pltpu.VMEM((SC_CHUNK_ROWS, width), table.dtype),
                       pltpu.SemaphoreType.DMA, pltpu.SemaphoreType.DMA],
        name="sc_row_gather")
    def gather(table_hbm, idx_hbm, out_hbm, idx_a, idx_b, rows_a, rows_b, sem_a, sem_b):
        base = _sc_worker_base(per_worker)
        bufs = ((idx_a, rows_a, sem_a), (idx_b, rows_b, sem_b))

        def rows_of(c):
            return pl.ds(pl.multiple_of(base + c * SC_CHUNK_ROWS, SC_CHUNK_ROWS), SC_CHUNK_ROWS)

        def fetch(c, buf):
            idx_v, rows_v, sem = buf
            pltpu.sync_copy(idx_hbm.at[rows_of(c)], idx_v)
            return pltpu.make_async_copy(table_hbm.at[idx_v], rows_v, sem)

        fetch(0, bufs[0]).start()

        @pl.loop(0, n_chunks, step=2)
        def _(c):
            for b in range(2):
                idx_v, rows_v, sem = bufs[b]
                pltpu.make_async_copy(table_hbm.at[idx_v], rows_v, sem).wait()

                @pl.when(c + b + 1 < n_chunks)
                def _():
                    fetch(c + b + 1, bufs[1 - b]).start()

                pltpu.sync_copy(rows_v, out_hbm.at[rows_of(c + b)])

    return gather(table, idx)


def _sc_row_scatter_pair(rows, idx_lo, idx_hi, n_out):
    n, width = rows.shape
    per_worker = n // (SC_CORES * SC_SUBCORES)
    n_chunks = per_worker // SC_CHUNK_ROWS
    assert n == n_chunks * SC_CHUNK_ROWS * SC_CORES * SC_SUBCORES and n_chunks % 2 == 0
    index_scratch = pltpu.VMEM((SC_CHUNK_ROWS,), jnp.int32)
    rows_scratch = pltpu.VMEM((SC_CHUNK_ROWS, width), rows.dtype)

    @functools.partial(
        pl.kernel, mesh=_sc_mesh(), out_type=jax.ShapeDtypeStruct((n_out, width), rows.dtype),
        scratch_types=[index_scratch, index_scratch, rows_scratch, rows_scratch,
                       pltpu.SemaphoreType.DMA, pltpu.SemaphoreType.DMA],
        name="sc_row_scatter")
    def scatter(rows_hbm, lo_hbm, hi_hbm, out_hbm, lo_v, hi_v, rows_a, rows_b, sem_a, sem_b):
        base = _sc_worker_base(per_worker)
        bufs = ((rows_a, sem_a), (rows_b, sem_b))

        def rows_of(c):
            return pl.ds(pl.multiple_of(base + c * SC_CHUNK_ROWS, SC_CHUNK_ROWS), SC_CHUNK_ROWS)

        def load(c, buf):
            rows_v, sem = buf
            return pltpu.make_async_copy(rows_hbm.at[rows_of(c)], rows_v, sem)

        load(0, bufs[0]).start()

        @pl.loop(0, n_chunks, step=2)
        def _(c):
            for b in range(2):
                rows_v, _ = bufs[b]
                load(c + b, bufs[b]).wait()

                @pl.when(c + b + 1 < n_chunks)
                def _():
                    load(c + b + 1, bufs[1 - b]).start()

                pltpu.sync_copy(lo_hbm.at[rows_of(c + b)], lo_v)
                pltpu.sync_copy(hi_hbm.at[rows_of(c + b)], hi_v)
                pltpu.sync_copy(rows_v, out_hbm.at[lo_v])
                pltpu.sync_copy(rows_v, out_hbm.at[hi_v])

    return scatter(rows, idx_lo, idx_hi)


def _expert_ffn_body(te_ref, nu_ref, nv_ref, h_ref, wg_ref, wu_ref, wd_ref, y_ref, acc_scr):
    i = pl.program_id(0)
    f = pl.program_id(1)

    @pl.when(i < nu_ref[0])
    def _():
        row = lax.broadcasted_iota(jnp.int32, h_ref.shape, 0)
        packed = jnp.where(row < nv_ref[i], h_ref[...], 0)
        a = _swiglu_hidden(_unpack_bf16_pairs(packed), wg_ref[...], wu_ref[...]).astype(BF16)
        y = jnp.dot(a, wd_ref[...], preferred_element_type=F32)

        @pl.when(f == 0)
        def _():
            acc_scr[...] = y

        @pl.when(f > 0)
        def _():
            acc_scr[...] += y

        @pl.when(f == pl.num_programs(1) - 1)
        def _():
            y_ref[...] = _pack_bf16_pairs(acc_scr[...])


def _expert_ffn(h_sorted, tile_expert, n_used, tile_valid, w_gate, w_up, w_down, layer):
    rows, half = h_sorted.shape
    d = 2 * half
    ff = w_gate.shape[3]
    n_f = ff // MOE_FF_TILE

    def row_map(i, f, te, nu, nv):
        return (jnp.minimum(i, nu[0] - 1), 0)

    def col_step(i, f, nu):
        return jnp.where(i < nu[0], f, n_f - 1)

    grid_spec = pltpu.PrefetchScalarGridSpec(
        num_scalar_prefetch=3,
        grid=(rows // MOE_ROW_TILE, n_f),
        in_specs=[
            pl.BlockSpec((MOE_ROW_TILE, half), row_map),
            pl.BlockSpec((None, None, d, MOE_FF_TILE),
                         lambda i, f, te, nu, nv: (layer, te[i], 0, col_step(i, f, nu))),
            pl.BlockSpec((None, None, d, MOE_FF_TILE),
                         lambda i, f, te, nu, nv: (layer, te[i], 0, col_step(i, f, nu))),
            pl.BlockSpec((None, None, MOE_FF_TILE, d),
                         lambda i, f, te, nu, nv: (layer, te[i], col_step(i, f, nu), 0)),
        ],
        out_specs=pl.BlockSpec((MOE_ROW_TILE, half), row_map),
        scratch_shapes=[pltpu.VMEM((MOE_ROW_TILE, d), F32)],
    )
    return pl.pallas_call(
        _expert_ffn_body,
        out_shape=jax.ShapeDtypeStruct((rows, half), jnp.int32),
        grid_spec=grid_spec,
        compiler_params=_params("arbitrary", "arbitrary"),
        name="expert_ffn",
    )(tile_expert, n_used, tile_valid, h_sorted, w_gate, w_up, w_down)


def _combine_body(x_ref, y_ref, g_ref, o_ref):
    g = g_ref[...]
    first = _unpack_bf16_pairs(y_ref[0]).astype(F32) * g[:, 0:1]
    second = _unpack_bf16_pairs(y_ref[1]).astype(F32) * g[:, 1:2]
    o_ref[...] = x_ref[...] + first + second


def _combine_residual(x, y_pairs, gates2):
    t, d = x.shape
    return pl.pallas_call(
        _combine_body,
        out_shape=jax.ShapeDtypeStruct((t, d), F32),
        grid=(t // ROW_TILE,),
        in_specs=[
            pl.BlockSpec((ROW_TILE, d), lambda i: (i, 0)),
            pl.BlockSpec((2, ROW_TILE, d // 2), lambda i: (0, i, 0)),
            pl.BlockSpec((ROW_TILE, 2), lambda i: (i, 0)),
        ],
        out_specs=pl.BlockSpec((ROW_TILE, d), lambda i: (i, 0)),
        compiler_params=_params("parallel"),
        name="moe_combine",
    )(x, y_pairs, gates2)


def _moe_residual(x, gain, router, w_gate, w_up, w_down, layer):
    t, d = x.shape
    gates, rank, h_packed = _router(x, gain, router)
    rank8 = rank[:, :N_EXPERTS].astype(jnp.int32)
    chosen = rank8 >= 0
    counts = jnp.sum(chosen, axis=0, dtype=jnp.int32)
    padded = (counts + MOE_ROW_TILE - 1) // MOE_ROW_TILE * MOE_ROW_TILE
    ends = jnp.cumsum(padded)
    starts = ends - padded
    pos = starts[None, :] + rank8
    max_rows = 2 * t + N_EXPERTS * MOE_ROW_TILE
    pos_lo = jnp.min(jnp.where(chosen, pos, max_rows), axis=1)
    pos_hi = jnp.max(jnp.where(chosen, pos, -1), axis=1)
    gates8 = gates[:, :N_EXPERTS]
    gates2 = jnp.stack([jnp.sum(jnp.where(chosen & (pos == pos_lo[:, None]), gates8, 0.0), axis=1),
                        jnp.sum(jnp.where(chosen & (pos == pos_hi[:, None]), gates8, 0.0), axis=1)], axis=1)
    n_tiles = max_rows // MOE_ROW_TILE
    n_used = (ends[-1] // MOE_ROW_TILE).astype(jnp.int32)
    tile_start = jnp.minimum(jnp.arange(n_tiles, dtype=jnp.int32), n_used - 1) * MOE_ROW_TILE
    tile_expert = jnp.sum(tile_start[:, None] >= ends[None, :], axis=1, dtype=jnp.int32)
    tile_valid = jnp.clip((starts + counts)[tile_expert] - tile_start, 0, MOE_ROW_TILE).astype(jnp.int32)

    h_sorted = _sc_row_scatter_pair(h_packed, pos_lo, pos_hi, max_rows)
    y_sorted = _expert_ffn(h_sorted, tile_expert, n_used.reshape(1), tile_valid, w_gate, w_up, w_down, layer)
    y_pairs = _sc_row_gather(y_sorted, jnp.concatenate([pos_lo, pos_hi])).reshape(2, t, d // 2)
    return _combine_residual(x, y_pairs, gates2)


def _split_head_pair(q):
    is_first = lax.broadcasted_iota(jnp.int32, (1, LANES), 1) < HEAD_DIM
    zero = jnp.zeros_like(q)
    return jnp.where(is_first, q, zero), jnp.where(is_first, zero, q)


def _merge_head_pair(first, second):
    is_first = lax.broadcasted_iota(jnp.int32, (1, LANES), 1) < HEAD_DIM
    return jnp.where(is_first, first, second)


def _qk(q, k):
    return lax.dot_general(q, k, (((1,), (1,)), ((), ())), preferred_element_type=F32)


def _transpose_values(v_ref, vt_scr):
    for c in range(v_ref.shape[0] // ATT_BLOCK):
        rows = slice(c * ATT_BLOCK, (c + 1) * ATT_BLOCK)
        vt_scr[:, rows] = v_ref[rows, :].astype(F32).T.astype(vt_scr.dtype)


def _sb_body(q_ref, k_ref, v_ref, u_ref, o_ref, vt_scr, acc_scr, carry_scr):
    qi = pl.program_id(2)

    @pl.when(qi == 0)
    def _():
        _transpose_values(v_ref, vt_scr)

    upper = u_ref[...]
    acc_scr[...] = jnp.zeros_like(acc_scr)
    carry_scr[...] = jnp.zeros_like(carry_scr)
    key = lax.broadcasted_iota(jnp.int32, (ATT_BLOCK, ATT_BLOCK), 0)
    query = lax.broadcasted_iota(jnp.int32, (ATT_BLOCK, ATT_BLOCK), 1)
    strict = key < query
    qs = [_split_head_pair(q_ref[j * ATT_BLOCK:(j + 1) * ATT_BLOCK, :]) for j in range(ATT_K_PER_Q)]

    def process(j, kb, keep):
        start = pl.multiple_of(kb * ATT_BLOCK, ATT_BLOCK)
        k = k_ref[pl.ds(start, ATT_BLOCK), :]
        for hh in range(2):
            z = _qk(k, qs[j][hh])
            log_beta = z - _softplus(z)
            log_keep = log_beta - z
            if keep is not None:
                log_keep = jnp.where(keep, log_keep, 0.0)
            remain = jnp.dot(upper, log_keep.astype(BF16), preferred_element_type=F32)
            w = jnp.exp(log_beta + remain + carry_scr[j, hh])
            if keep is not None:
                w = jnp.where(keep, w, 0.0)
            carry_scr[j, hh] += jnp.sum(log_keep, axis=0, keepdims=True)
            vt = vt_scr[pl.ds(hh * HEAD_DIM, HEAD_DIM), pl.ds(start, ATT_BLOCK)]
            acc_scr[j, hh] += jnp.dot(vt, w.astype(BF16), preferred_element_type=F32)

    for j in range(ATT_K_PER_Q):
        g = qi * ATT_K_PER_Q + j
        process(j, g, strict)
        if j == 0:
            has_previous = jnp.broadcast_to(g > 0, strict.shape)
            process(j, jnp.maximum(g - 1, 0), has_previous)
        else:
            process(j, g - 1, None)

    for j in range(ATT_K_PER_Q):
        def more(kb, j=j):
            return (kb >= 0) & (jnp.max(carry_scr[j]) > SB_UNDERFLOW_LOG)

        def step(kb, j=j):
            process(j, kb, None)
            return kb - 1

        lax.while_loop(more, step, qi * ATT_K_PER_Q + j - 2)

    for j in range(ATT_K_PER_Q):
        out_t = jnp.concatenate([acc_scr[j, 0], acc_scr[j, 1]], axis=0)
        o_ref[j * ATT_BLOCK:(j + 1) * ATT_BLOCK, :] = out_t.T.astype(o_ref.dtype)


def _sb_attention(proj, batch, seq):
    t = proj.shape[0]
    nq = seq // ATT_Q_BLOCK
    idx = jnp.arange(ATT_BLOCK)
    upper = (idx[None, :] > idx[:, None]).astype(BF16)
    return pl.pallas_call(
        _sb_body,
        out_shape=jax.ShapeDtypeStruct((t, D_MODEL), BF16),
        grid=(batch, HEAD_PAIRS, nq),
        in_specs=[
            pl.BlockSpec((ATT_Q_BLOCK, LANES), lambda b, p, i: (b * nq + i, p)),
            pl.BlockSpec((seq, LANES), lambda b, p, i: (b, HEAD_PAIRS + p)),
            pl.BlockSpec((seq, LANES), lambda b, p, i: (b, 2 * HEAD_PAIRS + p)),
            pl.BlockSpec((ATT_BLOCK, ATT_BLOCK), lambda b, p, i: (0, 0)),
        ],
        out_specs=pl.BlockSpec((ATT_Q_BLOCK, LANES), lambda b, p, i: (b * nq + i, p)),
        scratch_shapes=[
            pltpu.VMEM((LANES, seq), BF16),
            pltpu.VMEM((ATT_K_PER_Q, 2, HEAD_DIM, ATT_BLOCK), F32),
            pltpu.VMEM((ATT_K_PER_Q, 2, 1, ATT_BLOCK), F32),
        ],
        compiler_params=_params("parallel", "parallel", "arbitrary"),
        name="sb_attention",
    )(proj, proj, proj, upper)


def _fox_gate_body(x_ref, g_ref, w_ref, b_ref, tri_ref, c_ref):
    h = _rms_normalize(x_ref[...], g_ref[...])
    logits = jnp.dot(h, w_ref[...], preferred_element_type=F32, precision=lax.Precision.HIGHEST)
    log_f = -_softplus(-(logits + b_ref[...]))
    seq = log_f.shape[0]
    carry = jnp.zeros((1, LANES), F32)
    for blk in range(seq // ATT_BLOCK):
        rows = slice(blk * ATT_BLOCK, (blk + 1) * ATT_BLOCK)
        c = jnp.dot(tri_ref[...], log_f[rows], preferred_element_type=F32,
                    precision=lax.Precision.HIGHEST) + carry
        c_ref[rows, :] = c
        carry = c[ATT_BLOCK - 1:ATT_BLOCK, :]


def _fox_cum_log_forget(x, gain, w_gate, b_gate, batch, seq):
    t, d = x.shape
    w = jnp.zeros((d, LANES), F32).at[:, :N_HEADS].set(w_gate)
    b = jnp.zeros((1, LANES), F32).at[0, :N_HEADS].set(b_gate)
    idx = jnp.arange(ATT_BLOCK)
    tri = (idx[:, None] >= idx[None, :]).astype(F32)
    return pl.pallas_call(
        _fox_gate_body,
        out_shape=jax.ShapeDtypeStruct((t, LANES), F32),
        grid=(batch,),
        in_specs=[
            pl.BlockSpec((seq, d), lambda i: (i, 0)),
            pl.BlockSpec((1, d), lambda i: (0, 0)),
            pl.BlockSpec((d, LANES), lambda i: (0, 0)),
            pl.BlockSpec((1, LANES), lambda i: (0, 0)),
            pl.BlockSpec((ATT_BLOCK, ATT_BLOCK), lambda i: (0, 0)),
        ],
        out_specs=pl.BlockSpec((seq, LANES), lambda i: (i, 0)),
        compiler_params=_params("parallel"),
        name="fox_gate",
    )(x, gain.reshape(1, d), w, b, tri)


def _fox_body(q_ref, k_ref, v_ref, cq_ref, ck_ref, o_ref, vt_scr, ckb_scr, acc_scr, m_scr, l_scr):
    qi = pl.program_id(2)

    @pl.when(qi == 0)
    def _():
        _transpose_values(v_ref, vt_scr)
        for hh in range(2):
            ckb_scr[hh] = jnp.broadcast_to(ck_ref[0, hh], ckb_scr.shape[1:])

    qs = _split_head_pair(q_ref[...])
    acc_scr[...] = jnp.zeros_like(acc_scr)
    m_scr[...] = jnp.full_like(m_scr, NEG_INF)
    l_scr[...] = jnp.zeros_like(l_scr)
    key = lax.broadcasted_iota(jnp.int32, (ATT_BLOCK, ATT_Q_BLOCK), 0)
    query = lax.broadcasted_iota(jnp.int32, (ATT_BLOCK, ATT_Q_BLOCK), 1)

    def process(kb, diagonal):
        start = pl.multiple_of(kb * ATT_BLOCK, ATT_BLOCK)
        k = k_ref[pl.ds(start, ATT_BLOCK), :]
        if diagonal:
            causal = key + (kb * ATT_BLOCK - qi * ATT_Q_BLOCK) <= query
        for hh in range(2):
            ck = ckb_scr[hh, pl.ds(start, ATT_BLOCK), :]
            a = _qk(k, qs[hh]) - jnp.concatenate([ck] * (ATT_Q_BLOCK // LANES), axis=1)
            if diagonal:
                a = jnp.where(causal, a, NEG_INF)
            cq = cq_ref[0, hh]
            m_old = m_scr[hh]
            m_new = jnp.maximum(m_old, jnp.max(a, axis=0, keepdims=True) + cq)
            alpha = jnp.exp(m_old - m_new)
            p = jnp.exp(a + (cq - m_new))
            l_scr[hh] = alpha * l_scr[hh] + jnp.sum(p, axis=0, keepdims=True)
            vt = vt_scr[pl.ds(hh * HEAD_DIM, HEAD_DIM), pl.ds(start, ATT_BLOCK)]
            acc_scr[hh] = alpha * acc_scr[hh] + jnp.dot(vt, p.astype(BF16), preferred_element_type=F32)
            m_scr[hh] = m_new

    def step(it, c):
        for r in range(ATT_K_PER_Q):
            process(it * ATT_K_PER_Q + r, False)
        return c

    lax.fori_loop(0, qi, step, 0)
    for r in range(ATT_K_PER_Q):
        process(qi * ATT_K_PER_Q + r, True)
    out_t = jnp.concatenate([acc_scr[0] / l_scr[0], acc_scr[1] / l_scr[1]], axis=0)
    o_ref[...] = out_t.T.astype(o_ref.dtype)


def _fox_attention(proj, cum, batch, seq):
    t = proj.shape[0]
    nq = seq // ATT_Q_BLOCK
    cum_h = cum[:, :N_HEADS].reshape(batch, seq, N_HEADS).transpose(0, 2, 1)
    cum_q = cum_h.reshape(batch, N_HEADS, 1, seq)
    cum_k = cum_h.reshape(batch, N_HEADS, seq, 1)
    return pl.pallas_call(
        _fox_body,
        out_shape=jax.ShapeDtypeStruct((t, D_MODEL), BF16),
        grid=(batch, HEAD_PAIRS, nq),
        in_specs=[
            pl.BlockSpec((ATT_Q_BLOCK, LANES), lambda b, p, i: (b * nq + i, p)),
            pl.BlockSpec((seq, LANES), lambda b, p, i: (b, HEAD_PAIRS + p)),
            pl.BlockSpec((seq, LANES), lambda b, p, i: (b, 2 * HEAD_PAIRS + p)),
            pl.BlockSpec((1, 2, 1, ATT_Q_BLOCK), lambda b, p, i: (b, p, 0, i)),
            pl.BlockSpec((1, 2, seq, 1), lambda b, p, i: (b, p, 0, 0)),
        ],
        out_specs=pl.BlockSpec((ATT_Q_BLOCK, LANES), lambda b, p, i: (b * nq + i, p)),
        scratch_shapes=[
            pltpu.VMEM((LANES, seq), BF16),
            pltpu.VMEM((2, seq, LANES), F32),
            pltpu.VMEM((2, HEAD_DIM, ATT_Q_BLOCK), F32),
            pltpu.VMEM((2, 1, ATT_Q_BLOCK), F32),
            pltpu.VMEM((2, 1, ATT_Q_BLOCK), F32),
        ],
        compiler_params=_params("parallel", "parallel", "arbitrary"),
        name="fox_attention",
    )(proj, proj, proj, cum_q, cum_k)


def _dil_body(q1_ref, k1_ref, q2_ref, k2_ref, q3_ref, k3_ref, v_ref, bias_ref, o_ref,
              q_scr, k_scr, v_scr, acc_scr, m_scr, l_scr):
    seq = v_ref.shape[0]
    for g, (q_ref, k_ref) in enumerate(((q1_ref, k1_ref), (q2_ref, k2_ref), (q3_ref, k3_ref))):
        q_scr[g] = q_ref[...].astype(F32)
        k_scr[g] = k_ref[...].astype(F32)
    v_scr[...] = v_ref[...].astype(F32)

    def block(g, q_start, k_start, n_keys, stride):
        q_rows = pl.ds(q_start, DIL_BLOCK, stride=stride) if stride > 1 else pl.ds(q_start, DIL_BLOCK)
        k_rows = pl.ds(k_start, n_keys, stride=stride) if stride > 1 else pl.ds(k_start, n_keys)
        qs = _split_head_pair(q_scr[g, q_rows, :].astype(BF16))
        k = k_scr[g, k_rows, :].astype(BF16)
        v = v_scr[k_rows, :].astype(BF16)
        outs, maxes, sums = [], [], []
        for hh in range(2):
            if n_keys == 2 * DIL_BLOCK:
                bias = bias_ref[g, hh]
            else:
                bias = bias_ref[g, hh, :, DIL_BLOCK:]
            logits = _qk(qs[hh], k) + bias
            m = jnp.max(logits, axis=-1, keepdims=True)
            p = jnp.exp(logits - m)
            sums.append(jnp.sum(p, axis=-1, keepdims=True))
            maxes.append(m)
            outs.append(jnp.dot(p.astype(BF16), v, preferred_element_type=F32))
        shape = (DIL_BLOCK, LANES)
        acc_scr[g, q_rows, :] = _merge_head_pair(outs[0], outs[1])
        m_scr[g, q_rows, :] = _merge_head_pair(jnp.broadcast_to(maxes[0], shape),
                                               jnp.broadcast_to(maxes[1], shape))
        l_scr[g, q_rows, :] = _merge_head_pair(jnp.broadcast_to(sums[0], shape),
                                               jnp.broadcast_to(sums[1], shape))

    block(0, 0, 0, DIL_BLOCK, 1)

    def g0_step(n, c):
        q_start = pl.multiple_of(n * DIL_BLOCK, DIL_BLOCK)
        k_start = pl.multiple_of((n - 1) * DIL_BLOCK, DIL_BLOCK)
        block(0, q_start, k_start, 2 * DIL_BLOCK, 1)
        return c

    lax.fori_loop(1, seq // DIL_BLOCK, g0_step, 0)

    for g in (1, 2):
        dil = DILATED_PAIRS[g][1]
        class_rows = DIL_BLOCK * dil
        for r in range(dil):
            block(g, r, r, DIL_BLOCK, dil)
            for n in range(1, seq // class_rows):
                block(g, r + n * class_rows, r + (n - 1) * class_rows, 2 * DIL_BLOCK, dil)

    def merge_step(c, carry):
        rows = pl.ds(pl.multiple_of(c * ATT_BLOCK, ATT_BLOCK), ATT_BLOCK)
        m0, m1, m2 = m_scr[0, rows, :], m_scr[1, rows, :], m_scr[2, rows, :]
        m = jnp.maximum(jnp.maximum(m0, m1), m2)
        w0, w1, w2 = jnp.exp(m0 - m), jnp.exp(m1 - m), jnp.exp(m2 - m)
        num = w0 * acc_scr[0, rows, :] + w1 * acc_scr[1, rows, :] + w2 * acc_scr[2, rows, :]
        den = w0 * l_scr[0, rows, :] + w1 * l_scr[1, rows, :] + w2 * l_scr[2, rows, :]
        o_ref[rows, :] = (num / den).astype(o_ref.dtype)
        return carry

    lax.fori_loop(0, seq // ATT_BLOCK, merge_step, 0)


def _t5_causal_bucket(distance):
    max_exact = N_REL_BUCKETS // 2
    d = jnp.maximum(distance, 1).astype(F32)
    log_b = max_exact + (jnp.log(d / max_exact) / math.log(REL_MAX_DISTANCE / max_exact)
                         * (N_REL_BUCKETS - max_exact)).astype(jnp.int32)
    log_b = jnp.minimum(log_b, N_REL_BUCKETS - 1)
    return jnp.where(distance < max_exact, distance, log_b)


def _dilated_bias(rel_bias):
    qi = jnp.arange(DIL_BLOCK, dtype=jnp.int32)
    kj = jnp.arange(2 * DIL_BLOCK, dtype=jnp.int32)
    delta = qi[:, None] + DIL_BLOCK - kj[None, :]
    in_band = (delta >= 0) & (delta <= DIL_SPAN)
    buckets = jnp.stack([_t5_causal_bucket(jnp.maximum(delta, 0) * dil) for _, dil in DILATED_PAIRS])
    one_hot = (buckets[..., None] == jnp.arange(N_REL_BUCKETS)).astype(F32)
    bias = jnp.einsum("gqkb,bh->ghqk", one_hot, rel_bias.astype(F32), precision=lax.Precision.HIGHEST)
    return jnp.where(in_band[None, None], bias, NEG_INF)


def _dilated_attention(proj, rel_bias, batch, seq):
    t = proj.shape[0]
    n_groups = len(DILATED_PAIRS)

    def section(s):
        return pl.BlockSpec((seq, LANES), lambda b, p: (b, s * HEAD_PAIRS + p))

    group_scratch = pltpu.VMEM((n_groups, seq, LANES), F32)
    return pl.pallas_call(
        _dil_body,
        out_shape=jax.ShapeDtypeStruct((t, D_MODEL), BF16),
        grid=(batch, HEAD_PAIRS),
        in_specs=[section(s) for s in range(2 * n_groups + 1)] + [
            pl.BlockSpec((n_groups, 2, DIL_BLOCK, 2 * DIL_BLOCK), lambda b, p: (0, p, 0, 0)),
        ],
        out_specs=pl.BlockSpec((seq, LANES), lambda b, p: (b, p)),
        scratch_shapes=[group_scratch, group_scratch, pltpu.VMEM((seq, LANES), F32),
                        group_scratch, group_scratch, group_scratch],
        compiler_params=_params("parallel", "parallel"),
        name="dilated_attention",
    )(*([proj] * (2 * n_groups + 1)), _dilated_bias(rel_bias))


def _tile_heads(v):
    return jnp.tile(v.astype(F32), N_HEADS)


def _sb_mixer(x, gain, w_qkv, w_o, layer, batch, seq):
    ones = jnp.ones((D_MODEL,), F32)
    col_scale = jnp.concatenate([ones * QK_SCALE, ones, ones])
    proj = _norm_proj(x, gain, w_qkv, layer, col_scale, 0)
    return _out_proj_residual(x, _sb_attention(proj, batch, seq), w_o, layer)


def _dilated_mixer(x, gain, w_in, q_norm, k_norm, rel_bias, w_o, layer, batch, seq):
    scales = []
    for g in range(len(DILATED_PAIRS)):
        scales += [_tile_heads(q_norm[g]) * QK_SCALE, _tile_heads(k_norm[g])]
    scales.append(jnp.ones((D_MODEL,), F32))
    proj = _norm_proj(x, gain, w_in, layer, jnp.concatenate(scales), 2 * len(DILATED_PAIRS))
    return _out_proj_residual(x, _dilated_attention(proj, rel_bias, batch, seq), w_o, layer)


def _fox_mixer(x, gain, w_qkv, w_gate, b_f, q_norm, k_norm, w_o, layer, batch, seq):
    col_scale = jnp.concatenate([_tile_heads(q_norm) * QK_SCALE, _tile_heads(k_norm),
                                 jnp.ones((D_MODEL,), F32)])
    proj = _norm_proj(x, gain, w_qkv, layer, col_scale, 2)
    cum = _fox_cum_log_forget(x, gain, w_gate, b_f, batch, seq)
    return _out_proj_residual(x, _fox_attention(proj, cum, batch, seq), w_o, layer)


def kernel(x, sb_w_qkv, sb_w_o, dil_w_in, dil_q_norm, dil_k_norm, dil_w_o, fox_w_in, fox_b_f,
           fox_q_norm, fox_k_norm, fox_w_o, rel_bias, attn_norm, ffn_norm, mlp_w_gate, mlp_w_up,
           mlp_w_down, moe_router, moe_w_gate, moe_w_up, moe_w_down):
    batch, seq, d = x.shape
    depth = attn_norm.shape[0]
    sb_w_qkv, sb_w_o, dil_w_in, dil_w_o, fox_w_o, mlp_w_gate, mlp_w_up, mlp_w_down = (
        w.astype(BF16) for w in (sb_w_qkv, sb_w_o, dil_w_in, dil_w_o, fox_w_o, mlp_w_gate, mlp_w_up,
                                 mlp_w_down))
    moe_w_gate, moe_w_up, moe_w_down = (w.astype(BF16) for w in (moe_w_gate, moe_w_up, moe_w_down))
    fox_w_qkv = fox_w_in[:, :, :3 * D_MODEL].astype(BF16)
    h = x.reshape(batch * seq, d)
    for i in range(depth):
        kind, j = i % 3, i // 3
        if kind == 0:
            h = _sb_mixer(h, attn_norm[i], sb_w_qkv, sb_w_o, j, batch, seq)
        elif kind == 1:
            h = _dilated_mixer(h, attn_norm[i], dil_w_in, dil_q_norm[j], dil_k_norm[j], rel_bias,
                               dil_w_o, j, batch, seq)
        else:
            h = _fox_mixer(h, attn_norm[i], fox_w_qkv, fox_w_in[j, :, 3 * D_MODEL:], fox_b_f[j],
                           fox_q_norm[j], fox_k_norm[j], fox_w_o, j, batch, seq)
        f = i // 2
        if i % 2 == 0:
            h = _ffn_residual(h, ffn_norm[i], mlp_w_gate, mlp_w_up, mlp_w_down, f)
        else:
            h = _moe_residual(h, ffn_norm[i], moe_router[f], moe_w_gate, moe_w_up, moe_w_down, f)
    return h.reshape(batch, seq, d)
```

```python
import functools
import math

import jax
import jax.numpy as jnp
from jax import lax
from jax.experimental import pallas as pl
from jax.experimental.pallas import tpu as pltpu
from jax.experimental.pallas import tpu_sc as plsc

D_MODEL = 1024
N_HEADS = 16
HEAD_DIM = 64
LANES = 128
HEAD_PAIRS = D_MODEL // LANES
D_FF = 3584
N_EXPERTS = 8
N_REL_BUCKETS = 32
REL_MAX_DISTANCE = 2048
DILATED_PAIRS = ((128, 1), (512, 4), (2048, 16))
DIL_SPAN = 128
RMS_EPS = 1e-6
NEG_INF = -1e30
SB_UNDERFLOW_LOG = -104.0
QK_SCALE = 1.0 / math.sqrt(HEAD_DIM)

ROW_TILE = 1024
COL_TILE = 1024
FF_TILE = 512
MOE_ROW_TILE = 512
MOE_FF_TILE = 1792
SC_CORES = 2
SC_SUBCORES = 16
SC_CHUNK_ROWS = 64
NORM_CHUNK = 256
ATT_BLOCK = 256
ATT_Q_BLOCK = 512
ATT_K_PER_Q = ATT_Q_BLOCK // ATT_BLOCK
DIL_BLOCK = 128
VMEM_LIMIT = 56 * 1024 * 1024

F32 = jnp.float32
BF16 = jnp.bfloat16


def _params(*semantics):
    return pltpu.CompilerParams(dimension_semantics=semantics, vmem_limit_bytes=VMEM_LIMIT)


def _rms_normalize(x, gain):
    inv = lax.rsqrt(jnp.mean(x * x, axis=-1, keepdims=True) + RMS_EPS)
    return x * inv * gain


def _softplus(z):
    return jnp.maximum(z, 0.0) + jnp.log(1.0 + jnp.exp(-jnp.abs(z)))


def _norm_proj_body(x_ref, g_ref, w_ref, cs_ref, bd_ref, o_ref, h_scr, *, n_norm):
    j = pl.program_id(1)

    @pl.when(j == 0)
    def _():
        h_scr[...] = _rms_normalize(x_ref[...], g_ref[...]).astype(BF16)

    acc = jnp.dot(h_scr[...], w_ref[...], preferred_element_type=F32)

    def plain():
        o_ref[...] = (acc * cs_ref[...]).astype(o_ref.dtype)

    def head_normed():
        for c in range(COL_TILE // NORM_CHUNK):
            sl = slice(c * NORM_CHUNK, (c + 1) * NORM_CHUNK)
            a = acc[:, sl]
            ss = jnp.dot((a * a).astype(BF16), bd_ref[...], preferred_element_type=F32)
            inv = lax.rsqrt(ss * (1.0 / HEAD_DIM) + RMS_EPS)
            o_ref[:, sl] = (a * inv * cs_ref[:, sl]).astype(o_ref.dtype)

    if n_norm == 0:
        plain()
    else:
        pl.when(j < n_norm)(head_normed)
        pl.when(j >= n_norm)(plain)


def _norm_proj(x, gain, w, layer, col_scale, n_norm):
    t, d = x.shape
    n = w.shape[2]
    head_id = jnp.arange(NORM_CHUNK) // HEAD_DIM
    block_diag = (head_id[:, None] == head_id[None, :]).astype(BF16)
    return pl.pallas_call(
        functools.partial(_norm_proj_body, n_norm=n_norm),
        out_shape=jax.ShapeDtypeStruct((t, n), BF16),
        grid=(t // ROW_TILE, n // COL_TILE),
        in_specs=[
            pl.BlockSpec((ROW_TILE, d), lambda i, j: (i, 0)),
            pl.BlockSpec((1, d), lambda i, j: (0, 0)),
            pl.BlockSpec((None, d, COL_TILE), lambda i, j: (layer, 0, j)),
            pl.BlockSpec((1, COL_TILE), lambda i, j: (0, j)),
            pl.BlockSpec((NORM_CHUNK, NORM_CHUNK), lambda i, j: (0, 0)),
        ],
        out_specs=pl.BlockSpec((ROW_TILE, COL_TILE), lambda i, j: (i, j)),
        scratch_shapes=[pltpu.VMEM((ROW_TILE, d), BF16)],
        compiler_params=_params("parallel", "arbitrary"),
        name="norm_proj",
    )(x, gain.reshape(1, d), w, col_scale.reshape(1, n).astype(F32), block_diag)


def _out_proj_body(x_ref, o_ref, w_ref, y_ref):
    y_ref[...] = x_ref[...] + jnp.dot(o_ref[...], w_ref[...], preferred_element_type=F32)


def _out_proj_residual(x, o, w, layer):
    t, d = x.shape
    return pl.pallas_call(
        _out_proj_body,
        out_shape=jax.ShapeDtypeStruct((t, d), F32),
        grid=(t // ROW_TILE,),
        in_specs=[
            pl.BlockSpec((ROW_TILE, d), lambda i: (i, 0)),
            pl.BlockSpec((ROW_TILE, d), lambda i: (i, 0)),
            pl.BlockSpec((None, d, d), lambda i: (layer, 0, 0)),
        ],
        out_specs=pl.BlockSpec((ROW_TILE, d), lambda i: (i, 0)),
        compiler_params=_params("parallel"),
        name="out_proj",
    )(x, o, w)


def _swiglu_hidden(h, wg, wu):
    g = jnp.dot(h, wg, preferred_element_type=F32)
    u = jnp.dot(h, wu, preferred_element_type=F32)
    return g * (1.0 / (1.0 + jnp.exp(-g))) * u


def _ffn_body(x_ref, g_ref, wg_ref, wu_ref, wd_ref, y_ref, h_scr):
    f = pl.program_id(1)

    @pl.when(f == 0)
    def _():
        x = x_ref[...]
        h_scr[...] = _rms_normalize(x, g_ref[...]).astype(BF16)
        y_ref[...] = x

    a = _swiglu_hidden(h_scr[...], wg_ref[...], wu_ref[...]).astype(BF16)
    y_ref[...] += jnp.dot(a, wd_ref[...], preferred_element_type=F32)


def _ffn_residual(x, gain, w_gate, w_up, w_down, layer):
    t, d = x.shape
    ff = w_gate.shape[2]
    return pl.pallas_call(
        _ffn_body,
        out_shape=jax.ShapeDtypeStruct((t, d), F32),
        grid=(t // ROW_TILE, ff // FF_TILE),
        in_specs=[
            pl.BlockSpec((ROW_TILE, d), lambda i, f: (i, 0)),
            pl.BlockSpec((1, d), lambda i, f: (0, 0)),
            pl.BlockSpec((None, d, FF_TILE), lambda i, f: (layer, 0, f)),
            pl.BlockSpec((None, d, FF_TILE), lambda i, f: (layer, 0, f)),
            pl.BlockSpec((None, FF_TILE, d), lambda i, f: (layer, f, 0)),
        ],
        out_specs=pl.BlockSpec((ROW_TILE, d), lambda i, f: (i, 0)),
        scratch_shapes=[pltpu.VMEM((ROW_TILE, d), BF16)],
        compiler_params=_params("parallel", "arbitrary"),
        name="ffn",
    )(x, gain.reshape(1, d), w_gate, w_up, w_down)


def _pack_bf16_pairs(x):
    half = x.shape[1] // 2
    bits = pltpu.bitcast(x.astype(BF16).astype(F32), jnp.int32)
    return bits[:, :half] | lax.shift_right_logical(bits[:, half:], jnp.int32(16))


def _unpack_bf16_pairs(p):
    left = pltpu.bitcast(p & jnp.int32(-65536), F32)
    right = pltpu.bitcast(lax.shift_left(p, jnp.int32(16)), F32)
    return jnp.concatenate([left, right], axis=1).astype(BF16)


def _router_body(x_ref, g_ref, r_ref, tri_ref, gates_ref, rank_ref, h_ref, count_scr):
    @pl.when(pl.program_id(0) == 0)
    def _():
        count_scr[...] = jnp.zeros_like(count_scr)

    h = _rms_normalize(x_ref[...], g_ref[...])
    h_ref[...] = _pack_bf16_pairs(h)
    logits = jnp.dot(h, r_ref[...], preferred_element_type=F32, precision=lax.Precision.HIGHEST)
    lane = lax.broadcasted_iota(jnp.int32, logits.shape, 1).astype(F32)
    logits = jnp.where(lane < N_EXPERTS, logits, -jnp.inf)
    m1 = jnp.max(logits, axis=-1, keepdims=True)
    i1 = jnp.min(jnp.where(logits == m1, lane, float(LANES)), axis=-1, keepdims=True)
    rest = jnp.where(lane == i1, -jnp.inf, logits)
    m2 = jnp.max(rest, axis=-1, keepdims=True)
    i2 = jnp.min(jnp.where(rest == m2, lane, float(LANES)), axis=-1, keepdims=True)
    e = jnp.exp(m2 - m1)
    g1 = 1.0 / (1.0 + e)
    gates_ref[...] = jnp.where(lane == i1, g1, 0.0) + jnp.where(lane == i2, e * g1, 0.0)
    chosen = jnp.where((lane == i1) | (lane == i2), 1.0, 0.0)
    inclusive = jnp.dot(tri_ref[...], chosen.astype(BF16), preferred_element_type=F32)
    rank_ref[...] = jnp.where(chosen > 0.0, inclusive - 1.0 + count_scr[...], -1.0)
    count_scr[...] += inclusive[ROW_TILE - 1:ROW_TILE, :]


def _router(x, gain, router):
    t, d = x.shape
    r = jnp.zeros((d, LANES), F32).at[:, :N_EXPERTS].set(router)
    idx = jnp.arange(ROW_TILE)
    tri = (idx[:, None] >= idx[None, :]).astype(BF16)
    return pl.pallas_call(
        _router_body,
        out_shape=(jax.ShapeDtypeStruct((t, LANES), F32), jax.ShapeDtypeStruct((t, LANES), F32),
                   jax.ShapeDtypeStruct((t, d // 2), jnp.int32)),
        grid=(t // ROW_TILE,),
        in_specs=[
            pl.BlockSpec((ROW_TILE, d), lambda i: (i, 0)),
            pl.BlockSpec((1, d), lambda i: (0, 0)),
            pl.BlockSpec((d, LANES), lambda i: (0, 0)),
            pl.BlockSpec((ROW_TILE, ROW_TILE), lambda i: (0, 0)),
        ],
        out_specs=(pl.BlockSpec((ROW_TILE, LANES), lambda i: (i, 0)),
                   pl.BlockSpec((ROW_TILE, LANES), lambda i: (i, 0)),
                   pl.BlockSpec((ROW_TILE, d // 2), lambda i: (i, 0))),
        scratch_shapes=[pltpu.VMEM((1, LANES), F32)],
        compiler_params=_params("arbitrary"),
        name="router",
    )(x, gain.reshape(1, d), r, tri)


def _sc_mesh():
    return plsc.VectorSubcoreMesh(core_axis_name="core", subcore_axis_name="subcore",
                                  num_cores=SC_CORES, num_subcores=SC_SUBCORES)


def _sc_worker_base(per_worker):
    return (lax.axis_index("subcore") * SC_CORES + lax.axis_index("core")) * per_worker


def _sc_row_gather(table, idx):
    width = table.shape[1]
    n = idx.shape[0]
    per_worker = n // (SC_CORES * SC_SUBCORES)
    n_chunks = per_worker // SC_CHUNK_ROWS
    assert n == n_chunks * SC_CHUNK_ROWS * SC_CORES * SC_SUBCORES and n_chunks % 2 == 0

    @functools.partial(
        pl.kernel, mesh=_sc_mesh(), out_type=jax.ShapeDtypeStruct((n, width), table.dtype),
        scratch_types=[pltpu.VMEM((SC_CHUNK_ROWS,), jnp.int32), pltpu.VMEM((SC_CHUNK_ROWS,), jnp.int32),
                       pltpu.VMEM((SC_CHUNK_ROWS, width), table.dtype),
                       pltpu.VMEM((SC_CHUNK_ROWS, width), table.dtype),
                       pltpu.SemaphoreType.DMA, pltpu.SemaphoreType.DMA],
        name="sc_row_gather")
    def gather(table_hbm, idx_hbm, out_hbm, idx_a, idx_b, rows_a, rows_b, sem_a, sem_b):
        base = _sc_worker_base(per_worker)
        bufs = ((idx_a, rows_a, sem_a), (idx_b, rows_b, sem_b))

        def rows_of(c):
            return pl.ds(pl.multiple_of(base + c * SC_CHUNK_ROWS, SC_CHUNK_ROWS), SC_CHUNK_ROWS)

        def fetch(c, buf):
            idx_v, rows_v, sem = buf
            pltpu.sync_copy(idx_hbm.at[rows_of(c)], idx_v)
            return pltpu.make_async_copy(table_hbm.at[idx_v], rows_v, sem)

        fetch(0, bufs[0]).start()

        @pl.loop(0, n_chunks, step=2)
        def _(c):
            for b in range(2):
                idx_v, rows_v, sem = bufs[b]
                pltpu.make_async_copy(table_hbm.at[idx_v], rows_v, sem).wait()

                @pl.when(c + b + 1 < n_chunks)
                def _():
                    fetch(c + b + 1, bufs[1 - b]).start()

                pltpu.sync_copy(rows_v, out_hbm.at[rows_of(c + b)])

    return gather(table, idx)


def _sc_row_scatter_pair(rows, idx_lo, idx_hi, n_out):
    n, width = rows.shape
    per_worker = n // (SC_CORES * SC_SUBCORES)
    n_chunks = per_worker // SC_CHUNK_ROWS
    assert n == n_chunks * SC_CHUNK_ROWS * SC_CORES * SC_SUBCORES and n_chunks % 2 == 0
    index_scratch = pltpu.VMEM((SC_CHUNK_ROWS,), jnp.int32)
    rows_scratch = pltpu.VMEM((SC_CHUNK_ROWS, width), rows.dtype)

    @functools.partial(
        pl.kernel, mesh=_sc_mesh(), out_type=jax.ShapeDtypeStruct((n_out, width), rows.dtype),
        scratch_types=[index_scratch, index_scratch, rows_scratch, rows_scratch,
                       pltpu.SemaphoreType.DMA, pltpu.SemaphoreType.DMA],
        name="sc_row_scatter")
    def scatter(rows_hbm, lo_hbm, hi_hbm, out_hbm, lo_v, hi_v, rows_a, rows_b, sem_a, sem_b):
        base = _sc_worker_base(per_worker)
        bufs = ((rows_a, sem_a), (rows_b, sem_b))

        def rows_of(c):
            return pl.ds(pl.multiple_of(base + c * SC_CHUNK_ROWS, SC_CHUNK_ROWS), SC_CHUNK_ROWS)

        def load(c, buf):
            rows_v, sem = buf
            return pltpu.make_async_copy(rows_hbm.at[rows_of(c)], rows_v, sem)

        load(0, bufs[0]).start()

        @pl.loop(0, n_chunks, step=2)
        def _(c):
            for b in range(2):
                rows_v, _ = bufs[b]
                load(c + b, bufs[b]).wait()

                @pl.when(c + b + 1 < n_chunks)
                def _():
                    load(c + b + 1, bufs[1 - b]).start()

                pltpu.sync_copy(lo_hbm.at[rows_of(c + b)], lo_v)
                pltpu.sync_copy(hi_hbm.at[rows_of(c + b)], hi_v)
                pltpu.sync_copy(rows_v, out_hbm.at[lo_v])
                pltpu.sync_copy(rows_v, out_hbm.at[hi_v])

    return scatter(rows, idx_lo, idx_hi)


def _expert_ffn_body(te_ref, nu_ref, nv_ref, h_ref, wg_ref, wu_ref, wd_ref, y_ref, acc_scr):
    i = pl.program_id(0)
    f = pl.program_id(1)

    @pl.when(i < nu_ref[0])
    def _():
        row = lax.broadcasted_iota(jnp.int32, h_ref.shape, 0)
        packed = jnp.where(row < nv_ref[i], h_ref[...], 0)
        a = _swiglu_hidden(_unpack_bf16_pairs(packed), wg_ref[...], wu_ref[...]).astype(BF16)
        y = jnp.dot(a, wd_ref[...], preferred_element_type=F32)

        @pl.when(f == 0)
        def _():
            acc_scr[...] = y

        @pl.when(f > 0)
        def _():
            acc_scr[...] += y

        @pl.when(f == pl.num_programs(1) - 1)
        def _():
            y_ref[...] = _pack_bf16_pairs(acc_scr[...])


def _expert_ffn(h_sorted, tile_expert, n_used, tile_valid, w_gate, w_up, w_down, layer):
    rows, half = h_sorted.shape
    d = 2 * half
    ff = w_gate.shape[3]
    n_f = ff // MOE_FF_TILE

    def row_map(i, f, te, nu, nv):
        return (jnp.minimum(i, nu[0] - 1), 0)

    def col_step(i, f, nu):
        return jnp.where(i < nu[0], f, n_f - 1)

    grid_spec = pltpu.PrefetchScalarGridSpec(
        num_scalar_prefetch=3,
        grid=(rows // MOE_ROW_TILE, n_f),
        in_specs=[
            pl.BlockSpec((MOE_ROW_TILE, half), row_map),
            pl.BlockSpec((None, None, d, MOE_FF_TILE),
                         lambda i, f, te, nu, nv: (layer, te[i], 0, col_step(i, f, nu))),
            pl.BlockSpec((None, None, d, MOE_FF_TILE),
                         lambda i, f, te, nu, nv: (layer, te[i], 0, col_step(i, f, nu))),
            pl.BlockSpec((None, None, MOE_FF_TILE, d),
                         lambda i, f, te, nu, nv: (layer, te[i], col_step(i, f, nu), 0)),
        ],
        out_specs=pl.BlockSpec((MOE_ROW_TILE, half), row_map),
        scratch_shapes=[pltpu.VMEM((MOE_ROW_TILE, d), F32)],
    )
    return pl.pallas_call(
        _expert_ffn_body,
        out_shape=jax.ShapeDtypeStruct((rows, half), jnp.int32),
        grid_spec=grid_spec,
        compiler_params=_params("arbitrary", "arbitrary"),
        name="expert_ffn",
    )(tile_expert, n_used, tile_valid, h_sorted, w_gate, w_up, w_down)


def _combine_body(x_ref, y_ref, g_ref, o_ref):
    g = g_ref[...]
    first = _unpack_bf16_pairs(y_ref[0]).astype(F32) * g[:, 0:1]
    second = _unpack_bf16_pairs(y_ref[1]).astype(F32) * g[:, 1:2]
    o_ref[...] = x_ref[...] + first + second


def _combine_residual(x, y_pairs, gates2):
    t, d = x.shape
    return pl.pallas_call(
        _combine_body,
        out_shape=jax.ShapeDtypeStruct((t, d), F32),
        grid=(t // ROW_TILE,),
        in_specs=[
            pl.BlockSpec((ROW_TILE, d), lambda i: (i, 0)),
            pl.BlockSpec((2, ROW_TILE, d // 2), lambda i: (0, i, 0)),
            pl.BlockSpec((ROW_TILE, 2), lambda i: (i, 0)),
        ],
        out_specs=pl.BlockSpec((ROW_TILE, d), lambda i: (i, 0)),
        compiler_params=_params("parallel"),
        name="moe_combine",
    )(x, y_pairs, gates2)


def _moe_residual(x, gain, router, w_gate, w_up, w_down, layer):
    t, d = x.shape
    gates, rank, h_packed = _router(x, gain, router)
    rank8 = rank[:, :N_EXPERTS].astype(jnp.int32)
    chosen = rank8 >= 0
    counts = jnp.sum(chosen, axis=0, dtype=jnp.int32)
    padded = (counts + MOE_ROW_TILE - 1) // MOE_ROW_TILE * MOE_ROW_TILE
    ends = jnp.cumsum(padded)
    starts = ends - padded
    pos = starts[None, :] + rank8
    max_rows = 2 * t + N_EXPERTS * MOE_ROW_TILE
    pos_lo = jnp.min(jnp.where(chosen, pos, max_rows), axis=1)
    pos_hi = jnp.max(jnp.where(chosen, pos, -1), axis=1)
    gates8 = gates[:, :N_EXPERTS]
    gates2 = jnp.stack([jnp.sum(jnp.where(chosen & (pos == pos_lo[:, None]), gates8, 0.0), axis=1),
                        jnp.sum(jnp.where(chosen & (pos == pos_hi[:, None]), gates8, 0.0), axis=1)], axis=1)
    n_tiles = max_rows // MOE_ROW_TILE
    n_used = (ends[-1] // MOE_ROW_TILE).astype(jnp.int32)
    tile_start = jnp.minimum(jnp.arange(n_tiles, dtype=jnp.int32), n_used - 1) * MOE_ROW_TILE
    tile_expert = jnp.sum(tile_start[:, None] >= ends[None, :], axis=1, dtype=jnp.int32)
    tile_valid = jnp.clip((starts + counts)[tile_expert] - tile_start, 0, MOE_ROW_TILE).astype(jnp.int32)

    h_sorted = _sc_row_scatter_pair(h_packed, pos_lo, pos_hi, max_rows)
    y_sorted = _expert_ffn(h_sorted, tile_expert, n_used.reshape(1), tile_valid, w_gate, w_up, w_down, layer)
    y_pairs = _sc_row_gather(y_sorted, jnp.concatenate([pos_lo, pos_hi])).reshape(2, t, d // 2)
    return _combine_residual(x, y_pairs, gates2)


def _split_head_pair(q):
    is_first = lax.broadcasted_iota(jnp.int32, (1, LANES), 1) < HEAD_DIM
    zero = jnp.zeros_like(q)
    return jnp.where(is_first, q, zero), jnp.where(is_first, zero, q)


def _merge_head_pair(first, second):
    is_first = lax.broadcasted_iota(jnp.int32, (1, LANES), 1) < HEAD_DIM
    return jnp.where(is_first, first, second)


def _qk(q, k):
    return lax.dot_general(q, k, (((1,), (1,)), ((), ())), preferred_element_type=F32)


def _transpose_values(v_ref, vt_scr):
    for c in range(v_ref.shape[0] // ATT_BLOCK):
        rows = slice(c * ATT_BLOCK, (c + 1) * ATT_BLOCK)
        vt_scr[:, rows] = v_ref[rows, :].astype(F32).T.astype(vt_scr.dtype)


def _sb_body(q_ref, k_ref, v_ref, u_ref, o_ref, vt_scr, acc_scr, carry_scr):
    qi = pl.program_id(2)

    @pl.when(qi == 0)
    def _():
        _transpose_values(v_ref, vt_scr)

    upper = u_ref[...]
    acc_scr[...] = jnp.zeros_like(acc_scr)
    carry_scr[...] = jnp.zeros_like(carry_scr)
    key = lax.broadcasted_iota(jnp.int32, (ATT_BLOCK, ATT_BLOCK), 0)
    query = lax.broadcasted_iota(jnp.int32, (ATT_BLOCK, ATT_BLOCK), 1)
    strict = key < query
    qs = [_split_head_pair(q_ref[j * ATT_BLOCK:(j + 1) * ATT_BLOCK, :]) for j in range(ATT_K_PER_Q)]

    def process(j, kb, keep):
        start = pl.multiple_of(kb * ATT_BLOCK, ATT_BLOCK)
        k = k_ref[pl.ds(start, ATT_BLOCK), :]
        for hh in range(2):
            z = _qk(k, qs[j][hh])
            log_beta = z - _softplus(z)
            log_keep = log_beta - z
            if keep is not None:
                log_keep = jnp.where(keep, log_keep, 0.0)
            remain = jnp.dot(upper, log_keep.astype(BF16), preferred_element_type=F32)
            w = jnp.exp(log_beta + remain + carry_scr[j, hh])
            if keep is not None:
                w = jnp.where(keep, w, 0.0)
            carry_scr[j, hh] += jnp.sum(log_keep, axis=0, keepdims=True)
            vt = vt_scr[pl.ds(hh * HEAD_DIM, HEAD_DIM), pl.ds(start, ATT_BLOCK)]
            acc_scr[j, hh] += jnp.dot(vt, w.astype(BF16), preferred_element_type=F32)

    for j in range(ATT_K_PER_Q):
        g = qi * ATT_K_PER_Q + j
        process(j, g, strict)
        if j == 0:
            has_previous = jnp.broadcast_to(g > 0, strict.shape)
            process(j, jnp.maximum(g - 1, 0), has_previous)
        else:
            process(j, g - 1, None)

    for j in range(ATT_K_PER_Q):
        def more(kb, j=j):
            return (kb >= 0) & (jnp.max(carry_scr[j]) > SB_UNDERFLOW_LOG)

        def step(kb, j=j):
            process(j, kb, None)
            return kb - 1

        lax.while_loop(more, step, qi * ATT_K_PER_Q + j - 2)

    for j in range(ATT_K_PER_Q):
        out_t = jnp.concatenate([acc_scr[j, 0], acc_scr[j, 1]], axis=0)
        o_ref[j * ATT_BLOCK:(j + 1) * ATT_BLOCK, :] = out_t.T.astype(o_ref.dtype)


def _sb_attention(proj, batch, seq):
    t = proj.shape[0]
    nq = seq // ATT_Q_BLOCK
    idx = jnp.arange(ATT_BLOCK)
    upper = (idx[None, :] > idx[:, None]).astype(BF16)
    return pl.pallas_call(
        _sb_body,
        out_shape=jax.ShapeDtypeStruct((t, D_MODEL), BF16),
        grid=(batch, HEAD_PAIRS, nq),
        in_specs=[
            pl.BlockSpec((ATT_Q_BLOCK, LANES), lambda b, p, i: (b * nq + i, p)),
            pl.BlockSpec((seq, LANES), lambda b, p, i: (b, HEAD_PAIRS + p)),
            pl.BlockSpec((seq, LANES), lambda b, p, i: (b, 2 * HEAD_PAIRS + p)),
            pl.BlockSpec((ATT_BLOCK, ATT_BLOCK), lambda b, p, i: (0, 0)),
        ],
        out_specs=pl.BlockSpec((ATT_Q_BLOCK, LANES), lambda b, p, i: (b * nq + i, p)),
        scratch_shapes=[
            pltpu.VMEM((LANES, seq), BF16),
            pltpu.VMEM((ATT_K_PER_Q, 2, HEAD_DIM, ATT_BLOCK), F32),
            pltpu.VMEM((ATT_K_PER_Q, 2, 1, ATT_BLOCK), F32),
        ],
        compiler_params=_params("parallel", "parallel", "arbitrary"),
        name="sb_attention",
    )(proj, proj, proj, upper)


def _fox_gate_body(x_ref, g_ref, w_ref, b_ref, tri_ref, c_ref):
    h = _rms_normalize(x_ref[...], g_ref[...])
    logits = jnp.dot(h, w_ref[...], preferred_element_type=F32, precision=lax.Precision.HIGHEST)
    log_f = -_softplus(-(logits + b_ref[...]))
    seq = log_f.shape[0]
    carry = jnp.zeros((1, LANES), F32)
    for blk in range(seq // ATT_BLOCK):
        rows = slice(blk * ATT_BLOCK, (blk + 1) * ATT_BLOCK)
        c = jnp.dot(tri_ref[...], log_f[rows], preferred_element_type=F32,
                    precision=lax.Precision.HIGHEST) + carry
        c_ref[rows, :] = c
        carry = c[ATT_BLOCK - 1:ATT_BLOCK, :]


def _fox_cum_log_forget(x, gain, w_gate, b_gate, batch, seq):
    t, d = x.shape
    w = jnp.zeros((d, LANES), F32).at[:, :N_HEADS].set(w_gate)
    b = jnp.zeros((1, LANES), F32).at[0, :N_HEADS].set(b_gate)
    idx = jnp.arange(ATT_BLOCK)
    tri = (idx[:, None] >= idx[None, :]).astype(F32)
    return pl.pallas_call(
        _fox_gate_body,
        out_shape=jax.ShapeDtypeStruct((t, LANES), F32),
        grid=(batch,),
        in_specs=[
            pl.BlockSpec((seq, d), lambda i: (i, 0)),
            pl.BlockSpec((1, d), lambda i: (0, 0)),
            pl.BlockSpec((d, LANES), lambda i: (0, 0)),
            pl.BlockSpec((1, LANES), lambda i: (0, 0)),
            pl.BlockSpec((ATT_BLOCK, ATT_BLOCK), lambda i: (0, 0)),
        ],
        out_specs=pl.BlockSpec((seq, LANES), lambda i: (i, 0)),
        compiler_params=_params("parallel"),
        name="fox_gate",
    )(x, gain.reshape(1, d), w, b, tri)


def _fox_body(q_ref, k_ref, v_ref, cq_ref, ck_ref, o_ref, vt_scr, ckb_scr, acc_scr, m_scr, l_scr):
    qi = pl.program_id(2)

    @pl.when(qi == 0)
    def _():
        _transpose_values(v_ref, vt_scr)
        for hh in range(2):
            ckb_scr[hh] = jnp.broadcast_to(ck_ref[0, hh], ckb_scr.shape[1:])

    qs = _split_head_pair(q_ref[...])
    acc_scr[...] = jnp.zeros_like(acc_scr)
    m_scr[...] = jnp.full_like(m_scr, NEG_INF)
    l_scr[...] = jnp.zeros_like(l_scr)
    key = lax.broadcasted_iota(jnp.int32, (ATT_BLOCK, ATT_Q_BLOCK), 0)
    query = lax.broadcasted_iota(jnp.int32, (ATT_BLOCK, ATT_Q_BLOCK), 1)

    def process(kb, diagonal):
        start = pl.multiple_of(kb * ATT_BLOCK, ATT_BLOCK)
        k = k_ref[pl.ds(start, ATT_BLOCK), :]
        if diagonal:
            causal = key + (kb * ATT_BLOCK - qi * ATT_Q_BLOCK) <= query
        for hh in range(2):
            ck = ckb_scr[hh, pl.ds(start, ATT_BLOCK), :]
            a = _qk(k, qs[hh]) - jnp.concatenate([ck] * (ATT_Q_BLOCK // LANES), axis=1)
            if diagonal:
                a = jnp.where(causal, a, NEG_INF)
            cq = cq_ref[0, hh]
            m_old = m_scr[hh]
            m_new = jnp.maximum(m_old, jnp.max(a, axis=0, keepdims=True) + cq)
            alpha = jnp.exp(m_old - m_new)
            p = jnp.exp(a + (cq - m_new))
            l_scr[hh] = alpha * l_scr[hh] + jnp.sum(p, axis=0, keepdims=True)
            vt = vt_scr[pl.ds(hh * HEAD_DIM, HEAD_DIM), pl.ds(start, ATT_BLOCK)]
            acc_scr[hh] = alpha * acc_scr[hh] + jnp.dot(vt, p.astype(BF16), preferred_element_type=F32)
            m_scr[hh] = m_new

    def step(it, c):
        for r in range(ATT_K_PER_Q):
            process(it * ATT_K_PER_Q + r, False)
        return c

    lax.fori_loop(0, qi, step, 0)
    for r in range(ATT_K_PER_Q):
        process(qi * ATT_K_PER_Q + r, True)
    out_t = jnp.concatenate([acc_scr[0] / l_scr[0], acc_scr[1] / l_scr[1]], axis=0)
    o_ref[...] = out_t.T.astype(o_ref.dtype)


def _fox_attention(proj, cum, batch, seq):
    t = proj.shape[0]
    nq = seq // ATT_Q_BLOCK
    cum_h = cum[:, :N_HEADS].reshape(batch, seq, N_HEADS).transpose(0, 2, 1)
    cum_q = cum_h.reshape(batch, N_HEADS, 1, seq)
    cum_k = cum_h.reshape(batch, N_HEADS, seq, 1)
    return pl.pallas_call(
        _fox_body,
        out_shape=jax.ShapeDtypeStruct((t, D_MODEL), BF16),
        grid=(batch, HEAD_PAIRS, nq),
        in_specs=[
            pl.BlockSpec((ATT_Q_BLOCK, LANES), lambda b, p, i: (b * nq + i, p)),
            pl.BlockSpec((seq, LANES), lambda b, p, i: (b, HEAD_PAIRS + p)),
            pl.BlockSpec((seq, LANES), lambda b, p, i: (b, 2 * HEAD_PAIRS + p)),
            pl.BlockSpec((1, 2, 1, ATT_Q_BLOCK), lambda b, p, i: (b, p, 0, i)),
            pl.BlockSpec((1, 2, seq, 1), lambda b, p, i: (b, p, 0, 0)),
        ],
        out_specs=pl.BlockSpec((ATT_Q_BLOCK, LANES), lambda b, p, i: (b * nq + i, p)),
        scratch_shapes=[
            pltpu.VMEM((LANES, seq), BF16),
            pltpu.VMEM((2, seq, LANES), F32),
            pltpu.VMEM((2, HEAD_DIM, ATT_Q_BLOCK), F32),
            pltpu.VMEM((2, 1, ATT_Q_BLOCK), F32),
            pltpu.VMEM((2, 1, ATT_Q_BLOCK), F32),
        ],
        compiler_params=_params("parallel", "parallel", "arbitrary"),
        name="fox_attention",
    )(proj, proj, proj, cum_q, cum_k)


def _band_body(q_ref, k_ref, v_ref, bias_ref, o_ref, s_ref, vt_scr, *, sub_len):
    seq = v_ref.shape[0]
    _transpose_values(v_ref, vt_scr)
    for n in range(seq // DIL_BLOCK):
        first = (n * DIL_BLOCK) % sub_len == 0
        q_rows = slice(n * DIL_BLOCK, (n + 1) * DIL_BLOCK)
        k_rows = q_rows if first else slice((n - 1) * DIL_BLOCK, (n + 1) * DIL_BLOCK)
        qs = _split_head_pair(q_ref[q_rows, :])
        k = k_ref[k_rows, :]
        outs = []
        for hh in range(2):
            bias = bias_ref[hh, DIL_BLOCK:, :] if first else bias_ref[hh]
            logits = _qk(k, qs[hh]) + bias
            m = jnp.max(logits, axis=0, keepdims=True)
            p = jnp.exp(logits - m)
            l = jnp.sum(p, axis=0, keepdims=True)
            vt = vt_scr[hh * HEAD_DIM:(hh + 1) * HEAD_DIM, k_rows]
            outs.append(jnp.dot(vt, p.astype(BF16), preferred_element_type=F32) / l)
            s_ref[0, 0, hh:hh + 1, q_rows] = m + jnp.log(l)
        o_ref[q_rows, :] = jnp.concatenate(outs, axis=0).T.astype(o_ref.dtype)


def _band_attention(proj, sections, bias_t, batch, seq, sub_len):
    t = proj.shape[0]

    def section(s):
        return pl.BlockSpec((seq, LANES), lambda b, p: (b, s * HEAD_PAIRS + p))

    return pl.pallas_call(
        functools.partial(_band_body, sub_len=sub_len),
        out_shape=(jax.ShapeDtypeStruct((t, D_MODEL), BF16),
                   jax.ShapeDtypeStruct((batch, HEAD_PAIRS, 2, seq), F32)),
        grid=(batch, HEAD_PAIRS),
        in_specs=[section(s) for s in sections] + [
            pl.BlockSpec((2, 2 * DIL_BLOCK, DIL_BLOCK), lambda b, p: (p, 0, 0)),
        ],
        out_specs=(pl.BlockSpec((seq, LANES), lambda b, p: (b, p)),
                   pl.BlockSpec((1, 1, 2, seq), lambda b, p: (b, p, 0, 0))),
        scratch_shapes=[pltpu.VMEM((LANES, seq), BF16)],
        compiler_params=_params("parallel", "parallel"),
        name="band_attention",
    )(proj, proj, proj, bias_t)


def _softmax_merge_body(o1_ref, o2_ref, o3_ref, s_ref, e_ref, o_ref):
    s = s_ref[...]
    groups = [s, pltpu.roll(s, LANES - N_HEADS, axis=1), pltpu.roll(s, LANES - 2 * N_HEADS, axis=1)]
    top = jnp.maximum(jnp.maximum(groups[0], groups[1]), groups[2])
    weights = [jnp.exp(g - top) for g in groups]
    inv = 1.0 / (weights[0] + weights[1] + weights[2])
    out = jnp.zeros(o_ref.shape, F32)
    for w, part in zip(weights, (o1_ref, o2_ref, o3_ref)):
        spread = jnp.dot((w * inv).astype(BF16), e_ref[...], preferred_element_type=F32)
        out = out + spread * part[...].astype(F32)
    o_ref[...] = out.astype(o_ref.dtype)


def _softmax_merge(outs, s_all):
    t, d = outs[0].shape
    head_of = jnp.arange(d) // HEAD_DIM
    expand = (jnp.arange(LANES)[:, None] == head_of[None, :]).astype(BF16)
    rows = pl.BlockSpec((ROW_TILE, d), lambda i: (i, 0))
    return pl.pallas_call(
        _softmax_merge_body,
        out_shape=jax.ShapeDtypeStruct((t, d), BF16),
        grid=(t // ROW_TILE,),
        in_specs=[rows, rows, rows,
                  pl.BlockSpec((ROW_TILE, LANES), lambda i: (i, 0)),
                  pl.BlockSpec((LANES, d), lambda i: (0, 0))],
        out_specs=rows,
        compiler_params=_params("parallel"),
        name="softmax_merge",
    )(*outs, s_all, expand)


def _t5_causal_bucket(distance):
    max_exact = N_REL_BUCKETS // 2
    d = jnp.maximum(distance, 1).astype(F32)
    log_b = max_exact + (jnp.log(d / max_exact) / math.log(REL_MAX_DISTANCE / max_exact)
                         * (N_REL_BUCKETS - max_exact)).astype(jnp.int32)
    log_b = jnp.minimum(log_b, N_REL_BUCKETS - 1)
    return jnp.where(distance < max_exact, distance, log_b)


def _dilated_bias(rel_bias):
    kj = jnp.arange(2 * DIL_BLOCK, dtype=jnp.int32)
    qi = jnp.arange(DIL_BLOCK, dtype=jnp.int32)
    delta = qi[None, :] + DIL_BLOCK - kj[:, None]
    in_band = (delta >= 0) & (delta <= DIL_SPAN)
    buckets = jnp.stack([_t5_causal_bucket(jnp.maximum(delta, 0) * dil) for _, dil in DILATED_PAIRS])
    one_hot = (buckets[..., None] == jnp.arange(N_REL_BUCKETS)).astype(F32)
    bias = jnp.einsum("gkqb,bh->ghkq", one_hot, rel_bias.astype(F32), precision=lax.Precision.HIGHEST)
    return jnp.where(in_band[None, None], bias, NEG_INF)


def _dilated_attention(proj, rel_bias, batch, seq):
    t = proj.shape[0]
    n_groups = len(DILATED_PAIRS)
    bias_t = _dilated_bias(rel_bias)
    v_section = 2 * n_groups
    outs, stats = [], []
    for g, (_, dil) in enumerate(DILATED_PAIRS):
        sub_len = seq // dil
        if dil == 1:
            o, s = _band_attention(proj, (2 * g, 2 * g + 1, v_section), bias_t[g], batch, seq, sub_len)
        else:
            qkv = jnp.concatenate([proj[:, 2 * g * D_MODEL:(2 * g + 2) * D_MODEL],
                                   proj[:, v_section * D_MODEL:]], axis=1)
            qkv = qkv.reshape(batch, sub_len, dil, 3 * D_MODEL).transpose(0, 2, 1, 3)
            o, s = _band_attention(qkv.reshape(t, 3 * D_MODEL), (0, 1, 2), bias_t[g], batch, seq, sub_len)
            o = o.reshape(batch, dil, sub_len, D_MODEL).transpose(0, 2, 1, 3).reshape(t, D_MODEL)
            s = s.reshape(batch, HEAD_PAIRS, 2, dil, sub_len).swapaxes(3, 4)
        outs.append(o)
        stats.append(s.reshape(batch, N_HEADS, seq).transpose(0, 2, 1).reshape(t, N_HEADS))
    stats.append(jnp.zeros((t, LANES - n_groups * N_HEADS), F32))
    return _softmax_merge(outs, jnp.concatenate(stats, axis=1))


def _tile_heads(v):
    return jnp.tile(v.astype(F32), N_HEADS)


def _sb_mixer(x, gain, w_qkv, w_o, layer, batch, seq):
    ones = jnp.ones((D_MODEL,), F32)
    col_scale = jnp.concatenate([ones * QK_SCALE, ones, ones])
    proj = _norm_proj(x, gain, w_qkv, layer, col_scale, 0)
    return _out_proj_residual(x, _sb_attention(proj, batch, seq), w_o, layer)


def _dilated_mixer(x, gain, w_in, q_norm, k_norm, rel_bias, w_o, layer, batch, seq):
    scales = []
    for g in range(len(DILATED_PAIRS)):
        scales += [_tile_heads(q_norm[g]) * QK_SCALE, _tile_heads(k_norm[g])]
    scales.append(jnp.ones((D_MODEL,), F32))
    proj = _norm_proj(x, gain, w_in, layer, jnp.concatenate(scales), 2 * len(DILATED_PAIRS))
    return _out_proj_residual(x, _dilated_attention(proj, rel_bias, batch, seq), w_o, layer)


def _fox_mixer(x, gain, w_qkv, w_gate, b_f, q_norm, k_norm, w_o, layer, batch, seq):
    col_scale = jnp.concatenate([_tile_heads(q_norm) * QK_SCALE, _tile_heads(k_norm),
                                 jnp.ones((D_MODEL,), F32)])
    proj = _norm_proj(x, gain, w_qkv, layer, col_scale, 2)
    cum = _fox_cum_log_forget(x, gain, w_gate, b_f, batch, seq)
    return _out_proj_residual(x, _fox_attention(proj, cum, batch, seq), w_o, layer)


def kernel(x, sb_w_qkv, sb_w_o, dil_w_in, dil_q_norm, dil_k_norm, dil_w_o, fox_w_in, fox_b_f,
           fox_q_norm, fox_k_norm, fox_w_o, rel_bias, attn_norm, ffn_norm, mlp_w_gate, mlp_w_up,
           mlp_w_down, moe_router, moe_w_gate, moe_w_up, moe_w_down):
    batch, seq, d = x.shape
    depth = attn_norm.shape[0]
    sb_w_qkv, sb_w_o, dil_w_in, dil_w_o, fox_w_o, mlp_w_gate, mlp_w_up, mlp_w_down = (
        w.astype(BF16) for w in (sb_w_qkv, sb_w_o, dil_w_in, dil_w_o, fox_w_o, mlp_w_gate, mlp_w_up,
                                 mlp_w_down))
    moe_w_gate, moe_w_up, moe_w_down = (w.astype(BF16) for w in (moe_w_gate, moe_w_up, moe_w_down))
    fox_w_qkv = fox_w_in[:, :, :3 * D_MODEL].astype(BF16)
    h = x.reshape(batch * seq, d)
    for i in range(depth):
        kind, j = i % 3, i // 3
        if kind == 0:
            h = _sb_mixer(h, attn_norm[i], sb_w_qkv, sb_w_o, j, batch, seq)
        elif kind == 1:
            h = _dilated_mixer(h, attn_norm[i], dil_w_in, dil_q_norm[j], dil_k_norm[j], rel_bias,
                               dil_w_o, j, batch, seq)
        else:
            h = _fox_mixer(h, attn_norm[i], fox_w_qkv, fox_w_in[j, :, 3 * D_MODEL:], fox_b_f[j],
                           fox_q_norm[j], fox_k_norm[j], fox_w_o, j, batch, seq)
        f = i // 2
        if i % 2 == 0:
            h = _ffn_residual(h, ffn_norm[i], mlp_w_gate, mlp_w_up, mlp_w_down, f)
        else:
            h = _moe_residual(h, ffn_norm[i], moe_router[f], moe_w_gate, moe_w_up, moe_w_down, f)
    return h.reshape(batch, seq, d)
```

```python
import functools
import math

import jax
import jax.numpy as jnp
from jax import lax
from jax.experimental import pallas as pl
from jax.experimental.pallas import tpu as pltpu
from jax.experimental.pallas import tpu_sc as plsc

D_MODEL = 1024
N_HEADS = 16
HEAD_DIM = 64
LANES = 128
HEAD_PAIRS = D_MODEL // LANES
D_FF = 3584
N_EXPERTS = 8
N_REL_BUCKETS = 32
REL_MAX_DISTANCE = 2048
DILATED_PAIRS = ((128, 1), (512, 4), (2048, 16))
DIL_SPAN = 128
RMS_EPS = 1e-6
NEG_INF = -1e30
SB_UNDERFLOW_LOG = -104.0
QK_SCALE = 1.0 / math.sqrt(HEAD_DIM)

ROW_TILE = 1024
COL_TILE = 1024
FF_TILE = 512
MOE_ROW_TILE = 512
MOE_FF_TILE = 1792
SC_CORES = 2
SC_SUBCORES = 16
SC_CHUNK_ROWS = 64
NORM_CHUNK = 256
ATT_BLOCK = 256
ATT_Q_BLOCK = 512
ATT_K_PER_Q = ATT_Q_BLOCK // ATT_BLOCK
DIL_BLOCK = 128
VMEM_LIMIT = 56 * 1024 * 1024

F32 = jnp.float32
BF16 = jnp.bfloat16


def _params(*semantics):
    return pltpu.CompilerParams(dimension_semantics=semantics, vmem_limit_bytes=VMEM_LIMIT)


def _rms_normalize(x, gain):
    inv = lax.rsqrt(jnp.mean(x * x, axis=-1, keepdims=True) + RMS_EPS)
    return x * inv * gain


def _softplus(z):
    return jnp.maximum(z, 0.0) + jnp.log(1.0 + jnp.exp(-jnp.abs(z)))


def _norm_proj_body(x_ref, g_ref, w_ref, cs_ref, bd_ref, o_ref, h_scr, *, n_norm):
    j = pl.program_id(1)

    @pl.when(j == 0)
    def _():
        h_scr[...] = _rms_normalize(x_ref[...], g_ref[...]).astype(BF16)

    acc = jnp.dot(h_scr[...], w_ref[...], preferred_element_type=F32)

    def plain():
        o_ref[...] = (acc * cs_ref[...]).astype(o_ref.dtype)

    def head_normed():
        for c in range(COL_TILE // NORM_CHUNK):
            sl = slice(c * NORM_CHUNK, (c + 1) * NORM_CHUNK)
            a = acc[:, sl]
            ss = jnp.dot((a * a).astype(BF16), bd_ref[...], preferred_element_type=F32)
            inv = lax.rsqrt(ss * (1.0 / HEAD_DIM) + RMS_EPS)
            o_ref[:, sl] = (a * inv * cs_ref[:, sl]).astype(o_ref.dtype)

    if n_norm == 0:
        plain()
    else:
        pl.when(j < n_norm)(head_normed)
        pl.when(j >= n_norm)(plain)


def _norm_proj(x, gain, w, layer, col_scale, n_norm):
    t, d = x.shape
    n = w.shape[2]
    head_id = jnp.arange(NORM_CHUNK) // HEAD_DIM
    block_diag = (head_id[:, None] == head_id[None, :]).astype(BF16)
    return pl.pallas_call(
        functools.partial(_norm_proj_body, n_norm=n_norm),
        out_shape=jax.ShapeDtypeStruct((t, n), BF16),
        grid=(t // ROW_TILE, n // COL_TILE),
        in_specs=[
            pl.BlockSpec((ROW_TILE, d), lambda i, j: (i, 0)),
            pl.BlockSpec((1, d), lambda i, j: (0, 0)),
            pl.BlockSpec((None, d, COL_TILE), lambda i, j: (layer, 0, j)),
            pl.BlockSpec((1, COL_TILE), lambda i, j: (0, j)),
            pl.BlockSpec((NORM_CHUNK, NORM_CHUNK), lambda i, j: (0, 0)),
        ],
        out_specs=pl.BlockSpec((ROW_TILE, COL_TILE), lambda i, j: (i, j)),
        scratch_shapes=[pltpu.VMEM((ROW_TILE, d), BF16)],
        compiler_params=_params("parallel", "arbitrary"),
        name="norm_proj",
    )(x, gain.reshape(1, d), w, col_scale.reshape(1, n).astype(F32), block_diag)


def _out_proj_body(x_ref, o_ref, w_ref, y_ref):
    y_ref[...] = x_ref[...] + jnp.dot(o_ref[...], w_ref[...], preferred_element_type=F32)


def _out_proj_residual(x, o, w, layer):
    t, d = x.shape
    return pl.pallas_call(
        _out_proj_body,
        out_shape=jax.ShapeDtypeStruct((t, d), F32),
        grid=(t // ROW_TILE,),
        in_specs=[
            pl.BlockSpec((ROW_TILE, d), lambda i: (i, 0)),
            pl.BlockSpec((ROW_TILE, d), lambda i: (i, 0)),
            pl.BlockSpec((None, d, d), lambda i: (layer, 0, 0)),
        ],
        out_specs=pl.BlockSpec((ROW_TILE, d), lambda i: (i, 0)),
        compiler_params=_params("parallel"),
        name="out_proj",
    )(x, o, w)


def _swiglu_hidden(h, wg, wu):
    g = jnp.dot(h, wg, preferred_element_type=F32)
    u = jnp.dot(h, wu, preferred_element_type=F32)
    return g * (1.0 / (1.0 + jnp.exp(-g))) * u


def _ffn_body(x_ref, g_ref, wg_ref, wu_ref, wd_ref, y_ref, h_scr):
    f = pl.program_id(1)

    @pl.when(f == 0)
    def _():
        x = x_ref[...]
        h_scr[...] = _rms_normalize(x, g_ref[...]).astype(BF16)
        y_ref[...] = x

    a = _swiglu_hidden(h_scr[...], wg_ref[...], wu_ref[...]).astype(BF16)
    y_ref[...] += jnp.dot(a, wd_ref[...], preferred_element_type=F32)


def _ffn_residual(x, gain, w_gate, w_up, w_down, layer):
    t, d = x.shape
    ff = w_gate.shape[2]
    return pl.pallas_call(
        _ffn_body,
        out_shape=jax.ShapeDtypeStruct((t, d), F32),
        grid=(t // ROW_TILE, ff // FF_TILE),
        in_specs=[
            pl.BlockSpec((ROW_TILE, d), lambda i, f: (i, 0)),
            pl.BlockSpec((1, d), lambda i, f: (0, 0)),
            pl.BlockSpec((None, d, FF_TILE), lambda i, f: (layer, 0, f)),
            pl.BlockSpec((None, d, FF_TILE), lambda i, f: (layer, 0, f)),
            pl.BlockSpec((None, FF_TILE, d), lambda i, f: (layer, f, 0)),
        ],
        out_specs=pl.BlockSpec((ROW_TILE, d), lambda i, f: (i, 0)),
        scratch_shapes=[pltpu.VMEM((ROW_TILE, d), BF16)],
        compiler_params=_params("parallel", "arbitrary"),
        name="ffn",
    )(x, gain.reshape(1, d), w_gate, w_up, w_down)


def _pack_bf16_pairs(x):
    half = x.shape[1] // 2
    bits = pltpu.bitcast(x.astype(BF16).astype(F32), jnp.int32)
    return bits[:, :half] | lax.shift_right_logical(bits[:, half:], jnp.int32(16))


def _unpack_bf16_pairs(p):
    left = pltpu.bitcast(p & jnp.int32(-65536), F32)
    right = pltpu.bitcast(lax.shift_left(p, jnp.int32(16)), F32)
    return jnp.concatenate([left, right], axis=1).astype(BF16)


def _router_body(x_ref, g_ref, r_ref, tri_ref, gates_ref, rank_ref, h_ref, count_scr):
    @pl.when(pl.program_id(0) == 0)
    def _():
        count_scr[...] = jnp.zeros_like(count_scr)

    h = _rms_normalize(x_ref[...], g_ref[...])
    h_ref[...] = _pack_bf16_pairs(h)
    logits = jnp.dot(h, r_ref[...], preferred_element_type=F32, precision=lax.Precision.HIGHEST)
    lane = lax.broadcasted_iota(jnp.int32, logits.shape, 1).astype(F32)
    logits = jnp.where(lane < N_EXPERTS, logits, -jnp.inf)
    m1 = jnp.max(logits, axis=-1, keepdims=True)
    i1 = jnp.min(jnp.where(logits == m1, lane, float(LANES)), axis=-1, keepdims=True)
    rest = jnp.where(lane == i1, -jnp.inf, logits)
    m2 = jnp.max(rest, axis=-1, keepdims=True)
    i2 = jnp.min(jnp.where(rest == m2, lane, float(LANES)), axis=-1, keepdims=True)
    e = jnp.exp(m2 - m1)
    g1 = 1.0 / (1.0 + e)
    gates_ref[...] = jnp.where(lane == i1, g1, 0.0) + jnp.where(lane == i2, e * g1, 0.0)
    chosen = jnp.where((lane == i1) | (lane == i2), 1.0, 0.0)
    inclusive = jnp.dot(tri_ref[...], chosen.astype(BF16), preferred_element_type=F32)
    rank_ref[...] = jnp.where(chosen > 0.0, inclusive - 1.0 + count_scr[...], -1.0)
    count_scr[...] += inclusive[ROW_TILE - 1:ROW_TILE, :]


def _router(x, gain, router):
    t, d = x.shape
    r = jnp.zeros((d, LANES), F32).at[:, :N_EXPERTS].set(router)
    idx = jnp.arange(ROW_TILE)
    tri = (idx[:, None] >= idx[None, :]).astype(BF16)
    return pl.pallas_call(
        _router_body,
        out_shape=(jax.ShapeDtypeStruct((t, LANES), F32), jax.ShapeDtypeStruct((t, LANES), F32),
                   jax.ShapeDtypeStruct((t, d // 2), jnp.int32)),
        grid=(t // ROW_TILE,),
        in_specs=[
            pl.BlockSpec((ROW_TILE, d), lambda i: (i, 0)),
            pl.BlockSpec((1, d), lambda i: (0, 0)),
            pl.BlockSpec((d, LANES), lambda i: (0, 0)),
            pl.BlockSpec((ROW_TILE, ROW_TILE), lambda i: (0, 0)),
        ],
        out_specs=(pl.BlockSpec((ROW_TILE, LANES), lambda i: (i, 0)),
                   pl.BlockSpec((ROW_TILE, LANES), lambda i: (i, 0)),
                   pl.BlockSpec((ROW_TILE, d // 2), lambda i: (i, 0))),
        scratch_shapes=[pltpu.VMEM((1, LANES), F32)],
        compiler_params=_params("arbitrary"),
        name="router",
    )(x, gain.reshape(1, d), r, tri)


def _sc_mesh():
    return plsc.VectorSubcoreMesh(core_axis_name="core", subcore_axis_name="subcore",
                                  num_cores=SC_CORES, num_subcores=SC_SUBCORES)


def _sc_worker_base(per_worker):
    return (lax.axis_index("subcore") * SC_CORES + lax.axis_index("core")) * per_worker


def _sc_row_gather(table, idx):
    width = table.shape[1]
    n = idx.shape[0]
    per_worker = n // (SC_CORES * SC_SUBCORES)
    n_chunks = per_worker // SC_CHUNK_ROWS
    assert n == n_chunks * SC_CHUNK_ROWS * SC_CORES * SC_SUBCORES and n_chunks % 2 == 0

    @functools.partial(
        pl.kernel, mesh=_sc_mesh(), out_type=jax.ShapeDtypeStruct((n, width), table.dtype),
        scratch_types=[pltpu.VMEM((SC_CHUNK_ROWS,), jnp.int32), pltpu.VMEM((SC_CHUNK_ROWS,), jnp.int32),
                       pltpu.VMEM((SC_CHUNK_ROWS, width), table.dtype),
                       pltpu.VMEM((SC_CHUNK_ROWS, width), table.dtype),
                       pltpu.SemaphoreType.DMA, pltpu.SemaphoreType.DMA],
        name="sc_row_gather")
    def gather(table_hbm, idx_hbm, out_hbm, idx_a, idx_b, rows_a, rows_b, sem_a, sem_b):
        base = _sc_worker_base(per_worker)
        bufs = ((idx_a, rows_a, sem_a), (idx_b, rows_b, sem_b))

        def rows_of(c):
            return pl.ds(pl.multiple_of(base + c * SC_CHUNK_ROWS, SC_CHUNK_ROWS), SC_CHUNK_ROWS)

        def fetch(c, buf):
            idx_v, rows_v, sem = buf
            pltpu.sync_copy(idx_hbm.at[rows_of(c)], idx_v)
            return pltpu.make_async_copy(table_hbm.at[idx_v], rows_v, sem)

        fetch(0, bufs[0]).start()

        @pl.loop(0, n_chunks, step=2)
        def _(c):
            for b in range(2):
                idx_v, rows_v, sem = bufs[b]
                pltpu.make_async_copy(table_hbm.at[idx_v], rows_v, sem).wait()

                @pl.when(c + b + 1 < n_chunks)
                def _():
                    fetch(c + b + 1, bufs[1 - b]).start()

                pltpu.sync_copy(rows_v, out_hbm.at[rows_of(c + b)])

    return gather(table, idx)


def _sc_row_scatter_pair(rows, idx_lo, idx_hi, n_out):
    n, width = rows.shape
    per_worker = n // (SC_CORES * SC_SUBCORES)
    n_chunks = per_worker // SC_CHUNK_ROWS
    assert n == n_chunks * SC_CHUNK_ROWS * SC_CORES * SC_SUBCORES and n_chunks % 2 == 0
    index_scratch = pltpu.VMEM((SC_CHUNK_ROWS,), jnp.int32)
    rows_scratch = pltpu.VMEM((SC_CHUNK_ROWS, width), rows.dtype)

    @functools.partial(
        pl.kernel, mesh=_sc_mesh(), out_type=jax.ShapeDtypeStruct((n_out, width), rows.dtype),
        scratch_types=[index_scratch, index_scratch, rows_scratch, rows_scratch,
                       pltpu.SemaphoreType.DMA, pltpu.SemaphoreType.DMA],
        name="sc_row_scatter")
    def scatter(rows_hbm, lo_hbm, hi_hbm, out_hbm, lo_v, hi_v, rows_a, rows_b, sem_a, sem_b):
        base = _sc_worker_base(per_worker)
        bufs = ((rows_a, sem_a), (rows_b, sem_b))

        def rows_of(c):
            return pl.ds(pl.multiple_of(base + c * SC_CHUNK_ROWS, SC_CHUNK_ROWS), SC_CHUNK_ROWS)

        def load(c, buf):
            rows_v, sem = buf
            return pltpu.make_async_copy(rows_hbm.at[rows_of(c)], rows_v, sem)

        load(0, bufs[0]).start()

        @pl.loop(0, n_chunks, step=2)
        def _(c):
            for b in range(2):
                rows_v, _ = bufs[b]
                load(c + b, bufs[b]).wait()

                @pl.when(c + b + 1 < n_chunks)
                def _():
                    load(c + b + 1, bufs[1 - b]).start()

                pltpu.sync_copy(lo_hbm.at[rows_of(c + b)], lo_v)
                pltpu.sync_copy(hi_hbm.at[rows_of(c + b)], hi_v)
                pltpu.sync_copy(rows_v, out_hbm.at[lo_v])
                pltpu.sync_copy(rows_v, out_hbm.at[hi_v])

    return scatter(rows, idx_lo, idx_hi)


def _expert_ffn_body(te_ref, nu_ref, nv_ref, h_ref, wg_ref, wu_ref, wd_ref, y_ref, acc_scr):
    i = pl.program_id(0)
    f = pl.program_id(1)

    @pl.when(i < nu_ref[0])
    def _():
        row = lax.broadcasted_iota(jnp.int32, h_ref.shape, 0)
        packed = jnp.where(row < nv_ref[i], h_ref[...], 0)
        a = _swiglu_hidden(_unpack_bf16_pairs(packed), wg_ref[...], wu_ref[...]).astype(BF16)
        y = jnp.dot(a, wd_ref[...], preferred_element_type=F32)

        @pl.when(f == 0)
        def _():
            acc_scr[...] = y

        @pl.when(f > 0)
        def _():
            acc_scr[...] += y

        @pl.when(f == pl.num_programs(1) - 1)
        def _():
            y_ref[...] = _pack_bf16_pairs(acc_scr[...])


def _expert_ffn(h_sorted, tile_expert, n_used, tile_valid, w_gate, w_up, w_down, layer):
    rows, half = h_sorted.shape
    d = 2 * half
    ff = w_gate.shape[3]
    n_f = ff // MOE_FF_TILE

    def row_map(i, f, te, nu, nv):
        return (jnp.minimum(i, nu[0] - 1), 0)

    def col_step(i, f, nu):
        return jnp.where(i < nu[0], f, n_f - 1)

    grid_spec = pltpu.PrefetchScalarGridSpec(
        num_scalar_prefetch=3,
        grid=(rows // MOE_ROW_TILE, n_f),
        in_specs=[
            pl.BlockSpec((MOE_ROW_TILE, half), row_map),
            pl.BlockSpec((None, None, d, MOE_FF_TILE),
                         lambda i, f, te, nu, nv: (layer, te[i], 0, col_step(i, f, nu))),
            pl.BlockSpec((None, None, d, MOE_FF_TILE),
                         lambda i, f, te, nu, nv: (layer, te[i], 0, col_step(i, f, nu))),
            pl.BlockSpec((None, None, MOE_FF_TILE, d),
                         lambda i, f, te, nu, nv: (layer, te[i], col_step(i, f, nu), 0)),
        ],
        out_specs=pl.BlockSpec((MOE_ROW_TILE, half), row_map),
        scratch_shapes=[pltpu.VMEM((MOE_ROW_TILE, d), F32)],
    )
    return pl.pallas_call(
        _expert_ffn_body,
        out_shape=jax.ShapeDtypeStruct((rows, half), jnp.int32),
        grid_spec=grid_spec,
        compiler_params=_params("arbitrary", "arbitrary"),
        name="expert_ffn",
    )(tile_expert, n_used, tile_valid, h_sorted, w_gate, w_up, w_down)


def _combine_body(x_ref, y_ref, g_ref, o_ref):
    g = g_ref[...]
    first = _unpack_bf16_pairs(y_ref[0]).astype(F32) * g[:, 0:1]
    second = _unpack_bf16_pairs(y_ref[1]).astype(F32) * g[:, 1:2]
    o_ref[...] = x_ref[...] + first + second


def _combine_residual(x, y_pairs, gates2):
    t, d = x.shape
    return pl.pallas_call(
        _combine_body,
        out_shape=jax.ShapeDtypeStruct((t, d), F32),
        grid=(t // ROW_TILE,),
        in_specs=[
            pl.BlockSpec((ROW_TILE, d), lambda i: (i, 0)),
            pl.BlockSpec((2, ROW_TILE, d // 2), lambda i: (0, i, 0)),
            pl.BlockSpec((ROW_TILE, 2), lambda i: (i, 0)),
        ],
        out_specs=pl.BlockSpec((ROW_TILE, d), lambda i: (i, 0)),
        compiler_params=_params("parallel"),
        name="moe_combine",
    )(x, y_pairs, gates2)


def _moe_residual(x, gain, router, w_gate, w_up, w_down, layer):
    t, d = x.shape
    gates, rank, h_packed = _router(x, gain, router)
    rank8 = rank[:, :N_EXPERTS].astype(jnp.int32)
    chosen = rank8 >= 0
    counts = jnp.sum(chosen, axis=0, dtype=jnp.int32)
    padded = (counts + MOE_ROW_TILE - 1) // MOE_ROW_TILE * MOE_ROW_TILE
    ends = jnp.cumsum(padded)
    starts = ends - padded
    pos = starts[None, :] + rank8
    max_rows = 2 * t + N_EXPERTS * MOE_ROW_TILE
    pos_lo = jnp.min(jnp.where(chosen, pos, max_rows), axis=1)
    pos_hi = jnp.max(jnp.where(chosen, pos, -1), axis=1)
    gates8 = gates[:, :N_EXPERTS]
    gates2 = jnp.stack([jnp.sum(jnp.where(chosen & (pos == pos_lo[:, None]), gates8, 0.0), axis=1),
                        jnp.sum(jnp.where(chosen & (pos == pos_hi[:, None]), gates8, 0.0), axis=1)], axis=1)
    n_tiles = max_rows // MOE_ROW_TILE
    n_used = (ends[-1] // MOE_ROW_TILE).astype(jnp.int32)
    tile_start = jnp.minimum(jnp.arange(n_tiles, dtype=jnp.int32), n_used - 1) * MOE_ROW_TILE
    tile_expert = jnp.sum(tile_start[:, None] >= ends[None, :], axis=1, dtype=jnp.int32)
    tile_valid = jnp.clip((starts + counts)[tile_expert] - tile_start, 0, MOE_ROW_TILE).astype(jnp.int32)

    h_sorted = _sc_row_scatter_pair(h_packed, pos_lo, pos_hi, max_rows)
    y_sorted = _expert_ffn(h_sorted, tile_expert, n_used.reshape(1), tile_valid, w_gate, w_up, w_down, layer)
    y_pairs = _sc_row_gather(y_sorted, jnp.concatenate([pos_lo, pos_hi])).reshape(2, t, d // 2)
    return _combine_residual(x, y_pairs, gates2)


def _split_head_pair(q):
    is_first = lax.broadcasted_iota(jnp.int32, (1, LANES), 1) < HEAD_DIM
    zero = jnp.zeros_like(q)
    return jnp.where(is_first, q, zero), jnp.where(is_first, zero, q)


def _merge_head_pair(first, second):
    is_first = lax.broadcasted_iota(jnp.int32, (1, LANES), 1) < HEAD_DIM
    return jnp.where(is_first, first, second)


def _qk(q, k):
    return lax.dot_general(q, k, (((1,), (1,)), ((), ())), preferred_element_type=F32)


def _transpose_values(v_ref, vt_scr):
    for c in range(v_ref.shape[0] // ATT_BLOCK):
        rows = slice(c * ATT_BLOCK, (c + 1) * ATT_BLOCK)
        vt_scr[:, rows] = v_ref[rows, :].astype(F32).T.astype(vt_scr.dtype)


def _sb_body(q_ref, k_ref, v_ref, u_ref, o_ref, vt_scr, acc_scr, carry_scr, z_scr, keep_scr, sum_scr):
    qi = pl.program_id(2)

    @pl.when(qi == 0)
    def _():
        _transpose_values(v_ref, vt_scr)

    upper = u_ref[...]
    acc_scr[...] = jnp.zeros_like(acc_scr)
    carry_scr[...] = jnp.zeros_like(carry_scr)
    key = lax.broadcasted_iota(jnp.int32, (ATT_BLOCK, ATT_BLOCK), 0)
    query = lax.broadcasted_iota(jnp.int32, (ATT_BLOCK, ATT_BLOCK), 1)
    strict = key < query
    qs = [_split_head_pair(q_ref[j * ATT_BLOCK:(j + 1) * ATT_BLOCK, :]) for j in range(ATT_K_PER_Q)]

    def process(j, kb, keep):
        start = pl.multiple_of(kb * ATT_BLOCK, ATT_BLOCK)
        k = k_ref[pl.ds(start, ATT_BLOCK), :]
        for hh in range(2):
            z = _qk(k, qs[j][hh])
            log_beta = z - _softplus(z)
            log_keep = log_beta - z
            if keep is not None:
                log_keep = jnp.where(keep, log_keep, 0.0)
            remain = jnp.dot(upper, log_keep.astype(BF16), preferred_element_type=F32)
            w = jnp.exp(log_beta + remain + carry_scr[j, hh])
            if keep is not None:
                w = jnp.where(keep, w, 0.0)
            carry_scr[j, hh] += jnp.sum(log_keep, axis=0, keepdims=True)
            vt = vt_scr[pl.ds(hh * HEAD_DIM, HEAD_DIM), pl.ds(start, ATT_BLOCK)]
            acc_scr[j, hh] += jnp.dot(vt, w.astype(BF16), preferred_element_type=F32)

    units = []
    for j in range(ATT_K_PER_Q):
        g = qi * ATT_K_PER_Q + j
        has_previous = None if j > 0 else jnp.broadcast_to(g > 0, strict.shape)
        for kb, keep in ((g, strict), (jnp.maximum(g - 1, 0), has_previous)):
            for hh in range(2):
                units.append((j, hh, pl.multiple_of(kb * ATT_BLOCK, ATT_BLOCK), keep))

    for u, (j, hh, start, keep) in enumerate(units):
        z_scr[u] = _qk(k_ref[pl.ds(start, ATT_BLOCK), :], qs[j][hh])

    for u, (j, hh, start, keep) in enumerate(units):
        z = z_scr[u]
        log_beta = z - _softplus(z)
        log_keep = log_beta - z
        if keep is not None:
            log_keep = jnp.where(keep, log_keep, 0.0)
        z_scr[u] = log_beta
        keep_scr[u] = log_keep.astype(BF16)
        sum_scr[u] = jnp.sum(log_keep, axis=0, keepdims=True)

    for u in range(len(units)):
        z_scr[u] += jnp.dot(upper, keep_scr[u], preferred_element_type=F32)

    for u, (j, hh, start, keep) in enumerate(units):
        diagonal = u % 4 < 2
        log_w = z_scr[u] if diagonal else z_scr[u] + sum_scr[u - 2]
        w = jnp.exp(log_w)
        if keep is not None:
            w = jnp.where(keep, w, 0.0)
        keep_scr[u] = w.astype(BF16)

    for u, (j, hh, start, keep) in enumerate(units):
        vt = vt_scr[pl.ds(hh * HEAD_DIM, HEAD_DIM), pl.ds(start, ATT_BLOCK)]
        acc_scr[j, hh] += jnp.dot(vt, keep_scr[u], preferred_element_type=F32)
        carry_scr[j, hh] += sum_scr[u]

    for j in range(ATT_K_PER_Q):
        def more(kb, j=j):
            return (kb >= 0) & (jnp.max(carry_scr[j]) > SB_UNDERFLOW_LOG)

        def step(kb, j=j):
            process(j, kb, None)
            return kb - 1

        lax.while_loop(more, step, qi * ATT_K_PER_Q + j - 2)

    for j in range(ATT_K_PER_Q):
        out_t = jnp.concatenate([acc_scr[j, 0], acc_scr[j, 1]], axis=0)
        o_ref[j * ATT_BLOCK:(j + 1) * ATT_BLOCK, :] = out_t.T.astype(o_ref.dtype)


def _sb_attention(proj, batch, seq):
    t = proj.shape[0]
    nq = seq // ATT_Q_BLOCK
    idx = jnp.arange(ATT_BLOCK)
    upper = (idx[None, :] > idx[:, None]).astype(BF16)
    return pl.pallas_call(
        _sb_body,
        out_shape=jax.ShapeDtypeStruct((t, D_MODEL), BF16),
        grid=(batch, HEAD_PAIRS, nq),
        in_specs=[
            pl.BlockSpec((ATT_Q_BLOCK, LANES), lambda b, p, i: (b * nq + i, p)),
            pl.BlockSpec((seq, LANES), lambda b, p, i: (b, HEAD_PAIRS + p)),
            pl.BlockSpec((seq, LANES), lambda b, p, i: (b, 2 * HEAD_PAIRS + p)),
            pl.BlockSpec((ATT_BLOCK, ATT_BLOCK), lambda b, p, i: (0, 0)),
        ],
        out_specs=pl.BlockSpec((ATT_Q_BLOCK, LANES), lambda b, p, i: (b * nq + i, p)),
        scratch_shapes=[
            pltpu.VMEM((LANES, seq), BF16),
            pltpu.VMEM((ATT_K_PER_Q, 2, HEAD_DIM, ATT_BLOCK), F32),
            pltpu.VMEM((ATT_K_PER_Q, 2, 1, ATT_BLOCK), F32),
            pltpu.VMEM((4 * ATT_K_PER_Q, ATT_BLOCK, ATT_BLOCK), F32),
            pltpu.VMEM((4 * ATT_K_PER_Q, ATT_BLOCK, ATT_BLOCK), BF16),
            pltpu.VMEM((4 * ATT_K_PER_Q, 1, ATT_BLOCK), F32),
        ],
        compiler_params=_params("parallel", "parallel", "arbitrary"),
        name="sb_attention",
    )(proj, proj, proj, upper)


def _fox_gate_body(x_ref, g_ref, w_ref, b_ref, tri_ref, c_ref):
    h = _rms_normalize(x_ref[...], g_ref[...])
    logits = jnp.dot(h, w_ref[...], preferred_element_type=F32, precision=lax.Precision.HIGHEST)
    log_f = -_softplus(-(logits + b_ref[...]))
    seq = log_f.shape[0]
    carry = jnp.zeros((1, LANES), F32)
    for blk in range(seq // ATT_BLOCK):
        rows = slice(blk * ATT_BLOCK, (blk + 1) * ATT_BLOCK)
        c = jnp.dot(tri_ref[...], log_f[rows], preferred_element_type=F32,
                    precision=lax.Precision.HIGHEST) + carry
        c_ref[rows, :] = c
        carry = c[ATT_BLOCK - 1:ATT_BLOCK, :]


def _fox_cum_log_forget(x, gain, w_gate, b_gate, batch, seq):
    t, d = x.shape
    w = jnp.zeros((d, LANES), F32).at[:, :N_HEADS].set(w_gate)
    b = jnp.zeros((1, LANES), F32).at[0, :N_HEADS].set(b_gate)
    idx = jnp.arange(ATT_BLOCK)
    tri = (idx[:, None] >= idx[None, :]).astype(F32)
    return pl.pallas_call(
        _fox_gate_body,
        out_shape=jax.ShapeDtypeStruct((t, LANES), F32),
        grid=(batch,),
        in_specs=[
            pl.BlockSpec((seq, d), lambda i: (i, 0)),
            pl.BlockSpec((1, d), lambda i: (0, 0)),
            pl.BlockSpec((d, LANES), lambda i: (0, 0)),
            pl.BlockSpec((1, LANES), lambda i: (0, 0)),
            pl.BlockSpec((ATT_BLOCK, ATT_BLOCK), lambda i: (0, 0)),
        ],
        out_specs=pl.BlockSpec((seq, LANES), lambda i: (i, 0)),
        compiler_params=_params("parallel"),
        name="fox_gate",
    )(x, gain.reshape(1, d), w, b, tri)


def _fox_body(q_ref, k_ref, v_ref, cq_ref, ck_ref, o_ref, vt_scr, ckb_scr, a_scr, p_scr):
    qi = pl.program_id(2)

    @pl.when(qi == 0)
    def _():
        _transpose_values(v_ref, vt_scr)
        for hh in range(2):
            ckb_scr[hh] = jnp.broadcast_to(ck_ref[0, hh], ckb_scr.shape[1:])

    qs = _split_head_pair(q_ref[...])
    key = lax.broadcasted_iota(jnp.int32, (ATT_BLOCK, ATT_Q_BLOCK), 0)
    query = lax.broadcasted_iota(jnp.int32, (ATT_BLOCK, ATT_Q_BLOCK), 1)
    row_shape = (1, ATT_Q_BLOCK)

    def scores(kb, tops, diagonal):
        start = pl.multiple_of(kb * ATT_BLOCK, ATT_BLOCK)
        k = k_ref[pl.ds(start, ATT_BLOCK), :]
        if diagonal:
            causal = key + (kb * ATT_BLOCK - qi * ATT_Q_BLOCK) <= query
        new_tops = []
        for hh in range(2):
            ck = ckb_scr[hh, pl.ds(start, ATT_BLOCK), :]
            a = _qk(k, qs[hh]) - jnp.concatenate([ck] * (ATT_Q_BLOCK // LANES), axis=1)
            if diagonal:
                a = jnp.where(causal, a, NEG_INF)
            a_scr[hh, kb] = a
            new_tops.append(jnp.maximum(tops[hh], jnp.max(a, axis=0, keepdims=True)))
        return tuple(new_tops)

    def scores_step(it, tops):
        for r in range(ATT_K_PER_Q):
            tops = scores(it * ATT_K_PER_Q + r, tops, False)
        return tops

    tops = lax.fori_loop(0, qi, scores_step, (jnp.full(row_shape, NEG_INF, F32),) * 2)
    for r in range(ATT_K_PER_Q):
        tops = scores(qi * ATT_K_PER_Q + r, tops, True)
    shifts = tuple(cq_ref[0, hh] - (tops[hh] + cq_ref[0, hh]) for hh in range(2))

    def weights_step(it, sums):
        sums = list(sums)
        for r in range(ATT_K_PER_Q):
            kb = it * ATT_K_PER_Q + r
            for hh in range(2):
                p = jnp.exp(a_scr[hh, kb] + shifts[hh])
                p_scr[hh, kb] = p.astype(BF16)
                sums[hh] = sums[hh] + jnp.sum(p, axis=0, keepdims=True)
        return tuple(sums)

    sums = lax.fori_loop(0, qi + 1, weights_step, (jnp.zeros(row_shape, F32),) * 2)

    def values_step(it, accs):
        accs = list(accs)
        for r in range(ATT_K_PER_Q):
            kb = it * ATT_K_PER_Q + r
            start = pl.multiple_of(kb * ATT_BLOCK, ATT_BLOCK)
            for hh in range(2):
                vt = vt_scr[pl.ds(hh * HEAD_DIM, HEAD_DIM), pl.ds(start, ATT_BLOCK)]
                accs[hh] = accs[hh] + jnp.dot(vt, p_scr[hh, kb], preferred_element_type=F32)
        return tuple(accs)

    accs = lax.fori_loop(0, qi + 1, values_step, (jnp.zeros((HEAD_DIM, ATT_Q_BLOCK), F32),) * 2)
    out_t = jnp.concatenate([accs[0] / sums[0], accs[1] / sums[1]], axis=0)
    o_ref[...] = out_t.T.astype(o_ref.dtype)


def _fox_attention(proj, cum, batch, seq):
    t = proj.shape[0]
    nq = seq // ATT_Q_BLOCK
    cum_h = cum[:, :N_HEADS].reshape(batch, seq, N_HEADS).transpose(0, 2, 1)
    cum_q = cum_h.reshape(batch, N_HEADS, 1, seq)
    cum_k = cum_h.reshape(batch, N_HEADS, seq, 1)
    return pl.pallas_call(
        _fox_body,
        out_shape=jax.ShapeDtypeStruct((t, D_MODEL), BF16),
        grid=(batch, HEAD_PAIRS, nq),
        in_specs=[
            pl.BlockSpec((ATT_Q_BLOCK, LANES), lambda b, p, i: (b * nq + i, p)),
            pl.BlockSpec((seq, LANES), lambda b, p, i: (b, HEAD_PAIRS + p)),
            pl.BlockSpec((seq, LANES), lambda b, p, i: (b, 2 * HEAD_PAIRS + p)),
            pl.BlockSpec((1, 2, 1, ATT_Q_BLOCK), lambda b, p, i: (b, p, 0, i)),
            pl.BlockSpec((1, 2, seq, 1), lambda b, p, i: (b, p, 0, 0)),
        ],
        out_specs=pl.BlockSpec((ATT_Q_BLOCK, LANES), lambda b, p, i: (b * nq + i, p)),
        scratch_shapes=[
            pltpu.VMEM((LANES, seq), BF16),
            pltpu.VMEM((2, seq, LANES), F32),
            pltpu.VMEM((2, seq // ATT_BLOCK, ATT_BLOCK, ATT_Q_BLOCK), F32),
            pltpu.VMEM((2, seq // ATT_BLOCK, ATT_BLOCK, ATT_Q_BLOCK), BF16),
        ],
        compiler_params=_params("parallel", "parallel", "arbitrary"),
        name="fox_attention",
    )(proj, proj, proj, cum_q, cum_k)


def _band_body(q_ref, k_ref, v_ref, bias_ref, o_ref, s_ref, vt_scr, logit_scr, p_scr, inv_scr, *, sub_len):
    seq = v_ref.shape[0]
    n_blocks = seq // DIL_BLOCK
    _transpose_values(v_ref, vt_scr)

    def key_rows(n):
        first = (n * DIL_BLOCK) % sub_len == 0
        return first, slice((n if first else n - 1) * DIL_BLOCK, (n + 1) * DIL_BLOCK)

    for n in range(n_blocks):
        first, k_rows = key_rows(n)
        qs = _split_head_pair(q_ref[n * DIL_BLOCK:(n + 1) * DIL_BLOCK, :])
        k = k_ref[k_rows, :]
        for hh in range(2):
            bias = bias_ref[hh, DIL_BLOCK:, :] if first else bias_ref[hh]
            logit_scr[2 * n + hh, :k.shape[0], :] = _qk(k, qs[hh]) + bias

    for n in range(n_blocks):
        first, k_rows = key_rows(n)
        n_keys = k_rows.stop - k_rows.start
        for hh in range(2):
            logits = logit_scr[2 * n + hh, :n_keys, :]
            m = jnp.max(logits, axis=0, keepdims=True)
            p = jnp.exp(logits - m)
            l = jnp.sum(p, axis=0, keepdims=True)
            p_scr[2 * n + hh, :n_keys, :] = p.astype(BF16)
            inv_scr[2 * n + hh] = 1.0 / l
            s_ref[0, 0, hh:hh + 1, n * DIL_BLOCK:(n + 1) * DIL_BLOCK] = m + jnp.log(l)

    for n in range(n_blocks):
        first, k_rows = key_rows(n)
        n_keys = k_rows.stop - k_rows.start
        outs = []
        for hh in range(2):
            vt = vt_scr[hh * HEAD_DIM:(hh + 1) * HEAD_DIM, k_rows]
            o = jnp.dot(vt, p_scr[2 * n + hh, :n_keys, :], preferred_element_type=F32)
            outs.append(o * inv_scr[2 * n + hh])
        o_ref[n * DIL_BLOCK:(n + 1) * DIL_BLOCK, :] = jnp.concatenate(outs, axis=0).T.astype(o_ref.dtype)


def _band_attention(proj, sections, bias_t, batch, seq, sub_len):
    t = proj.shape[0]

    def section(s):
        return pl.BlockSpec((seq, LANES), lambda b, p: (b, s * HEAD_PAIRS + p))

    return pl.pallas_call(
        functools.partial(_band_body, sub_len=sub_len),
        out_shape=(jax.ShapeDtypeStruct((t, D_MODEL), BF16),
                   jax.ShapeDtypeStruct((batch, HEAD_PAIRS, 2, seq), F32)),
        grid=(batch, HEAD_PAIRS),
        in_specs=[section(s) for s in sections] + [
            pl.BlockSpec((2, 2 * DIL_BLOCK, DIL_BLOCK), lambda b, p: (p, 0, 0)),
        ],
        out_specs=(pl.BlockSpec((seq, LANES), lambda b, p: (b, p)),
                   pl.BlockSpec((1, 1, 2, seq), lambda b, p: (b, p, 0, 0))),
        scratch_shapes=[pltpu.VMEM((LANES, seq), BF16),
                        pltpu.VMEM((2 * seq // DIL_BLOCK, 2 * DIL_BLOCK, DIL_BLOCK), F32),
                        pltpu.VMEM((2 * seq // DIL_BLOCK, 2 * DIL_BLOCK, DIL_BLOCK), BF16),
                        pltpu.VMEM((2 * seq // DIL_BLOCK, 1, DIL_BLOCK), F32)],
        compiler_params=_params("parallel", "parallel"),
        name="band_attention",
    )(proj, proj, proj, bias_t)


def _softmax_merge_body(o1_ref, o2_ref, o3_ref, s_ref, e_ref, o_ref):
    s = s_ref[...]
    groups = [s, pltpu.roll(s, LANES - N_HEADS, axis=1), pltpu.roll(s, LANES - 2 * N_HEADS, axis=1)]
    top = jnp.maximum(jnp.maximum(groups[0], groups[1]), groups[2])
    weights = [jnp.exp(g - top) for g in groups]
    inv = 1.0 / (weights[0] + weights[1] + weights[2])
    out = jnp.zeros(o_ref.shape, F32)
    for w, part in zip(weights, (o1_ref, o2_ref, o3_ref)):
        spread = jnp.dot((w * inv).astype(BF16), e_ref[...], preferred_element_type=F32)
        out = out + spread * part[...].astype(F32)
    o_ref[...] = out.astype(o_ref.dtype)


def _softmax_merge(outs, s_all):
    t, d = outs[0].shape
    head_of = jnp.arange(d) // HEAD_DIM
    expand = (jnp.arange(LANES)[:, None] == head_of[None, :]).astype(BF16)
    rows = pl.BlockSpec((ROW_TILE, d), lambda i: (i, 0))
    return pl.pallas_call(
        _softmax_merge_body,
        out_shape=jax.ShapeDtypeStruct((t, d), BF16),
        grid=(t // ROW_TILE,),
        in_specs=[rows, rows, rows,
                  pl.BlockSpec((ROW_TILE, LANES), lambda i: (i, 0)),
                  pl.BlockSpec((LANES, d), lambda i: (0, 0))],
        out_specs=rows,
        compiler_params=_params("parallel"),
        name="softmax_merge",
    )(*outs, s_all, expand)


def _t5_causal_bucket(distance):
    max_exact = N_REL_BUCKETS // 2
    d = jnp.maximum(distance, 1).astype(F32)
    log_b = max_exact + (jnp.log(d / max_exact) / math.log(REL_MAX_DISTANCE / max_exact)
                         * (N_REL_BUCKETS - max_exact)).astype(jnp.int32)
    log_b = jnp.minimum(log_b, N_REL_BUCKETS - 1)
    return jnp.where(distance < max_exact, distance, log_b)


def _dilated_bias(rel_bias):
    kj = jnp.arange(2 * DIL_BLOCK, dtype=jnp.int32)
    qi = jnp.arange(DIL_BLOCK, dtype=jnp.int32)
    delta = qi[None, :] + DIL_BLOCK - kj[:, None]
    in_band = (delta >= 0) & (delta <= DIL_SPAN)
    buckets = jnp.stack([_t5_causal_bucket(jnp.maximum(delta, 0) * dil) for _, dil in DILATED_PAIRS])
    one_hot = (buckets[..., None] == jnp.arange(N_REL_BUCKETS)).astype(F32)
    bias = jnp.einsum("gkqb,bh->ghkq", one_hot, rel_bias.astype(F32), precision=lax.Precision.HIGHEST)
    return jnp.where(in_band[None, None], bias, NEG_INF)


def _dilated_attention(proj, rel_bias, batch, seq):
    t = proj.shape[0]
    n_groups = len(DILATED_PAIRS)
    bias_t = _dilated_bias(rel_bias)
    v_section = 2 * n_groups
    outs, stats = [], []
    for g, (_, dil) in enumerate(DILATED_PAIRS):
        sub_len = seq // dil
        if dil == 1:
            o, s = _band_attention(proj, (2 * g, 2 * g + 1, v_section), bias_t[g], batch, seq, sub_len)
        else:
            qkv = jnp.concatenate([proj[:, 2 * g * D_MODEL:(2 * g + 2) * D_MODEL],
                                   proj[:, v_section * D_MODEL:]], axis=1)
            qkv = qkv.reshape(batch, sub_len, dil, 3 * D_MODEL).transpose(0, 2, 1, 3)
            o, s = _band_attention(qkv.reshape(t, 3 * D_MODEL), (0, 1, 2), bias_t[g], batch, seq, sub_len)
            o = o.reshape(batch, dil, sub_len, D_MODEL).transpose(0, 2, 1, 3).reshape(t, D_MODEL)
            s = s.reshape(batch, HEAD_PAIRS, 2, dil, sub_len).swapaxes(3, 4)
        outs.append(o)
        stats.append(s.reshape(batch, N_HEADS, seq).transpose(0, 2, 1).reshape(t, N_HEADS))
    stats.append(jnp.zeros((t, LANES - n_groups * N_HEADS), F32))
    return _softmax_merge(outs, jnp.concatenate(stats, axis=1))


def _tile_heads(v):
    return jnp.tile(v.astype(F32), N_HEADS)


def _sb_mixer(x, gain, w_qkv, w_o, layer, batch, seq):
    ones = jnp.ones((D_MODEL,), F32)
    col_scale = jnp.concatenate([ones * QK_SCALE, ones, ones])
    proj = _norm_proj(x, gain, w_qkv, layer, col_scale, 0)
    return _out_proj_residual(x, _sb_attention(proj, batch, seq), w_o, layer)


def _dilated_mixer(x, gain, w_in, q_norm, k_norm, rel_bias, w_o, layer, batch, seq):
    scales = []
    for g in range(len(DILATED_PAIRS)):
        scales += [_tile_heads(q_norm[g]) * QK_SCALE, _tile_heads(k_norm[g])]
    scales.append(jnp.ones((D_MODEL,), F32))
    proj = _norm_proj(x, gain, w_in, layer, jnp.concatenate(scales), 2 * len(DILATED_PAIRS))
    return _out_proj_residual(x, _dilated_attention(proj, rel_bias, batch, seq), w_o, layer)


def _fox_mixer(x, gain, w_qkv, w_gate, b_f, q_norm, k_norm, w_o, layer, batch, seq):
    col_scale = jnp.concatenate([_tile_heads(q_norm) * QK_SCALE, _tile_heads(k_norm),
                                 jnp.ones((D_MODEL,), F32)])
    proj = _norm_proj(x, gain, w_qkv, layer, col_scale, 2)
    cum = _fox_cum_log_forget(x, gain, w_gate, b_f, batch, seq)
    return _out_proj_residual(x, _fox_attention(proj, cum, batch, seq), w_o, layer)


def kernel(x, sb_w_qkv, sb_w_o, dil_w_in, dil_q_norm, dil_k_norm, dil_w_o, fox_w_in, fox_b_f,
           fox_q_norm, fox_k_norm, fox_w_o, rel_bias, attn_norm, ffn_norm, mlp_w_gate, mlp_w_up,
           mlp_w_down, moe_router, moe_w_gate, moe_w_up, moe_w_down):
    batch, seq, d = x.shape
    depth = attn_norm.shape[0]
    sb_w_qkv, sb_w_o, dil_w_in, dil_w_o, fox_w_o, mlp_w_gate, mlp_w_up, mlp_w_down = (
        w.astype(BF16) for w in (sb_w_qkv, sb_w_o, dil_w_in, dil_w_o, fox_w_o, mlp_w_gate, mlp_w_up,
                                 mlp_w_down))
    moe_w_gate, moe_w_up, moe_w_down = (w.astype(BF16) for w in (moe_w_gate, moe_w_up, moe_w_down))
    fox_w_qkv = fox_w_in[:, :, :3 * D_MODEL].astype(BF16)
    h = x.reshape(batch * seq, d)
    for i in range(depth):
        kind, j = i % 3, i // 3
        if kind == 0:
            h = _sb_mixer(h, attn_norm[i], sb_w_qkv, sb_w_o, j, batch, seq)
        elif kind == 1:
            h = _dilated_mixer(h, attn_norm[i], dil_w_in, dil_q_norm[j], dil_k_norm[j], rel_bias,
                               dil_w_o, j, batch, seq)
        else:
            h = _fox_mixer(h, attn_norm[i], fox_w_qkv, fox_w_in[j, :, 3 * D_MODEL:], fox_b_f[j],
                           fox_q_norm[j], fox_k_norm[j], fox_w_o, j, batch, seq)
        f = i // 2
        if i % 2 == 0:
            h = _ffn_residual(h, ffn_norm[i], mlp_w_gate, mlp_w_up, mlp_w_down, f)
        else:
            h = _moe_residual(h, ffn_norm[i], moe_router[f], moe_w_gate, moe_w_up, moe_w_down, f)
    return h.reshape(batch, seq, d)
```

```python
import functools
import math

import jax
import jax.numpy as jnp
from jax import lax
from jax.experimental import pallas as pl
from jax.experimental.pallas import tpu as pltpu
from jax.experimental.pallas import tpu_sc as plsc

D_MODEL = 1024
N_HEADS = 16
HEAD_DIM = 64
LANES = 128
HEAD_PAIRS = D_MODEL // LANES
D_FF = 3584
N_EXPERTS = 8
N_REL_BUCKETS = 32
REL_MAX_DISTANCE = 2048
DILATED_PAIRS = ((128, 1), (512, 4), (2048, 16))
DIL_SPAN = 128
RMS_EPS = 1e-6
NEG_INF = -1e30
SB_UNDERFLOW_LOG = -104.0
QK_SCALE = 1.0 / math.sqrt(HEAD_DIM)

ROW_TILE = 1024
COL_TILE = 1024
FF_TILE = 512
MOE_ROW_TILE = 512
MOE_FF_TILE = 1792
SC_CORES = 2
SC_SUBCORES = 16
SC_CHUNK_ROWS = 64
NORM_CHUNK = 256
ATT_BLOCK = 256
ATT_Q_BLOCK = 512
ATT_K_PER_Q = ATT_Q_BLOCK // ATT_BLOCK
DIL_BLOCK = 128
DIL_ROW_TILE = 512
VMEM_LIMIT = 56 * 1024 * 1024

F32 = jnp.float32
BF16 = jnp.bfloat16


def _params(*semantics):
    return pltpu.CompilerParams(dimension_semantics=semantics, vmem_limit_bytes=VMEM_LIMIT)


def _rms_normalize(x, gain):
    inv = lax.rsqrt(jnp.mean(x * x, axis=-1, keepdims=True) + RMS_EPS)
    return x * inv * gain


def _softplus(z):
    return jnp.maximum(z, 0.0) + jnp.log(1.0 + jnp.exp(-jnp.abs(z)))


def _norm_proj_body(x_ref, g_ref, w_ref, cs_ref, bd_ref, o_ref, h_scr, *, n_norm):
    j = pl.program_id(1)

    @pl.when(j == 0)
    def _():
        h_scr[...] = _rms_normalize(x_ref[...], g_ref[...]).astype(BF16)

    acc = jnp.dot(h_scr[...], w_ref[...], preferred_element_type=F32)

    def plain():
        o_ref[...] = (acc * cs_ref[...]).astype(o_ref.dtype)

    def head_normed():
        for c in range(COL_TILE // NORM_CHUNK):
            sl = slice(c * NORM_CHUNK, (c + 1) * NORM_CHUNK)
            a = acc[:, sl]
            ss = jnp.dot((a * a).astype(BF16), bd_ref[...], preferred_element_type=F32)
            inv = lax.rsqrt(ss * (1.0 / HEAD_DIM) + RMS_EPS)
            o_ref[:, sl] = (a * inv * cs_ref[:, sl]).astype(o_ref.dtype)

    if n_norm == 0:
        plain()
    else:
        pl.when(j < n_norm)(head_normed)
        pl.when(j >= n_norm)(plain)


def _norm_proj(x, gain, w, layer, col_scale, n_norm):
    t, d = x.shape
    n = w.shape[2]
    head_id = jnp.arange(NORM_CHUNK) // HEAD_DIM
    block_diag = (head_id[:, None] == head_id[None, :]).astype(BF16)
    return pl.pallas_call(
        functools.partial(_norm_proj_body, n_norm=n_norm),
        out_shape=jax.ShapeDtypeStruct((t, n), BF16),
        grid=(t // ROW_TILE, n // COL_TILE),
        in_specs=[
            pl.BlockSpec((ROW_TILE, d), lambda i, j: (i, 0)),
            pl.BlockSpec((1, d), lambda i, j: (0, 0)),
            pl.BlockSpec((None, d, COL_TILE), lambda i, j: (layer, 0, j)),
            pl.BlockSpec((1, COL_TILE), lambda i, j: (0, j)),
            pl.BlockSpec((NORM_CHUNK, NORM_CHUNK), lambda i, j: (0, 0)),
        ],
        out_specs=pl.BlockSpec((ROW_TILE, COL_TILE), lambda i, j: (i, j)),
        scratch_shapes=[pltpu.VMEM((ROW_TILE, d), BF16)],
        compiler_params=_params("parallel", "arbitrary"),
        name="norm_proj",
    )(x, gain.reshape(1, d), w, col_scale.reshape(1, n).astype(F32), block_diag)


def _out_proj_body(x_ref, o_ref, w_ref, y_ref):
    y_ref[...] = x_ref[...] + jnp.dot(o_ref[...], w_ref[...], preferred_element_type=F32)


def _out_proj_residual(x, o, w, layer):
    t, d = x.shape
    return pl.pallas_call(
        _out_proj_body,
        out_shape=jax.ShapeDtypeStruct((t, d), F32),
        grid=(t // ROW_TILE,),
        in_specs=[
            pl.BlockSpec((ROW_TILE, d), lambda i: (i, 0)),
            pl.BlockSpec((ROW_TILE, d), lambda i: (i, 0)),
            pl.BlockSpec((None, d, d), lambda i: (layer, 0, 0)),
        ],
        out_specs=pl.BlockSpec((ROW_TILE, d), lambda i: (i, 0)),
        compiler_params=_params("parallel"),
        name="out_proj",
    )(x, o, w)


def _swiglu_hidden(h, wg, wu):
    g = jnp.dot(h, wg, preferred_element_type=F32)
    u = jnp.dot(h, wu, preferred_element_type=F32)
    return g * (1.0 / (1.0 + jnp.exp(-g))) * u


def _ffn_body(x_ref, g_ref, wg_ref, wu_ref, wd_ref, y_ref, h_scr):
    f = pl.program_id(1)

    @pl.when(f == 0)
    def _():
        x = x_ref[...]
        h_scr[...] = _rms_normalize(x, g_ref[...]).astype(BF16)
        y_ref[...] = x

    a = _swiglu_hidden(h_scr[...], wg_ref[...], wu_ref[...]).astype(BF16)
    y_ref[...] += jnp.dot(a, wd_ref[...], preferred_element_type=F32)


def _ffn_residual(x, gain, w_gate, w_up, w_down, layer):
    t, d = x.shape
    ff = w_gate.shape[2]
    return pl.pallas_call(
        _ffn_body,
        out_shape=jax.ShapeDtypeStruct((t, d), F32),
        grid=(t // ROW_TILE, ff // FF_TILE),
        in_specs=[
            pl.BlockSpec((ROW_TILE, d), lambda i, f: (i, 0)),
            pl.BlockSpec((1, d), lambda i, f: (0, 0)),
            pl.BlockSpec((None, d, FF_TILE), lambda i, f: (layer, 0, f)),
            pl.BlockSpec((None, d, FF_TILE), lambda i, f: (layer, 0, f)),
            pl.BlockSpec((None, FF_TILE, d), lambda i, f: (layer, f, 0)),
        ],
        out_specs=pl.BlockSpec((ROW_TILE, d), lambda i, f: (i, 0)),
        scratch_shapes=[pltpu.VMEM((ROW_TILE, d), BF16)],
        compiler_params=_params("parallel", "arbitrary"),
        name="ffn",
    )(x, gain.reshape(1, d), w_gate, w_up, w_down)


def _pack_bf16_pairs(x):
    half = x.shape[1] // 2
    bits = pltpu.bitcast(x.astype(BF16).astype(F32), jnp.int32)
    return bits[:, :half] | lax.shift_right_logical(bits[:, half:], jnp.int32(16))


def _unpack_bf16_pairs(p):
    left = pltpu.bitcast(p & jnp.int32(-65536), F32)
    right = pltpu.bitcast(lax.shift_left(p, jnp.int32(16)), F32)
    return jnp.concatenate([left, right], axis=1).astype(BF16)


def _router_body(x_ref, g_ref, r_ref, tri_ref, gates_ref, rank_ref, h_ref, count_scr):
    @pl.when(pl.program_id(0) == 0)
    def _():
        count_scr[...] = jnp.zeros_like(count_scr)

    h = _rms_normalize(x_ref[...], g_ref[...])
    h_ref[...] = _pack_bf16_pairs(h)
    logits = jnp.dot(h, r_ref[...], preferred_element_type=F32, precision=lax.Precision.HIGHEST)
    lane = lax.broadcasted_iota(jnp.int32, logits.shape, 1).astype(F32)
    logits = jnp.where(lane < N_EXPERTS, logits, -jnp.inf)
    m1 = jnp.max(logits, axis=-1, keepdims=True)
    i1 = jnp.min(jnp.where(logits == m1, lane, float(LANES)), axis=-1, keepdims=True)
    rest = jnp.where(lane == i1, -jnp.inf, logits)
    m2 = jnp.max(rest, axis=-1, keepdims=True)
    i2 = jnp.min(jnp.where(rest == m2, lane, float(LANES)), axis=-1, keepdims=True)
    e = jnp.exp(m2 - m1)
    g1 = 1.0 / (1.0 + e)
    gates_ref[...] = jnp.where(lane == i1, g1, 0.0) + jnp.where(lane == i2, e * g1, 0.0)
    chosen = jnp.where((lane == i1) | (lane == i2), 1.0, 0.0)
    inclusive = jnp.dot(tri_ref[...], chosen.astype(BF16), preferred_element_type=F32)
    rank_ref[...] = jnp.where(chosen > 0.0, inclusive - 1.0 + count_scr[...], -1.0)
    count_scr[...] += inclusive[ROW_TILE - 1:ROW_TILE, :]


def _router(x, gain, router):
    t, d = x.shape
    r = jnp.zeros((d, LANES), F32).at[:, :N_EXPERTS].set(router)
    idx = jnp.arange(ROW_TILE)
    tri = (idx[:, None] >= idx[None, :]).astype(BF16)
    return pl.pallas_call(
        _router_body,
        out_shape=(jax.ShapeDtypeStruct((t, LANES), F32), jax.ShapeDtypeStruct((t, LANES), F32),
                   jax.ShapeDtypeStruct((t, d // 2), jnp.int32)),
        grid=(t // ROW_TILE,),
        in_specs=[
            pl.BlockSpec((ROW_TILE, d), lambda i: (i, 0)),
            pl.BlockSpec((1, d), lambda i: (0, 0)),
            pl.BlockSpec((d, LANES), lambda i: (0, 0)),
            pl.BlockSpec((ROW_TILE, ROW_TILE), lambda i: (0, 0)),
        ],
        out_specs=(pl.BlockSpec((ROW_TILE, LANES), lambda i: (i, 0)),
                   pl.BlockSpec((ROW_TILE, LANES), lambda i: (i, 0)),
                   pl.BlockSpec((ROW_TILE, d // 2), lambda i: (i, 0))),
        scratch_shapes=[pltpu.VMEM((1, LANES), F32)],
        compiler_params=_params("arbitrary"),
        name="router",
    )(x, gain.reshape(1, d), r, tri)


def _sc_mesh():
    return plsc.VectorSubcoreMesh(core_axis_name="core", subcore_axis_name="subcore",
                                  num_cores=SC_CORES, num_subcores=SC_SUBCORES)


def _sc_worker_base(per_worker):
    return (lax.axis_index("subcore") * SC_CORES + lax.axis_index("core")) * per_worker


def _sc_row_gather(table, idx):
    width = table.shape[1]
    n = idx.shape[0]
    per_worker = n // (SC_CORES * SC_SUBCORES)
    n_chunks = per_worker // SC_CHUNK_ROWS
    assert n == n_chunks * SC_CHUNK_ROWS * SC_CORES * SC_SUBCORES and n_chunks % 2 == 0

    @functools.partial(
        pl.kernel, mesh=_sc_mesh(), out_type=jax.ShapeDtypeStruct((n, width), table.dtype),
        scratch_types=[pltpu.VMEM((SC_CHUNK_ROWS,), jnp.int32), pltpu.VMEM((SC_CHUNK_ROWS,), jnp.int32),
                       pltpu.VMEM((SC_CHUNK_ROWS, width), table.dtype),
                       pltpu.VMEM((SC_CHUNK_ROWS, width), table.dtype),
                       pltpu.SemaphoreType.DMA, pltpu.SemaphoreType.DMA],
        name="sc_row_gather")
    def gather(table_hbm, idx_hbm, out_hbm, idx_a, idx_b, rows_a, rows_b, sem_a, sem_b):
        base = _sc_worker_base(per_worker)
        bufs = ((idx_a, rows_a, sem_a), (idx_b, rows_b, sem_b))

        def rows_of(c):
            return pl.ds(pl.multiple_of(base + c * SC_CHUNK_ROWS, SC_CHUNK_ROWS), SC_CHUNK_ROWS)

        def fetch(c, buf):
            idx_v, rows_v, sem = buf
            pltpu.sync_copy(idx_hbm.at[rows_of(c)], idx_v)
            return pltpu.make_async_copy(table_hbm.at[idx_v], rows_v, sem)

        fetch(0, bufs[0]).start()

        @pl.loop(0, n_chunks, step=2)
        def _(c):
            for b in range(2):
                idx_v, rows_v, sem = bufs[b]
                pltpu.make_async_copy(table_hbm.at[idx_v], rows_v, sem).wait()

                @pl.when(c + b + 1 < n_chunks)
                def _():
                    fetch(c + b + 1, bufs[1 - b]).start()

                pltpu.sync_copy(rows_v, out_hbm.at[rows_of(c + b)])

    return gather(table, idx)


def _sc_row_scatter_pair(rows, idx_lo, idx_hi, n_out):
    n, width = rows.shape
    per_worker = n // (SC_CORES * SC_SUBCORES)
    n_chunks = per_worker // SC_CHUNK_ROWS
    assert n == n_chunks * SC_CHUNK_ROWS * SC_CORES * SC_SUBCORES and n_chunks % 2 == 0
    index_scratch = pltpu.VMEM((SC_CHUNK_ROWS,), jnp.int32)
    rows_scratch = pltpu.VMEM((SC_CHUNK_ROWS, width), rows.dtype)

    @functools.partial(
        pl.kernel, mesh=_sc_mesh(), out_type=jax.ShapeDtypeStruct((n_out, width), rows.dtype),
        scratch_types=[index_scratch, index_scratch, rows_scratch, rows_scratch,
                       pltpu.SemaphoreType.DMA, pltpu.SemaphoreType.DMA],
        name="sc_row_scatter")
    def scatter(rows_hbm, lo_hbm, hi_hbm, out_hbm, lo_v, hi_v, rows_a, rows_b, sem_a, sem_b):
        base = _sc_worker_base(per_worker)
        bufs = ((rows_a, sem_a), (rows_b, sem_b))

        def rows_of(c):
            return pl.ds(pl.multiple_of(base + c * SC_CHUNK_ROWS, SC_CHUNK_ROWS), SC_CHUNK_ROWS)

        def load(c, buf):
            rows_v, sem = buf
            return pltpu.make_async_copy(rows_hbm.at[rows_of(c)], rows_v, sem)

        load(0, bufs[0]).start()

        @pl.loop(0, n_chunks, step=2)
        def _(c):
            for b in range(2):
                rows_v, _ = bufs[b]
                load(c + b, bufs[b]).wait()

                @pl.when(c + b + 1 < n_chunks)
                def _():
                    load(c + b + 1, bufs[1 - b]).start()

                pltpu.sync_copy(lo_hbm.at[rows_of(c + b)], lo_v)
                pltpu.sync_copy(hi_hbm.at[rows_of(c + b)], hi_v)
                pltpu.sync_copy(rows_v, out_hbm.at[lo_v])
                pltpu.sync_copy(rows_v, out_hbm.at[hi_v])

    return scatter(rows, idx_lo, idx_hi)


def _expert_ffn_body(te_ref, nu_ref, nv_ref, h_ref, wg_ref, wu_ref, wd_ref, y_ref, acc_scr):
    i = pl.program_id(0)
    f = pl.program_id(1)

    @pl.when(i < nu_ref[0])
    def _():
        row = lax.broadcasted_iota(jnp.int32, h_ref.shape, 0)
        packed = jnp.where(row < nv_ref[i], h_ref[...], 0)
        a = _swiglu_hidden(_unpack_bf16_pairs(packed), wg_ref[...], wu_ref[...]).astype(BF16)
        y = jnp.dot(a, wd_ref[...], preferred_element_type=F32)

        @pl.when(f == 0)
        def _():
            acc_scr[...] = y

        @pl.when(f > 0)
        def _():
            acc_scr[...] += y

        @pl.when(f == pl.num_programs(1) - 1)
        def _():
            y_ref[...] = _pack_bf16_pairs(acc_scr[...])


def _expert_ffn(h_sorted, tile_expert, n_used, tile_valid, w_gate, w_up, w_down, layer):
    rows, half = h_sorted.shape
    d = 2 * half
    ff = w_gate.shape[3]
    n_f = ff // MOE_FF_TILE

    def row_map(i, f, te, nu, nv):
        return (jnp.minimum(i, nu[0] - 1), 0)

    def col_step(i, f, nu):
        return jnp.where(i < nu[0], f, n_f - 1)

    grid_spec = pltpu.PrefetchScalarGridSpec(
        num_scalar_prefetch=3,
        grid=(rows // MOE_ROW_TILE, n_f),
        in_specs=[
            pl.BlockSpec((MOE_ROW_TILE, half), row_map),
            pl.BlockSpec((None, None, d, MOE_FF_TILE),
                         lambda i, f, te, nu, nv: (layer, te[i], 0, col_step(i, f, nu))),
            pl.BlockSpec((None, None, d, MOE_FF_TILE),
                         lambda i, f, te, nu, nv: (layer, te[i], 0, col_step(i, f, nu))),
            pl.BlockSpec((None, None, MOE_FF_TILE, d),
                         lambda i, f, te, nu, nv: (layer, te[i], col_step(i, f, nu), 0)),
        ],
        out_specs=pl.BlockSpec((MOE_ROW_TILE, half), row_map),
        scratch_shapes=[pltpu.VMEM((MOE_ROW_TILE, d), F32)],
    )
    return pl.pallas_call(
        _expert_ffn_body,
        out_shape=jax.ShapeDtypeStruct((rows, half), jnp.int32),
        grid_spec=grid_spec,
        compiler_params=_params("arbitrary", "arbitrary"),
        name="expert_ffn",
    )(tile_expert, n_used, tile_valid, h_sorted, w_gate, w_up, w_down)


def _combine_body(x_ref, y_ref, g_ref, o_ref):
    g = g_ref[...]
    first = _unpack_bf16_pairs(y_ref[0]).astype(F32) * g[:, 0:1]
    second = _unpack_bf16_pairs(y_ref[1]).astype(F32) * g[:, 1:2]
    o_ref[...] = x_ref[...] + first + second


def _combine_residual(x, y_pairs, gates2):
    t, d = x.shape
    return pl.pallas_call(
        _combine_body,
        out_shape=jax.ShapeDtypeStruct((t, d), F32),
        grid=(t // ROW_TILE,),
        in_specs=[
            pl.BlockSpec((ROW_TILE, d), lambda i: (i, 0)),
            pl.BlockSpec((2, ROW_TILE, d // 2), lambda i: (0, i, 0)),
            pl.BlockSpec((ROW_TILE, 2), lambda i: (i, 0)),
        ],
        out_specs=pl.BlockSpec((ROW_TILE, d), lambda i: (i, 0)),
        compiler_params=_params("parallel"),
        name="moe_combine",
    )(x, y_pairs, gates2)


def _moe_residual(x, gain, router, w_gate, w_up, w_down, layer):
    t, d = x.shape
    gates, rank, h_packed = _router(x, gain, router)
    rank8 = rank[:, :N_EXPERTS].astype(jnp.int32)
    chosen = rank8 >= 0
    counts = jnp.sum(chosen, axis=0, dtype=jnp.int32)
    padded = (counts + MOE_ROW_TILE - 1) // MOE_ROW_TILE * MOE_ROW_TILE
    ends = jnp.cumsum(padded)
    starts = ends - padded
    pos = starts[None, :] + rank8
    max_rows = 2 * t + N_EXPERTS * MOE_ROW_TILE
    pos_lo = jnp.min(jnp.where(chosen, pos, max_rows), axis=1)
    pos_hi = jnp.max(jnp.where(chosen, pos, -1), axis=1)
    gates8 = gates[:, :N_EXPERTS]
    gates2 = jnp.stack([jnp.sum(jnp.where(chosen & (pos == pos_lo[:, None]), gates8, 0.0), axis=1),
                        jnp.sum(jnp.where(chosen & (pos == pos_hi[:, None]), gates8, 0.0), axis=1)], axis=1)
    n_tiles = max_rows // MOE_ROW_TILE
    n_used = (ends[-1] // MOE_ROW_TILE).astype(jnp.int32)
    tile_start = jnp.minimum(jnp.arange(n_tiles, dtype=jnp.int32), n_used - 1) * MOE_ROW_TILE
    tile_expert = jnp.sum(tile_start[:, None] >= ends[None, :], axis=1, dtype=jnp.int32)
    tile_valid = jnp.clip((starts + counts)[tile_expert] - tile_start, 0, MOE_ROW_TILE).astype(jnp.int32)

    h_sorted = _sc_row_scatter_pair(h_packed, pos_lo, pos_hi, max_rows)
    y_sorted = _expert_ffn(h_sorted, tile_expert, n_used.reshape(1), tile_valid, w_gate, w_up, w_down, layer)
    y_pairs = _sc_row_gather(y_sorted, jnp.concatenate([pos_lo, pos_hi])).reshape(2, t, d // 2)
    return _combine_residual(x, y_pairs, gates2)


def _split_head_pair(q):
    is_first = lax.broadcasted_iota(jnp.int32, (1, LANES), 1) < HEAD_DIM
    zero = jnp.zeros_like(q)
    return jnp.where(is_first, q, zero), jnp.where(is_first, zero, q)


def _merge_head_pair(first, second):
    is_first = lax.broadcasted_iota(jnp.int32, (1, LANES), 1) < HEAD_DIM
    return jnp.where(is_first, first, second)


def _qk(q, k):
    return lax.dot_general(q, k, (((1,), (1,)), ((), ())), preferred_element_type=F32)


def _transpose_values(v_ref, vt_scr):
    for c in range(v_ref.shape[0] // ATT_BLOCK):
        rows = slice(c * ATT_BLOCK, (c + 1) * ATT_BLOCK)
        vt_scr[:, rows] = v_ref[rows, :].astype(F32).T.astype(vt_scr.dtype)


def _sb_body(q_ref, k_ref, v_ref, u_ref, o_ref, vt_scr, acc_scr, carry_scr, z_scr, keep_scr, sum_scr):
    qi = pl.program_id(2)

    @pl.when(qi == 0)
    def _():
        _transpose_values(v_ref, vt_scr)

    upper = u_ref[...]
    acc_scr[...] = jnp.zeros_like(acc_scr)
    carry_scr[...] = jnp.zeros_like(carry_scr)
    key = lax.broadcasted_iota(jnp.int32, (ATT_BLOCK, ATT_BLOCK), 0)
    query = lax.broadcasted_iota(jnp.int32, (ATT_BLOCK, ATT_BLOCK), 1)
    strict = key < query
    qs = [_split_head_pair(q_ref[j * ATT_BLOCK:(j + 1) * ATT_BLOCK, :]) for j in range(ATT_K_PER_Q)]

    def process(j, kb, keep):
        start = pl.multiple_of(kb * ATT_BLOCK, ATT_BLOCK)
        k = k_ref[pl.ds(start, ATT_BLOCK), :]
        for hh in range(2):
            z = _qk(k, qs[j][hh])
            log_beta = z - _softplus(z)
            log_keep = log_beta - z
            if keep is not None:
                log_keep = jnp.where(keep, log_keep, 0.0)
            remain = jnp.dot(upper, log_keep.astype(BF16), preferred_element_type=F32)
            w = jnp.exp(log_beta + remain + carry_scr[j, hh])
            if keep is not None:
                w = jnp.where(keep, w, 0.0)
            carry_scr[j, hh] += jnp.sum(log_keep, axis=0, keepdims=True)
            vt = vt_scr[pl.ds(hh * HEAD_DIM, HEAD_DIM), pl.ds(start, ATT_BLOCK)]
            acc_scr[j, hh] += jnp.dot(vt, w.astype(BF16), preferred_element_type=F32)

    units = []
    for j in range(ATT_K_PER_Q):
        g = qi * ATT_K_PER_Q + j
        has_previous = None if j > 0 else jnp.broadcast_to(g > 0, strict.shape)
        for kb, keep in ((g, strict), (jnp.maximum(g - 1, 0), has_previous)):
            for hh in range(2):
                units.append((j, hh, pl.multiple_of(kb * ATT_BLOCK, ATT_BLOCK), keep))

    for u, (j, hh, start, keep) in enumerate(units):
        z_scr[u] = _qk(k_ref[pl.ds(start, ATT_BLOCK), :], qs[j][hh])

    for u, (j, hh, start, keep) in enumerate(units):
        z = z_scr[u]
        log_beta = z - _softplus(z)
        log_keep = log_beta - z
        if keep is not None:
            log_keep = jnp.where(keep, log_keep, 0.0)
        z_scr[u] = log_beta
        keep_scr[u] = log_keep.astype(BF16)
        sum_scr[u] = jnp.sum(log_keep, axis=0, keepdims=True)

    for u in range(len(units)):
        z_scr[u] += jnp.dot(upper, keep_scr[u], preferred_element_type=F32)

    for u, (j, hh, start, keep) in enumerate(units):
        diagonal = u % 4 < 2
        log_w = z_scr[u] if diagonal else z_scr[u] + sum_scr[u - 2]
        w = jnp.exp(log_w)
        if keep is not None:
            w = jnp.where(keep, w, 0.0)
        keep_scr[u] = w.astype(BF16)

    for u, (j, hh, start, keep) in enumerate(units):
        vt = vt_scr[pl.ds(hh * HEAD_DIM, HEAD_DIM), pl.ds(start, ATT_BLOCK)]
        acc_scr[j, hh] += jnp.dot(vt, keep_scr[u], preferred_element_type=F32)
        carry_scr[j, hh] += sum_scr[u]

    for j in range(ATT_K_PER_Q):
        def more(kb, j=j):
            return (kb >= 0) & (jnp.max(carry_scr[j]) > SB_UNDERFLOW_LOG)

        def step(kb, j=j):
            process(j, kb, None)
            return kb - 1

        lax.while_loop(more, step, qi * ATT_K_PER_Q + j - 2)

    for j in range(ATT_K_PER_Q):
        out_t = jnp.concatenate([acc_scr[j, 0], acc_scr[j, 1]], axis=0)
        o_ref[j * ATT_BLOCK:(j + 1) * ATT_BLOCK, :] = out_t.T.astype(o_ref.dtype)


def _sb_attention(proj, batch, seq):
    t = proj.shape[0]
    nq = seq // ATT_Q_BLOCK
    idx = jnp.arange(ATT_BLOCK)
    upper = (idx[None, :] > idx[:, None]).astype(BF16)
    return pl.pallas_call(
        _sb_body,
        out_shape=jax.ShapeDtypeStruct((t, D_MODEL), BF16),
        grid=(batch, HEAD_PAIRS, nq),
        in_specs=[
            pl.BlockSpec((ATT_Q_BLOCK, LANES), lambda b, p, i: (b * nq + i, p)),
            pl.BlockSpec((seq, LANES), lambda b, p, i: (b, HEAD_PAIRS + p)),
            pl.BlockSpec((seq, LANES), lambda b, p, i: (b, 2 * HEAD_PAIRS + p)),
            pl.BlockSpec((ATT_BLOCK, ATT_BLOCK), lambda b, p, i: (0, 0)),
        ],
        out_specs=pl.BlockSpec((ATT_Q_BLOCK, LANES), lambda b, p, i: (b * nq + i, p)),
        scratch_shapes=[
            pltpu.VMEM((LANES, seq), BF16),
            pltpu.VMEM((ATT_K_PER_Q, 2, HEAD_DIM, ATT_BLOCK), F32),
            pltpu.VMEM((ATT_K_PER_Q, 2, 1, ATT_BLOCK), F32),
            pltpu.VMEM((4 * ATT_K_PER_Q, ATT_BLOCK, ATT_BLOCK), F32),
            pltpu.VMEM((4 * ATT_K_PER_Q, ATT_BLOCK, ATT_BLOCK), BF16),
            pltpu.VMEM((4 * ATT_K_PER_Q, 1, ATT_BLOCK), F32),
        ],
        compiler_params=_params("parallel", "parallel", "arbitrary"),
        name="sb_attention",
    )(proj, proj, proj, upper)


def _fox_gate_body(x_ref, g_ref, w_ref, b_ref, tri_ref, c_ref):
    h = _rms_normalize(x_ref[...], g_ref[...])
    logits = jnp.dot(h, w_ref[...], preferred_element_type=F32, precision=lax.Precision.HIGHEST)
    log_f = -_softplus(-(logits + b_ref[...]))
    seq = log_f.shape[0]
    carry = jnp.zeros((1, LANES), F32)
    for blk in range(seq // ATT_BLOCK):
        rows = slice(blk * ATT_BLOCK, (blk + 1) * ATT_BLOCK)
        c = jnp.dot(tri_ref[...], log_f[rows], preferred_element_type=F32,
                    precision=lax.Precision.HIGHEST) + carry
        c_ref[rows, :] = c
        carry = c[ATT_BLOCK - 1:ATT_BLOCK, :]


def _fox_cum_log_forget(x, gain, w_gate, b_gate, batch, seq):
    t, d = x.shape
    w = jnp.zeros((d, LANES), F32).at[:, :N_HEADS].set(w_gate)
    b = jnp.zeros((1, LANES), F32).at[0, :N_HEADS].set(b_gate)
    idx = jnp.arange(ATT_BLOCK)
    tri = (idx[:, None] >= idx[None, :]).astype(F32)
    return pl.pallas_call(
        _fox_gate_body,
        out_shape=jax.ShapeDtypeStruct((t, LANES), F32),
        grid=(batch,),
        in_specs=[
            pl.BlockSpec((seq, d), lambda i: (i, 0)),
            pl.BlockSpec((1, d), lambda i: (0, 0)),
            pl.BlockSpec((d, LANES), lambda i: (0, 0)),
            pl.BlockSpec((1, LANES), lambda i: (0, 0)),
            pl.BlockSpec((ATT_BLOCK, ATT_BLOCK), lambda i: (0, 0)),
        ],
        out_specs=pl.BlockSpec((seq, LANES), lambda i: (i, 0)),
        compiler_params=_params("parallel"),
        name="fox_gate",
    )(x, gain.reshape(1, d), w, b, tri)


def _fox_body(q_ref, k_ref, v_ref, cq_ref, ck_ref, o_ref, vt_scr, ckb_scr, a_scr, p_scr):
    qi = pl.program_id(2)

    @pl.when(qi == 0)
    def _():
        _transpose_values(v_ref, vt_scr)
        for hh in range(2):
            ckb_scr[hh] = jnp.broadcast_to(ck_ref[0, hh], ckb_scr.shape[1:])

    qs = _split_head_pair(q_ref[...])
    key = lax.broadcasted_iota(jnp.int32, (ATT_BLOCK, ATT_Q_BLOCK), 0)
    query = lax.broadcasted_iota(jnp.int32, (ATT_BLOCK, ATT_Q_BLOCK), 1)
    row_shape = (1, ATT_Q_BLOCK)

    def scores(kb, tops, diagonal):
        start = pl.multiple_of(kb * ATT_BLOCK, ATT_BLOCK)
        k = k_ref[pl.ds(start, ATT_BLOCK), :]
        if diagonal:
            causal = key + (kb * ATT_BLOCK - qi * ATT_Q_BLOCK) <= query
        new_tops = []
        for hh in range(2):
            ck = ckb_scr[hh, pl.ds(start, ATT_BLOCK), :]
            a = _qk(k, qs[hh]) - jnp.concatenate([ck] * (ATT_Q_BLOCK // LANES), axis=1)
            if diagonal:
                a = jnp.where(causal, a, NEG_INF)
            a_scr[hh, kb] = a
            new_tops.append(jnp.maximum(tops[hh], jnp.max(a, axis=0, keepdims=True)))
        return tuple(new_tops)

    def scores_step(it, tops):
        for r in range(ATT_K_PER_Q):
            tops = scores(it * ATT_K_PER_Q + r, tops, False)
        return tops

    tops = lax.fori_loop(0, qi, scores_step, (jnp.full(row_shape, NEG_INF, F32),) * 2)
    for r in range(ATT_K_PER_Q):
        tops = scores(qi * ATT_K_PER_Q + r, tops, True)
    shifts = tuple(cq_ref[0, hh] - (tops[hh] + cq_ref[0, hh]) for hh in range(2))

    def weights_step(it, sums):
        sums = list(sums)
        for r in range(ATT_K_PER_Q):
            kb = it * ATT_K_PER_Q + r
            for hh in range(2):
                p = jnp.exp(a_scr[hh, kb] + shifts[hh])
                p_scr[hh, kb] = p.astype(BF16)
                sums[hh] = sums[hh] + jnp.sum(p, axis=0, keepdims=True)
        return tuple(sums)

    sums = lax.fori_loop(0, qi + 1, weights_step, (jnp.zeros(row_shape, F32),) * 2)

    def values_step(it, accs):
        accs = list(accs)
        for r in range(ATT_K_PER_Q):
            kb = it * ATT_K_PER_Q + r
            start = pl.multiple_of(kb * ATT_BLOCK, ATT_BLOCK)
            for hh in range(2):
                vt = vt_scr[pl.ds(hh * HEAD_DIM, HEAD_DIM), pl.ds(start, ATT_BLOCK)]
                accs[hh] = accs[hh] + jnp.dot(vt, p_scr[hh, kb], preferred_element_type=F32)
        return tuple(accs)

    accs = lax.fori_loop(0, qi + 1, values_step, (jnp.zeros((HEAD_DIM, ATT_Q_BLOCK), F32),) * 2)
    out_t = jnp.concatenate([accs[0] / sums[0], accs[1] / sums[1]], axis=0)
    o_ref[...] = out_t.T.astype(o_ref.dtype)


def _fox_attention(proj, cum, batch, seq):
    t = proj.shape[0]
    nq = seq // ATT_Q_BLOCK
    cum_h = cum[:, :N_HEADS].reshape(batch, seq, N_HEADS).transpose(0, 2, 1)
    cum_q = cum_h.reshape(batch, N_HEADS, 1, seq)
    cum_k = cum_h.reshape(batch, N_HEADS, seq, 1)
    return pl.pallas_call(
        _fox_body,
        out_shape=jax.ShapeDtypeStruct((t, D_MODEL), BF16),
        grid=(batch, HEAD_PAIRS, nq),
        in_specs=[
            pl.BlockSpec((ATT_Q_BLOCK, LANES), lambda b, p, i: (b * nq + i, p)),
            pl.BlockSpec((seq, LANES), lambda b, p, i: (b, HEAD_PAIRS + p)),
            pl.BlockSpec((seq, LANES), lambda b, p, i: (b, 2 * HEAD_PAIRS + p)),
            pl.BlockSpec((1, 2, 1, ATT_Q_BLOCK), lambda b, p, i: (b, p, 0, i)),
            pl.BlockSpec((1, 2, seq, 1), lambda b, p, i: (b, p, 0, 0)),
        ],
        out_specs=pl.BlockSpec((ATT_Q_BLOCK, LANES), lambda b, p, i: (b * nq + i, p)),
        scratch_shapes=[
            pltpu.VMEM((LANES, seq), BF16),
            pltpu.VMEM((2, seq, LANES), F32),
            pltpu.VMEM((2, seq // ATT_BLOCK, ATT_BLOCK, ATT_Q_BLOCK), F32),
            pltpu.VMEM((2, seq // ATT_BLOCK, ATT_BLOCK, ATT_Q_BLOCK), BF16),
        ],
        compiler_params=_params("parallel", "parallel", "arbitrary"),
        name="fox_attention",
    )(proj, proj, proj, cum_q, cum_k)


def _band_body(q_ref, k_ref, v_ref, bias_ref, o_ref, s_ref, vt_scr, logit_scr, p_scr, inv_scr, *, sub_len):
    seq = v_ref.shape[0]
    n_blocks = seq // DIL_BLOCK
    _transpose_values(v_ref, vt_scr)

    def key_rows(n):
        first = (n * DIL_BLOCK) % sub_len == 0
        return first, slice((n if first else n - 1) * DIL_BLOCK, (n + 1) * DIL_BLOCK)

    for n in range(n_blocks):
        first, k_rows = key_rows(n)
        qs = _split_head_pair(q_ref[n * DIL_BLOCK:(n + 1) * DIL_BLOCK, :])
        k = k_ref[k_rows, :]
        for hh in range(2):
            bias = bias_ref[hh, DIL_BLOCK:, :] if first else bias_ref[hh]
            logit_scr[2 * n + hh, :k.shape[0], :] = _qk(k, qs[hh]) + bias

    for n in range(n_blocks):
        first, k_rows = key_rows(n)
        n_keys = k_rows.stop - k_rows.start
        for hh in range(2):
            logits = logit_scr[2 * n + hh, :n_keys, :]
            m = jnp.max(logits, axis=0, keepdims=True)
            p = jnp.exp(logits - m)
            l = jnp.sum(p, axis=0, keepdims=True)
            p_scr[2 * n + hh, :n_keys, :] = p.astype(BF16)
            inv_scr[2 * n + hh] = 1.0 / l
            s_ref[0, 0, hh:hh + 1, n * DIL_BLOCK:(n + 1) * DIL_BLOCK] = m + jnp.log(l)

    for n in range(n_blocks):
        first, k_rows = key_rows(n)
        n_keys = k_rows.stop - k_rows.start
        outs = []
        for hh in range(2):
            vt = vt_scr[hh * HEAD_DIM:(hh + 1) * HEAD_DIM, k_rows]
            o = jnp.dot(vt, p_scr[2 * n + hh, :n_keys, :], preferred_element_type=F32)
            outs.append(o * inv_scr[2 * n + hh])
        o_ref[n * DIL_BLOCK:(n + 1) * DIL_BLOCK, :] = jnp.concatenate(outs, axis=0).T.astype(o_ref.dtype)


def _band_attention(qk, v, bias_t, batch, seq, sub_len):
    t = v.shape[0]
    return pl.pallas_call(
        functools.partial(_band_body, sub_len=sub_len),
        out_shape=(jax.ShapeDtypeStruct((t, D_MODEL), BF16),
                   jax.ShapeDtypeStruct((batch, HEAD_PAIRS, 2, seq), F32)),
        grid=(batch, HEAD_PAIRS),
        in_specs=[
            pl.BlockSpec((None, seq, LANES), lambda b, p: (0, b, p)),
            pl.BlockSpec((None, seq, LANES), lambda b, p: (1, b, p)),
            pl.BlockSpec((seq, LANES), lambda b, p: (b, p)),
            pl.BlockSpec((2, 2 * DIL_BLOCK, DIL_BLOCK), lambda b, p: (p, 0, 0)),
        ],
        out_specs=(pl.BlockSpec((seq, LANES), lambda b, p: (b, p)),
                   pl.BlockSpec((1, 1, 2, seq), lambda b, p: (b, p, 0, 0))),
        scratch_shapes=[pltpu.VMEM((LANES, seq), BF16),
                        pltpu.VMEM((2 * seq // DIL_BLOCK, 2 * DIL_BLOCK, DIL_BLOCK), F32),
                        pltpu.VMEM((2 * seq // DIL_BLOCK, 2 * DIL_BLOCK, DIL_BLOCK), BF16),
                        pltpu.VMEM((2 * seq // DIL_BLOCK, 1, DIL_BLOCK), F32)],
        compiler_params=_params("parallel", "parallel"),
        name="band_attention",
    )(qk, qk, v, bias_t)


def _softmax_merge_body(o1_ref, o2_ref, o3_ref, s_ref, e_ref, o_ref):
    s = s_ref[...]
    groups = [s, pltpu.roll(s, LANES - N_HEADS, axis=1), pltpu.roll(s, LANES - 2 * N_HEADS, axis=1)]
    top = jnp.maximum(jnp.maximum(groups[0], groups[1]), groups[2])
    weights = [jnp.exp(g - top) for g in groups]
    inv = 1.0 / (weights[0] + weights[1] + weights[2])
    out = jnp.zeros(o_ref.shape, F32)
    for w, part in zip(weights, (o1_ref, o2_ref, o3_ref)):
        spread = jnp.dot((w * inv).astype(BF16), e_ref[...], preferred_element_type=F32)
        out = out + spread * part[...].astype(F32)
    o_ref[...] = out.astype(o_ref.dtype)


def _softmax_merge(outs, s_all):
    t, d = outs[0].shape
    head_of = jnp.arange(d) // HEAD_DIM
    expand = (jnp.arange(LANES)[:, None] == head_of[None, :]).astype(BF16)
    rows = pl.BlockSpec((ROW_TILE, d), lambda i: (i, 0))
    return pl.pallas_call(
        _softmax_merge_body,
        out_shape=jax.ShapeDtypeStruct((t, d), BF16),
        grid=(t // ROW_TILE,),
        in_specs=[rows, rows, rows,
                  pl.BlockSpec((ROW_TILE, LANES), lambda i: (i, 0)),
                  pl.BlockSpec((LANES, d), lambda i: (0, 0))],
        out_specs=rows,
        compiler_params=_params("parallel"),
        name="softmax_merge",
    )(*outs, s_all, expand)


def _t5_causal_bucket(distance):
    max_exact = N_REL_BUCKETS // 2
    d = jnp.maximum(distance, 1).astype(F32)
    log_b = max_exact + (jnp.log(d / max_exact) / math.log(REL_MAX_DISTANCE / max_exact)
                         * (N_REL_BUCKETS - max_exact)).astype(jnp.int32)
    log_b = jnp.minimum(log_b, N_REL_BUCKETS - 1)
    return jnp.where(distance < max_exact, distance, log_b)


def _dilated_bias(rel_bias):
    kj = jnp.arange(2 * DIL_BLOCK, dtype=jnp.int32)
    qi = jnp.arange(DIL_BLOCK, dtype=jnp.int32)
    delta = qi[None, :] + DIL_BLOCK - kj[:, None]
    in_band = (delta >= 0) & (delta <= DIL_SPAN)
    buckets = jnp.stack([_t5_causal_bucket(jnp.maximum(delta, 0) * dil) for _, dil in DILATED_PAIRS])
    one_hot = (buckets[..., None] == jnp.arange(N_REL_BUCKETS)).astype(F32)
    bias = jnp.einsum("gkqb,bh->ghkq", one_hot, rel_bias.astype(F32), precision=lax.Precision.HIGHEST)
    return jnp.where(in_band[None, None], bias, NEG_INF)


def _dilated_proj_body(x_ref, g_ref, w_ref, cs_ref, bd_ref, *rest):
    n_groups = len(DILATED_PAIRS)
    qk_refs, v_refs = rest[:n_groups], rest[n_groups:2 * n_groups]
    h_scr, res_scr = rest[2 * n_groups:]
    j = pl.program_id(1)

    @pl.when(j == 0)
    def _():
        h_scr[...] = _rms_normalize(x_ref[...], g_ref[...]).astype(BF16)

    acc = jnp.dot(h_scr[...], w_ref[...], preferred_element_type=F32)

    def keep(res, first_lane):
        for c in range(res.shape[1] // LANES):
            res_scr[first_lane // LANES + c] = res[:, c * LANES:(c + 1) * LANES]

    @pl.when(j < 2 * n_groups)
    def _():
        for c in range(COL_TILE // NORM_CHUNK):
            sl = slice(c * NORM_CHUNK, (c + 1) * NORM_CHUNK)
            a = acc[:, sl]
            ss = jnp.dot((a * a).astype(BF16), bd_ref[...], preferred_element_type=F32)
            keep(a * lax.rsqrt(ss * (1.0 / HEAD_DIM) + RMS_EPS) * cs_ref[:, sl], c * NORM_CHUNK)

    @pl.when(j == 2 * n_groups)
    def _():
        keep(acc * cs_ref[...], 0)

    def write_classes(ref, dil):
        rows = res_scr.shape[1] // dil
        for c in range(res_scr.shape[0]):
            lanes = slice(c * LANES, (c + 1) * LANES)
            if dil == 1:
                ref[:, lanes] = res_scr[c].astype(ref.dtype)
            else:
                for r in range(dil):
                    ref[r, :, lanes] = res_scr[c, pl.ds(r, rows, stride=dil), :].astype(ref.dtype)

    for g, (_, dil) in enumerate(DILATED_PAIRS):
        pl.when(j // 2 == g)(functools.partial(write_classes, qk_refs[g], dil))
        pl.when(j == 2 * n_groups)(functools.partial(write_classes, v_refs[g], dil))


def _dilated_proj(x, gain, w, layer, col_scale, batch, seq):
    t, d = x.shape
    n = w.shape[2]
    n_groups = len(DILATED_PAIRS)
    tiles_per_seq = seq // DIL_ROW_TILE
    head_id = jnp.arange(NORM_CHUNK) // HEAD_DIM
    block_diag = (head_id[:, None] == head_id[None, :]).astype(BF16)
    out_shape, out_specs = [], []
    for kind in ("qk", "v"):
        for g, (_, dil) in enumerate(DILATED_PAIRS):
            rows = DIL_ROW_TILE // dil
            if kind == "qk":
                shape = (2, batch, dil, seq // dil, d)
                block = (None, None, dil, rows, d)
                index = lambda i, j, g=g: (jnp.clip(j - 2 * g, 0, 1), i // tiles_per_seq, 0, i % tiles_per_seq, 0)
            else:
                shape = (batch, dil, seq // dil, d)
                block = (None, dil, rows, d)
                index = lambda i, j: (i // tiles_per_seq, 0, i % tiles_per_seq, 0)
            if dil == 1:
                block = block[:-3] + (None,) + block[-2:]
            out_shape.append(jax.ShapeDtypeStruct(shape, BF16))
            out_specs.append(pl.BlockSpec(block, index))
    outs = pl.pallas_call(
        _dilated_proj_body,
        out_shape=tuple(out_shape),
        grid=(t // DIL_ROW_TILE, n // COL_TILE),
        in_specs=[
            pl.BlockSpec((DIL_ROW_TILE, d), lambda i, j: (i, 0)),
            pl.BlockSpec((1, d), lambda i, j: (0, 0)),
            pl.BlockSpec((None, d, COL_TILE), lambda i, j: (layer, 0, j)),
            pl.BlockSpec((1, COL_TILE), lambda i, j: (0, j)),
            pl.BlockSpec((NORM_CHUNK, NORM_CHUNK), lambda i, j: (0, 0)),
        ],
        out_specs=tuple(out_specs),
        scratch_shapes=[pltpu.VMEM((DIL_ROW_TILE, d), BF16),
                        pltpu.VMEM((COL_TILE // LANES, DIL_ROW_TILE, LANES), F32)],
        compiler_params=_params("parallel", "arbitrary"),
        name="dilated_proj",
    )(x, gain.reshape(1, d), w, col_scale.reshape(1, n).astype(F32), block_diag)
    qk = [o.reshape(2, t, d) for o in outs[:n_groups]]
    v = [o.reshape(t, d) for o in outs[n_groups:]]
    return qk, v


def _dilated_attention(qk, v, rel_bias, batch, seq):
    t = v[0].shape[0]
    n_groups = len(DILATED_PAIRS)
    bias_t = _dilated_bias(rel_bias)
    outs, stats = [], []
    for g, (_, dil) in enumerate(DILATED_PAIRS):
        sub_len = seq // dil
        o, s = _band_attention(qk[g], v[g], bias_t[g], batch, seq, sub_len)
        o = o.reshape(batch, dil, sub_len, D_MODEL).transpose(0, 2, 1, 3).reshape(t, D_MODEL)
        s = s.reshape(batch, HEAD_PAIRS, 2, dil, sub_len).swapaxes(3, 4)
        outs.append(o)
        stats.append(s.reshape(batch, N_HEADS, seq).transpose(0, 2, 1).reshape(t, N_HEADS))
    stats.append(jnp.zeros((t, LANES - n_groups * N_HEADS), F32))
    return _softmax_merge(outs, jnp.concatenate(stats, axis=1))


def _tile_heads(v):
    return jnp.tile(v.astype(F32), N_HEADS)


def _sb_mixer(x, gain, w_qkv, w_o, layer, batch, seq):
    ones = jnp.ones((D_MODEL,), F32)
    col_scale = jnp.concatenate([ones * QK_SCALE, ones, ones])
    proj = _norm_proj(x, gain, w_qkv, layer, col_scale, 0)
    return _out_proj_residual(x, _sb_attention(proj, batch, seq), w_o, layer)


def _dilated_mixer(x, gain, w_in, q_norm, k_norm, rel_bias, w_o, layer, batch, seq):
    scales = []
    for g in range(len(DILATED_PAIRS)):
        scales += [_tile_heads(q_norm[g]) * QK_SCALE, _tile_heads(k_norm[g])]
    scales.append(jnp.ones((D_MODEL,), F32))
    qk, v = _dilated_proj(x, gain, w_in, layer, jnp.concatenate(scales), batch, seq)
    return _out_proj_residual(x, _dilated_attention(qk, v, rel_bias, batch, seq), w_o, layer)


def _fox_mixer(x, gain, w_qkv, w_gate, b_f, q_norm, k_norm, w_o, layer, batch, seq):
    col_scale = jnp.concatenate([_tile_heads(q_norm) * QK_SCALE, _tile_heads(k_norm),
                                 jnp.ones((D_MODEL,), F32)])
    proj = _norm_proj(x, gain, w_qkv, layer, col_scale, 2)
    cum = _fox_cum_log_forget(x, gain, w_gate, b_f, batch, seq)
    return _out_proj_residual(x, _fox_attention(proj, cum, batch, seq), w_o, layer)


def kernel(x, sb_w_qkv, sb_w_o, dil_w_in, dil_q_norm, dil_k_norm, dil_w_o, fox_w_in, fox_b_f,
           fox_q_norm, fox_k_norm, fox_w_o, rel_bias, attn_norm, ffn_norm, mlp_w_gate, mlp_w_up,
           mlp_w_down, moe_router, moe_w_gate, moe_w_up, moe_w_down):
    batch, seq, d = x.shape
    depth = attn_norm.shape[0]
    sb_w_qkv, sb_w_o, dil_w_in, dil_w_o, fox_w_o, mlp_w_gate, mlp_w_up, mlp_w_down = (
        w.astype(BF16) for w in (sb_w_qkv, sb_w_o, dil_w_in, dil_w_o, fox_w_o, mlp_w_gate, mlp_w_up,
                                 mlp_w_down))
    moe_w_gate, moe_w_up, moe_w_down = (w.astype(BF16) for w in (moe_w_gate, moe_w_up, moe_w_down))
    fox_w_qkv = fox_w_in[:, :, :3 * D_MODEL].astype(BF16)
    h = x.reshape(batch * seq, d)
    for i in range(depth):
        kind, j = i % 3, i // 3
        if kind == 0:
            h = _sb_mixer(h, attn_norm[i], sb_w_qkv, sb_w_o, j, batch, seq)
        elif kind == 1:
            h = _dilated_mixer(h, attn_norm[i], dil_w_in, dil_q_norm[j], dil_k_norm[j], rel_bias,
                               dil_w_o, j, batch, seq)
        else:
            h = _fox_mixer(h, attn_norm[i], fox_w_qkv, fox_w_in[j, :, 3 * D_MODEL:], fox_b_f[j],
                           fox_q_norm[j], fox_k_norm[j], fox_w_o, j, batch, seq)
        f = i // 2
        if i % 2 == 0:
            h = _ffn_residual(h, ffn_norm[i], mlp_w_gate, mlp_w_up, mlp_w_down, f)
        else:
            h = _moe_residual(h, ffn_norm[i], moe_router[f], moe_w_gate, moe_w_up, moe_w_down, f)
    return h.reshape(batch, seq, d)
```

```python
import functools
import math

import jax
import jax.numpy as jnp
from jax import lax
from jax.experimental import pallas as pl
from jax.experimental.pallas import tpu as pltpu
from jax.experimental.pallas import tpu_sc as plsc

D_MODEL = 1024
N_HEADS = 16
HEAD_DIM = 64
LANES = 128
HEAD_PAIRS = D_MODEL // LANES
D_FF = 3584
N_EXPERTS = 8
N_REL_BUCKETS = 32
REL_MAX_DISTANCE = 2048
DILATED_PAIRS = ((128, 1), (512, 4), (2048, 16))
DIL_SPAN = 128
RMS_EPS = 1e-6
NEG_INF = -1e30
SB_UNDERFLOW_LOG = -104.0
QK_SCALE = 1.0 / math.sqrt(HEAD_DIM)

ROW_TILE = 1024
COL_TILE = 1024
FF_TILE = 512
MOE_ROW_TILE = 512
MOE_FF_TILE = 1792
SC_CORES = 2
SC_SUBCORES = 16
SC_CHUNK_ROWS = 64
NORM_CHUNK = 256
ATT_BLOCK = 256
ATT_Q_BLOCK = 512
ATT_K_PER_Q = ATT_Q_BLOCK // ATT_BLOCK
DIL_BLOCK = 128
DIL_ROW_TILE = 512
VMEM_LIMIT = 56 * 1024 * 1024

F32 = jnp.float32
BF16 = jnp.bfloat16


def _params(*semantics):
    return pltpu.CompilerParams(dimension_semantics=semantics, vmem_limit_bytes=VMEM_LIMIT)


def _rms_normalize(x, gain):
    inv = lax.rsqrt(jnp.mean(x * x, axis=-1, keepdims=True) + RMS_EPS)
    return x * inv * gain


def _softplus(z):
    return jnp.maximum(z, 0.0) + jnp.log(1.0 + jnp.exp(-jnp.abs(z)))


def _norm_proj_body(x_ref, g_ref, w_ref, cs_ref, bd_ref, o_ref, h_scr, *, n_norm):
    j = pl.program_id(1)

    @pl.when(j == 0)
    def _():
        h_scr[...] = _rms_normalize(x_ref[...], g_ref[...]).astype(BF16)

    acc = jnp.dot(h_scr[...], w_ref[...], preferred_element_type=F32)

    def plain():
        o_ref[...] = (acc * cs_ref[...]).astype(o_ref.dtype)

    def head_normed():
        for c in range(COL_TILE // NORM_CHUNK):
            sl = slice(c * NORM_CHUNK, (c + 1) * NORM_CHUNK)
            a = acc[:, sl]
            ss = jnp.dot((a * a).astype(BF16), bd_ref[...], preferred_element_type=F32)
            inv = lax.rsqrt(ss * (1.0 / HEAD_DIM) + RMS_EPS)
            o_ref[:, sl] = (a * inv * cs_ref[:, sl]).astype(o_ref.dtype)

    if n_norm == 0:
        plain()
    else:
        pl.when(j < n_norm)(head_normed)
        pl.when(j >= n_norm)(plain)


def _norm_proj(x, gain, w, layer, col_scale, n_norm):
    t, d = x.shape
    n = w.shape[2]
    head_id = jnp.arange(NORM_CHUNK) // HEAD_DIM
    block_diag = (head_id[:, None] == head_id[None, :]).astype(BF16)
    return pl.pallas_call(
        functools.partial(_norm_proj_body, n_norm=n_norm),
        out_shape=jax.ShapeDtypeStruct((t, n), BF16),
        grid=(t // ROW_TILE, n // COL_TILE),
        in_specs=[
            pl.BlockSpec((ROW_TILE, d), lambda i, j: (i, 0)),
            pl.BlockSpec((1, d), lambda i, j: (0, 0)),
            pl.BlockSpec((None, d, COL_TILE), lambda i, j: (layer, 0, j)),
            pl.BlockSpec((1, COL_TILE), lambda i, j: (0, j)),
            pl.BlockSpec((NORM_CHUNK, NORM_CHUNK), lambda i, j: (0, 0)),
        ],
        out_specs=pl.BlockSpec((ROW_TILE, COL_TILE), lambda i, j: (i, j)),
        scratch_shapes=[pltpu.VMEM((ROW_TILE, d), BF16)],
        compiler_params=_params("parallel", "arbitrary"),
        name="norm_proj",
    )(x, gain.reshape(1, d), w, col_scale.reshape(1, n).astype(F32), block_diag)


def _out_proj_body(x_ref, o_ref, w_ref, y_ref):
    y_ref[...] = x_ref[...] + jnp.dot(o_ref[...], w_ref[...], preferred_element_type=F32)


def _out_proj_residual(x, o, w, layer):
    t, d = x.shape
    return pl.pallas_call(
        _out_proj_body,
        out_shape=jax.ShapeDtypeStruct((t, d), F32),
        grid=(t // ROW_TILE,),
        in_specs=[
            pl.BlockSpec((ROW_TILE, d), lambda i: (i, 0)),
            pl.BlockSpec((ROW_TILE, d), lambda i: (i, 0)),
            pl.BlockSpec((None, d, d), lambda i: (layer, 0, 0)),
        ],
        out_specs=pl.BlockSpec((ROW_TILE, d), lambda i: (i, 0)),
        compiler_params=_params("parallel"),
        name="out_proj",
    )(x, o, w)


def _swiglu_hidden(h, wg, wu):
    g = jnp.dot(h, wg, preferred_element_type=F32)
    u = jnp.dot(h, wu, preferred_element_type=F32)
    return g * (1.0 / (1.0 + jnp.exp(-g))) * u


def _ffn_body(x_ref, g_ref, wg_ref, wu_ref, wd_ref, y_ref, h_scr):
    f = pl.program_id(1)

    @pl.when(f == 0)
    def _():
        x = x_ref[...]
        h_scr[...] = _rms_normalize(x, g_ref[...]).astype(BF16)
        y_ref[...] = x

    a = _swiglu_hidden(h_scr[...], wg_ref[...], wu_ref[...]).astype(BF16)
    y_ref[...] += jnp.dot(a, wd_ref[...], preferred_element_type=F32)


def _ffn_residual(x, gain, w_gate, w_up, w_down, layer):
    t, d = x.shape
    ff = w_gate.shape[2]
    return pl.pallas_call(
        _ffn_body,
        out_shape=jax.ShapeDtypeStruct((t, d), F32),
        grid=(t // ROW_TILE, ff // FF_TILE),
        in_specs=[
            pl.BlockSpec((ROW_TILE, d), lambda i, f: (i, 0)),
            pl.BlockSpec((1, d), lambda i, f: (0, 0)),
            pl.BlockSpec((None, d, FF_TILE), lambda i, f: (layer, 0, f)),
            pl.BlockSpec((None, d, FF_TILE), lambda i, f: (layer, 0, f)),
            pl.BlockSpec((None, FF_TILE, d), lambda i, f: (layer, f, 0)),
        ],
        out_specs=pl.BlockSpec((ROW_TILE, d), lambda i, f: (i, 0)),
        scratch_shapes=[pltpu.VMEM((ROW_TILE, d), BF16)],
        compiler_params=_params("parallel", "arbitrary"),
        name="ffn",
    )(x, gain.reshape(1, d), w_gate, w_up, w_down)


def _pack_bf16_pairs(x):
    half = x.shape[1] // 2
    bits = pltpu.bitcast(x.astype(BF16).astype(F32), jnp.int32)
    return bits[:, :half] | lax.shift_right_logical(bits[:, half:], jnp.int32(16))


def _unpack_bf16_pairs(p):
    left = pltpu.bitcast(p & jnp.int32(-65536), F32)
    right = pltpu.bitcast(lax.shift_left(p, jnp.int32(16)), F32)
    return jnp.concatenate([left, right], axis=1).astype(BF16)


def _router_body(x_ref, g_ref, r_ref, tri_ref, gates_ref, rank_ref, h_ref, count_scr):
    @pl.when(pl.program_id(0) == 0)
    def _():
        count_scr[...] = jnp.zeros_like(count_scr)

    h = _rms_normalize(x_ref[...], g_ref[...])
    h_ref[...] = _pack_bf16_pairs(h)
    logits = jnp.dot(h, r_ref[...], preferred_element_type=F32, precision=lax.Precision.HIGHEST)
    lane = lax.broadcasted_iota(jnp.int32, logits.shape, 1).astype(F32)
    logits = jnp.where(lane < N_EXPERTS, logits, -jnp.inf)
    m1 = jnp.max(logits, axis=-1, keepdims=True)
    i1 = jnp.min(jnp.where(logits == m1, lane, float(LANES)), axis=-1, keepdims=True)
    rest = jnp.where(lane == i1, -jnp.inf, logits)
    m2 = jnp.max(rest, axis=-1, keepdims=True)
    i2 = jnp.min(jnp.where(rest == m2, lane, float(LANES)), axis=-1, keepdims=True)
    e = jnp.exp(m2 - m1)
    g1 = 1.0 / (1.0 + e)
    gates = jnp.where(lane == i1, g1, 0.0) + jnp.where(lane == i2, e * g1, 0.0)
    gates_ref[...] = gates.T[:N_EXPERTS, :]
    chosen = jnp.where((lane == i1) | (lane == i2), 1.0, 0.0)
    inclusive = jnp.dot(tri_ref[...], chosen.astype(BF16), preferred_element_type=F32)
    rank = jnp.where(chosen > 0.0, inclusive - 1.0 + count_scr[...], -1.0)
    rank_ref[...] = rank.T[:N_EXPERTS, :]
    count_scr[...] += inclusive[ROW_TILE - 1:ROW_TILE, :]


def _router(x, gain, router):
    t, d = x.shape
    r = jnp.zeros((d, LANES), F32).at[:, :N_EXPERTS].set(router)
    idx = jnp.arange(ROW_TILE)
    tri = (idx[:, None] >= idx[None, :]).astype(BF16)
    return pl.pallas_call(
        _router_body,
        out_shape=(jax.ShapeDtypeStruct((N_EXPERTS, t), F32), jax.ShapeDtypeStruct((N_EXPERTS, t), F32),
                   jax.ShapeDtypeStruct((t, d // 2), jnp.int32)),
        grid=(t // ROW_TILE,),
        in_specs=[
            pl.BlockSpec((ROW_TILE, d), lambda i: (i, 0)),
            pl.BlockSpec((1, d), lambda i: (0, 0)),
            pl.BlockSpec((d, LANES), lambda i: (0, 0)),
            pl.BlockSpec((ROW_TILE, ROW_TILE), lambda i: (0, 0)),
        ],
        out_specs=(pl.BlockSpec((N_EXPERTS, ROW_TILE), lambda i: (0, i)),
                   pl.BlockSpec((N_EXPERTS, ROW_TILE), lambda i: (0, i)),
                   pl.BlockSpec((ROW_TILE, d // 2), lambda i: (i, 0))),
        scratch_shapes=[pltpu.VMEM((1, LANES), F32)],
        compiler_params=_params("arbitrary"),
        name="router",
    )(x, gain.reshape(1, d), r, tri)


def _sc_mesh():
    return plsc.VectorSubcoreMesh(core_axis_name="core", subcore_axis_name="subcore",
                                  num_cores=SC_CORES, num_subcores=SC_SUBCORES)


def _sc_worker_base(per_worker):
    return (lax.axis_index("subcore") * SC_CORES + lax.axis_index("core")) * per_worker


def _sc_row_gather(table, idx):
    width = table.shape[1]
    n = idx.shape[0]
    per_worker = n // (SC_CORES * SC_SUBCORES)
    n_chunks = per_worker // SC_CHUNK_ROWS
    assert n == n_chunks * SC_CHUNK_ROWS * SC_CORES * SC_SUBCORES and n_chunks % 2 == 0

    @functools.partial(
        pl.kernel, mesh=_sc_mesh(), out_type=jax.ShapeDtypeStruct((n, width), table.dtype),
        scratch_types=[pltpu.VMEM((SC_CHUNK_ROWS,), jnp.int32), pltpu.VMEM((SC_CHUNK_ROWS,), jnp.int32),
                       pltpu.VMEM((SC_CHUNK_ROWS, width), table.dtype),
                       pltpu.VMEM((SC_CHUNK_ROWS, width), table.dtype),
                       pltpu.SemaphoreType.DMA, pltpu.SemaphoreType.DMA],
        name="sc_row_gather")
    def gather(table_hbm, idx_hbm, out_hbm, idx_a, idx_b, rows_a, rows_b, sem_a, sem_b):
        base = _sc_worker_base(per_worker)
        bufs = ((idx_a, rows_a, sem_a), (idx_b, rows_b, sem_b))

        def rows_of(c):
            return pl.ds(pl.multiple_of(base + c * SC_CHUNK_ROWS, SC_CHUNK_ROWS), SC_CHUNK_ROWS)

        def fetch(c, buf):
            idx_v, rows_v, sem = buf
            pltpu.sync_copy(idx_hbm.at[rows_of(c)], idx_v)
            return pltpu.make_async_copy(table_hbm.at[idx_v], rows_v, sem)

        fetch(0, bufs[0]).start()

        @pl.loop(0, n_chunks, step=2)
        def _(c):
            for b in range(2):
                idx_v, rows_v, sem = bufs[b]
                pltpu.make_async_copy(table_hbm.at[idx_v], rows_v, sem).wait()

                @pl.when(c + b + 1 < n_chunks)
                def _():
                    fetch(c + b + 1, bufs[1 - b]).start()

                pltpu.sync_copy(rows_v, out_hbm.at[rows_of(c + b)])

    return gather(table, idx)


def _sc_row_scatter_pair(rows, idx_lo, idx_hi, n_out):
    n, width = rows.shape
    per_worker = n // (SC_CORES * SC_SUBCORES)
    n_chunks = per_worker // SC_CHUNK_ROWS
    assert n == n_chunks * SC_CHUNK_ROWS * SC_CORES * SC_SUBCORES and n_chunks % 2 == 0
    index_scratch = pltpu.VMEM((SC_CHUNK_ROWS,), jnp.int32)
    rows_scratch = pltpu.VMEM((SC_CHUNK_ROWS, width), rows.dtype)

    @functools.partial(
        pl.kernel, mesh=_sc_mesh(), out_type=jax.ShapeDtypeStruct((n_out, width), rows.dtype),
        scratch_types=[index_scratch, index_scratch, rows_scratch, rows_scratch,
                       pltpu.SemaphoreType.DMA, pltpu.SemaphoreType.DMA],
        name="sc_row_scatter")
    def scatter(rows_hbm, lo_hbm, hi_hbm, out_hbm, lo_v, hi_v, rows_a, rows_b, sem_a, sem_b):
        base = _sc_worker_base(per_worker)
        bufs = ((rows_a, sem_a), (rows_b, sem_b))

        def rows_of(c):
            return pl.ds(pl.multiple_of(base + c * SC_CHUNK_ROWS, SC_CHUNK_ROWS), SC_CHUNK_ROWS)

        def load(c, buf):
            rows_v, sem = buf
            return pltpu.make_async_copy(rows_hbm.at[rows_of(c)], rows_v, sem)

        load(0, bufs[0]).start()

        @pl.loop(0, n_chunks, step=2)
        def _(c):
            for b in range(2):
                rows_v, _ = bufs[b]
                load(c + b, bufs[b]).wait()

                @pl.when(c + b + 1 < n_chunks)
                def _():
                    load(c + b + 1, bufs[1 - b]).start()

                pltpu.sync_copy(lo_hbm.at[rows_of(c + b)], lo_v)
                pltpu.sync_copy(hi_hbm.at[rows_of(c + b)], hi_v)
                pltpu.sync_copy(rows_v, out_hbm.at[lo_v])
                pltpu.sync_copy(rows_v, out_hbm.at[hi_v])

    return scatter(rows, idx_lo, idx_hi)


def _expert_ffn_body(te_ref, nu_ref, nv_ref, h_ref, wg_ref, wu_ref, wd_ref, y_ref, acc_scr):
    i = pl.program_id(0)
    f = pl.program_id(1)

    @pl.when(i < nu_ref[0])
    def _():
        row = lax.broadcasted_iota(jnp.int32, h_ref.shape, 0)
        packed = jnp.where(row < nv_ref[i], h_ref[...], 0)
        a = _swiglu_hidden(_unpack_bf16_pairs(packed), wg_ref[...], wu_ref[...]).astype(BF16)
        y = jnp.dot(a, wd_ref[...], preferred_element_type=F32)

        @pl.when(f == 0)
        def _():
            acc_scr[...] = y

        @pl.when(f > 0)
        def _():
            acc_scr[...] += y

        @pl.when(f == pl.num_programs(1) - 1)
        def _():
            y_ref[...] = _pack_bf16_pairs(acc_scr[...])


def _expert_ffn(h_sorted, tile_expert, n_used, tile_valid, w_gate, w_up, w_down, layer):
    rows, half = h_sorted.shape
    d = 2 * half
    ff = w_gate.shape[3]
    n_f = ff // MOE_FF_TILE

    def row_map(i, f, te, nu, nv):
        return (jnp.minimum(i, nu[0] - 1), 0)

    def col_step(i, f, nu):
        return jnp.where(i < nu[0], f, n_f - 1)

    grid_spec = pltpu.PrefetchScalarGridSpec(
        num_scalar_prefetch=3,
        grid=(rows // MOE_ROW_TILE, n_f),
        in_specs=[
            pl.BlockSpec((MOE_ROW_TILE, half), row_map),
            pl.BlockSpec((None, None, d, MOE_FF_TILE),
                         lambda i, f, te, nu, nv: (layer, te[i], 0, col_step(i, f, nu))),
            pl.BlockSpec((None, None, d, MOE_FF_TILE),
                         lambda i, f, te, nu, nv: (layer, te[i], 0, col_step(i, f, nu))),
            pl.BlockSpec((None, None, MOE_FF_TILE, d),
                         lambda i, f, te, nu, nv: (layer, te[i], col_step(i, f, nu), 0)),
        ],
        out_specs=pl.BlockSpec((MOE_ROW_TILE, half), row_map),
        scratch_shapes=[pltpu.VMEM((MOE_ROW_TILE, d), F32)],
    )
    return pl.pallas_call(
        _expert_ffn_body,
        out_shape=jax.ShapeDtypeStruct((rows, half), jnp.int32),
        grid_spec=grid_spec,
        compiler_params=_params("arbitrary", "arbitrary"),
        name="expert_ffn",
    )(tile_expert, n_used, tile_valid, h_sorted, w_gate, w_up, w_down)


def _combine_body(x_ref, y_ref, g_ref, o_ref):
    out = x_ref[...]
    for s in range(2):
        column = jnp.broadcast_to(g_ref[s:s + 1, :], (LANES, g_ref.shape[1])).T
        gate = jnp.concatenate([column] * (out.shape[1] // LANES), axis=1)
        out = out + _unpack_bf16_pairs(y_ref[s]).astype(F32) * gate
    o_ref[...] = out


def _combine_residual(x, y_pairs, gates2):
    t, d = x.shape
    return pl.pallas_call(
        _combine_body,
        out_shape=jax.ShapeDtypeStruct((t, d), F32),
        grid=(t // ROW_TILE,),
        in_specs=[
            pl.BlockSpec((ROW_TILE, d), lambda i: (i, 0)),
            pl.BlockSpec((2, ROW_TILE, d // 2), lambda i: (0, i, 0)),
            pl.BlockSpec((2, ROW_TILE), lambda i: (0, i)),
        ],
        out_specs=pl.BlockSpec((ROW_TILE, d), lambda i: (i, 0)),
        compiler_params=_params("parallel"),
        name="moe_combine",
    )(x, y_pairs, gates2)


def _moe_residual(x, gain, router, w_gate, w_up, w_down, layer):
    t, d = x.shape
    gates, rank, h_packed = _router(x, gain, router)
    rank8 = rank.astype(jnp.int32)
    chosen = rank8 >= 0
    counts = jnp.sum(chosen, axis=1, dtype=jnp.int32)
    padded = (counts + MOE_ROW_TILE - 1) // MOE_ROW_TILE * MOE_ROW_TILE
    ends = jnp.cumsum(padded)
    starts = ends - padded
    pos = starts[:, None] + rank8
    max_rows = 2 * t + N_EXPERTS * MOE_ROW_TILE
    pos_lo = jnp.min(jnp.where(chosen, pos, max_rows), axis=0)
    pos_hi = jnp.max(jnp.where(chosen, pos, -1), axis=0)
    gates2 = jnp.stack([jnp.sum(jnp.where(chosen & (pos == pos_lo[None, :]), gates, 0.0), axis=0),
                        jnp.sum(jnp.where(chosen & (pos == pos_hi[None, :]), gates, 0.0), axis=0)])
    n_tiles = max_rows // MOE_ROW_TILE
    n_used = (ends[-1] // MOE_ROW_TILE).astype(jnp.int32)
    tile_start = jnp.minimum(jnp.arange(n_tiles, dtype=jnp.int32), n_used - 1) * MOE_ROW_TILE
    tile_expert = jnp.sum(tile_start[:, None] >= ends[None, :], axis=1, dtype=jnp.int32)
    tile_valid = jnp.clip((starts + counts)[tile_expert] - tile_start, 0, MOE_ROW_TILE).astype(jnp.int32)

    h_sorted = _sc_row_scatter_pair(h_packed, pos_lo, pos_hi, max_rows)
    y_sorted = _expert_ffn(h_sorted, tile_expert, n_used.reshape(1), tile_valid, w_gate, w_up, w_down, layer)
    y_pairs = _sc_row_gather(y_sorted, jnp.concatenate([pos_lo, pos_hi])).reshape(2, t, d // 2)
    return _combine_residual(x, y_pairs, gates2)


def _split_head_pair(q):
    is_first = lax.broadcasted_iota(jnp.int32, (1, LANES), 1) < HEAD_DIM
    zero = jnp.zeros_like(q)
    return jnp.where(is_first, q, zero), jnp.where(is_first, zero, q)


def _merge_head_pair(first, second):
    is_first = lax.broadcasted_iota(jnp.int32, (1, LANES), 1) < HEAD_DIM
    return jnp.where(is_first, first, second)


def _qk(q, k):
    return lax.dot_general(q, k, (((1,), (1,)), ((), ())), preferred_element_type=F32)


def _transpose_values(v_ref, vt_scr):
    for c in range(v_ref.shape[0] // ATT_BLOCK):
        rows = slice(c * ATT_BLOCK, (c + 1) * ATT_BLOCK)
        vt_scr[:, rows] = v_ref[rows, :].astype(F32).T.astype(vt_scr.dtype)


def _sb_body(q_ref, k_ref, v_ref, u_ref, o_ref, vt_scr, acc_scr, carry_scr, z_scr, keep_scr, sum_scr):
    qi = pl.program_id(2)

    @pl.when(qi == 0)
    def _():
        _transpose_values(v_ref, vt_scr)

    upper = u_ref[...]
    acc_scr[...] = jnp.zeros_like(acc_scr)
    carry_scr[...] = jnp.zeros_like(carry_scr)
    key = lax.broadcasted_iota(jnp.int32, (ATT_BLOCK, ATT_BLOCK), 0)
    query = lax.broadcasted_iota(jnp.int32, (ATT_BLOCK, ATT_BLOCK), 1)
    strict = key < query
    qs = [_split_head_pair(q_ref[j * ATT_BLOCK:(j + 1) * ATT_BLOCK, :]) for j in range(ATT_K_PER_Q)]

    def process(j, kb, keep):
        start = pl.multiple_of(kb * ATT_BLOCK, ATT_BLOCK)
        k = k_ref[pl.ds(start, ATT_BLOCK), :]
        for hh in range(2):
            z = _qk(k, qs[j][hh])
            log_beta = z - _softplus(z)
            log_keep = log_beta - z
            if keep is not None:
                log_keep = jnp.where(keep, log_keep, 0.0)
            remain = jnp.dot(upper, log_keep.astype(BF16), preferred_element_type=F32)
            w = jnp.exp(log_beta + remain + carry_scr[j, hh])
            if keep is not None:
                w = jnp.where(keep, w, 0.0)
            carry_scr[j, hh] += jnp.sum(log_keep, axis=0, keepdims=True)
            vt = vt_scr[pl.ds(hh * HEAD_DIM, HEAD_DIM), pl.ds(start, ATT_BLOCK)]
            acc_scr[j, hh] += jnp.dot(vt, w.astype(BF16), preferred_element_type=F32)

    units = []
    for j in range(ATT_K_PER_Q):
        g = qi * ATT_K_PER_Q + j
        has_previous = None if j > 0 else jnp.broadcast_to(g > 0, strict.shape)
        for kb, keep in ((g, strict), (jnp.maximum(g - 1, 0), has_previous)):
            for hh in range(2):
                units.append((j, hh, pl.multiple_of(kb * ATT_BLOCK, ATT_BLOCK), keep))

    for u, (j, hh, start, keep) in enumerate(units):
        z_scr[u] = _qk(k_ref[pl.ds(start, ATT_BLOCK), :], qs[j][hh])

    for u, (j, hh, start, keep) in enumerate(units):
        z = z_scr[u]
        log_beta = z - _softplus(z)
        log_keep = log_beta - z
        if keep is not None:
            log_keep = jnp.where(keep, log_keep, 0.0)
        z_scr[u] = log_beta
        keep_scr[u] = log_keep.astype(BF16)
        sum_scr[u] = jnp.sum(log_keep, axis=0, keepdims=True)

    for u in range(len(units)):
        z_scr[u] += jnp.dot(upper, keep_scr[u], preferred_element_type=F32)

    for u, (j, hh, start, keep) in enumerate(units):
        diagonal = u % 4 < 2
        log_w = z_scr[u] if diagonal else z_scr[u] + sum_scr[u - 2]
        w = jnp.exp(log_w)
        if keep is not None:
            w = jnp.where(keep, w, 0.0)
        keep_scr[u] = w.astype(BF16)

    for u, (j, hh, start, keep) in enumerate(units):
        vt = vt_scr[pl.ds(hh * HEAD_DIM, HEAD_DIM), pl.ds(start, ATT_BLOCK)]
        acc_scr[j, hh] += jnp.dot(vt, keep_scr[u], preferred_element_type=F32)
        carry_scr[j, hh] += sum_scr[u]

    for j in range(ATT_K_PER_Q):
        def more(kb, j=j):
            return (kb >= 0) & (jnp.max(carry_scr[j]) > SB_UNDERFLOW_LOG)

        def step(kb, j=j):
            process(j, kb, None)
            return kb - 1

        lax.while_loop(more, step, qi * ATT_K_PER_Q + j - 2)

    for j in range(ATT_K_PER_Q):
        out_t = jnp.concatenate([acc_scr[j, 0], acc_scr[j, 1]], axis=0)
        o_ref[j * ATT_BLOCK:(j + 1) * ATT_BLOCK, :] = out_t.T.astype(o_ref.dtype)


def _sb_attention(proj, batch, seq):
    t = proj.shape[0]
    nq = seq // ATT_Q_BLOCK
    idx = jnp.arange(ATT_BLOCK)
    upper = (idx[None, :] > idx[:, None]).astype(BF16)
    return pl.pallas_call(
        _sb_body,
        out_shape=jax.ShapeDtypeStruct((t, D_MODEL), BF16),
        grid=(batch, HEAD_PAIRS, nq),
        in_specs=[
            pl.BlockSpec((ATT_Q_BLOCK, LANES), lambda b, p, i: (b * nq + i, p)),
            pl.BlockSpec((seq, LANES), lambda b, p, i: (b, HEAD_PAIRS + p)),
            pl.BlockSpec((seq, LANES), lambda b, p, i: (b, 2 * HEAD_PAIRS + p)),
            pl.BlockSpec((ATT_BLOCK, ATT_BLOCK), lambda b, p, i: (0, 0)),
        ],
        out_specs=pl.BlockSpec((ATT_Q_BLOCK, LANES), lambda b, p, i: (b * nq + i, p)),
        scratch_shapes=[
            pltpu.VMEM((LANES, seq), BF16),
            pltpu.VMEM((ATT_K_PER_Q, 2, HEAD_DIM, ATT_BLOCK), F32),
            pltpu.VMEM((ATT_K_PER_Q, 2, 1, ATT_BLOCK), F32),
            pltpu.VMEM((4 * ATT_K_PER_Q, ATT_BLOCK, ATT_BLOCK), F32),
            pltpu.VMEM((4 * ATT_K_PER_Q, ATT_BLOCK, ATT_BLOCK), BF16),
            pltpu.VMEM((4 * ATT_K_PER_Q, 1, ATT_BLOCK), F32),
        ],
        compiler_params=_params("parallel", "parallel", "arbitrary"),
        name="sb_attention",
    )(proj, proj, proj, upper)


def _fox_gate_body(x_ref, g_ref, w_ref, b_ref, tri_ref, c_ref):
    h = _rms_normalize(x_ref[...], g_ref[...])
    logits = jnp.dot(h, w_ref[...], preferred_element_type=F32, precision=lax.Precision.HIGHEST)
    log_f = -_softplus(-(logits + b_ref[...]))
    seq = log_f.shape[0]
    carry = jnp.zeros((1, LANES), F32)
    for blk in range(seq // ATT_BLOCK):
        rows = slice(blk * ATT_BLOCK, (blk + 1) * ATT_BLOCK)
        c = jnp.dot(tri_ref[...], log_f[rows], preferred_element_type=F32,
                    precision=lax.Precision.HIGHEST) + carry
        c_ref[rows, :] = c
        carry = c[ATT_BLOCK - 1:ATT_BLOCK, :]


def _fox_cum_log_forget(x, gain, w_gate, b_gate, batch, seq):
    t, d = x.shape
    w = jnp.zeros((d, LANES), F32).at[:, :N_HEADS].set(w_gate)
    b = jnp.zeros((1, LANES), F32).at[0, :N_HEADS].set(b_gate)
    idx = jnp.arange(ATT_BLOCK)
    tri = (idx[:, None] >= idx[None, :]).astype(F32)
    return pl.pallas_call(
        _fox_gate_body,
        out_shape=jax.ShapeDtypeStruct((t, LANES), F32),
        grid=(batch,),
        in_specs=[
            pl.BlockSpec((seq, d), lambda i: (i, 0)),
            pl.BlockSpec((1, d), lambda i: (0, 0)),
            pl.BlockSpec((d, LANES), lambda i: (0, 0)),
            pl.BlockSpec((1, LANES), lambda i: (0, 0)),
            pl.BlockSpec((ATT_BLOCK, ATT_BLOCK), lambda i: (0, 0)),
        ],
        out_specs=pl.BlockSpec((seq, LANES), lambda i: (i, 0)),
        compiler_params=_params("parallel"),
        name="fox_gate",
    )(x, gain.reshape(1, d), w, b, tri)


def _fox_body(q_ref, k_ref, v_ref, cq_ref, ck_ref, o_ref, vt_scr, ckb_scr, a_scr, p_scr):
    qi = pl.program_id(2)

    @pl.when(qi == 0)
    def _():
        _transpose_values(v_ref, vt_scr)
        for hh in range(2):
            for c in range(ckb_scr.shape[1] // ATT_BLOCK):
                rows = slice(c * ATT_BLOCK, (c + 1) * ATT_BLOCK)
                ckb_scr[hh, rows, :] = jnp.broadcast_to(ck_ref[0, hh, :, rows], (LANES, ATT_BLOCK)).T

    qs = _split_head_pair(q_ref[...])
    key = lax.broadcasted_iota(jnp.int32, (ATT_BLOCK, ATT_Q_BLOCK), 0)
    query = lax.broadcasted_iota(jnp.int32, (ATT_BLOCK, ATT_Q_BLOCK), 1)
    row_shape = (1, ATT_Q_BLOCK)

    def scores(kb, tops, diagonal):
        start = pl.multiple_of(kb * ATT_BLOCK, ATT_BLOCK)
        k = k_ref[pl.ds(start, ATT_BLOCK), :]
        if diagonal:
            causal = key + (kb * ATT_BLOCK - qi * ATT_Q_BLOCK) <= query
        new_tops = []
        for hh in range(2):
            ck = ckb_scr[hh, pl.ds(start, ATT_BLOCK), :]
            a = _qk(k, qs[hh]) - jnp.concatenate([ck] * (ATT_Q_BLOCK // LANES), axis=1)
            if diagonal:
                a = jnp.where(causal, a, NEG_INF)
            a_scr[hh, kb] = a
            new_tops.append(jnp.maximum(tops[hh], jnp.max(a, axis=0, keepdims=True)))
        return tuple(new_tops)

    def scores_step(it, tops):
        for r in range(ATT_K_PER_Q):
            tops = scores(it * ATT_K_PER_Q + r, tops, False)
        return tops

    tops = lax.fori_loop(0, qi, scores_step, (jnp.full(row_shape, NEG_INF, F32),) * 2)
    for r in range(ATT_K_PER_Q):
        tops = scores(qi * ATT_K_PER_Q + r, tops, True)
    shifts = tuple(cq_ref[0, hh] - (tops[hh] + cq_ref[0, hh]) for hh in range(2))

    def weights_step(it, sums):
        sums = list(sums)
        for r in range(ATT_K_PER_Q):
            kb = it * ATT_K_PER_Q + r
            for hh in range(2):
                p = jnp.exp(a_scr[hh, kb] + shifts[hh])
                p_scr[hh, kb] = p.astype(BF16)
                sums[hh] = sums[hh] + jnp.sum(p, axis=0, keepdims=True)
        return tuple(sums)

    sums = lax.fori_loop(0, qi + 1, weights_step, (jnp.zeros(row_shape, F32),) * 2)

    def values_step(it, accs):
        accs = list(accs)
        for r in range(ATT_K_PER_Q):
            kb = it * ATT_K_PER_Q + r
            start = pl.multiple_of(kb * ATT_BLOCK, ATT_BLOCK)
            for hh in range(2):
                vt = vt_scr[pl.ds(hh * HEAD_DIM, HEAD_DIM), pl.ds(start, ATT_BLOCK)]
                accs[hh] = accs[hh] + jnp.dot(vt, p_scr[hh, kb], preferred_element_type=F32)
        return tuple(accs)

    accs = lax.fori_loop(0, qi + 1, values_step, (jnp.zeros((HEAD_DIM, ATT_Q_BLOCK), F32),) * 2)
    out_t = jnp.concatenate([accs[0] / sums[0], accs[1] / sums[1]], axis=0)
    o_ref[...] = out_t.T.astype(o_ref.dtype)


def _fox_attention(proj, cum, batch, seq):
    t = proj.shape[0]
    nq = seq // ATT_Q_BLOCK
    cum_h = cum[:, :N_HEADS].reshape(batch, seq, N_HEADS).transpose(0, 2, 1)
    cum_rows = cum_h.reshape(batch, N_HEADS, 1, seq)
    return pl.pallas_call(
        _fox_body,
        out_shape=jax.ShapeDtypeStruct((t, D_MODEL), BF16),
        grid=(batch, HEAD_PAIRS, nq),
        in_specs=[
            pl.BlockSpec((ATT_Q_BLOCK, LANES), lambda b, p, i: (b * nq + i, p)),
            pl.BlockSpec((seq, LANES), lambda b, p, i: (b, HEAD_PAIRS + p)),
            pl.BlockSpec((seq, LANES), lambda b, p, i: (b, 2 * HEAD_PAIRS + p)),
            pl.BlockSpec((1, 2, 1, ATT_Q_BLOCK), lambda b, p, i: (b, p, 0, i)),
            pl.BlockSpec((1, 2, 1, seq), lambda b, p, i: (b, p, 0, 0)),
        ],
        out_specs=pl.BlockSpec((ATT_Q_BLOCK, LANES), lambda b, p, i: (b * nq + i, p)),
        scratch_shapes=[
            pltpu.VMEM((LANES, seq), BF16),
            pltpu.VMEM((2, seq, LANES), F32),
            pltpu.VMEM((2, seq // ATT_BLOCK, ATT_BLOCK, ATT_Q_BLOCK), F32),
            pltpu.VMEM((2, seq // ATT_BLOCK, ATT_BLOCK, ATT_Q_BLOCK), BF16),
        ],
        compiler_params=_params("parallel", "parallel", "arbitrary"),
        name="fox_attention",
    )(proj, proj, proj, cum_rows, cum_rows)


def _band_body(q_ref, k_ref, v_ref, bias_ref, o_ref, s_ref, vt_scr, logit_scr, p_scr, inv_scr, *, sub_len):
    seq = v_ref.shape[0]
    n_blocks = seq // DIL_BLOCK
    _transpose_values(v_ref, vt_scr)

    def key_rows(n):
        first = (n * DIL_BLOCK) % sub_len == 0
        return first, slice((n if first else n - 1) * DIL_BLOCK, (n + 1) * DIL_BLOCK)

    for n in range(n_blocks):
        first, k_rows = key_rows(n)
        qs = _split_head_pair(q_ref[n * DIL_BLOCK:(n + 1) * DIL_BLOCK, :])
        k = k_ref[k_rows, :]
        for hh in range(2):
            bias = bias_ref[hh, DIL_BLOCK:, :] if first else bias_ref[hh]
            logit_scr[2 * n + hh, :k.shape[0], :] = _qk(k, qs[hh]) + bias

    for n in range(n_blocks):
        first, k_rows = key_rows(n)
        n_keys = k_rows.stop - k_rows.start
        for hh in range(2):
            logits = logit_scr[2 * n + hh, :n_keys, :]
            m = jnp.max(logits, axis=0, keepdims=True)
            p = jnp.exp(logits - m)
            l = jnp.sum(p, axis=0, keepdims=True)
            p_scr[2 * n + hh, :n_keys, :] = p.astype(BF16)
            inv_scr[2 * n + hh] = 1.0 / l
            s_ref[0, 0, hh:hh + 1, n * DIL_BLOCK:(n + 1) * DIL_BLOCK] = m + jnp.log(l)

    for n in range(n_blocks):
        first, k_rows = key_rows(n)
        n_keys = k_rows.stop - k_rows.start
        outs = []
        for hh in range(2):
            vt = vt_scr[hh * HEAD_DIM:(hh + 1) * HEAD_DIM, k_rows]
            o = jnp.dot(vt, p_scr[2 * n + hh, :n_keys, :], preferred_element_type=F32)
            outs.append(o * inv_scr[2 * n + hh])
        o_ref[n * DIL_BLOCK:(n + 1) * DIL_BLOCK, :] = jnp.concatenate(outs, axis=0).T.astype(o_ref.dtype)


def _band_attention(qk, v, bias_t, batch, seq, sub_len):
    t = v.shape[0]
    return pl.pallas_call(
        functools.partial(_band_body, sub_len=sub_len),
        out_shape=(jax.ShapeDtypeStruct((t, D_MODEL), BF16),
                   jax.ShapeDtypeStruct((batch, HEAD_PAIRS, 2, seq), F32)),
        grid=(batch, HEAD_PAIRS),
        in_specs=[
            pl.BlockSpec((None, seq, LANES), lambda b, p: (0, b, p)),
            pl.BlockSpec((None, seq, LANES), lambda b, p: (1, b, p)),
            pl.BlockSpec((seq, LANES), lambda b, p: (b, p)),
            pl.BlockSpec((2, 2 * DIL_BLOCK, DIL_BLOCK), lambda b, p: (p, 0, 0)),
        ],
        out_specs=(pl.BlockSpec((seq, LANES), lambda b, p: (b, p)),
                   pl.BlockSpec((1, 1, 2, seq), lambda b, p: (b, p, 0, 0))),
        scratch_shapes=[pltpu.VMEM((LANES, seq), BF16),
                        pltpu.VMEM((2 * seq // DIL_BLOCK, 2 * DIL_BLOCK, DIL_BLOCK), F32),
                        pltpu.VMEM((2 * seq // DIL_BLOCK, 2 * DIL_BLOCK, DIL_BLOCK), BF16),
                        pltpu.VMEM((2 * seq // DIL_BLOCK, 1, DIL_BLOCK), F32)],
        compiler_params=_params("parallel", "parallel"),
        name="band_attention",
    )(qk, qk, v, bias_t)


def _softmax_merge_body(o1_ref, o2_ref, o3_ref, s_ref, e_ref, o_ref):
    s = s_ref[...]
    groups = [s, pltpu.roll(s, LANES - N_HEADS, axis=1), pltpu.roll(s, LANES - 2 * N_HEADS, axis=1)]
    top = jnp.maximum(jnp.maximum(groups[0], groups[1]), groups[2])
    weights = [jnp.exp(g - top) for g in groups]
    inv = 1.0 / (weights[0] + weights[1] + weights[2])
    out = jnp.zeros(o_ref.shape, F32)
    for w, part in zip(weights, (o1_ref, o2_ref, o3_ref)):
        spread = jnp.dot((w * inv).astype(BF16), e_ref[...], preferred_element_type=F32)
        out = out + spread * part[...].astype(F32)
    o_ref[...] = out.astype(o_ref.dtype)


def _softmax_merge(outs, s_all):
    t, d = outs[0].shape
    head_of = jnp.arange(d) // HEAD_DIM
    expand = (jnp.arange(LANES)[:, None] == head_of[None, :]).astype(BF16)
    rows = pl.BlockSpec((ROW_TILE, d), lambda i: (i, 0))
    return pl.pallas_call(
        _softmax_merge_body,
        out_shape=jax.ShapeDtypeStruct((t, d), BF16),
        grid=(t // ROW_TILE,),
        in_specs=[rows, rows, rows,
                  pl.BlockSpec((ROW_TILE, LANES), lambda i: (i, 0)),
                  pl.BlockSpec((LANES, d), lambda i: (0, 0))],
        out_specs=rows,
        compiler_params=_params("parallel"),
        name="softmax_merge",
    )(*outs, s_all, expand)


def _t5_causal_bucket(distance):
    max_exact = N_REL_BUCKETS // 2
    d = jnp.maximum(distance, 1).astype(F32)
    log_b = max_exact + (jnp.log(d / max_exact) / math.log(REL_MAX_DISTANCE / max_exact)
                         * (N_REL_BUCKETS - max_exact)).astype(jnp.int32)
    log_b = jnp.minimum(log_b, N_REL_BUCKETS - 1)
    return jnp.where(distance < max_exact, distance, log_b)


def _dilated_bias(rel_bias):
    kj = jnp.arange(2 * DIL_BLOCK, dtype=jnp.int32)
    qi = jnp.arange(DIL_BLOCK, dtype=jnp.int32)
    delta = qi[None, :] + DIL_BLOCK - kj[:, None]
    in_band = (delta >= 0) & (delta <= DIL_SPAN)
    buckets = jnp.stack([_t5_causal_bucket(jnp.maximum(delta, 0) * dil) for _, dil in DILATED_PAIRS])
    one_hot = (buckets[..., None] == jnp.arange(N_REL_BUCKETS)).astype(F32)
    bias = jnp.einsum("gkqb,bh->ghkq", one_hot, rel_bias.astype(F32), precision=lax.Precision.HIGHEST)
    return jnp.where(in_band[None, None], bias, NEG_INF)


def _dilated_proj_body(x_ref, g_ref, w_ref, cs_ref, bd_ref, *rest):
    n_groups = len(DILATED_PAIRS)
    qk_refs, v_refs = rest[:n_groups], rest[n_groups:2 * n_groups]
    h_scr, res_scr = rest[2 * n_groups:]
    j = pl.program_id(1)

    @pl.when(j == 0)
    def _():
        h_scr[...] = _rms_normalize(x_ref[...], g_ref[...]).astype(BF16)

    acc = jnp.dot(h_scr[...], w_ref[...], preferred_element_type=F32)

    def keep(res, first_lane):
        for c in range(res.shape[1] // LANES):
            res_scr[first_lane // LANES + c] = res[:, c * LANES:(c + 1) * LANES]

    @pl.when(j < 2 * n_groups)
    def _():
        for c in range(COL_TILE // NORM_CHUNK):
            sl = slice(c * NORM_CHUNK, (c + 1) * NORM_CHUNK)
            a = acc[:, sl]
            ss = jnp.dot((a * a).astype(BF16), bd_ref[...], preferred_element_type=F32)
            keep(a * lax.rsqrt(ss * (1.0 / HEAD_DIM) + RMS_EPS) * cs_ref[:, sl], c * NORM_CHUNK)

    @pl.when(j == 2 * n_groups)
    def _():
        keep(acc * cs_ref[...], 0)

    def write_classes(ref, dil):
        rows = res_scr.shape[1] // dil
        for c in range(res_scr.shape[0]):
            lanes = slice(c * LANES, (c + 1) * LANES)
            if dil == 1:
                ref[:, lanes] = res_scr[c].astype(ref.dtype)
            else:
                for r in range(dil):
                    ref[r, :, lanes] = res_scr[c, pl.ds(r, rows, stride=dil), :].astype(ref.dtype)

    for g, (_, dil) in enumerate(DILATED_PAIRS):
        pl.when(j // 2 == g)(functools.partial(write_classes, qk_refs[g], dil))
        pl.when(j == 2 * n_groups)(functools.partial(write_classes, v_refs[g], dil))


def _dilated_proj(x, gain, w, layer, col_scale, batch, seq):
    t, d = x.shape
    n = w.shape[2]
    n_groups = len(DILATED_PAIRS)
    tiles_per_seq = seq // DIL_ROW_TILE
    head_id = jnp.arange(NORM_CHUNK) // HEAD_DIM
    block_diag = (head_id[:, None] == head_id[None, :]).astype(BF16)
    out_shape, out_specs = [], []
    for kind in ("qk", "v"):
        for g, (_, dil) in enumerate(DILATED_PAIRS):
            rows = DIL_ROW_TILE // dil
            if kind == "qk":
                shape = (2, batch, dil, seq // dil, d)
                block = (None, None, dil, rows, d)
                index = lambda i, j, g=g: (jnp.clip(j - 2 * g, 0, 1), i // tiles_per_seq, 0, i % tiles_per_seq, 0)
            else:
                shape = (batch, dil, seq // dil, d)
                block = (None, dil, rows, d)
                index = lambda i, j: (i // tiles_per_seq, 0, i % tiles_per_seq, 0)
            if dil == 1:
                block = block[:-3] + (None,) + block[-2:]
            out_shape.append(jax.ShapeDtypeStruct(shape, BF16))
            out_specs.append(pl.BlockSpec(block, index))
    outs = pl.pallas_call(
        _dilated_proj_body,
        out_shape=tuple(out_shape),
        grid=(t // DIL_ROW_TILE, n // COL_TILE),
        in_specs=[
            pl.BlockSpec((DIL_ROW_TILE, d), lambda i, j: (i, 0)),
            pl.BlockSpec((1, d), lambda i, j: (0, 0)),
            pl.BlockSpec((None, d, COL_TILE), lambda i, j: (layer, 0, j)),
            pl.BlockSpec((1, COL_TILE), lambda i, j: (0, j)),
            pl.BlockSpec((NORM_CHUNK, NORM_CHUNK), lambda i, j: (0, 0)),
        ],
        out_specs=tuple(out_specs),
        scratch_shapes=[pltpu.VMEM((DIL_ROW_TILE, d), BF16),
                        pltpu.VMEM((COL_TILE // LANES, DIL_ROW_TILE, LANES), F32)],
        compiler_params=_params("parallel", "arbitrary"),
        name="dilated_proj",
    )(x, gain.reshape(1, d), w, col_scale.reshape(1, n).astype(F32), block_diag)
    qk = [o.reshape(2, t, d) for o in outs[:n_groups]]
    v = [o.reshape(t, d) for o in outs[n_groups:]]
    return qk, v


def _dilated_attention(qk, v, rel_bias, batch, seq):
    t = v[0].shape[0]
    n_groups = len(DILATED_PAIRS)
    bias_t = _dilated_bias(rel_bias)
    outs, stats = [], []
    for g, (_, dil) in enumerate(DILATED_PAIRS):
        sub_len = seq // dil
        o, s = _band_attention(qk[g], v[g], bias_t[g], batch, seq, sub_len)
        o = o.reshape(batch, dil, sub_len, D_MODEL).transpose(0, 2, 1, 3).reshape(t, D_MODEL)
        s = s.reshape(batch, HEAD_PAIRS, 2, dil, sub_len).swapaxes(3, 4)
        outs.append(o)
        stats.append(s.reshape(batch, N_HEADS, seq).transpose(0, 2, 1).reshape(t, N_HEADS))
    stats.append(jnp.zeros((t, LANES - n_groups * N_HEADS), F32))
    return _softmax_merge(outs, jnp.concatenate(stats, axis=1))


def _tile_heads(v):
    return jnp.tile(v.astype(F32), N_HEADS)


def _sb_mixer(x, gain, w_qkv, w_o, layer, batch, seq):
    ones = jnp.ones((D_MODEL,), F32)
    col_scale = jnp.concatenate([ones * QK_SCALE, ones, ones])
    proj = _norm_proj(x, gain, w_qkv, layer, col_scale, 0)
    return _out_proj_residual(x, _sb_attention(proj, batch, seq), w_o, layer)


def _dilated_mixer(x, gain, w_in, q_norm, k_norm, rel_bias, w_o, layer, batch, seq):
    scales = []
    for g in range(len(DILATED_PAIRS)):
        scales += [_tile_heads(q_norm[g]) * QK_SCALE, _tile_heads(k_norm[g])]
    scales.append(jnp.ones((D_MODEL,), F32))
    qk, v = _dilated_proj(x, gain, w_in, layer, jnp.concatenate(scales), batch, seq)
    return _out_proj_residual(x, _dilated_attention(qk, v, rel_bias, batch, seq), w_o, layer)


def _fox_mixer(x, gain, w_qkv, w_gate, b_f, q_norm, k_norm, w_o, layer, batch, seq):
    col_scale = jnp.concatenate([_tile_heads(q_norm) * QK_SCALE, _tile_heads(k_norm),
                                 jnp.ones((D_MODEL,), F32)])
    proj = _norm_proj(x, gain, w_qkv, layer, col_scale, 2)
    cum = _fox_cum_log_forget(x, gain, w_gate, b_f, batch, seq)
    return _out_proj_residual(x, _fox_attention(proj, cum, batch, seq), w_o, layer)


def kernel(x, sb_w_qkv, sb_w_o, dil_w_in, dil_q_norm, dil_k_norm, dil_w_o, fox_w_in, fox_b_f,
           fox_q_norm, fox_k_norm, fox_w_o, rel_bias, attn_norm, ffn_norm, mlp_w_gate, mlp_w_up,
           mlp_w_down, moe_router, moe_w_gate, moe_w_up, moe_w_down):
    batch, seq, d = x.shape
    depth = attn_norm.shape[0]
    sb_w_qkv, sb_w_o, dil_w_in, dil_w_o, fox_w_o, mlp_w_gate, mlp_w_up, mlp_w_down = (
        w.astype(BF16) for w in (sb_w_qkv, sb_w_o, dil_w_in, dil_w_o, fox_w_o, mlp_w_gate, mlp_w_up,
                                 mlp_w_down))
    moe_w_gate, moe_w_up, moe_w_down = (w.astype(BF16) for w in (moe_w_gate, moe_w_up, moe_w_down))
    fox_w_qkv = fox_w_in[:, :, :3 * D_MODEL].astype(BF16)
    h = x.reshape(batch * seq, d)
    for i in range(depth):
        kind, j = i % 3, i // 3
        if kind == 0:
            h = _sb_mixer(h, attn_norm[i], sb_w_qkv, sb_w_o, j, batch, seq)
        elif kind == 1:
            h = _dilated_mixer(h, attn_norm[i], dil_w_in, dil_q_norm[j], dil_k_norm[j], rel_bias,
                               dil_w_o, j, batch, seq)
        else:
            h = _fox_mixer(h, attn_norm[i], fox_w_qkv, fox_w_in[j, :, 3 * D_MODEL:], fox_b_f[j],
                           fox_q_norm[j], fox_k_norm[j], fox_w_o, j, batch, seq)
        f = i // 2
        if i % 2 == 0:
            h = _ffn_residual(h, ffn_norm[i], mlp_w_gate, mlp_w_up, mlp_w_down, f)
        else:
            h = _moe_residual(h, ffn_norm[i], moe_router[f], moe_w_gate, moe_w_up, moe_w_down, f)
    return h.reshape(batch, seq, d)
```

```python
import functools
import math

import jax
import jax.numpy as jnp
from jax import lax
from jax.experimental import pallas as pl
from jax.experimental.pallas import tpu as pltpu
from jax.experimental.pallas import tpu_sc as plsc

D_MODEL = 1024
N_HEADS = 16
HEAD_DIM = 64
LANES = 128
HEAD_PAIRS = D_MODEL // LANES
D_FF = 3584
N_EXPERTS = 8
N_REL_BUCKETS = 32
REL_MAX_DISTANCE = 2048
DILATED_PAIRS = ((128, 1), (512, 4), (2048, 16))
DIL_SPAN = 128
RMS_EPS = 1e-6
NEG_INF = -1e30
SB_UNDERFLOW_LOG = -104.0
QK_SCALE = 1.0 / math.sqrt(HEAD_DIM)

ROW_TILE = 1024
COL_TILE = 1024
FF_TILE = 1792
SWIGLU_CHUNK = 256
MOE_ROW_TILE = 512
MOE_FF_TILE = 1792
SC_CORES = 2
SC_SUBCORES = 16
SC_CHUNK_ROWS = 64
NORM_CHUNK = 256
ATT_BLOCK = 256
ATT_Q_BLOCK = 512
ATT_K_PER_Q = ATT_Q_BLOCK // ATT_BLOCK
DIL_BLOCK = 128
DIL_ROW_TILE = 512
VMEM_LIMIT = 56 * 1024 * 1024

F32 = jnp.float32
BF16 = jnp.bfloat16


def _params(*semantics):
    return pltpu.CompilerParams(dimension_semantics=semantics, vmem_limit_bytes=VMEM_LIMIT)


def _rms_normalize(x, gain):
    inv = lax.rsqrt(jnp.mean(x * x, axis=-1, keepdims=True) + RMS_EPS)
    return x * inv * gain


def _softplus(z):
    return jnp.maximum(z, 0.0) + jnp.log(1.0 + jnp.exp(-jnp.abs(z)))


def _norm_proj_body(x_ref, g_ref, w_ref, cs_ref, bd_ref, o_ref, h_scr, *, n_norm):
    j = pl.program_id(1)

    @pl.when(j == 0)
    def _():
        h_scr[...] = _rms_normalize(x_ref[...], g_ref[...]).astype(BF16)

    acc = jnp.dot(h_scr[...], w_ref[...], preferred_element_type=F32)

    def plain():
        o_ref[...] = (acc * cs_ref[...]).astype(o_ref.dtype)

    def head_normed():
        for c in range(COL_TILE // NORM_CHUNK):
            sl = slice(c * NORM_CHUNK, (c + 1) * NORM_CHUNK)
            a = acc[:, sl]
            ss = jnp.dot((a * a).astype(BF16), bd_ref[...], preferred_element_type=F32)
            inv = lax.rsqrt(ss * (1.0 / HEAD_DIM) + RMS_EPS)
            o_ref[:, sl] = (a * inv * cs_ref[:, sl]).astype(o_ref.dtype)

    if n_norm == 0:
        plain()
    else:
        pl.when(j < n_norm)(head_normed)
        pl.when(j >= n_norm)(plain)


def _norm_proj(x, gain, w, layer, col_scale, n_norm):
    t, d = x.shape
    n = w.shape[2]
    head_id = jnp.arange(NORM_CHUNK) // HEAD_DIM
    block_diag = (head_id[:, None] == head_id[None, :]).astype(BF16)
    return pl.pallas_call(
        functools.partial(_norm_proj_body, n_norm=n_norm),
        out_shape=jax.ShapeDtypeStruct((t, n), BF16),
        grid=(t // ROW_TILE, n // COL_TILE),
        in_specs=[
            pl.BlockSpec((ROW_TILE, d), lambda i, j: (i, 0)),
            pl.BlockSpec((1, d), lambda i, j: (0, 0)),
            pl.BlockSpec((None, d, COL_TILE), lambda i, j: (layer, 0, j)),
            pl.BlockSpec((1, COL_TILE), lambda i, j: (0, j)),
            pl.BlockSpec((NORM_CHUNK, NORM_CHUNK), lambda i, j: (0, 0)),
        ],
        out_specs=pl.BlockSpec((ROW_TILE, COL_TILE), lambda i, j: (i, j)),
        scratch_shapes=[pltpu.VMEM((ROW_TILE, d), BF16)],
        compiler_params=_params("parallel", "arbitrary"),
        name="norm_proj",
    )(x, gain.reshape(1, d), w, col_scale.reshape(1, n).astype(F32), block_diag)


def _out_proj_body(x_ref, o_ref, w_ref, y_ref):
    y_ref[...] = x_ref[...] + jnp.dot(o_ref[...], w_ref[...], preferred_element_type=F32)


def _out_proj_residual(x, o, w, layer):
    t, d = x.shape
    return pl.pallas_call(
        _out_proj_body,
        out_shape=jax.ShapeDtypeStruct((t, d), F32),
        grid=(t // ROW_TILE,),
        in_specs=[
            pl.BlockSpec((ROW_TILE, d), lambda i: (i, 0)),
            pl.BlockSpec((ROW_TILE, d), lambda i: (i, 0)),
            pl.BlockSpec((None, d, d), lambda i: (layer, 0, 0)),
        ],
        out_specs=pl.BlockSpec((ROW_TILE, d), lambda i: (i, 0)),
        compiler_params=_params("parallel"),
        name="out_proj",
    )(x, o, w)


def _swiglu_hidden(h, wg, wu):
    g = jnp.dot(h, wg, preferred_element_type=F32)
    u = jnp.dot(h, wu, preferred_element_type=F32)
    return g * (1.0 / (1.0 + jnp.exp(-g))) * u


def _swiglu_down(h, wg_ref, wu_ref, wd_ref):
    y = None
    for c in range(wg_ref.shape[1] // SWIGLU_CHUNK):
        cols = slice(c * SWIGLU_CHUNK, (c + 1) * SWIGLU_CHUNK)
        a = _swiglu_hidden(h, wg_ref[:, cols], wu_ref[:, cols]).astype(BF16)
        part = jnp.dot(a, wd_ref[cols, :], preferred_element_type=F32)
        y = part if y is None else y + part
    return y


def _ffn_body(x_ref, g_ref, wg_ref, wu_ref, wd_ref, y_ref, h_scr):
    f = pl.program_id(1)

    @pl.when(f == 0)
    def _():
        x = x_ref[...]
        h_scr[...] = _rms_normalize(x, g_ref[...]).astype(BF16)
        y_ref[...] = x

    y_ref[...] += _swiglu_down(h_scr[...], wg_ref, wu_ref, wd_ref)


def _ffn_residual(x, gain, w_gate, w_up, w_down, layer):
    t, d = x.shape
    ff = w_gate.shape[2]
    return pl.pallas_call(
        _ffn_body,
        out_shape=jax.ShapeDtypeStruct((t, d), F32),
        grid=(t // ROW_TILE, ff // FF_TILE),
        in_specs=[
            pl.BlockSpec((ROW_TILE, d), lambda i, f: (i, 0)),
            pl.BlockSpec((1, d), lambda i, f: (0, 0)),
            pl.BlockSpec((None, d, FF_TILE), lambda i, f: (layer, 0, f)),
            pl.BlockSpec((None, d, FF_TILE), lambda i, f: (layer, 0, f)),
            pl.BlockSpec((None, FF_TILE, d), lambda i, f: (layer, f, 0)),
        ],
        out_specs=pl.BlockSpec((ROW_TILE, d), lambda i, f: (i, 0)),
        scratch_shapes=[pltpu.VMEM((ROW_TILE, d), BF16)],
        compiler_params=_params("parallel", "arbitrary"),
        name="ffn",
    )(x, gain.reshape(1, d), w_gate, w_up, w_down)


def _pack_bf16_pairs(x):
    half = x.shape[1] // 2
    bits = pltpu.bitcast(x.astype(BF16).astype(F32), jnp.int32)
    return bits[:, :half] | lax.shift_right_logical(bits[:, half:], jnp.int32(16))


def _unpack_bf16_pairs(p):
    left = pltpu.bitcast(p & jnp.int32(-65536), F32)
    right = pltpu.bitcast(lax.shift_left(p, jnp.int32(16)), F32)
    return jnp.concatenate([left, right], axis=1).astype(BF16)


def _router_body(x_ref, g_ref, r_ref, tri_ref, gates_ref, rank_ref, h_ref, count_scr):
    @pl.when(pl.program_id(0) == 0)
    def _():
        count_scr[...] = jnp.zeros_like(count_scr)

    h = _rms_normalize(x_ref[...], g_ref[...])
    h_ref[...] = _pack_bf16_pairs(h)
    logits = jnp.dot(h, r_ref[...], preferred_element_type=F32, precision=lax.Precision.HIGHEST)
    lane = lax.broadcasted_iota(jnp.int32, logits.shape, 1).astype(F32)
    logits = jnp.where(lane < N_EXPERTS, logits, -jnp.inf)
    m1 = jnp.max(logits, axis=-1, keepdims=True)
    i1 = jnp.min(jnp.where(logits == m1, lane, float(LANES)), axis=-1, keepdims=True)
    rest = jnp.where(lane == i1, -jnp.inf, logits)
    m2 = jnp.max(rest, axis=-1, keepdims=True)
    i2 = jnp.min(jnp.where(rest == m2, lane, float(LANES)), axis=-1, keepdims=True)
    e = jnp.exp(m2 - m1)
    g1 = 1.0 / (1.0 + e)
    gates = jnp.where(lane == i1, g1, 0.0) + jnp.where(lane == i2, e * g1, 0.0)
    gates_ref[...] = gates.T[:N_EXPERTS, :]
    chosen = jnp.where((lane == i1) | (lane == i2), 1.0, 0.0)
    inclusive = jnp.dot(tri_ref[...], chosen.astype(BF16), preferred_element_type=F32)
    rank = jnp.where(chosen > 0.0, inclusive - 1.0 + count_scr[...], -1.0)
    rank_ref[...] = rank.T[:N_EXPERTS, :]
    count_scr[...] += inclusive[ROW_TILE - 1:ROW_TILE, :]


def _router(x, gain, router):
    t, d = x.shape
    r = jnp.zeros((d, LANES), F32).at[:, :N_EXPERTS].set(router)
    idx = jnp.arange(ROW_TILE)
    tri = (idx[:, None] >= idx[None, :]).astype(BF16)
    return pl.pallas_call(
        _router_body,
        out_shape=(jax.ShapeDtypeStruct((N_EXPERTS, t), F32), jax.ShapeDtypeStruct((N_EXPERTS, t), F32),
                   jax.ShapeDtypeStruct((t, d // 2), jnp.int32)),
        grid=(t // ROW_TILE,),
        in_specs=[
            pl.BlockSpec((ROW_TILE, d), lambda i: (i, 0)),
            pl.BlockSpec((1, d), lambda i: (0, 0)),
            pl.BlockSpec((d, LANES), lambda i: (0, 0)),
            pl.BlockSpec((ROW_TILE, ROW_TILE), lambda i: (0, 0)),
        ],
        out_specs=(pl.BlockSpec((N_EXPERTS, ROW_TILE), lambda i: (0, i)),
                   pl.BlockSpec((N_EXPERTS, ROW_TILE), lambda i: (0, i)),
                   pl.BlockSpec((ROW_TILE, d // 2), lambda i: (i, 0))),
        scratch_shapes=[pltpu.VMEM((1, LANES), F32)],
        compiler_params=_params("arbitrary"),
        name="router",
    )(x, gain.reshape(1, d), r, tri)


def _sc_mesh():
    return plsc.VectorSubcoreMesh(core_axis_name="core", subcore_axis_name="subcore",
                                  num_cores=SC_CORES, num_subcores=SC_SUBCORES)


def _sc_worker_base(per_worker):
    return (lax.axis_index("subcore") * SC_CORES + lax.axis_index("core")) * per_worker


def _sc_row_gather(table, idx):
    width = table.shape[1]
    n = idx.shape[0]
    per_worker = n // (SC_CORES * SC_SUBCORES)
    n_chunks = per_worker // SC_CHUNK_ROWS
    assert n == n_chunks * SC_CHUNK_ROWS * SC_CORES * SC_SUBCORES and n_chunks % 2 == 0

    @functools.partial(
        pl.kernel, mesh=_sc_mesh(), out_type=jax.ShapeDtypeStruct((n, width), table.dtype),
        scratch_types=[pltpu.VMEM((SC_CHUNK_ROWS,), jnp.int32), pltpu.VMEM((SC_CHUNK_ROWS,), jnp.int32),
                       pltpu.VMEM((SC_CHUNK_ROWS, width), table.dtype),
                       pltpu.VMEM((SC_CHUNK_ROWS, width), table.dtype),
                       pltpu.SemaphoreType.DMA, pltpu.SemaphoreType.DMA],
        name="sc_row_gather")
    def gather(table_hbm, idx_hbm, out_hbm, idx_a, idx_b, rows_a, rows_b, sem_a, sem_b):
        base = _sc_worker_base(per_worker)
        bufs = ((idx_a, rows_a, sem_a), (idx_b, rows_b, sem_b))

        def rows_of(c):
            return pl.ds(pl.multiple_of(base + c * SC_CHUNK_ROWS, SC_CHUNK_ROWS), SC_CHUNK_ROWS)

        def fetch(c, buf):
            idx_v, rows_v, sem = buf
            pltpu.sync_copy(idx_hbm.at[rows_of(c)], idx_v)
            return pltpu.make_async_copy(table_hbm.at[idx_v], rows_v, sem)

        fetch(0, bufs[0]).start()

        @pl.loop(0, n_chunks, step=2)
        def _(c):
            for b in range(2):
                idx_v, rows_v, sem = bufs[b]
                pltpu.make_async_copy(table_hbm.at[idx_v], rows_v, sem).wait()

                @pl.when(c + b + 1 < n_chunks)
                def _():
                    fetch(c + b + 1, bufs[1 - b]).start()

                pltpu.sync_copy(rows_v, out_hbm.at[rows_of(c + b)])

    return gather(table, idx)


def _sc_row_scatter_pair(rows, idx_lo, idx_hi, n_out):
    n, width = rows.shape
    per_worker = n // (SC_CORES * SC_SUBCORES)
    n_chunks = per_worker // SC_CHUNK_ROWS
    assert n == n_chunks * SC_CHUNK_ROWS * SC_CORES * SC_SUBCORES and n_chunks % 2 == 0
    index_scratch = pltpu.VMEM((SC_CHUNK_ROWS,), jnp.int32)
    rows_scratch = pltpu.VMEM((SC_CHUNK_ROWS, width), rows.dtype)

    @functools.partial(
        pl.kernel, mesh=_sc_mesh(), out_type=jax.ShapeDtypeStruct((n_out, width), rows.dtype),
        scratch_types=[index_scratch, index_scratch, rows_scratch, rows_scratch,
                       pltpu.SemaphoreType.DMA, pltpu.SemaphoreType.DMA],
        name="sc_row_scatter")
    def scatter(rows_hbm, lo_hbm, hi_hbm, out_hbm, lo_v, hi_v, rows_a, rows_b, sem_a, sem_b):
        base = _sc_worker_base(per_worker)
        bufs = ((rows_a, sem_a), (rows_b, sem_b))

        def rows_of(c):
            return pl.ds(pl.multiple_of(base + c * SC_CHUNK_ROWS, SC_CHUNK_ROWS), SC_CHUNK_ROWS)

        def load(c, buf):
            rows_v, sem = buf
            return pltpu.make_async_copy(rows_hbm.at[rows_of(c)], rows_v, sem)

        load(0, bufs[0]).start()

        @pl.loop(0, n_chunks, step=2)
        def _(c):
            for b in range(2):
                rows_v, _ = bufs[b]
                load(c + b, bufs[b]).wait()

                @pl.when(c + b + 1 < n_chunks)
                def _():
                    load(c + b + 1, bufs[1 - b]).start()

                pltpu.sync_copy(lo_hbm.at[rows_of(c + b)], lo_v)
                pltpu.sync_copy(hi_hbm.at[rows_of(c + b)], hi_v)
                pltpu.sync_copy(rows_v, out_hbm.at[lo_v])
                pltpu.sync_copy(rows_v, out_hbm.at[hi_v])

    return scatter(rows, idx_lo, idx_hi)


def _expert_ffn_body(te_ref, nu_ref, nv_ref, h_ref, wg_ref, wu_ref, wd_ref, y_ref, acc_scr):
    i = pl.program_id(0)
    f = pl.program_id(1)

    @pl.when(i < nu_ref[0])
    def _():
        row = lax.broadcasted_iota(jnp.int32, h_ref.shape, 0)
        packed = jnp.where(row < nv_ref[i], h_ref[...], 0)
        y = _swiglu_down(_unpack_bf16_pairs(packed), wg_ref, wu_ref, wd_ref)

        @pl.when(f == 0)
        def _():
            acc_scr[...] = y

        @pl.when(f > 0)
        def _():
            acc_scr[...] += y

        @pl.when(f == pl.num_programs(1) - 1)
        def _():
            y_ref[...] = _pack_bf16_pairs(acc_scr[...])


def _expert_ffn(h_sorted, tile_expert, n_used, tile_valid, w_gate, w_up, w_down, layer):
    rows, half = h_sorted.shape
    d = 2 * half
    ff = w_gate.shape[3]
    n_f = ff // MOE_FF_TILE

    def row_map(i, f, te, nu, nv):
        return (jnp.minimum(i, nu[0] - 1), 0)

    def col_step(i, f, nu):
        return jnp.where(i < nu[0], f, n_f - 1)

    grid_spec = pltpu.PrefetchScalarGridSpec(
        num_scalar_prefetch=3,
        grid=(rows // MOE_ROW_TILE, n_f),
        in_specs=[
            pl.BlockSpec((MOE_ROW_TILE, half), row_map),
            pl.BlockSpec((None, None, d, MOE_FF_TILE),
                         lambda i, f, te, nu, nv: (layer, te[i], 0, col_step(i, f, nu))),
            pl.BlockSpec((None, None, d, MOE_FF_TILE),
                         lambda i, f, te, nu, nv: (layer, te[i], 0, col_step(i, f, nu))),
            pl.BlockSpec((None, None, MOE_FF_TILE, d),
                         lambda i, f, te, nu, nv: (layer, te[i], col_step(i, f, nu), 0)),
        ],
        out_specs=pl.BlockSpec((MOE_ROW_TILE, half), row_map),
        scratch_shapes=[pltpu.VMEM((MOE_ROW_TILE, d), F32)],
    )
    return pl.pallas_call(
        _expert_ffn_body,
        out_shape=jax.ShapeDtypeStruct((rows, half), jnp.int32),
        grid_spec=grid_spec,
        compiler_params=_params("arbitrary", "arbitrary"),
        name="expert_ffn",
    )(tile_expert, n_used, tile_valid, h_sorted, w_gate, w_up, w_down)


def _combine_body(x_ref, y_ref, g_ref, o_ref):
    out = x_ref[...]
    for s in range(2):
        column = jnp.broadcast_to(g_ref[s:s + 1, :], (LANES, g_ref.shape[1])).T
        gate = jnp.concatenate([column] * (out.shape[1] // LANES), axis=1)
        out = out + _unpack_bf16_pairs(y_ref[s]).astype(F32) * gate
    o_ref[...] = out


def _combine_residual(x, y_pairs, gates2):
    t, d = x.shape
    return pl.pallas_call(
        _combine_body,
        out_shape=jax.ShapeDtypeStruct((t, d), F32),
        grid=(t // ROW_TILE,),
        in_specs=[
            pl.BlockSpec((ROW_TILE, d), lambda i: (i, 0)),
            pl.BlockSpec((2, ROW_TILE, d // 2), lambda i: (0, i, 0)),
            pl.BlockSpec((2, ROW_TILE), lambda i: (0, i)),
        ],
        out_specs=pl.BlockSpec((ROW_TILE, d), lambda i: (i, 0)),
        compiler_params=_params("parallel"),
        name="moe_combine",
    )(x, y_pairs, gates2)


def _moe_residual(x, gain, router, w_gate, w_up, w_down, layer):
    t, d = x.shape
    gates, rank, h_packed = _router(x, gain, router)
    rank8 = rank.astype(jnp.int32)
    chosen = rank8 >= 0
    counts = jnp.sum(chosen, axis=1, dtype=jnp.int32)
    padded = (counts + MOE_ROW_TILE - 1) // MOE_ROW_TILE * MOE_ROW_TILE
    ends = jnp.cumsum(padded)
    starts = ends - padded
    pos = starts[:, None] + rank8
    max_rows = 2 * t + N_EXPERTS * MOE_ROW_TILE
    pos_lo = jnp.min(jnp.where(chosen, pos, max_rows), axis=0)
    pos_hi = jnp.max(jnp.where(chosen, pos, -1), axis=0)
    gates2 = jnp.stack([jnp.sum(jnp.where(chosen & (pos == pos_lo[None, :]), gates, 0.0), axis=0),
                        jnp.sum(jnp.where(chosen & (pos == pos_hi[None, :]), gates, 0.0), axis=0)])
    n_tiles = max_rows // MOE_ROW_TILE
    n_used = (ends[-1] // MOE_ROW_TILE).astype(jnp.int32)
    tile_start = jnp.minimum(jnp.arange(n_tiles, dtype=jnp.int32), n_used - 1) * MOE_ROW_TILE
    tile_expert = jnp.sum(tile_start[:, None] >= ends[None, :], axis=1, dtype=jnp.int32)
    tile_valid = jnp.clip((starts + counts)[tile_expert] - tile_start, 0, MOE_ROW_TILE).astype(jnp.int32)

    h_sorted = _sc_row_scatter_pair(h_packed, pos_lo, pos_hi, max_rows)
    y_sorted = _expert_ffn(h_sorted, tile_expert, n_used.reshape(1), tile_valid, w_gate, w_up, w_down, layer)
    y_pairs = _sc_row_gather(y_sorted, jnp.concatenate([pos_lo, pos_hi])).reshape(2, t, d // 2)
    return _combine_residual(x, y_pairs, gates2)


def _split_head_pair(q):
    is_first = lax.broadcasted_iota(jnp.int32, (1, LANES), 1) < HEAD_DIM
    zero = jnp.zeros_like(q)
    return jnp.where(is_first, q, zero), jnp.where(is_first, zero, q)


def _merge_head_pair(first, second):
    is_first = lax.broadcasted_iota(jnp.int32, (1, LANES), 1) < HEAD_DIM
    return jnp.where(is_first, first, second)


def _qk(q, k):
    return lax.dot_general(q, k, (((1,), (1,)), ((), ())), preferred_element_type=F32)


def _transpose_values(v_ref, vt_scr):
    for c in range(v_ref.shape[0] // ATT_BLOCK):
        rows = slice(c * ATT_BLOCK, (c + 1) * ATT_BLOCK)
        vt_scr[:, rows] = v_ref[rows, :].astype(F32).T.astype(vt_scr.dtype)


def _sb_body(q_ref, k_ref, v_ref, u_ref, o_ref, vt_scr, acc_scr, carry_scr, z_scr, keep_scr, sum_scr):
    qi = pl.program_id(2)

    @pl.when(qi == 0)
    def _():
        _transpose_values(v_ref, vt_scr)

    upper = u_ref[...]
    acc_scr[...] = jnp.zeros_like(acc_scr)
    carry_scr[...] = jnp.zeros_like(carry_scr)
    key = lax.broadcasted_iota(jnp.int32, (ATT_BLOCK, ATT_BLOCK), 0)
    query = lax.broadcasted_iota(jnp.int32, (ATT_BLOCK, ATT_BLOCK), 1)
    strict = key < query
    qs = [_split_head_pair(q_ref[j * ATT_BLOCK:(j + 1) * ATT_BLOCK, :]) for j in range(ATT_K_PER_Q)]

    def process(j, kb, keep):
        start = pl.multiple_of(kb * ATT_BLOCK, ATT_BLOCK)
        k = k_ref[pl.ds(start, ATT_BLOCK), :]
        for hh in range(2):
            z = _qk(k, qs[j][hh])
            log_beta = z - _softplus(z)
            log_keep = log_beta - z
            if keep is not None:
                log_keep = jnp.where(keep, log_keep, 0.0)
            remain = jnp.dot(upper, log_keep.astype(BF16), preferred_element_type=F32)
            w = jnp.exp(log_beta + remain + carry_scr[j, hh])
            if keep is not None:
                w = jnp.where(keep, w, 0.0)
            carry_scr[j, hh] += jnp.sum(log_keep, axis=0, keepdims=True)
            vt = vt_scr[pl.ds(hh * HEAD_DIM, HEAD_DIM), pl.ds(start, ATT_BLOCK)]
            acc_scr[j, hh] += jnp.dot(vt, w.astype(BF16), preferred_element_type=F32)

    units = []
    for j in range(ATT_K_PER_Q):
        g = qi * ATT_K_PER_Q + j
        has_previous = None if j > 0 else jnp.broadcast_to(g > 0, strict.shape)
        for kb, keep in ((g, strict), (jnp.maximum(g - 1, 0), has_previous)):
            for hh in range(2):
                units.append((j, hh, pl.multiple_of(kb * ATT_BLOCK, ATT_BLOCK), keep))

    for u, (j, hh, start, keep) in enumerate(units):
        z_scr[u] = _qk(k_ref[pl.ds(start, ATT_BLOCK), :], qs[j][hh])

    for u, (j, hh, start, keep) in enumerate(units):
        z = z_scr[u]
        log_beta = z - _softplus(z)
        log_keep = log_beta - z
        if keep is not None:
            log_keep = jnp.where(keep, log_keep, 0.0)
        z_scr[u] = log_beta
        keep_scr[u] = log_keep.astype(BF16)
        sum_scr[u] = jnp.sum(log_keep, axis=0, keepdims=True)

    for u in range(len(units)):
        z_scr[u] += jnp.dot(upper, keep_scr[u], preferred_element_type=F32)

    for u, (j, hh, start, keep) in enumerate(units):
        diagonal = u % 4 < 2
        log_w = z_scr[u] if diagonal else z_scr[u] + sum_scr[u - 2]
        w = jnp.exp(log_w)
        if keep is not None:
            w = jnp.where(keep, w, 0.0)
        keep_scr[u] = w.astype(BF16)

    for u, (j, hh, start, keep) in enumerate(units):
        vt = vt_scr[pl.ds(hh * HEAD_DIM, HEAD_DIM), pl.ds(start, ATT_BLOCK)]
        acc_scr[j, hh] += jnp.dot(vt, keep_scr[u], preferred_element_type=F32)
        carry_scr[j, hh] += sum_scr[u]

    for j in range(ATT_K_PER_Q):
        def more(kb, j=j):
            return (kb >= 0) & (jnp.max(carry_scr[j]) > SB_UNDERFLOW_LOG)

        def step(kb, j=j):
            process(j, kb, None)
            return kb - 1

        lax.while_loop(more, step, qi * ATT_K_PER_Q + j - 2)

    for j in range(ATT_K_PER_Q):
        out_t = jnp.concatenate([acc_scr[j, 0], acc_scr[j, 1]], axis=0)
        o_ref[j * ATT_BLOCK:(j + 1) * ATT_BLOCK, :] = out_t.T.astype(o_ref.dtype)


def _sb_attention(proj, batch, seq):
    t = proj.shape[0]
    nq = seq // ATT_Q_BLOCK
    idx = jnp.arange(ATT_BLOCK)
    upper = (idx[None, :] > idx[:, None]).astype(BF16)
    return pl.pallas_call(
        _sb_body,
        out_shape=jax.ShapeDtypeStruct((t, D_MODEL), BF16),
        grid=(batch, HEAD_PAIRS, nq),
        in_specs=[
            pl.BlockSpec((ATT_Q_BLOCK, LANES), lambda b, p, i: (b * nq + i, p)),
            pl.BlockSpec((seq, LANES), lambda b, p, i: (b, HEAD_PAIRS + p)),
            pl.BlockSpec((seq, LANES), lambda b, p, i: (b, 2 * HEAD_PAIRS + p)),
            pl.BlockSpec((ATT_BLOCK, ATT_BLOCK), lambda b, p, i: (0, 0)),
        ],
        out_specs=pl.BlockSpec((ATT_Q_BLOCK, LANES), lambda b, p, i: (b * nq + i, p)),
        scratch_shapes=[
            pltpu.VMEM((LANES, seq), BF16),
            pltpu.VMEM((ATT_K_PER_Q, 2, HEAD_DIM, ATT_BLOCK), F32),
            pltpu.VMEM((ATT_K_PER_Q, 2, 1, ATT_BLOCK), F32),
            pltpu.VMEM((4 * ATT_K_PER_Q, ATT_BLOCK, ATT_BLOCK), F32),
            pltpu.VMEM((4 * ATT_K_PER_Q, ATT_BLOCK, ATT_BLOCK), BF16),
            pltpu.VMEM((4 * ATT_K_PER_Q, 1, ATT_BLOCK), F32),
        ],
        compiler_params=_params("parallel", "parallel", "arbitrary"),
        name="sb_attention",
    )(proj, proj, proj, upper)


def _fox_gate_body(x_ref, g_ref, w_ref, b_ref, tri_ref, c_ref):
    h = _rms_normalize(x_ref[...], g_ref[...])
    logits = jnp.dot(h, w_ref[...], preferred_element_type=F32, precision=lax.Precision.HIGHEST)
    log_f = -_softplus(-(logits + b_ref[...]))
    seq = log_f.shape[0]
    carry = jnp.zeros((1, LANES), F32)
    for blk in range(seq // ATT_BLOCK):
        rows = slice(blk * ATT_BLOCK, (blk + 1) * ATT_BLOCK)
        c = jnp.dot(tri_ref[...], log_f[rows], preferred_element_type=F32,
                    precision=lax.Precision.HIGHEST) + carry
        c_ref[rows, :] = c
        carry = c[ATT_BLOCK - 1:ATT_BLOCK, :]


def _fox_cum_log_forget(x, gain, w_gate, b_gate, batch, seq):
    t, d = x.shape
    w = jnp.zeros((d, LANES), F32).at[:, :N_HEADS].set(w_gate)
    b = jnp.zeros((1, LANES), F32).at[0, :N_HEADS].set(b_gate)
    idx = jnp.arange(ATT_BLOCK)
    tri = (idx[:, None] >= idx[None, :]).astype(F32)
    return pl.pallas_call(
        _fox_gate_body,
        out_shape=jax.ShapeDtypeStruct((t, LANES), F32),
        grid=(batch,),
        in_specs=[
            pl.BlockSpec((seq, d), lambda i: (i, 0)),
            pl.BlockSpec((1, d), lambda i: (0, 0)),
            pl.BlockSpec((d, LANES), lambda i: (0, 0)),
            pl.BlockSpec((1, LANES), lambda i: (0, 0)),
            pl.BlockSpec((ATT_BLOCK, ATT_BLOCK), lambda i: (0, 0)),
        ],
        out_specs=pl.BlockSpec((seq, LANES), lambda i: (i, 0)),
        compiler_params=_params("parallel"),
        name="fox_gate",
    )(x, gain.reshape(1, d), w, b, tri)


def _fox_body(q_ref, k_ref, v_ref, cq_ref, ck_ref, o_ref, vt_scr, ckb_scr, a_scr, p_scr):
    qi = pl.program_id(2)

    @pl.when(qi == 0)
    def _():
        _transpose_values(v_ref, vt_scr)
        for hh in range(2):
            for c in range(ckb_scr.shape[1] // ATT_BLOCK):
                rows = slice(c * ATT_BLOCK, (c + 1) * ATT_BLOCK)
                ckb_scr[hh, rows, :] = jnp.broadcast_to(ck_ref[0, hh, :, rows], (LANES, ATT_BLOCK)).T

    qs = _split_head_pair(q_ref[...])
    key = lax.broadcasted_iota(jnp.int32, (ATT_BLOCK, ATT_Q_BLOCK), 0)
    query = lax.broadcasted_iota(jnp.int32, (ATT_BLOCK, ATT_Q_BLOCK), 1)
    row_shape = (1, ATT_Q_BLOCK)

    def scores(kb, tops, diagonal):
        start = pl.multiple_of(kb * ATT_BLOCK, ATT_BLOCK)
        k = k_ref[pl.ds(start, ATT_BLOCK), :]
        if diagonal:
            causal = key + (kb * ATT_BLOCK - qi * ATT_Q_BLOCK) <= query
        new_tops = []
        for hh in range(2):
            ck = ckb_scr[hh, pl.ds(start, ATT_BLOCK), :]
            a = _qk(k, qs[hh]) - jnp.concatenate([ck] * (ATT_Q_BLOCK // LANES), axis=1)
            if diagonal:
                a = jnp.where(causal, a, NEG_INF)
            a_scr[hh, kb] = a
            new_tops.append(jnp.maximum(tops[hh], jnp.max(a, axis=0, keepdims=True)))
        return tuple(new_tops)

    def scores_step(it, tops):
        for r in range(ATT_K_PER_Q):
            tops = scores(it * ATT_K_PER_Q + r, tops, False)
        return tops

    tops = lax.fori_loop(0, qi, scores_step, (jnp.full(row_shape, NEG_INF, F32),) * 2)
    for r in range(ATT_K_PER_Q):
        tops = scores(qi * ATT_K_PER_Q + r, tops, True)
    shifts = tuple(cq_ref[0, hh] - (tops[hh] + cq_ref[0, hh]) for hh in range(2))

    def weights_step(it, sums):
        sums = list(sums)
        for r in range(ATT_K_PER_Q):
            kb = it * ATT_K_PER_Q + r
            for hh in range(2):
                p = jnp.exp(a_scr[hh, kb] + shifts[hh])
                p_scr[hh, kb] = p.astype(BF16)
                sums[hh] = sums[hh] + jnp.sum(p, axis=0, keepdims=True)
        return tuple(sums)

    sums = lax.fori_loop(0, qi + 1, weights_step, (jnp.zeros(row_shape, F32),) * 2)

    def values_step(it, accs):
        accs = list(accs)
        for r in range(ATT_K_PER_Q):
            kb = it * ATT_K_PER_Q + r
            start = pl.multiple_of(kb * ATT_BLOCK, ATT_BLOCK)
            for hh in range(2):
                vt = vt_scr[pl.ds(hh * HEAD_DIM, HEAD_DIM), pl.ds(start, ATT_BLOCK)]
                accs[hh] = accs[hh] + jnp.dot(vt, p_scr[hh, kb], preferred_element_type=F32)
        return tuple(accs)

    accs = lax.fori_loop(0, qi + 1, values_step, (jnp.zeros((HEAD_DIM, ATT_Q_BLOCK), F32),) * 2)
    out_t = jnp.concatenate([accs[0] / sums[0], accs[1] / sums[1]], axis=0)
    o_ref[...] = out_t.T.astype(o_ref.dtype)


def _fox_attention(proj, cum, batch, seq):
    t = proj.shape[0]
    nq = seq // ATT_Q_BLOCK
    cum_h = cum[:, :N_HEADS].reshape(batch, seq, N_HEADS).transpose(0, 2, 1)
    cum_rows = cum_h.reshape(batch, N_HEADS, 1, seq)
    return pl.pallas_call(
        _fox_body,
        out_shape=jax.ShapeDtypeStruct((t, D_MODEL), BF16),
        grid=(batch, HEAD_PAIRS, nq),
        in_specs=[
            pl.BlockSpec((ATT_Q_BLOCK, LANES), lambda b, p, i: (b * nq + i, p)),
            pl.BlockSpec((seq, LANES), lambda b, p, i: (b, HEAD_PAIRS + p)),
            pl.BlockSpec((seq, LANES), lambda b, p, i: (b, 2 * HEAD_PAIRS + p)),
            pl.BlockSpec((1, 2, 1, ATT_Q_BLOCK), lambda b, p, i: (b, p, 0, i)),
            pl.BlockSpec((1, 2, 1, seq), lambda b, p, i: (b, p, 0, 0)),
        ],
        out_specs=pl.BlockSpec((ATT_Q_BLOCK, LANES), lambda b, p, i: (b * nq + i, p)),
        scratch_shapes=[
            pltpu.VMEM((LANES, seq), BF16),
            pltpu.VMEM((2, seq, LANES), F32),
            pltpu.VMEM((2, seq // ATT_BLOCK, ATT_BLOCK, ATT_Q_BLOCK), F32),
            pltpu.VMEM((2, seq // ATT_BLOCK, ATT_BLOCK, ATT_Q_BLOCK), BF16),
        ],
        compiler_params=_params("parallel", "parallel", "arbitrary"),
        name="fox_attention",
    )(proj, proj, proj, cum_rows, cum_rows)


def _band_body(q_ref, k_ref, v_ref, bias_ref, o_ref, s_ref, vt_scr, logit_scr, p_scr, inv_scr, *, sub_len):
    seq = v_ref.shape[0]
    n_blocks = seq // DIL_BLOCK
    _transpose_values(v_ref, vt_scr)

    def key_rows(n):
        first = (n * DIL_BLOCK) % sub_len == 0
        return first, slice((n if first else n - 1) * DIL_BLOCK, (n + 1) * DIL_BLOCK)

    for n in range(n_blocks):
        first, k_rows = key_rows(n)
        qs = _split_head_pair(q_ref[n * DIL_BLOCK:(n + 1) * DIL_BLOCK, :])
        k = k_ref[k_rows, :]
        for hh in range(2):
            bias = bias_ref[hh, DIL_BLOCK:, :] if first else bias_ref[hh]
            logit_scr[2 * n + hh, :k.shape[0], :] = _qk(k, qs[hh]) + bias

    for n in range(n_blocks):
        first, k_rows = key_rows(n)
        n_keys = k_rows.stop - k_rows.start
        for hh in range(2):
            logits = logit_scr[2 * n + hh, :n_keys, :]
            m = jnp.max(logits, axis=0, keepdims=True)
            p = jnp.exp(logits - m)
            l = jnp.sum(p, axis=0, keepdims=True)
            p_scr[2 * n + hh, :n_keys, :] = p.astype(BF16)
            inv_scr[2 * n + hh] = 1.0 / l
            s_ref[0, 0, hh:hh + 1, n * DIL_BLOCK:(n + 1) * DIL_BLOCK] = m + jnp.log(l)

    for n in range(n_blocks):
        first, k_rows = key_rows(n)
        n_keys = k_rows.stop - k_rows.start
        outs = []
        for hh in range(2):
            vt = vt_scr[hh * HEAD_DIM:(hh + 1) * HEAD_DIM, k_rows]
            o = jnp.dot(vt, p_scr[2 * n + hh, :n_keys, :], preferred_element_type=F32)
            outs.append(o * inv_scr[2 * n + hh])
        o_ref[n * DIL_BLOCK:(n + 1) * DIL_BLOCK, :] = jnp.concatenate(outs, axis=0).T.astype(o_ref.dtype)


def _band_attention(qk, v, bias_t, batch, seq, sub_len):
    t = v.shape[0]
    return pl.pallas_call(
        functools.partial(_band_body, sub_len=sub_len),
        out_shape=(jax.ShapeDtypeStruct((t, D_MODEL), BF16),
                   jax.ShapeDtypeStruct((batch, HEAD_PAIRS, 2, seq), F32)),
        grid=(batch, HEAD_PAIRS),
        in_specs=[
            pl.BlockSpec((None, seq, LANES), lambda b, p: (0, b, p)),
            pl.BlockSpec((None, seq, LANES), lambda b, p: (1, b, p)),
            pl.BlockSpec((seq, LANES), lambda b, p: (b, p)),
            pl.BlockSpec((2, 2 * DIL_BLOCK, DIL_BLOCK), lambda b, p: (p, 0, 0)),
        ],
        out_specs=(pl.BlockSpec((seq, LANES), lambda b, p: (b, p)),
                   pl.BlockSpec((1, 1, 2, seq), lambda b, p: (b, p, 0, 0))),
        scratch_shapes=[pltpu.VMEM((LANES, seq), BF16),
                        pltpu.VMEM((2 * seq // DIL_BLOCK, 2 * DIL_BLOCK, DIL_BLOCK), F32),
                        pltpu.VMEM((2 * seq // DIL_BLOCK, 2 * DIL_BLOCK, DIL_BLOCK), BF16),
                        pltpu.VMEM((2 * seq // DIL_BLOCK, 1, DIL_BLOCK), F32)],
        compiler_params=_params("parallel", "parallel"),
        name="band_attention",
    )(qk, qk, v, bias_t)


def _softmax_merge_body(o1_ref, o2_ref, o3_ref, s_ref, e_ref, o_ref):
    s = s_ref[...]
    groups = [s, pltpu.roll(s, LANES - N_HEADS, axis=1), pltpu.roll(s, LANES - 2 * N_HEADS, axis=1)]
    top = jnp.maximum(jnp.maximum(groups[0], groups[1]), groups[2])
    weights = [jnp.exp(g - top) for g in groups]
    inv = 1.0 / (weights[0] + weights[1] + weights[2])
    out = jnp.zeros(o_ref.shape, F32)
    for w, part in zip(weights, (o1_ref, o2_ref, o3_ref)):
        spread = jnp.dot((w * inv).astype(BF16), e_ref[...], preferred_element_type=F32)
        out = out + spread * part[...].astype(F32)
    o_ref[...] = out.astype(o_ref.dtype)


def _softmax_merge(outs, s_all):
    t, d = outs[0].shape
    head_of = jnp.arange(d) // HEAD_DIM
    expand = (jnp.arange(LANES)[:, None] == head_of[None, :]).astype(BF16)
    rows = pl.BlockSpec((ROW_TILE, d), lambda i: (i, 0))
    return pl.pallas_call(
        _softmax_merge_body,
        out_shape=jax.ShapeDtypeStruct((t, d), BF16),
        grid=(t // ROW_TILE,),
        in_specs=[rows, rows, rows,
                  pl.BlockSpec((ROW_TILE, LANES), lambda i: (i, 0)),
                  pl.BlockSpec((LANES, d), lambda i: (0, 0))],
        out_specs=rows,
        compiler_params=_params("parallel"),
        name="softmax_merge",
    )(*outs, s_all, expand)


def _t5_causal_bucket(distance):
    max_exact = N_REL_BUCKETS // 2
    d = jnp.maximum(distance, 1).astype(F32)
    log_b = max_exact + (jnp.log(d / max_exact) / math.log(REL_MAX_DISTANCE / max_exact)
                         * (N_REL_BUCKETS - max_exact)).astype(jnp.int32)
    log_b = jnp.minimum(log_b, N_REL_BUCKETS - 1)
    return jnp.where(distance < max_exact, distance, log_b)


def _dilated_bias(rel_bias):
    kj = jnp.arange(2 * DIL_BLOCK, dtype=jnp.int32)
    qi = jnp.arange(DIL_BLOCK, dtype=jnp.int32)
    delta = qi[None, :] + DIL_BLOCK - kj[:, None]
    in_band = (delta >= 0) & (delta <= DIL_SPAN)
    buckets = jnp.stack([_t5_causal_bucket(jnp.maximum(delta, 0) * dil) for _, dil in DILATED_PAIRS])
    one_hot = (buckets[..., None] == jnp.arange(N_REL_BUCKETS)).astype(F32)
    bias = jnp.einsum("gkqb,bh->ghkq", one_hot, rel_bias.astype(F32), precision=lax.Precision.HIGHEST)
    return jnp.where(in_band[None, None], bias, NEG_INF)


def _dilated_proj_body(x_ref, g_ref, w_ref, cs_ref, bd_ref, *rest):
    n_groups = len(DILATED_PAIRS)
    qk_refs, v_refs = rest[:n_groups], rest[n_groups:2 * n_groups]
    h_scr, res_scr = rest[2 * n_groups:]
    j = pl.program_id(1)

    @pl.when(j == 0)
    def _():
        h_scr[...] = _rms_normalize(x_ref[...], g_ref[...]).astype(BF16)

    acc = jnp.dot(h_scr[...], w_ref[...], preferred_element_type=F32)

    def keep(res, first_lane):
        for c in range(res.shape[1] // LANES):
            res_scr[first_lane // LANES + c] = res[:, c * LANES:(c + 1) * LANES]

    @pl.when(j < 2 * n_groups)
    def _():
        for c in range(COL_TILE // NORM_CHUNK):
            sl = slice(c * NORM_CHUNK, (c + 1) * NORM_CHUNK)
            a = acc[:, sl]
            ss = jnp.dot((a * a).astype(BF16), bd_ref[...], preferred_element_type=F32)
            keep(a * lax.rsqrt(ss * (1.0 / HEAD_DIM) + RMS_EPS) * cs_ref[:, sl], c * NORM_CHUNK)

    @pl.when(j == 2 * n_groups)
    def _():
        keep(acc * cs_ref[...], 0)

    def write_classes(ref, dil):
        rows = res_scr.shape[1] // dil
        for c in range(res_scr.shape[0]):
            lanes = slice(c * LANES, (c + 1) * LANES)
            if dil == 1:
                ref[:, lanes] = res_scr[c].astype(ref.dtype)
            else:
                for r in range(dil):
                    ref[r, :, lanes] = res_scr[c, pl.ds(r, rows, stride=dil), :].astype(ref.dtype)

    for g, (_, dil) in enumerate(DILATED_PAIRS):
        pl.when(j // 2 == g)(functools.partial(write_classes, qk_refs[g], dil))
        pl.when(j == 2 * n_groups)(functools.partial(write_classes, v_refs[g], dil))


def _dilated_proj(x, gain, w, layer, col_scale, batch, seq):
    t, d = x.shape
    n = w.shape[2]
    n_groups = len(DILATED_PAIRS)
    tiles_per_seq = seq // DIL_ROW_TILE
    head_id = jnp.arange(NORM_CHUNK) // HEAD_DIM
    block_diag = (head_id[:, None] == head_id[None, :]).astype(BF16)
    out_shape, out_specs = [], []
    for kind in ("qk", "v"):
        for g, (_, dil) in enumerate(DILATED_PAIRS):
            rows = DIL_ROW_TILE // dil
            if kind == "qk":
                shape = (2, batch, dil, seq // dil, d)
                block = (None, None, dil, rows, d)
                index = lambda i, j, g=g: (jnp.clip(j - 2 * g, 0, 1), i // tiles_per_seq, 0, i % tiles_per_seq, 0)
            else:
                shape = (batch, dil, seq // dil, d)
                block = (None, dil, rows, d)
                index = lambda i, j: (i // tiles_per_seq, 0, i % tiles_per_seq, 0)
            if dil == 1:
                block = block[:-3] + (None,) + block[-2:]
            out_shape.append(jax.ShapeDtypeStruct(shape, BF16))
            out_specs.append(pl.BlockSpec(block, index))
    outs = pl.pallas_call(
        _dilated_proj_body,
        out_shape=tuple(out_shape),
        grid=(t // DIL_ROW_TILE, n // COL_TILE),
        in_specs=[
            pl.BlockSpec((DIL_ROW_TILE, d), lambda i, j: (i, 0)),
            pl.BlockSpec((1, d), lambda i, j: (0, 0)),
            pl.BlockSpec((None, d, COL_TILE), lambda i, j: (layer, 0, j)),
            pl.BlockSpec((1, COL_TILE), lambda i, j: (0, j)),
            pl.BlockSpec((NORM_CHUNK, NORM_CHUNK), lambda i, j: (0, 0)),
        ],
        out_specs=tuple(out_specs),
        scratch_shapes=[pltpu.VMEM((DIL_ROW_TILE, d), BF16),
                        pltpu.VMEM((COL_TILE // LANES, DIL_ROW_TILE, LANES), F32)],
        compiler_params=_params("parallel", "arbitrary"),
        name="dilated_proj",
    )(x, gain.reshape(1, d), w, col_scale.reshape(1, n).astype(F32), block_diag)
    qk = [o.reshape(2, t, d) for o in outs[:n_groups]]
    v = [o.reshape(t, d) for o in outs[n_groups:]]
    return qk, v


def _dilated_attention(qk, v, rel_bias, batch, seq):
    t = v[0].shape[0]
    n_groups = len(DILATED_PAIRS)
    bias_t = _dilated_bias(rel_bias)
    outs, stats = [], []
    for g, (_, dil) in enumerate(DILATED_PAIRS):
        sub_len = seq // dil
        o, s = _band_attention(qk[g], v[g], bias_t[g], batch, seq, sub_len)
        o = o.reshape(batch, dil, sub_len, D_MODEL).transpose(0, 2, 1, 3).reshape(t, D_MODEL)
        s = s.reshape(batch, HEAD_PAIRS, 2, dil, sub_len).swapaxes(3, 4)
        outs.append(o)
        stats.append(s.reshape(batch, N_HEADS, seq).transpose(0, 2, 1).reshape(t, N_HEADS))
    stats.append(jnp.zeros((t, LANES - n_groups * N_HEADS), F32))
    return _softmax_merge(outs, jnp.concatenate(stats, axis=1))


def _tile_heads(v):
    return jnp.tile(v.astype(F32), N_HEADS)


def _sb_mixer(x, gain, w_qkv, w_o, layer, batch, seq):
    ones = jnp.ones((D_MODEL,), F32)
    col_scale = jnp.concatenate([ones * QK_SCALE, ones, ones])
    proj = _norm_proj(x, gain, w_qkv, layer, col_scale, 0)
    return _out_proj_residual(x, _sb_attention(proj, batch, seq), w_o, layer)


def _dilated_mixer(x, gain, w_in, q_norm, k_norm, rel_bias, w_o, layer, batch, seq):
    scales = []
    for g in range(len(DILATED_PAIRS)):
        scales += [_tile_heads(q_norm[g]) * QK_SCALE, _tile_heads(k_norm[g])]
    scales.append(jnp.ones((D_MODEL,), F32))
    qk, v = _dilated_proj(x, gain, w_in, layer, jnp.concatenate(scales), batch, seq)
    return _out_proj_residual(x, _dilated_attention(qk, v, rel_bias, batch, seq), w_o, layer)


def _fox_mixer(x, gain, w_qkv, w_gate, b_f, q_norm, k_norm, w_o, layer, batch, seq):
    col_scale = jnp.concatenate([_tile_heads(q_norm) * QK_SCALE, _tile_heads(k_norm),
                                 jnp.ones((D_MODEL,), F32)])
    proj = _norm_proj(x, gain, w_qkv, layer, col_scale, 2)
    cum = _fox_cum_log_forget(x, gain, w_gate, b_f, batch, seq)
    return _out_proj_residual(x, _fox_attention(proj, cum, batch, seq), w_o, layer)


def kernel(x, sb_w_qkv, sb_w_o, dil_w_in, dil_q_norm, dil_k_norm, dil_w_o, fox_w_in, fox_b_f,
           fox_q_norm, fox_k_norm, fox_w_o, rel_bias, attn_norm, ffn_norm, mlp_w_gate, mlp_w_up,
           mlp_w_down, moe_router, moe_w_gate, moe_w_up, moe_w_down):
    batch, seq, d = x.shape
    depth = attn_norm.shape[0]
    sb_w_qkv, sb_w_o, dil_w_in, dil_w_o, fox_w_o, mlp_w_gate, mlp_w_up, mlp_w_down = (
        w.astype(BF16) for w in (sb_w_qkv, sb_w_o, dil_w_in, dil_w_o, fox_w_o, mlp_w_gate, mlp_w_up,
                                 mlp_w_down))
    moe_w_gate, moe_w_up, moe_w_down = (w.astype(BF16) for w in (moe_w_gate, moe_w_up, moe_w_down))
    fox_w_qkv = fox_w_in[:, :, :3 * D_MODEL].astype(BF16)
    h = x.reshape(batch * seq, d)
    for i in range(depth):
        kind, j = i % 3, i // 3
        if kind == 0:
            h = _sb_mixer(h, attn_norm[i], sb_w_qkv, sb_w_o, j, batch, seq)
        elif kind == 1:
            h = _dilated_mixer(h, attn_norm[i], dil_w_in, dil_q_norm[j], dil_k_norm[j], rel_bias,
                               dil_w_o, j, batch, seq)
        else:
            h = _fox_mixer(h, attn_norm[i], fox_w_qkv, fox_w_in[j, :, 3 * D_MODEL:], fox_b_f[j],
                           fox_q_norm[j], fox_k_norm[j], fox_w_o, j, batch, seq)
        f = i // 2
        if i % 2 == 0:
            h = _ffn_residual(h, ffn_norm[i], mlp_w_gate, mlp_w_up, mlp_w_down, f)
        else:
            h = _moe_residual(h, ffn_norm[i], moe_router[f], moe_w_gate, moe_w_up, moe_w_down, f)
    return h.reshape(batch, seq, d)
```

```python
import functools
import math

import jax
import jax.numpy as jnp
from jax import lax
from jax.experimental import pallas as pl
from jax.experimental.pallas import tpu as pltpu
from jax.experimental.pallas import tpu_sc as plsc

D_MODEL = 1024
N_HEADS = 16
HEAD_DIM = 64
LANES = 128
HEAD_PAIRS = D_MODEL // LANES
D_FF = 3584
N_EXPERTS = 8
N_REL_BUCKETS = 32
REL_MAX_DISTANCE = 2048
DILATED_PAIRS = ((128, 1), (512, 4), (2048, 16))
DIL_SPAN = 128
RMS_EPS = 1e-6
NEG_INF = -1e30
SB_UNDERFLOW_LOG = -104.0
QK_SCALE = 1.0 / math.sqrt(HEAD_DIM)

ROW_TILE = 1024
COL_TILE = 1024
FF_TILE = 1792
SWIGLU_CHUNK = 256
MOE_ROW_TILE = 512
MOE_FF_TILE = 1792
SC_CORES = 2
SC_SUBCORES = 16
SC_CHUNK_ROWS = 64
NORM_CHUNK = 256
ATT_BLOCK = 256
ATT_Q_BLOCK = 512
ATT_K_PER_Q = ATT_Q_BLOCK // ATT_BLOCK
DIL_BLOCK = 128
DIL_ROW_TILE = 512
VMEM_LIMIT = 56 * 1024 * 1024

F32 = jnp.float32
BF16 = jnp.bfloat16


def _params(*semantics):
    return pltpu.CompilerParams(dimension_semantics=semantics, vmem_limit_bytes=VMEM_LIMIT)


def _rms_normalize(x, gain):
    inv = lax.rsqrt(jnp.mean(x * x, axis=-1, keepdims=True) + RMS_EPS)
    return x * inv * gain


def _softplus(z):
    return jnp.maximum(z, 0.0) + jnp.log(1.0 + jnp.exp(-jnp.abs(z)))


def _norm_proj_body(x_ref, g_ref, w_ref, cs_ref, bd_ref, o_ref, h_scr, *, n_norm):
    j = pl.program_id(1)

    @pl.when(j == 0)
    def _():
        h_scr[...] = _rms_normalize(x_ref[...], g_ref[...]).astype(BF16)

    acc = jnp.dot(h_scr[...], w_ref[...], preferred_element_type=F32)

    def plain():
        o_ref[...] = (acc * cs_ref[...]).astype(o_ref.dtype)

    def head_normed():
        for c in range(COL_TILE // NORM_CHUNK):
            sl = slice(c * NORM_CHUNK, (c + 1) * NORM_CHUNK)
            a = acc[:, sl]
            ss = jnp.dot((a * a).astype(BF16), bd_ref[...], preferred_element_type=F32)
            inv = lax.rsqrt(ss * (1.0 / HEAD_DIM) + RMS_EPS)
            o_ref[:, sl] = (a * inv * cs_ref[:, sl]).astype(o_ref.dtype)

    if n_norm == 0:
        plain()
    else:
        pl.when(j < n_norm)(head_normed)
        pl.when(j >= n_norm)(plain)


def _norm_proj(x, gain, w, layer, col_scale, n_norm):
    t, d = x.shape
    n = w.shape[2]
    head_id = jnp.arange(NORM_CHUNK) // HEAD_DIM
    block_diag = (head_id[:, None] == head_id[None, :]).astype(BF16)
    return pl.pallas_call(
        functools.partial(_norm_proj_body, n_norm=n_norm),
        out_shape=jax.ShapeDtypeStruct((t, n), BF16),
        grid=(t // ROW_TILE, n // COL_TILE),
        in_specs=[
            pl.BlockSpec((ROW_TILE, d), lambda i, j: (i, 0)),
            pl.BlockSpec((1, d), lambda i, j: (0, 0)),
            pl.BlockSpec((None, d, COL_TILE), lambda i, j: (layer, 0, j)),
            pl.BlockSpec((1, COL_TILE), lambda i, j: (0, j)),
            pl.BlockSpec((NORM_CHUNK, NORM_CHUNK), lambda i, j: (0, 0)),
        ],
        out_specs=pl.BlockSpec((ROW_TILE, COL_TILE), lambda i, j: (i, j)),
        scratch_shapes=[pltpu.VMEM((ROW_TILE, d), BF16)],
        compiler_params=_params("parallel", "arbitrary"),
        name="norm_proj",
    )(x, gain.reshape(1, d), w, col_scale.reshape(1, n).astype(F32), block_diag)


def _out_proj_body(x_ref, o_ref, w_ref, y_ref):
    y_ref[...] = x_ref[...] + jnp.dot(o_ref[...], w_ref[...], preferred_element_type=F32)


def _out_proj_residual(x, o, w, layer):
    t, d = x.shape
    return pl.pallas_call(
        _out_proj_body,
        out_shape=jax.ShapeDtypeStruct((t, d), F32),
        grid=(t // ROW_TILE,),
        in_specs=[
            pl.BlockSpec((ROW_TILE, d), lambda i: (i, 0)),
            pl.BlockSpec((ROW_TILE, d), lambda i: (i, 0)),
            pl.BlockSpec((None, d, d), lambda i: (layer, 0, 0)),
        ],
        out_specs=pl.BlockSpec((ROW_TILE, d), lambda i: (i, 0)),
        compiler_params=_params("parallel"),
        name="out_proj",
    )(x, o, w)


def _swiglu_hidden(h, wg, wu):
    g = jnp.dot(h, wg, preferred_element_type=F32)
    u = jnp.dot(h, wu, preferred_element_type=F32)
    return g * (1.0 / (1.0 + jnp.exp(-g))) * u


def _swiglu_down(h, wg_ref, wu_ref, wd_ref):
    y = None
    for c in range(wg_ref.shape[1] // SWIGLU_CHUNK):
        cols = slice(c * SWIGLU_CHUNK, (c + 1) * SWIGLU_CHUNK)
        a = _swiglu_hidden(h, wg_ref[:, cols], wu_ref[:, cols]).astype(BF16)
        part = jnp.dot(a, wd_ref[cols, :], preferred_element_type=F32)
        y = part if y is None else y + part
    return y


def _ffn_body(x_ref, g_ref, wg_ref, wu_ref, wd_ref, y_ref, h_scr):
    f = pl.program_id(1)

    @pl.when(f == 0)
    def _():
        x = x_ref[...]
        h_scr[...] = _rms_normalize(x, g_ref[...]).astype(BF16)
        y_ref[...] = x

    y_ref[...] += _swiglu_down(h_scr[...], wg_ref, wu_ref, wd_ref)


def _ffn_residual(x, gain, w_gate, w_up, w_down, layer):
    t, d = x.shape
    ff = w_gate.shape[2]
    return pl.pallas_call(
        _ffn_body,
        out_shape=jax.ShapeDtypeStruct((t, d), F32),
        grid=(t // ROW_TILE, ff // FF_TILE),
        in_specs=[
            pl.BlockSpec((ROW_TILE, d), lambda i, f: (i, 0)),
            pl.BlockSpec((1, d), lambda i, f: (0, 0)),
            pl.BlockSpec((None, d, FF_TILE), lambda i, f: (layer, 0, f)),
            pl.BlockSpec((None, d, FF_TILE), lambda i, f: (layer, 0, f)),
            pl.BlockSpec((None, FF_TILE, d), lambda i, f: (layer, f, 0)),
        ],
        out_specs=pl.BlockSpec((ROW_TILE, d), lambda i, f: (i, 0)),
        scratch_shapes=[pltpu.VMEM((ROW_TILE, d), BF16)],
        compiler_params=_params("parallel", "arbitrary"),
        name="ffn",
    )(x, gain.reshape(1, d), w_gate, w_up, w_down)


def _pack_bf16_pairs(x):
    half = x.shape[1] // 2
    bits = pltpu.bitcast(x.astype(BF16).astype(F32), jnp.int32)
    return bits[:, :half] | lax.shift_right_logical(bits[:, half:], jnp.int32(16))


def _unpack_bf16_pairs(p):
    left = pltpu.bitcast(p & jnp.int32(-65536), F32)
    right = pltpu.bitcast(lax.shift_left(p, jnp.int32(16)), F32)
    return jnp.concatenate([left, right], axis=1).astype(BF16)


def _router_body(x_ref, g_ref, r_ref, tri_ref, gates_ref, rank_ref, h_ref, count_scr):
    @pl.when(pl.program_id(0) == 0)
    def _():
        count_scr[...] = jnp.zeros_like(count_scr)

    h = _rms_normalize(x_ref[...], g_ref[...])
    h_ref[...] = _pack_bf16_pairs(h)
    logits = jnp.dot(h, r_ref[...], preferred_element_type=F32, precision=lax.Precision.HIGHEST)
    lane = lax.broadcasted_iota(jnp.int32, logits.shape, 1).astype(F32)
    logits = jnp.where(lane < N_EXPERTS, logits, -jnp.inf)
    m1 = jnp.max(logits, axis=-1, keepdims=True)
    i1 = jnp.min(jnp.where(logits == m1, lane, float(LANES)), axis=-1, keepdims=True)
    rest = jnp.where(lane == i1, -jnp.inf, logits)
    m2 = jnp.max(rest, axis=-1, keepdims=True)
    i2 = jnp.min(jnp.where(rest == m2, lane, float(LANES)), axis=-1, keepdims=True)
    e = jnp.exp(m2 - m1)
    g1 = 1.0 / (1.0 + e)
    gates = jnp.where(lane == i1, g1, 0.0) + jnp.where(lane == i2, e * g1, 0.0)
    gates_ref[...] = gates.T[:N_EXPERTS, :]
    chosen = jnp.where((lane == i1) | (lane == i2), 1.0, 0.0)
    inclusive = jnp.dot(tri_ref[...], chosen.astype(BF16), preferred_element_type=F32)
    rank = jnp.where(chosen > 0.0, inclusive - 1.0 + count_scr[...], -1.0)
    rank_ref[...] = rank.T[:N_EXPERTS, :]
    count_scr[...] += inclusive[ROW_TILE - 1:ROW_TILE, :]


def _router(x, gain, router):
    t, d = x.shape
    r = jnp.zeros((d, LANES), F32).at[:, :N_EXPERTS].set(router)
    idx = jnp.arange(ROW_TILE)
    tri = (idx[:, None] >= idx[None, :]).astype(BF16)
    return pl.pallas_call(
        _router_body,
        out_shape=(jax.ShapeDtypeStruct((N_EXPERTS, t), F32), jax.ShapeDtypeStruct((N_EXPERTS, t), F32),
                   jax.ShapeDtypeStruct((t, d // 2), jnp.int32)),
        grid=(t // ROW_TILE,),
        in_specs=[
            pl.BlockSpec((ROW_TILE, d), lambda i: (i, 0)),
            pl.BlockSpec((1, d), lambda i: (0, 0)),
            pl.BlockSpec((d, LANES), lambda i: (0, 0)),
            pl.BlockSpec((ROW_TILE, ROW_TILE), lambda i: (0, 0)),
        ],
        out_specs=(pl.BlockSpec((N_EXPERTS, ROW_TILE), lambda i: (0, i)),
                   pl.BlockSpec((N_EXPERTS, ROW_TILE), lambda i: (0, i)),
                   pl.BlockSpec((ROW_TILE, d // 2), lambda i: (i, 0))),
        scratch_shapes=[pltpu.VMEM((1, LANES), F32)],
        compiler_params=_params("arbitrary"),
        name="router",
    )(x, gain.reshape(1, d), r, tri)


def _sc_mesh():
    return plsc.VectorSubcoreMesh(core_axis_name="core", subcore_axis_name="subcore",
                                  num_cores=SC_CORES, num_subcores=SC_SUBCORES)


def _sc_worker_base(per_worker):
    return (lax.axis_index("subcore") * SC_CORES + lax.axis_index("core")) * per_worker


def _sc_row_gather(table, idx):
    width = table.shape[1]
    n = idx.shape[0]
    per_worker = n // (SC_CORES * SC_SUBCORES)
    n_chunks = per_worker // SC_CHUNK_ROWS
    assert n == n_chunks * SC_CHUNK_ROWS * SC_CORES * SC_SUBCORES and n_chunks % 2 == 0

    @functools.partial(
        pl.kernel, mesh=_sc_mesh(), out_type=jax.ShapeDtypeStruct((n, width), table.dtype),
        scratch_types=[pltpu.VMEM((SC_CHUNK_ROWS,), jnp.int32), pltpu.VMEM((SC_CHUNK_ROWS,), jnp.int32),
                       pltpu.VMEM((SC_CHUNK_ROWS, width), table.dtype),
                       pltpu.VMEM((SC_CHUNK_ROWS, width), table.dtype),
                       pltpu.SemaphoreType.DMA, pltpu.SemaphoreType.DMA],
        name="sc_row_gather")
    def gather(table_hbm, idx_hbm, out_hbm, idx_a, idx_b, rows_a, rows_b, sem_a, sem_b):
        base = _sc_worker_base(per_worker)
        bufs = ((idx_a, rows_a, sem_a), (idx_b, rows_b, sem_b))

        def rows_of(c):
            return pl.ds(pl.multiple_of(base + c * SC_CHUNK_ROWS, SC_CHUNK_ROWS), SC_CHUNK_ROWS)

        def fetch(c, buf):
            idx_v, rows_v, sem = buf
            pltpu.sync_copy(idx_hbm.at[rows_of(c)], idx_v)
            return pltpu.make_async_copy(table_hbm.at[idx_v], rows_v, sem)

        fetch(0, bufs[0]).start()

        @pl.loop(0, n_chunks, step=2)
        def _(c):
            for b in range(2):
                idx_v, rows_v, sem = bufs[b]
                pltpu.make_async_copy(table_hbm.at[idx_v], rows_v, sem).wait()

                @pl.when(c + b + 1 < n_chunks)
                def _():
                    fetch(c + b + 1, bufs[1 - b]).start()

                pltpu.sync_copy(rows_v, out_hbm.at[rows_of(c + b)])

    return gather(table, idx)


def _sc_row_scatter_pair(rows, idx_lo, idx_hi, n_out):
    n, width = rows.shape
    per_worker = n // (SC_CORES * SC_SUBCORES)
    n_chunks = per_worker // SC_CHUNK_ROWS
    assert n == n_chunks * SC_CHUNK_ROWS * SC_CORES * SC_SUBCORES and n_chunks % 2 == 0
    index_scratch = pltpu.VMEM((SC_CHUNK_ROWS,), jnp.int32)
    rows_scratch = pltpu.VMEM((SC_CHUNK_ROWS, width), rows.dtype)

    @functools.partial(
        pl.kernel, mesh=_sc_mesh(), out_type=jax.ShapeDtypeStruct((n_out, width), rows.dtype),
        scratch_types=[index_scratch, index_scratch, rows_scratch, rows_scratch,
                       pltpu.SemaphoreType.DMA, pltpu.SemaphoreType.DMA],
        name="sc_row_scatter")
    def scatter(rows_hbm, lo_hbm, hi_hbm, out_hbm, lo_v, hi_v, rows_a, rows_b, sem_a, sem_b):
        base = _sc_worker_base(per_worker)
        bufs = ((rows_a, sem_a), (rows_b, sem_b))

        def rows_of(c):
            return pl.ds(pl.multiple_of(base + c * SC_CHUNK_ROWS, SC_CHUNK_ROWS), SC_CHUNK_ROWS)

        def load(c, buf):
            rows_v, sem = buf
            return pltpu.make_async_copy(rows_hbm.at[rows_of(c)], rows_v, sem)

        load(0, bufs[0]).start()

        @pl.loop(0, n_chunks, step=2)
        def _(c):
            for b in range(2):
                rows_v, _ = bufs[b]
                load(c + b, bufs[b]).wait()

                @pl.when(c + b + 1 < n_chunks)
                def _():
                    load(c + b + 1, bufs[1 - b]).start()

                pltpu.sync_copy(lo_hbm.at[rows_of(c + b)], lo_v)
                pltpu.sync_copy(hi_hbm.at[rows_of(c + b)], hi_v)
                pltpu.sync_copy(rows_v, out_hbm.at[lo_v])
                pltpu.sync_copy(rows_v, out_hbm.at[hi_v])

    return scatter(rows, idx_lo, idx_hi)


def _expert_ffn_body(te_ref, nu_ref, nv_ref, h_ref, wg_ref, wu_ref, wd_ref, y_ref, acc_scr):
    i = pl.program_id(0)
    f = pl.program_id(1)

    @pl.when(i < nu_ref[0])
    def _():
        row = lax.broadcasted_iota(jnp.int32, h_ref.shape, 0)
        packed = jnp.where(row < nv_ref[i], h_ref[...], 0)
        y = _swiglu_down(_unpack_bf16_pairs(packed), wg_ref, wu_ref, wd_ref)

        @pl.when(f == 0)
        def _():
            acc_scr[...] = y

        @pl.when(f > 0)
        def _():
            acc_scr[...] += y

        @pl.when(f == pl.num_programs(1) - 1)
        def _():
            y_ref[...] = _pack_bf16_pairs(acc_scr[...])


def _expert_ffn(h_sorted, tile_expert, n_used, tile_valid, w_gate, w_up, w_down, layer):
    rows, half = h_sorted.shape
    d = 2 * half
    ff = w_gate.shape[3]
    n_f = ff // MOE_FF_TILE

    def row_map(i, f, te, nu, nv):
        return (jnp.minimum(i, nu[0] - 1), 0)

    def col_step(i, f, nu):
        return jnp.where(i < nu[0], f, n_f - 1)

    grid_spec = pltpu.PrefetchScalarGridSpec(
        num_scalar_prefetch=3,
        grid=(rows // MOE_ROW_TILE, n_f),
        in_specs=[
            pl.BlockSpec((MOE_ROW_TILE, half), row_map),
            pl.BlockSpec((None, None, d, MOE_FF_TILE),
                         lambda i, f, te, nu, nv: (layer, te[i], 0, col_step(i, f, nu))),
            pl.BlockSpec((None, None, d, MOE_FF_TILE),
                         lambda i, f, te, nu, nv: (layer, te[i], 0, col_step(i, f, nu))),
            pl.BlockSpec((None, None, MOE_FF_TILE, d),
                         lambda i, f, te, nu, nv: (layer, te[i], col_step(i, f, nu), 0)),
        ],
        out_specs=pl.BlockSpec((MOE_ROW_TILE, half), row_map),
        scratch_shapes=[pltpu.VMEM((MOE_ROW_TILE, d), F32)],
    )
    return pl.pallas_call(
        _expert_ffn_body,
        out_shape=jax.ShapeDtypeStruct((rows, half), jnp.int32),
        grid_spec=grid_spec,
        compiler_params=_params("arbitrary", "arbitrary"),
        name="expert_ffn",
    )(tile_expert, n_used, tile_valid, h_sorted, w_gate, w_up, w_down)


def _combine_body(x_ref, y_ref, g_ref, o_ref):
    out = x_ref[...]
    for s in range(2):
        column = jnp.broadcast_to(g_ref[s:s + 1, :], (LANES, g_ref.shape[1])).T
        gate = jnp.concatenate([column] * (out.shape[1] // LANES), axis=1)
        out = out + _unpack_bf16_pairs(y_ref[s]).astype(F32) * gate
    o_ref[...] = out


def _combine_residual(x, y_pairs, gates2):
    t, d = x.shape
    return pl.pallas_call(
        _combine_body,
        out_shape=jax.ShapeDtypeStruct((t, d), F32),
        grid=(t // ROW_TILE,),
        in_specs=[
            pl.BlockSpec((ROW_TILE, d), lambda i: (i, 0)),
            pl.BlockSpec((2, ROW_TILE, d // 2), lambda i: (0, i, 0)),
            pl.BlockSpec((2, ROW_TILE), lambda i: (0, i)),
        ],
        out_specs=pl.BlockSpec((ROW_TILE, d), lambda i: (i, 0)),
        compiler_params=_params("parallel"),
        name="moe_combine",
    )(x, y_pairs, gates2)


def _moe_residual(x, gain, router, w_gate, w_up, w_down, layer):
    t, d = x.shape
    gates, rank, h_packed = _router(x, gain, router)
    rank8 = rank.astype(jnp.int32)
    chosen = rank8 >= 0
    counts = jnp.sum(chosen, axis=1, dtype=jnp.int32)
    padded = (counts + MOE_ROW_TILE - 1) // MOE_ROW_TILE * MOE_ROW_TILE
    ends = jnp.cumsum(padded)
    starts = ends - padded
    pos = starts[:, None] + rank8
    max_rows = 2 * t + N_EXPERTS * MOE_ROW_TILE
    pos_lo = jnp.min(jnp.where(chosen, pos, max_rows), axis=0)
    pos_hi = jnp.max(jnp.where(chosen, pos, -1), axis=0)
    gates2 = jnp.stack([jnp.sum(jnp.where(chosen & (pos == pos_lo[None, :]), gates, 0.0), axis=0),
                        jnp.sum(jnp.where(chosen & (pos == pos_hi[None, :]), gates, 0.0), axis=0)])
    n_tiles = max_rows // MOE_ROW_TILE
    n_used = (ends[-1] // MOE_ROW_TILE).astype(jnp.int32)
    tile_start = jnp.minimum(jnp.arange(n_tiles, dtype=jnp.int32), n_used - 1) * MOE_ROW_TILE
    tile_expert = jnp.sum(tile_start[:, None] >= ends[None, :], axis=1, dtype=jnp.int32)
    tile_valid = jnp.clip((starts + counts)[tile_expert] - tile_start, 0, MOE_ROW_TILE).astype(jnp.int32)

    h_sorted = _sc_row_scatter_pair(h_packed, pos_lo, pos_hi, max_rows)
    y_sorted = _expert_ffn(h_sorted, tile_expert, n_used.reshape(1), tile_valid, w_gate, w_up, w_down, layer)
    y_pairs = _sc_row_gather(y_sorted, jnp.concatenate([pos_lo, pos_hi])).reshape(2, t, d // 2)
    return _combine_residual(x, y_pairs, gates2)


def _split_head_pair(q):
    is_first = lax.broadcasted_iota(jnp.int32, (1, LANES), 1) < HEAD_DIM
    zero = jnp.zeros_like(q)
    return jnp.where(is_first, q, zero), jnp.where(is_first, zero, q)


def _merge_head_pair(first, second):
    is_first = lax.broadcasted_iota(jnp.int32, (1, LANES), 1) < HEAD_DIM
    return jnp.where(is_first, first, second)


def _qk(q, k):
    return lax.dot_general(q, k, (((1,), (1,)), ((), ())), preferred_element_type=F32)


def _transpose_values(v_ref, vt_scr):
    for c in range(v_ref.shape[0] // ATT_BLOCK):
        rows = slice(c * ATT_BLOCK, (c + 1) * ATT_BLOCK)
        vt_scr[:, rows] = v_ref[rows, :].astype(F32).T.astype(vt_scr.dtype)


def _sb_body(q_ref, k_ref, v_ref, u_ref, o_ref, vt_scr, acc_scr, carry_scr, z_scr, keep_scr, sum_scr):
    qi = pl.program_id(2)

    @pl.when(qi == 0)
    def _():
        _transpose_values(v_ref, vt_scr)

    upper = u_ref[...]
    acc_scr[...] = jnp.zeros_like(acc_scr)
    carry_scr[...] = jnp.zeros_like(carry_scr)
    key = lax.broadcasted_iota(jnp.int32, (ATT_BLOCK, ATT_BLOCK), 0)
    query = lax.broadcasted_iota(jnp.int32, (ATT_BLOCK, ATT_BLOCK), 1)
    strict = key < query
    qs = [_split_head_pair(q_ref[j * ATT_BLOCK:(j + 1) * ATT_BLOCK, :]) for j in range(ATT_K_PER_Q)]

    def process(j, kb, keep):
        start = pl.multiple_of(kb * ATT_BLOCK, ATT_BLOCK)
        k = k_ref[pl.ds(start, ATT_BLOCK), :]
        for hh in range(2):
            z = _qk(k, qs[j][hh])
            log_beta = z - _softplus(z)
            log_keep = log_beta - z
            if keep is not None:
                log_keep = jnp.where(keep, log_keep, 0.0)
            remain = jnp.dot(upper, log_keep.astype(BF16), preferred_element_type=F32)
            w = jnp.exp(log_beta + remain + carry_scr[j, hh])
            if keep is not None:
                w = jnp.where(keep, w, 0.0)
            carry_scr[j, hh] += jnp.sum(log_keep, axis=0, keepdims=True)
            vt = vt_scr[pl.ds(hh * HEAD_DIM, HEAD_DIM), pl.ds(start, ATT_BLOCK)]
            acc_scr[j, hh] += jnp.dot(vt, w.astype(BF16), preferred_element_type=F32)

    units = []
    for j in range(ATT_K_PER_Q):
        g = qi * ATT_K_PER_Q + j
        has_previous = None if j > 0 else jnp.broadcast_to(g > 0, strict.shape)
        for kb, keep in ((g, strict), (jnp.maximum(g - 1, 0), has_previous)):
            for hh in range(2):
                units.append((j, hh, pl.multiple_of(kb * ATT_BLOCK, ATT_BLOCK), keep))

    for u, (j, hh, start, keep) in enumerate(units):
        z_scr[u] = _qk(k_ref[pl.ds(start, ATT_BLOCK), :], qs[j][hh])

    for u, (j, hh, start, keep) in enumerate(units):
        z = z_scr[u]
        log_beta = z - _softplus(z)
        log_keep = log_beta - z
        if keep is not None:
            log_keep = jnp.where(keep, log_keep, 0.0)
        z_scr[u] = log_beta
        keep_scr[u] = log_keep.astype(BF16)
        sum_scr[u] = jnp.sum(log_keep, axis=0, keepdims=True)

    for u in range(len(units)):
        z_scr[u] += jnp.dot(upper, keep_scr[u], preferred_element_type=F32)

    for u, (j, hh, start, keep) in enumerate(units):
        diagonal = u % 4 < 2
        log_w = z_scr[u] if diagonal else z_scr[u] + sum_scr[u - 2]
        w = jnp.exp(log_w)
        if keep is not None:
            w = jnp.where(keep, w, 0.0)
        keep_scr[u] = w.astype(BF16)

    for u, (j, hh, start, keep) in enumerate(units):
        vt = vt_scr[pl.ds(hh * HEAD_DIM, HEAD_DIM), pl.ds(start, ATT_BLOCK)]
        acc_scr[j, hh] += jnp.dot(vt, keep_scr[u], preferred_element_type=F32)
        carry_scr[j, hh] += sum_scr[u]

    for j in range(ATT_K_PER_Q):
        def more(kb, j=j):
            return (kb >= 0) & (jnp.max(carry_scr[j]) > SB_UNDERFLOW_LOG)

        def step(kb, j=j):
            process(j, kb, None)
            return kb - 1

        lax.while_loop(more, step, qi * ATT_K_PER_Q + j - 2)

    for j in range(ATT_K_PER_Q):
        out_t = jnp.concatenate([acc_scr[j, 0], acc_scr[j, 1]], axis=0)
        o_ref[j * ATT_BLOCK:(j + 1) * ATT_BLOCK, :] = out_t.T.astype(o_ref.dtype)


def _sb_attention(proj, batch, seq):
    t = proj.shape[0]
    nq = seq // ATT_Q_BLOCK
    idx = jnp.arange(ATT_BLOCK)
    upper = (idx[None, :] > idx[:, None]).astype(BF16)
    return pl.pallas_call(
        _sb_body,
        out_shape=jax.ShapeDtypeStruct((t, D_MODEL), BF16),
        grid=(batch, HEAD_PAIRS, nq),
        in_specs=[
            pl.BlockSpec((ATT_Q_BLOCK, LANES), lambda b, p, i: (b * nq + i, p)),
            pl.BlockSpec((seq, LANES), lambda b, p, i: (b, HEAD_PAIRS + p)),
            pl.BlockSpec((seq, LANES), lambda b, p, i: (b, 2 * HEAD_PAIRS + p)),
            pl.BlockSpec((ATT_BLOCK, ATT_BLOCK), lambda b, p, i: (0, 0)),
        ],
        out_specs=pl.BlockSpec((ATT_Q_BLOCK, LANES), lambda b, p, i: (b * nq + i, p)),
        scratch_shapes=[
            pltpu.VMEM((LANES, seq), BF16),
            pltpu.VMEM((ATT_K_PER_Q, 2, HEAD_DIM, ATT_BLOCK), F32),
            pltpu.VMEM((ATT_K_PER_Q, 2, 1, ATT_BLOCK), F32),
            pltpu.VMEM((4 * ATT_K_PER_Q, ATT_BLOCK, ATT_BLOCK), F32),
            pltpu.VMEM((4 * ATT_K_PER_Q, ATT_BLOCK, ATT_BLOCK), BF16),
            pltpu.VMEM((4 * ATT_K_PER_Q, 1, ATT_BLOCK), F32),
        ],
        compiler_params=_params("parallel", "parallel", "arbitrary"),
        name="sb_attention",
    )(proj, proj, proj, upper)


def _fox_gate_body(x_ref, g_ref, w_ref, b_ref, tri_ref, c_ref):
    h = _rms_normalize(x_ref[...], g_ref[...])
    logits = jnp.dot(h, w_ref[...], preferred_element_type=F32, precision=lax.Precision.HIGHEST)
    log_f = -_softplus(-(logits + b_ref[...]))
    seq = log_f.shape[0]
    carry = jnp.zeros((1, LANES), F32)
    for blk in range(seq // ATT_BLOCK):
        rows = slice(blk * ATT_BLOCK, (blk + 1) * ATT_BLOCK)
        c = jnp.dot(tri_ref[...], log_f[rows], preferred_element_type=F32,
                    precision=lax.Precision.HIGHEST) + carry
        c_ref[rows, :] = c
        carry = c[ATT_BLOCK - 1:ATT_BLOCK, :]


def _fox_cum_log_forget(x, gain, w_gate, b_gate, batch, seq):
    t, d = x.shape
    w = jnp.zeros((d, LANES), F32).at[:, :N_HEADS].set(w_gate)
    b = jnp.zeros((1, LANES), F32).at[0, :N_HEADS].set(b_gate)
    idx = jnp.arange(ATT_BLOCK)
    tri = (idx[:, None] >= idx[None, :]).astype(F32)
    return pl.pallas_call(
        _fox_gate_body,
        out_shape=jax.ShapeDtypeStruct((t, LANES), F32),
        grid=(batch,),
        in_specs=[
            pl.BlockSpec((seq, d), lambda i: (i, 0)),
            pl.BlockSpec((1, d), lambda i: (0, 0)),
            pl.BlockSpec((d, LANES), lambda i: (0, 0)),
            pl.BlockSpec((1, LANES), lambda i: (0, 0)),
            pl.BlockSpec((ATT_BLOCK, ATT_BLOCK), lambda i: (0, 0)),
        ],
        out_specs=pl.BlockSpec((seq, LANES), lambda i: (i, 0)),
        compiler_params=_params("parallel"),
        name="fox_gate",
    )(x, gain.reshape(1, d), w, b, tri)


def _fox_body(q_ref, k_ref, v_ref, cq_ref, ck_ref, o_ref, vt_scr, ckb_scr, a_scr, p_scr):
    qi = pl.program_id(2)

    @pl.when(qi == 0)
    def _():
        _transpose_values(v_ref, vt_scr)
        for hh in range(2):
            for c in range(ckb_scr.shape[1] // ATT_BLOCK):
                rows = slice(c * ATT_BLOCK, (c + 1) * ATT_BLOCK)
                ckb_scr[hh, rows, :] = jnp.broadcast_to(ck_ref[0, hh, :, rows], (LANES, ATT_BLOCK)).T

    qs = _split_head_pair(q_ref[...])
    key = lax.broadcasted_iota(jnp.int32, (ATT_BLOCK, ATT_Q_BLOCK), 0)
    query = lax.broadcasted_iota(jnp.int32, (ATT_BLOCK, ATT_Q_BLOCK), 1)
    row_shape = (1, ATT_Q_BLOCK)

    def scores(kb, tops, diagonal):
        start = kb * ATT_BLOCK
        k = k_ref[pl.ds(start, ATT_BLOCK), :]
        if diagonal:
            causal = key + (kb * ATT_BLOCK - qi * ATT_Q_BLOCK) <= query
        new_tops = []
        for hh in range(2):
            ck = ckb_scr[hh, pl.ds(start, ATT_BLOCK), :]
            a = _qk(k, qs[hh]) - jnp.concatenate([ck] * (ATT_Q_BLOCK // LANES), axis=1)
            if diagonal:
                a = jnp.where(causal, a, NEG_INF)
            a_scr[hh, kb] = a
            new_tops.append(jnp.maximum(tops[hh], jnp.max(a, axis=0, keepdims=True)))
        return tuple(new_tops)

    def sweep(n_blocks):
        tops = (jnp.full(row_shape, NEG_INF, F32),) * 2
        for kb in range(n_blocks):
            tops = scores(kb, tops, kb >= n_blocks - ATT_K_PER_Q)
        shifts = tuple(cq_ref[0, hh] - (tops[hh] + cq_ref[0, hh]) for hh in range(2))
        sums = [jnp.zeros(row_shape, F32)] * 2
        for kb in range(n_blocks):
            for hh in range(2):
                p = jnp.exp(a_scr[hh, kb] + shifts[hh])
                p_scr[hh, kb] = p.astype(BF16)
                sums[hh] = sums[hh] + jnp.sum(p, axis=0, keepdims=True)
        accs = [jnp.zeros((HEAD_DIM, ATT_Q_BLOCK), F32)] * 2
        for kb in range(n_blocks):
            for hh in range(2):
                vt = vt_scr[hh * HEAD_DIM:(hh + 1) * HEAD_DIM, kb * ATT_BLOCK:(kb + 1) * ATT_BLOCK]
                accs[hh] = accs[hh] + jnp.dot(vt, p_scr[hh, kb], preferred_element_type=F32)
        out_t = jnp.concatenate([accs[0] / sums[0], accs[1] / sums[1]], axis=0)
        o_ref[...] = out_t.T.astype(o_ref.dtype)

    for q in range(k_ref.shape[0] // ATT_Q_BLOCK):
        pl.when(qi == q)(functools.partial(sweep, (q + 1) * ATT_K_PER_Q))


def _fox_attention(proj, cum, batch, seq):
    t = proj.shape[0]
    nq = seq // ATT_Q_BLOCK
    cum_h = cum[:, :N_HEADS].reshape(batch, seq, N_HEADS).transpose(0, 2, 1)
    cum_rows = cum_h.reshape(batch, N_HEADS, 1, seq)
    return pl.pallas_call(
        _fox_body,
        out_shape=jax.ShapeDtypeStruct((t, D_MODEL), BF16),
        grid=(batch, HEAD_PAIRS, nq),
        in_specs=[
            pl.BlockSpec((ATT_Q_BLOCK, LANES), lambda b, p, i: (b * nq + i, p)),
            pl.BlockSpec((seq, LANES), lambda b, p, i: (b, HEAD_PAIRS + p)),
            pl.BlockSpec((seq, LANES), lambda b, p, i: (b, 2 * HEAD_PAIRS + p)),
            pl.BlockSpec((1, 2, 1, ATT_Q_BLOCK), lambda b, p, i: (b, p, 0, i)),
            pl.BlockSpec((1, 2, 1, seq), lambda b, p, i: (b, p, 0, 0)),
        ],
        out_specs=pl.BlockSpec((ATT_Q_BLOCK, LANES), lambda b, p, i: (b * nq + i, p)),
        scratch_shapes=[
            pltpu.VMEM((LANES, seq), BF16),
            pltpu.VMEM((2, seq, LANES), F32),
            pltpu.VMEM((2, seq // ATT_BLOCK, ATT_BLOCK, ATT_Q_BLOCK), F32),
            pltpu.VMEM((2, seq // ATT_BLOCK, ATT_BLOCK, ATT_Q_BLOCK), BF16),
        ],
        compiler_params=_params("parallel", "parallel", "arbitrary"),
        name="fox_attention",
    )(proj, proj, proj, cum_rows, cum_rows)


def _band_body(q_ref, k_ref, v_ref, bias_ref, o_ref, s_ref, vt_scr, logit_scr, p_scr, inv_scr, *, sub_len):
    seq = v_ref.shape[0]
    n_blocks = seq // DIL_BLOCK
    _transpose_values(v_ref, vt_scr)

    def key_rows(n):
        first = (n * DIL_BLOCK) % sub_len == 0
        return first, slice((n if first else n - 1) * DIL_BLOCK, (n + 1) * DIL_BLOCK)

    for n in range(n_blocks):
        first, k_rows = key_rows(n)
        qs = _split_head_pair(q_ref[n * DIL_BLOCK:(n + 1) * DIL_BLOCK, :])
        k = k_ref[k_rows, :]
        for hh in range(2):
            bias = bias_ref[hh, DIL_BLOCK:, :] if first else bias_ref[hh]
            logit_scr[2 * n + hh, :k.shape[0], :] = _qk(k, qs[hh]) + bias

    for n in range(n_blocks):
        first, k_rows = key_rows(n)
        n_keys = k_rows.stop - k_rows.start
        for hh in range(2):
            logits = logit_scr[2 * n + hh, :n_keys, :]
            m = jnp.max(logits, axis=0, keepdims=True)
            p = jnp.exp(logits - m)
            l = jnp.sum(p, axis=0, keepdims=True)
            p_scr[2 * n + hh, :n_keys, :] = p.astype(BF16)
            inv_scr[2 * n + hh] = 1.0 / l
            s_ref[0, 0, hh:hh + 1, n * DIL_BLOCK:(n + 1) * DIL_BLOCK] = m + jnp.log(l)

    for n in range(n_blocks):
        first, k_rows = key_rows(n)
        n_keys = k_rows.stop - k_rows.start
        outs = []
        for hh in range(2):
            vt = vt_scr[hh * HEAD_DIM:(hh + 1) * HEAD_DIM, k_rows]
            o = jnp.dot(vt, p_scr[2 * n + hh, :n_keys, :], preferred_element_type=F32)
            outs.append(o * inv_scr[2 * n + hh])
        o_ref[n * DIL_BLOCK:(n + 1) * DIL_BLOCK, :] = jnp.concatenate(outs, axis=0).T.astype(o_ref.dtype)


def _band_attention(qk, v, bias_t, batch, seq, sub_len):
    t = v.shape[0]
    return pl.pallas_call(
        functools.partial(_band_body, sub_len=sub_len),
        out_shape=(jax.ShapeDtypeStruct((t, D_MODEL), BF16),
                   jax.ShapeDtypeStruct((batch, HEAD_PAIRS, 2, seq), F32)),
        grid=(batch, HEAD_PAIRS),
        in_specs=[
            pl.BlockSpec((None, seq, LANES), lambda b, p: (0, b, p)),
            pl.BlockSpec((None, seq, LANES), lambda b, p: (1, b, p)),
            pl.BlockSpec((seq, LANES), lambda b, p: (b, p)),
            pl.BlockSpec((2, 2 * DIL_BLOCK, DIL_BLOCK), lambda b, p: (p, 0, 0)),
        ],
        out_specs=(pl.BlockSpec((seq, LANES), lambda b, p: (b, p)),
                   pl.BlockSpec((1, 1, 2, seq), lambda b, p: (b, p, 0, 0))),
        scratch_shapes=[pltpu.VMEM((LANES, seq), BF16),
                        pltpu.VMEM((2 * seq // DIL_BLOCK, 2 * DIL_BLOCK, DIL_BLOCK), F32),
                        pltpu.VMEM((2 * seq // DIL_BLOCK, 2 * DIL_BLOCK, DIL_BLOCK), BF16),
                        pltpu.VMEM((2 * seq // DIL_BLOCK, 1, DIL_BLOCK), F32)],
        compiler_params=_params("parallel", "parallel"),
        name="band_attention",
    )(qk, qk, v, bias_t)


def _softmax_merge_body(o1_ref, o2_ref, o3_ref, s_ref, e_ref, o_ref):
    s = s_ref[...]
    groups = [s, pltpu.roll(s, LANES - N_HEADS, axis=1), pltpu.roll(s, LANES - 2 * N_HEADS, axis=1)]
    top = jnp.maximum(jnp.maximum(groups[0], groups[1]), groups[2])
    weights = [jnp.exp(g - top) for g in groups]
    inv = 1.0 / (weights[0] + weights[1] + weights[2])
    out = jnp.zeros(o_ref.shape, F32)
    for w, part in zip(weights, (o1_ref, o2_ref, o3_ref)):
        spread = jnp.dot((w * inv).astype(BF16), e_ref[...], preferred_element_type=F32)
        out = out + spread * part[...].astype(F32)
    o_ref[...] = out.astype(o_ref.dtype)


def _softmax_merge(outs, s_all):
    t, d = outs[0].shape
    head_of = jnp.arange(d) // HEAD_DIM
    expand = (jnp.arange(LANES)[:, None] == head_of[None, :]).astype(BF16)
    rows = pl.BlockSpec((ROW_TILE, d), lambda i: (i, 0))
    return pl.pallas_call(
        _softmax_merge_body,
        out_shape=jax.ShapeDtypeStruct((t, d), BF16),
        grid=(t // ROW_TILE,),
        in_specs=[rows, rows, rows,
                  pl.BlockSpec((ROW_TILE, LANES), lambda i: (i, 0)),
                  pl.BlockSpec((LANES, d), lambda i: (0, 0))],
        out_specs=rows,
        compiler_params=_params("parallel"),
        name="softmax_merge",
    )(*outs, s_all, expand)


def _t5_causal_bucket(distance):
    max_exact = N_REL_BUCKETS // 2
    d = jnp.maximum(distance, 1).astype(F32)
    log_b = max_exact + (jnp.log(d / max_exact) / math.log(REL_MAX_DISTANCE / max_exact)
                         * (N_REL_BUCKETS - max_exact)).astype(jnp.int32)
    log_b = jnp.minimum(log_b, N_REL_BUCKETS - 1)
    return jnp.where(distance < max_exact, distance, log_b)


def _dilated_bias(rel_bias):
    kj = jnp.arange(2 * DIL_BLOCK, dtype=jnp.int32)
    qi = jnp.arange(DIL_BLOCK, dtype=jnp.int32)
    delta = qi[None, :] + DIL_BLOCK - kj[:, None]
    in_band = (delta >= 0) & (delta <= DIL_SPAN)
    buckets = jnp.stack([_t5_causal_bucket(jnp.maximum(delta, 0) * dil) for _, dil in DILATED_PAIRS])
    one_hot = (buckets[..., None] == jnp.arange(N_REL_BUCKETS)).astype(F32)
    bias = jnp.einsum("gkqb,bh->ghkq", one_hot, rel_bias.astype(F32), precision=lax.Precision.HIGHEST)
    return jnp.where(in_band[None, None], bias, NEG_INF)


def _dilated_proj_body(x_ref, g_ref, w_ref, cs_ref, bd_ref, *rest):
    n_groups = len(DILATED_PAIRS)
    qk_refs, v_refs = rest[:n_groups], rest[n_groups:2 * n_groups]
    h_scr, res_scr = rest[2 * n_groups:]
    j = pl.program_id(1)

    @pl.when(j == 0)
    def _():
        h_scr[...] = _rms_normalize(x_ref[...], g_ref[...]).astype(BF16)

    acc = jnp.dot(h_scr[...], w_ref[...], preferred_element_type=F32)

    def keep(res, first_lane):
        for c in range(res.shape[1] // LANES):
            res_scr[first_lane // LANES + c] = res[:, c * LANES:(c + 1) * LANES]

    @pl.when(j < 2 * n_groups)
    def _():
        for c in range(COL_TILE // NORM_CHUNK):
            sl = slice(c * NORM_CHUNK, (c + 1) * NORM_CHUNK)
            a = acc[:, sl]
            ss = jnp.dot((a * a).astype(BF16), bd_ref[...], preferred_element_type=F32)
            keep(a * lax.rsqrt(ss * (1.0 / HEAD_DIM) + RMS_EPS) * cs_ref[:, sl], c * NORM_CHUNK)

    @pl.when(j == 2 * n_groups)
    def _():
        keep(acc * cs_ref[...], 0)

    def write_classes(ref, dil):
        rows = res_scr.shape[1] // dil
        for c in range(res_scr.shape[0]):
            lanes = slice(c * LANES, (c + 1) * LANES)
            if dil == 1:
                ref[:, lanes] = res_scr[c].astype(ref.dtype)
            else:
                for r in range(dil):
                    ref[r, :, lanes] = res_scr[c, pl.ds(r, rows, stride=dil), :].astype(ref.dtype)

    for g, (_, dil) in enumerate(DILATED_PAIRS):
        pl.when(j // 2 == g)(functools.partial(write_classes, qk_refs[g], dil))
        pl.when(j == 2 * n_groups)(functools.partial(write_classes, v_refs[g], dil))


def _dilated_proj(x, gain, w, layer, col_scale, batch, seq):
    t, d = x.shape
    n = w.shape[2]
    n_groups = len(DILATED_PAIRS)
    tiles_per_seq = seq // DIL_ROW_TILE
    head_id = jnp.arange(NORM_CHUNK) // HEAD_DIM
    block_diag = (head_id[:, None] == head_id[None, :]).astype(BF16)
    out_shape, out_specs = [], []
    for kind in ("qk", "v"):
        for g, (_, dil) in enumerate(DILATED_PAIRS):
            rows = DIL_ROW_TILE // dil
            if kind == "qk":
                shape = (2, batch, dil, seq // dil, d)
                block = (None, None, dil, rows, d)
                index = lambda i, j, g=g: (jnp.clip(j - 2 * g, 0, 1), i // tiles_per_seq, 0, i % tiles_per_seq, 0)
            else:
                shape = (batch, dil, seq // dil, d)
                block = (None, dil, rows, d)
                index = lambda i, j: (i // tiles_per_seq, 0, i % tiles_per_seq, 0)
            if dil == 1:
                block = block[:-3] + (None,) + block[-2:]
            out_shape.append(jax.ShapeDtypeStruct(shape, BF16))
            out_specs.append(pl.BlockSpec(block, index))
    outs = pl.pallas_call(
        _dilated_proj_body,
        out_shape=tuple(out_shape),
        grid=(t // DIL_ROW_TILE, n // COL_TILE),
        in_specs=[
            pl.BlockSpec((DIL_ROW_TILE, d), lambda i, j: (i, 0)),
            pl.BlockSpec((1, d), lambda i, j: (0, 0)),
            pl.BlockSpec((None, d, COL_TILE), lambda i, j: (layer, 0, j)),
            pl.BlockSpec((1, COL_TILE), lambda i, j: (0, j)),
            pl.BlockSpec((NORM_CHUNK, NORM_CHUNK), lambda i, j: (0, 0)),
        ],
        out_specs=tuple(out_specs),
        scratch_shapes=[pltpu.VMEM((DIL_ROW_TILE, d), BF16),
                        pltpu.VMEM((COL_TILE // LANES, DIL_ROW_TILE, LANES), F32)],
        compiler_params=_params("parallel", "arbitrary"),
        name="dilated_proj",
    )(x, gain.reshape(1, d), w, col_scale.reshape(1, n).astype(F32), block_diag)
    qk = [o.reshape(2, t, d) for o in outs[:n_groups]]
    v = [o.reshape(t, d) for o in outs[n_groups:]]
    return qk, v


def _dilated_attention(qk, v, rel_bias, batch, seq):
    t = v[0].shape[0]
    n_groups = len(DILATED_PAIRS)
    bias_t = _dilated_bias(rel_bias)
    outs, stats = [], []
    for g, (_, dil) in enumerate(DILATED_PAIRS):
        sub_len = seq // dil
        o, s = _band_attention(qk[g], v[g], bias_t[g], batch, seq, sub_len)
        o = o.reshape(batch, dil, sub_len, D_MODEL).transpose(0, 2, 1, 3).reshape(t, D_MODEL)
        s = s.reshape(batch, HEAD_PAIRS, 2, dil, sub_len).swapaxes(3, 4)
        outs.append(o)
        stats.append(s.reshape(batch, N_HEADS, seq).transpose(0, 2, 1).reshape(t, N_HEADS))
    stats.append(jnp.zeros((t, LANES - n_groups * N_HEADS), F32))
    return _softmax_merge(outs, jnp.concatenate(stats, axis=1))


def _tile_heads(v):
    return jnp.tile(v.astype(F32), N_HEADS)


def _sb_mixer(x, gain, w_qkv, w_o, layer, batch, seq):
    ones = jnp.ones((D_MODEL,), F32)
    col_scale = jnp.concatenate([ones * QK_SCALE, ones, ones])
    proj = _norm_proj(x, gain, w_qkv, layer, col_scale, 0)
    return _out_proj_residual(x, _sb_attention(proj, batch, seq), w_o, layer)


def _dilated_mixer(x, gain, w_in, q_norm, k_norm, rel_bias, w_o, layer, batch, seq):
    scales = []
    for g in range(len(DILATED_PAIRS)):
        scales += [_tile_heads(q_norm[g]) * QK_SCALE, _tile_heads(k_norm[g])]
    scales.append(jnp.ones((D_MODEL,), F32))
    qk, v = _dilated_proj(x, gain, w_in, layer, jnp.concatenate(scales), batch, seq)
    return _out_proj_residual(x, _dilated_attention(qk, v, rel_bias, batch, seq), w_o, layer)


def _fox_mixer(x, gain, w_qkv, w_gate, b_f, q_norm, k_norm, w_o, layer, batch, seq):
    col_scale = jnp.concatenate([_tile_heads(q_norm) * QK_SCALE, _tile_heads(k_norm),
                                 jnp.ones((D_MODEL,), F32)])
    proj = _norm_proj(x, gain, w_qkv, layer, col_scale, 2)
    cum = _fox_cum_log_forget(x, gain, w_gate, b_f, batch, seq)
    return _out_proj_residual(x, _fox_attention(proj, cum, batch, seq), w_o, layer)


def kernel(x, sb_w_qkv, sb_w_o, dil_w_in, dil_q_norm, dil_k_norm, dil_w_o, fox_w_in, fox_b_f,
           fox_q_norm, fox_k_norm, fox_w_o, rel_bias, attn_norm, ffn_norm, mlp_w_gate, mlp_w_up,
           mlp_w_down, moe_router, moe_w_gate, moe_w_up, moe_w_down):
    batch, seq, d = x.shape
    depth = attn_norm.shape[0]
    sb_w_qkv, sb_w_o, dil_w_in, dil_w_o, fox_w_o, mlp_w_gate, mlp_w_up, mlp_w_down = (
        w.astype(BF16) for w in (sb_w_qkv, sb_w_o, dil_w_in, dil_w_o, fox_w_o, mlp_w_gate, mlp_w_up,
                                 mlp_w_down))
    moe_w_gate, moe_w_up, moe_w_down = (w.astype(BF16) for w in (moe_w_gate, moe_w_up, moe_w_down))
    fox_w_qkv = fox_w_in[:, :, :3 * D_MODEL].astype(BF16)
    h = x.reshape(batch * seq, d)
    for i in range(depth):
        kind, j = i % 3, i // 3
        if kind == 0:
            h = _sb_mixer(h, attn_norm[i], sb_w_qkv, sb_w_o, j, batch, seq)
        elif kind == 1:
            h = _dilated_mixer(h, attn_norm[i], dil_w_in, dil_q_norm[j], dil_k_norm[j], rel_bias,
                               dil_w_o, j, batch, seq)
        else:
            h = _fox_mixer(h, attn_norm[i], fox_w_qkv, fox_w_in[j, :, 3 * D_MODEL:], fox_b_f[j],
                           fox_q_norm[j], fox_k_norm[j], fox_w_o, j, batch, seq)
        f = i // 2
        if i % 2 == 0:
            h = _ffn_residual(h, ffn_norm[i], mlp_w_gate, mlp_w_up, mlp_w_down, f)
        else:
            h = _moe_residual(h, ffn_norm[i], moe_router[f], moe_w_gate, moe_w_up, moe_w_down, f)
    return h.reshape(batch, seq, d)
```

```python
import functools
import math

import jax
import jax.numpy as jnp
from jax import lax
from jax.experimental import pallas as pl
from jax.experimental.pallas import tpu as pltpu
from jax.experimental.pallas import tpu_sc as plsc

D_MODEL = 1024
N_HEADS = 16
HEAD_DIM = 64
LANES = 128
HEAD_PAIRS = D_MODEL // LANES
D_FF = 3584
N_EXPERTS = 8
N_REL_BUCKETS = 32
REL_MAX_DISTANCE = 2048
DILATED_PAIRS = ((128, 1), (512, 4), (2048, 16))
DIL_SPAN = 128
RMS_EPS = 1e-6
NEG_INF = -1e30
SB_UNDERFLOW_LOG = -104.0
QK_SCALE = 1.0 / math.sqrt(HEAD_DIM)

ROW_TILE = 1024
COL_TILE = 1024
FF_TILE = 1792
SWIGLU_CHUNK = 256
MOE_ROW_TILE = 512
MOE_FF_TILE = 1792
SC_CORES = 2
SC_SUBCORES = 16
SC_CHUNK_ROWS = 64
NORM_CHUNK = 256
ATT_BLOCK = 256
ATT_Q_BLOCK = 512
ATT_K_PER_Q = ATT_Q_BLOCK // ATT_BLOCK
DIL_BLOCK = 128
DIL_ROW_TILE = 512
VMEM_LIMIT = 56 * 1024 * 1024

F32 = jnp.float32
BF16 = jnp.bfloat16


def _params(*semantics):
    return pltpu.CompilerParams(dimension_semantics=semantics, vmem_limit_bytes=VMEM_LIMIT)


def _rms_normalize(x, gain):
    inv = lax.rsqrt(jnp.mean(x * x, axis=-1, keepdims=True) + RMS_EPS)
    return x * inv * gain


def _split_bf16(x, terms):
    parts = []
    for _ in range(terms):
        part = x.astype(BF16)
        parts.append(part)
        x = x - part.astype(F32)
    return parts


def _dot_split(a, b):
    a_hi, a_lo = _split_bf16(a, 2)
    b_hi, b_lo = _split_bf16(b, 2)
    return (jnp.dot(a_hi, b_hi, preferred_element_type=F32)
            + (jnp.dot(a_hi, b_lo, preferred_element_type=F32)
               + jnp.dot(a_lo, b_hi, preferred_element_type=F32)))


def _softplus(z):
    return jnp.maximum(z, 0.0) + jnp.log(1.0 + jnp.exp(-jnp.abs(z)))


def _norm_proj_body(x_ref, g_ref, w_ref, cs_ref, bd_ref, o_ref, h_scr, *, n_norm):
    j = pl.program_id(1)

    @pl.when(j == 0)
    def _():
        h_scr[...] = _rms_normalize(x_ref[...], g_ref[...]).astype(BF16)

    acc = jnp.dot(h_scr[...], w_ref[...], preferred_element_type=F32)

    def plain():
        o_ref[...] = (acc * cs_ref[...]).astype(o_ref.dtype)

    def head_normed():
        for c in range(COL_TILE // NORM_CHUNK):
            sl = slice(c * NORM_CHUNK, (c + 1) * NORM_CHUNK)
            a = acc[:, sl]
            ss = jnp.dot((a * a).astype(BF16), bd_ref[...], preferred_element_type=F32)
            inv = lax.rsqrt(ss * (1.0 / HEAD_DIM) + RMS_EPS)
            o_ref[:, sl] = (a * inv * cs_ref[:, sl]).astype(o_ref.dtype)

    if n_norm == 0:
        plain()
    else:
        pl.when(j < n_norm)(head_normed)
        pl.when(j >= n_norm)(plain)


def _norm_proj(x, gain, w, layer, col_scale, n_norm):
    t, d = x.shape
    n = w.shape[2]
    head_id = jnp.arange(NORM_CHUNK) // HEAD_DIM
    block_diag = (head_id[:, None] == head_id[None, :]).astype(BF16)
    return pl.pallas_call(
        functools.partial(_norm_proj_body, n_norm=n_norm),
        out_shape=jax.ShapeDtypeStruct((t, n), BF16),
        grid=(t // ROW_TILE, n // COL_TILE),
        in_specs=[
            pl.BlockSpec((ROW_TILE, d), lambda i, j: (i, 0)),
            pl.BlockSpec((1, d), lambda i, j: (0, 0)),
            pl.BlockSpec((None, d, COL_TILE), lambda i, j: (layer, 0, j)),
            pl.BlockSpec((1, COL_TILE), lambda i, j: (0, j)),
            pl.BlockSpec((NORM_CHUNK, NORM_CHUNK), lambda i, j: (0, 0)),
        ],
        out_specs=pl.BlockSpec((ROW_TILE, COL_TILE), lambda i, j: (i, j)),
        scratch_shapes=[pltpu.VMEM((ROW_TILE, d), BF16)],
        compiler_params=_params("parallel", "arbitrary"),
        name="norm_proj",
    )(x, gain.reshape(1, d), w, col_scale.reshape(1, n).astype(F32), block_diag)


def _out_proj_body(x_ref, o_ref, w_ref, y_ref):
    y_ref[...] = x_ref[...] + jnp.dot(o_ref[...], w_ref[...], preferred_element_type=F32)


def _out_proj_residual(x, o, w, layer):
    t, d = x.shape
    return pl.pallas_call(
        _out_proj_body,
        out_shape=jax.ShapeDtypeStruct((t, d), F32),
        grid=(t // ROW_TILE,),
        in_specs=[
            pl.BlockSpec((ROW_TILE, d), lambda i: (i, 0)),
            pl.BlockSpec((ROW_TILE, d), lambda i: (i, 0)),
            pl.BlockSpec((None, d, d), lambda i: (layer, 0, 0)),
        ],
        out_specs=pl.BlockSpec((ROW_TILE, d), lambda i: (i, 0)),
        compiler_params=_params("parallel"),
        name="out_proj",
    )(x, o, w)


def _swiglu_hidden(h, wg, wu):
    g = jnp.dot(h, wg, preferred_element_type=F32)
    u = jnp.dot(h, wu, preferred_element_type=F32)
    return g * (1.0 / (1.0 + jnp.exp(-g))) * u


def _swiglu_down(h, wg_ref, wu_ref, wd_ref):
    y = None
    for c in range(wg_ref.shape[1] // SWIGLU_CHUNK):
        cols = slice(c * SWIGLU_CHUNK, (c + 1) * SWIGLU_CHUNK)
        a = _swiglu_hidden(h, wg_ref[:, cols], wu_ref[:, cols]).astype(BF16)
        part = jnp.dot(a, wd_ref[cols, :], preferred_element_type=F32)
        y = part if y is None else y + part
    return y


def _ffn_body(x_ref, g_ref, wg_ref, wu_ref, wd_ref, y_ref, h_scr):
    f = pl.program_id(1)

    @pl.when(f == 0)
    def _():
        x = x_ref[...]
        h_scr[...] = _rms_normalize(x, g_ref[...]).astype(BF16)
        y_ref[...] = x

    y_ref[...] += _swiglu_down(h_scr[...], wg_ref, wu_ref, wd_ref)


def _ffn_residual(x, gain, w_gate, w_up, w_down, layer):
    t, d = x.shape
    ff = w_gate.shape[2]
    return pl.pallas_call(
        _ffn_body,
        out_shape=jax.ShapeDtypeStruct((t, d), F32),
        grid=(t // ROW_TILE, ff // FF_TILE),
        in_specs=[
            pl.BlockSpec((ROW_TILE, d), lambda i, f: (i, 0)),
            pl.BlockSpec((1, d), lambda i, f: (0, 0)),
            pl.BlockSpec((None, d, FF_TILE), lambda i, f: (layer, 0, f)),
            pl.BlockSpec((None, d, FF_TILE), lambda i, f: (layer, 0, f)),
            pl.BlockSpec((None, FF_TILE, d), lambda i, f: (layer, f, 0)),
        ],
        out_specs=pl.BlockSpec((ROW_TILE, d), lambda i, f: (i, 0)),
        scratch_shapes=[pltpu.VMEM((ROW_TILE, d), BF16)],
        compiler_params=_params("parallel", "arbitrary"),
        name="ffn",
    )(x, gain.reshape(1, d), w_gate, w_up, w_down)


def _pack_bf16_pairs(x):
    half = x.shape[1] // 2
    bits = pltpu.bitcast(x.astype(BF16).astype(F32), jnp.int32)
    return bits[:, :half] | lax.shift_right_logical(bits[:, half:], jnp.int32(16))


def _unpack_bf16_pairs(p):
    left = pltpu.bitcast(p & jnp.int32(-65536), F32)
    right = pltpu.bitcast(lax.shift_left(p, jnp.int32(16)), F32)
    return jnp.concatenate([left, right], axis=1).astype(BF16)


def _router_body(x_ref, g_ref, r_ref, tri_ref, gates_ref, rank_ref, h_ref, count_scr):
    @pl.when(pl.program_id(0) == 0)
    def _():
        count_scr[...] = jnp.zeros_like(count_scr)

    h = _rms_normalize(x_ref[...], g_ref[...])
    h_ref[...] = _pack_bf16_pairs(h)
    logits = _dot_split(h, r_ref[...])
    lane = lax.broadcasted_iota(jnp.int32, logits.shape, 1).astype(F32)
    logits = jnp.where(lane < N_EXPERTS, logits, -jnp.inf)
    m1 = jnp.max(logits, axis=-1, keepdims=True)
    i1 = jnp.min(jnp.where(logits == m1, lane, float(LANES)), axis=-1, keepdims=True)
    rest = jnp.where(lane == i1, -jnp.inf, logits)
    m2 = jnp.max(rest, axis=-1, keepdims=True)
    i2 = jnp.min(jnp.where(rest == m2, lane, float(LANES)), axis=-1, keepdims=True)
    e = jnp.exp(m2 - m1)
    g1 = 1.0 / (1.0 + e)
    gates = jnp.where(lane == i1, g1, 0.0) + jnp.where(lane == i2, e * g1, 0.0)
    gates_ref[...] = gates.T[:N_EXPERTS, :]
    chosen = jnp.where((lane == i1) | (lane == i2), 1.0, 0.0)
    inclusive = jnp.dot(tri_ref[...], chosen.astype(BF16), preferred_element_type=F32)
    rank = jnp.where(chosen > 0.0, inclusive - 1.0 + count_scr[...], -1.0)
    rank_ref[...] = rank.T[:N_EXPERTS, :]
    count_scr[...] += inclusive[ROW_TILE - 1:ROW_TILE, :]


def _router(x, gain, router):
    t, d = x.shape
    r = jnp.zeros((d, LANES), F32).at[:, :N_EXPERTS].set(router)
    idx = jnp.arange(ROW_TILE)
    tri = (idx[:, None] >= idx[None, :]).astype(BF16)
    return pl.pallas_call(
        _router_body,
        out_shape=(jax.ShapeDtypeStruct((N_EXPERTS, t), F32), jax.ShapeDtypeStruct((N_EXPERTS, t), F32),
                   jax.ShapeDtypeStruct((t, d // 2), jnp.int32)),
        grid=(t // ROW_TILE,),
        in_specs=[
            pl.BlockSpec((ROW_TILE, d), lambda i: (i, 0)),
            pl.BlockSpec((1, d), lambda i: (0, 0)),
            pl.BlockSpec((d, LANES), lambda i: (0, 0)),
            pl.BlockSpec((ROW_TILE, ROW_TILE), lambda i: (0, 0)),
        ],
        out_specs=(pl.BlockSpec((N_EXPERTS, ROW_TILE), lambda i: (0, i)),
                   pl.BlockSpec((N_EXPERTS, ROW_TILE), lambda i: (0, i)),
                   pl.BlockSpec((ROW_TILE, d // 2), lambda i: (i, 0))),
        scratch_shapes=[pltpu.VMEM((1, LANES), F32)],
        compiler_params=_params("arbitrary"),
        name="router",
    )(x, gain.reshape(1, d), r, tri)


def _sc_mesh():
    return plsc.VectorSubcoreMesh(core_axis_name="core", subcore_axis_name="subcore",
                                  num_cores=SC_CORES, num_subcores=SC_SUBCORES)


def _sc_worker_base(per_worker):
    return (lax.axis_index("subcore") * SC_CORES + lax.axis_index("core")) * per_worker


def _sc_row_gather(table, idx):
    width = table.shape[1]
    n = idx.shape[0]
    per_worker = n // (SC_CORES * SC_SUBCORES)
    n_chunks = per_worker // SC_CHUNK_ROWS
    assert n == n_chunks * SC_CHUNK_ROWS * SC_CORES * SC_SUBCORES and n_chunks % 2 == 0

    @functools.partial(
        pl.kernel, mesh=_sc_mesh(), out_type=jax.ShapeDtypeStruct((n, width), table.dtype),
        scratch_types=[pltpu.VMEM((SC_CHUNK_ROWS,), jnp.int32), pltpu.VMEM((SC_CHUNK_ROWS,), jnp.int32),
                       pltpu.VMEM((SC_CHUNK_ROWS, width), table.dtype),
                       pltpu.VMEM((SC_CHUNK_ROWS, width), table.dtype),
                       pltpu.SemaphoreType.DMA, pltpu.SemaphoreType.DMA],
        name="sc_row_gather")
    def gather(table_hbm, idx_hbm, out_hbm, idx_a, idx_b, rows_a, rows_b, sem_a, sem_b):
        base = _sc_worker_base(per_worker)
        bufs = ((idx_a, rows_a, sem_a), (idx_b, rows_b, sem_b))

        def rows_of(c):
            return pl.ds(pl.multiple_of(base + c * SC_CHUNK_ROWS, SC_CHUNK_ROWS), SC_CHUNK_ROWS)

        def fetch(c, buf):
            idx_v, rows_v, sem = buf
            pltpu.sync_copy(idx_hbm.at[rows_of(c)], idx_v)
            return pltpu.make_async_copy(table_hbm.at[idx_v], rows_v, sem)

        fetch(0, bufs[0]).start()

        @pl.loop(0, n_chunks, step=2)
        def _(c):
            for b in range(2):
                idx_v, rows_v, sem = bufs[b]
                pltpu.make_async_copy(table_hbm.at[idx_v], rows_v, sem).wait()

                @pl.when(c + b + 1 < n_chunks)
                def _():
                    fetch(c + b + 1, bufs[1 - b]).start()

                pltpu.sync_copy(rows_v, out_hbm.at[rows_of(c + b)])

    return gather(table, idx)


def _sc_row_scatter_pair(rows, idx_lo, idx_hi, n_out):
    n, width = rows.shape
    per_worker = n // (SC_CORES * SC_SUBCORES)
    n_chunks = per_worker // SC_CHUNK_ROWS
    assert n == n_chunks * SC_CHUNK_ROWS * SC_CORES * SC_SUBCORES and n_chunks % 2 == 0
    index_scratch = pltpu.VMEM((SC_CHUNK_ROWS,), jnp.int32)
    rows_scratch = pltpu.VMEM((SC_CHUNK_ROWS, width), rows.dtype)

    @functools.partial(
        pl.kernel, mesh=_sc_mesh(), out_type=jax.ShapeDtypeStruct((n_out, width), rows.dtype),
        scratch_types=[index_scratch, index_scratch, rows_scratch, rows_scratch,
                       pltpu.SemaphoreType.DMA, pltpu.SemaphoreType.DMA],
        name="sc_row_scatter")
    def scatter(rows_hbm, lo_hbm, hi_hbm, out_hbm, lo_v, hi_v, rows_a, rows_b, sem_a, sem_b):
        base = _sc_worker_base(per_worker)
        bufs = ((rows_a, sem_a), (rows_b, sem_b))

        def rows_of(c):
            return pl.ds(pl.multiple_of(base + c * SC_CHUNK_ROWS, SC_CHUNK_ROWS), SC_CHUNK_ROWS)

        def load(c, buf):
            rows_v, sem = buf
            return pltpu.make_async_copy(rows_hbm.at[rows_of(c)], rows_v, sem)

        load(0, bufs[0]).start()

        @pl.loop(0, n_chunks, step=2)
        def _(c):
            for b in range(2):
                rows_v, _ = bufs[b]
                load(c + b, bufs[b]).wait()

                @pl.when(c + b + 1 < n_chunks)
                def _():
                    load(c + b + 1, bufs[1 - b]).start()

                pltpu.sync_copy(lo_hbm.at[rows_of(c + b)], lo_v)
                pltpu.sync_copy(hi_hbm.at[rows_of(c + b)], hi_v)
                pltpu.sync_copy(rows_v, out_hbm.at[lo_v])
                pltpu.sync_copy(rows_v, out_hbm.at[hi_v])

    return scatter(rows, idx_lo, idx_hi)


def _expert_ffn_body(te_ref, nu_ref, nv_ref, h_ref, wg_ref, wu_ref, wd_ref, y_ref, acc_scr):
    i = pl.program_id(0)
    f = pl.program_id(1)

    @pl.when(i < nu_ref[0])
    def _():
        row = lax.broadcasted_iota(jnp.int32, h_ref.shape, 0)
        packed = jnp.where(row < nv_ref[i], h_ref[...], 0)
        y = _swiglu_down(_unpack_bf16_pairs(packed), wg_ref, wu_ref, wd_ref)

        @pl.when(f == 0)
        def _():
            acc_scr[...] = y

        @pl.when(f > 0)
        def _():
            acc_scr[...] += y

        @pl.when(f == pl.num_programs(1) - 1)
        def _():
            y_ref[...] = _pack_bf16_pairs(acc_scr[...])


def _expert_ffn(h_sorted, tile_expert, n_used, tile_valid, w_gate, w_up, w_down, layer):
    rows, half = h_sorted.shape
    d = 2 * half
    ff = w_gate.shape[3]
    n_f = ff // MOE_FF_TILE

    def row_map(i, f, te, nu, nv):
        return (jnp.minimum(i, nu[0] - 1), 0)

    def col_step(i, f, nu):
        return jnp.where(i < nu[0], f, n_f - 1)

    grid_spec = pltpu.PrefetchScalarGridSpec(
        num_scalar_prefetch=3,
        grid=(rows // MOE_ROW_TILE, n_f),
        in_specs=[
            pl.BlockSpec((MOE_ROW_TILE, half), row_map),
            pl.BlockSpec((None, None, d, MOE_FF_TILE),
                         lambda i, f, te, nu, nv: (layer, te[i], 0, col_step(i, f, nu))),
            pl.BlockSpec((None, None, d, MOE_FF_TILE),
                         lambda i, f, te, nu, nv: (layer, te[i], 0, col_step(i, f, nu))),
            pl.BlockSpec((None, None, MOE_FF_TILE, d),
                         lambda i, f, te, nu, nv: (layer, te[i], col_step(i, f, nu), 0)),
        ],
        out_specs=pl.BlockSpec((MOE_ROW_TILE, half), row_map),
        scratch_shapes=[pltpu.VMEM((MOE_ROW_TILE, d), F32)],
    )
    return pl.pallas_call(
        _expert_ffn_body,
        out_shape=jax.ShapeDtypeStruct((rows, half), jnp.int32),
        grid_spec=grid_spec,
        compiler_params=_params("arbitrary", "arbitrary"),
        name="expert_ffn",
    )(tile_expert, n_used, tile_valid, h_sorted, w_gate, w_up, w_down)


def _combine_body(x_ref, y_ref, g_ref, o_ref):
    out = x_ref[...]
    for s in range(2):
        column = jnp.broadcast_to(g_ref[s:s + 1, :], (LANES, g_ref.shape[1])).T
        gate = jnp.concatenate([column] * (out.shape[1] // LANES), axis=1)
        out = out + _unpack_bf16_pairs(y_ref[s]).astype(F32) * gate
    o_ref[...] = out


def _combine_residual(x, y_pairs, gates2):
    t, d = x.shape
    return pl.pallas_call(
        _combine_body,
        out_shape=jax.ShapeDtypeStruct((t, d), F32),
        grid=(t // ROW_TILE,),
        in_specs=[
            pl.BlockSpec((ROW_TILE, d), lambda i: (i, 0)),
            pl.BlockSpec((2, ROW_TILE, d // 2), lambda i: (0, i, 0)),
            pl.BlockSpec((2, ROW_TILE), lambda i: (0, i)),
        ],
        out_specs=pl.BlockSpec((ROW_TILE, d), lambda i: (i, 0)),
        compiler_params=_params("parallel"),
        name="moe_combine",
    )(x, y_pairs, gates2)


def _moe_residual(x, gain, router, w_gate, w_up, w_down, layer):
    t, d = x.shape
    gates, rank, h_packed = _router(x, gain, router)
    rank8 = rank.astype(jnp.int32)
    chosen = rank8 >= 0
    counts = jnp.sum(chosen, axis=1, dtype=jnp.int32)
    padded = (counts + MOE_ROW_TILE - 1) // MOE_ROW_TILE * MOE_ROW_TILE
    ends = jnp.cumsum(padded)
    starts = ends - padded
    pos = starts[:, None] + rank8
    max_rows = 2 * t + N_EXPERTS * MOE_ROW_TILE
    pos_lo = jnp.min(jnp.where(chosen, pos, max_rows), axis=0)
    pos_hi = jnp.max(jnp.where(chosen, pos, -1), axis=0)
    gates2 = jnp.stack([jnp.sum(jnp.where(chosen & (pos == pos_lo[None, :]), gates, 0.0), axis=0),
                        jnp.sum(jnp.where(chosen & (pos == pos_hi[None, :]), gates, 0.0), axis=0)])
    n_tiles = max_rows // MOE_ROW_TILE
    n_used = (ends[-1] // MOE_ROW_TILE).astype(jnp.int32)
    tile_start = jnp.minimum(jnp.arange(n_tiles, dtype=jnp.int32), n_used - 1) * MOE_ROW_TILE
    tile_expert = jnp.sum(tile_start[:, None] >= ends[None, :], axis=1, dtype=jnp.int32)
    tile_valid = jnp.clip((starts + counts)[tile_expert] - tile_start, 0, MOE_ROW_TILE).astype(jnp.int32)

    h_sorted = _sc_row_scatter_pair(h_packed, pos_lo, pos_hi, max_rows)
    y_sorted = _expert_ffn(h_sorted, tile_expert, n_used.reshape(1), tile_valid, w_gate, w_up, w_down, layer)
    y_pairs = _sc_row_gather(y_sorted, jnp.concatenate([pos_lo, pos_hi])).reshape(2, t, d // 2)
    return _combine_residual(x, y_pairs, gates2)


def _split_head_pair(q):
    is_first = lax.broadcasted_iota(jnp.int32, (1, LANES), 1) < HEAD_DIM
    zero = jnp.zeros_like(q)
    return jnp.where(is_first, q, zero), jnp.where(is_first, zero, q)


def _merge_head_pair(first, second):
    is_first = lax.broadcasted_iota(jnp.int32, (1, LANES), 1) < HEAD_DIM
    return jnp.where(is_first, first, second)


def _qk(q, k):
    return lax.dot_general(q, k, (((1,), (1,)), ((), ())), preferred_element_type=F32)


def _transpose_values(v_ref, vt_scr):
    for c in range(v_ref.shape[0] // ATT_BLOCK):
        rows = slice(c * ATT_BLOCK, (c + 1) * ATT_BLOCK)
        vt_scr[:, rows] = v_ref[rows, :].astype(F32).T.astype(vt_scr.dtype)


def _sb_body(q_ref, k_ref, v_ref, u_ref, o_ref, vt_scr, acc_scr, carry_scr, z_scr, keep_scr, sum_scr):
    qi = pl.program_id(2)

    @pl.when(qi == 0)
    def _():
        _transpose_values(v_ref, vt_scr)

    upper = u_ref[...]
    acc_scr[...] = jnp.zeros_like(acc_scr)
    carry_scr[...] = jnp.zeros_like(carry_scr)
    key = lax.broadcasted_iota(jnp.int32, (ATT_BLOCK, ATT_BLOCK), 0)
    query = lax.broadcasted_iota(jnp.int32, (ATT_BLOCK, ATT_BLOCK), 1)
    strict = key < query
    qs = [_split_head_pair(q_ref[j * ATT_BLOCK:(j + 1) * ATT_BLOCK, :]) for j in range(ATT_K_PER_Q)]

    def process(j, kb, keep):
        start = pl.multiple_of(kb * ATT_BLOCK, ATT_BLOCK)
        k = k_ref[pl.ds(start, ATT_BLOCK), :]
        for hh in range(2):
            z = _qk(k, qs[j][hh])
            log_beta = z - _softplus(z)
            log_keep = log_beta - z
            if keep is not None:
                log_keep = jnp.where(keep, log_keep, 0.0)
            remain = jnp.dot(upper, log_keep.astype(BF16), preferred_element_type=F32)
            w = jnp.exp(log_beta + remain + carry_scr[j, hh])
            if keep is not None:
                w = jnp.where(keep, w, 0.0)
            carry_scr[j, hh] += jnp.sum(log_keep, axis=0, keepdims=True)
            vt = vt_scr[pl.ds(hh * HEAD_DIM, HEAD_DIM), pl.ds(start, ATT_BLOCK)]
            acc_scr[j, hh] += jnp.dot(vt, w.astype(BF16), preferred_element_type=F32)

    units = []
    for j in range(ATT_K_PER_Q):
        g = qi * ATT_K_PER_Q + j
        has_previous = None if j > 0 else jnp.broadcast_to(g > 0, strict.shape)
        for kb, keep in ((g, strict), (jnp.maximum(g - 1, 0), has_previous)):
            for hh in range(2):
                units.append((j, hh, pl.multiple_of(kb * ATT_BLOCK, ATT_BLOCK), keep))

    for u, (j, hh, start, keep) in enumerate(units):
        z_scr[u] = _qk(k_ref[pl.ds(start, ATT_BLOCK), :], qs[j][hh])

    for u, (j, hh, start, keep) in enumerate(units):
        z = z_scr[u]
        log_beta = z - _softplus(z)
        log_keep = log_beta - z
        if keep is not None:
            log_keep = jnp.where(keep, log_keep, 0.0)
        z_scr[u] = log_beta
        keep_scr[u] = log_keep.astype(BF16)
        sum_scr[u] = jnp.sum(log_keep, axis=0, keepdims=True)

    for u in range(len(units)):
        z_scr[u] += jnp.dot(upper, keep_scr[u], preferred_element_type=F32)

    for u, (j, hh, start, keep) in enumerate(units):
        diagonal = u % 4 < 2
        log_w = z_scr[u] if diagonal else z_scr[u] + sum_scr[u - 2]
        w = jnp.exp(log_w)
        if keep is not None:
            w = jnp.where(keep, w, 0.0)
        keep_scr[u] = w.astype(BF16)

    for u, (j, hh, start, keep) in enumerate(units):
        vt = vt_scr[pl.ds(hh * HEAD_DIM, HEAD_DIM), pl.ds(start, ATT_BLOCK)]
        acc_scr[j, hh] += jnp.dot(vt, keep_scr[u], preferred_element_type=F32)
        carry_scr[j, hh] += sum_scr[u]

    for j in range(ATT_K_PER_Q):
        def more(kb, j=j):
            return (kb >= 0) & (jnp.max(carry_scr[j]) > SB_UNDERFLOW_LOG)

        def step(kb, j=j):
            process(j, kb, None)
            return kb - 1

        lax.while_loop(more, step, qi * ATT_K_PER_Q + j - 2)

    for j in range(ATT_K_PER_Q):
        out_t = jnp.concatenate([acc_scr[j, 0], acc_scr[j, 1]], axis=0)
        o_ref[j * ATT_BLOCK:(j + 1) * ATT_BLOCK, :] = out_t.T.astype(o_ref.dtype)


def _sb_attention(proj, batch, seq):
    t = proj.shape[0]
    nq = seq // ATT_Q_BLOCK
    idx = jnp.arange(ATT_BLOCK)
    upper = (idx[None, :] > idx[:, None]).astype(BF16)
    return pl.pallas_call(
        _sb_body,
        out_shape=jax.ShapeDtypeStruct((t, D_MODEL), BF16),
        grid=(batch, HEAD_PAIRS, nq),
        in_specs=[
            pl.BlockSpec((ATT_Q_BLOCK, LANES), lambda b, p, i: (b * nq + i, p)),
            pl.BlockSpec((seq, LANES), lambda b, p, i: (b, HEAD_PAIRS + p)),
            pl.BlockSpec((seq, LANES), lambda b, p, i: (b, 2 * HEAD_PAIRS + p)),
            pl.BlockSpec((ATT_BLOCK, ATT_BLOCK), lambda b, p, i: (0, 0)),
        ],
        out_specs=pl.BlockSpec((ATT_Q_BLOCK, LANES), lambda b, p, i: (b * nq + i, p)),
        scratch_shapes=[
            pltpu.VMEM((LANES, seq), BF16),
            pltpu.VMEM((ATT_K_PER_Q, 2, HEAD_DIM, ATT_BLOCK), F32),
            pltpu.VMEM((ATT_K_PER_Q, 2, 1, ATT_BLOCK), F32),
            pltpu.VMEM((4 * ATT_K_PER_Q, ATT_BLOCK, ATT_BLOCK), F32),
            pltpu.VMEM((4 * ATT_K_PER_Q, ATT_BLOCK, ATT_BLOCK), BF16),
            pltpu.VMEM((4 * ATT_K_PER_Q, 1, ATT_BLOCK), F32),
        ],
        compiler_params=_params("parallel", "parallel", "arbitrary"),
        name="sb_attention",
    )(proj, proj, proj, upper)


def _fox_gate_body(x_ref, g_ref, w_ref, b_ref, tri_ref, c_ref):
    h = _rms_normalize(x_ref[...], g_ref[...])
    logits = _dot_split(h, w_ref[...])
    log_f = -_softplus(-(logits + b_ref[...]))
    seq = log_f.shape[0]
    carry = jnp.zeros((1, LANES), F32)
    for blk in range(seq // ATT_BLOCK):
        rows = slice(blk * ATT_BLOCK, (blk + 1) * ATT_BLOCK)
        c = carry
        for part in _split_bf16(log_f[rows], 3):
            c = c + jnp.dot(tri_ref[...], part, preferred_element_type=F32)
        c_ref[rows, :] = c
        carry = c[ATT_BLOCK - 1:ATT_BLOCK, :]


def _fox_cum_log_forget(x, gain, w_gate, b_gate, batch, seq):
    t, d = x.shape
    w = jnp.zeros((d, LANES), F32).at[:, :N_HEADS].set(w_gate)
    b = jnp.zeros((1, LANES), F32).at[0, :N_HEADS].set(b_gate)
    idx = jnp.arange(ATT_BLOCK)
    tri = (idx[:, None] >= idx[None, :]).astype(BF16)
    return pl.pallas_call(
        _fox_gate_body,
        out_shape=jax.ShapeDtypeStruct((t, LANES), F32),
        grid=(batch,),
        in_specs=[
            pl.BlockSpec((seq, d), lambda i: (i, 0)),
            pl.BlockSpec((1, d), lambda i: (0, 0)),
            pl.BlockSpec((d, LANES), lambda i: (0, 0)),
            pl.BlockSpec((1, LANES), lambda i: (0, 0)),
            pl.BlockSpec((ATT_BLOCK, ATT_BLOCK), lambda i: (0, 0)),
        ],
        out_specs=pl.BlockSpec((seq, LANES), lambda i: (i, 0)),
        compiler_params=_params("parallel"),
        name="fox_gate",
    )(x, gain.reshape(1, d), w, b, tri)


def _fox_body(q_ref, k_ref, v_ref, cq_ref, ck_ref, o_ref, vt_scr, ckb_scr, a_scr, p_scr):
    qi = pl.program_id(2)

    @pl.when(qi == 0)
    def _():
        _transpose_values(v_ref, vt_scr)
        for hh in range(2):
            for c in range(ckb_scr.shape[1] // ATT_BLOCK):
                rows = slice(c * ATT_BLOCK, (c + 1) * ATT_BLOCK)
                ckb_scr[hh, rows, :] = jnp.broadcast_to(ck_ref[0, hh, :, rows], (LANES, ATT_BLOCK)).T

    qs = _split_head_pair(q_ref[...])
    key = lax.broadcasted_iota(jnp.int32, (ATT_BLOCK, ATT_Q_BLOCK), 0)
    query = lax.broadcasted_iota(jnp.int32, (ATT_BLOCK, ATT_Q_BLOCK), 1)
    row_shape = (1, ATT_Q_BLOCK)

    def scores(kb, tops, diagonal):
        start = kb * ATT_BLOCK
        k = k_ref[pl.ds(start, ATT_BLOCK), :]
        if diagonal:
            causal = key + (kb * ATT_BLOCK - qi * ATT_Q_BLOCK) <= query
        new_tops = []
        for hh in range(2):
            ck = ckb_scr[hh, pl.ds(start, ATT_BLOCK), :]
            a = _qk(k, qs[hh]) - jnp.concatenate([ck] * (ATT_Q_BLOCK // LANES), axis=1)
            if diagonal:
                a = jnp.where(causal, a, NEG_INF)
            a_scr[hh, kb] = a
            new_tops.append(jnp.maximum(tops[hh], jnp.max(a, axis=0, keepdims=True)))
        return tuple(new_tops)

    def sweep(n_blocks):
        tops = (jnp.full(row_shape, NEG_INF, F32),) * 2
        for kb in range(n_blocks):
            tops = scores(kb, tops, kb >= n_blocks - ATT_K_PER_Q)
        shifts = tuple(cq_ref[0, hh] - (tops[hh] + cq_ref[0, hh]) for hh in range(2))
        sums = [jnp.zeros(row_shape, F32)] * 2
        for kb in range(n_blocks):
            for hh in range(2):
                p = jnp.exp(a_scr[hh, kb] + shifts[hh])
                p_scr[hh, kb] = p.astype(BF16)
                sums[hh] = sums[hh] + jnp.sum(p, axis=0, keepdims=True)
        accs = [jnp.zeros((HEAD_DIM, ATT_Q_BLOCK), F32)] * 2
        for kb in range(n_blocks):
            for hh in range(2):
                vt = vt_scr[hh * HEAD_DIM:(hh + 1) * HEAD_DIM, kb * ATT_BLOCK:(kb + 1) * ATT_BLOCK]
                accs[hh] = accs[hh] + jnp.dot(vt, p_scr[hh, kb], preferred_element_type=F32)
        out_t = jnp.concatenate([accs[0] / sums[0], accs[1] / sums[1]], axis=0)
        o_ref[...] = out_t.T.astype(o_ref.dtype)

    for q in range(k_ref.shape[0] // ATT_Q_BLOCK):
        pl.when(qi == q)(functools.partial(sweep, (q + 1) * ATT_K_PER_Q))


def _fox_attention(proj, cum, batch, seq):
    t = proj.shape[0]
    nq = seq // ATT_Q_BLOCK
    cum_h = cum[:, :N_HEADS].reshape(batch, seq, N_HEADS).transpose(0, 2, 1)
    cum_rows = cum_h.reshape(batch, N_HEADS, 1, seq)
    return pl.pallas_call(
        _fox_body,
        out_shape=jax.ShapeDtypeStruct((t, D_MODEL), BF16),
        grid=(batch, HEAD_PAIRS, nq),
        in_specs=[
            pl.BlockSpec((ATT_Q_BLOCK, LANES), lambda b, p, i: (b * nq + i, p)),
            pl.BlockSpec((seq, LANES), lambda b, p, i: (b, HEAD_PAIRS + p)),
            pl.BlockSpec((seq, LANES), lambda b, p, i: (b, 2 * HEAD_PAIRS + p)),
            pl.BlockSpec((1, 2, 1, ATT_Q_BLOCK), lambda b, p, i: (b, p, 0, i)),
            pl.BlockSpec((1, 2, 1, seq), lambda b, p, i: (b, p, 0, 0)),
        ],
        out_specs=pl.BlockSpec((ATT_Q_BLOCK, LANES), lambda b, p, i: (b * nq + i, p)),
        scratch_shapes=[
            pltpu.VMEM((LANES, seq), BF16),
            pltpu.VMEM((2, seq, LANES), F32),
            pltpu.VMEM((2, seq // ATT_BLOCK, ATT_BLOCK, ATT_Q_BLOCK), F32),
            pltpu.VMEM((2, seq // ATT_BLOCK, ATT_BLOCK, ATT_Q_BLOCK), BF16),
        ],
        compiler_params=_params("parallel", "parallel", "arbitrary"),
        name="fox_attention",
    )(proj, proj, proj, cum_rows, cum_rows)


def _band_body(q_ref, k_ref, v_ref, bias_ref, o_ref, s_ref, vt_scr, logit_scr, p_scr, inv_scr, *, sub_len):
    seq = v_ref.shape[0]
    n_blocks = seq // DIL_BLOCK
    _transpose_values(v_ref, vt_scr)

    def key_rows(n):
        first = (n * DIL_BLOCK) % sub_len == 0
        return first, slice((n if first else n - 1) * DIL_BLOCK, (n + 1) * DIL_BLOCK)

    for n in range(n_blocks):
        first, k_rows = key_rows(n)
        qs = _split_head_pair(q_ref[n * DIL_BLOCK:(n + 1) * DIL_BLOCK, :])
        k = k_ref[k_rows, :]
        for hh in range(2):
            bias = bias_ref[hh, DIL_BLOCK:, :] if first else bias_ref[hh]
            logit_scr[2 * n + hh, :k.shape[0], :] = _qk(k, qs[hh]) + bias

    for n in range(n_blocks):
        first, k_rows = key_rows(n)
        n_keys = k_rows.stop - k_rows.start
        for hh in range(2):
            logits = logit_scr[2 * n + hh, :n_keys, :]
            m = jnp.max(logits, axis=0, keepdims=True)
            p = jnp.exp(logits - m)
            l = jnp.sum(p, axis=0, keepdims=True)
            p_scr[2 * n + hh, :n_keys, :] = p.astype(BF16)
            inv_scr[2 * n + hh] = 1.0 / l
            s_ref[0, 0, hh:hh + 1, n * DIL_BLOCK:(n + 1) * DIL_BLOCK] = m + jnp.log(l)

    for n in range(n_blocks):
        first, k_rows = key_rows(n)
        n_keys = k_rows.stop - k_rows.start
        outs = []
        for hh in range(2):
            vt = vt_scr[hh * HEAD_DIM:(hh + 1) * HEAD_DIM, k_rows]
            o = jnp.dot(vt, p_scr[2 * n + hh, :n_keys, :], preferred_element_type=F32)
            outs.append(o * inv_scr[2 * n + hh])
        o_ref[n * DIL_BLOCK:(n + 1) * DIL_BLOCK, :] = jnp.concatenate(outs, axis=0).T.astype(o_ref.dtype)


def _band_attention(qk, v, bias_t, batch, seq, sub_len):
    t = v.shape[0]
    return pl.pallas_call(
        functools.partial(_band_body, sub_len=sub_len),
        out_shape=(jax.ShapeDtypeStruct((t, D_MODEL), BF16),
                   jax.ShapeDtypeStruct((batch, HEAD_PAIRS, 2, seq), F32)),
        grid=(batch, HEAD_PAIRS),
        in_specs=[
            pl.BlockSpec((None, seq, LANES), lambda b, p: (0, b, p)),
            pl.BlockSpec((None, seq, LANES), lambda b, p: (1, b, p)),
            pl.BlockSpec((seq, LANES), lambda b, p: (b, p)),
            pl.BlockSpec((2, 2 * DIL_BLOCK, DIL_BLOCK), lambda b, p: (p, 0, 0)),
        ],
        out_specs=(pl.BlockSpec((seq, LANES), lambda b, p: (b, p)),
                   pl.BlockSpec((1, 1, 2, seq), lambda b, p: (b, p, 0, 0))),
        scratch_shapes=[pltpu.VMEM((LANES, seq), BF16),
                        pltpu.VMEM((2 * seq // DIL_BLOCK, 2 * DIL_BLOCK, DIL_BLOCK), F32),
                        pltpu.VMEM((2 * seq // DIL_BLOCK, 2 * DIL_BLOCK, DIL_BLOCK), BF16),
                        pltpu.VMEM((2 * seq // DIL_BLOCK, 1, DIL_BLOCK), F32)],
        compiler_params=_params("parallel", "parallel"),
        name="band_attention",
    )(qk, qk, v, bias_t)


def _softmax_merge_body(o1_ref, o2_ref, o3_ref, s_ref, e_ref, o_ref):
    s = s_ref[...]
    groups = [s, pltpu.roll(s, LANES - N_HEADS, axis=1), pltpu.roll(s, LANES - 2 * N_HEADS, axis=1)]
    top = jnp.maximum(jnp.maximum(groups[0], groups[1]), groups[2])
    weights = [jnp.exp(g - top) for g in groups]
    inv = 1.0 / (weights[0] + weights[1] + weights[2])
    out = jnp.zeros(o_ref.shape, F32)
    for w, part in zip(weights, (o1_ref, o2_ref, o3_ref)):
        spread = jnp.dot((w * inv).astype(BF16), e_ref[...], preferred_element_type=F32)
        out = out + spread * part[...].astype(F32)
    o_ref[...] = out.astype(o_ref.dtype)


def _softmax_merge(outs, s_all):
    t, d = outs[0].shape
    head_of = jnp.arange(d) // HEAD_DIM
    expand = (jnp.arange(LANES)[:, None] == head_of[None, :]).astype(BF16)
    rows = pl.BlockSpec((ROW_TILE, d), lambda i: (i, 0))
    return pl.pallas_call(
        _softmax_merge_body,
        out_shape=jax.ShapeDtypeStruct((t, d), BF16),
        grid=(t // ROW_TILE,),
        in_specs=[rows, rows, rows,
                  pl.BlockSpec((ROW_TILE, LANES), lambda i: (i, 0)),
                  pl.BlockSpec((LANES, d), lambda i: (0, 0))],
        out_specs=rows,
        compiler_params=_params("parallel"),
        name="softmax_merge",
    )(*outs, s_all, expand)


def _t5_causal_bucket(distance):
    max_exact = N_REL_BUCKETS // 2
    d = jnp.maximum(distance, 1).astype(F32)
    log_b = max_exact + (jnp.log(d / max_exact) / math.log(REL_MAX_DISTANCE / max_exact)
                         * (N_REL_BUCKETS - max_exact)).astype(jnp.int32)
    log_b = jnp.minimum(log_b, N_REL_BUCKETS - 1)
    return jnp.where(distance < max_exact, distance, log_b)


def _dilated_bias(rel_bias):
    kj = jnp.arange(2 * DIL_BLOCK, dtype=jnp.int32)
    qi = jnp.arange(DIL_BLOCK, dtype=jnp.int32)
    delta = qi[None, :] + DIL_BLOCK - kj[:, None]
    in_band = (delta >= 0) & (delta <= DIL_SPAN)
    buckets = jnp.stack([_t5_causal_bucket(jnp.maximum(delta, 0) * dil) for _, dil in DILATED_PAIRS])
    one_hot = (buckets[..., None] == jnp.arange(N_REL_BUCKETS)).astype(F32)
    bias = jnp.einsum("gkqb,bh->ghkq", one_hot, rel_bias.astype(F32), precision=lax.Precision.HIGHEST)
    return jnp.where(in_band[None, None], bias, NEG_INF)


def _dilated_proj_body(x_ref, g_ref, w_ref, cs_ref, bd_ref, *rest):
    n_groups = len(DILATED_PAIRS)
    qk_refs, v_refs = rest[:n_groups], rest[n_groups:2 * n_groups]
    h_scr, res_scr = rest[2 * n_groups:]
    j = pl.program_id(1)

    @pl.when(j == 0)
    def _():
        h_scr[...] = _rms_normalize(x_ref[...], g_ref[...]).astype(BF16)

    acc = jnp.dot(h_scr[...], w_ref[...], preferred_element_type=F32)

    def keep(res, first_lane):
        for c in range(res.shape[1] // LANES):
            res_scr[first_lane // LANES + c] = res[:, c * LANES:(c + 1) * LANES]

    @pl.when(j < 2 * n_groups)
    def _():
        for c in range(COL_TILE // NORM_CHUNK):
            sl = slice(c * NORM_CHUNK, (c + 1) * NORM_CHUNK)
            a = acc[:, sl]
            ss = jnp.dot((a * a).astype(BF16), bd_ref[...], preferred_element_type=F32)
            keep(a * lax.rsqrt(ss * (1.0 / HEAD_DIM) + RMS_EPS) * cs_ref[:, sl], c * NORM_CHUNK)

    @pl.when(j == 2 * n_groups)
    def _():
        keep(acc * cs_ref[...], 0)

    def write_classes(ref, dil):
        rows = res_scr.shape[1] // dil
        for c in range(res_scr.shape[0]):
            lanes = slice(c * LANES, (c + 1) * LANES)
            if dil == 1:
                ref[:, lanes] = res_scr[c].astype(ref.dtype)
            else:
                for r in range(dil):
                    ref[r, :, lanes] = res_scr[c, pl.ds(r, rows, stride=dil), :].astype(ref.dtype)

    for g, (_, dil) in enumerate(DILATED_PAIRS):
        pl.when(j // 2 == g)(functools.partial(write_classes, qk_refs[g], dil))
        pl.when(j == 2 * n_groups)(functools.partial(write_classes, v_refs[g], dil))


def _dilated_proj(x, gain, w, layer, col_scale, batch, seq):
    t, d = x.shape
    n = w.shape[2]
    n_groups = len(DILATED_PAIRS)
    tiles_per_seq = seq // DIL_ROW_TILE
    head_id = jnp.arange(NORM_CHUNK) // HEAD_DIM
    block_diag = (head_id[:, None] == head_id[None, :]).astype(BF16)
    out_shape, out_specs = [], []
    for kind in ("qk", "v"):
        for g, (_, dil) in enumerate(DILATED_PAIRS):
            rows = DIL_ROW_TILE // dil
            if kind == "qk":
                shape = (2, batch, dil, seq // dil, d)
                block = (None, None, dil, rows, d)
                index = lambda i, j, g=g: (jnp.clip(j - 2 * g, 0, 1), i // tiles_per_seq, 0, i % tiles_per_seq, 0)
            else:
                shape = (batch, dil, seq // dil, d)
                block = (None, dil, rows, d)
                index = lambda i, j: (i // tiles_per_seq, 0, i % tiles_per_seq, 0)
            if dil == 1:
                block = block[:-3] + (None,) + block[-2:]
            out_shape.append(jax.ShapeDtypeStruct(shape, BF16))
            out_specs.append(pl.BlockSpec(block, index))
    outs = pl.pallas_call(
        _dilated_proj_body,
        out_shape=tuple(out_shape),
        grid=(t // DIL_ROW_TILE, n // COL_TILE),
        in_specs=[
            pl.BlockSpec((DIL_ROW_TILE, d), lambda i, j: (i, 0)),
            pl.BlockSpec((1, d), lambda i, j: (0, 0)),
            pl.BlockSpec((None, d, COL_TILE), lambda i, j: (layer, 0, j)),
            pl.BlockSpec((1, COL_TILE), lambda i, j: (0, j)),
            pl.BlockSpec((NORM_CHUNK, NORM_CHUNK), lambda i, j: (0, 0)),
        ],
        out_specs=tuple(out_specs),
        scratch_shapes=[pltpu.VMEM((DIL_ROW_TILE, d), BF16),
                        pltpu.VMEM((COL_TILE // LANES, DIL_ROW_TILE, LANES), F32)],
        compiler_params=_params("parallel", "arbitrary"),
        name="dilated_proj",
    )(x, gain.reshape(1, d), w, col_scale.reshape(1, n).astype(F32), block_diag)
    qk = [o.reshape(2, t, d) for o in outs[:n_groups]]
    v = [o.reshape(t, d) for o in outs[n_groups:]]
    return qk, v


def _dilated_attention(qk, v, rel_bias, batch, seq):
    t = v[0].shape[0]
    n_groups = len(DILATED_PAIRS)
    bias_t = _dilated_bias(rel_bias)
    outs, stats = [], []
    for g, (_, dil) in enumerate(DILATED_PAIRS):
        sub_len = seq // dil
        o, s = _band_attention(qk[g], v[g], bias_t[g], batch, seq, sub_len)
        o = o.reshape(batch, dil, sub_len, D_MODEL).transpose(0, 2, 1, 3).reshape(t, D_MODEL)
        s = s.reshape(batch, HEAD_PAIRS, 2, dil, sub_len).swapaxes(3, 4)
        outs.append(o)
        stats.append(s.reshape(batch, N_HEADS, seq).transpose(0, 2, 1).reshape(t, N_HEADS))
    stats.append(jnp.zeros((t, LANES - n_groups * N_HEADS), F32))
    return _softmax_merge(outs, jnp.concatenate(stats, axis=1))


def _tile_heads(v):
    return jnp.tile(v.astype(F32), N_HEADS)


def _sb_mixer(x, gain, w_qkv, w_o, layer, batch, seq):
    ones = jnp.ones((D_MODEL,), F32)
    col_scale = jnp.concatenate([ones * QK_SCALE, ones, ones])
    proj = _norm_proj(x, gain, w_qkv, layer, col_scale, 0)
    return _out_proj_residual(x, _sb_attention(proj, batch, seq), w_o, layer)


def _dilated_mixer(x, gain, w_in, q_norm, k_norm, rel_bias, w_o, layer, batch, seq):
    scales = []
    for g in range(len(DILATED_PAIRS)):
        scales += [_tile_heads(q_norm[g]) * QK_SCALE, _tile_heads(k_norm[g])]
    scales.append(jnp.ones((D_MODEL,), F32))
    qk, v = _dilated_proj(x, gain, w_in, layer, jnp.concatenate(scales), batch, seq)
    return _out_proj_residual(x, _dilated_attention(qk, v, rel_bias, batch, seq), w_o, layer)


def _fox_mixer(x, gain, w_qkv, w_gate, b_f, q_norm, k_norm, w_o, layer, batch, seq):
    col_scale = jnp.concatenate([_tile_heads(q_norm) * QK_SCALE, _tile_heads(k_norm),
                                 jnp.ones((D_MODEL,), F32)])
    proj = _norm_proj(x, gain, w_qkv, layer, col_scale, 2)
    cum = _fox_cum_log_forget(x, gain, w_gate, b_f, batch, seq)
    return _out_proj_residual(x, _fox_attention(proj, cum, batch, seq), w_o, layer)


def kernel(x, sb_w_qkv, sb_w_o, dil_w_in, dil_q_norm, dil_k_norm, dil_w_o, fox_w_in, fox_b_f,
           fox_q_norm, fox_k_norm, fox_w_o, rel_bias, attn_norm, ffn_norm, mlp_w_gate, mlp_w_up,
           mlp_w_down, moe_router, moe_w_gate, moe_w_up, moe_w_down):
    batch, seq, d = x.shape
    depth = attn_norm.shape[0]
    sb_w_qkv, sb_w_o, dil_w_in, dil_w_o, fox_w_o, mlp_w_gate, mlp_w_up, mlp_w_down = (
        w.astype(BF16) for w in (sb_w_qkv, sb_w_o, dil_w_in, dil_w_o, fox_w_o, mlp_w_gate, mlp_w_up,
                                 mlp_w_down))
    moe_w_gate, moe_w_up, moe_w_down = (w.astype(BF16) for w in (moe_w_gate, moe_w_up, moe_w_down))
    fox_w_qkv = fox_w_in[:, :, :3 * D_MODEL].astype(BF16)
    h = x.reshape(batch * seq, d)
    for i in range(depth):
        kind, j = i % 3, i // 3
        if kind == 0:
            h = _sb_mixer(h, attn_norm[i], sb_w_qkv, sb_w_o, j, batch, seq)
        elif kind == 1:
            h = _dilated_mixer(h, attn_norm[i], dil_w_in, dil_q_norm[j], dil_k_norm[j], rel_bias,
                               dil_w_o, j, batch, seq)
        else:
            h = _fox_mixer(h, attn_norm[i], fox_w_qkv, fox_w_in[j, :, 3 * D_MODEL:], fox_b_f[j],
                           fox_q_norm[j], fox_k_norm[j], fox_w_o, j, batch, seq)
        f = i // 2
        if i % 2 == 0:
            h = _ffn_residual(h, ffn_norm[i], mlp_w_gate, mlp_w_up, mlp_w_down, f)
        else:
            h = _moe_residual(h, ffn_norm[i], moe_router[f], moe_w_gate, moe_w_up, moe_w_down, f)
    return h.reshape(batch, seq, d)
```

```python
import functools
import math

import jax
import jax.numpy as jnp
from jax import lax
from jax.experimental import pallas as pl
from jax.experimental.pallas import tpu as pltpu
from jax.experimental.pallas import tpu_sc as plsc

D_MODEL = 1024
N_HEADS = 16
HEAD_DIM = 64
LANES = 128
HEAD_PAIRS = D_MODEL // LANES
D_FF = 3584
N_EXPERTS = 8
N_REL_BUCKETS = 32
REL_MAX_DISTANCE = 2048
DILATED_PAIRS = ((128, 1), (512, 4), (2048, 16))
DIL_SPAN = 128
RMS_EPS = 1e-6
NEG_INF = -1e30
SB_UNDERFLOW_LOG = -104.0
QK_SCALE = 1.0 / math.sqrt(HEAD_DIM)

ROW_TILE = 1024
COL_TILE = 1024
FF_TILE = 1792
SWIGLU_CHUNK = 256
MOE_ROW_TILE = 512
MOE_FF_TILE = 1792
SC_CORES = 2
SC_SUBCORES = 16
SC_CHUNK_ROWS = 64
NORM_CHUNK = 256
ATT_BLOCK = 256
ATT_Q_BLOCK = 512
ATT_K_PER_Q = ATT_Q_BLOCK // ATT_BLOCK
DIL_BLOCK = 128
DIL_ROW_TILE = 512
VMEM_LIMIT = 56 * 1024 * 1024

F32 = jnp.float32
BF16 = jnp.bfloat16


def _params(*semantics):
    return pltpu.CompilerParams(dimension_semantics=semantics, vmem_limit_bytes=VMEM_LIMIT)


def _rms_normalize(x, gain):
    inv = lax.rsqrt(jnp.mean(x * x, axis=-1, keepdims=True) + RMS_EPS)
    return x * inv * gain


def _split_bf16(x, terms):
    parts = []
    for _ in range(terms):
        part = x.astype(BF16)
        parts.append(part)
        x = x - part.astype(F32)
    return parts


def _dot_split(a, b):
    a_hi, a_lo = _split_bf16(a, 2)
    b_hi, b_lo = _split_bf16(b, 2)
    return (jnp.dot(a_hi, b_hi, preferred_element_type=F32)
            + (jnp.dot(a_hi, b_lo, preferred_element_type=F32)
               + jnp.dot(a_lo, b_hi, preferred_element_type=F32)))


def _softplus(z):
    return jnp.maximum(z, 0.0) + jnp.log(1.0 + jnp.exp(-jnp.abs(z)))


def _norm_proj_body(x_ref, g_ref, w_ref, cs_ref, bd_ref, o_ref, h_scr, *, n_norm):
    j = pl.program_id(1)

    @pl.when(j == 0)
    def _():
        h_scr[...] = _rms_normalize(x_ref[...], g_ref[...]).astype(BF16)

    acc = jnp.dot(h_scr[...], w_ref[...], preferred_element_type=F32)

    def plain():
        o_ref[...] = (acc * cs_ref[...]).astype(o_ref.dtype)

    def head_normed():
        for c in range(COL_TILE // NORM_CHUNK):
            sl = slice(c * NORM_CHUNK, (c + 1) * NORM_CHUNK)
            a = acc[:, sl]
            ss = jnp.dot((a * a).astype(BF16), bd_ref[...], preferred_element_type=F32)
            inv = lax.rsqrt(ss * (1.0 / HEAD_DIM) + RMS_EPS)
            o_ref[:, sl] = (a * inv * cs_ref[:, sl]).astype(o_ref.dtype)

    if n_norm == 0:
        plain()
    else:
        pl.when(j < n_norm)(head_normed)
        pl.when(j >= n_norm)(plain)


def _norm_proj(x, gain, w, layer, col_scale, n_norm):
    t, d = x.shape
    n = w.shape[2]
    head_id = jnp.arange(NORM_CHUNK) // HEAD_DIM
    block_diag = (head_id[:, None] == head_id[None, :]).astype(BF16)
    return pl.pallas_call(
        functools.partial(_norm_proj_body, n_norm=n_norm),
        out_shape=jax.ShapeDtypeStruct((t, n), BF16),
        grid=(t // ROW_TILE, n // COL_TILE),
        in_specs=[
            pl.BlockSpec((ROW_TILE, d), lambda i, j: (i, 0)),
            pl.BlockSpec((1, d), lambda i, j: (0, 0)),
            pl.BlockSpec((None, d, COL_TILE), lambda i, j: (layer, 0, j)),
            pl.BlockSpec((1, COL_TILE), lambda i, j: (0, j)),
            pl.BlockSpec((NORM_CHUNK, NORM_CHUNK), lambda i, j: (0, 0)),
        ],
        out_specs=pl.BlockSpec((ROW_TILE, COL_TILE), lambda i, j: (i, j)),
        scratch_shapes=[pltpu.VMEM((ROW_TILE, d), BF16)],
        compiler_params=_params("parallel", "arbitrary"),
        name="norm_proj",
    )(x, gain.reshape(1, d), w, col_scale.reshape(1, n).astype(F32), block_diag)


def _out_proj_body(x_ref, o_ref, w_ref, y_ref):
    y_ref[...] = x_ref[...] + jnp.dot(o_ref[...], w_ref[...], preferred_element_type=F32)


def _out_proj_residual(x, o, w, layer):
    t, d = x.shape
    return pl.pallas_call(
        _out_proj_body,
        out_shape=jax.ShapeDtypeStruct((t, d), F32),
        grid=(t // ROW_TILE,),
        in_specs=[
            pl.BlockSpec((ROW_TILE, d), lambda i: (i, 0)),
            pl.BlockSpec((ROW_TILE, d), lambda i: (i, 0)),
            pl.BlockSpec((None, d, d), lambda i: (layer, 0, 0)),
        ],
        out_specs=pl.BlockSpec((ROW_TILE, d), lambda i: (i, 0)),
        compiler_params=_params("parallel"),
        name="out_proj",
    )(x, o, w)


def _swiglu_hidden(h, wg, wu):
    g = jnp.dot(h, wg, preferred_element_type=F32)
    u = jnp.dot(h, wu, preferred_element_type=F32)
    return g * (1.0 / (1.0 + jnp.exp(-g))) * u


def _swiglu_down(h, wg_ref, wu_ref, wd_ref):
    y = None
    for c in range(wg_ref.shape[1] // SWIGLU_CHUNK):
        cols = slice(c * SWIGLU_CHUNK, (c + 1) * SWIGLU_CHUNK)
        a = _swiglu_hidden(h, wg_ref[:, cols], wu_ref[:, cols]).astype(BF16)
        part = jnp.dot(a, wd_ref[cols, :], preferred_element_type=F32)
        y = part if y is None else y + part
    return y


def _ffn_body(x_ref, o_ref, wo_ref, g_ref, wg_ref, wu_ref, wd_ref, y_ref, h_scr):
    f = pl.program_id(1)

    @pl.when(f == 0)
    def _():
        x = x_ref[...] + jnp.dot(o_ref[...], wo_ref[...], preferred_element_type=F32)
        h_scr[...] = _rms_normalize(x, g_ref[...]).astype(BF16)
        y_ref[...] = x

    y_ref[...] += _swiglu_down(h_scr[...], wg_ref, wu_ref, wd_ref)


def _mix_ffn_residual(x, o, w_o, mix_layer, gain, w_gate, w_up, w_down, layer):
    t, d = x.shape
    ff = w_gate.shape[2]
    return pl.pallas_call(
        _ffn_body,
        out_shape=jax.ShapeDtypeStruct((t, d), F32),
        grid=(t // ROW_TILE, ff // FF_TILE),
        in_specs=[
            pl.BlockSpec((ROW_TILE, d), lambda i, f: (i, 0)),
            pl.BlockSpec((ROW_TILE, d), lambda i, f: (i, 0)),
            pl.BlockSpec((None, d, d), lambda i, f: (mix_layer, 0, 0)),
            pl.BlockSpec((1, d), lambda i, f: (0, 0)),
            pl.BlockSpec((None, d, FF_TILE), lambda i, f: (layer, 0, f)),
            pl.BlockSpec((None, d, FF_TILE), lambda i, f: (layer, 0, f)),
            pl.BlockSpec((None, FF_TILE, d), lambda i, f: (layer, f, 0)),
        ],
        out_specs=pl.BlockSpec((ROW_TILE, d), lambda i, f: (i, 0)),
        scratch_shapes=[pltpu.VMEM((ROW_TILE, d), BF16)],
        compiler_params=_params("parallel", "arbitrary"),
        name="ffn",
    )(x, o, w_o, gain.reshape(1, d), w_gate, w_up, w_down)


def _pack_bf16_pairs(x):
    half = x.shape[1] // 2
    bits = pltpu.bitcast(x.astype(BF16).astype(F32), jnp.int32)
    return bits[:, :half] | lax.shift_right_logical(bits[:, half:], jnp.int32(16))


def _unpack_bf16_pairs(p):
    left = pltpu.bitcast(p & jnp.int32(-65536), F32)
    right = pltpu.bitcast(lax.shift_left(p, jnp.int32(16)), F32)
    return jnp.concatenate([left, right], axis=1).astype(BF16)


def _router_body(x_ref, g_ref, r_ref, tri_ref, gates_ref, rank_ref, h_ref, count_scr):
    @pl.when(pl.program_id(0) == 0)
    def _():
        count_scr[...] = jnp.zeros_like(count_scr)

    h = _rms_normalize(x_ref[...], g_ref[...])
    h_ref[...] = _pack_bf16_pairs(h)
    logits = _dot_split(h, r_ref[...])
    lane = lax.broadcasted_iota(jnp.int32, logits.shape, 1).astype(F32)
    logits = jnp.where(lane < N_EXPERTS, logits, -jnp.inf)
    m1 = jnp.max(logits, axis=-1, keepdims=True)
    i1 = jnp.min(jnp.where(logits == m1, lane, float(LANES)), axis=-1, keepdims=True)
    rest = jnp.where(lane == i1, -jnp.inf, logits)
    m2 = jnp.max(rest, axis=-1, keepdims=True)
    i2 = jnp.min(jnp.where(rest == m2, lane, float(LANES)), axis=-1, keepdims=True)
    e = jnp.exp(m2 - m1)
    g1 = 1.0 / (1.0 + e)
    gates = jnp.where(lane == i1, g1, 0.0) + jnp.where(lane == i2, e * g1, 0.0)
    gates_ref[...] = gates.T[:N_EXPERTS, :]
    chosen = jnp.where((lane == i1) | (lane == i2), 1.0, 0.0)
    inclusive = jnp.dot(tri_ref[...], chosen.astype(BF16), preferred_element_type=F32)
    rank = jnp.where(chosen > 0.0, inclusive - 1.0 + count_scr[...], -1.0)
    rank_ref[...] = rank.T[:N_EXPERTS, :]
    count_scr[...] += inclusive[ROW_TILE - 1:ROW_TILE, :]


def _router(x, gain, router):
    t, d = x.shape
    r = jnp.zeros((d, LANES), F32).at[:, :N_EXPERTS].set(router)
    idx = jnp.arange(ROW_TILE)
    tri = (idx[:, None] >= idx[None, :]).astype(BF16)
    return pl.pallas_call(
        _router_body,
        out_shape=(jax.ShapeDtypeStruct((N_EXPERTS, t), F32), jax.ShapeDtypeStruct((N_EXPERTS, t), F32),
                   jax.ShapeDtypeStruct((t, d // 2), jnp.int32)),
        grid=(t // ROW_TILE,),
        in_specs=[
            pl.BlockSpec((ROW_TILE, d), lambda i: (i, 0)),
            pl.BlockSpec((1, d), lambda i: (0, 0)),
            pl.BlockSpec((d, LANES), lambda i: (0, 0)),
            pl.BlockSpec((ROW_TILE, ROW_TILE), lambda i: (0, 0)),
        ],
        out_specs=(pl.BlockSpec((N_EXPERTS, ROW_TILE), lambda i: (0, i)),
                   pl.BlockSpec((N_EXPERTS, ROW_TILE), lambda i: (0, i)),
                   pl.BlockSpec((ROW_TILE, d // 2), lambda i: (i, 0))),
        scratch_shapes=[pltpu.VMEM((1, LANES), F32)],
        compiler_params=_params("arbitrary"),
        name="router",
    )(x, gain.reshape(1, d), r, tri)


def _sc_mesh():
    return plsc.VectorSubcoreMesh(core_axis_name="core", subcore_axis_name="subcore",
                                  num_cores=SC_CORES, num_subcores=SC_SUBCORES)


def _sc_worker_base(per_worker):
    return (lax.axis_index("subcore") * SC_CORES + lax.axis_index("core")) * per_worker


def _sc_row_gather(table, idx):
    width = table.shape[1]
    n = idx.shape[0]
    per_worker = n // (SC_CORES * SC_SUBCORES)
    n_chunks = per_worker // SC_CHUNK_ROWS
    assert n == n_chunks * SC_CHUNK_ROWS * SC_CORES * SC_SUBCORES and n_chunks % 2 == 0

    @functools.partial(
        pl.kernel, mesh=_sc_mesh(), out_type=jax.ShapeDtypeStruct((n, width), table.dtype),
        scratch_types=[pltpu.VMEM((SC_CHUNK_ROWS,), jnp.int32), pltpu.VMEM((SC_CHUNK_ROWS,), jnp.int32),
                       pltpu.VMEM((SC_CHUNK_ROWS, width), table.dtype),
                       pltpu.VMEM((SC_CHUNK_ROWS, width), table.dtype),
                       pltpu.SemaphoreType.DMA, pltpu.SemaphoreType.DMA],
        name="sc_row_gather")
    def gather(table_hbm, idx_hbm, out_hbm, idx_a, idx_b, rows_a, rows_b, sem_a, sem_b):
        base = _sc_worker_base(per_worker)
        bufs = ((idx_a, rows_a, sem_a), (idx_b, rows_b, sem_b))

        def rows_of(c):
            return pl.ds(pl.multiple_of(base + c * SC_CHUNK_ROWS, SC_CHUNK_ROWS), SC_CHUNK_ROWS)

        def fetch(c, buf):
            idx_v, rows_v, sem = buf
            pltpu.sync_copy(idx_hbm.at[rows_of(c)], idx_v)
            return pltpu.make_async_copy(table_hbm.at[idx_v], rows_v, sem)

        fetch(0, bufs[0]).start()

        @pl.loop(0, n_chunks, step=2)
        def _(c):
            for b in range(2):
                idx_v, rows_v, sem = bufs[b]
                pltpu.make_async_copy(table_hbm.at[idx_v], rows_v, sem).wait()

                @pl.when(c + b + 1 < n_chunks)
                def _():
                    fetch(c + b + 1, bufs[1 - b]).start()

                pltpu.sync_copy(rows_v, out_hbm.at[rows_of(c + b)])

    return gather(table, idx)


def _sc_row_scatter_pair(rows, idx_lo, idx_hi, n_out):
    n, width = rows.shape
    per_worker = n // (SC_CORES * SC_SUBCORES)
    n_chunks = per_worker // SC_CHUNK_ROWS
    assert n == n_chunks * SC_CHUNK_ROWS * SC_CORES * SC_SUBCORES and n_chunks % 2 == 0
    index_scratch = pltpu.VMEM((SC_CHUNK_ROWS,), jnp.int32)
    rows_scratch = pltpu.VMEM((SC_CHUNK_ROWS, width), rows.dtype)

    @functools.partial(
        pl.kernel, mesh=_sc_mesh(), out_type=jax.ShapeDtypeStruct((n_out, width), rows.dtype),
        scratch_types=[index_scratch, index_scratch, rows_scratch, rows_scratch,
                       pltpu.SemaphoreType.DMA, pltpu.SemaphoreType.DMA],
        name="sc_row_scatter")
    def scatter(rows_hbm, lo_hbm, hi_hbm, out_hbm, lo_v, hi_v, rows_a, rows_b, sem_a, sem_b):
        base = _sc_worker_base(per_worker)
        bufs = ((rows_a, sem_a), (rows_b, sem_b))

        def rows_of(c):
            return pl.ds(pl.multiple_of(base + c * SC_CHUNK_ROWS, SC_CHUNK_ROWS), SC_CHUNK_ROWS)

        def load(c, buf):
            rows_v, sem = buf
            return pltpu.make_async_copy(rows_hbm.at[rows_of(c)], rows_v, sem)

        load(0, bufs[0]).start()

        @pl.loop(0, n_chunks, step=2)
        def _(c):
            for b in range(2):
                rows_v, _ = bufs[b]
                load(c + b, bufs[b]).wait()

                @pl.when(c + b + 1 < n_chunks)
                def _():
                    load(c + b + 1, bufs[1 - b]).start()

                pltpu.sync_copy(lo_hbm.at[rows_of(c + b)], lo_v)
                pltpu.sync_copy(hi_hbm.at[rows_of(c + b)], hi_v)
                pltpu.sync_copy(rows_v, out_hbm.at[lo_v])
                pltpu.sync_copy(rows_v, out_hbm.at[hi_v])

    return scatter(rows, idx_lo, idx_hi)


def _expert_ffn_body(te_ref, nu_ref, nv_ref, h_ref, wg_ref, wu_ref, wd_ref, y_ref, acc_scr):
    i = pl.program_id(0)
    f = pl.program_id(1)

    @pl.when(i < nu_ref[0])
    def _():
        row = lax.broadcasted_iota(jnp.int32, h_ref.shape, 0)
        packed = jnp.where(row < nv_ref[i], h_ref[...], 0)
        y = _swiglu_down(_unpack_bf16_pairs(packed), wg_ref, wu_ref, wd_ref)

        @pl.when(f == 0)
        def _():
            acc_scr[...] = y

        @pl.when(f > 0)
        def _():
            acc_scr[...] += y

        @pl.when(f == pl.num_programs(1) - 1)
        def _():
            y_ref[...] = _pack_bf16_pairs(acc_scr[...])


def _expert_ffn(h_sorted, tile_expert, n_used, tile_valid, w_gate, w_up, w_down, layer):
    rows, half = h_sorted.shape
    d = 2 * half
    ff = w_gate.shape[3]
    n_f = ff // MOE_FF_TILE

    def row_map(i, f, te, nu, nv):
        return (jnp.minimum(i, nu[0] - 1), 0)

    def col_step(i, f, nu):
        return jnp.where(i < nu[0], f, n_f - 1)

    grid_spec = pltpu.PrefetchScalarGridSpec(
        num_scalar_prefetch=3,
        grid=(rows // MOE_ROW_TILE, n_f),
        in_specs=[
            pl.BlockSpec((MOE_ROW_TILE, half), row_map),
            pl.BlockSpec((None, None, d, MOE_FF_TILE),
                         lambda i, f, te, nu, nv: (layer, te[i], 0, col_step(i, f, nu))),
            pl.BlockSpec((None, None, d, MOE_FF_TILE),
                         lambda i, f, te, nu, nv: (layer, te[i], 0, col_step(i, f, nu))),
            pl.BlockSpec((None, None, MOE_FF_TILE, d),
                         lambda i, f, te, nu, nv: (layer, te[i], col_step(i, f, nu), 0)),
        ],
        out_specs=pl.BlockSpec((MOE_ROW_TILE, half), row_map),
        scratch_shapes=[pltpu.VMEM((MOE_ROW_TILE, d), F32)],
    )
    return pl.pallas_call(
        _expert_ffn_body,
        out_shape=jax.ShapeDtypeStruct((rows, half), jnp.int32),
        grid_spec=grid_spec,
        compiler_params=_params("arbitrary", "arbitrary"),
        name="expert_ffn",
    )(tile_expert, n_used, tile_valid, h_sorted, w_gate, w_up, w_down)


def _combine_body(x_ref, y_ref, g_ref, o_ref):
    out = x_ref[...]
    for s in range(2):
        column = jnp.broadcast_to(g_ref[s:s + 1, :], (LANES, g_ref.shape[1])).T
        gate = jnp.concatenate([column] * (out.shape[1] // LANES), axis=1)
        out = out + _unpack_bf16_pairs(y_ref[s]).astype(F32) * gate
    o_ref[...] = out


def _combine_residual(x, y_pairs, gates2):
    t, d = x.shape
    return pl.pallas_call(
        _combine_body,
        out_shape=jax.ShapeDtypeStruct((t, d), F32),
        grid=(t // ROW_TILE,),
        in_specs=[
            pl.BlockSpec((ROW_TILE, d), lambda i: (i, 0)),
            pl.BlockSpec((2, ROW_TILE, d // 2), lambda i: (0, i, 0)),
            pl.BlockSpec((2, ROW_TILE), lambda i: (0, i)),
        ],
        out_specs=pl.BlockSpec((ROW_TILE, d), lambda i: (i, 0)),
        compiler_params=_params("parallel"),
        name="moe_combine",
    )(x, y_pairs, gates2)


def _moe_residual(x, gain, router, w_gate, w_up, w_down, layer):
    t, d = x.shape
    gates, rank, h_packed = _router(x, gain, router)
    rank8 = rank.astype(jnp.int32)
    chosen = rank8 >= 0
    counts = jnp.sum(chosen, axis=1, dtype=jnp.int32)
    padded = (counts + MOE_ROW_TILE - 1) // MOE_ROW_TILE * MOE_ROW_TILE
    ends = jnp.cumsum(padded)
    starts = ends - padded
    pos = starts[:, None] + rank8
    max_rows = 2 * t + N_EXPERTS * MOE_ROW_TILE
    pos_lo = jnp.min(jnp.where(chosen, pos, max_rows), axis=0)
    pos_hi = jnp.max(jnp.where(chosen, pos, -1), axis=0)
    gates2 = jnp.stack([jnp.sum(jnp.where(chosen & (pos == pos_lo[None, :]), gates, 0.0), axis=0),
                        jnp.sum(jnp.where(chosen & (pos == pos_hi[None, :]), gates, 0.0), axis=0)])
    n_tiles = max_rows // MOE_ROW_TILE
    n_used = (ends[-1] // MOE_ROW_TILE).astype(jnp.int32)
    tile_start = jnp.minimum(jnp.arange(n_tiles, dtype=jnp.int32), n_used - 1) * MOE_ROW_TILE
    tile_expert = jnp.sum(tile_start[:, None] >= ends[None, :], axis=1, dtype=jnp.int32)
    tile_valid = jnp.clip((starts + counts)[tile_expert] - tile_start, 0, MOE_ROW_TILE).astype(jnp.int32)

    h_sorted = _sc_row_scatter_pair(h_packed, pos_lo, pos_hi, max_rows)
    y_sorted = _expert_ffn(h_sorted, tile_expert, n_used.reshape(1), tile_valid, w_gate, w_up, w_down, layer)
    y_pairs = _sc_row_gather(y_sorted, jnp.concatenate([pos_lo, pos_hi])).reshape(2, t, d // 2)
    return _combine_residual(x, y_pairs, gates2)


def _split_head_pair(q):
    is_first = lax.broadcasted_iota(jnp.int32, (1, LANES), 1) < HEAD_DIM
    zero = jnp.zeros_like(q)
    return jnp.where(is_first, q, zero), jnp.where(is_first, zero, q)


def _merge_head_pair(first, second):
    is_first = lax.broadcasted_iota(jnp.int32, (1, LANES), 1) < HEAD_DIM
    return jnp.where(is_first, first, second)


def _qk(q, k):
    return lax.dot_general(q, k, (((1,), (1,)), ((), ())), preferred_element_type=F32)


def _transpose_values(v_ref, vt_scr):
    for c in range(v_ref.shape[0] // ATT_BLOCK):
        rows = slice(c * ATT_BLOCK, (c + 1) * ATT_BLOCK)
        vt_scr[:, rows] = v_ref[rows, :].astype(F32).T.astype(vt_scr.dtype)


def _sb_body(q_ref, k_ref, v_ref, u_ref, o_ref, vt_scr, acc_scr, carry_scr, z_scr, keep_scr, sum_scr):
    qi = pl.program_id(2)

    @pl.when(qi == 0)
    def _():
        _transpose_values(v_ref, vt_scr)

    upper = u_ref[...]
    acc_scr[...] = jnp.zeros_like(acc_scr)
    carry_scr[...] = jnp.zeros_like(carry_scr)
    key = lax.broadcasted_iota(jnp.int32, (ATT_BLOCK, ATT_BLOCK), 0)
    query = lax.broadcasted_iota(jnp.int32, (ATT_BLOCK, ATT_BLOCK), 1)
    strict = key < query
    qs = [_split_head_pair(q_ref[j * ATT_BLOCK:(j + 1) * ATT_BLOCK, :]) for j in range(ATT_K_PER_Q)]

    def process(j, kb, keep):
        start = pl.multiple_of(kb * ATT_BLOCK, ATT_BLOCK)
        k = k_ref[pl.ds(start, ATT_BLOCK), :]
        for hh in range(2):
            z = _qk(k, qs[j][hh])
            log_beta = z - _softplus(z)
            log_keep = log_beta - z
            if keep is not None:
                log_keep = jnp.where(keep, log_keep, 0.0)
            remain = jnp.dot(upper, log_keep.astype(BF16), preferred_element_type=F32)
            w = jnp.exp(log_beta + remain + carry_scr[j, hh])
            if keep is not None:
                w = jnp.where(keep, w, 0.0)
            carry_scr[j, hh] += jnp.sum(log_keep, axis=0, keepdims=True)
            vt = vt_scr[pl.ds(hh * HEAD_DIM, HEAD_DIM), pl.ds(start, ATT_BLOCK)]
            acc_scr[j, hh] += jnp.dot(vt, w.astype(BF16), preferred_element_type=F32)

    units = []
    for j in range(ATT_K_PER_Q):
        g = qi * ATT_K_PER_Q + j
        has_previous = None if j > 0 else jnp.broadcast_to(g > 0, strict.shape)
        for kb, keep in ((g, strict), (jnp.maximum(g - 1, 0), has_previous)):
            for hh in range(2):
                units.append((j, hh, pl.multiple_of(kb * ATT_BLOCK, ATT_BLOCK), keep))

    for u, (j, hh, start, keep) in enumerate(units):
        z_scr[u] = _qk(k_ref[pl.ds(start, ATT_BLOCK), :], qs[j][hh])

    for u, (j, hh, start, keep) in enumerate(units):
        z = z_scr[u]
        log_beta = z - _softplus(z)
        log_keep = log_beta - z
        if keep is not None:
            log_keep = jnp.where(keep, log_keep, 0.0)
        z_scr[u] = log_beta
        keep_scr[u] = log_keep.astype(BF16)
        sum_scr[u] = jnp.sum(log_keep, axis=0, keepdims=True)

    for u in range(len(units)):
        z_scr[u] += jnp.dot(upper, keep_scr[u], preferred_element_type=F32)

    for u, (j, hh, start, keep) in enumerate(units):
        diagonal = u % 4 < 2
        log_w = z_scr[u] if diagonal else z_scr[u] + sum_scr[u - 2]
        w = jnp.exp(log_w)
        if keep is not None:
            w = jnp.where(keep, w, 0.0)
        keep_scr[u] = w.astype(BF16)

    for u, (j, hh, start, keep) in enumerate(units):
        vt = vt_scr[pl.ds(hh * HEAD_DIM, HEAD_DIM), pl.ds(start, ATT_BLOCK)]
        acc_scr[j, hh] += jnp.dot(vt, keep_scr[u], preferred_element_type=F32)
        carry_scr[j, hh] += sum_scr[u]

    for j in range(ATT_K_PER_Q):
        def more(kb, j=j):
            return (kb >= 0) & (jnp.max(carry_scr[j]) > SB_UNDERFLOW_LOG)

        def step(kb, j=j):
            process(j, kb, None)
            return kb - 1

        lax.while_loop(more, step, qi * ATT_K_PER_Q + j - 2)

    for j in range(ATT_K_PER_Q):
        out_t = jnp.concatenate([acc_scr[j, 0], acc_scr[j, 1]], axis=0)
        o_ref[j * ATT_BLOCK:(j + 1) * ATT_BLOCK, :] = out_t.T.astype(o_ref.dtype)


def _sb_attention(proj, batch, seq):
    t = proj.shape[0]
    nq = seq // ATT_Q_BLOCK
    idx = jnp.arange(ATT_BLOCK)
    upper = (idx[None, :] > idx[:, None]).astype(BF16)
    return pl.pallas_call(
        _sb_body,
        out_shape=jax.ShapeDtypeStruct((t, D_MODEL), BF16),
        grid=(batch, HEAD_PAIRS, nq),
        in_specs=[
            pl.BlockSpec((ATT_Q_BLOCK, LANES), lambda b, p, i: (b * nq + i, p)),
            pl.BlockSpec((seq, LANES), lambda b, p, i: (b, HEAD_PAIRS + p)),
            pl.BlockSpec((seq, LANES), lambda b, p, i: (b, 2 * HEAD_PAIRS + p)),
            pl.BlockSpec((ATT_BLOCK, ATT_BLOCK), lambda b, p, i: (0, 0)),
        ],
        out_specs=pl.BlockSpec((ATT_Q_BLOCK, LANES), lambda b, p, i: (b * nq + i, p)),
        scratch_shapes=[
            pltpu.VMEM((LANES, seq), BF16),
            pltpu.VMEM((ATT_K_PER_Q, 2, HEAD_DIM, ATT_BLOCK), F32),
            pltpu.VMEM((ATT_K_PER_Q, 2, 1, ATT_BLOCK), F32),
            pltpu.VMEM((4 * ATT_K_PER_Q, ATT_BLOCK, ATT_BLOCK), F32),
            pltpu.VMEM((4 * ATT_K_PER_Q, ATT_BLOCK, ATT_BLOCK), BF16),
            pltpu.VMEM((4 * ATT_K_PER_Q, 1, ATT_BLOCK), F32),
        ],
        compiler_params=_params("parallel", "parallel", "arbitrary"),
        name="sb_attention",
    )(proj, proj, proj, upper)


def _fox_gate_body(x_ref, g_ref, w_ref, b_ref, tri_ref, c_ref):
    h = _rms_normalize(x_ref[...], g_ref[...])
    logits = _dot_split(h, w_ref[...])
    log_f = -_softplus(-(logits + b_ref[...]))
    seq = log_f.shape[0]
    carry = jnp.zeros((1, LANES), F32)
    for blk in range(seq // ATT_BLOCK):
        rows = slice(blk * ATT_BLOCK, (blk + 1) * ATT_BLOCK)
        c = carry
        for part in _split_bf16(log_f[rows], 3):
            c = c + jnp.dot(tri_ref[...], part, preferred_element_type=F32)
        c_ref[rows, :] = c
        carry = c[ATT_BLOCK - 1:ATT_BLOCK, :]


def _fox_cum_log_forget(x, gain, w_gate, b_gate, batch, seq):
    t, d = x.shape
    w = jnp.zeros((d, LANES), F32).at[:, :N_HEADS].set(w_gate)
    b = jnp.zeros((1, LANES), F32).at[0, :N_HEADS].set(b_gate)
    idx = jnp.arange(ATT_BLOCK)
    tri = (idx[:, None] >= idx[None, :]).astype(BF16)
    return pl.pallas_call(
        _fox_gate_body,
        out_shape=jax.ShapeDtypeStruct((t, LANES), F32),
        grid=(batch,),
        in_specs=[
            pl.BlockSpec((seq, d), lambda i: (i, 0)),
            pl.BlockSpec((1, d), lambda i: (0, 0)),
            pl.BlockSpec((d, LANES), lambda i: (0, 0)),
            pl.BlockSpec((1, LANES), lambda i: (0, 0)),
            pl.BlockSpec((ATT_BLOCK, ATT_BLOCK), lambda i: (0, 0)),
        ],
        out_specs=pl.BlockSpec((seq, LANES), lambda i: (i, 0)),
        compiler_params=_params("parallel"),
        name="fox_gate",
    )(x, gain.reshape(1, d), w, b, tri)


def _fox_body(q_ref, k_ref, v_ref, cq_ref, ck_ref, o_ref, vt_scr, ckb_scr, a_scr, p_scr):
    qi = pl.program_id(2)

    @pl.when(qi == 0)
    def _():
        _transpose_values(v_ref, vt_scr)
        for hh in range(2):
            for c in range(ckb_scr.shape[1] // ATT_BLOCK):
                rows = slice(c * ATT_BLOCK, (c + 1) * ATT_BLOCK)
                ckb_scr[hh, rows, :] = jnp.broadcast_to(ck_ref[0, hh, :, rows], (LANES, ATT_BLOCK)).T

    qs = _split_head_pair(q_ref[...])
    key = lax.broadcasted_iota(jnp.int32, (ATT_BLOCK, ATT_Q_BLOCK), 0)
    query = lax.broadcasted_iota(jnp.int32, (ATT_BLOCK, ATT_Q_BLOCK), 1)
    row_shape = (1, ATT_Q_BLOCK)

    def scores(kb, tops, diagonal):
        start = kb * ATT_BLOCK
        k = k_ref[pl.ds(start, ATT_BLOCK), :]
        if diagonal:
            causal = key + (kb * ATT_BLOCK - qi * ATT_Q_BLOCK) <= query
        new_tops = []
        for hh in range(2):
            ck = ckb_scr[hh, pl.ds(start, ATT_BLOCK), :]
            a = _qk(k, qs[hh]) - jnp.concatenate([ck] * (ATT_Q_BLOCK // LANES), axis=1)
            if diagonal:
                a = jnp.where(causal, a, NEG_INF)
            a_scr[hh, kb] = a
            new_tops.append(jnp.maximum(tops[hh], jnp.max(a, axis=0, keepdims=True)))
        return tuple(new_tops)

    def sweep(n_blocks):
        tops = (jnp.full(row_shape, NEG_INF, F32),) * 2
        for kb in range(n_blocks):
            tops = scores(kb, tops, kb >= n_blocks - ATT_K_PER_Q)
        shifts = tuple(cq_ref[0, hh] - (tops[hh] + cq_ref[0, hh]) for hh in range(2))
        sums = [jnp.zeros(row_shape, F32)] * 2
        for kb in range(n_blocks):
            for hh in range(2):
                p = jnp.exp(a_scr[hh, kb] + shifts[hh])
                p_scr[hh, kb] = p.astype(BF16)
                sums[hh] = sums[hh] + jnp.sum(p, axis=0, keepdims=True)
        accs = [jnp.zeros((HEAD_DIM, ATT_Q_BLOCK), F32)] * 2
        for kb in range(n_blocks):
            for hh in range(2):
                vt = vt_scr[hh * HEAD_DIM:(hh + 1) * HEAD_DIM, kb * ATT_BLOCK:(kb + 1) * ATT_BLOCK]
                accs[hh] = accs[hh] + jnp.dot(vt, p_scr[hh, kb], preferred_element_type=F32)
        out_t = jnp.concatenate([accs[0] / sums[0], accs[1] / sums[1]], axis=0)
        o_ref[...] = out_t.T.astype(o_ref.dtype)

    for q in range(k_ref.shape[0] // ATT_Q_BLOCK):
        pl.when(qi == q)(functools.partial(sweep, (q + 1) * ATT_K_PER_Q))


def _fox_attention(proj, cum, batch, seq):
    t = proj.shape[0]
    nq = seq // ATT_Q_BLOCK
    cum_h = cum[:, :N_HEADS].reshape(batch, seq, N_HEADS).transpose(0, 2, 1)
    cum_rows = cum_h.reshape(batch, N_HEADS, 1, seq)
    return pl.pallas_call(
        _fox_body,
        out_shape=jax.ShapeDtypeStruct((t, D_MODEL), BF16),
        grid=(batch, HEAD_PAIRS, nq),
        in_specs=[
            pl.BlockSpec((ATT_Q_BLOCK, LANES), lambda b, p, i: (b * nq + i, p)),
            pl.BlockSpec((seq, LANES), lambda b, p, i: (b, HEAD_PAIRS + p)),
            pl.BlockSpec((seq, LANES), lambda b, p, i: (b, 2 * HEAD_PAIRS + p)),
            pl.BlockSpec((1, 2, 1, ATT_Q_BLOCK), lambda b, p, i: (b, p, 0, i)),
            pl.BlockSpec((1, 2, 1, seq), lambda b, p, i: (b, p, 0, 0)),
        ],
        out_specs=pl.BlockSpec((ATT_Q_BLOCK, LANES), lambda b, p, i: (b * nq + i, p)),
        scratch_shapes=[
            pltpu.VMEM((LANES, seq), BF16),
            pltpu.VMEM((2, seq, LANES), F32),
            pltpu.VMEM((2, seq // ATT_BLOCK, ATT_BLOCK, ATT_Q_BLOCK), F32),
            pltpu.VMEM((2, seq // ATT_BLOCK, ATT_BLOCK, ATT_Q_BLOCK), BF16),
        ],
        compiler_params=_params("parallel", "parallel", "arbitrary"),
        name="fox_attention",
    )(proj, proj, proj, cum_rows, cum_rows)


def _band_body(q_ref, k_ref, v_ref, bias_ref, o_ref, s_ref, vt_scr, logit_scr, p_scr, inv_scr, *, sub_len):
    seq = v_ref.shape[0]
    n_blocks = seq // DIL_BLOCK
    _transpose_values(v_ref, vt_scr)

    def key_rows(n):
        first = (n * DIL_BLOCK) % sub_len == 0
        return first, slice((n if first else n - 1) * DIL_BLOCK, (n + 1) * DIL_BLOCK)

    for n in range(n_blocks):
        first, k_rows = key_rows(n)
        qs = _split_head_pair(q_ref[n * DIL_BLOCK:(n + 1) * DIL_BLOCK, :])
        k = k_ref[k_rows, :]
        for hh in range(2):
            bias = bias_ref[hh, DIL_BLOCK:, :] if first else bias_ref[hh]
            logit_scr[2 * n + hh, :k.shape[0], :] = _qk(k, qs[hh]) + bias

    for n in range(n_blocks):
        first, k_rows = key_rows(n)
        n_keys = k_rows.stop - k_rows.start
        for hh in range(2):
            logits = logit_scr[2 * n + hh, :n_keys, :]
            m = jnp.max(logits, axis=0, keepdims=True)
            p = jnp.exp(logits - m)
            l = jnp.sum(p, axis=0, keepdims=True)
            p_scr[2 * n + hh, :n_keys, :] = p.astype(BF16)
            inv_scr[2 * n + hh] = 1.0 / l
            s_ref[0, 0, hh:hh + 1, n * DIL_BLOCK:(n + 1) * DIL_BLOCK] = m + jnp.log(l)

    for n in range(n_blocks):
        first, k_rows = key_rows(n)
        n_keys = k_rows.stop - k_rows.start
        outs = []
        for hh in range(2):
            vt = vt_scr[hh * HEAD_DIM:(hh + 1) * HEAD_DIM, k_rows]
            o = jnp.dot(vt, p_scr[2 * n + hh, :n_keys, :], preferred_element_type=F32)
            outs.append(o * inv_scr[2 * n + hh])
        o_ref[n * DIL_BLOCK:(n + 1) * DIL_BLOCK, :] = jnp.concatenate(outs, axis=0).T.astype(o_ref.dtype)


def _band_attention(qk, v, bias_t, batch, seq, sub_len):
    t = v.shape[0]
    return pl.pallas_call(
        functools.partial(_band_body, sub_len=sub_len),
        out_shape=(jax.ShapeDtypeStruct((t, D_MODEL), BF16),
                   jax.ShapeDtypeStruct((batch, HEAD_PAIRS, 2, seq), F32)),
        grid=(batch, HEAD_PAIRS),
        in_specs=[
            pl.BlockSpec((None, seq, LANES), lambda b, p: (0, b, p)),
            pl.BlockSpec((None, seq, LANES), lambda b, p: (1, b, p)),
            pl.BlockSpec((seq, LANES), lambda b, p: (b, p)),
            pl.BlockSpec((2, 2 * DIL_BLOCK, DIL_BLOCK), lambda b, p: (p, 0, 0)),
        ],
        out_specs=(pl.BlockSpec((seq, LANES), lambda b, p: (b, p)),
                   pl.BlockSpec((1, 1, 2, seq), lambda b, p: (b, p, 0, 0))),
        scratch_shapes=[pltpu.VMEM((LANES, seq), BF16),
                        pltpu.VMEM((2 * seq // DIL_BLOCK, 2 * DIL_BLOCK, DIL_BLOCK), F32),
                        pltpu.VMEM((2 * seq // DIL_BLOCK, 2 * DIL_BLOCK, DIL_BLOCK), BF16),
                        pltpu.VMEM((2 * seq // DIL_BLOCK, 1, DIL_BLOCK), F32)],
        compiler_params=_params("parallel", "parallel"),
        name="band_attention",
    )(qk, qk, v, bias_t)


def _softmax_merge_body(o1_ref, o2_ref, o3_ref, s_ref, e_ref, o_ref):
    s = s_ref[...]
    groups = [s, pltpu.roll(s, LANES - N_HEADS, axis=1), pltpu.roll(s, LANES - 2 * N_HEADS, axis=1)]
    top = jnp.maximum(jnp.maximum(groups[0], groups[1]), groups[2])
    weights = [jnp.exp(g - top) for g in groups]
    inv = 1.0 / (weights[0] + weights[1] + weights[2])
    out = jnp.zeros(o_ref.shape, F32)
    for w, part in zip(weights, (o1_ref, o2_ref, o3_ref)):
        spread = jnp.dot((w * inv).astype(BF16), e_ref[...], preferred_element_type=F32)
        out = out + spread * part[...].astype(F32)
    o_ref[...] = out.astype(o_ref.dtype)


def _softmax_merge(outs, s_all):
    t, d = outs[0].shape
    head_of = jnp.arange(d) // HEAD_DIM
    expand = (jnp.arange(LANES)[:, None] == head_of[None, :]).astype(BF16)
    rows = pl.BlockSpec((ROW_TILE, d), lambda i: (i, 0))
    return pl.pallas_call(
        _softmax_merge_body,
        out_shape=jax.ShapeDtypeStruct((t, d), BF16),
        grid=(t // ROW_TILE,),
        in_specs=[rows, rows, rows,
                  pl.BlockSpec((ROW_TILE, LANES), lambda i: (i, 0)),
                  pl.BlockSpec((LANES, d), lambda i: (0, 0))],
        out_specs=rows,
        compiler_params=_params("parallel"),
        name="softmax_merge",
    )(*outs, s_all, expand)


def _t5_causal_bucket(distance):
    max_exact = N_REL_BUCKETS // 2
    d = jnp.maximum(distance, 1).astype(F32)
    log_b = max_exact + (jnp.log(d / max_exact) / math.log(REL_MAX_DISTANCE / max_exact)
                         * (N_REL_BUCKETS - max_exact)).astype(jnp.int32)
    log_b = jnp.minimum(log_b, N_REL_BUCKETS - 1)
    return jnp.where(distance < max_exact, distance, log_b)


def _dilated_bias(rel_bias):
    kj = jnp.arange(2 * DIL_BLOCK, dtype=jnp.int32)
    qi = jnp.arange(DIL_BLOCK, dtype=jnp.int32)
    delta = qi[None, :] + DIL_BLOCK - kj[:, None]
    in_band = (delta >= 0) & (delta <= DIL_SPAN)
    buckets = jnp.stack([_t5_causal_bucket(jnp.maximum(delta, 0) * dil) for _, dil in DILATED_PAIRS])
    one_hot = (buckets[..., None] == jnp.arange(N_REL_BUCKETS)).astype(F32)
    bias = jnp.einsum("gkqb,bh->ghkq", one_hot, rel_bias.astype(F32), precision=lax.Precision.HIGHEST)
    return jnp.where(in_band[None, None], bias, NEG_INF)


def _dilated_proj_body(x_ref, g_ref, w_ref, cs_ref, bd_ref, *rest):
    n_groups = len(DILATED_PAIRS)
    qk_refs, v_refs = rest[:n_groups], rest[n_groups:2 * n_groups]
    h_scr, res_scr = rest[2 * n_groups:]
    j = pl.program_id(1)

    @pl.when(j == 0)
    def _():
        h_scr[...] = _rms_normalize(x_ref[...], g_ref[...]).astype(BF16)

    acc = jnp.dot(h_scr[...], w_ref[...], preferred_element_type=F32)

    def keep(res, first_lane):
        for c in range(res.shape[1] // LANES):
            res_scr[first_lane // LANES + c] = res[:, c * LANES:(c + 1) * LANES]

    @pl.when(j < 2 * n_groups)
    def _():
        for c in range(COL_TILE // NORM_CHUNK):
            sl = slice(c * NORM_CHUNK, (c + 1) * NORM_CHUNK)
            a = acc[:, sl]
            ss = jnp.dot((a * a).astype(BF16), bd_ref[...], preferred_element_type=F32)
            keep(a * lax.rsqrt(ss * (1.0 / HEAD_DIM) + RMS_EPS) * cs_ref[:, sl], c * NORM_CHUNK)

    @pl.when(j == 2 * n_groups)
    def _():
        keep(acc * cs_ref[...], 0)

    def write_classes(ref, dil):
        rows = res_scr.shape[1] // dil
        for c in range(res_scr.shape[0]):
            lanes = slice(c * LANES, (c + 1) * LANES)
            if dil == 1:
                ref[:, lanes] = res_scr[c].astype(ref.dtype)
            else:
                for r in range(dil):
                    ref[r, :, lanes] = res_scr[c, pl.ds(r, rows, stride=dil), :].astype(ref.dtype)

    for g, (_, dil) in enumerate(DILATED_PAIRS):
        pl.when(j // 2 == g)(functools.partial(write_classes, qk_refs[g], dil))
        pl.when(j == 2 * n_groups)(functools.partial(write_classes, v_refs[g], dil))


def _dilated_proj(x, gain, w, layer, col_scale, batch, seq):
    t, d = x.shape
    n = w.shape[2]
    n_groups = len(DILATED_PAIRS)
    tiles_per_seq = seq // DIL_ROW_TILE
    head_id = jnp.arange(NORM_CHUNK) // HEAD_DIM
    block_diag = (head_id[:, None] == head_id[None, :]).astype(BF16)
    out_shape, out_specs = [], []
    for kind in ("qk", "v"):
        for g, (_, dil) in enumerate(DILATED_PAIRS):
            rows = DIL_ROW_TILE // dil
            if kind == "qk":
                shape = (2, batch, dil, seq // dil, d)
                block = (None, None, dil, rows, d)
                index = lambda i, j, g=g: (jnp.clip(j - 2 * g, 0, 1), i // tiles_per_seq, 0, i % tiles_per_seq, 0)
            else:
                shape = (batch, dil, seq // dil, d)
                block = (None, dil, rows, d)
                index = lambda i, j: (i // tiles_per_seq, 0, i % tiles_per_seq, 0)
            if dil == 1:
                block = block[:-3] + (None,) + block[-2:]
            out_shape.append(jax.ShapeDtypeStruct(shape, BF16))
            out_specs.append(pl.BlockSpec(block, index))
    outs = pl.pallas_call(
        _dilated_proj_body,
        out_shape=tuple(out_shape),
        grid=(t // DIL_ROW_TILE, n // COL_TILE),
        in_specs=[
            pl.BlockSpec((DIL_ROW_TILE, d), lambda i, j: (i, 0)),
            pl.BlockSpec((1, d), lambda i, j: (0, 0)),
            pl.BlockSpec((None, d, COL_TILE), lambda i, j: (layer, 0, j)),
            pl.BlockSpec((1, COL_TILE), lambda i, j: (0, j)),
            pl.BlockSpec((NORM_CHUNK, NORM_CHUNK), lambda i, j: (0, 0)),
        ],
        out_specs=tuple(out_specs),
        scratch_shapes=[pltpu.VMEM((DIL_ROW_TILE, d), BF16),
                        pltpu.VMEM((COL_TILE // LANES, DIL_ROW_TILE, LANES), F32)],
        compiler_params=_params("parallel", "arbitrary"),
        name="dilated_proj",
    )(x, gain.reshape(1, d), w, col_scale.reshape(1, n).astype(F32), block_diag)
    qk = [o.reshape(2, t, d) for o in outs[:n_groups]]
    v = [o.reshape(t, d) for o in outs[n_groups:]]
    return qk, v


def _dilated_attention(qk, v, rel_bias, batch, seq):
    t = v[0].shape[0]
    n_groups = len(DILATED_PAIRS)
    bias_t = _dilated_bias(rel_bias)
    outs, stats = [], []
    for g, (_, dil) in enumerate(DILATED_PAIRS):
        sub_len = seq // dil
        o, s = _band_attention(qk[g], v[g], bias_t[g], batch, seq, sub_len)
        o = o.reshape(batch, dil, sub_len, D_MODEL).transpose(0, 2, 1, 3).reshape(t, D_MODEL)
        s = s.reshape(batch, HEAD_PAIRS, 2, dil, sub_len).swapaxes(3, 4)
        outs.append(o)
        stats.append(s.reshape(batch, N_HEADS, seq).transpose(0, 2, 1).reshape(t, N_HEADS))
    stats.append(jnp.zeros((t, LANES - n_groups * N_HEADS), F32))
    return _softmax_merge(outs, jnp.concatenate(stats, axis=1))


def _tile_heads(v):
    return jnp.tile(v.astype(F32), N_HEADS)


def _sb_mixer(x, gain, w_qkv, layer, batch, seq):
    ones = jnp.ones((D_MODEL,), F32)
    col_scale = jnp.concatenate([ones * QK_SCALE, ones, ones])
    proj = _norm_proj(x, gain, w_qkv, layer, col_scale, 0)
    return _sb_attention(proj, batch, seq)


def _dilated_mixer(x, gain, w_in, q_norm, k_norm, rel_bias, layer, batch, seq):
    scales = []
    for g in range(len(DILATED_PAIRS)):
        scales += [_tile_heads(q_norm[g]) * QK_SCALE, _tile_heads(k_norm[g])]
    scales.append(jnp.ones((D_MODEL,), F32))
    qk, v = _dilated_proj(x, gain, w_in, layer, jnp.concatenate(scales), batch, seq)
    return _dilated_attention(qk, v, rel_bias, batch, seq)


def _fox_mixer(x, gain, w_qkv, w_gate, b_f, q_norm, k_norm, layer, batch, seq):
    col_scale = jnp.concatenate([_tile_heads(q_norm) * QK_SCALE, _tile_heads(k_norm),
                                 jnp.ones((D_MODEL,), F32)])
    proj = _norm_proj(x, gain, w_qkv, layer, col_scale, 2)
    cum = _fox_cum_log_forget(x, gain, w_gate, b_f, batch, seq)
    return _fox_attention(proj, cum, batch, seq)


def kernel(x, sb_w_qkv, sb_w_o, dil_w_in, dil_q_norm, dil_k_norm, dil_w_o, fox_w_in, fox_b_f,
           fox_q_norm, fox_k_norm, fox_w_o, rel_bias, attn_norm, ffn_norm, mlp_w_gate, mlp_w_up,
           mlp_w_down, moe_router, moe_w_gate, moe_w_up, moe_w_down):
    batch, seq, d = x.shape
    depth = attn_norm.shape[0]
    sb_w_qkv, sb_w_o, dil_w_in, dil_w_o, fox_w_o, mlp_w_gate, mlp_w_up, mlp_w_down = (
        w.astype(BF16) for w in (sb_w_qkv, sb_w_o, dil_w_in, dil_w_o, fox_w_o, mlp_w_gate, mlp_w_up,
                                 mlp_w_down))
    moe_w_gate, moe_w_up, moe_w_down = (w.astype(BF16) for w in (moe_w_gate, moe_w_up, moe_w_down))
    fox_w_qkv = fox_w_in[:, :, :3 * D_MODEL].astype(BF16)
    h = x.reshape(batch * seq, d)
    for i in range(depth):
        kind, j = i % 3, i // 3
        if kind == 0:
            mixed, w_o = _sb_mixer(h, attn_norm[i], sb_w_qkv, j, batch, seq), sb_w_o
        elif kind == 1:
            mixed, w_o = _dilated_mixer(h, attn_norm[i], dil_w_in, dil_q_norm[j], dil_k_norm[j], rel_bias,
                                        j, batch, seq), dil_w_o
        else:
            mixed, w_o = _fox_mixer(h, attn_norm[i], fox_w_qkv, fox_w_in[j, :, 3 * D_MODEL:], fox_b_f[j],
                                    fox_q_norm[j], fox_k_norm[j], j, batch, seq), fox_w_o
        f = i // 2
        if i % 2 == 0:
            h = _mix_ffn_residual(h, mixed, w_o, j, ffn_norm[i], mlp_w_gate, mlp_w_up, mlp_w_down, f)
        else:
            h = _out_proj_residual(h, mixed, w_o, j)
            h = _moe_residual(h, ffn_norm[i], moe_router[f], moe_w_gate, moe_w_up, moe_w_down, f)
    return h.reshape(batch, seq, d)
```

```python
import functools
import math

import jax
import jax.numpy as jnp
from jax import lax
from jax.experimental import pallas as pl
from jax.experimental.pallas import tpu as pltpu
from jax.experimental.pallas import tpu_sc as plsc

D_MODEL = 1024
N_HEADS = 16
HEAD_DIM = 64
LANES = 128
HEAD_PAIRS = D_MODEL // LANES
D_FF = 3584
N_EXPERTS = 8
N_REL_BUCKETS = 32
REL_MAX_DISTANCE = 2048
DILATED_PAIRS = ((128, 1), (512, 4), (2048, 16))
DIL_SPAN = 128
RMS_EPS = 1e-6
NEG_INF = -1e30
SB_UNDERFLOW_LOG = -104.0
QK_SCALE = 1.0 / math.sqrt(HEAD_DIM)

ROW_TILE = 1024
COL_TILE = 1024
FF_TILE = 1792
SWIGLU_CHUNK = 256
MOE_ROW_TILE = 512
MOE_FF_TILE = 1792
SC_CORES = 2
SC_SUBCORES = 16
SC_CHUNK_ROWS = 64
NORM_CHUNK = 256
ATT_BLOCK = 256
ATT_Q_BLOCK = 512
ATT_K_PER_Q = ATT_Q_BLOCK // ATT_BLOCK
DIL_BLOCK = 128
DIL_ROW_TILE = 512
VMEM_LIMIT = 56 * 1024 * 1024

F32 = jnp.float32
BF16 = jnp.bfloat16


def _params(*semantics):
    return pltpu.CompilerParams(dimension_semantics=semantics, vmem_limit_bytes=VMEM_LIMIT)


def _rms_normalize(x, gain):
    inv = lax.rsqrt(jnp.mean(x * x, axis=-1, keepdims=True) + RMS_EPS)
    return x * inv * gain


def _split_bf16(x, terms):
    parts = []
    for _ in range(terms):
        part = x.astype(BF16)
        parts.append(part)
        x = x - part.astype(F32)
    return parts


def _dot_split(a, b):
    a_hi, a_lo = _split_bf16(a, 2)
    b_hi, b_lo = _split_bf16(b, 2)
    return (jnp.dot(a_hi, b_hi, preferred_element_type=F32)
            + (jnp.dot(a_hi, b_lo, preferred_element_type=F32)
               + jnp.dot(a_lo, b_hi, preferred_element_type=F32)))


def _softplus(z):
    return jnp.maximum(z, 0.0) + jnp.log(1.0 + jnp.exp(-jnp.abs(z)))


def _norm_proj_body(x_ref, g_ref, w_ref, cs_ref, bd_ref, o_ref, h_scr, *, n_norm):
    j = pl.program_id(1)

    @pl.when(j == 0)
    def _():
        h_scr[...] = _rms_normalize(x_ref[...], g_ref[...]).astype(BF16)

    acc = jnp.dot(h_scr[...], w_ref[...], preferred_element_type=F32)

    def plain():
        o_ref[...] = (acc * cs_ref[...]).astype(o_ref.dtype)

    def head_normed():
        for c in range(COL_TILE // NORM_CHUNK):
            sl = slice(c * NORM_CHUNK, (c + 1) * NORM_CHUNK)
            a = acc[:, sl]
            ss = jnp.dot((a * a).astype(BF16), bd_ref[...], preferred_element_type=F32)
            inv = lax.rsqrt(ss * (1.0 / HEAD_DIM) + RMS_EPS)
            o_ref[:, sl] = (a * inv * cs_ref[:, sl]).astype(o_ref.dtype)

    if n_norm == 0:
        plain()
    else:
        pl.when(j < n_norm)(head_normed)
        pl.when(j >= n_norm)(plain)


def _norm_proj(x, gain, w, layer, col_scale, n_norm):
    t, d = x.shape
    n = w.shape[2]
    head_id = jnp.arange(NORM_CHUNK) // HEAD_DIM
    block_diag = (head_id[:, None] == head_id[None, :]).astype(BF16)
    return pl.pallas_call(
        functools.partial(_norm_proj_body, n_norm=n_norm),
        out_shape=jax.ShapeDtypeStruct((t, n), BF16),
        grid=(t // ROW_TILE, n // COL_TILE),
        in_specs=[
            pl.BlockSpec((ROW_TILE, d), lambda i, j: (i, 0)),
            pl.BlockSpec((1, d), lambda i, j: (0, 0)),
            pl.BlockSpec((None, d, COL_TILE), lambda i, j: (layer, 0, j)),
            pl.BlockSpec((1, COL_TILE), lambda i, j: (0, j)),
            pl.BlockSpec((NORM_CHUNK, NORM_CHUNK), lambda i, j: (0, 0)),
        ],
        out_specs=pl.BlockSpec((ROW_TILE, COL_TILE), lambda i, j: (i, j)),
        scratch_shapes=[pltpu.VMEM((ROW_TILE, d), BF16)],
        compiler_params=_params("parallel", "arbitrary"),
        name="norm_proj",
    )(x, gain.reshape(1, d), w, col_scale.reshape(1, n).astype(F32), block_diag)


def _swiglu_hidden(h, wg, wu):
    g = jnp.dot(h, wg, preferred_element_type=F32)
    u = jnp.dot(h, wu, preferred_element_type=F32)
    return g * (1.0 / (1.0 + jnp.exp(-g))) * u


def _swiglu_down(h, wg_ref, wu_ref, wd_ref):
    y = None
    for c in range(wg_ref.shape[1] // SWIGLU_CHUNK):
        cols = slice(c * SWIGLU_CHUNK, (c + 1) * SWIGLU_CHUNK)
        a = _swiglu_hidden(h, wg_ref[:, cols], wu_ref[:, cols]).astype(BF16)
        part = jnp.dot(a, wd_ref[cols, :], preferred_element_type=F32)
        y = part if y is None else y + part
    return y


def _ffn_body(x_ref, o_ref, wo_ref, g_ref, wg_ref, wu_ref, wd_ref, y_ref, h_scr):
    f = pl.program_id(1)

    @pl.when(f == 0)
    def _():
        x = x_ref[...] + jnp.dot(o_ref[...], wo_ref[...], preferred_element_type=F32)
        h_scr[...] = _rms_normalize(x, g_ref[...]).astype(BF16)
        y_ref[...] = x

    y_ref[...] += _swiglu_down(h_scr[...], wg_ref, wu_ref, wd_ref)


def _mix_ffn_residual(x, o, w_o, mix_layer, gain, w_gate, w_up, w_down, layer):
    t, d = x.shape
    ff = w_gate.shape[2]
    return pl.pallas_call(
        _ffn_body,
        out_shape=jax.ShapeDtypeStruct((t, d), F32),
        grid=(t // ROW_TILE, ff // FF_TILE),
        in_specs=[
            pl.BlockSpec((ROW_TILE, d), lambda i, f: (i, 0)),
            pl.BlockSpec((ROW_TILE, d), lambda i, f: (i, 0)),
            pl.BlockSpec((None, d, d), lambda i, f: (mix_layer, 0, 0)),
            pl.BlockSpec((1, d), lambda i, f: (0, 0)),
            pl.BlockSpec((None, d, FF_TILE), lambda i, f: (layer, 0, f)),
            pl.BlockSpec((None, d, FF_TILE), lambda i, f: (layer, 0, f)),
            pl.BlockSpec((None, FF_TILE, d), lambda i, f: (layer, f, 0)),
        ],
        out_specs=pl.BlockSpec((ROW_TILE, d), lambda i, f: (i, 0)),
        scratch_shapes=[pltpu.VMEM((ROW_TILE, d), BF16)],
        compiler_params=_params("parallel", "arbitrary"),
        name="ffn",
    )(x, o, w_o, gain.reshape(1, d), w_gate, w_up, w_down)


def _pack_bf16_pairs(x):
    half = x.shape[1] // 2
    bits = pltpu.bitcast(x.astype(BF16).astype(F32), jnp.int32)
    return bits[:, :half] | lax.shift_right_logical(bits[:, half:], jnp.int32(16))


def _unpack_bf16_pairs(p):
    left = pltpu.bitcast(p & jnp.int32(-65536), F32)
    right = pltpu.bitcast(lax.shift_left(p, jnp.int32(16)), F32)
    return jnp.concatenate([left, right], axis=1).astype(BF16)


def _router_body(x_ref, o_ref, wo_ref, g_ref, r_ref, tri_ref, gates_ref, rank_ref, h_ref, x1_ref, count_scr):
    @pl.when(pl.program_id(0) == 0)
    def _():
        count_scr[...] = jnp.zeros_like(count_scr)

    x1 = x_ref[...] + jnp.dot(o_ref[...], wo_ref[...], preferred_element_type=F32)
    x1_ref[...] = x1
    h = _rms_normalize(x1, g_ref[...])
    h_ref[...] = _pack_bf16_pairs(h)
    logits = _dot_split(h, r_ref[...])
    lane = lax.broadcasted_iota(jnp.int32, logits.shape, 1).astype(F32)
    logits = jnp.where(lane < N_EXPERTS, logits, -jnp.inf)
    m1 = jnp.max(logits, axis=-1, keepdims=True)
    i1 = jnp.min(jnp.where(logits == m1, lane, float(LANES)), axis=-1, keepdims=True)
    rest = jnp.where(lane == i1, -jnp.inf, logits)
    m2 = jnp.max(rest, axis=-1, keepdims=True)
    i2 = jnp.min(jnp.where(rest == m2, lane, float(LANES)), axis=-1, keepdims=True)
    e = jnp.exp(m2 - m1)
    g1 = 1.0 / (1.0 + e)
    gates = jnp.where(lane == i1, g1, 0.0) + jnp.where(lane == i2, e * g1, 0.0)
    gates_ref[...] = gates.T[:N_EXPERTS, :]
    chosen = jnp.where((lane == i1) | (lane == i2), 1.0, 0.0)
    inclusive = jnp.dot(tri_ref[...], chosen.astype(BF16), preferred_element_type=F32)
    rank = jnp.where(chosen > 0.0, inclusive - 1.0 + count_scr[...], -1.0)
    rank_ref[...] = rank.T[:N_EXPERTS, :]
    count_scr[...] += inclusive[ROW_TILE - 1:ROW_TILE, :]


def _mix_router(x, o, w_o, mix_layer, gain, router):
    t, d = x.shape
    r = jnp.zeros((d, LANES), F32).at[:, :N_EXPERTS].set(router)
    idx = jnp.arange(ROW_TILE)
    tri = (idx[:, None] >= idx[None, :]).astype(BF16)
    rows = pl.BlockSpec((ROW_TILE, d), lambda i: (i, 0))
    return pl.pallas_call(
        _router_body,
        out_shape=(jax.ShapeDtypeStruct((N_EXPERTS, t), F32), jax.ShapeDtypeStruct((N_EXPERTS, t), F32),
                   jax.ShapeDtypeStruct((t, d // 2), jnp.int32), jax.ShapeDtypeStruct((t, d), F32)),
        grid=(t // ROW_TILE,),
        in_specs=[
            rows,
            rows,
            pl.BlockSpec((None, d, d), lambda i: (mix_layer, 0, 0)),
            pl.BlockSpec((1, d), lambda i: (0, 0)),
            pl.BlockSpec((d, LANES), lambda i: (0, 0)),
            pl.BlockSpec((ROW_TILE, ROW_TILE), lambda i: (0, 0)),
        ],
        out_specs=(pl.BlockSpec((N_EXPERTS, ROW_TILE), lambda i: (0, i)),
                   pl.BlockSpec((N_EXPERTS, ROW_TILE), lambda i: (0, i)),
                   pl.BlockSpec((ROW_TILE, d // 2), lambda i: (i, 0)),
                   rows),
        scratch_shapes=[pltpu.VMEM((1, LANES), F32)],
        compiler_params=_params("arbitrary"),
        name="router",
    )(x, o, w_o, gain.reshape(1, d), r, tri)


def _sc_mesh():
    return plsc.VectorSubcoreMesh(core_axis_name="core", subcore_axis_name="subcore",
                                  num_cores=SC_CORES, num_subcores=SC_SUBCORES)


def _sc_worker_base(per_worker):
    return (lax.axis_index("subcore") * SC_CORES + lax.axis_index("core")) * per_worker


def _sc_row_gather(table, idx):
    width = table.shape[1]
    n = idx.shape[0]
    per_worker = n // (SC_CORES * SC_SUBCORES)
    n_chunks = per_worker // SC_CHUNK_ROWS
    assert n == n_chunks * SC_CHUNK_ROWS * SC_CORES * SC_SUBCORES and n_chunks % 2 == 0

    @functools.partial(
        pl.kernel, mesh=_sc_mesh(), out_type=jax.ShapeDtypeStruct((n, width), table.dtype),
        scratch_types=[pltpu.VMEM((SC_CHUNK_ROWS,), jnp.int32), pltpu.VMEM((SC_CHUNK_ROWS,), jnp.int32),
                       pltpu.VMEM((SC_CHUNK_ROWS, width), table.dtype),
                       pltpu.VMEM((SC_CHUNK_ROWS, width), table.dtype),
                       pltpu.SemaphoreType.DMA, pltpu.SemaphoreType.DMA],
        name="sc_row_gather")
    def gather(table_hbm, idx_hbm, out_hbm, idx_a, idx_b, rows_a, rows_b, sem_a, sem_b):
        base = _sc_worker_base(per_worker)
        bufs = ((idx_a, rows_a, sem_a), (idx_b, rows_b, sem_b))

        def rows_of(c):
            return pl.ds(pl.multiple_of(base + c * SC_CHUNK_ROWS, SC_CHUNK_ROWS), SC_CHUNK_ROWS)

        def fetch(c, buf):
            idx_v, rows_v, sem = buf
            pltpu.sync_copy(idx_hbm.at[rows_of(c)], idx_v)
            return pltpu.make_async_copy(table_hbm.at[idx_v], rows_v, sem)

        fetch(0, bufs[0]).start()

        @pl.loop(0, n_chunks, step=2)
        def _(c):
            for b in range(2):
                idx_v, rows_v, sem = bufs[b]
                pltpu.make_async_copy(table_hbm.at[idx_v], rows_v, sem).wait()

                @pl.when(c + b + 1 < n_chunks)
                def _():
                    fetch(c + b + 1, bufs[1 - b]).start()

                pltpu.sync_copy(rows_v, out_hbm.at[rows_of(c + b)])

    return gather(table, idx)


def _sc_row_scatter_pair(rows, idx_lo, idx_hi, n_out):
    n, width = rows.shape
    per_worker = n // (SC_CORES * SC_SUBCORES)
    n_chunks = per_worker // SC_CHUNK_ROWS
    assert n == n_chunks * SC_CHUNK_ROWS * SC_CORES * SC_SUBCORES and n_chunks % 2 == 0
    index_scratch = pltpu.VMEM((SC_CHUNK_ROWS,), jnp.int32)
    rows_scratch = pltpu.VMEM((SC_CHUNK_ROWS, width), rows.dtype)

    @functools.partial(
        pl.kernel, mesh=_sc_mesh(), out_type=jax.ShapeDtypeStruct((n_out, width), rows.dtype),
        scratch_types=[index_scratch, index_scratch, rows_scratch, rows_scratch,
                       pltpu.SemaphoreType.DMA, pltpu.SemaphoreType.DMA],
        name="sc_row_scatter")
    def scatter(rows_hbm, lo_hbm, hi_hbm, out_hbm, lo_v, hi_v, rows_a, rows_b, sem_a, sem_b):
        base = _sc_worker_base(per_worker)
        bufs = ((rows_a, sem_a), (rows_b, sem_b))

        def rows_of(c):
            return pl.ds(pl.multiple_of(base + c * SC_CHUNK_ROWS, SC_CHUNK_ROWS), SC_CHUNK_ROWS)

        def load(c, buf):
            rows_v, sem = buf
            return pltpu.make_async_copy(rows_hbm.at[rows_of(c)], rows_v, sem)

        load(0, bufs[0]).start()

        @pl.loop(0, n_chunks, step=2)
        def _(c):
            for b in range(2):
                rows_v, _ = bufs[b]
                load(c + b, bufs[b]).wait()

                @pl.when(c + b + 1 < n_chunks)
                def _():
                    load(c + b + 1, bufs[1 - b]).start()

                pltpu.sync_copy(lo_hbm.at[rows_of(c + b)], lo_v)
                pltpu.sync_copy(hi_hbm.at[rows_of(c + b)], hi_v)
                pltpu.sync_copy(rows_v, out_hbm.at[lo_v])
                pltpu.sync_copy(rows_v, out_hbm.at[hi_v])

    return scatter(rows, idx_lo, idx_hi)


def _expert_ffn_body(te_ref, nu_ref, nv_ref, h_ref, wg_ref, wu_ref, wd_ref, y_ref, acc_scr):
    i = pl.program_id(0)
    f = pl.program_id(1)

    @pl.when(i < nu_ref[0])
    def _():
        row = lax.broadcasted_iota(jnp.int32, h_ref.shape, 0)
        packed = jnp.where(row < nv_ref[i], h_ref[...], 0)
        y = _swiglu_down(_unpack_bf16_pairs(packed), wg_ref, wu_ref, wd_ref)

        @pl.when(f == 0)
        def _():
            acc_scr[...] = y

        @pl.when(f > 0)
        def _():
            acc_scr[...] += y

        @pl.when(f == pl.num_programs(1) - 1)
        def _():
            y_ref[...] = _pack_bf16_pairs(acc_scr[...])


def _expert_ffn(h_sorted, tile_expert, n_used, tile_valid, w_gate, w_up, w_down, layer):
    rows, half = h_sorted.shape
    d = 2 * half
    ff = w_gate.shape[3]
    n_f = ff // MOE_FF_TILE

    def row_map(i, f, te, nu, nv):
        return (jnp.minimum(i, nu[0] - 1), 0)

    def col_step(i, f, nu):
        return jnp.where(i < nu[0], f, n_f - 1)

    grid_spec = pltpu.PrefetchScalarGridSpec(
        num_scalar_prefetch=3,
        grid=(rows // MOE_ROW_TILE, n_f),
        in_specs=[
            pl.BlockSpec((MOE_ROW_TILE, half), row_map),
            pl.BlockSpec((None, None, d, MOE_FF_TILE),
                         lambda i, f, te, nu, nv: (layer, te[i], 0, col_step(i, f, nu))),
            pl.BlockSpec((None, None, d, MOE_FF_TILE),
                         lambda i, f, te, nu, nv: (layer, te[i], 0, col_step(i, f, nu))),
            pl.BlockSpec((None, None, MOE_FF_TILE, d),
                         lambda i, f, te, nu, nv: (layer, te[i], col_step(i, f, nu), 0)),
        ],
        out_specs=pl.BlockSpec((MOE_ROW_TILE, half), row_map),
        scratch_shapes=[pltpu.VMEM((MOE_ROW_TILE, d), F32)],
    )
    return pl.pallas_call(
        _expert_ffn_body,
        out_shape=jax.ShapeDtypeStruct((rows, half), jnp.int32),
        grid_spec=grid_spec,
        compiler_params=_params("arbitrary", "arbitrary"),
        name="expert_ffn",
    )(tile_expert, n_used, tile_valid, h_sorted, w_gate, w_up, w_down)


def _combine_body(x_ref, y_ref, g_ref, o_ref):
    out = x_ref[...]
    for s in range(2):
        column = jnp.broadcast_to(g_ref[s:s + 1, :], (LANES, g_ref.shape[1])).T
        gate = jnp.concatenate([column] * (out.shape[1] // LANES), axis=1)
        out = out + _unpack_bf16_pairs(y_ref[s]).astype(F32) * gate
    o_ref[...] = out


def _combine_residual(x, y_pairs, gates2):
    t, d = x.shape
    return pl.pallas_call(
        _combine_body,
        out_shape=jax.ShapeDtypeStruct((t, d), F32),
        grid=(t // ROW_TILE,),
        in_specs=[
            pl.BlockSpec((ROW_TILE, d), lambda i: (i, 0)),
            pl.BlockSpec((2, ROW_TILE, d // 2), lambda i: (0, i, 0)),
            pl.BlockSpec((2, ROW_TILE), lambda i: (0, i)),
        ],
        out_specs=pl.BlockSpec((ROW_TILE, d), lambda i: (i, 0)),
        compiler_params=_params("parallel"),
        name="moe_combine",
    )(x, y_pairs, gates2)


def _mix_moe_residual(x, o, w_o, mix_layer, gain, router, w_gate, w_up, w_down, layer):
    t, d = x.shape
    gates, rank, h_packed, x = _mix_router(x, o, w_o, mix_layer, gain, router)
    rank8 = rank.astype(jnp.int32)
    chosen = rank8 >= 0
    counts = jnp.sum(chosen, axis=1, dtype=jnp.int32)
    padded = (counts + MOE_ROW_TILE - 1) // MOE_ROW_TILE * MOE_ROW_TILE
    ends = jnp.cumsum(padded)
    starts = ends - padded
    pos = starts[:, None] + rank8
    max_rows = 2 * t + N_EXPERTS * MOE_ROW_TILE
    pos_lo = jnp.min(jnp.where(chosen, pos, max_rows), axis=0)
    pos_hi = jnp.max(jnp.where(chosen, pos, -1), axis=0)
    gates2 = jnp.stack([jnp.sum(jnp.where(chosen & (pos == pos_lo[None, :]), gates, 0.0), axis=0),
                        jnp.sum(jnp.where(chosen & (pos == pos_hi[None, :]), gates, 0.0), axis=0)])
    n_tiles = max_rows // MOE_ROW_TILE
    n_used = (ends[-1] // MOE_ROW_TILE).astype(jnp.int32)
    tile_start = jnp.minimum(jnp.arange(n_tiles, dtype=jnp.int32), n_used - 1) * MOE_ROW_TILE
    tile_expert = jnp.sum(tile_start[:, None] >= ends[None, :], axis=1, dtype=jnp.int32)
    tile_valid = jnp.clip((starts + counts)[tile_expert] - tile_start, 0, MOE_ROW_TILE).astype(jnp.int32)

    h_sorted = _sc_row_scatter_pair(h_packed, pos_lo, pos_hi, max_rows)
    y_sorted = _expert_ffn(h_sorted, tile_expert, n_used.reshape(1), tile_valid, w_gate, w_up, w_down, layer)
    y_pairs = _sc_row_gather(y_sorted, jnp.concatenate([pos_lo, pos_hi])).reshape(2, t, d // 2)
    return _combine_residual(x, y_pairs, gates2)


def _split_head_pair(q):
    is_first = lax.broadcasted_iota(jnp.int32, (1, LANES), 1) < HEAD_DIM
    zero = jnp.zeros_like(q)
    return jnp.where(is_first, q, zero), jnp.where(is_first, zero, q)


def _merge_head_pair(first, second):
    is_first = lax.broadcasted_iota(jnp.int32, (1, LANES), 1) < HEAD_DIM
    return jnp.where(is_first, first, second)


def _qk(q, k):
    return lax.dot_general(q, k, (((1,), (1,)), ((), ())), preferred_element_type=F32)


def _transpose_values(v_ref, vt_scr):
    for c in range(v_ref.shape[0] // ATT_BLOCK):
        rows = slice(c * ATT_BLOCK, (c + 1) * ATT_BLOCK)
        vt_scr[:, rows] = v_ref[rows, :].astype(F32).T.astype(vt_scr.dtype)


def _sb_body(q_ref, k_ref, v_ref, u_ref, o_ref, vt_scr, acc_scr, carry_scr, z_scr, keep_scr, sum_scr):
    qi = pl.program_id(2)

    @pl.when(qi == 0)
    def _():
        _transpose_values(v_ref, vt_scr)

    upper = u_ref[...]
    acc_scr[...] = jnp.zeros_like(acc_scr)
    carry_scr[...] = jnp.zeros_like(carry_scr)
    key = lax.broadcasted_iota(jnp.int32, (ATT_BLOCK, ATT_BLOCK), 0)
    query = lax.broadcasted_iota(jnp.int32, (ATT_BLOCK, ATT_BLOCK), 1)
    strict = key < query
    qs = [_split_head_pair(q_ref[j * ATT_BLOCK:(j + 1) * ATT_BLOCK, :]) for j in range(ATT_K_PER_Q)]

    def process(j, kb, keep):
        start = pl.multiple_of(kb * ATT_BLOCK, ATT_BLOCK)
        k = k_ref[pl.ds(start, ATT_BLOCK), :]
        for hh in range(2):
            z = _qk(k, qs[j][hh])
            log_beta = z - _softplus(z)
            log_keep = log_beta - z
            if keep is not None:
                log_keep = jnp.where(keep, log_keep, 0.0)
            remain = jnp.dot(upper, log_keep.astype(BF16), preferred_element_type=F32)
            w = jnp.exp(log_beta + remain + carry_scr[j, hh])
            if keep is not None:
                w = jnp.where(keep, w, 0.0)
            carry_scr[j, hh] += jnp.sum(log_keep, axis=0, keepdims=True)
            vt = vt_scr[pl.ds(hh * HEAD_DIM, HEAD_DIM), pl.ds(start, ATT_BLOCK)]
            acc_scr[j, hh] += jnp.dot(vt, w.astype(BF16), preferred_element_type=F32)

    units = []
    for j in range(ATT_K_PER_Q):
        g = qi * ATT_K_PER_Q + j
        has_previous = None if j > 0 else jnp.broadcast_to(g > 0, strict.shape)
        for kb, keep in ((g, strict), (jnp.maximum(g - 1, 0), has_previous)):
            for hh in range(2):
                units.append((j, hh, pl.multiple_of(kb * ATT_BLOCK, ATT_BLOCK), keep))

    for u, (j, hh, start, keep) in enumerate(units):
        z_scr[u] = _qk(k_ref[pl.ds(start, ATT_BLOCK), :], qs[j][hh])

    for u, (j, hh, start, keep) in enumerate(units):
        z = z_scr[u]
        log_beta = z - _softplus(z)
        log_keep = log_beta - z
        if keep is not None:
            log_keep = jnp.where(keep, log_keep, 0.0)
        z_scr[u] = log_beta
        keep_scr[u] = log_keep.astype(BF16)
        sum_scr[u] = jnp.sum(log_keep, axis=0, keepdims=True)

    for u in range(len(units)):
        z_scr[u] += jnp.dot(upper, keep_scr[u], preferred_element_type=F32)

    for u, (j, hh, start, keep) in enumerate(units):
        diagonal = u % 4 < 2
        log_w = z_scr[u] if diagonal else z_scr[u] + sum_scr[u - 2]
        w = jnp.exp(log_w)
        if keep is not None:
            w = jnp.where(keep, w, 0.0)
        keep_scr[u] = w.astype(BF16)

    for u, (j, hh, start, keep) in enumerate(units):
        vt = vt_scr[pl.ds(hh * HEAD_DIM, HEAD_DIM), pl.ds(start, ATT_BLOCK)]
        acc_scr[j, hh] += jnp.dot(vt, keep_scr[u], preferred_element_type=F32)
        carry_scr[j, hh] += sum_scr[u]

    for j in range(ATT_K_PER_Q):
        def more(kb, j=j):
            return (kb >= 0) & (jnp.max(carry_scr[j]) > SB_UNDERFLOW_LOG)

        def step(kb, j=j):
            process(j, kb, None)
            return kb - 1

        lax.while_loop(more, step, qi * ATT_K_PER_Q + j - 2)

    for j in range(ATT_K_PER_Q):
        out_t = jnp.concatenate([acc_scr[j, 0], acc_scr[j, 1]], axis=0)
        o_ref[j * ATT_BLOCK:(j + 1) * ATT_BLOCK, :] = out_t.T.astype(o_ref.dtype)


def _sb_attention(proj, batch, seq):
    t = proj.shape[0]
    nq = seq // ATT_Q_BLOCK
    idx = jnp.arange(ATT_BLOCK)
    upper = (idx[None, :] > idx[:, None]).astype(BF16)
    return pl.pallas_call(
        _sb_body,
        out_shape=jax.ShapeDtypeStruct((t, D_MODEL), BF16),
        grid=(batch, HEAD_PAIRS, nq),
        in_specs=[
            pl.BlockSpec((ATT_Q_BLOCK, LANES), lambda b, p, i: (b * nq + i, p)),
            pl.BlockSpec((seq, LANES), lambda b, p, i: (b, HEAD_PAIRS + p)),
            pl.BlockSpec((seq, LANES), lambda b, p, i: (b, 2 * HEAD_PAIRS + p)),
            pl.BlockSpec((ATT_BLOCK, ATT_BLOCK), lambda b, p, i: (0, 0)),
        ],
        out_specs=pl.BlockSpec((ATT_Q_BLOCK, LANES), lambda b, p, i: (b * nq + i, p)),
        scratch_shapes=[
            pltpu.VMEM((LANES, seq), BF16),
            pltpu.VMEM((ATT_K_PER_Q, 2, HEAD_DIM, ATT_BLOCK), F32),
            pltpu.VMEM((ATT_K_PER_Q, 2, 1, ATT_BLOCK), F32),
            pltpu.VMEM((4 * ATT_K_PER_Q, ATT_BLOCK, ATT_BLOCK), F32),
            pltpu.VMEM((4 * ATT_K_PER_Q, ATT_BLOCK, ATT_BLOCK), BF16),
            pltpu.VMEM((4 * ATT_K_PER_Q, 1, ATT_BLOCK), F32),
        ],
        compiler_params=_params("parallel", "parallel", "arbitrary"),
        name="sb_attention",
    )(proj, proj, proj, upper)


def _fox_gate_body(x_ref, g_ref, w_ref, b_ref, tri_ref, c_ref):
    h = _rms_normalize(x_ref[...], g_ref[...])
    logits = _dot_split(h, w_ref[...])
    log_f = -_softplus(-(logits + b_ref[...]))
    seq = log_f.shape[0]
    carry = jnp.zeros((1, LANES), F32)
    for blk in range(seq // ATT_BLOCK):
        rows = slice(blk * ATT_BLOCK, (blk + 1) * ATT_BLOCK)
        c = carry
        for part in _split_bf16(log_f[rows], 3):
            c = c + jnp.dot(tri_ref[...], part, preferred_element_type=F32)
        c_ref[rows, :] = c
        carry = c[ATT_BLOCK - 1:ATT_BLOCK, :]


def _fox_cum_log_forget(x, gain, w_gate, b_gate, batch, seq):
    t, d = x.shape
    w = jnp.zeros((d, LANES), F32).at[:, :N_HEADS].set(w_gate)
    b = jnp.zeros((1, LANES), F32).at[0, :N_HEADS].set(b_gate)
    idx = jnp.arange(ATT_BLOCK)
    tri = (idx[:, None] >= idx[None, :]).astype(BF16)
    return pl.pallas_call(
        _fox_gate_body,
        out_shape=jax.ShapeDtypeStruct((t, LANES), F32),
        grid=(batch,),
        in_specs=[
            pl.BlockSpec((seq, d), lambda i: (i, 0)),
            pl.BlockSpec((1, d), lambda i: (0, 0)),
            pl.BlockSpec((d, LANES), lambda i: (0, 0)),
            pl.BlockSpec((1, LANES), lambda i: (0, 0)),
            pl.BlockSpec((ATT_BLOCK, ATT_BLOCK), lambda i: (0, 0)),
        ],
        out_specs=pl.BlockSpec((seq, LANES), lambda i: (i, 0)),
        compiler_params=_params("parallel"),
        name="fox_gate",
    )(x, gain.reshape(1, d), w, b, tri)


def _fox_body(q_ref, k_ref, v_ref, cq_ref, ck_ref, o_ref, vt_scr, ckb_scr, a_scr, p_scr):
    qi = pl.program_id(2)

    @pl.when(qi == 0)
    def _():
        _transpose_values(v_ref, vt_scr)
        for hh in range(2):
            for c in range(ckb_scr.shape[1] // ATT_BLOCK):
                rows = slice(c * ATT_BLOCK, (c + 1) * ATT_BLOCK)
                ckb_scr[hh, rows, :] = jnp.broadcast_to(ck_ref[0, hh, :, rows], (LANES, ATT_BLOCK)).T

    qs = _split_head_pair(q_ref[...])
    key = lax.broadcasted_iota(jnp.int32, (ATT_BLOCK, ATT_Q_BLOCK), 0)
    query = lax.broadcasted_iota(jnp.int32, (ATT_BLOCK, ATT_Q_BLOCK), 1)
    row_shape = (1, ATT_Q_BLOCK)

    def scores(kb, tops, diagonal):
        start = kb * ATT_BLOCK
        k = k_ref[pl.ds(start, ATT_BLOCK), :]
        if diagonal:
            causal = key + (kb * ATT_BLOCK - qi * ATT_Q_BLOCK) <= query
        new_tops = []
        for hh in range(2):
            ck = ckb_scr[hh, pl.ds(start, ATT_BLOCK), :]
            a = _qk(k, qs[hh]) - jnp.concatenate([ck] * (ATT_Q_BLOCK // LANES), axis=1)
            if diagonal:
                a = jnp.where(causal, a, NEG_INF)
            a_scr[hh, kb] = a
            new_tops.append(jnp.maximum(tops[hh], jnp.max(a, axis=0, keepdims=True)))
        return tuple(new_tops)

    def sweep(n_blocks):
        tops = (jnp.full(row_shape, NEG_INF, F32),) * 2
        for kb in range(n_blocks):
            tops = scores(kb, tops, kb >= n_blocks - ATT_K_PER_Q)
        shifts = tuple(cq_ref[0, hh] - (tops[hh] + cq_ref[0, hh]) for hh in range(2))
        sums = [jnp.zeros(row_shape, F32)] * 2
        for kb in range(n_blocks):
            for hh in range(2):
                p = jnp.exp(a_scr[hh, kb] + shifts[hh])
                p_scr[hh, kb] = p.astype(BF16)
                sums[hh] = sums[hh] + jnp.sum(p, axis=0, keepdims=True)
        accs = [jnp.zeros((HEAD_DIM, ATT_Q_BLOCK), F32)] * 2
        for kb in range(n_blocks):
            for hh in range(2):
                vt = vt_scr[hh * HEAD_DIM:(hh + 1) * HEAD_DIM, kb * ATT_BLOCK:(kb + 1) * ATT_BLOCK]
                accs[hh] = accs[hh] + jnp.dot(vt, p_scr[hh, kb], preferred_element_type=F32)
        out_t = jnp.concatenate([accs[0] / sums[0], accs[1] / sums[1]], axis=0)
        o_ref[...] = out_t.T.astype(o_ref.dtype)

    for q in range(k_ref.shape[0] // ATT_Q_BLOCK):
        pl.when(qi == q)(functools.partial(sweep, (q + 1) * ATT_K_PER_Q))


def _fox_attention(proj, cum, batch, seq):
    t = proj.shape[0]
    nq = seq // ATT_Q_BLOCK
    cum_h = cum[:, :N_HEADS].reshape(batch, seq, N_HEADS).transpose(0, 2, 1)
    cum_rows = cum_h.reshape(batch, N_HEADS, 1, seq)
    return pl.pallas_call(
        _fox_body,
        out_shape=jax.ShapeDtypeStruct((t, D_MODEL), BF16),
        grid=(batch, HEAD_PAIRS, nq),
        in_specs=[
            pl.BlockSpec((ATT_Q_BLOCK, LANES), lambda b, p, i: (b * nq + i, p)),
            pl.BlockSpec((seq, LANES), lambda b, p, i: (b, HEAD_PAIRS + p)),
            pl.BlockSpec((seq, LANES), lambda b, p, i: (b, 2 * HEAD_PAIRS + p)),
            pl.BlockSpec((1, 2, 1, ATT_Q_BLOCK), lambda b, p, i: (b, p, 0, i)),
            pl.BlockSpec((1, 2, 1, seq), lambda b, p, i: (b, p, 0, 0)),
        ],
        out_specs=pl.BlockSpec((ATT_Q_BLOCK, LANES), lambda b, p, i: (b * nq + i, p)),
        scratch_shapes=[
            pltpu.VMEM((LANES, seq), BF16),
            pltpu.VMEM((2, seq, LANES), F32),
            pltpu.VMEM((2, seq // ATT_BLOCK, ATT_BLOCK, ATT_Q_BLOCK), F32),
            pltpu.VMEM((2, seq // ATT_BLOCK, ATT_BLOCK, ATT_Q_BLOCK), BF16),
        ],
        compiler_params=_params("parallel", "parallel", "arbitrary"),
        name="fox_attention",
    )(proj, proj, proj, cum_rows, cum_rows)


def _band_body(q_ref, k_ref, v_ref, bias_ref, o_ref, s_ref, vt_scr, logit_scr, p_scr, inv_scr, *, sub_len):
    seq = v_ref.shape[0]
    n_blocks = seq // DIL_BLOCK
    _transpose_values(v_ref, vt_scr)

    def key_rows(n):
        first = (n * DIL_BLOCK) % sub_len == 0
        return first, slice((n if first else n - 1) * DIL_BLOCK, (n + 1) * DIL_BLOCK)

    for n in range(n_blocks):
        first, k_rows = key_rows(n)
        qs = _split_head_pair(q_ref[n * DIL_BLOCK:(n + 1) * DIL_BLOCK, :])
        k = k_ref[k_rows, :]
        for hh in range(2):
            bias = bias_ref[hh, DIL_BLOCK:, :] if first else bias_ref[hh]
            logit_scr[2 * n + hh, :k.shape[0], :] = _qk(k, qs[hh]) + bias

    for n in range(n_blocks):
        first, k_rows = key_rows(n)
        n_keys = k_rows.stop - k_rows.start
        for hh in range(2):
            logits = logit_scr[2 * n + hh, :n_keys, :]
            m = jnp.max(logits, axis=0, keepdims=True)
            p = jnp.exp(logits - m)
            l = jnp.sum(p, axis=0, keepdims=True)
            p_scr[2 * n + hh, :n_keys, :] = p.astype(BF16)
            inv_scr[2 * n + hh] = 1.0 / l
            s_ref[0, 0, hh:hh + 1, n * DIL_BLOCK:(n + 1) * DIL_BLOCK] = m + jnp.log(l)

    for n in range(n_blocks):
        first, k_rows = key_rows(n)
        n_keys = k_rows.stop - k_rows.start
        outs = []
        for hh in range(2):
            vt = vt_scr[hh * HEAD_DIM:(hh + 1) * HEAD_DIM, k_rows]
            o = jnp.dot(vt, p_scr[2 * n + hh, :n_keys, :], preferred_element_type=F32)
            outs.append(o * inv_scr[2 * n + hh])
        o_ref[n * DIL_BLOCK:(n + 1) * DIL_BLOCK, :] = jnp.concatenate(outs, axis=0).T.astype(o_ref.dtype)


def _band_attention(qk, v, bias_t, batch, seq, sub_len):
    t = v.shape[0]
    return pl.pallas_call(
        functools.partial(_band_body, sub_len=sub_len),
        out_shape=(jax.ShapeDtypeStruct((t, D_MODEL), BF16),
                   jax.ShapeDtypeStruct((batch, HEAD_PAIRS, 2, seq), F32)),
        grid=(batch, HEAD_PAIRS),
        in_specs=[
            pl.BlockSpec((None, seq, LANES), lambda b, p: (0, b, p)),
            pl.BlockSpec((None, seq, LANES), lambda b, p: (1, b, p)),
            pl.BlockSpec((seq, LANES), lambda b, p: (b, p)),
            pl.BlockSpec((2, 2 * DIL_BLOCK, DIL_BLOCK), lambda b, p: (p, 0, 0)),
        ],
        out_specs=(pl.BlockSpec((seq, LANES), lambda b, p: (b, p)),
                   pl.BlockSpec((1, 1, 2, seq), lambda b, p: (b, p, 0, 0))),
        scratch_shapes=[pltpu.VMEM((LANES, seq), BF16),
                        pltpu.VMEM((2 * seq // DIL_BLOCK, 2 * DIL_BLOCK, DIL_BLOCK), F32),
                        pltpu.VMEM((2 * seq // DIL_BLOCK, 2 * DIL_BLOCK, DIL_BLOCK), BF16),
                        pltpu.VMEM((2 * seq // DIL_BLOCK, 1, DIL_BLOCK), F32)],
        compiler_params=_params("parallel", "parallel"),
        name="band_attention",
    )(qk, qk, v, bias_t)


def _softmax_merge_body(o1_ref, o2_ref, o3_ref, s_ref, e_ref, o_ref):
    s = s_ref[...]
    groups = [s, pltpu.roll(s, LANES - N_HEADS, axis=1), pltpu.roll(s, LANES - 2 * N_HEADS, axis=1)]
    top = jnp.maximum(jnp.maximum(groups[0], groups[1]), groups[2])
    weights = [jnp.exp(g - top) for g in groups]
    inv = 1.0 / (weights[0] + weights[1] + weights[2])
    out = jnp.zeros(o_ref.shape, F32)
    for w, part in zip(weights, (o1_ref, o2_ref, o3_ref)):
        spread = jnp.dot((w * inv).astype(BF16), e_ref[...], preferred_element_type=F32)
        out = out + spread * part[...].astype(F32)
    o_ref[...] = out.astype(o_ref.dtype)


def _softmax_merge(outs, s_all):
    t, d = outs[0].shape
    head_of = jnp.arange(d) // HEAD_DIM
    expand = (jnp.arange(LANES)[:, None] == head_of[None, :]).astype(BF16)
    rows = pl.BlockSpec((ROW_TILE, d), lambda i: (i, 0))
    return pl.pallas_call(
        _softmax_merge_body,
        out_shape=jax.ShapeDtypeStruct((t, d), BF16),
        grid=(t // ROW_TILE,),
        in_specs=[rows, rows, rows,
                  pl.BlockSpec((ROW_TILE, LANES), lambda i: (i, 0)),
                  pl.BlockSpec((LANES, d), lambda i: (0, 0))],
        out_specs=rows,
        compiler_params=_params("parallel"),
        name="softmax_merge",
    )(*outs, s_all, expand)


def _t5_causal_bucket(distance):
    max_exact = N_REL_BUCKETS // 2
    d = jnp.maximum(distance, 1).astype(F32)
    log_b = max_exact + (jnp.log(d / max_exact) / math.log(REL_MAX_DISTANCE / max_exact)
                         * (N_REL_BUCKETS - max_exact)).astype(jnp.int32)
    log_b = jnp.minimum(log_b, N_REL_BUCKETS - 1)
    return jnp.where(distance < max_exact, distance, log_b)


def _dilated_bias(rel_bias):
    kj = jnp.arange(2 * DIL_BLOCK, dtype=jnp.int32)
    qi = jnp.arange(DIL_BLOCK, dtype=jnp.int32)
    delta = qi[None, :] + DIL_BLOCK - kj[:, None]
    in_band = (delta >= 0) & (delta <= DIL_SPAN)
    buckets = jnp.stack([_t5_causal_bucket(jnp.maximum(delta, 0) * dil) for _, dil in DILATED_PAIRS])
    one_hot = (buckets[..., None] == jnp.arange(N_REL_BUCKETS)).astype(F32)
    bias = jnp.einsum("gkqb,bh->ghkq", one_hot, rel_bias.astype(F32), precision=lax.Precision.HIGHEST)
    return jnp.where(in_band[None, None], bias, NEG_INF)


def _dilated_proj_body(x_ref, g_ref, w_ref, cs_ref, bd_ref, *rest):
    n_groups = len(DILATED_PAIRS)
    qk_refs, v_refs = rest[:n_groups], rest[n_groups:2 * n_groups]
    h_scr, res_scr = rest[2 * n_groups:]
    j = pl.program_id(1)

    @pl.when(j == 0)
    def _():
        h_scr[...] = _rms_normalize(x_ref[...], g_ref[...]).astype(BF16)

    acc = jnp.dot(h_scr[...], w_ref[...], preferred_element_type=F32)

    def keep(res, first_lane):
        for c in range(res.shape[1] // LANES):
            res_scr[first_lane // LANES + c] = res[:, c * LANES:(c + 1) * LANES]

    @pl.when(j < 2 * n_groups)
    def _():
        for c in range(COL_TILE // NORM_CHUNK):
            sl = slice(c * NORM_CHUNK, (c + 1) * NORM_CHUNK)
            a = acc[:, sl]
            ss = jnp.dot((a * a).astype(BF16), bd_ref[...], preferred_element_type=F32)
            keep(a * lax.rsqrt(ss * (1.0 / HEAD_DIM) + RMS_EPS) * cs_ref[:, sl], c * NORM_CHUNK)

    @pl.when(j == 2 * n_groups)
    def _():
        keep(acc * cs_ref[...], 0)

    def write_classes(ref, dil):
        rows = res_scr.shape[1] // dil
        for c in range(res_scr.shape[0]):
            lanes = slice(c * LANES, (c + 1) * LANES)
            if dil == 1:
                ref[:, lanes] = res_scr[c].astype(ref.dtype)
            else:
                for r in range(dil):
                    ref[r, :, lanes] = res_scr[c, pl.ds(r, rows, stride=dil), :].astype(ref.dtype)

    for g, (_, dil) in enumerate(DILATED_PAIRS):
        pl.when(j // 2 == g)(functools.partial(write_classes, qk_refs[g], dil))
        pl.when(j == 2 * n_groups)(functools.partial(write_classes, v_refs[g], dil))


def _dilated_proj(x, gain, w, layer, col_scale, batch, seq):
    t, d = x.shape
    n = w.shape[2]
    n_groups = len(DILATED_PAIRS)
    tiles_per_seq = seq // DIL_ROW_TILE
    head_id = jnp.arange(NORM_CHUNK) // HEAD_DIM
    block_diag = (head_id[:, None] == head_id[None, :]).astype(BF16)
    out_shape, out_specs = [], []
    for kind in ("qk", "v"):
        for g, (_, dil) in enumerate(DILATED_PAIRS):
            rows = DIL_ROW_TILE // dil
            if kind == "qk":
                shape = (2, batch, dil, seq // dil, d)
                block = (None, None, dil, rows, d)
                index = lambda i, j, g=g: (jnp.clip(j - 2 * g, 0, 1), i // tiles_per_seq, 0, i % tiles_per_seq, 0)
            else:
                shape = (batch, dil, seq // dil, d)
                block = (None, dil, rows, d)
                index = lambda i, j: (i // tiles_per_seq, 0, i % tiles_per_seq, 0)
            if dil == 1:
                block = block[:-3] + (None,) + block[-2:]
            out_shape.append(jax.ShapeDtypeStruct(shape, BF16))
            out_specs.append(pl.BlockSpec(block, index))
    outs = pl.pallas_call(
        _dilated_proj_body,
        out_shape=tuple(out_shape),
        grid=(t // DIL_ROW_TILE, n // COL_TILE),
        in_specs=[
            pl.BlockSpec((DIL_ROW_TILE, d), lambda i, j: (i, 0)),
            pl.BlockSpec((1, d), lambda i, j: (0, 0)),
            pl.BlockSpec((None, d, COL_TILE), lambda i, j: (layer, 0, j)),
            pl.BlockSpec((1, COL_TILE), lambda i, j: (0, j)),
            pl.BlockSpec((NORM_CHUNK, NORM_CHUNK), lambda i, j: (0, 0)),
        ],
        out_specs=tuple(out_specs),
        scratch_shapes=[pltpu.VMEM((DIL_ROW_TILE, d), BF16),
                        pltpu.VMEM((COL_TILE // LANES, DIL_ROW_TILE, LANES), F32)],
        compiler_params=_params("parallel", "arbitrary"),
        name="dilated_proj",
    )(x, gain.reshape(1, d), w, col_scale.reshape(1, n).astype(F32), block_diag)
    qk = [o.reshape(2, t, d) for o in outs[:n_groups]]
    v = [o.reshape(t, d) for o in outs[n_groups:]]
    return qk, v


def _dilated_attention(qk, v, rel_bias, batch, seq):
    t = v[0].shape[0]
    n_groups = len(DILATED_PAIRS)
    bias_t = _dilated_bias(rel_bias)
    outs, stats = [], []
    for g, (_, dil) in enumerate(DILATED_PAIRS):
        sub_len = seq // dil
        o, s = _band_attention(qk[g], v[g], bias_t[g], batch, seq, sub_len)
        o = o.reshape(batch, dil, sub_len, D_MODEL).transpose(0, 2, 1, 3).reshape(t, D_MODEL)
        s = s.reshape(batch, HEAD_PAIRS, 2, dil, sub_len).swapaxes(3, 4)
        outs.append(o)
        stats.append(s.reshape(batch, N_HEADS, seq).transpose(0, 2, 1).reshape(t, N_HEADS))
    stats.append(jnp.zeros((t, LANES - n_groups * N_HEADS), F32))
    return _softmax_merge(outs, jnp.concatenate(stats, axis=1))


def _tile_heads(v):
    return jnp.tile(v.astype(F32), N_HEADS)


def _sb_mixer(x, gain, w_qkv, layer, batch, seq):
    ones = jnp.ones((D_MODEL,), F32)
    col_scale = jnp.concatenate([ones * QK_SCALE, ones, ones])
    proj = _norm_proj(x, gain, w_qkv, layer, col_scale, 0)
    return _sb_attention(proj, batch, seq)


def _dilated_mixer(x, gain, w_in, q_norm, k_norm, rel_bias, layer, batch, seq):
    scales = []
    for g in range(len(DILATED_PAIRS)):
        scales += [_tile_heads(q_norm[g]) * QK_SCALE, _tile_heads(k_norm[g])]
    scales.append(jnp.ones((D_MODEL,), F32))
    qk, v = _dilated_proj(x, gain, w_in, layer, jnp.concatenate(scales), batch, seq)
    return _dilated_attention(qk, v, rel_bias, batch, seq)


def _fox_mixer(x, gain, w_qkv, w_gate, b_f, q_norm, k_norm, layer, batch, seq):
    col_scale = jnp.concatenate([_tile_heads(q_norm) * QK_SCALE, _tile_heads(k_norm),
                                 jnp.ones((D_MODEL,), F32)])
    proj = _norm_proj(x, gain, w_qkv, layer, col_scale, 2)
    cum = _fox_cum_log_forget(x, gain, w_gate, b_f, batch, seq)
    return _fox_attention(proj, cum, batch, seq)


def kernel(x, sb_w_qkv, sb_w_o, dil_w_in, dil_q_norm, dil_k_norm, dil_w_o, fox_w_in, fox_b_f,
           fox_q_norm, fox_k_norm, fox_w_o, rel_bias, attn_norm, ffn_norm, mlp_w_gate, mlp_w_up,
           mlp_w_down, moe_router, moe_w_gate, moe_w_up, moe_w_down):
    batch, seq, d = x.shape
    depth = attn_norm.shape[0]
    sb_w_qkv, sb_w_o, dil_w_in, dil_w_o, fox_w_o, mlp_w_gate, mlp_w_up, mlp_w_down = (
        w.astype(BF16) for w in (sb_w_qkv, sb_w_o, dil_w_in, dil_w_o, fox_w_o, mlp_w_gate, mlp_w_up,
                                 mlp_w_down))
    moe_w_gate, moe_w_up, moe_w_down = (w.astype(BF16) for w in (moe_w_gate, moe_w_up, moe_w_down))
    fox_w_qkv = fox_w_in[:, :, :3 * D_MODEL].astype(BF16)
    h = x.reshape(batch * seq, d)
    for i in range(depth):
        kind, j = i % 3, i // 3
        if kind == 0:
            mixed, w_o = _sb_mixer(h, attn_norm[i], sb_w_qkv, j, batch, seq), sb_w_o
        elif kind == 1:
            mixed, w_o = _dilated_mixer(h, attn_norm[i], dil_w_in, dil_q_norm[j], dil_k_norm[j], rel_bias,
                                        j, batch, seq), dil_w_o
        else:
            mixed, w_o = _fox_mixer(h, attn_norm[i], fox_w_qkv, fox_w_in[j, :, 3 * D_MODEL:], fox_b_f[j],
                                    fox_q_norm[j], fox_k_norm[j], j, batch, seq), fox_w_o
        f = i // 2
        if i % 2 == 0:
            h = _mix_ffn_residual(h, mixed, w_o, j, ffn_norm[i], mlp_w_gate, mlp_w_up, mlp_w_down, f)
        else:
            h = _mix_moe_residual(h, mixed, w_o, j, ffn_norm[i], moe_router[f], moe_w_gate, moe_w_up,
                                  moe_w_down, f)
    return h.reshape(batch, seq, d)
```

```python
import functools
import math

import jax
import jax.numpy as jnp
from jax import lax
from jax.experimental import pallas as pl
from jax.experimental.pallas import tpu as pltpu
from jax.experimental.pallas import tpu_sc as plsc

D_MODEL = 1024
N_HEADS = 16
HEAD_DIM = 64
LANES = 128
HEAD_PAIRS = D_MODEL // LANES
D_FF = 3584
N_EXPERTS = 8
N_REL_BUCKETS = 32
REL_MAX_DISTANCE = 2048
DILATED_PAIRS = ((128, 1), (512, 4), (2048, 16))
DIL_SPAN = 128
RMS_EPS = 1e-6
NEG_INF = -1e30
SB_UNDERFLOW_LOG = -104.0
QK_SCALE = 1.0 / math.sqrt(HEAD_DIM)

ROW_TILE = 1024
COL_TILE = 1024
FF_TILE = 1792
SWIGLU_CHUNK = 256
MOE_ROW_TILE = 512
MOE_FF_TILE = 1792
SC_CORES = 2
SC_SUBCORES = 16
SC_CHUNK_ROWS = 64
NORM_CHUNK = 256
ATT_BLOCK = 256
ATT_Q_BLOCK = 512
ATT_K_PER_Q = ATT_Q_BLOCK // ATT_BLOCK
DIL_BLOCK = 128
DIL_ROW_TILE = 1024
VMEM_LIMIT = 56 * 1024 * 1024

F32 = jnp.float32
BF16 = jnp.bfloat16


def _params(*semantics):
    return pltpu.CompilerParams(dimension_semantics=semantics, vmem_limit_bytes=VMEM_LIMIT)


def _rms_normalize(x, gain):
    inv = lax.rsqrt(jnp.mean(x * x, axis=-1, keepdims=True) + RMS_EPS)
    return x * inv * gain


def _split_bf16(x, terms):
    parts = []
    for _ in range(terms):
        part = x.astype(BF16)
        parts.append(part)
        x = x - part.astype(F32)
    return parts


def _dot_split(a, b):
    a_hi, a_lo = _split_bf16(a, 2)
    b_hi, b_lo = _split_bf16(b, 2)
    return (jnp.dot(a_hi, b_hi, preferred_element_type=F32)
            + (jnp.dot(a_hi, b_lo, preferred_element_type=F32)
               + jnp.dot(a_lo, b_hi, preferred_element_type=F32)))


def _softplus(z):
    return jnp.maximum(z, 0.0) + jnp.log(1.0 + jnp.exp(-jnp.abs(z)))


def _norm_proj_body(x_ref, g_ref, w_ref, cs_ref, bd_ref, o_ref, h_scr, *, n_norm):
    j = pl.program_id(1)

    @pl.when(j == 0)
    def _():
        h_scr[...] = _rms_normalize(x_ref[...], g_ref[...]).astype(BF16)

    acc = jnp.dot(h_scr[...], w_ref[...], preferred_element_type=F32)

    def plain():
        o_ref[...] = (acc * cs_ref[...]).astype(o_ref.dtype)

    def head_normed():
        for c in range(COL_TILE // NORM_CHUNK):
            sl = slice(c * NORM_CHUNK, (c + 1) * NORM_CHUNK)
            a = acc[:, sl]
            ss = jnp.dot((a * a).astype(BF16), bd_ref[...], preferred_element_type=F32)
            inv = lax.rsqrt(ss * (1.0 / HEAD_DIM) + RMS_EPS)
            o_ref[:, sl] = (a * inv * cs_ref[:, sl]).astype(o_ref.dtype)

    if n_norm == 0:
        plain()
    else:
        pl.when(j < n_norm)(head_normed)
        pl.when(j >= n_norm)(plain)


def _norm_proj(x, gain, w, layer, col_scale, n_norm):
    t, d = x.shape
    n = w.shape[2]
    head_id = jnp.arange(NORM_CHUNK) // HEAD_DIM
    block_diag = (head_id[:, None] == head_id[None, :]).astype(BF16)
    return pl.pallas_call(
        functools.partial(_norm_proj_body, n_norm=n_norm),
        out_shape=jax.ShapeDtypeStruct((t, n), BF16),
        grid=(t // ROW_TILE, n // COL_TILE),
        in_specs=[
            pl.BlockSpec((ROW_TILE, d), lambda i, j: (i, 0)),
            pl.BlockSpec((1, d), lambda i, j: (0, 0)),
            pl.BlockSpec((None, d, COL_TILE), lambda i, j: (layer, 0, j)),
            pl.BlockSpec((1, COL_TILE), lambda i, j: (0, j)),
            pl.BlockSpec((NORM_CHUNK, NORM_CHUNK), lambda i, j: (0, 0)),
        ],
        out_specs=pl.BlockSpec((ROW_TILE, COL_TILE), lambda i, j: (i, j)),
        scratch_shapes=[pltpu.VMEM((ROW_TILE, d), BF16)],
        compiler_params=_params("parallel", "arbitrary"),
        name="norm_proj",
    )(x, gain.reshape(1, d), w, col_scale.reshape(1, n).astype(F32), block_diag)


def _swiglu_hidden(h, wg, wu):
    g = jnp.dot(h, wg, preferred_element_type=F32)
    u = jnp.dot(h, wu, preferred_element_type=F32)
    return g * (1.0 / (1.0 + jnp.exp(-g))) * u


def _swiglu_down(h, wg_ref, wu_ref, wd_ref):
    y = None
    for c in range(wg_ref.shape[1] // SWIGLU_CHUNK):
        cols = slice(c * SWIGLU_CHUNK, (c + 1) * SWIGLU_CHUNK)
        a = _swiglu_hidden(h, wg_ref[:, cols], wu_ref[:, cols]).astype(BF16)
        part = jnp.dot(a, wd_ref[cols, :], preferred_element_type=F32)
        y = part if y is None else y + part
    return y


def _ffn_body(x_ref, o_ref, wo_ref, g_ref, wg_ref, wu_ref, wd_ref, y_ref, h_scr):
    f = pl.program_id(1)

    @pl.when(f == 0)
    def _():
        x = x_ref[...] + jnp.dot(o_ref[...], wo_ref[...], preferred_element_type=F32)
        h_scr[...] = _rms_normalize(x, g_ref[...]).astype(BF16)
        y_ref[...] = x

    y_ref[...] += _swiglu_down(h_scr[...], wg_ref, wu_ref, wd_ref)


def _mix_ffn_residual(x, o, w_o, mix_layer, gain, w_gate, w_up, w_down, layer):
    t, d = x.shape
    ff = w_gate.shape[2]
    return pl.pallas_call(
        _ffn_body,
        out_shape=jax.ShapeDtypeStruct((t, d), F32),
        grid=(t // ROW_TILE, ff // FF_TILE),
        in_specs=[
            pl.BlockSpec((ROW_TILE, d), lambda i, f: (i, 0)),
            pl.BlockSpec((ROW_TILE, d), lambda i, f: (i, 0)),
            pl.BlockSpec((None, d, d), lambda i, f: (mix_layer, 0, 0)),
            pl.BlockSpec((1, d), lambda i, f: (0, 0)),
            pl.BlockSpec((None, d, FF_TILE), lambda i, f: (layer, 0, f)),
            pl.BlockSpec((None, d, FF_TILE), lambda i, f: (layer, 0, f)),
            pl.BlockSpec((None, FF_TILE, d), lambda i, f: (layer, f, 0)),
        ],
        out_specs=pl.BlockSpec((ROW_TILE, d), lambda i, f: (i, 0)),
        scratch_shapes=[pltpu.VMEM((ROW_TILE, d), BF16)],
        compiler_params=_params("parallel", "arbitrary"),
        name="ffn",
    )(x, o, w_o, gain.reshape(1, d), w_gate, w_up, w_down)


def _pack_bf16_pairs(x):
    half = x.shape[1] // 2
    bits = pltpu.bitcast(x.astype(BF16).astype(F32), jnp.int32)
    return bits[:, :half] | lax.shift_right_logical(bits[:, half:], jnp.int32(16))


def _unpack_bf16_pairs(p):
    left = pltpu.bitcast(p & jnp.int32(-65536), F32)
    right = pltpu.bitcast(lax.shift_left(p, jnp.int32(16)), F32)
    return jnp.concatenate([left, right], axis=1).astype(BF16)


def _router_body(x_ref, o_ref, wo_ref, g_ref, r_ref, tri_ref, gates_ref, rank_ref, h_ref, x1_ref, count_scr):
    @pl.when(pl.program_id(0) == 0)
    def _():
        count_scr[...] = jnp.zeros_like(count_scr)

    x1 = x_ref[...] + jnp.dot(o_ref[...], wo_ref[...], preferred_element_type=F32)
    x1_ref[...] = x1
    h = _rms_normalize(x1, g_ref[...])
    h_ref[...] = _pack_bf16_pairs(h)
    logits = _dot_split(h, r_ref[...])
    lane = lax.broadcasted_iota(jnp.int32, logits.shape, 1).astype(F32)
    logits = jnp.where(lane < N_EXPERTS, logits, -jnp.inf)
    m1 = jnp.max(logits, axis=-1, keepdims=True)
    i1 = jnp.min(jnp.where(logits == m1, lane, float(LANES)), axis=-1, keepdims=True)
    rest = jnp.where(lane == i1, -jnp.inf, logits)
    m2 = jnp.max(rest, axis=-1, keepdims=True)
    i2 = jnp.min(jnp.where(rest == m2, lane, float(LANES)), axis=-1, keepdims=True)
    e = jnp.exp(m2 - m1)
    g1 = 1.0 / (1.0 + e)
    gates = jnp.where(lane == i1, g1, 0.0) + jnp.where(lane == i2, e * g1, 0.0)
    gates_ref[...] = gates.T[:N_EXPERTS, :]
    chosen = jnp.where((lane == i1) | (lane == i2), 1.0, 0.0)
    inclusive = jnp.dot(tri_ref[...], chosen.astype(BF16), preferred_element_type=F32)
    rank = jnp.where(chosen > 0.0, inclusive - 1.0 + count_scr[...], -1.0)
    rank_ref[...] = rank.T[:N_EXPERTS, :]
    count_scr[...] += inclusive[ROW_TILE - 1:ROW_TILE, :]


def _mix_router(x, o, w_o, mix_layer, gain, router):
    t, d = x.shape
    r = jnp.zeros((d, LANES), F32).at[:, :N_EXPERTS].set(router)
    idx = jnp.arange(ROW_TILE)
    tri = (idx[:, None] >= idx[None, :]).astype(BF16)
    rows = pl.BlockSpec((ROW_TILE, d), lambda i: (i, 0))
    return pl.pallas_call(
        _router_body,
        out_shape=(jax.ShapeDtypeStruct((N_EXPERTS, t), F32), jax.ShapeDtypeStruct((N_EXPERTS, t), F32),
                   jax.ShapeDtypeStruct((t, d // 2), jnp.int32), jax.ShapeDtypeStruct((t, d), F32)),
        grid=(t // ROW_TILE,),
        in_specs=[
            rows,
            rows,
            pl.BlockSpec((None, d, d), lambda i: (mix_layer, 0, 0)),
            pl.BlockSpec((1, d), lambda i: (0, 0)),
            pl.BlockSpec((d, LANES), lambda i: (0, 0)),
            pl.BlockSpec((ROW_TILE, ROW_TILE), lambda i: (0, 0)),
        ],
        out_specs=(pl.BlockSpec((N_EXPERTS, ROW_TILE), lambda i: (0, i)),
                   pl.BlockSpec((N_EXPERTS, ROW_TILE), lambda i: (0, i)),
                   pl.BlockSpec((ROW_TILE, d // 2), lambda i: (i, 0)),
                   rows),
        scratch_shapes=[pltpu.VMEM((1, LANES), F32)],
        compiler_params=_params("arbitrary"),
        name="router",
    )(x, o, w_o, gain.reshape(1, d), r, tri)


def _sc_mesh():
    return plsc.VectorSubcoreMesh(core_axis_name="core", subcore_axis_name="subcore",
                                  num_cores=SC_CORES, num_subcores=SC_SUBCORES)


def _sc_worker_base(per_worker):
    return (lax.axis_index("subcore") * SC_CORES + lax.axis_index("core")) * per_worker


def _sc_row_gather(table, idx):
    width = table.shape[1]
    n = idx.shape[0]
    per_worker = n // (SC_CORES * SC_SUBCORES)
    n_chunks = per_worker // SC_CHUNK_ROWS
    assert n == n_chunks * SC_CHUNK_ROWS * SC_CORES * SC_SUBCORES and n_chunks % 2 == 0

    @functools.partial(
        pl.kernel, mesh=_sc_mesh(), out_type=jax.ShapeDtypeStruct((n, width), table.dtype),
        scratch_types=[pltpu.VMEM((SC_CHUNK_ROWS,), jnp.int32), pltpu.VMEM((SC_CHUNK_ROWS,), jnp.int32),
                       pltpu.VMEM((SC_CHUNK_ROWS, width), table.dtype),
                       pltpu.VMEM((SC_CHUNK_ROWS, width), table.dtype),
                       pltpu.SemaphoreType.DMA, pltpu.SemaphoreType.DMA],
        name="sc_row_gather")
    def gather(table_hbm, idx_hbm, out_hbm, idx_a, idx_b, rows_a, rows_b, sem_a, sem_b):
        base = _sc_worker_base(per_worker)
        bufs = ((idx_a, rows_a, sem_a), (idx_b, rows_b, sem_b))

        def rows_of(c):
            return pl.ds(pl.multiple_of(base + c * SC_CHUNK_ROWS, SC_CHUNK_ROWS), SC_CHUNK_ROWS)

        def fetch(c, buf):
            idx_v, rows_v, sem = buf
            pltpu.sync_copy(idx_hbm.at[rows_of(c)], idx_v)
            return pltpu.make_async_copy(table_hbm.at[idx_v], rows_v, sem)

        fetch(0, bufs[0]).start()

        @pl.loop(0, n_chunks, step=2)
        def _(c):
            for b in range(2):
                idx_v, rows_v, sem = bufs[b]
                pltpu.make_async_copy(table_hbm.at[idx_v], rows_v, sem).wait()

                @pl.when(c + b + 1 < n_chunks)
                def _():
                    fetch(c + b + 1, bufs[1 - b]).start()

                pltpu.sync_copy(rows_v, out_hbm.at[rows_of(c + b)])

    return gather(table, idx)


def _sc_row_scatter_pair(rows, idx_lo, idx_hi, n_out):
    n, width = rows.shape
    per_worker = n // (SC_CORES * SC_SUBCORES)
    n_chunks = per_worker // SC_CHUNK_ROWS
    assert n == n_chunks * SC_CHUNK_ROWS * SC_CORES * SC_SUBCORES and n_chunks % 2 == 0
    index_scratch = pltpu.VMEM((SC_CHUNK_ROWS,), jnp.int32)
    rows_scratch = pltpu.VMEM((SC_CHUNK_ROWS, width), rows.dtype)

    @functools.partial(
        pl.kernel, mesh=_sc_mesh(), out_type=jax.ShapeDtypeStruct((n_out, width), rows.dtype),
        scratch_types=[index_scratch, index_scratch, rows_scratch, rows_scratch,
                       pltpu.SemaphoreType.DMA, pltpu.SemaphoreType.DMA],
        name="sc_row_scatter")
    def scatter(rows_hbm, lo_hbm, hi_hbm, out_hbm, lo_v, hi_v, rows_a, rows_b, sem_a, sem_b):
        base = _sc_worker_base(per_worker)
        bufs = ((rows_a, sem_a), (rows_b, sem_b))

        def rows_of(c):
            return pl.ds(pl.multiple_of(base + c * SC_CHUNK_ROWS, SC_CHUNK_ROWS), SC_CHUNK_ROWS)

        def load(c, buf):
            rows_v, sem = buf
            return pltpu.make_async_copy(rows_hbm.at[rows_of(c)], rows_v, sem)

        load(0, bufs[0]).start()

        @pl.loop(0, n_chunks, step=2)
        def _(c):
            for b in range(2):
                rows_v, _ = bufs[b]
                load(c + b, bufs[b]).wait()

                @pl.when(c + b + 1 < n_chunks)
                def _():
                    load(c + b + 1, bufs[1 - b]).start()

                pltpu.sync_copy(lo_hbm.at[rows_of(c + b)], lo_v)
                pltpu.sync_copy(hi_hbm.at[rows_of(c + b)], hi_v)
                pltpu.sync_copy(rows_v, out_hbm.at[lo_v])
                pltpu.sync_copy(rows_v, out_hbm.at[hi_v])

    return scatter(rows, idx_lo, idx_hi)


def _expert_ffn_body(te_ref, nu_ref, nv_ref, h_ref, wg_ref, wu_ref, wd_ref, y_ref, acc_scr):
    i = pl.program_id(0)
    f = pl.program_id(1)

    @pl.when(i < nu_ref[0])
    def _():
        row = lax.broadcasted_iota(jnp.int32, h_ref.shape, 0)
        packed = jnp.where(row < nv_ref[i], h_ref[...], 0)
        y = _swiglu_down(_unpack_bf16_pairs(packed), wg_ref, wu_ref, wd_ref)

        @pl.when(f == 0)
        def _():
            acc_scr[...] = y

        @pl.when(f > 0)
        def _():
            acc_scr[...] += y

        @pl.when(f == pl.num_programs(1) - 1)
        def _():
            y_ref[...] = _pack_bf16_pairs(acc_scr[...])


def _expert_ffn(h_sorted, tile_expert, n_used, tile_valid, w_gate, w_up, w_down, layer):
    rows, half = h_sorted.shape
    d = 2 * half
    ff = w_gate.shape[3]
    n_f = ff // MOE_FF_TILE

    def row_map(i, f, te, nu, nv):
        return (jnp.minimum(i, nu[0] - 1), 0)

    def col_step(i, f, nu):
        return jnp.where(i < nu[0], f, n_f - 1)

    grid_spec = pltpu.PrefetchScalarGridSpec(
        num_scalar_prefetch=3,
        grid=(rows // MOE_ROW_TILE, n_f),
        in_specs=[
            pl.BlockSpec((MOE_ROW_TILE, half), row_map),
            pl.BlockSpec((None, None, d, MOE_FF_TILE),
                         lambda i, f, te, nu, nv: (layer, te[i], 0, col_step(i, f, nu))),
            pl.BlockSpec((None, None, d, MOE_FF_TILE),
                         lambda i, f, te, nu, nv: (layer, te[i], 0, col_step(i, f, nu))),
            pl.BlockSpec((None, None, MOE_FF_TILE, d),
                         lambda i, f, te, nu, nv: (layer, te[i], col_step(i, f, nu), 0)),
        ],
        out_specs=pl.BlockSpec((MOE_ROW_TILE, half), row_map),
        scratch_shapes=[pltpu.VMEM((MOE_ROW_TILE, d), F32)],
    )
    return pl.pallas_call(
        _expert_ffn_body,
        out_shape=jax.ShapeDtypeStruct((rows, half), jnp.int32),
        grid_spec=grid_spec,
        compiler_params=_params("arbitrary", "arbitrary"),
        name="expert_ffn",
    )(tile_expert, n_used, tile_valid, h_sorted, w_gate, w_up, w_down)


def _combine_body(x_ref, y_ref, g_ref, o_ref):
    out = x_ref[...]
    for s in range(2):
        column = jnp.broadcast_to(g_ref[s:s + 1, :], (LANES, g_ref.shape[1])).T
        gate = jnp.concatenate([column] * (out.shape[1] // LANES), axis=1)
        out = out + _unpack_bf16_pairs(y_ref[s]).astype(F32) * gate
    o_ref[...] = out


def _combine_residual(x, y_pairs, gates2):
    t, d = x.shape
    return pl.pallas_call(
        _combine_body,
        out_shape=jax.ShapeDtypeStruct((t, d), F32),
        grid=(t // ROW_TILE,),
        in_specs=[
            pl.BlockSpec((ROW_TILE, d), lambda i: (i, 0)),
            pl.BlockSpec((2, ROW_TILE, d // 2), lambda i: (0, i, 0)),
            pl.BlockSpec((2, ROW_TILE), lambda i: (0, i)),
        ],
        out_specs=pl.BlockSpec((ROW_TILE, d), lambda i: (i, 0)),
        compiler_params=_params("parallel"),
        name="moe_combine",
    )(x, y_pairs, gates2)


def _mix_moe_residual(x, o, w_o, mix_layer, gain, router, w_gate, w_up, w_down, layer):
    t, d = x.shape
    gates, rank, h_packed, x = _mix_router(x, o, w_o, mix_layer, gain, router)
    rank8 = rank.astype(jnp.int32)
    chosen = rank8 >= 0
    counts = jnp.sum(chosen, axis=1, dtype=jnp.int32)
    padded = (counts + MOE_ROW_TILE - 1) // MOE_ROW_TILE * MOE_ROW_TILE
    ends = jnp.cumsum(padded)
    starts = ends - padded
    pos = starts[:, None] + rank8
    max_rows = 2 * t + N_EXPERTS * MOE_ROW_TILE
    pos_lo = jnp.min(jnp.where(chosen, pos, max_rows), axis=0)
    pos_hi = jnp.max(jnp.where(chosen, pos, -1), axis=0)
    gates2 = jnp.stack([jnp.sum(jnp.where(chosen & (pos == pos_lo[None, :]), gates, 0.0), axis=0),
                        jnp.sum(jnp.where(chosen & (pos == pos_hi[None, :]), gates, 0.0), axis=0)])
    n_tiles = max_rows // MOE_ROW_TILE
    n_used = (ends[-1] // MOE_ROW_TILE).astype(jnp.int32)
    tile_start = jnp.minimum(jnp.arange(n_tiles, dtype=jnp.int32), n_used - 1) * MOE_ROW_TILE
    tile_expert = jnp.sum(tile_start[:, None] >= ends[None, :], axis=1, dtype=jnp.int32)
    tile_valid = jnp.clip((starts + counts)[tile_expert] - tile_start, 0, MOE_ROW_TILE).astype(jnp.int32)

    h_sorted = _sc_row_scatter_pair(h_packed, pos_lo, pos_hi, max_rows)
    y_sorted = _expert_ffn(h_sorted, tile_expert, n_used.reshape(1), tile_valid, w_gate, w_up, w_down, layer)
    y_pairs = _sc_row_gather(y_sorted, jnp.concatenate([pos_lo, pos_hi])).reshape(2, t, d // 2)
    return _combine_residual(x, y_pairs, gates2)


def _split_head_pair(q):
    is_first = lax.broadcasted_iota(jnp.int32, (1, LANES), 1) < HEAD_DIM
    zero = jnp.zeros_like(q)
    return jnp.where(is_first, q, zero), jnp.where(is_first, zero, q)


def _merge_head_pair(first, second):
    is_first = lax.broadcasted_iota(jnp.int32, (1, LANES), 1) < HEAD_DIM
    return jnp.where(is_first, first, second)


def _qk(q, k):
    return lax.dot_general(q, k, (((1,), (1,)), ((), ())), preferred_element_type=F32)


def _transpose_values(v_ref, vt_scr):
    for c in range(v_ref.shape[0] // ATT_BLOCK):
        rows = slice(c * ATT_BLOCK, (c + 1) * ATT_BLOCK)
        vt_scr[:, rows] = v_ref[rows, :].astype(F32).T.astype(vt_scr.dtype)


def _sb_body(q_ref, k_ref, v_ref, u_ref, o_ref, vt_scr, acc_scr, carry_scr, z_scr, keep_scr, sum_scr):
    qi = pl.program_id(2)

    @pl.when(qi == 0)
    def _():
        _transpose_values(v_ref, vt_scr)

    upper = u_ref[...]
    acc_scr[...] = jnp.zeros_like(acc_scr)
    carry_scr[...] = jnp.zeros_like(carry_scr)
    key = lax.broadcasted_iota(jnp.int32, (ATT_BLOCK, ATT_BLOCK), 0)
    query = lax.broadcasted_iota(jnp.int32, (ATT_BLOCK, ATT_BLOCK), 1)
    strict = key < query
    qs = [_split_head_pair(q_ref[j * ATT_BLOCK:(j + 1) * ATT_BLOCK, :]) for j in range(ATT_K_PER_Q)]

    def process(j, kb, keep):
        start = pl.multiple_of(kb * ATT_BLOCK, ATT_BLOCK)
        k = k_ref[pl.ds(start, ATT_BLOCK), :]
        for hh in range(2):
            z = _qk(k, qs[j][hh])
            log_beta = z - _softplus(z)
            log_keep = log_beta - z
            if keep is not None:
                log_keep = jnp.where(keep, log_keep, 0.0)
            remain = jnp.dot(upper, log_keep.astype(BF16), preferred_element_type=F32)
            w = jnp.exp(log_beta + remain + carry_scr[j, hh])
            if keep is not None:
                w = jnp.where(keep, w, 0.0)
            carry_scr[j, hh] += jnp.sum(log_keep, axis=0, keepdims=True)
            vt = vt_scr[pl.ds(hh * HEAD_DIM, HEAD_DIM), pl.ds(start, ATT_BLOCK)]
            acc_scr[j, hh] += jnp.dot(vt, w.astype(BF16), preferred_element_type=F32)

    units = []
    for j in range(ATT_K_PER_Q):
        g = qi * ATT_K_PER_Q + j
        has_previous = None if j > 0 else jnp.broadcast_to(g > 0, strict.shape)
        for kb, keep in ((g, strict), (jnp.maximum(g - 1, 0), has_previous)):
            for hh in range(2):
                units.append((j, hh, pl.multiple_of(kb * ATT_BLOCK, ATT_BLOCK), keep))

    for u, (j, hh, start, keep) in enumerate(units):
        z_scr[u] = _qk(k_ref[pl.ds(start, ATT_BLOCK), :], qs[j][hh])

    for u, (j, hh, start, keep) in enumerate(units):
        z = z_scr[u]
        log_beta = z - _softplus(z)
        log_keep = log_beta - z
        if keep is not None:
            log_keep = jnp.where(keep, log_keep, 0.0)
        z_scr[u] = log_beta
        keep_scr[u] = log_keep.astype(BF16)
        sum_scr[u] = jnp.sum(log_keep, axis=0, keepdims=True)

    for u in range(len(units)):
        z_scr[u] += jnp.dot(upper, keep_scr[u], preferred_element_type=F32)

    for u, (j, hh, start, keep) in enumerate(units):
        diagonal = u % 4 < 2
        log_w = z_scr[u] if diagonal else z_scr[u] + sum_scr[u - 2]
        w = jnp.exp(log_w)
        if keep is not None:
            w = jnp.where(keep, w, 0.0)
        keep_scr[u] = w.astype(BF16)

    for u, (j, hh, start, keep) in enumerate(units):
        vt = vt_scr[pl.ds(hh * HEAD_DIM, HEAD_DIM), pl.ds(start, ATT_BLOCK)]
        acc_scr[j, hh] += jnp.dot(vt, keep_scr[u], preferred_element_type=F32)
        carry_scr[j, hh] += sum_scr[u]

    for j in range(ATT_K_PER_Q):
        def more(kb, j=j):
            return (kb >= 0) & (jnp.max(carry_scr[j]) > SB_UNDERFLOW_LOG)

        def step(kb, j=j):
            process(j, kb, None)
            return kb - 1

        lax.while_loop(more, step, qi * ATT_K_PER_Q + j - 2)

    for j in range(ATT_K_PER_Q):
        out_t = jnp.concatenate([acc_scr[j, 0], acc_scr[j, 1]], axis=0)
        o_ref[j * ATT_BLOCK:(j + 1) * ATT_BLOCK, :] = out_t.T.astype(o_ref.dtype)


def _sb_attention(proj, batch, seq):
    t = proj.shape[0]
    nq = seq // ATT_Q_BLOCK
    idx = jnp.arange(ATT_BLOCK)
    upper = (idx[None, :] > idx[:, None]).astype(BF16)
    return pl.pallas_call(
        _sb_body,
        out_shape=jax.ShapeDtypeStruct((t, D_MODEL), BF16),
        grid=(batch, HEAD_PAIRS, nq),
        in_specs=[
            pl.BlockSpec((ATT_Q_BLOCK, LANES), lambda b, p, i: (b * nq + i, p)),
            pl.BlockSpec((seq, LANES), lambda b, p, i: (b, HEAD_PAIRS + p)),
            pl.BlockSpec((seq, LANES), lambda b, p, i: (b, 2 * HEAD_PAIRS + p)),
            pl.BlockSpec((ATT_BLOCK, ATT_BLOCK), lambda b, p, i: (0, 0)),
        ],
        out_specs=pl.BlockSpec((ATT_Q_BLOCK, LANES), lambda b, p, i: (b * nq + i, p)),
        scratch_shapes=[
            pltpu.VMEM((LANES, seq), BF16),
            pltpu.VMEM((ATT_K_PER_Q, 2, HEAD_DIM, ATT_BLOCK), F32),
            pltpu.VMEM((ATT_K_PER_Q, 2, 1, ATT_BLOCK), F32),
            pltpu.VMEM((4 * ATT_K_PER_Q, ATT_BLOCK, ATT_BLOCK), F32),
            pltpu.VMEM((4 * ATT_K_PER_Q, ATT_BLOCK, ATT_BLOCK), BF16),
            pltpu.VMEM((4 * ATT_K_PER_Q, 1, ATT_BLOCK), F32),
        ],
        compiler_params=_params("parallel", "parallel", "arbitrary"),
        name="sb_attention",
    )(proj, proj, proj, upper)


def _fox_gate_body(x_ref, g_ref, w_ref, b_ref, tri_ref, c_ref):
    h = _rms_normalize(x_ref[...], g_ref[...])
    logits = _dot_split(h, w_ref[...])
    log_f = -_softplus(-(logits + b_ref[...]))
    seq = log_f.shape[0]
    carry = jnp.zeros((1, LANES), F32)
    for blk in range(seq // ATT_BLOCK):
        rows = slice(blk * ATT_BLOCK, (blk + 1) * ATT_BLOCK)
        c = carry
        for part in _split_bf16(log_f[rows], 3):
            c = c + jnp.dot(tri_ref[...], part, preferred_element_type=F32)
        c_ref[rows, :] = c
        carry = c[ATT_BLOCK - 1:ATT_BLOCK, :]


def _fox_cum_log_forget(x, gain, w_gate, b_gate, batch, seq):
    t, d = x.shape
    w = jnp.zeros((d, LANES), F32).at[:, :N_HEADS].set(w_gate)
    b = jnp.zeros((1, LANES), F32).at[0, :N_HEADS].set(b_gate)
    idx = jnp.arange(ATT_BLOCK)
    tri = (idx[:, None] >= idx[None, :]).astype(BF16)
    return pl.pallas_call(
        _fox_gate_body,
        out_shape=jax.ShapeDtypeStruct((t, LANES), F32),
        grid=(batch,),
        in_specs=[
            pl.BlockSpec((seq, d), lambda i: (i, 0)),
            pl.BlockSpec((1, d), lambda i: (0, 0)),
            pl.BlockSpec((d, LANES), lambda i: (0, 0)),
            pl.BlockSpec((1, LANES), lambda i: (0, 0)),
            pl.BlockSpec((ATT_BLOCK, ATT_BLOCK), lambda i: (0, 0)),
        ],
        out_specs=pl.BlockSpec((seq, LANES), lambda i: (i, 0)),
        compiler_params=_params("parallel"),
        name="fox_gate",
    )(x, gain.reshape(1, d), w, b, tri)


def _fox_body(q_ref, k_ref, v_ref, cq_ref, ck_ref, o_ref, vt_scr, ckb_scr, a_scr, p_scr):
    qi = pl.program_id(2)

    @pl.when(qi == 0)
    def _():
        _transpose_values(v_ref, vt_scr)
        for hh in range(2):
            for c in range(ckb_scr.shape[1] // ATT_BLOCK):
                rows = slice(c * ATT_BLOCK, (c + 1) * ATT_BLOCK)
                ckb_scr[hh, rows, :] = jnp.broadcast_to(ck_ref[0, hh, :, rows], (LANES, ATT_BLOCK)).T

    qs = _split_head_pair(q_ref[...])
    key = lax.broadcasted_iota(jnp.int32, (ATT_BLOCK, ATT_Q_BLOCK), 0)
    query = lax.broadcasted_iota(jnp.int32, (ATT_BLOCK, ATT_Q_BLOCK), 1)
    row_shape = (1, ATT_Q_BLOCK)

    def scores(kb, tops, diagonal):
        start = kb * ATT_BLOCK
        k = k_ref[pl.ds(start, ATT_BLOCK), :]
        if diagonal:
            causal = key + (kb * ATT_BLOCK - qi * ATT_Q_BLOCK) <= query
        new_tops = []
        for hh in range(2):
            ck = ckb_scr[hh, pl.ds(start, ATT_BLOCK), :]
            a = _qk(k, qs[hh]) - jnp.concatenate([ck] * (ATT_Q_BLOCK // LANES), axis=1)
            if diagonal:
                a = jnp.where(causal, a, NEG_INF)
            a_scr[hh, kb] = a
            new_tops.append(jnp.maximum(tops[hh], jnp.max(a, axis=0, keepdims=True)))
        return tuple(new_tops)

    def sweep(n_blocks):
        tops = (jnp.full(row_shape, NEG_INF, F32),) * 2
        for kb in range(n_blocks):
            tops = scores(kb, tops, kb >= n_blocks - ATT_K_PER_Q)
        shifts = tuple(cq_ref[0, hh] - (tops[hh] + cq_ref[0, hh]) for hh in range(2))
        sums = [jnp.zeros(row_shape, F32)] * 2
        for kb in range(n_blocks):
            for hh in range(2):
                p = jnp.exp(a_scr[hh, kb] + shifts[hh])
                p_scr[hh, kb] = p.astype(BF16)
                sums[hh] = sums[hh] + jnp.sum(p, axis=0, keepdims=True)
        accs = [jnp.zeros((HEAD_DIM, ATT_Q_BLOCK), F32)] * 2
        for kb in range(n_blocks):
            for hh in range(2):
                vt = vt_scr[hh * HEAD_DIM:(hh + 1) * HEAD_DIM, kb * ATT_BLOCK:(kb + 1) * ATT_BLOCK]
                accs[hh] = accs[hh] + jnp.dot(vt, p_scr[hh, kb], preferred_element_type=F32)
        out_t = jnp.concatenate([accs[0] / sums[0], accs[1] / sums[1]], axis=0)
        o_ref[...] = out_t.T.astype(o_ref.dtype)

    for q in range(k_ref.shape[0] // ATT_Q_BLOCK):
        pl.when(qi == q)(functools.partial(sweep, (q + 1) * ATT_K_PER_Q))


def _fox_attention(proj, cum, batch, seq):
    t = proj.shape[0]
    nq = seq // ATT_Q_BLOCK
    cum_h = cum[:, :N_HEADS].reshape(batch, seq, N_HEADS).transpose(0, 2, 1)
    cum_rows = cum_h.reshape(batch, N_HEADS, 1, seq)
    return pl.pallas_call(
        _fox_body,
        out_shape=jax.ShapeDtypeStruct((t, D_MODEL), BF16),
        grid=(batch, HEAD_PAIRS, nq),
        in_specs=[
            pl.BlockSpec((ATT_Q_BLOCK, LANES), lambda b, p, i: (b * nq + i, p)),
            pl.BlockSpec((seq, LANES), lambda b, p, i: (b, HEAD_PAIRS + p)),
            pl.BlockSpec((seq, LANES), lambda b, p, i: (b, 2 * HEAD_PAIRS + p)),
            pl.BlockSpec((1, 2, 1, ATT_Q_BLOCK), lambda b, p, i: (b, p, 0, i)),
            pl.BlockSpec((1, 2, 1, seq), lambda b, p, i: (b, p, 0, 0)),
        ],
        out_specs=pl.BlockSpec((ATT_Q_BLOCK, LANES), lambda b, p, i: (b * nq + i, p)),
        scratch_shapes=[
            pltpu.VMEM((LANES, seq), BF16),
            pltpu.VMEM((2, seq, LANES), F32),
            pltpu.VMEM((2, seq // ATT_BLOCK, ATT_BLOCK, ATT_Q_BLOCK), F32),
            pltpu.VMEM((2, seq // ATT_BLOCK, ATT_BLOCK, ATT_Q_BLOCK), BF16),
        ],
        compiler_params=_params("parallel", "parallel", "arbitrary"),
        name="fox_attention",
    )(proj, proj, proj, cum_rows, cum_rows)


def _band_body(q_ref, k_ref, v_ref, bias_ref, o_ref, s_ref, vt_scr, logit_scr, p_scr, inv_scr, *, sub_len):
    seq = v_ref.shape[0]
    n_blocks = seq // DIL_BLOCK
    _transpose_values(v_ref, vt_scr)

    def key_rows(n):
        first = (n * DIL_BLOCK) % sub_len == 0
        return first, slice((n if first else n - 1) * DIL_BLOCK, (n + 1) * DIL_BLOCK)

    for n in range(n_blocks):
        first, k_rows = key_rows(n)
        qs = _split_head_pair(q_ref[n * DIL_BLOCK:(n + 1) * DIL_BLOCK, :])
        k = k_ref[k_rows, :]
        for hh in range(2):
            bias = bias_ref[hh, DIL_BLOCK:, :] if first else bias_ref[hh]
            logit_scr[2 * n + hh, :k.shape[0], :] = _qk(k, qs[hh]) + bias

    for n in range(n_blocks):
        first, k_rows = key_rows(n)
        n_keys = k_rows.stop - k_rows.start
        for hh in range(2):
            logits = logit_scr[2 * n + hh, :n_keys, :]
            m = jnp.max(logits, axis=0, keepdims=True)
            p = jnp.exp(logits - m)
            l = jnp.sum(p, axis=0, keepdims=True)
            p_scr[2 * n + hh, :n_keys, :] = p.astype(BF16)
            inv_scr[2 * n + hh] = 1.0 / l
            s_ref[0, 0, hh:hh + 1, n * DIL_BLOCK:(n + 1) * DIL_BLOCK] = m + jnp.log(l)

    for n in range(n_blocks):
        first, k_rows = key_rows(n)
        n_keys = k_rows.stop - k_rows.start
        outs = []
        for hh in range(2):
            vt = vt_scr[hh * HEAD_DIM:(hh + 1) * HEAD_DIM, k_rows]
            o = jnp.dot(vt, p_scr[2 * n + hh, :n_keys, :], preferred_element_type=F32)
            outs.append(o * inv_scr[2 * n + hh])
        o_ref[n * DIL_BLOCK:(n + 1) * DIL_BLOCK, :] = jnp.concatenate(outs, axis=0).T.astype(o_ref.dtype)


def _band_attention(qk, v, bias_t, batch, seq, sub_len):
    t = v.shape[0]
    return pl.pallas_call(
        functools.partial(_band_body, sub_len=sub_len),
        out_shape=(jax.ShapeDtypeStruct((t, D_MODEL), BF16),
                   jax.ShapeDtypeStruct((batch, HEAD_PAIRS, 2, seq), F32)),
        grid=(batch, HEAD_PAIRS),
        in_specs=[
            pl.BlockSpec((None, seq, LANES), lambda b, p: (0, b, p)),
            pl.BlockSpec((None, seq, LANES), lambda b, p: (1, b, p)),
            pl.BlockSpec((seq, LANES), lambda b, p: (b, p)),
            pl.BlockSpec((2, 2 * DIL_BLOCK, DIL_BLOCK), lambda b, p: (p, 0, 0)),
        ],
        out_specs=(pl.BlockSpec((seq, LANES), lambda b, p: (b, p)),
                   pl.BlockSpec((1, 1, 2, seq), lambda b, p: (b, p, 0, 0))),
        scratch_shapes=[pltpu.VMEM((LANES, seq), BF16),
                        pltpu.VMEM((2 * seq // DIL_BLOCK, 2 * DIL_BLOCK, DIL_BLOCK), F32),
                        pltpu.VMEM((2 * seq // DIL_BLOCK, 2 * DIL_BLOCK, DIL_BLOCK), BF16),
                        pltpu.VMEM((2 * seq // DIL_BLOCK, 1, DIL_BLOCK), F32)],
        compiler_params=_params("parallel", "parallel"),
        name="band_attention",
    )(qk, qk, v, bias_t)


def _softmax_merge_body(o1_ref, o2_ref, o3_ref, s_ref, e_ref, o_ref):
    s = s_ref[...]
    groups = [s, pltpu.roll(s, LANES - N_HEADS, axis=1), pltpu.roll(s, LANES - 2 * N_HEADS, axis=1)]
    top = jnp.maximum(jnp.maximum(groups[0], groups[1]), groups[2])
    weights = [jnp.exp(g - top) for g in groups]
    inv = 1.0 / (weights[0] + weights[1] + weights[2])
    out = jnp.zeros(o_ref.shape, F32)
    for w, part in zip(weights, (o1_ref, o2_ref, o3_ref)):
        spread = jnp.dot((w * inv).astype(BF16), e_ref[...], preferred_element_type=F32)
        out = out + spread * part[...].astype(F32)
    o_ref[...] = out.astype(o_ref.dtype)


def _softmax_merge(outs, s_all):
    t, d = outs[0].shape
    head_of = jnp.arange(d) // HEAD_DIM
    expand = (jnp.arange(LANES)[:, None] == head_of[None, :]).astype(BF16)
    rows = pl.BlockSpec((ROW_TILE, d), lambda i: (i, 0))
    return pl.pallas_call(
        _softmax_merge_body,
        out_shape=jax.ShapeDtypeStruct((t, d), BF16),
        grid=(t // ROW_TILE,),
        in_specs=[rows, rows, rows,
                  pl.BlockSpec((ROW_TILE, LANES), lambda i: (i, 0)),
                  pl.BlockSpec((LANES, d), lambda i: (0, 0))],
        out_specs=rows,
        compiler_params=_params("parallel"),
        name="softmax_merge",
    )(*outs, s_all, expand)


def _t5_causal_bucket(distance):
    max_exact = N_REL_BUCKETS // 2
    d = jnp.maximum(distance, 1).astype(F32)
    log_b = max_exact + (jnp.log(d / max_exact) / math.log(REL_MAX_DISTANCE / max_exact)
                         * (N_REL_BUCKETS - max_exact)).astype(jnp.int32)
    log_b = jnp.minimum(log_b, N_REL_BUCKETS - 1)
    return jnp.where(distance < max_exact, distance, log_b)


def _dilated_bias(rel_bias):
    kj = jnp.arange(2 * DIL_BLOCK, dtype=jnp.int32)
    qi = jnp.arange(DIL_BLOCK, dtype=jnp.int32)
    delta = qi[None, :] + DIL_BLOCK - kj[:, None]
    in_band = (delta >= 0) & (delta <= DIL_SPAN)
    buckets = jnp.stack([_t5_causal_bucket(jnp.maximum(delta, 0) * dil) for _, dil in DILATED_PAIRS])
    one_hot = (buckets[..., None] == jnp.arange(N_REL_BUCKETS)).astype(F32)
    bias = jnp.einsum("gkqb,bh->ghkq", one_hot, rel_bias.astype(F32), precision=lax.Precision.HIGHEST)
    return jnp.where(in_band[None, None], bias, NEG_INF)


def _dilated_proj_body(x_ref, g_ref, w_ref, cs_ref, bd_ref, *rest):
    n_groups = len(DILATED_PAIRS)
    qk_refs, v_refs = rest[:n_groups], rest[n_groups:2 * n_groups]
    h_scr, res_scr = rest[2 * n_groups:]
    j = pl.program_id(1)

    @pl.when(j == 0)
    def _():
        h_scr[...] = _rms_normalize(x_ref[...], g_ref[...]).astype(BF16)

    acc = jnp.dot(h_scr[...], w_ref[...], preferred_element_type=F32)

    def keep(res, first_lane):
        for c in range(res.shape[1] // LANES):
            res_scr[first_lane // LANES + c] = res[:, c * LANES:(c + 1) * LANES]

    @pl.when(j < 2 * n_groups)
    def _():
        for c in range(COL_TILE // NORM_CHUNK):
            sl = slice(c * NORM_CHUNK, (c + 1) * NORM_CHUNK)
            a = acc[:, sl]
            ss = jnp.dot((a * a).astype(BF16), bd_ref[...], preferred_element_type=F32)
            keep(a * lax.rsqrt(ss * (1.0 / HEAD_DIM) + RMS_EPS) * cs_ref[:, sl], c * NORM_CHUNK)

    @pl.when(j == 2 * n_groups)
    def _():
        keep(acc * cs_ref[...], 0)

    def write_classes(ref, dil):
        rows = res_scr.shape[1] // dil
        for c in range(res_scr.shape[0]):
            lanes = slice(c * LANES, (c + 1) * LANES)
            if dil == 1:
                ref[:, lanes] = res_scr[c].astype(ref.dtype)
            else:
                for r in range(dil):
                    ref[r, :, lanes] = res_scr[c, pl.ds(r, rows, stride=dil), :].astype(ref.dtype)

    for g, (_, dil) in enumerate(DILATED_PAIRS):
        pl.when(j // 2 == g)(functools.partial(write_classes, qk_refs[g], dil))
        pl.when(j == 2 * n_groups)(functools.partial(write_classes, v_refs[g], dil))


def _dilated_proj(x, gain, w, layer, col_scale, batch, seq):
    t, d = x.shape
    n = w.shape[2]
    n_groups = len(DILATED_PAIRS)
    tiles_per_seq = seq // DIL_ROW_TILE
    head_id = jnp.arange(NORM_CHUNK) // HEAD_DIM
    block_diag = (head_id[:, None] == head_id[None, :]).astype(BF16)
    out_shape, out_specs = [], []
    for kind in ("qk", "v"):
        for g, (_, dil) in enumerate(DILATED_PAIRS):
            rows = DIL_ROW_TILE // dil
            if kind == "qk":
                shape = (2, batch, dil, seq // dil, d)
                block = (None, None, dil, rows, d)
                index = lambda i, j, g=g: (jnp.clip(j - 2 * g, 0, 1), i // tiles_per_seq, 0, i % tiles_per_seq, 0)
            else:
                shape = (batch, dil, seq // dil, d)
                block = (None, dil, rows, d)
                index = lambda i, j: (i // tiles_per_seq, 0, i % tiles_per_seq, 0)
            if dil == 1:
                block = block[:-3] + (None,) + block[-2:]
            out_shape.append(jax.ShapeDtypeStruct(shape, BF16))
            out_specs.append(pl.BlockSpec(block, index))
    outs = pl.pallas_call(
        _dilated_proj_body,
        out_shape=tuple(out_shape),
        grid=(t // DIL_ROW_TILE, n // COL_TILE),
        in_specs=[
            pl.BlockSpec((DIL_ROW_TILE, d), lambda i, j: (i, 0)),
            pl.BlockSpec((1, d), lambda i, j: (0, 0)),
            pl.BlockSpec((None, d, COL_TILE), lambda i, j: (layer, 0, j)),
            pl.BlockSpec((1, COL_TILE), lambda i, j: (0, j)),
            pl.BlockSpec((NORM_CHUNK, NORM_CHUNK), lambda i, j: (0, 0)),
        ],
        out_specs=tuple(out_specs),
        scratch_shapes=[pltpu.VMEM((DIL_ROW_TILE, d), BF16),
                        pltpu.VMEM((COL_TILE // LANES, DIL_ROW_TILE, LANES), F32)],
        compiler_params=_params("parallel", "arbitrary"),
        name="dilated_proj",
    )(x, gain.reshape(1, d), w, col_scale.reshape(1, n).astype(F32), block_diag)
    qk = [o.reshape(2, t, d) for o in outs[:n_groups]]
    v = [o.reshape(t, d) for o in outs[n_groups:]]
    return qk, v


def _dilated_attention(qk, v, rel_bias, batch, seq):
    t = v[0].shape[0]
    n_groups = len(DILATED_PAIRS)
    bias_t = _dilated_bias(rel_bias)
    outs, stats = [], []
    for g, (_, dil) in enumerate(DILATED_PAIRS):
        sub_len = seq // dil
        o, s = _band_attention(qk[g], v[g], bias_t[g], batch, seq, sub_len)
        o = o.reshape(batch, dil, sub_len, D_MODEL).transpose(0, 2, 1, 3).reshape(t, D_MODEL)
        s = s.reshape(batch, HEAD_PAIRS, 2, dil, sub_len).swapaxes(3, 4)
        outs.append(o)
        stats.append(s.reshape(batch, N_HEADS, seq).transpose(0, 2, 1).reshape(t, N_HEADS))
    stats.append(jnp.zeros((t, LANES - n_groups * N_HEADS), F32))
    return _softmax_merge(outs, jnp.concatenate(stats, axis=1))


def _tile_heads(v):
    return jnp.tile(v.astype(F32), N_HEADS)


def _sb_mixer(x, gain, w_qkv, layer, batch, seq):
    ones = jnp.ones((D_MODEL,), F32)
    col_scale = jnp.concatenate([ones * QK_SCALE, ones, ones])
    proj = _norm_proj(x, gain, w_qkv, layer, col_scale, 0)
    return _sb_attention(proj, batch, seq)


def _dilated_mixer(x, gain, w_in, q_norm, k_norm, rel_bias, layer, batch, seq):
    scales = []
    for g in range(len(DILATED_PAIRS)):
        scales += [_tile_heads(q_norm[g]) * QK_SCALE, _tile_heads(k_norm[g])]
    scales.append(jnp.ones((D_MODEL,), F32))
    qk, v = _dilated_proj(x, gain, w_in, layer, jnp.concatenate(scales), batch, seq)
    return _dilated_attention(qk, v, rel_bias, batch, seq)


def _fox_mixer(x, gain, w_qkv, w_gate, b_f, q_norm, k_norm, layer, batch, seq):
    col_scale = jnp.concatenate([_tile_heads(q_norm) * QK_SCALE, _tile_heads(k_norm),
                                 jnp.ones((D_MODEL,), F32)])
    proj = _norm_proj(x, gain, w_qkv, layer, col_scale, 2)
    cum = _fox_cum_log_forget(x, gain, w_gate, b_f, batch, seq)
    return _fox_attention(proj, cum, batch, seq)


def kernel(x, sb_w_qkv, sb_w_o, dil_w_in, dil_q_norm, dil_k_norm, dil_w_o, fox_w_in, fox_b_f,
           fox_q_norm, fox_k_norm, fox_w_o, rel_bias, attn_norm, ffn_norm, mlp_w_gate, mlp_w_up,
           mlp_w_down, moe_router, moe_w_gate, moe_w_up, moe_w_down):
    batch, seq, d = x.shape
    depth = attn_norm.shape[0]
    sb_w_qkv, sb_w_o, dil_w_in, dil_w_o, fox_w_o, mlp_w_gate, mlp_w_up, mlp_w_down = (
        w.astype(BF16) for w in (sb_w_qkv, sb_w_o, dil_w_in, dil_w_o, fox_w_o, mlp_w_gate, mlp_w_up,
                                 mlp_w_down))
    moe_w_gate, moe_w_up, moe_w_down = (w.astype(BF16) for w in (moe_w_gate, moe_w_up, moe_w_down))
    fox_w_qkv = fox_w_in[:, :, :3 * D_MODEL].astype(BF16)
    h = x.reshape(batch * seq, d)
    for i in range(depth):
        kind, j = i % 3, i // 3
        if kind == 0:
            mixed, w_o = _sb_mixer(h, attn_norm[i], sb_w_qkv, j, batch, seq), sb_w_o
        elif kind == 1:
            mixed, w_o = _dilated_mixer(h, attn_norm[i], dil_w_in, dil_q_norm[j], dil_k_norm[j], rel_bias,
                                        j, batch, seq), dil_w_o
        else:
            mixed, w_o = _fox_mixer(h, attn_norm[i], fox_w_qkv, fox_w_in[j, :, 3 * D_MODEL:], fox_b_f[j],
                                    fox_q_norm[j], fox_k_norm[j], j, batch, seq), fox_w_o
        f = i // 2
        if i % 2 == 0:
            h = _mix_ffn_residual(h, mixed, w_o, j, ffn_norm[i], mlp_w_gate, mlp_w_up, mlp_w_down, f)
        else:
            h = _mix_moe_residual(h, mixed, w_o, j, ffn_norm[i], moe_router[f], moe_w_gate, moe_w_up,
                                  moe_w_down, f)
    return h.reshape(batch, seq, d)
```

```python
import functools
import math

import jax
import jax.numpy as jnp
from jax import lax
from jax.experimental import pallas as pl
from jax.experimental.pallas import tpu as pltpu
from jax.experimental.pallas import tpu_sc as plsc

D_MODEL = 1024
N_HEADS = 16
HEAD_DIM = 64
LANES = 128
HEAD_PAIRS = D_MODEL // LANES
D_FF = 3584
N_EXPERTS = 8
N_REL_BUCKETS = 32
REL_MAX_DISTANCE = 2048
DILATED_PAIRS = ((128, 1), (512, 4), (2048, 16))
DIL_SPAN = 128
RMS_EPS = 1e-6
NEG_INF = -1e30
SB_UNDERFLOW_LOG = -104.0
QK_SCALE = 1.0 / math.sqrt(HEAD_DIM)

ROW_TILE = 1024
COL_TILE = 1024
FF_TILE = 1792
SWIGLU_CHUNK = 256
MOE_ROW_TILE = 512
MOE_FF_TILE = 1792
SC_CORES = 2
SC_SUBCORES = 16
SC_CHUNK_ROWS = 64
NORM_CHUNK = 256
ATT_BLOCK = 256
ATT_Q_BLOCK = 512
ATT_K_PER_Q = ATT_Q_BLOCK // ATT_BLOCK
DIL_BLOCK = 128
DIL_ROW_TILE = 1024
VMEM_LIMIT = 56 * 1024 * 1024

F32 = jnp.float32
BF16 = jnp.bfloat16


def _params(*semantics):
    return pltpu.CompilerParams(dimension_semantics=semantics, vmem_limit_bytes=VMEM_LIMIT)


def _rms_normalize(x, gain):
    inv = lax.rsqrt(jnp.mean(x * x, axis=-1, keepdims=True) + RMS_EPS)
    return x * inv * gain


def _split_bf16(x, terms):
    parts = []
    for _ in range(terms):
        part = x.astype(BF16)
        parts.append(part)
        x = x - part.astype(F32)
    return parts


def _dot_split(a, b):
    a_hi, a_lo = _split_bf16(a, 2)
    b_hi, b_lo = _split_bf16(b, 2)
    return (jnp.dot(a_hi, b_hi, preferred_element_type=F32)
            + (jnp.dot(a_hi, b_lo, preferred_element_type=F32)
               + jnp.dot(a_lo, b_hi, preferred_element_type=F32)))


def _softplus(z):
    return jnp.maximum(z, 0.0) + jnp.log(1.0 + jnp.exp(-jnp.abs(z)))


def _proj_chunk(h_scr, w_ref, cs_ref, bd_ref, sl, head_normed):
    a = jnp.dot(h_scr[...], w_ref[:, sl], preferred_element_type=F32)
    if head_normed:
        ss = jnp.dot((a * a).astype(BF16), bd_ref[...], preferred_element_type=F32)
        a = a * lax.rsqrt(ss * (1.0 / HEAD_DIM) + RMS_EPS)
    return a * cs_ref[:, sl]


def _norm_proj_body(x_ref, g_ref, w_ref, cs_ref, bd_ref, o_ref, h_scr, *, n_norm):
    j = pl.program_id(1)

    @pl.when(j == 0)
    def _():
        h_scr[...] = _rms_normalize(x_ref[...], g_ref[...]).astype(BF16)

    def run(head_normed):
        for c in range(COL_TILE // NORM_CHUNK):
            sl = slice(c * NORM_CHUNK, (c + 1) * NORM_CHUNK)
            o_ref[:, sl] = _proj_chunk(h_scr, w_ref, cs_ref, bd_ref, sl, head_normed).astype(o_ref.dtype)

    if n_norm == 0:
        run(False)
    else:
        pl.when(j < n_norm)(functools.partial(run, True))
        pl.when(j >= n_norm)(functools.partial(run, False))


def _norm_proj(x, gain, w, layer, col_scale, n_norm):
    t, d = x.shape
    n = w.shape[2]
    head_id = jnp.arange(NORM_CHUNK) // HEAD_DIM
    block_diag = (head_id[:, None] == head_id[None, :]).astype(BF16)
    return pl.pallas_call(
        functools.partial(_norm_proj_body, n_norm=n_norm),
        out_shape=jax.ShapeDtypeStruct((t, n), BF16),
        grid=(t // ROW_TILE, n // COL_TILE),
        in_specs=[
            pl.BlockSpec((ROW_TILE, d), lambda i, j: (i, 0)),
            pl.BlockSpec((1, d), lambda i, j: (0, 0)),
            pl.BlockSpec((None, d, COL_TILE), lambda i, j: (layer, 0, j)),
            pl.BlockSpec((1, COL_TILE), lambda i, j: (0, j)),
            pl.BlockSpec((NORM_CHUNK, NORM_CHUNK), lambda i, j: (0, 0)),
        ],
        out_specs=pl.BlockSpec((ROW_TILE, COL_TILE), lambda i, j: (i, j)),
        scratch_shapes=[pltpu.VMEM((ROW_TILE, d), BF16)],
        compiler_params=_params("parallel", "arbitrary"),
        name="norm_proj",
    )(x, gain.reshape(1, d), w, col_scale.reshape(1, n).astype(F32), block_diag)


def _swiglu_hidden(h, wg, wu):
    g = jnp.dot(h, wg, preferred_element_type=F32)
    u = jnp.dot(h, wu, preferred_element_type=F32)
    return g * (1.0 / (1.0 + jnp.exp(-g))) * u


def _swiglu_down(h, wg_ref, wu_ref, wd_ref):
    y = None
    for c in range(wg_ref.shape[1] // SWIGLU_CHUNK):
        cols = slice(c * SWIGLU_CHUNK, (c + 1) * SWIGLU_CHUNK)
        a = _swiglu_hidden(h, wg_ref[:, cols], wu_ref[:, cols]).astype(BF16)
        part = jnp.dot(a, wd_ref[cols, :], preferred_element_type=F32)
        y = part if y is None else y + part
    return y


def _ffn_body(x_ref, o_ref, wo_ref, g_ref, wg_ref, wu_ref, wd_ref, y_ref, h_scr):
    f = pl.program_id(1)

    @pl.when(f == 0)
    def _():
        x = x_ref[...] + jnp.dot(o_ref[...], wo_ref[...], preferred_element_type=F32)
        h_scr[...] = _rms_normalize(x, g_ref[...]).astype(BF16)
        y_ref[...] = x

    y_ref[...] += _swiglu_down(h_scr[...], wg_ref, wu_ref, wd_ref)


def _mix_ffn_residual(x, o, w_o, mix_layer, gain, w_gate, w_up, w_down, layer):
    t, d = x.shape
    ff = w_gate.shape[2]
    return pl.pallas_call(
        _ffn_body,
        out_shape=jax.ShapeDtypeStruct((t, d), F32),
        grid=(t // ROW_TILE, ff // FF_TILE),
        in_specs=[
            pl.BlockSpec((ROW_TILE, d), lambda i, f: (i, 0)),
            pl.BlockSpec((ROW_TILE, d), lambda i, f: (i, 0)),
            pl.BlockSpec((None, d, d), lambda i, f: (mix_layer, 0, 0)),
            pl.BlockSpec((1, d), lambda i, f: (0, 0)),
            pl.BlockSpec((None, d, FF_TILE), lambda i, f: (layer, 0, f)),
            pl.BlockSpec((None, d, FF_TILE), lambda i, f: (layer, 0, f)),
            pl.BlockSpec((None, FF_TILE, d), lambda i, f: (layer, f, 0)),
        ],
        out_specs=pl.BlockSpec((ROW_TILE, d), lambda i, f: (i, 0)),
        scratch_shapes=[pltpu.VMEM((ROW_TILE, d), BF16)],
        compiler_params=_params("parallel", "arbitrary"),
        name="ffn",
    )(x, o, w_o, gain.reshape(1, d), w_gate, w_up, w_down)


def _pack_bf16_pairs(x):
    half = x.shape[1] // 2
    bits = pltpu.bitcast(x.astype(BF16).astype(F32), jnp.int32)
    return bits[:, :half] | lax.shift_right_logical(bits[:, half:], jnp.int32(16))


def _unpack_bf16_pairs(p):
    left = pltpu.bitcast(p & jnp.int32(-65536), F32)
    right = pltpu.bitcast(lax.shift_left(p, jnp.int32(16)), F32)
    return jnp.concatenate([left, right], axis=1).astype(BF16)


def _router_body(x_ref, o_ref, wo_ref, g_ref, r_ref, tri_ref, gates_ref, rank_ref, h_ref, x1_ref, count_scr):
    @pl.when(pl.program_id(0) == 0)
    def _():
        count_scr[...] = jnp.zeros_like(count_scr)

    x1 = x_ref[...] + jnp.dot(o_ref[...], wo_ref[...], preferred_element_type=F32)
    x1_ref[...] = x1
    h = _rms_normalize(x1, g_ref[...])
    h_ref[...] = _pack_bf16_pairs(h)
    logits = _dot_split(h, r_ref[...])
    lane = lax.broadcasted_iota(jnp.int32, logits.shape, 1).astype(F32)
    logits = jnp.where(lane < N_EXPERTS, logits, -jnp.inf)
    m1 = jnp.max(logits, axis=-1, keepdims=True)
    i1 = jnp.min(jnp.where(logits == m1, lane, float(LANES)), axis=-1, keepdims=True)
    rest = jnp.where(lane == i1, -jnp.inf, logits)
    m2 = jnp.max(rest, axis=-1, keepdims=True)
    i2 = jnp.min(jnp.where(rest == m2, lane, float(LANES)), axis=-1, keepdims=True)
    e = jnp.exp(m2 - m1)
    g1 = 1.0 / (1.0 + e)
    gates = jnp.where(lane == i1, g1, 0.0) + jnp.where(lane == i2, e * g1, 0.0)
    gates_ref[...] = gates.T[:N_EXPERTS, :]
    chosen = jnp.where((lane == i1) | (lane == i2), 1.0, 0.0)
    inclusive = jnp.dot(tri_ref[...], chosen.astype(BF16), preferred_element_type=F32)
    rank = jnp.where(chosen > 0.0, inclusive - 1.0 + count_scr[...], -1.0)
    rank_ref[...] = rank.T[:N_EXPERTS, :]
    count_scr[...] += inclusive[ROW_TILE - 1:ROW_TILE, :]


def _mix_router(x, o, w_o, mix_layer, gain, router):
    t, d = x.shape
    r = jnp.zeros((d, LANES), F32).at[:, :N_EXPERTS].set(router)
    idx = jnp.arange(ROW_TILE)
    tri = (idx[:, None] >= idx[None, :]).astype(BF16)
    rows = pl.BlockSpec((ROW_TILE, d), lambda i: (i, 0))
    return pl.pallas_call(
        _router_body,
        out_shape=(jax.ShapeDtypeStruct((N_EXPERTS, t), F32), jax.ShapeDtypeStruct((N_EXPERTS, t), F32),
                   jax.ShapeDtypeStruct((t, d // 2), jnp.int32), jax.ShapeDtypeStruct((t, d), F32)),
        grid=(t // ROW_TILE,),
        in_specs=[
            rows,
            rows,
            pl.BlockSpec((None, d, d), lambda i: (mix_layer, 0, 0)),
            pl.BlockSpec((1, d), lambda i: (0, 0)),
            pl.BlockSpec((d, LANES), lambda i: (0, 0)),
            pl.BlockSpec((ROW_TILE, ROW_TILE), lambda i: (0, 0)),
        ],
        out_specs=(pl.BlockSpec((N_EXPERTS, ROW_TILE), lambda i: (0, i)),
                   pl.BlockSpec((N_EXPERTS, ROW_TILE), lambda i: (0, i)),
                   pl.BlockSpec((ROW_TILE, d // 2), lambda i: (i, 0)),
                   rows),
        scratch_shapes=[pltpu.VMEM((1, LANES), F32)],
        compiler_params=_params("arbitrary"),
        name="router",
    )(x, o, w_o, gain.reshape(1, d), r, tri)


def _sc_mesh():
    return plsc.VectorSubcoreMesh(core_axis_name="core", subcore_axis_name="subcore",
                                  num_cores=SC_CORES, num_subcores=SC_SUBCORES)


def _sc_worker_base(per_worker):
    return (lax.axis_index("subcore") * SC_CORES + lax.axis_index("core")) * per_worker


def _sc_row_gather(table, idx):
    width = table.shape[1]
    n = idx.shape[0]
    per_worker = n // (SC_CORES * SC_SUBCORES)
    n_chunks = per_worker // SC_CHUNK_ROWS
    assert n == n_chunks * SC_CHUNK_ROWS * SC_CORES * SC_SUBCORES and n_chunks % 2 == 0

    @functools.partial(
        pl.kernel, mesh=_sc_mesh(), out_type=jax.ShapeDtypeStruct((n, width), table.dtype),
        scratch_types=[pltpu.VMEM((SC_CHUNK_ROWS,), jnp.int32), pltpu.VMEM((SC_CHUNK_ROWS,), jnp.int32),
                       pltpu.VMEM((SC_CHUNK_ROWS, width), table.dtype),
                       pltpu.VMEM((SC_CHUNK_ROWS, width), table.dtype),
                       pltpu.SemaphoreType.DMA, pltpu.SemaphoreType.DMA],
        name="sc_row_gather")
    def gather(table_hbm, idx_hbm, out_hbm, idx_a, idx_b, rows_a, rows_b, sem_a, sem_b):
        base = _sc_worker_base(per_worker)
        bufs = ((idx_a, rows_a, sem_a), (idx_b, rows_b, sem_b))

        def rows_of(c):
            return pl.ds(pl.multiple_of(base + c * SC_CHUNK_ROWS, SC_CHUNK_ROWS), SC_CHUNK_ROWS)

        def fetch(c, buf):
            idx_v, rows_v, sem = buf
            pltpu.sync_copy(idx_hbm.at[rows_of(c)], idx_v)
            return pltpu.make_async_copy(table_hbm.at[idx_v], rows_v, sem)

        fetch(0, bufs[0]).start()

        @pl.loop(0, n_chunks, step=2)
        def _(c):
            for b in range(2):
                idx_v, rows_v, sem = bufs[b]
                pltpu.make_async_copy(table_hbm.at[idx_v], rows_v, sem).wait()

                @pl.when(c + b + 1 < n_chunks)
                def _():
                    fetch(c + b + 1, bufs[1 - b]).start()

                pltpu.sync_copy(rows_v, out_hbm.at[rows_of(c + b)])

    return gather(table, idx)


def _sc_row_scatter_pair(rows, idx_lo, idx_hi, n_out):
    n, width = rows.shape
    per_worker = n // (SC_CORES * SC_SUBCORES)
    n_chunks = per_worker // SC_CHUNK_ROWS
    assert n == n_chunks * SC_CHUNK_ROWS * SC_CORES * SC_SUBCORES and n_chunks % 2 == 0
    index_scratch = pltpu.VMEM((SC_CHUNK_ROWS,), jnp.int32)
    rows_scratch = pltpu.VMEM((SC_CHUNK_ROWS, width), rows.dtype)

    @functools.partial(
        pl.kernel, mesh=_sc_mesh(), out_type=jax.ShapeDtypeStruct((n_out, width), rows.dtype),
        scratch_types=[index_scratch, index_scratch, rows_scratch, rows_scratch,
                       pltpu.SemaphoreType.DMA, pltpu.SemaphoreType.DMA],
        name="sc_row_scatter")
    def scatter(rows_hbm, lo_hbm, hi_hbm, out_hbm, lo_v, hi_v, rows_a, rows_b, sem_a, sem_b):
        base = _sc_worker_base(per_worker)
        bufs = ((rows_a, sem_a), (rows_b, sem_b))

        def rows_of(c):
            return pl.ds(pl.multiple_of(base + c * SC_CHUNK_ROWS, SC_CHUNK_ROWS), SC_CHUNK_ROWS)

        def load(c, buf):
            rows_v, sem = buf
            return pltpu.make_async_copy(rows_hbm.at[rows_of(c)], rows_v, sem)

        load(0, bufs[0]).start()

        @pl.loop(0, n_chunks, step=2)
        def _(c):
            for b in range(2):
                rows_v, _ = bufs[b]
                load(c + b, bufs[b]).wait()

                @pl.when(c + b + 1 < n_chunks)
                def _():
                    load(c + b + 1, bufs[1 - b]).start()

                pltpu.sync_copy(lo_hbm.at[rows_of(c + b)], lo_v)
                pltpu.sync_copy(hi_hbm.at[rows_of(c + b)], hi_v)
                pltpu.sync_copy(rows_v, out_hbm.at[lo_v])
                pltpu.sync_copy(rows_v, out_hbm.at[hi_v])

    return scatter(rows, idx_lo, idx_hi)


def _expert_ffn_body(te_ref, nu_ref, nv_ref, h_ref, wg_ref, wu_ref, wd_ref, y_ref, acc_scr):
    i = pl.program_id(0)
    f = pl.program_id(1)

    @pl.when(i < nu_ref[0])
    def _():
        row = lax.broadcasted_iota(jnp.int32, h_ref.shape, 0)
        packed = jnp.where(row < nv_ref[i], h_ref[...], 0)
        y = _swiglu_down(_unpack_bf16_pairs(packed), wg_ref, wu_ref, wd_ref)

        @pl.when(f == 0)
        def _():
            acc_scr[...] = y

        @pl.when(f > 0)
        def _():
            acc_scr[...] += y

        @pl.when(f == pl.num_programs(1) - 1)
        def _():
            y_ref[...] = _pack_bf16_pairs(acc_scr[...])


def _expert_ffn(h_sorted, tile_expert, n_used, tile_valid, w_gate, w_up, w_down, layer):
    rows, half = h_sorted.shape
    d = 2 * half
    ff = w_gate.shape[3]
    n_f = ff // MOE_FF_TILE

    def row_map(i, f, te, nu, nv):
        return (jnp.minimum(i, nu[0] - 1), 0)

    def col_step(i, f, nu):
        return jnp.where(i < nu[0], f, n_f - 1)

    grid_spec = pltpu.PrefetchScalarGridSpec(
        num_scalar_prefetch=3,
        grid=(rows // MOE_ROW_TILE, n_f),
        in_specs=[
            pl.BlockSpec((MOE_ROW_TILE, half), row_map),
            pl.BlockSpec((None, None, d, MOE_FF_TILE),
                         lambda i, f, te, nu, nv: (layer, te[i], 0, col_step(i, f, nu))),
            pl.BlockSpec((None, None, d, MOE_FF_TILE),
                         lambda i, f, te, nu, nv: (layer, te[i], 0, col_step(i, f, nu))),
            pl.BlockSpec((None, None, MOE_FF_TILE, d),
                         lambda i, f, te, nu, nv: (layer, te[i], col_step(i, f, nu), 0)),
        ],
        out_specs=pl.BlockSpec((MOE_ROW_TILE, half), row_map),
        scratch_shapes=[pltpu.VMEM((MOE_ROW_TILE, d), F32)],
    )
    return pl.pallas_call(
        _expert_ffn_body,
        out_shape=jax.ShapeDtypeStruct((rows, half), jnp.int32),
        grid_spec=grid_spec,
        compiler_params=_params("arbitrary", "arbitrary"),
        name="expert_ffn",
    )(tile_expert, n_used, tile_valid, h_sorted, w_gate, w_up, w_down)


def _combine_body(x_ref, y_ref, g_ref, o_ref):
    out = x_ref[...]
    for s in range(2):
        column = jnp.broadcast_to(g_ref[s:s + 1, :], (LANES, g_ref.shape[1])).T
        gate = jnp.concatenate([column] * (out.shape[1] // LANES), axis=1)
        out = out + _unpack_bf16_pairs(y_ref[s]).astype(F32) * gate
    o_ref[...] = out


def _combine_residual(x, y_pairs, gates2):
    t, d = x.shape
    return pl.pallas_call(
        _combine_body,
        out_shape=jax.ShapeDtypeStruct((t, d), F32),
        grid=(t // ROW_TILE,),
        in_specs=[
            pl.BlockSpec((ROW_TILE, d), lambda i: (i, 0)),
            pl.BlockSpec((2, ROW_TILE, d // 2), lambda i: (0, i, 0)),
            pl.BlockSpec((2, ROW_TILE), lambda i: (0, i)),
        ],
        out_specs=pl.BlockSpec((ROW_TILE, d), lambda i: (i, 0)),
        compiler_params=_params("parallel"),
        name="moe_combine",
    )(x, y_pairs, gates2)


def _mix_moe_residual(x, o, w_o, mix_layer, gain, router, w_gate, w_up, w_down, layer):
    t, d = x.shape
    gates, rank, h_packed, x = _mix_router(x, o, w_o, mix_layer, gain, router)
    rank8 = rank.astype(jnp.int32)
    chosen = rank8 >= 0
    counts = jnp.sum(chosen, axis=1, dtype=jnp.int32)
    padded = (counts + MOE_ROW_TILE - 1) // MOE_ROW_TILE * MOE_ROW_TILE
    ends = jnp.cumsum(padded)
    starts = ends - padded
    pos = starts[:, None] + rank8
    max_rows = 2 * t + N_EXPERTS * MOE_ROW_TILE
    pos_lo = jnp.min(jnp.where(chosen, pos, max_rows), axis=0)
    pos_hi = jnp.max(jnp.where(chosen, pos, -1), axis=0)
    gates2 = jnp.stack([jnp.sum(jnp.where(chosen & (pos == pos_lo[None, :]), gates, 0.0), axis=0),
                        jnp.sum(jnp.where(chosen & (pos == pos_hi[None, :]), gates, 0.0), axis=0)])
    n_tiles = max_rows // MOE_ROW_TILE
    n_used = (ends[-1] // MOE_ROW_TILE).astype(jnp.int32)
    tile_start = jnp.minimum(jnp.arange(n_tiles, dtype=jnp.int32), n_used - 1) * MOE_ROW_TILE
    tile_expert = jnp.sum(tile_start[:, None] >= ends[None, :], axis=1, dtype=jnp.int32)
    tile_valid = jnp.clip((starts + counts)[tile_expert] - tile_start, 0, MOE_ROW_TILE).astype(jnp.int32)

    h_sorted = _sc_row_scatter_pair(h_packed, pos_lo, pos_hi, max_rows)
    y_sorted = _expert_ffn(h_sorted, tile_expert, n_used.reshape(1), tile_valid, w_gate, w_up, w_down, layer)
    y_pairs = _sc_row_gather(y_sorted, jnp.concatenate([pos_lo, pos_hi])).reshape(2, t, d // 2)
    return _combine_residual(x, y_pairs, gates2)


def _split_head_pair(q):
    is_first = lax.broadcasted_iota(jnp.int32, (1, LANES), 1) < HEAD_DIM
    zero = jnp.zeros_like(q)
    return jnp.where(is_first, q, zero), jnp.where(is_first, zero, q)


def _merge_head_pair(first, second):
    is_first = lax.broadcasted_iota(jnp.int32, (1, LANES), 1) < HEAD_DIM
    return jnp.where(is_first, first, second)


def _qk(q, k):
    return lax.dot_general(q, k, (((1,), (1,)), ((), ())), preferred_element_type=F32)


def _transpose_values(v_ref, vt_scr):
    for c in range(v_ref.shape[0] // ATT_BLOCK):
        rows = slice(c * ATT_BLOCK, (c + 1) * ATT_BLOCK)
        vt_scr[:, rows] = v_ref[rows, :].astype(F32).T.astype(vt_scr.dtype)


def _sb_body(q_ref, k_ref, v_ref, u_ref, o_ref, vt_scr, acc_scr, carry_scr, z_scr, keep_scr, sum_scr):
    qi = pl.program_id(2)

    @pl.when(qi == 0)
    def _():
        _transpose_values(v_ref, vt_scr)

    upper = u_ref[...]
    acc_scr[...] = jnp.zeros_like(acc_scr)
    carry_scr[...] = jnp.zeros_like(carry_scr)
    key = lax.broadcasted_iota(jnp.int32, (ATT_BLOCK, ATT_BLOCK), 0)
    query = lax.broadcasted_iota(jnp.int32, (ATT_BLOCK, ATT_BLOCK), 1)
    strict = key < query
    qs = [_split_head_pair(q_ref[j * ATT_BLOCK:(j + 1) * ATT_BLOCK, :]) for j in range(ATT_K_PER_Q)]

    def process(j, kb, keep):
        start = pl.multiple_of(kb * ATT_BLOCK, ATT_BLOCK)
        k = k_ref[pl.ds(start, ATT_BLOCK), :]
        for hh in range(2):
            z = _qk(k, qs[j][hh])
            log_beta = z - _softplus(z)
            log_keep = log_beta - z
            if keep is not None:
                log_keep = jnp.where(keep, log_keep, 0.0)
            remain = jnp.dot(upper, log_keep.astype(BF16), preferred_element_type=F32)
            w = jnp.exp(log_beta + remain + carry_scr[j, hh])
            if keep is not None:
                w = jnp.where(keep, w, 0.0)
            carry_scr[j, hh] += jnp.sum(log_keep, axis=0, keepdims=True)
            vt = vt_scr[pl.ds(hh * HEAD_DIM, HEAD_DIM), pl.ds(start, ATT_BLOCK)]
            acc_scr[j, hh] += jnp.dot(vt, w.astype(BF16), preferred_element_type=F32)

    units = []
    for j in range(ATT_K_PER_Q):
        g = qi * ATT_K_PER_Q + j
        has_previous = None if j > 0 else jnp.broadcast_to(g > 0, strict.shape)
        for kb, keep in ((g, strict), (jnp.maximum(g - 1, 0), has_previous)):
            for hh in range(2):
                units.append((j, hh, pl.multiple_of(kb * ATT_BLOCK, ATT_BLOCK), keep))

    for u, (j, hh, start, keep) in enumerate(units):
        z_scr[u] = _qk(k_ref[pl.ds(start, ATT_BLOCK), :], qs[j][hh])

    for u, (j, hh, start, keep) in enumerate(units):
        z = z_scr[u]
        log_beta = z - _softplus(z)
        log_keep = log_beta - z
        if keep is not None:
            log_keep = jnp.where(keep, log_keep, 0.0)
        z_scr[u] = log_beta
        keep_scr[u] = log_keep.astype(BF16)
        sum_scr[u] = jnp.sum(log_keep, axis=0, keepdims=True)

    for u in range(len(units)):
        z_scr[u] += jnp.dot(upper, keep_scr[u], preferred_element_type=F32)

    for u, (j, hh, start, keep) in enumerate(units):
        diagonal = u % 4 < 2
        log_w = z_scr[u] if diagonal else z_scr[u] + sum_scr[u - 2]
        w = jnp.exp(log_w)
        if keep is not None:
            w = jnp.where(keep, w, 0.0)
        keep_scr[u] = w.astype(BF16)

    for u, (j, hh, start, keep) in enumerate(units):
        vt = vt_scr[pl.ds(hh * HEAD_DIM, HEAD_DIM), pl.ds(start, ATT_BLOCK)]
        acc_scr[j, hh] += jnp.dot(vt, keep_scr[u], preferred_element_type=F32)
        carry_scr[j, hh] += sum_scr[u]

    for j in range(ATT_K_PER_Q):
        def more(kb, j=j):
            return (kb >= 0) & (jnp.max(carry_scr[j]) > SB_UNDERFLOW_LOG)

        def step(kb, j=j):
            process(j, kb, None)
            return kb - 1

        lax.while_loop(more, step, qi * ATT_K_PER_Q + j - 2)

    for j in range(ATT_K_PER_Q):
        out_t = jnp.concatenate([acc_scr[j, 0], acc_scr[j, 1]], axis=0)
        o_ref[j * ATT_BLOCK:(j + 1) * ATT_BLOCK, :] = out_t.T.astype(o_ref.dtype)


def _sb_attention(proj, batch, seq):
    t = proj.shape[0]
    nq = seq // ATT_Q_BLOCK
    idx = jnp.arange(ATT_BLOCK)
    upper = (idx[None, :] > idx[:, None]).astype(BF16)
    return pl.pallas_call(
        _sb_body,
        out_shape=jax.ShapeDtypeStruct((t, D_MODEL), BF16),
        grid=(batch, HEAD_PAIRS, nq),
        in_specs=[
            pl.BlockSpec((ATT_Q_BLOCK, LANES), lambda b, p, i: (b * nq + i, p)),
            pl.BlockSpec((seq, LANES), lambda b, p, i: (b, HEAD_PAIRS + p)),
            pl.BlockSpec((seq, LANES), lambda b, p, i: (b, 2 * HEAD_PAIRS + p)),
            pl.BlockSpec((ATT_BLOCK, ATT_BLOCK), lambda b, p, i: (0, 0)),
        ],
        out_specs=pl.BlockSpec((ATT_Q_BLOCK, LANES), lambda b, p, i: (b * nq + i, p)),
        scratch_shapes=[
            pltpu.VMEM((LANES, seq), BF16),
            pltpu.VMEM((ATT_K_PER_Q, 2, HEAD_DIM, ATT_BLOCK), F32),
            pltpu.VMEM((ATT_K_PER_Q, 2, 1, ATT_BLOCK), F32),
            pltpu.VMEM((4 * ATT_K_PER_Q, ATT_BLOCK, ATT_BLOCK), F32),
            pltpu.VMEM((4 * ATT_K_PER_Q, ATT_BLOCK, ATT_BLOCK), BF16),
            pltpu.VMEM((4 * ATT_K_PER_Q, 1, ATT_BLOCK), F32),
        ],
        compiler_params=_params("parallel", "parallel", "arbitrary"),
        name="sb_attention",
    )(proj, proj, proj, upper)


def _fox_gate_body(x_ref, g_ref, w_ref, b_ref, tri_ref, c_ref):
    h = _rms_normalize(x_ref[...], g_ref[...])
    logits = _dot_split(h, w_ref[...])
    log_f = -_softplus(-(logits + b_ref[...]))
    seq = log_f.shape[0]
    carry = jnp.zeros((1, LANES), F32)
    for blk in range(seq // ATT_BLOCK):
        rows = slice(blk * ATT_BLOCK, (blk + 1) * ATT_BLOCK)
        c = carry
        for part in _split_bf16(log_f[rows], 3):
            c = c + jnp.dot(tri_ref[...], part, preferred_element_type=F32)
        c_ref[rows, :] = c
        carry = c[ATT_BLOCK - 1:ATT_BLOCK, :]


def _fox_cum_log_forget(x, gain, w_gate, b_gate, batch, seq):
    t, d = x.shape
    w = jnp.zeros((d, LANES), F32).at[:, :N_HEADS].set(w_gate)
    b = jnp.zeros((1, LANES), F32).at[0, :N_HEADS].set(b_gate)
    idx = jnp.arange(ATT_BLOCK)
    tri = (idx[:, None] >= idx[None, :]).astype(BF16)
    return pl.pallas_call(
        _fox_gate_body,
        out_shape=jax.ShapeDtypeStruct((t, LANES), F32),
        grid=(batch,),
        in_specs=[
            pl.BlockSpec((seq, d), lambda i: (i, 0)),
            pl.BlockSpec((1, d), lambda i: (0, 0)),
            pl.BlockSpec((d, LANES), lambda i: (0, 0)),
            pl.BlockSpec((1, LANES), lambda i: (0, 0)),
            pl.BlockSpec((ATT_BLOCK, ATT_BLOCK), lambda i: (0, 0)),
        ],
        out_specs=pl.BlockSpec((seq, LANES), lambda i: (i, 0)),
        compiler_params=_params("parallel"),
        name="fox_gate",
    )(x, gain.reshape(1, d), w, b, tri)


def _fox_body(q_ref, k_ref, v_ref, cq_ref, ck_ref, o_ref, vt_scr, ckb_scr, a_scr, p_scr):
    qi = pl.program_id(2)

    @pl.when(qi == 0)
    def _():
        _transpose_values(v_ref, vt_scr)
        for hh in range(2):
            for c in range(ckb_scr.shape[1] // ATT_BLOCK):
                rows = slice(c * ATT_BLOCK, (c + 1) * ATT_BLOCK)
                ckb_scr[hh, rows, :] = jnp.broadcast_to(ck_ref[0, hh, :, rows], (LANES, ATT_BLOCK)).T

    qs = _split_head_pair(q_ref[...])
    key = lax.broadcasted_iota(jnp.int32, (ATT_BLOCK, ATT_Q_BLOCK), 0)
    query = lax.broadcasted_iota(jnp.int32, (ATT_BLOCK, ATT_Q_BLOCK), 1)
    row_shape = (1, ATT_Q_BLOCK)

    def scores(kb, tops, diagonal):
        start = kb * ATT_BLOCK
        k = k_ref[pl.ds(start, ATT_BLOCK), :]
        if diagonal:
            causal = key + (kb * ATT_BLOCK - qi * ATT_Q_BLOCK) <= query
        new_tops = []
        for hh in range(2):
            ck = ckb_scr[hh, pl.ds(start, ATT_BLOCK), :]
            a = _qk(k, qs[hh]) - jnp.concatenate([ck] * (ATT_Q_BLOCK // LANES), axis=1)
            if diagonal:
                a = jnp.where(causal, a, NEG_INF)
            a_scr[hh, kb] = a
            new_tops.append(jnp.maximum(tops[hh], jnp.max(a, axis=0, keepdims=True)))
        return tuple(new_tops)

    def sweep(n_blocks):
        tops = (jnp.full(row_shape, NEG_INF, F32),) * 2
        for kb in range(n_blocks):
            tops = scores(kb, tops, kb >= n_blocks - ATT_K_PER_Q)
        shifts = tuple(cq_ref[0, hh] - (tops[hh] + cq_ref[0, hh]) for hh in range(2))
        sums = [jnp.zeros(row_shape, F32)] * 2
        for kb in range(n_blocks):
            for hh in range(2):
                p = jnp.exp(a_scr[hh, kb] + shifts[hh])
                p_scr[hh, kb] = p.astype(BF16)
                sums[hh] = sums[hh] + jnp.sum(p, axis=0, keepdims=True)
        accs = [jnp.zeros((HEAD_DIM, ATT_Q_BLOCK), F32)] * 2
        for kb in range(n_blocks):
            for hh in range(2):
                vt = vt_scr[hh * HEAD_DIM:(hh + 1) * HEAD_DIM, kb * ATT_BLOCK:(kb + 1) * ATT_BLOCK]
                accs[hh] = accs[hh] + jnp.dot(vt, p_scr[hh, kb], preferred_element_type=F32)
        out_t = jnp.concatenate([accs[0] / sums[0], accs[1] / sums[1]], axis=0)
        o_ref[...] = out_t.T.astype(o_ref.dtype)

    for q in range(k_ref.shape[0] // ATT_Q_BLOCK):
        pl.when(qi == q)(functools.partial(sweep, (q + 1) * ATT_K_PER_Q))


def _fox_attention(proj, cum, batch, seq):
    t = proj.shape[0]
    nq = seq // ATT_Q_BLOCK
    cum_h = cum[:, :N_HEADS].reshape(batch, seq, N_HEADS).transpose(0, 2, 1)
    cum_rows = cum_h.reshape(batch, N_HEADS, 1, seq)
    return pl.pallas_call(
        _fox_body,
        out_shape=jax.ShapeDtypeStruct((t, D_MODEL), BF16),
        grid=(batch, HEAD_PAIRS, nq),
        in_specs=[
            pl.BlockSpec((ATT_Q_BLOCK, LANES), lambda b, p, i: (b * nq + i, p)),
            pl.BlockSpec((seq, LANES), lambda b, p, i: (b, HEAD_PAIRS + p)),
            pl.BlockSpec((seq, LANES), lambda b, p, i: (b, 2 * HEAD_PAIRS + p)),
            pl.BlockSpec((1, 2, 1, ATT_Q_BLOCK), lambda b, p, i: (b, p, 0, i)),
            pl.BlockSpec((1, 2, 1, seq), lambda b, p, i: (b, p, 0, 0)),
        ],
        out_specs=pl.BlockSpec((ATT_Q_BLOCK, LANES), lambda b, p, i: (b * nq + i, p)),
        scratch_shapes=[
            pltpu.VMEM((LANES, seq), BF16),
            pltpu.VMEM((2, seq, LANES), F32),
            pltpu.VMEM((2, seq // ATT_BLOCK, ATT_BLOCK, ATT_Q_BLOCK), F32),
            pltpu.VMEM((2, seq // ATT_BLOCK, ATT_BLOCK, ATT_Q_BLOCK), BF16),
        ],
        compiler_params=_params("parallel", "parallel", "arbitrary"),
        name="fox_attention",
    )(proj, proj, proj, cum_rows, cum_rows)


def _band_body(q_ref, k_ref, v_ref, bias_ref, o_ref, s_ref, vt_scr, logit_scr, p_scr, inv_scr, *, sub_len):
    seq = v_ref.shape[0]
    n_blocks = seq // DIL_BLOCK
    _transpose_values(v_ref, vt_scr)

    def key_rows(n):
        first = (n * DIL_BLOCK) % sub_len == 0
        return first, slice((n if first else n - 1) * DIL_BLOCK, (n + 1) * DIL_BLOCK)

    for n in range(n_blocks):
        first, k_rows = key_rows(n)
        qs = _split_head_pair(q_ref[n * DIL_BLOCK:(n + 1) * DIL_BLOCK, :])
        k = k_ref[k_rows, :]
        for hh in range(2):
            bias = bias_ref[hh, DIL_BLOCK:, :] if first else bias_ref[hh]
            logit_scr[2 * n + hh, :k.shape[0], :] = _qk(k, qs[hh]) + bias

    for n in range(n_blocks):
        first, k_rows = key_rows(n)
        n_keys = k_rows.stop - k_rows.start
        for hh in range(2):
            logits = logit_scr[2 * n + hh, :n_keys, :]
            m = jnp.max(logits, axis=0, keepdims=True)
            p = jnp.exp(logits - m)
            l = jnp.sum(p, axis=0, keepdims=True)
            p_scr[2 * n + hh, :n_keys, :] = p.astype(BF16)
            inv_scr[2 * n + hh] = 1.0 / l
            s_ref[0, 0, hh:hh + 1, n * DIL_BLOCK:(n + 1) * DIL_BLOCK] = m + jnp.log(l)

    for n in range(n_blocks):
        first, k_rows = key_rows(n)
        n_keys = k_rows.stop - k_rows.start
        outs = []
        for hh in range(2):
            vt = vt_scr[hh * HEAD_DIM:(hh + 1) * HEAD_DIM, k_rows]
            o = jnp.dot(vt, p_scr[2 * n + hh, :n_keys, :], preferred_element_type=F32)
            outs.append(o * inv_scr[2 * n + hh])
        o_ref[n * DIL_BLOCK:(n + 1) * DIL_BLOCK, :] = jnp.concatenate(outs, axis=0).T.astype(o_ref.dtype)


def _band_attention(qk, v, bias_t, batch, seq, sub_len):
    t = v.shape[0]
    return pl.pallas_call(
        functools.partial(_band_body, sub_len=sub_len),
        out_shape=(jax.ShapeDtypeStruct((t, D_MODEL), BF16),
                   jax.ShapeDtypeStruct((batch, HEAD_PAIRS, 2, seq), F32)),
        grid=(batch, HEAD_PAIRS),
        in_specs=[
            pl.BlockSpec((None, seq, LANES), lambda b, p: (0, b, p)),
            pl.BlockSpec((None, seq, LANES), lambda b, p: (1, b, p)),
            pl.BlockSpec((seq, LANES), lambda b, p: (b, p)),
            pl.BlockSpec((2, 2 * DIL_BLOCK, DIL_BLOCK), lambda b, p: (p, 0, 0)),
        ],
        out_specs=(pl.BlockSpec((seq, LANES), lambda b, p: (b, p)),
                   pl.BlockSpec((1, 1, 2, seq), lambda b, p: (b, p, 0, 0))),
        scratch_shapes=[pltpu.VMEM((LANES, seq), BF16),
                        pltpu.VMEM((2 * seq // DIL_BLOCK, 2 * DIL_BLOCK, DIL_BLOCK), F32),
                        pltpu.VMEM((2 * seq // DIL_BLOCK, 2 * DIL_BLOCK, DIL_BLOCK), BF16),
                        pltpu.VMEM((2 * seq // DIL_BLOCK, 1, DIL_BLOCK), F32)],
        compiler_params=_params("parallel", "parallel"),
        name="band_attention",
    )(qk, qk, v, bias_t)


def _softmax_merge_body(o1_ref, o2_ref, o3_ref, s_ref, e_ref, o_ref):
    s = s_ref[...]
    groups = [s, pltpu.roll(s, LANES - N_HEADS, axis=1), pltpu.roll(s, LANES - 2 * N_HEADS, axis=1)]
    top = jnp.maximum(jnp.maximum(groups[0], groups[1]), groups[2])
    weights = [jnp.exp(g - top) for g in groups]
    inv = 1.0 / (weights[0] + weights[1] + weights[2])
    out = jnp.zeros(o_ref.shape, F32)
    for w, part in zip(weights, (o1_ref, o2_ref, o3_ref)):
        spread = jnp.dot((w * inv).astype(BF16), e_ref[...], preferred_element_type=F32)
        out = out + spread * part[...].astype(F32)
    o_ref[...] = out.astype(o_ref.dtype)


def _softmax_merge(outs, s_all):
    t, d = outs[0].shape
    head_of = jnp.arange(d) // HEAD_DIM
    expand = (jnp.arange(LANES)[:, None] == head_of[None, :]).astype(BF16)
    rows = pl.BlockSpec((ROW_TILE, d), lambda i: (i, 0))
    return pl.pallas_call(
        _softmax_merge_body,
        out_shape=jax.ShapeDtypeStruct((t, d), BF16),
        grid=(t // ROW_TILE,),
        in_specs=[rows, rows, rows,
                  pl.BlockSpec((ROW_TILE, LANES), lambda i: (i, 0)),
                  pl.BlockSpec((LANES, d), lambda i: (0, 0))],
        out_specs=rows,
        compiler_params=_params("parallel"),
        name="softmax_merge",
    )(*outs, s_all, expand)


def _t5_causal_bucket(distance):
    max_exact = N_REL_BUCKETS // 2
    d = jnp.maximum(distance, 1).astype(F32)
    log_b = max_exact + (jnp.log(d / max_exact) / math.log(REL_MAX_DISTANCE / max_exact)
                         * (N_REL_BUCKETS - max_exact)).astype(jnp.int32)
    log_b = jnp.minimum(log_b, N_REL_BUCKETS - 1)
    return jnp.where(distance < max_exact, distance, log_b)


def _dilated_bias(rel_bias):
    kj = jnp.arange(2 * DIL_BLOCK, dtype=jnp.int32)
    qi = jnp.arange(DIL_BLOCK, dtype=jnp.int32)
    delta = qi[None, :] + DIL_BLOCK - kj[:, None]
    in_band = (delta >= 0) & (delta <= DIL_SPAN)
    buckets = jnp.stack([_t5_causal_bucket(jnp.maximum(delta, 0) * dil) for _, dil in DILATED_PAIRS])
    one_hot = (buckets[..., None] == jnp.arange(N_REL_BUCKETS)).astype(F32)
    bias = jnp.einsum("gkqb,bh->ghkq", one_hot, rel_bias.astype(F32), precision=lax.Precision.HIGHEST)
    return jnp.where(in_band[None, None], bias, NEG_INF)


def _dilated_proj_body(x_ref, g_ref, w_ref, cs_ref, bd_ref, *rest):
    n_groups = len(DILATED_PAIRS)
    qk_refs, v_refs = rest[:n_groups], rest[n_groups:2 * n_groups]
    h_scr, res_scr = rest[2 * n_groups:]
    j = pl.program_id(1)

    @pl.when(j == 0)
    def _():
        h_scr[...] = _rms_normalize(x_ref[...], g_ref[...]).astype(BF16)

    def write_classes(ref, dil, slab):
        rows = res_scr.shape[1] // dil
        lanes = slice(slab * LANES, (slab + 1) * LANES)
        if dil == 1:
            ref[:, lanes] = res_scr[slab].astype(ref.dtype)
        else:
            for r in range(dil):
                ref[r, :, lanes] = res_scr[slab, pl.ds(r, rows, stride=dil), :].astype(ref.dtype)

    def run(targets, head_normed):
        for c in range(COL_TILE // NORM_CHUNK):
            sl = slice(c * NORM_CHUNK, (c + 1) * NORM_CHUNK)
            res = _proj_chunk(h_scr, w_ref, cs_ref, bd_ref, sl, head_normed)
            for s in range(NORM_CHUNK // LANES):
                slab = c * (NORM_CHUNK // LANES) + s
                res_scr[slab] = res[:, s * LANES:(s + 1) * LANES]
                for ref, dil in targets:
                    write_classes(ref, dil, slab)

    dilations = [dil for _, dil in DILATED_PAIRS]
    for g, dil in enumerate(dilations):
        pl.when(j // 2 == g)(functools.partial(run, [(qk_refs[g], dil)], True))
    pl.when(j == 2 * n_groups)(functools.partial(run, list(zip(v_refs, dilations)), False))


def _dilated_proj(x, gain, w, layer, col_scale, batch, seq):
    t, d = x.shape
    n = w.shape[2]
    n_groups = len(DILATED_PAIRS)
    tiles_per_seq = seq // DIL_ROW_TILE
    head_id = jnp.arange(NORM_CHUNK) // HEAD_DIM
    block_diag = (head_id[:, None] == head_id[None, :]).astype(BF16)
    out_shape, out_specs = [], []
    for kind in ("qk", "v"):
        for g, (_, dil) in enumerate(DILATED_PAIRS):
            rows = DIL_ROW_TILE // dil
            if kind == "qk":
                shape = (2, batch, dil, seq // dil, d)
                block = (None, None, dil, rows, d)
                index = lambda i, j, g=g: (jnp.clip(j - 2 * g, 0, 1), i // tiles_per_seq, 0, i % tiles_per_seq, 0)
            else:
                shape = (batch, dil, seq // dil, d)
                block = (None, dil, rows, d)
                index = lambda i, j: (i // tiles_per_seq, 0, i % tiles_per_seq, 0)
            if dil == 1:
                block = block[:-3] + (None,) + block[-2:]
            out_shape.append(jax.ShapeDtypeStruct(shape, BF16))
            out_specs.append(pl.BlockSpec(block, index))
    outs = pl.pallas_call(
        _dilated_proj_body,
        out_shape=tuple(out_shape),
        grid=(t // DIL_ROW_TILE, n // COL_TILE),
        in_specs=[
            pl.BlockSpec((DIL_ROW_TILE, d), lambda i, j: (i, 0)),
            pl.BlockSpec((1, d), lambda i, j: (0, 0)),
            pl.BlockSpec((None, d, COL_TILE), lambda i, j: (layer, 0, j)),
            pl.BlockSpec((1, COL_TILE), lambda i, j: (0, j)),
            pl.BlockSpec((NORM_CHUNK, NORM_CHUNK), lambda i, j: (0, 0)),
        ],
        out_specs=tuple(out_specs),
        scratch_shapes=[pltpu.VMEM((DIL_ROW_TILE, d), BF16),
                        pltpu.VMEM((COL_TILE // LANES, DIL_ROW_TILE, LANES), F32)],
        compiler_params=_params("parallel", "arbitrary"),
        name="dilated_proj",
    )(x, gain.reshape(1, d), w, col_scale.reshape(1, n).astype(F32), block_diag)
    qk = [o.reshape(2, t, d) for o in outs[:n_groups]]
    v = [o.reshape(t, d) for o in outs[n_groups:]]
    return qk, v


def _dilated_attention(qk, v, rel_bias, batch, seq):
    t = v[0].shape[0]
    n_groups = len(DILATED_PAIRS)
    bias_t = _dilated_bias(rel_bias)
    outs, stats = [], []
    for g, (_, dil) in enumerate(DILATED_PAIRS):
        sub_len = seq // dil
        o, s = _band_attention(qk[g], v[g], bias_t[g], batch, seq, sub_len)
        o = o.reshape(batch, dil, sub_len, D_MODEL).transpose(0, 2, 1, 3).reshape(t, D_MODEL)
        s = s.reshape(batch, HEAD_PAIRS, 2, dil, sub_len).swapaxes(3, 4)
        outs.append(o)
        stats.append(s.reshape(batch, N_HEADS, seq).transpose(0, 2, 1).reshape(t, N_HEADS))
    stats.append(jnp.zeros((t, LANES - n_groups * N_HEADS), F32))
    return _softmax_merge(outs, jnp.concatenate(stats, axis=1))


def _tile_heads(v):
    return jnp.tile(v.astype(F32), N_HEADS)


def _sb_mixer(x, gain, w_qkv, layer, batch, seq):
    ones = jnp.ones((D_MODEL,), F32)
    col_scale = jnp.concatenate([ones * QK_SCALE, ones, ones])
    proj = _norm_proj(x, gain, w_qkv, layer, col_scale, 0)
    return _sb_attention(proj, batch, seq)


def _dilated_mixer(x, gain, w_in, q_norm, k_norm, rel_bias, layer, batch, seq):
    scales = []
    for g in range(len(DILATED_PAIRS)):
        scales += [_tile_heads(q_norm[g]) * QK_SCALE, _tile_heads(k_norm[g])]
    scales.append(jnp.ones((D_MODEL,), F32))
    qk, v = _dilated_proj(x, gain, w_in, layer, jnp.concatenate(scales), batch, seq)
    return _dilated_attention(qk, v, rel_bias, batch, seq)


def _fox_mixer(x, gain, w_qkv, w_gate, b_f, q_norm, k_norm, layer, batch, seq):
    col_scale = jnp.concatenate([_tile_heads(q_norm) * QK_SCALE, _tile_heads(k_norm),
                                 jnp.ones((D_MODEL,), F32)])
    proj = _norm_proj(x, gain, w_qkv, layer, col_scale, 2)
    cum = _fox_cum_log_forget(x, gain, w_gate, b_f, batch, seq)
    return _fox_attention(proj, cum, batch, seq)


def kernel(x, sb_w_qkv, sb_w_o, dil_w_in, dil_q_norm, dil_k_norm, dil_w_o, fox_w_in, fox_b_f,
           fox_q_norm, fox_k_norm, fox_w_o, rel_bias, attn_norm, ffn_norm, mlp_w_gate, mlp_w_up,
           mlp_w_down, moe_router, moe_w_gate, moe_w_up, moe_w_down):
    batch, seq, d = x.shape
    depth = attn_norm.shape[0]
    sb_w_qkv, sb_w_o, dil_w_in, dil_w_o, fox_w_o, mlp_w_gate, mlp_w_up, mlp_w_down = (
        w.astype(BF16) for w in (sb_w_qkv, sb_w_o, dil_w_in, dil_w_o, fox_w_o, mlp_w_gate, mlp_w_up,
                                 mlp_w_down))
    moe_w_gate, moe_w_up, moe_w_down = (w.astype(BF16) for w in (moe_w_gate, moe_w_up, moe_w_down))
    fox_w_qkv = fox_w_in[:, :, :3 * D_MODEL].astype(BF16)
    h = x.reshape(batch * seq, d)
    for i in range(depth):
        kind, j = i % 3, i // 3
        if kind == 0:
            mixed, w_o = _sb_mixer(h, attn_norm[i], sb_w_qkv, j, batch, seq), sb_w_o
        elif kind == 1:
            mixed, w_o = _dilated_mixer(h, attn_norm[i], dil_w_in, dil_q_norm[j], dil_k_norm[j], rel_bias,
                                        j, batch, seq), dil_w_o
        else:
            mixed, w_o = _fox_mixer(h, attn_norm[i], fox_w_qkv, fox_w_in[j, :, 3 * D_MODEL:], fox_b_f[j],
                                    fox_q_norm[j], fox_k_norm[j], j, batch, seq), fox_w_o
        f = i // 2
        if i % 2 == 0:
            h = _mix_ffn_residual(h, mixed, w_o, j, ffn_norm[i], mlp_w_gate, mlp_w_up, mlp_w_down, f)
        else:
            h = _mix_moe_residual(h, mixed, w_o, j, ffn_norm[i], moe_router[f], moe_w_gate, moe_w_up,
                                  moe_w_down, f)
    return h.reshape(batch, seq, d)
```

```python
import functools
import math

import jax
import jax.numpy as jnp
from jax import lax
from jax.experimental import pallas as pl
from jax.experimental.pallas import tpu as pltpu
from jax.experimental.pallas import tpu_sc as plsc

D_MODEL = 1024
N_HEADS = 16
HEAD_DIM = 64
LANES = 128
HEAD_PAIRS = D_MODEL // LANES
N_EXPERTS = 8
N_REL_BUCKETS = 32
REL_MAX_DISTANCE = 2048
DILATED_PAIRS = ((128, 1), (512, 4), (2048, 16))
DIL_SPAN = 128
RMS_EPS = 1e-6
NEG_INF = -1e30
SB_UNDERFLOW_LOG = -104.0
QK_SCALE = 1.0 / math.sqrt(HEAD_DIM)

ROW_TILE = 1024
COL_TILE = 1024
FF_TILE = 1792
SWIGLU_CHUNK = 256
MOE_ROW_TILE = 512
MOE_FF_TILE = 1792
SC_CORES = 2
SC_SUBCORES = 16
SC_CHUNK_ROWS = 64
NORM_CHUNK = 256
ATT_BLOCK = 256
ATT_Q_BLOCK = 512
ATT_K_PER_Q = ATT_Q_BLOCK // ATT_BLOCK
DIL_BLOCK = 128
DIL_ROW_TILE = 1024
VMEM_LIMIT = 56 * 1024 * 1024

F32 = jnp.float32
BF16 = jnp.bfloat16


def _params(*semantics):
    return pltpu.CompilerParams(dimension_semantics=semantics, vmem_limit_bytes=VMEM_LIMIT)


def _rms_normalize(x, gain):
    inv = lax.rsqrt(jnp.mean(x * x, axis=-1, keepdims=True) + RMS_EPS)
    return x * inv * gain


def _split_bf16(x, terms):
    parts = []
    for _ in range(terms):
        part = x.astype(BF16)
        parts.append(part)
        x = x - part.astype(F32)
    return parts


def _dot_split(a, b):
    a_hi, a_lo = _split_bf16(a, 2)
    b_hi, b_lo = _split_bf16(b, 2)
    return (jnp.dot(a_hi, b_hi, preferred_element_type=F32)
            + (jnp.dot(a_hi, b_lo, preferred_element_type=F32)
               + jnp.dot(a_lo, b_hi, preferred_element_type=F32)))


def _softplus(z):
    return jnp.maximum(z, 0.0) + jnp.log(1.0 + jnp.exp(-jnp.abs(z)))


def _norm_proj_body(x_ref, g_ref, w_ref, cs_ref, bd_ref, o_ref, h_scr, *, n_norm):
    j = pl.program_id(1)

    @pl.when(j == 0)
    def _():
        h_scr[...] = _rms_normalize(x_ref[...], g_ref[...]).astype(BF16)

    acc = jnp.dot(h_scr[...], w_ref[...], preferred_element_type=F32)

    def plain():
        o_ref[...] = (acc * cs_ref[...]).astype(o_ref.dtype)

    def head_normed():
        for c in range(COL_TILE // NORM_CHUNK):
            sl = slice(c * NORM_CHUNK, (c + 1) * NORM_CHUNK)
            a = acc[:, sl]
            ss = jnp.dot((a * a).astype(BF16), bd_ref[...], preferred_element_type=F32)
            inv = lax.rsqrt(ss * (1.0 / HEAD_DIM) + RMS_EPS)
            o_ref[:, sl] = (a * inv * cs_ref[:, sl]).astype(o_ref.dtype)

    if n_norm == 0:
        plain()
    else:
        pl.when(j < n_norm)(head_normed)
        pl.when(j >= n_norm)(plain)


def _norm_proj(x, gain, w, layer, col_scale, n_norm):
    t, d = x.shape
    n = w.shape[2]
    head_id = jnp.arange(NORM_CHUNK) // HEAD_DIM
    block_diag = (head_id[:, None] == head_id[None, :]).astype(BF16)
    return pl.pallas_call(
        functools.partial(_norm_proj_body, n_norm=n_norm),
        out_shape=jax.ShapeDtypeStruct((t, n), BF16),
        grid=(t // ROW_TILE, n // COL_TILE),
        in_specs=[
            pl.BlockSpec((ROW_TILE, d), lambda i, j: (i, 0)),
            pl.BlockSpec((1, d), lambda i, j: (0, 0)),
            pl.BlockSpec((None, d, COL_TILE), lambda i, j: (layer, 0, j)),
            pl.BlockSpec((1, COL_TILE), lambda i, j: (0, j)),
            pl.BlockSpec((NORM_CHUNK, NORM_CHUNK), lambda i, j: (0, 0)),
        ],
        out_specs=pl.BlockSpec((ROW_TILE, COL_TILE), lambda i, j: (i, j)),
        scratch_shapes=[pltpu.VMEM((ROW_TILE, d), BF16)],
        compiler_params=_params("parallel", "arbitrary"),
        name="norm_proj",
    )(x, gain.reshape(1, d), w, col_scale.reshape(1, n).astype(F32), block_diag)


def _swiglu_hidden(h, wg, wu):
    g = jnp.dot(h, wg, preferred_element_type=F32)
    u = jnp.dot(h, wu, preferred_element_type=F32)
    return g * (1.0 / (1.0 + jnp.exp(-g))) * u


def _swiglu_down(h, wg_ref, wu_ref, wd_ref):
    y = None
    for c in range(wg_ref.shape[1] // SWIGLU_CHUNK):
        cols = slice(c * SWIGLU_CHUNK, (c + 1) * SWIGLU_CHUNK)
        a = _swiglu_hidden(h, wg_ref[:, cols], wu_ref[:, cols]).astype(BF16)
        part = jnp.dot(a, wd_ref[cols, :], preferred_element_type=F32)
        y = part if y is None else y + part
    return y


def _ffn_body(x_ref, o_ref, wo_ref, g_ref, wg_ref, wu_ref, wd_ref, y_ref, h_scr):
    f = pl.program_id(1)

    @pl.when(f == 0)
    def _():
        x = x_ref[...] + jnp.dot(o_ref[...], wo_ref[...], preferred_element_type=F32)
        h_scr[...] = _rms_normalize(x, g_ref[...]).astype(BF16)
        y_ref[...] = x

    y_ref[...] += _swiglu_down(h_scr[...], wg_ref, wu_ref, wd_ref)


def _mix_ffn_residual(x, o, w_o, mix_layer, gain, w_gate, w_up, w_down, layer):
    t, d = x.shape
    ff = w_gate.shape[2]
    return pl.pallas_call(
        _ffn_body,
        out_shape=jax.ShapeDtypeStruct((t, d), F32),
        grid=(t // ROW_TILE, ff // FF_TILE),
        in_specs=[
            pl.BlockSpec((ROW_TILE, d), lambda i, f: (i, 0)),
            pl.BlockSpec((ROW_TILE, d), lambda i, f: (i, 0)),
            pl.BlockSpec((None, d, d), lambda i, f: (mix_layer, 0, 0)),
            pl.BlockSpec((1, d), lambda i, f: (0, 0)),
            pl.BlockSpec((None, d, FF_TILE), lambda i, f: (layer, 0, f)),
            pl.BlockSpec((None, d, FF_TILE), lambda i, f: (layer, 0, f)),
            pl.BlockSpec((None, FF_TILE, d), lambda i, f: (layer, f, 0)),
        ],
        out_specs=pl.BlockSpec((ROW_TILE, d), lambda i, f: (i, 0)),
        scratch_shapes=[pltpu.VMEM((ROW_TILE, d), BF16)],
        compiler_params=_params("parallel", "arbitrary"),
        name="ffn",
    )(x, o, w_o, gain.reshape(1, d), w_gate, w_up, w_down)


def _pack_bf16_pairs(x):
    half = x.shape[1] // 2
    bits = pltpu.bitcast(x.astype(BF16).astype(F32), jnp.int32)
    return bits[:, :half] | lax.shift_right_logical(bits[:, half:], jnp.int32(16))


def _unpack_bf16_pairs(p):
    left = pltpu.bitcast(p & jnp.int32(-65536), F32)
    right = pltpu.bitcast(lax.shift_left(p, jnp.int32(16)), F32)
    return jnp.concatenate([left, right], axis=1).astype(BF16)


def _router_body(x_ref, o_ref, wo_ref, g_ref, r_ref, tri_ref, gates_ref, rank_ref, h_ref, x1_ref, count_scr):
    @pl.when(pl.program_id(0) == 0)
    def _():
        count_scr[...] = jnp.zeros_like(count_scr)

    x1 = x_ref[...] + jnp.dot(o_ref[...], wo_ref[...], preferred_element_type=F32)
    x1_ref[...] = x1
    h = _rms_normalize(x1, g_ref[...])
    h_ref[...] = _pack_bf16_pairs(h)
    logits = _dot_split(h, r_ref[...])
    lane = lax.broadcasted_iota(jnp.int32, logits.shape, 1).astype(F32)
    logits = jnp.where(lane < N_EXPERTS, logits, -jnp.inf)
    m1 = jnp.max(logits, axis=-1, keepdims=True)
    i1 = jnp.min(jnp.where(logits == m1, lane, float(LANES)), axis=-1, keepdims=True)
    rest = jnp.where(lane == i1, -jnp.inf, logits)
    m2 = jnp.max(rest, axis=-1, keepdims=True)
    i2 = jnp.min(jnp.where(rest == m2, lane, float(LANES)), axis=-1, keepdims=True)
    e = jnp.exp(m2 - m1)
    g1 = 1.0 / (1.0 + e)
    gates = jnp.where(lane == i1, g1, 0.0) + jnp.where(lane == i2, e * g1, 0.0)
    gates_ref[...] = gates.T[:N_EXPERTS, :]
    chosen = jnp.where((lane == i1) | (lane == i2), 1.0, 0.0)
    inclusive = jnp.dot(tri_ref[...], chosen.astype(BF16), preferred_element_type=F32)
    rank = jnp.where(chosen > 0.0, inclusive - 1.0 + count_scr[...], -1.0)
    rank_ref[...] = rank.T[:N_EXPERTS, :]
    count_scr[...] += inclusive[ROW_TILE - 1:ROW_TILE, :]


def _mix_router(x, o, w_o, mix_layer, gain, router):
    t, d = x.shape
    r = jnp.zeros((d, LANES), F32).at[:, :N_EXPERTS].set(router)
    idx = jnp.arange(ROW_TILE)
    tri = (idx[:, None] >= idx[None, :]).astype(BF16)
    rows = pl.BlockSpec((ROW_TILE, d), lambda i: (i, 0))
    return pl.pallas_call(
        _router_body,
        out_shape=(jax.ShapeDtypeStruct((N_EXPERTS, t), F32), jax.ShapeDtypeStruct((N_EXPERTS, t), F32),
                   jax.ShapeDtypeStruct((t, d // 2), jnp.int32), jax.ShapeDtypeStruct((t, d), F32)),
        grid=(t // ROW_TILE,),
        in_specs=[
            rows,
            rows,
            pl.BlockSpec((None, d, d), lambda i: (mix_layer, 0, 0)),
            pl.BlockSpec((1, d), lambda i: (0, 0)),
            pl.BlockSpec((d, LANES), lambda i: (0, 0)),
            pl.BlockSpec((ROW_TILE, ROW_TILE), lambda i: (0, 0)),
        ],
        out_specs=(pl.BlockSpec((N_EXPERTS, ROW_TILE), lambda i: (0, i)),
                   pl.BlockSpec((N_EXPERTS, ROW_TILE), lambda i: (0, i)),
                   pl.BlockSpec((ROW_TILE, d // 2), lambda i: (i, 0)),
                   rows),
        scratch_shapes=[pltpu.VMEM((1, LANES), F32)],
        compiler_params=_params("arbitrary"),
        name="router",
    )(x, o, w_o, gain.reshape(1, d), r, tri)


def _sc_mesh():
    return plsc.VectorSubcoreMesh(core_axis_name="core", subcore_axis_name="subcore",
                                  num_cores=SC_CORES, num_subcores=SC_SUBCORES)


def _sc_worker_base(per_worker):
    return (lax.axis_index("subcore") * SC_CORES + lax.axis_index("core")) * per_worker


def _sc_row_gather(table, idx):
    width = table.shape[1]
    n = idx.shape[0]
    per_worker = n // (SC_CORES * SC_SUBCORES)
    n_chunks = per_worker // SC_CHUNK_ROWS
    assert n == n_chunks * SC_CHUNK_ROWS * SC_CORES * SC_SUBCORES and n_chunks % 2 == 0

    @functools.partial(
        pl.kernel, mesh=_sc_mesh(), out_type=jax.ShapeDtypeStruct((n, width), table.dtype),
        scratch_types=[pltpu.VMEM((SC_CHUNK_ROWS,), jnp.int32), pltpu.VMEM((SC_CHUNK_ROWS,), jnp.int32),
                       pltpu.VMEM((SC_CHUNK_ROWS, width), table.dtype),
                       pltpu.VMEM((SC_CHUNK_ROWS, width), table.dtype),
                       pltpu.SemaphoreType.DMA, pltpu.SemaphoreType.DMA],
        name="sc_row_gather")
    def gather(table_hbm, idx_hbm, out_hbm, idx_a, idx_b, rows_a, rows_b, sem_a, sem_b):
        base = _sc_worker_base(per_worker)
        bufs = ((idx_a, rows_a, sem_a), (idx_b, rows_b, sem_b))

        def rows_of(c):
            return pl.ds(pl.multiple_of(base + c * SC_CHUNK_ROWS, SC_CHUNK_ROWS), SC_CHUNK_ROWS)

        def fetch(c, buf):
            idx_v, rows_v, sem = buf
            pltpu.sync_copy(idx_hbm.at[rows_of(c)], idx_v)
            return pltpu.make_async_copy(table_hbm.at[idx_v], rows_v, sem)

        fetch(0, bufs[0]).start()

        @pl.loop(0, n_chunks, step=2)
        def _(c):
            for b in range(2):
                idx_v, rows_v, sem = bufs[b]
                pltpu.make_async_copy(table_hbm.at[idx_v], rows_v, sem).wait()

                @pl.when(c + b + 1 < n_chunks)
                def _():
                    fetch(c + b + 1, bufs[1 - b]).start()

                pltpu.sync_copy(rows_v, out_hbm.at[rows_of(c + b)])

    return gather(table, idx)


def _sc_row_scatter_pair(rows, idx_lo, idx_hi, n_out):
    n, width = rows.shape
    per_worker = n // (SC_CORES * SC_SUBCORES)
    n_chunks = per_worker // SC_CHUNK_ROWS
    assert n == n_chunks * SC_CHUNK_ROWS * SC_CORES * SC_SUBCORES and n_chunks % 2 == 0
    index_scratch = pltpu.VMEM((SC_CHUNK_ROWS,), jnp.int32)
    rows_scratch = pltpu.VMEM((SC_CHUNK_ROWS, width), rows.dtype)

    @functools.partial(
        pl.kernel, mesh=_sc_mesh(), out_type=jax.ShapeDtypeStruct((n_out, width), rows.dtype),
        scratch_types=[index_scratch, index_scratch, rows_scratch, rows_scratch,
                       pltpu.SemaphoreType.DMA, pltpu.SemaphoreType.DMA],
        name="sc_row_scatter")
    def scatter(rows_hbm, lo_hbm, hi_hbm, out_hbm, lo_v, hi_v, rows_a, rows_b, sem_a, sem_b):
        base = _sc_worker_base(per_worker)
        bufs = ((rows_a, sem_a), (rows_b, sem_b))

        def rows_of(c):
            return pl.ds(pl.multiple_of(base + c * SC_CHUNK_ROWS, SC_CHUNK_ROWS), SC_CHUNK_ROWS)

        def load(c, buf):
            rows_v, sem = buf
            return pltpu.make_async_copy(rows_hbm.at[rows_of(c)], rows_v, sem)

        load(0, bufs[0]).start()

        @pl.loop(0, n_chunks, step=2)
        def _(c):
            for b in range(2):
                rows_v, _ = bufs[b]
                load(c + b, bufs[b]).wait()

                @pl.when(c + b + 1 < n_chunks)
                def _():
                    load(c + b + 1, bufs[1 - b]).start()

                pltpu.sync_copy(lo_hbm.at[rows_of(c + b)], lo_v)
                pltpu.sync_copy(hi_hbm.at[rows_of(c + b)], hi_v)
                pltpu.sync_copy(rows_v, out_hbm.at[lo_v])
                pltpu.sync_copy(rows_v, out_hbm.at[hi_v])

    return scatter(rows, idx_lo, idx_hi)


def _expert_ffn_body(te_ref, nu_ref, nv_ref, h_ref, wg_ref, wu_ref, wd_ref, y_ref, acc_scr):
    i = pl.program_id(0)
    f = pl.program_id(1)

    @pl.when(i < nu_ref[0])
    def _():
        row = lax.broadcasted_iota(jnp.int32, h_ref.shape, 0)
        packed = jnp.where(row < nv_ref[i], h_ref[...], 0)
        y = _swiglu_down(_unpack_bf16_pairs(packed), wg_ref, wu_ref, wd_ref)

        @pl.when(f == 0)
        def _():
            acc_scr[...] = y

        @pl.when(f > 0)
        def _():
            acc_scr[...] += y

        @pl.when(f == pl.num_programs(1) - 1)
        def _():
            y_ref[...] = _pack_bf16_pairs(acc_scr[...])


def _expert_ffn(h_sorted, tile_expert, n_used, tile_valid, w_gate, w_up, w_down, layer):
    rows, half = h_sorted.shape
    d = 2 * half
    ff = w_gate.shape[3]
    n_f = ff // MOE_FF_TILE

    def row_map(i, f, te, nu, nv):
        return (jnp.minimum(i, nu[0] - 1), 0)

    def col_step(i, f, nu):
        return jnp.where(i < nu[0], f, n_f - 1)

    grid_spec = pltpu.PrefetchScalarGridSpec(
        num_scalar_prefetch=3,
        grid=(rows // MOE_ROW_TILE, n_f),
        in_specs=[
            pl.BlockSpec((MOE_ROW_TILE, half), row_map),
            pl.BlockSpec((None, None, d, MOE_FF_TILE),
                         lambda i, f, te, nu, nv: (layer, te[i], 0, col_step(i, f, nu))),
            pl.BlockSpec((None, None, d, MOE_FF_TILE),
                         lambda i, f, te, nu, nv: (layer, te[i], 0, col_step(i, f, nu))),
            pl.BlockSpec((None, None, MOE_FF_TILE, d),
                         lambda i, f, te, nu, nv: (layer, te[i], col_step(i, f, nu), 0)),
        ],
        out_specs=pl.BlockSpec((MOE_ROW_TILE, half), row_map),
        scratch_shapes=[pltpu.VMEM((MOE_ROW_TILE, d), F32)],
    )
    return pl.pallas_call(
        _expert_ffn_body,
        out_shape=jax.ShapeDtypeStruct((rows, half), jnp.int32),
        grid_spec=grid_spec,
        compiler_params=_params("arbitrary", "arbitrary"),
        name="expert_ffn",
    )(tile_expert, n_used, tile_valid, h_sorted, w_gate, w_up, w_down)


def _combine_body(x_ref, y_ref, g_ref, o_ref):
    out = x_ref[...]
    for s in range(2):
        column = jnp.broadcast_to(g_ref[s:s + 1, :], (LANES, g_ref.shape[1])).T
        gate = jnp.concatenate([column] * (out.shape[1] // LANES), axis=1)
        out = out + _unpack_bf16_pairs(y_ref[s]).astype(F32) * gate
    o_ref[...] = out


def _combine_residual(x, y_pairs, gates2):
    t, d = x.shape
    return pl.pallas_call(
        _combine_body,
        out_shape=jax.ShapeDtypeStruct((t, d), F32),
        grid=(t // ROW_TILE,),
        in_specs=[
            pl.BlockSpec((ROW_TILE, d), lambda i: (i, 0)),
            pl.BlockSpec((2, ROW_TILE, d // 2), lambda i: (0, i, 0)),
            pl.BlockSpec((2, ROW_TILE), lambda i: (0, i)),
        ],
        out_specs=pl.BlockSpec((ROW_TILE, d), lambda i: (i, 0)),
        compiler_params=_params("parallel"),
        name="moe_combine",
    )(x, y_pairs, gates2)


def _mix_moe_residual(x, o, w_o, mix_layer, gain, router, w_gate, w_up, w_down, layer):
    t, d = x.shape
    gates, rank, h_packed, x = _mix_router(x, o, w_o, mix_layer, gain, router)
    rank8 = rank.astype(jnp.int32)
    chosen = rank8 >= 0
    counts = jnp.sum(chosen, axis=1, dtype=jnp.int32)
    padded = (counts + MOE_ROW_TILE - 1) // MOE_ROW_TILE * MOE_ROW_TILE
    ends = jnp.cumsum(padded)
    starts = ends - padded
    pos = starts[:, None] + rank8
    max_rows = 2 * t + N_EXPERTS * MOE_ROW_TILE
    pos_lo = jnp.min(jnp.where(chosen, pos, max_rows), axis=0)
    pos_hi = jnp.max(jnp.where(chosen, pos, -1), axis=0)
    gates2 = jnp.stack([jnp.sum(jnp.where(chosen & (pos == pos_lo[None, :]), gates, 0.0), axis=0),
                        jnp.sum(jnp.where(chosen & (pos == pos_hi[None, :]), gates, 0.0), axis=0)])
    n_tiles = max_rows // MOE_ROW_TILE
    n_used = (ends[-1] // MOE_ROW_TILE).astype(jnp.int32)
    tile_start = jnp.minimum(jnp.arange(n_tiles, dtype=jnp.int32), n_used - 1) * MOE_ROW_TILE
    tile_expert = jnp.sum(tile_start[:, None] >= ends[None, :], axis=1, dtype=jnp.int32)
    tile_valid = jnp.clip((starts + counts)[tile_expert] - tile_start, 0, MOE_ROW_TILE).astype(jnp.int32)

    h_sorted = _sc_row_scatter_pair(h_packed, pos_lo, pos_hi, max_rows)
    y_sorted = _expert_ffn(h_sorted, tile_expert, n_used.reshape(1), tile_valid, w_gate, w_up, w_down, layer)
    y_pairs = _sc_row_gather(y_sorted, jnp.concatenate([pos_lo, pos_hi])).reshape(2, t, d // 2)
    return _combine_residual(x, y_pairs, gates2)


def _split_head_pair(q):
    is_first = lax.broadcasted_iota(jnp.int32, (1, LANES), 1) < HEAD_DIM
    zero = jnp.zeros_like(q)
    return jnp.where(is_first, q, zero), jnp.where(is_first, zero, q)


def _qk(q, k):
    return lax.dot_general(q, k, (((1,), (1,)), ((), ())), preferred_element_type=F32)


def _transpose_values(v_ref, vt_scr):
    for c in range(v_ref.shape[0] // ATT_BLOCK):
        rows = slice(c * ATT_BLOCK, (c + 1) * ATT_BLOCK)
        vt_scr[:, rows] = v_ref[rows, :].astype(F32).T.astype(vt_scr.dtype)


def _sb_body(q_ref, k_ref, v_ref, u_ref, o_ref, vt_scr, acc_scr, carry_scr, z_scr, keep_scr, sum_scr):
    qi = pl.program_id(2)

    @pl.when(qi == 0)
    def _():
        _transpose_values(v_ref, vt_scr)

    upper = u_ref[...]
    acc_scr[...] = jnp.zeros_like(acc_scr)
    carry_scr[...] = jnp.zeros_like(carry_scr)
    key = lax.broadcasted_iota(jnp.int32, (ATT_BLOCK, ATT_BLOCK), 0)
    query = lax.broadcasted_iota(jnp.int32, (ATT_BLOCK, ATT_BLOCK), 1)
    strict = key < query
    qs = [_split_head_pair(q_ref[j * ATT_BLOCK:(j + 1) * ATT_BLOCK, :]) for j in range(ATT_K_PER_Q)]

    def process(j, kb, keep):
        start = pl.multiple_of(kb * ATT_BLOCK, ATT_BLOCK)
        k = k_ref[pl.ds(start, ATT_BLOCK), :]
        for hh in range(2):
            z = _qk(k, qs[j][hh])
            log_beta = z - _softplus(z)
            log_keep = log_beta - z
            if keep is not None:
                log_keep = jnp.where(keep, log_keep, 0.0)
            remain = jnp.dot(upper, log_keep.astype(BF16), preferred_element_type=F32)
            w = jnp.exp(log_beta + remain + carry_scr[j, hh])
            if keep is not None:
                w = jnp.where(keep, w, 0.0)
            carry_scr[j, hh] += jnp.sum(log_keep, axis=0, keepdims=True)
            vt = vt_scr[pl.ds(hh * HEAD_DIM, HEAD_DIM), pl.ds(start, ATT_BLOCK)]
            acc_scr[j, hh] += jnp.dot(vt, w.astype(BF16), preferred_element_type=F32)

    units = []
    for j in range(ATT_K_PER_Q):
        g = qi * ATT_K_PER_Q + j
        has_previous = None if j > 0 else jnp.broadcast_to(g > 0, strict.shape)
        for kb, keep in ((g, strict), (jnp.maximum(g - 1, 0), has_previous)):
            for hh in range(2):
                units.append((j, hh, pl.multiple_of(kb * ATT_BLOCK, ATT_BLOCK), keep))

    for u, (j, hh, start, keep) in enumerate(units):
        z_scr[u] = _qk(k_ref[pl.ds(start, ATT_BLOCK), :], qs[j][hh])

    for u, (j, hh, start, keep) in enumerate(units):
        z = z_scr[u]
        log_beta = z - _softplus(z)
        log_keep = log_beta - z
        if keep is not None:
            log_keep = jnp.where(keep, log_keep, 0.0)
        z_scr[u] = log_beta
        keep_scr[u] = log_keep.astype(BF16)
        sum_scr[u] = jnp.sum(log_keep, axis=0, keepdims=True)

    for u in range(len(units)):
        z_scr[u] += jnp.dot(upper, keep_scr[u], preferred_element_type=F32)

    for u, (j, hh, start, keep) in enumerate(units):
        diagonal = u % 4 < 2
        log_w = z_scr[u] if diagonal else z_scr[u] + sum_scr[u - 2]
        w = jnp.exp(log_w)
        if keep is not None:
            w = jnp.where(keep, w, 0.0)
        keep_scr[u] = w.astype(BF16)

    for u, (j, hh, start, keep) in enumerate(units):
        vt = vt_scr[pl.ds(hh * HEAD_DIM, HEAD_DIM), pl.ds(start, ATT_BLOCK)]
        acc_scr[j, hh] += jnp.dot(vt, keep_scr[u], preferred_element_type=F32)
        carry_scr[j, hh] += sum_scr[u]

    for j in range(ATT_K_PER_Q):
        def more(kb, j=j):
            return (kb >= 0) & (jnp.max(carry_scr[j]) > SB_UNDERFLOW_LOG)

        def step(kb, j=j):
            process(j, kb, None)
            return kb - 1

        lax.while_loop(more, step, qi * ATT_K_PER_Q + j - 2)

    for j in range(ATT_K_PER_Q):
        out_t = jnp.concatenate([acc_scr[j, 0], acc_scr[j, 1]], axis=0)
        o_ref[j * ATT_BLOCK:(j + 1) * ATT_BLOCK, :] = out_t.T.astype(o_ref.dtype)


def _sb_attention(proj, batch, seq):
    t = proj.shape[0]
    nq = seq // ATT_Q_BLOCK
    idx = jnp.arange(ATT_BLOCK)
    upper = (idx[None, :] > idx[:, None]).astype(BF16)
    return pl.pallas_call(
        _sb_body,
        out_shape=jax.ShapeDtypeStruct((t, D_MODEL), BF16),
        grid=(batch, HEAD_PAIRS, nq),
        in_specs=[
            pl.BlockSpec((ATT_Q_BLOCK, LANES), lambda b, p, i: (b * nq + i, p)),
            pl.BlockSpec((seq, LANES), lambda b, p, i: (b, HEAD_PAIRS + p)),
            pl.BlockSpec((seq, LANES), lambda b, p, i: (b, 2 * HEAD_PAIRS + p)),
            pl.BlockSpec((ATT_BLOCK, ATT_BLOCK), lambda b, p, i: (0, 0)),
        ],
        out_specs=pl.BlockSpec((ATT_Q_BLOCK, LANES), lambda b, p, i: (b * nq + i, p)),
        scratch_shapes=[
            pltpu.VMEM((LANES, seq), BF16),
            pltpu.VMEM((ATT_K_PER_Q, 2, HEAD_DIM, ATT_BLOCK), F32),
            pltpu.VMEM((ATT_K_PER_Q, 2, 1, ATT_BLOCK), F32),
            pltpu.VMEM((4 * ATT_K_PER_Q, ATT_BLOCK, ATT_BLOCK), F32),
            pltpu.VMEM((4 * ATT_K_PER_Q, ATT_BLOCK, ATT_BLOCK), BF16),
            pltpu.VMEM((4 * ATT_K_PER_Q, 1, ATT_BLOCK), F32),
        ],
        compiler_params=_params("parallel", "parallel", "arbitrary"),
        name="sb_attention",
    )(proj, proj, proj, upper)


def _fox_gate_body(x_ref, g_ref, w_ref, b_ref, tri_ref, c_ref):
    h = _rms_normalize(x_ref[...], g_ref[...])
    logits = _dot_split(h, w_ref[...])
    log_f = -_softplus(-(logits + b_ref[...]))
    seq = log_f.shape[0]
    carry = jnp.zeros((1, LANES), F32)
    for blk in range(seq // ATT_BLOCK):
        rows = slice(blk * ATT_BLOCK, (blk + 1) * ATT_BLOCK)
        c = carry
        for part in _split_bf16(log_f[rows], 3):
            c = c + jnp.dot(tri_ref[...], part, preferred_element_type=F32)
        c_ref[rows, :] = c
        carry = c[ATT_BLOCK - 1:ATT_BLOCK, :]


def _fox_cum_log_forget(x, gain, w_gate, b_gate, batch, seq):
    t, d = x.shape
    w = jnp.zeros((d, LANES), F32).at[:, :N_HEADS].set(w_gate)
    b = jnp.zeros((1, LANES), F32).at[0, :N_HEADS].set(b_gate)
    idx = jnp.arange(ATT_BLOCK)
    tri = (idx[:, None] >= idx[None, :]).astype(BF16)
    return pl.pallas_call(
        _fox_gate_body,
        out_shape=jax.ShapeDtypeStruct((t, LANES), F32),
        grid=(batch,),
        in_specs=[
            pl.BlockSpec((seq, d), lambda i: (i, 0)),
            pl.BlockSpec((1, d), lambda i: (0, 0)),
            pl.BlockSpec((d, LANES), lambda i: (0, 0)),
            pl.BlockSpec((1, LANES), lambda i: (0, 0)),
            pl.BlockSpec((ATT_BLOCK, ATT_BLOCK), lambda i: (0, 0)),
        ],
        out_specs=pl.BlockSpec((seq, LANES), lambda i: (i, 0)),
        compiler_params=_params("parallel"),
        name="fox_gate",
    )(x, gain.reshape(1, d), w, b, tri)


def _fox_body(q_ref, k_ref, v_ref, cq_ref, ck_ref, o_ref, vt_scr, ckb_scr, a_scr, p_scr):
    qi = pl.program_id(2)

    @pl.when(qi == 0)
    def _():
        _transpose_values(v_ref, vt_scr)
        for hh in range(2):
            for c in range(ckb_scr.shape[1] // ATT_BLOCK):
                rows = slice(c * ATT_BLOCK, (c + 1) * ATT_BLOCK)
                ckb_scr[hh, rows, :] = jnp.broadcast_to(ck_ref[0, hh, :, rows], (LANES, ATT_BLOCK)).T

    qs = _split_head_pair(q_ref[...])
    key = lax.broadcasted_iota(jnp.int32, (ATT_BLOCK, ATT_Q_BLOCK), 0)
    query = lax.broadcasted_iota(jnp.int32, (ATT_BLOCK, ATT_Q_BLOCK), 1)
    row_shape = (1, ATT_Q_BLOCK)

    def scores(kb, tops, diagonal):
        start = kb * ATT_BLOCK
        k = k_ref[pl.ds(start, ATT_BLOCK), :]
        if diagonal:
            causal = key + (kb * ATT_BLOCK - qi * ATT_Q_BLOCK) <= query
        new_tops = []
        for hh in range(2):
            ck = ckb_scr[hh, pl.ds(start, ATT_BLOCK), :]
            a = _qk(k, qs[hh]) - jnp.concatenate([ck] * (ATT_Q_BLOCK // LANES), axis=1)
            if diagonal:
                a = jnp.where(causal, a, NEG_INF)
            a_scr[hh, kb] = a
            new_tops.append(jnp.maximum(tops[hh], jnp.max(a, axis=0, keepdims=True)))
        return tuple(new_tops)

    def sweep(n_blocks):
        tops = (jnp.full(row_shape, NEG_INF, F32),) * 2
        for kb in range(n_blocks):
            tops = scores(kb, tops, kb >= n_blocks - ATT_K_PER_Q)
        shifts = tuple(cq_ref[0, hh] - (tops[hh] + cq_ref[0, hh]) for hh in range(2))
        sums = [jnp.zeros(row_shape, F32)] * 2
        for kb in range(n_blocks):
            for hh in range(2):
                p = jnp.exp(a_scr[hh, kb] + shifts[hh])
                p_scr[hh, kb] = p.astype(BF16)
                sums[hh] = sums[hh] + jnp.sum(p, axis=0, keepdims=True)
        accs = [jnp.zeros((HEAD_DIM, ATT_Q_BLOCK), F32)] * 2
        for kb in range(n_blocks):
            for hh in range(2):
                vt = vt_scr[hh * HEAD_DIM:(hh + 1) * HEAD_DIM, kb * ATT_BLOCK:(kb + 1) * ATT_BLOCK]
                accs[hh] = accs[hh] + jnp.dot(vt, p_scr[hh, kb], preferred_element_type=F32)
        out_t = jnp.concatenate([accs[0] / sums[0], accs[1] / sums[1]], axis=0)
        o_ref[...] = out_t.T.astype(o_ref.dtype)

    for q in range(k_ref.shape[0] // ATT_Q_BLOCK):
        pl.when(qi == q)(functools.partial(sweep, (q + 1) * ATT_K_PER_Q))


def _fox_attention(proj, cum, batch, seq):
    t = proj.shape[0]
    nq = seq // ATT_Q_BLOCK
    cum_h = cum[:, :N_HEADS].reshape(batch, seq, N_HEADS).transpose(0, 2, 1)
    cum_rows = cum_h.reshape(batch, N_HEADS, 1, seq)
    return pl.pallas_call(
        _fox_body,
        out_shape=jax.ShapeDtypeStruct((t, D_MODEL), BF16),
        grid=(batch, HEAD_PAIRS, nq),
        in_specs=[
            pl.BlockSpec((ATT_Q_BLOCK, LANES), lambda b, p, i: (b * nq + i, p)),
            pl.BlockSpec((seq, LANES), lambda b, p, i: (b, HEAD_PAIRS + p)),
            pl.BlockSpec((seq, LANES), lambda b, p, i: (b, 2 * HEAD_PAIRS + p)),
            pl.BlockSpec((1, 2, 1, ATT_Q_BLOCK), lambda b, p, i: (b, p, 0, i)),
            pl.BlockSpec((1, 2, 1, seq), lambda b, p, i: (b, p, 0, 0)),
        ],
        out_specs=pl.BlockSpec((ATT_Q_BLOCK, LANES), lambda b, p, i: (b * nq + i, p)),
        scratch_shapes=[
            pltpu.VMEM((LANES, seq), BF16),
            pltpu.VMEM((2, seq, LANES), F32),
            pltpu.VMEM((2, seq // ATT_BLOCK, ATT_BLOCK, ATT_Q_BLOCK), F32),
            pltpu.VMEM((2, seq // ATT_BLOCK, ATT_BLOCK, ATT_Q_BLOCK), BF16),
        ],
        compiler_params=_params("parallel", "parallel", "arbitrary"),
        name="fox_attention",
    )(proj, proj, proj, cum_rows, cum_rows)


def _band_body(q_ref, k_ref, v_ref, bias_ref, o_ref, s_ref, vt_scr, logit_scr, p_scr, inv_scr, *, sub_len):
    seq = v_ref.shape[0]
    n_blocks = seq // DIL_BLOCK
    _transpose_values(v_ref, vt_scr)

    def key_rows(n):
        first = (n * DIL_BLOCK) % sub_len == 0
        return first, slice((n if first else n - 1) * DIL_BLOCK, (n + 1) * DIL_BLOCK)

    for n in range(n_blocks):
        first, k_rows = key_rows(n)
        qs = _split_head_pair(q_ref[n * DIL_BLOCK:(n + 1) * DIL_BLOCK, :])
        k = k_ref[k_rows, :]
        for hh in range(2):
            bias = bias_ref[hh, DIL_BLOCK:, :] if first else bias_ref[hh]
            logit_scr[2 * n + hh, :k.shape[0], :] = _qk(k, qs[hh]) + bias

    for n in range(n_blocks):
        first, k_rows = key_rows(n)
        n_keys = k_rows.stop - k_rows.start
        for hh in range(2):
            logits = logit_scr[2 * n + hh, :n_keys, :]
            m = jnp.max(logits, axis=0, keepdims=True)
            p = jnp.exp(logits - m)
            l = jnp.sum(p, axis=0, keepdims=True)
            p_scr[2 * n + hh, :n_keys, :] = p.astype(BF16)
            inv_scr[2 * n + hh] = 1.0 / l
            s_ref[0, 0, hh:hh + 1, n * DIL_BLOCK:(n + 1) * DIL_BLOCK] = m + jnp.log(l)

    for n in range(n_blocks):
        first, k_rows = key_rows(n)
        n_keys = k_rows.stop - k_rows.start
        outs = []
        for hh in range(2):
            vt = vt_scr[hh * HEAD_DIM:(hh + 1) * HEAD_DIM, k_rows]
            o = jnp.dot(vt, p_scr[2 * n + hh, :n_keys, :], preferred_element_type=F32)
            outs.append(o * inv_scr[2 * n + hh])
        o_ref[n * DIL_BLOCK:(n + 1) * DIL_BLOCK, :] = jnp.concatenate(outs, axis=0).T.astype(o_ref.dtype)


def _band_attention(qk, v, bias_t, batch, seq, sub_len):
    t = v.shape[0]
    return pl.pallas_call(
        functools.partial(_band_body, sub_len=sub_len),
        out_shape=(jax.ShapeDtypeStruct((t, D_MODEL), BF16),
                   jax.ShapeDtypeStruct((batch, HEAD_PAIRS, 2, seq), F32)),
        grid=(batch, HEAD_PAIRS),
        in_specs=[
            pl.BlockSpec((None, seq, LANES), lambda b, p: (0, b, p)),
            pl.BlockSpec((None, seq, LANES), lambda b, p: (1, b, p)),
            pl.BlockSpec((seq, LANES), lambda b, p: (b, p)),
            pl.BlockSpec((2, 2 * DIL_BLOCK, DIL_BLOCK), lambda b, p: (p, 0, 0)),
        ],
        out_specs=(pl.BlockSpec((seq, LANES), lambda b, p: (b, p)),
                   pl.BlockSpec((1, 1, 2, seq), lambda b, p: (b, p, 0, 0))),
        scratch_shapes=[pltpu.VMEM((LANES, seq), BF16),
                        pltpu.VMEM((2 * seq // DIL_BLOCK, 2 * DIL_BLOCK, DIL_BLOCK), F32),
                        pltpu.VMEM((2 * seq // DIL_BLOCK, 2 * DIL_BLOCK, DIL_BLOCK), BF16),
                        pltpu.VMEM((2 * seq // DIL_BLOCK, 1, DIL_BLOCK), F32)],
        compiler_params=_params("parallel", "parallel"),
        name="band_attention",
    )(qk, qk, v, bias_t)


def _softmax_merge_body(o1_ref, o2_ref, o3_ref, s_ref, e_ref, o_ref):
    s = s_ref[...]
    groups = [s, pltpu.roll(s, LANES - N_HEADS, axis=1), pltpu.roll(s, LANES - 2 * N_HEADS, axis=1)]
    top = jnp.maximum(jnp.maximum(groups[0], groups[1]), groups[2])
    weights = [jnp.exp(g - top) for g in groups]
    inv = 1.0 / (weights[0] + weights[1] + weights[2])
    out = jnp.zeros(o_ref.shape, F32)
    for w, part in zip(weights, (o1_ref, o2_ref, o3_ref)):
        spread = jnp.dot((w * inv).astype(BF16), e_ref[...], preferred_element_type=F32)
        out = out + spread * part[...].astype(F32)
    o_ref[...] = out.astype(o_ref.dtype)


def _softmax_merge(outs, s_all):
    t, d = outs[0].shape
    head_of = jnp.arange(d) // HEAD_DIM
    expand = (jnp.arange(LANES)[:, None] == head_of[None, :]).astype(BF16)
    rows = pl.BlockSpec((ROW_TILE, d), lambda i: (i, 0))
    return pl.pallas_call(
        _softmax_merge_body,
        out_shape=jax.ShapeDtypeStruct((t, d), BF16),
        grid=(t // ROW_TILE,),
        in_specs=[rows, rows, rows,
                  pl.BlockSpec((ROW_TILE, LANES), lambda i: (i, 0)),
                  pl.BlockSpec((LANES, d), lambda i: (0, 0))],
        out_specs=rows,
        compiler_params=_params("parallel"),
        name="softmax_merge",
    )(*outs, s_all, expand)


def _t5_causal_bucket(distance):
    max_exact = N_REL_BUCKETS // 2
    d = jnp.maximum(distance, 1).astype(F32)
    log_b = max_exact + (jnp.log(d / max_exact) / math.log(REL_MAX_DISTANCE / max_exact)
                         * (N_REL_BUCKETS - max_exact)).astype(jnp.int32)
    log_b = jnp.minimum(log_b, N_REL_BUCKETS - 1)
    return jnp.where(distance < max_exact, distance, log_b)


def _dilated_bias(rel_bias):
    kj = jnp.arange(2 * DIL_BLOCK, dtype=jnp.int32)
    qi = jnp.arange(DIL_BLOCK, dtype=jnp.int32)
    delta = qi[None, :] + DIL_BLOCK - kj[:, None]
    in_band = (delta >= 0) & (delta <= DIL_SPAN)
    buckets = jnp.stack([_t5_causal_bucket(jnp.maximum(delta, 0) * dil) for _, dil in DILATED_PAIRS])
    one_hot = (buckets[..., None] == jnp.arange(N_REL_BUCKETS)).astype(F32)
    bias = jnp.einsum("gkqb,bh->ghkq", one_hot, rel_bias.astype(F32), precision=lax.Precision.HIGHEST)
    return jnp.where(in_band[None, None], bias, NEG_INF)


def _dilated_proj_body(x_ref, g_ref, w_ref, cs_ref, bd_ref, *rest):
    n_groups = len(DILATED_PAIRS)
    qk_refs, v_refs = rest[:n_groups], rest[n_groups:2 * n_groups]
    h_scr, res_scr = rest[2 * n_groups:]
    j = pl.program_id(1)

    @pl.when(j == 0)
    def _():
        h_scr[...] = _rms_normalize(x_ref[...], g_ref[...]).astype(BF16)

    acc = jnp.dot(h_scr[...], w_ref[...], preferred_element_type=F32)

    def keep(res, first_lane):
        for c in range(res.shape[1] // LANES):
            res_scr[first_lane // LANES + c] = res[:, c * LANES:(c + 1) * LANES]

    @pl.when(j < 2 * n_groups)
    def _():
        for c in range(COL_TILE // NORM_CHUNK):
            sl = slice(c * NORM_CHUNK, (c + 1) * NORM_CHUNK)
            a = acc[:, sl]
            ss = jnp.dot((a * a).astype(BF16), bd_ref[...], preferred_element_type=F32)
            keep(a * lax.rsqrt(ss * (1.0 / HEAD_DIM) + RMS_EPS) * cs_ref[:, sl], c * NORM_CHUNK)

    @pl.when(j == 2 * n_groups)
    def _():
        keep(acc * cs_ref[...], 0)

    def write_classes(ref, dil):
        rows = res_scr.shape[1] // dil
        for c in range(res_scr.shape[0]):
            lanes = slice(c * LANES, (c + 1) * LANES)
            if dil == 1:
                ref[:, lanes] = res_scr[c].astype(ref.dtype)
            else:
                for r in range(dil):
                    ref[r, :, lanes] = res_scr[c, pl.ds(r, rows, stride=dil), :].astype(ref.dtype)

    for g, (_, dil) in enumerate(DILATED_PAIRS):
        pl.when(j // 2 == g)(functools.partial(write_classes, qk_refs[g], dil))
        pl.when(j == 2 * n_groups)(functools.partial(write_classes, v_refs[g], dil))


def _dilated_proj(x, gain, w, layer, col_scale, batch, seq):
    t, d = x.shape
    n = w.shape[2]
    n_groups = len(DILATED_PAIRS)
    tiles_per_seq = seq // DIL_ROW_TILE
    head_id = jnp.arange(NORM_CHUNK) // HEAD_DIM
    block_diag = (head_id[:, None] == head_id[None, :]).astype(BF16)
    out_shape, out_specs = [], []
    for kind in ("qk", "v"):
        for g, (_, dil) in enumerate(DILATED_PAIRS):
            rows = DIL_ROW_TILE // dil
            if kind == "qk":
                shape = (2, batch, dil, seq // dil, d)
                block = (None, None, dil, rows, d)
                index = lambda i, j, g=g: (jnp.clip(j - 2 * g, 0, 1), i // tiles_per_seq, 0, i % tiles_per_seq, 0)
            else:
                shape = (batch, dil, seq // dil, d)
                block = (None, dil, rows, d)
                index = lambda i, j: (i // tiles_per_seq, 0, i % tiles_per_seq, 0)
            if dil == 1:
                block = block[:-3] + (None,) + block[-2:]
            out_shape.append(jax.ShapeDtypeStruct(shape, BF16))
            out_specs.append(pl.BlockSpec(block, index))
    outs = pl.pallas_call(
        _dilated_proj_body,
        out_shape=tuple(out_shape),
        grid=(t // DIL_ROW_TILE, n // COL_TILE),
        in_specs=[
            pl.BlockSpec((DIL_ROW_TILE, d), lambda i, j: (i, 0)),
            pl.BlockSpec((1, d), lambda i, j: (0, 0)),
            pl.BlockSpec((None, d, COL_TILE), lambda i, j: (layer, 0, j)),
            pl.BlockSpec((1, COL_TILE), lambda i, j: (0, j)),
            pl.BlockSpec((NORM_CHUNK, NORM_CHUNK), lambda i, j: (0, 0)),
        ],
        out_specs=tuple(out_specs),
        scratch_shapes=[pltpu.VMEM((DIL_ROW_TILE, d), BF16),
                        pltpu.VMEM((COL_TILE // LANES, DIL_ROW_TILE, LANES), F32)],
        compiler_params=_params("parallel", "arbitrary"),
        name="dilated_proj",
    )(x, gain.reshape(1, d), w, col_scale.reshape(1, n).astype(F32), block_diag)
    qk = [o.reshape(2, t, d) for o in outs[:n_groups]]
    v = [o.reshape(t, d) for o in outs[n_groups:]]
    return qk, v


def _dilated_attention(qk, v, rel_bias, batch, seq):
    t = v[0].shape[0]
    n_groups = len(DILATED_PAIRS)
    bias_t = _dilated_bias(rel_bias)
    outs, stats = [], []
    for g, (_, dil) in enumerate(DILATED_PAIRS):
        sub_len = seq // dil
        o, s = _band_attention(qk[g], v[g], bias_t[g], batch, seq, sub_len)
        o = o.reshape(batch, dil, sub_len, D_MODEL).transpose(0, 2, 1, 3).reshape(t, D_MODEL)
        s = s.reshape(batch, HEAD_PAIRS, 2, dil, sub_len).swapaxes(3, 4)
        outs.append(o)
        stats.append(s.reshape(batch, N_HEADS, seq).transpose(0, 2, 1).reshape(t, N_HEADS))
    stats.append(jnp.zeros((t, LANES - n_groups * N_HEADS), F32))
    return _softmax_merge(outs, jnp.concatenate(stats, axis=1))


def _tile_heads(v):
    return jnp.tile(v.astype(F32), N_HEADS)


def _sb_mixer(x, gain, w_qkv, layer, batch, seq):
    ones = jnp.ones((D_MODEL,), F32)
    col_scale = jnp.concatenate([ones * QK_SCALE, ones, ones])
    proj = _norm_proj(x, gain, w_qkv, layer, col_scale, 0)
    return _sb_attention(proj, batch, seq)


def _dilated_mixer(x, gain, w_in, q_norm, k_norm, rel_bias, layer, batch, seq):
    scales = []
    for g in range(len(DILATED_PAIRS)):
        scales += [_tile_heads(q_norm[g]) * QK_SCALE, _tile_heads(k_norm[g])]
    scales.append(jnp.ones((D_MODEL,), F32))
    qk, v = _dilated_proj(x, gain, w_in, layer, jnp.concatenate(scales), batch, seq)
    return _dilated_attention(qk, v, rel_bias, batch, seq)


def _fox_mixer(x, gain, w_qkv, w_gate, b_f, q_norm, k_norm, layer, batch, seq):
    col_scale = jnp.concatenate([_tile_heads(q_norm) * QK_SCALE, _tile_heads(k_norm),
                                 jnp.ones((D_MODEL,), F32)])
    proj = _norm_proj(x, gain, w_qkv, layer, col_scale, 2)
    cum = _fox_cum_log_forget(x, gain, w_gate, b_f, batch, seq)
    return _fox_attention(proj, cum, batch, seq)


def kernel(x, sb_w_qkv, sb_w_o, dil_w_in, dil_q_norm, dil_k_norm, dil_w_o, fox_w_in, fox_b_f,
           fox_q_norm, fox_k_norm, fox_w_o, rel_bias, attn_norm, ffn_norm, mlp_w_gate, mlp_w_up,
           mlp_w_down, moe_router, moe_w_gate, moe_w_up, moe_w_down):
    batch, seq, d = x.shape
    depth = attn_norm.shape[0]
    assert d == D_MODEL and seq % max(ROW_TILE, DIL_ROW_TILE, ATT_Q_BLOCK) == 0
    assert all((seq // dil) % DIL_BLOCK == 0 for _, dil in DILATED_PAIRS)
    sb_w_qkv, sb_w_o, dil_w_in, dil_w_o, fox_w_o, mlp_w_gate, mlp_w_up, mlp_w_down = (
        w.astype(BF16) for w in (sb_w_qkv, sb_w_o, dil_w_in, dil_w_o, fox_w_o, mlp_w_gate, mlp_w_up,
                                 mlp_w_down))
    moe_w_gate, moe_w_up, moe_w_down = (w.astype(BF16) for w in (moe_w_gate, moe_w_up, moe_w_down))
    fox_w_qkv = fox_w_in[:, :, :3 * D_MODEL].astype(BF16)
    h = x.reshape(batch * seq, d)
    for i in range(depth):
        kind, j = i % 3, i // 3
        if kind == 0:
            mixed, w_o = _sb_mixer(h, attn_norm[i], sb_w_qkv, j, batch, seq), sb_w_o
        elif kind == 1:
            mixed, w_o = _dilated_mixer(h, attn_norm[i], dil_w_in, dil_q_norm[j], dil_k_norm[j], rel_bias,
                                        j, batch, seq), dil_w_o
        else:
            mixed, w_o = _fox_mixer(h, attn_norm[i], fox_w_qkv, fox_w_in[j, :, 3 * D_MODEL:], fox_b_f[j],
                                    fox_q_norm[j], fox_k_norm[j], j, batch, seq), fox_w_o
        f = i // 2
        if i % 2 == 0:
            h = _mix_ffn_residual(h, mixed, w_o, j, ffn_norm[i], mlp_w_gate, mlp_w_up, mlp_w_down, f)
        else:
            h = _mix_moe_residual(h, mixed, w_o, j, ffn_norm[i], moe_router[f], moe_w_gate, moe_w_up,
                                  moe_w_down, f)
    return h.reshape(batch, seq, d)
```

```python
import functools
import math

import jax
import jax.numpy as jnp
from jax import lax
from jax.experimental import pallas as pl
from jax.experimental.pallas import tpu as pltpu
from jax.experimental.pallas import tpu_sc as plsc

D_MODEL = 1024
N_HEADS = 16
HEAD_DIM = 64
LANES = 128
HEAD_PAIRS = D_MODEL // LANES
N_EXPERTS = 8
N_REL_BUCKETS = 32
REL_MAX_DISTANCE = 2048
DILATED_PAIRS = ((128, 1), (512, 4), (2048, 16))
DIL_SPAN = 128
RMS_EPS = 1e-6
NEG_INF = -1e30
SB_UNDERFLOW_LOG = -104.0
QK_SCALE = 1.0 / math.sqrt(HEAD_DIM)

ROW_TILE = 1024
COL_TILE = 1024
FF_TILE = 1792
SWIGLU_CHUNK = 256
MOE_ROW_TILE = 512
MOE_FF_TILE = 1792
SC_CORES = 2
SC_SUBCORES = 16
SC_CHUNK_ROWS = 64
NORM_CHUNK = 256
ATT_BLOCK = 256
ATT_Q_BLOCK = 512
ATT_K_PER_Q = ATT_Q_BLOCK // ATT_BLOCK
SB_Q_BLOCK = 1024
SB_SUB_BLOCKS = SB_Q_BLOCK // ATT_BLOCK
DIL_BLOCK = 128
DIL_ROW_TILE = 1024
VMEM_LIMIT = 56 * 1024 * 1024

F32 = jnp.float32
BF16 = jnp.bfloat16


def _params(*semantics):
    return pltpu.CompilerParams(dimension_semantics=semantics, vmem_limit_bytes=VMEM_LIMIT)


def _rms_normalize(x, gain):
    inv = lax.rsqrt(jnp.mean(x * x, axis=-1, keepdims=True) + RMS_EPS)
    return x * inv * gain


def _split_bf16(x, terms):
    parts = []
    for _ in range(terms):
        part = x.astype(BF16)
        parts.append(part)
        x = x - part.astype(F32)
    return parts


def _dot_split(a, b):
    a_hi, a_lo = _split_bf16(a, 2)
    b_hi, b_lo = _split_bf16(b, 2)
    return (jnp.dot(a_hi, b_hi, preferred_element_type=F32)
            + (jnp.dot(a_hi, b_lo, preferred_element_type=F32)
               + jnp.dot(a_lo, b_hi, preferred_element_type=F32)))


def _softplus(z):
    return jnp.maximum(z, 0.0) + jnp.log(1.0 + jnp.exp(-jnp.abs(z)))


def _norm_proj_body(x_ref, g_ref, w_ref, cs_ref, bd_ref, o_ref, h_scr, *, n_norm):
    j = pl.program_id(1)

    @pl.when(j == 0)
    def _():
        h_scr[...] = _rms_normalize(x_ref[...], g_ref[...]).astype(BF16)

    acc = jnp.dot(h_scr[...], w_ref[...], preferred_element_type=F32)

    def plain():
        o_ref[...] = (acc * cs_ref[...]).astype(o_ref.dtype)

    def head_normed():
        for c in range(COL_TILE // NORM_CHUNK):
            sl = slice(c * NORM_CHUNK, (c + 1) * NORM_CHUNK)
            a = acc[:, sl]
            ss = jnp.dot((a * a).astype(BF16), bd_ref[...], preferred_element_type=F32)
            inv = lax.rsqrt(ss * (1.0 / HEAD_DIM) + RMS_EPS)
            o_ref[:, sl] = (a * inv * cs_ref[:, sl]).astype(o_ref.dtype)

    if n_norm == 0:
        plain()
    else:
        pl.when(j < n_norm)(head_normed)
        pl.when(j >= n_norm)(plain)


def _norm_proj(x, gain, w, layer, col_scale, n_norm):
    t, d = x.shape
    n = w.shape[2]
    head_id = jnp.arange(NORM_CHUNK) // HEAD_DIM
    block_diag = (head_id[:, None] == head_id[None, :]).astype(BF16)
    return pl.pallas_call(
        functools.partial(_norm_proj_body, n_norm=n_norm),
        out_shape=jax.ShapeDtypeStruct((t, n), BF16),
        grid=(t // ROW_TILE, n // COL_TILE),
        in_specs=[
            pl.BlockSpec((ROW_TILE, d), lambda i, j: (i, 0)),
            pl.BlockSpec((1, d), lambda i, j: (0, 0)),
            pl.BlockSpec((None, d, COL_TILE), lambda i, j: (layer, 0, j)),
            pl.BlockSpec((1, COL_TILE), lambda i, j: (0, j)),
            pl.BlockSpec((NORM_CHUNK, NORM_CHUNK), lambda i, j: (0, 0)),
        ],
        out_specs=pl.BlockSpec((ROW_TILE, COL_TILE), lambda i, j: (i, j)),
        scratch_shapes=[pltpu.VMEM((ROW_TILE, d), BF16)],
        compiler_params=_params("parallel", "arbitrary"),
        name="norm_proj",
    )(x, gain.reshape(1, d), w, col_scale.reshape(1, n).astype(F32), block_diag)


def _swiglu_hidden(h, wg, wu):
    g = jnp.dot(h, wg, preferred_element_type=F32)
    u = jnp.dot(h, wu, preferred_element_type=F32)
    return g * (1.0 / (1.0 + jnp.exp(-g))) * u


def _swiglu_down(h, wg_ref, wu_ref, wd_ref):
    y = None
    for c in range(wg_ref.shape[1] // SWIGLU_CHUNK):
        cols = slice(c * SWIGLU_CHUNK, (c + 1) * SWIGLU_CHUNK)
        a = _swiglu_hidden(h, wg_ref[:, cols], wu_ref[:, cols]).astype(BF16)
        part = jnp.dot(a, wd_ref[cols, :], preferred_element_type=F32)
        y = part if y is None else y + part
    return y


def _ffn_body(x_ref, o_ref, wo_ref, g_ref, wg_ref, wu_ref, wd_ref, y_ref, h_scr):
    f = pl.program_id(1)

    @pl.when(f == 0)
    def _():
        x = x_ref[...] + jnp.dot(o_ref[...], wo_ref[...], preferred_element_type=F32)
        h_scr[...] = _rms_normalize(x, g_ref[...]).astype(BF16)
        y_ref[...] = x

    y_ref[...] += _swiglu_down(h_scr[...], wg_ref, wu_ref, wd_ref)


def _mix_ffn_residual(x, o, w_o, mix_layer, gain, w_gate, w_up, w_down, layer):
    t, d = x.shape
    ff = w_gate.shape[2]
    return pl.pallas_call(
        _ffn_body,
        out_shape=jax.ShapeDtypeStruct((t, d), F32),
        grid=(t // ROW_TILE, ff // FF_TILE),
        in_specs=[
            pl.BlockSpec((ROW_TILE, d), lambda i, f: (i, 0)),
            pl.BlockSpec((ROW_TILE, d), lambda i, f: (i, 0)),
            pl.BlockSpec((None, d, d), lambda i, f: (mix_layer, 0, 0)),
            pl.BlockSpec((1, d), lambda i, f: (0, 0)),
            pl.BlockSpec((None, d, FF_TILE), lambda i, f: (layer, 0, f)),
            pl.BlockSpec((None, d, FF_TILE), lambda i, f: (layer, 0, f)),
            pl.BlockSpec((None, FF_TILE, d), lambda i, f: (layer, f, 0)),
        ],
        out_specs=pl.BlockSpec((ROW_TILE, d), lambda i, f: (i, 0)),
        scratch_shapes=[pltpu.VMEM((ROW_TILE, d), BF16)],
        compiler_params=_params("parallel", "arbitrary"),
        name="ffn",
    )(x, o, w_o, gain.reshape(1, d), w_gate, w_up, w_down)


def _pack_bf16_pairs(x):
    half = x.shape[1] // 2
    bits = pltpu.bitcast(x.astype(BF16).astype(F32), jnp.int32)
    return bits[:, :half] | lax.shift_right_logical(bits[:, half:], jnp.int32(16))


def _unpack_bf16_pairs(p):
    left = pltpu.bitcast(p & jnp.int32(-65536), F32)
    right = pltpu.bitcast(lax.shift_left(p, jnp.int32(16)), F32)
    return jnp.concatenate([left, right], axis=1).astype(BF16)


def _router_body(x_ref, o_ref, wo_ref, g_ref, r_ref, tri_ref, gates_ref, rank_ref, h_ref, x1_ref, count_scr):
    @pl.when(pl.program_id(0) == 0)
    def _():
        count_scr[...] = jnp.zeros_like(count_scr)

    x1 = x_ref[...] + jnp.dot(o_ref[...], wo_ref[...], preferred_element_type=F32)
    x1_ref[...] = x1
    h = _rms_normalize(x1, g_ref[...])
    h_ref[...] = _pack_bf16_pairs(h)
    logits = _dot_split(h, r_ref[...])
    lane = lax.broadcasted_iota(jnp.int32, logits.shape, 1).astype(F32)
    logits = jnp.where(lane < N_EXPERTS, logits, -jnp.inf)
    m1 = jnp.max(logits, axis=-1, keepdims=True)
    i1 = jnp.min(jnp.where(logits == m1, lane, float(LANES)), axis=-1, keepdims=True)
    rest = jnp.where(lane == i1, -jnp.inf, logits)
    m2 = jnp.max(rest, axis=-1, keepdims=True)
    i2 = jnp.min(jnp.where(rest == m2, lane, float(LANES)), axis=-1, keepdims=True)
    e = jnp.exp(m2 - m1)
    g1 = 1.0 / (1.0 + e)
    gates = jnp.where(lane == i1, g1, 0.0) + jnp.where(lane == i2, e * g1, 0.0)
    gates_ref[...] = gates.T[:N_EXPERTS, :]
    chosen = jnp.where((lane == i1) | (lane == i2), 1.0, 0.0)
    inclusive = jnp.dot(tri_ref[...], chosen.astype(BF16), preferred_element_type=F32)
    rank = jnp.where(chosen > 0.0, inclusive - 1.0 + count_scr[...], -1.0)
    rank_ref[...] = rank.T[:N_EXPERTS, :]
    count_scr[...] += inclusive[ROW_TILE - 1:ROW_TILE, :]


def _mix_router(x, o, w_o, mix_layer, gain, router):
    t, d = x.shape
    r = jnp.zeros((d, LANES), F32).at[:, :N_EXPERTS].set(router)
    idx = jnp.arange(ROW_TILE)
    tri = (idx[:, None] >= idx[None, :]).astype(BF16)
    rows = pl.BlockSpec((ROW_TILE, d), lambda i: (i, 0))
    return pl.pallas_call(
        _router_body,
        out_shape=(jax.ShapeDtypeStruct((N_EXPERTS, t), F32), jax.ShapeDtypeStruct((N_EXPERTS, t), F32),
                   jax.ShapeDtypeStruct((t, d // 2), jnp.int32), jax.ShapeDtypeStruct((t, d), F32)),
        grid=(t // ROW_TILE,),
        in_specs=[
            rows,
            rows,
            pl.BlockSpec((None, d, d), lambda i: (mix_layer, 0, 0)),
            pl.BlockSpec((1, d), lambda i: (0, 0)),
            pl.BlockSpec((d, LANES), lambda i: (0, 0)),
            pl.BlockSpec((ROW_TILE, ROW_TILE), lambda i: (0, 0)),
        ],
        out_specs=(pl.BlockSpec((N_EXPERTS, ROW_TILE), lambda i: (0, i)),
                   pl.BlockSpec((N_EXPERTS, ROW_TILE), lambda i: (0, i)),
                   pl.BlockSpec((ROW_TILE, d // 2), lambda i: (i, 0)),
                   rows),
        scratch_shapes=[pltpu.VMEM((1, LANES), F32)],
        compiler_params=_params("arbitrary"),
        name="router",
    )(x, o, w_o, gain.reshape(1, d), r, tri)


def _sc_mesh():
    return plsc.VectorSubcoreMesh(core_axis_name="core", subcore_axis_name="subcore",
                                  num_cores=SC_CORES, num_subcores=SC_SUBCORES)


def _sc_worker_base(per_worker):
    return (lax.axis_index("subcore") * SC_CORES + lax.axis_index("core")) * per_worker


def _sc_row_gather(table, idx):
    width = table.shape[1]
    n = idx.shape[0]
    per_worker = n // (SC_CORES * SC_SUBCORES)
    n_chunks = per_worker // SC_CHUNK_ROWS
    assert n == n_chunks * SC_CHUNK_ROWS * SC_CORES * SC_SUBCORES and n_chunks % 2 == 0

    @functools.partial(
        pl.kernel, mesh=_sc_mesh(), out_type=jax.ShapeDtypeStruct((n, width), table.dtype),
        scratch_types=[pltpu.VMEM((SC_CHUNK_ROWS,), jnp.int32), pltpu.VMEM((SC_CHUNK_ROWS,), jnp.int32),
                       pltpu.VMEM((SC_CHUNK_ROWS, width), table.dtype),
                       pltpu.VMEM((SC_CHUNK_ROWS, width), table.dtype),
                       pltpu.SemaphoreType.DMA, pltpu.SemaphoreType.DMA],
        name="sc_row_gather")
    def gather(table_hbm, idx_hbm, out_hbm, idx_a, idx_b, rows_a, rows_b, sem_a, sem_b):
        base = _sc_worker_base(per_worker)
        bufs = ((idx_a, rows_a, sem_a), (idx_b, rows_b, sem_b))

        def rows_of(c):
            return pl.ds(pl.multiple_of(base + c * SC_CHUNK_ROWS, SC_CHUNK_ROWS), SC_CHUNK_ROWS)

        def fetch(c, buf):
            idx_v, rows_v, sem = buf
            pltpu.sync_copy(idx_hbm.at[rows_of(c)], idx_v)
            return pltpu.make_async_copy(table_hbm.at[idx_v], rows_v, sem)

        fetch(0, bufs[0]).start()

        @pl.loop(0, n_chunks, step=2)
        def _(c):
            for b in range(2):
                idx_v, rows_v, sem = bufs[b]
                pltpu.make_async_copy(table_hbm.at[idx_v], rows_v, sem).wait()

                @pl.when(c + b + 1 < n_chunks)
                def _():
                    fetch(c + b + 1, bufs[1 - b]).start()

                pltpu.sync_copy(rows_v, out_hbm.at[rows_of(c + b)])

    return gather(table, idx)


def _sc_row_scatter_pair(rows, idx_lo, idx_hi, n_out):
    n, width = rows.shape
    per_worker = n // (SC_CORES * SC_SUBCORES)
    n_chunks = per_worker // SC_CHUNK_ROWS
    assert n == n_chunks * SC_CHUNK_ROWS * SC_CORES * SC_SUBCORES and n_chunks % 2 == 0
    index_scratch = pltpu.VMEM((SC_CHUNK_ROWS,), jnp.int32)
    rows_scratch = pltpu.VMEM((SC_CHUNK_ROWS, width), rows.dtype)

    @functools.partial(
        pl.kernel, mesh=_sc_mesh(), out_type=jax.ShapeDtypeStruct((n_out, width), rows.dtype),
        scratch_types=[index_scratch, index_scratch, rows_scratch, rows_scratch,
                       pltpu.SemaphoreType.DMA, pltpu.SemaphoreType.DMA],
        name="sc_row_scatter")
    def scatter(rows_hbm, lo_hbm, hi_hbm, out_hbm, lo_v, hi_v, rows_a, rows_b, sem_a, sem_b):
        base = _sc_worker_base(per_worker)
        bufs = ((rows_a, sem_a), (rows_b, sem_b))

        def rows_of(c):
            return pl.ds(pl.multiple_of(base + c * SC_CHUNK_ROWS, SC_CHUNK_ROWS), SC_CHUNK_ROWS)

        def load(c, buf):
            rows_v, sem = buf
            return pltpu.make_async_copy(rows_hbm.at[rows_of(c)], rows_v, sem)

        load(0, bufs[0]).start()

        @pl.loop(0, n_chunks, step=2)
        def _(c):
            for b in range(2):
                rows_v, _ = bufs[b]
                load(c + b, bufs[b]).wait()

                @pl.when(c + b + 1 < n_chunks)
                def _():
                    load(c + b + 1, bufs[1 - b]).start()

                pltpu.sync_copy(lo_hbm.at[rows_of(c + b)], lo_v)
                pltpu.sync_copy(hi_hbm.at[rows_of(c + b)], hi_v)
                pltpu.sync_copy(rows_v, out_hbm.at[lo_v])
                pltpu.sync_copy(rows_v, out_hbm.at[hi_v])

    return scatter(rows, idx_lo, idx_hi)


def _expert_ffn_body(te_ref, nu_ref, nv_ref, h_ref, wg_ref, wu_ref, wd_ref, y_ref, acc_scr):
    i = pl.program_id(0)
    f = pl.program_id(1)

    @pl.when(i < nu_ref[0])
    def _():
        row = lax.broadcasted_iota(jnp.int32, h_ref.shape, 0)
        packed = jnp.where(row < nv_ref[i], h_ref[...], 0)
        y = _swiglu_down(_unpack_bf16_pairs(packed), wg_ref, wu_ref, wd_ref)

        @pl.when(f == 0)
        def _():
            acc_scr[...] = y

        @pl.when(f > 0)
        def _():
            acc_scr[...] += y

        @pl.when(f == pl.num_programs(1) - 1)
        def _():
            y_ref[...] = _pack_bf16_pairs(acc_scr[...])


def _expert_ffn(h_sorted, tile_expert, n_used, tile_valid, w_gate, w_up, w_down, layer):
    rows, half = h_sorted.shape
    d = 2 * half
    ff = w_gate.shape[3]
    n_f = ff // MOE_FF_TILE

    def row_map(i, f, te, nu, nv):
        return (jnp.minimum(i, nu[0] - 1), 0)

    def col_step(i, f, nu):
        return jnp.where(i < nu[0], f, n_f - 1)

    grid_spec = pltpu.PrefetchScalarGridSpec(
        num_scalar_prefetch=3,
        grid=(rows // MOE_ROW_TILE, n_f),
        in_specs=[
            pl.BlockSpec((MOE_ROW_TILE, half), row_map),
            pl.BlockSpec((None, None, d, MOE_FF_TILE),
                         lambda i, f, te, nu, nv: (layer, te[i], 0, col_step(i, f, nu))),
            pl.BlockSpec((None, None, d, MOE_FF_TILE),
                         lambda i, f, te, nu, nv: (layer, te[i], 0, col_step(i, f, nu))),
            pl.BlockSpec((None, None, MOE_FF_TILE, d),
                         lambda i, f, te, nu, nv: (layer, te[i], col_step(i, f, nu), 0)),
        ],
        out_specs=pl.BlockSpec((MOE_ROW_TILE, half), row_map),
        scratch_shapes=[pltpu.VMEM((MOE_ROW_TILE, d), F32)],
    )
    return pl.pallas_call(
        _expert_ffn_body,
        out_shape=jax.ShapeDtypeStruct((rows, half), jnp.int32),
        grid_spec=grid_spec,
        compiler_params=_params("arbitrary", "arbitrary"),
        name="expert_ffn",
    )(tile_expert, n_used, tile_valid, h_sorted, w_gate, w_up, w_down)


def _combine_body(x_ref, y_ref, g_ref, o_ref):
    out = x_ref[...]
    for s in range(2):
        column = jnp.broadcast_to(g_ref[s:s + 1, :], (LANES, g_ref.shape[1])).T
        gate = jnp.concatenate([column] * (out.shape[1] // LANES), axis=1)
        out = out + _unpack_bf16_pairs(y_ref[s]).astype(F32) * gate
    o_ref[...] = out


def _combine_residual(x, y_pairs, gates2):
    t, d = x.shape
    return pl.pallas_call(
        _combine_body,
        out_shape=jax.ShapeDtypeStruct((t, d), F32),
        grid=(t // ROW_TILE,),
        in_specs=[
            pl.BlockSpec((ROW_TILE, d), lambda i: (i, 0)),
            pl.BlockSpec((2, ROW_TILE, d // 2), lambda i: (0, i, 0)),
            pl.BlockSpec((2, ROW_TILE), lambda i: (0, i)),
        ],
        out_specs=pl.BlockSpec((ROW_TILE, d), lambda i: (i, 0)),
        compiler_params=_params("parallel"),
        name="moe_combine",
    )(x, y_pairs, gates2)


def _mix_moe_residual(x, o, w_o, mix_layer, gain, router, w_gate, w_up, w_down, layer):
    t, d = x.shape
    gates, rank, h_packed, x = _mix_router(x, o, w_o, mix_layer, gain, router)
    rank8 = rank.astype(jnp.int32)
    chosen = rank8 >= 0
    counts = jnp.sum(chosen, axis=1, dtype=jnp.int32)
    padded = (counts + MOE_ROW_TILE - 1) // MOE_ROW_TILE * MOE_ROW_TILE
    ends = jnp.cumsum(padded)
    starts = ends - padded
    pos = starts[:, None] + rank8
    max_rows = 2 * t + N_EXPERTS * MOE_ROW_TILE
    pos_lo = jnp.min(jnp.where(chosen, pos, max_rows), axis=0)
    pos_hi = jnp.max(jnp.where(chosen, pos, -1), axis=0)
    gates2 = jnp.stack([jnp.sum(jnp.where(chosen & (pos == pos_lo[None, :]), gates, 0.0), axis=0),
                        jnp.sum(jnp.where(chosen & (pos == pos_hi[None, :]), gates, 0.0), axis=0)])
    n_tiles = max_rows // MOE_ROW_TILE
    n_used = (ends[-1] // MOE_ROW_TILE).astype(jnp.int32)
    tile_start = jnp.minimum(jnp.arange(n_tiles, dtype=jnp.int32), n_used - 1) * MOE_ROW_TILE
    tile_expert = jnp.sum(tile_start[:, None] >= ends[None, :], axis=1, dtype=jnp.int32)
    tile_valid = jnp.clip((starts + counts)[tile_expert] - tile_start, 0, MOE_ROW_TILE).astype(jnp.int32)

    h_sorted = _sc_row_scatter_pair(h_packed, pos_lo, pos_hi, max_rows)
    y_sorted = _expert_ffn(h_sorted, tile_expert, n_used.reshape(1), tile_valid, w_gate, w_up, w_down, layer)
    y_pairs = _sc_row_gather(y_sorted, jnp.concatenate([pos_lo, pos_hi])).reshape(2, t, d // 2)
    return _combine_residual(x, y_pairs, gates2)


def _split_head_pair(q):
    is_first = lax.broadcasted_iota(jnp.int32, (1, LANES), 1) < HEAD_DIM
    zero = jnp.zeros_like(q)
    return jnp.where(is_first, q, zero), jnp.where(is_first, zero, q)


def _qk(q, k):
    return lax.dot_general(q, k, (((1,), (1,)), ((), ())), preferred_element_type=F32)


def _transpose_values(v_ref, vt_scr):
    for c in range(v_ref.shape[0] // ATT_BLOCK):
        rows = slice(c * ATT_BLOCK, (c + 1) * ATT_BLOCK)
        vt_scr[:, rows] = v_ref[rows, :].astype(F32).T.astype(vt_scr.dtype)


def _sb_body(q_ref, k_ref, v_ref, u_ref, o_ref, vt_scr, acc_scr, carry_scr, z_scr, keep_scr, sum_scr):
    qi = pl.program_id(2)

    @pl.when(qi == 0)
    def _():
        _transpose_values(v_ref, vt_scr)

    upper = u_ref[...]
    acc_scr[...] = jnp.zeros_like(acc_scr)
    carry_scr[...] = jnp.zeros_like(carry_scr)
    key = lax.broadcasted_iota(jnp.int32, (ATT_BLOCK, ATT_BLOCK), 0)
    query = lax.broadcasted_iota(jnp.int32, (ATT_BLOCK, ATT_BLOCK), 1)
    strict = key < query
    qs = [_split_head_pair(q_ref[j * ATT_BLOCK:(j + 1) * ATT_BLOCK, :]) for j in range(SB_SUB_BLOCKS)]

    def process(j, kb, keep):
        start = pl.multiple_of(kb * ATT_BLOCK, ATT_BLOCK)
        k = k_ref[pl.ds(start, ATT_BLOCK), :]
        for hh in range(2):
            z = _qk(k, qs[j][hh])
            log_beta = z - _softplus(z)
            log_keep = log_beta - z
            if keep is not None:
                log_keep = jnp.where(keep, log_keep, 0.0)
            remain = jnp.dot(upper, log_keep.astype(BF16), preferred_element_type=F32)
            w = jnp.exp(log_beta + remain + carry_scr[j, hh])
            if keep is not None:
                w = jnp.where(keep, w, 0.0)
            carry_scr[j, hh] += jnp.sum(log_keep, axis=0, keepdims=True)
            vt = vt_scr[pl.ds(hh * HEAD_DIM, HEAD_DIM), pl.ds(start, ATT_BLOCK)]
            acc_scr[j, hh] += jnp.dot(vt, w.astype(BF16), preferred_element_type=F32)

    units = []
    for j in range(SB_SUB_BLOCKS):
        g = qi * SB_SUB_BLOCKS + j
        has_previous = None if j > 0 else jnp.broadcast_to(g > 0, strict.shape)
        for kb, keep in ((g, strict), (jnp.maximum(g - 1, 0), has_previous)):
            for hh in range(2):
                units.append((j, hh, pl.multiple_of(kb * ATT_BLOCK, ATT_BLOCK), keep))

    for u, (j, hh, start, keep) in enumerate(units):
        z_scr[u] = _qk(k_ref[pl.ds(start, ATT_BLOCK), :], qs[j][hh])

    for u, (j, hh, start, keep) in enumerate(units):
        z = z_scr[u]
        log_beta = z - _softplus(z)
        log_keep = log_beta - z
        if keep is not None:
            log_keep = jnp.where(keep, log_keep, 0.0)
        z_scr[u] = log_beta
        keep_scr[u] = log_keep.astype(BF16)
        sum_scr[u] = jnp.sum(log_keep, axis=0, keepdims=True)

    for u in range(len(units)):
        z_scr[u] += jnp.dot(upper, keep_scr[u], preferred_element_type=F32)

    for u, (j, hh, start, keep) in enumerate(units):
        diagonal = u % 4 < 2
        log_w = z_scr[u] if diagonal else z_scr[u] + sum_scr[u - 2]
        w = jnp.exp(log_w)
        if keep is not None:
            w = jnp.where(keep, w, 0.0)
        keep_scr[u] = w.astype(BF16)

    for u, (j, hh, start, keep) in enumerate(units):
        vt = vt_scr[pl.ds(hh * HEAD_DIM, HEAD_DIM), pl.ds(start, ATT_BLOCK)]
        acc_scr[j, hh] += jnp.dot(vt, keep_scr[u], preferred_element_type=F32)
        carry_scr[j, hh] += sum_scr[u]

    def sweep_earlier():
        for j in range(SB_SUB_BLOCKS):
            def more(kb, j=j):
                return (kb >= 0) & (jnp.max(carry_scr[j]) > SB_UNDERFLOW_LOG)

            def step(kb, j=j):
                process(j, kb, None)
                return kb - 1

            lax.while_loop(more, step, qi * SB_SUB_BLOCKS + j - 2)

    pl.when(jnp.max(carry_scr[...]) > SB_UNDERFLOW_LOG)(sweep_earlier)

    for j in range(SB_SUB_BLOCKS):
        out_t = jnp.concatenate([acc_scr[j, 0], acc_scr[j, 1]], axis=0)
        o_ref[j * ATT_BLOCK:(j + 1) * ATT_BLOCK, :] = out_t.T.astype(o_ref.dtype)


def _sb_attention(proj, batch, seq):
    t = proj.shape[0]
    nq = seq // SB_Q_BLOCK
    idx = jnp.arange(ATT_BLOCK)
    upper = (idx[None, :] > idx[:, None]).astype(BF16)
    return pl.pallas_call(
        _sb_body,
        out_shape=jax.ShapeDtypeStruct((t, D_MODEL), BF16),
        grid=(batch, HEAD_PAIRS, nq),
        in_specs=[
            pl.BlockSpec((SB_Q_BLOCK, LANES), lambda b, p, i: (b * nq + i, p)),
            pl.BlockSpec((seq, LANES), lambda b, p, i: (b, HEAD_PAIRS + p)),
            pl.BlockSpec((seq, LANES), lambda b, p, i: (b, 2 * HEAD_PAIRS + p)),
            pl.BlockSpec((ATT_BLOCK, ATT_BLOCK), lambda b, p, i: (0, 0)),
        ],
        out_specs=pl.BlockSpec((SB_Q_BLOCK, LANES), lambda b, p, i: (b * nq + i, p)),
        scratch_shapes=[
            pltpu.VMEM((LANES, seq), BF16),
            pltpu.VMEM((SB_SUB_BLOCKS, 2, HEAD_DIM, ATT_BLOCK), F32),
            pltpu.VMEM((SB_SUB_BLOCKS, 2, 1, ATT_BLOCK), F32),
            pltpu.VMEM((4 * SB_SUB_BLOCKS, ATT_BLOCK, ATT_BLOCK), F32),
            pltpu.VMEM((4 * SB_SUB_BLOCKS, ATT_BLOCK, ATT_BLOCK), BF16),
            pltpu.VMEM((4 * SB_SUB_BLOCKS, 1, ATT_BLOCK), F32),
        ],
        compiler_params=_params("parallel", "parallel", "arbitrary"),
        name="sb_attention",
    )(proj, proj, proj, upper)


def _fox_gate_body(x_ref, g_ref, w_ref, b_ref, tri_ref, c_ref):
    h = _rms_normalize(x_ref[...], g_ref[...])
    logits = _dot_split(h, w_ref[...])
    log_f = -_softplus(-(logits + b_ref[...]))
    seq = log_f.shape[0]
    carry = jnp.zeros((1, LANES), F32)
    for blk in range(seq // ATT_BLOCK):
        rows = slice(blk * ATT_BLOCK, (blk + 1) * ATT_BLOCK)
        c = carry
        for part in _split_bf16(log_f[rows], 3):
            c = c + jnp.dot(tri_ref[...], part, preferred_element_type=F32)
        c_ref[rows, :] = c
        carry = c[ATT_BLOCK - 1:ATT_BLOCK, :]


def _fox_cum_log_forget(x, gain, w_gate, b_gate, batch, seq):
    t, d = x.shape
    w = jnp.zeros((d, LANES), F32).at[:, :N_HEADS].set(w_gate)
    b = jnp.zeros((1, LANES), F32).at[0, :N_HEADS].set(b_gate)
    idx = jnp.arange(ATT_BLOCK)
    tri = (idx[:, None] >= idx[None, :]).astype(BF16)
    return pl.pallas_call(
        _fox_gate_body,
        out_shape=jax.ShapeDtypeStruct((t, LANES), F32),
        grid=(batch,),
        in_specs=[
            pl.BlockSpec((seq, d), lambda i: (i, 0)),
            pl.BlockSpec((1, d), lambda i: (0, 0)),
            pl.BlockSpec((d, LANES), lambda i: (0, 0)),
            pl.BlockSpec((1, LANES), lambda i: (0, 0)),
            pl.BlockSpec((ATT_BLOCK, ATT_BLOCK), lambda i: (0, 0)),
        ],
        out_specs=pl.BlockSpec((seq, LANES), lambda i: (i, 0)),
        compiler_params=_params("parallel"),
        name="fox_gate",
    )(x, gain.reshape(1, d), w, b, tri)


def _fox_body(q_ref, k_ref, v_ref, cq_ref, ck_ref, o_ref, vt_scr, ckb_scr, a_scr, p_scr):
    qi = pl.program_id(2)

    @pl.when(qi == 0)
    def _():
        _transpose_values(v_ref, vt_scr)
        for hh in range(2):
            for c in range(ckb_scr.shape[1] // ATT_BLOCK):
                rows = slice(c * ATT_BLOCK, (c + 1) * ATT_BLOCK)
                ckb_scr[hh, rows, :] = jnp.broadcast_to(ck_ref[0, hh, :, rows], (LANES, ATT_BLOCK)).T

    qs = _split_head_pair(q_ref[...])
    key = lax.broadcasted_iota(jnp.int32, (ATT_BLOCK, ATT_Q_BLOCK), 0)
    query = lax.broadcasted_iota(jnp.int32, (ATT_BLOCK, ATT_Q_BLOCK), 1)
    row_shape = (1, ATT_Q_BLOCK)

    def scores(kb, tops, diagonal):
        start = kb * ATT_BLOCK
        k = k_ref[pl.ds(start, ATT_BLOCK), :]
        if diagonal:
            causal = key + (kb * ATT_BLOCK - qi * ATT_Q_BLOCK) <= query
        new_tops = []
        for hh in range(2):
            ck = ckb_scr[hh, pl.ds(start, ATT_BLOCK), :]
            a = _qk(k, qs[hh]) - jnp.concatenate([ck] * (ATT_Q_BLOCK // LANES), axis=1)
            if diagonal:
                a = jnp.where(causal, a, NEG_INF)
            a_scr[hh, kb] = a
            new_tops.append(jnp.maximum(tops[hh], jnp.max(a, axis=0, keepdims=True)))
        return tuple(new_tops)

    def sweep(n_blocks):
        tops = (jnp.full(row_shape, NEG_INF, F32),) * 2
        for kb in range(n_blocks):
            tops = scores(kb, tops, kb >= n_blocks - ATT_K_PER_Q)
        shifts = tuple(cq_ref[0, hh] - (tops[hh] + cq_ref[0, hh]) for hh in range(2))
        sums = [jnp.zeros(row_shape, F32)] * 2
        for kb in range(n_blocks):
            for hh in range(2):
                p = jnp.exp(a_scr[hh, kb] + shifts[hh])
                p_scr[hh, kb] = p.astype(BF16)
                sums[hh] = sums[hh] + jnp.sum(p, axis=0, keepdims=True)
        accs = [jnp.zeros((HEAD_DIM, ATT_Q_BLOCK), F32)] * 2
        for kb in range(n_blocks):
            for hh in range(2):
                vt = vt_scr[hh * HEAD_DIM:(hh + 1) * HEAD_DIM, kb * ATT_BLOCK:(kb + 1) * ATT_BLOCK]
                accs[hh] = accs[hh] + jnp.dot(vt, p_scr[hh, kb], preferred_element_type=F32)
        out_t = jnp.concatenate([accs[0] / sums[0], accs[1] / sums[1]], axis=0)
        o_ref[...] = out_t.T.astype(o_ref.dtype)

    for q in range(k_ref.shape[0] // ATT_Q_BLOCK):
        pl.when(qi == q)(functools.partial(sweep, (q + 1) * ATT_K_PER_Q))


def _fox_attention(proj, cum, batch, seq):
    t = proj.shape[0]
    nq = seq // ATT_Q_BLOCK
    cum_h = cum[:, :N_HEADS].reshape(batch, seq, N_HEADS).transpose(0, 2, 1)
    cum_rows = cum_h.reshape(batch, N_HEADS, 1, seq)
    return pl.pallas_call(
        _fox_body,
        out_shape=jax.ShapeDtypeStruct((t, D_MODEL), BF16),
        grid=(batch, HEAD_PAIRS, nq),
        in_specs=[
            pl.BlockSpec((ATT_Q_BLOCK, LANES), lambda b, p, i: (b * nq + i, p)),
            pl.BlockSpec((seq, LANES), lambda b, p, i: (b, HEAD_PAIRS + p)),
            pl.BlockSpec((seq, LANES), lambda b, p, i: (b, 2 * HEAD_PAIRS + p)),
            pl.BlockSpec((1, 2, 1, ATT_Q_BLOCK), lambda b, p, i: (b, p, 0, i)),
            pl.BlockSpec((1, 2, 1, seq), lambda b, p, i: (b, p, 0, 0)),
        ],
        out_specs=pl.BlockSpec((ATT_Q_BLOCK, LANES), lambda b, p, i: (b * nq + i, p)),
        scratch_shapes=[
            pltpu.VMEM((LANES, seq), BF16),
            pltpu.VMEM((2, seq, LANES), F32),
            pltpu.VMEM((2, seq // ATT_BLOCK, ATT_BLOCK, ATT_Q_BLOCK), F32),
            pltpu.VMEM((2, seq // ATT_BLOCK, ATT_BLOCK, ATT_Q_BLOCK), BF16),
        ],
        compiler_params=_params("parallel", "parallel", "arbitrary"),
        name="fox_attention",
    )(proj, proj, proj, cum_rows, cum_rows)


def _band_body(q_ref, k_ref, v_ref, bias_ref, o_ref, s_ref, vt_scr, logit_scr, p_scr, inv_scr, *, sub_len):
    seq = v_ref.shape[0]
    n_blocks = seq // DIL_BLOCK
    _transpose_values(v_ref, vt_scr)

    def key_rows(n):
        first = (n * DIL_BLOCK) % sub_len == 0
        return first, slice((n if first else n - 1) * DIL_BLOCK, (n + 1) * DIL_BLOCK)

    for n in range(n_blocks):
        first, k_rows = key_rows(n)
        qs = _split_head_pair(q_ref[n * DIL_BLOCK:(n + 1) * DIL_BLOCK, :])
        k = k_ref[k_rows, :]
        for hh in range(2):
            bias = bias_ref[hh, DIL_BLOCK:, :] if first else bias_ref[hh]
            logit_scr[2 * n + hh, :k.shape[0], :] = _qk(k, qs[hh]) + bias

    for n in range(n_blocks):
        first, k_rows = key_rows(n)
        n_keys = k_rows.stop - k_rows.start
        for hh in range(2):
            logits = logit_scr[2 * n + hh, :n_keys, :]
            m = jnp.max(logits, axis=0, keepdims=True)
            p = jnp.exp(logits - m)
            l = jnp.sum(p, axis=0, keepdims=True)
            p_scr[2 * n + hh, :n_keys, :] = p.astype(BF16)
            inv_scr[2 * n + hh] = 1.0 / l
            s_ref[0, 0, hh:hh + 1, n * DIL_BLOCK:(n + 1) * DIL_BLOCK] = m + jnp.log(l)

    for n in range(n_blocks):
        first, k_rows = key_rows(n)
        n_keys = k_rows.stop - k_rows.start
        outs = []
        for hh in range(2):
            vt = vt_scr[hh * HEAD_DIM:(hh + 1) * HEAD_DIM, k_rows]
            o = jnp.dot(vt, p_scr[2 * n + hh, :n_keys, :], preferred_element_type=F32)
            outs.append(o * inv_scr[2 * n + hh])
        o_ref[n * DIL_BLOCK:(n + 1) * DIL_BLOCK, :] = jnp.concatenate(outs, axis=0).T.astype(o_ref.dtype)


def _band_attention(qk, v, bias_t, batch, seq, sub_len):
    t = v.shape[0]
    return pl.pallas_call(
        functools.partial(_band_body, sub_len=sub_len),
        out_shape=(jax.ShapeDtypeStruct((t, D_MODEL), BF16),
                   jax.ShapeDtypeStruct((batch, HEAD_PAIRS, 2, seq), F32)),
        grid=(batch, HEAD_PAIRS),
        in_specs=[
            pl.BlockSpec((None, seq, LANES), lambda b, p: (0, b, p)),
            pl.BlockSpec((None, seq, LANES), lambda b, p: (1, b, p)),
            pl.BlockSpec((seq, LANES), lambda b, p: (b, p)),
            pl.BlockSpec((2, 2 * DIL_BLOCK, DIL_BLOCK), lambda b, p: (p, 0, 0)),
        ],
        out_specs=(pl.BlockSpec((seq, LANES), lambda b, p: (b, p)),
                   pl.BlockSpec((1, 1, 2, seq), lambda b, p: (b, p, 0, 0))),
        scratch_shapes=[pltpu.VMEM((LANES, seq), BF16),
                        pltpu.VMEM((2 * seq // DIL_BLOCK, 2 * DIL_BLOCK, DIL_BLOCK), F32),
                        pltpu.VMEM((2 * seq // DIL_BLOCK, 2 * DIL_BLOCK, DIL_BLOCK), BF16),
                        pltpu.VMEM((2 * seq // DIL_BLOCK, 1, DIL_BLOCK), F32)],
        compiler_params=_params("parallel", "parallel"),
        name="band_attention",
    )(qk, qk, v, bias_t)


def _softmax_merge_body(o1_ref, o2_ref, o3_ref, s_ref, e_ref, o_ref):
    s = s_ref[...]
    groups = [s, pltpu.roll(s, LANES - N_HEADS, axis=1), pltpu.roll(s, LANES - 2 * N_HEADS, axis=1)]
    top = jnp.maximum(jnp.maximum(groups[0], groups[1]), groups[2])
    weights = [jnp.exp(g - top) for g in groups]
    inv = 1.0 / (weights[0] + weights[1] + weights[2])
    out = jnp.zeros(o_ref.shape, F32)
    for w, part in zip(weights, (o1_ref, o2_ref, o3_ref)):
        spread = jnp.dot((w * inv).astype(BF16), e_ref[...], preferred_element_type=F32)
        out = out + spread * part[...].astype(F32)
    o_ref[...] = out.astype(o_ref.dtype)


def _softmax_merge(outs, s_all):
    t, d = outs[0].shape
    head_of = jnp.arange(d) // HEAD_DIM
    expand = (jnp.arange(LANES)[:, None] == head_of[None, :]).astype(BF16)
    rows = pl.BlockSpec((ROW_TILE, d), lambda i: (i, 0))
    return pl.pallas_call(
        _softmax_merge_body,
        out_shape=jax.ShapeDtypeStruct((t, d), BF16),
        grid=(t // ROW_TILE,),
        in_specs=[rows, rows, rows,
                  pl.BlockSpec((ROW_TILE, LANES), lambda i: (i, 0)),
                  pl.BlockSpec((LANES, d), lambda i: (0, 0))],
        out_specs=rows,
        compiler_params=_params("parallel"),
        name="softmax_merge",
    )(*outs, s_all, expand)


def _t5_causal_bucket(distance):
    max_exact = N_REL_BUCKETS // 2
    d = jnp.maximum(distance, 1).astype(F32)
    log_b = max_exact + (jnp.log(d / max_exact) / math.log(REL_MAX_DISTANCE / max_exact)
                         * (N_REL_BUCKETS - max_exact)).astype(jnp.int32)
    log_b = jnp.minimum(log_b, N_REL_BUCKETS - 1)
    return jnp.where(distance < max_exact, distance, log_b)


def _dilated_bias(rel_bias):
    kj = jnp.arange(2 * DIL_BLOCK, dtype=jnp.int32)
    qi = jnp.arange(DIL_BLOCK, dtype=jnp.int32)
    delta = qi[None, :] + DIL_BLOCK - kj[:, None]
    in_band = (delta >= 0) & (delta <= DIL_SPAN)
    buckets = jnp.stack([_t5_causal_bucket(jnp.maximum(delta, 0) * dil) for _, dil in DILATED_PAIRS])
    one_hot = (buckets[..., None] == jnp.arange(N_REL_BUCKETS)).astype(F32)
    bias = jnp.einsum("gkqb,bh->ghkq", one_hot, rel_bias.astype(F32), precision=lax.Precision.HIGHEST)
    return jnp.where(in_band[None, None], bias, NEG_INF)


def _dilated_proj_body(x_ref, g_ref, w_ref, cs_ref, bd_ref, *rest):
    n_groups = len(DILATED_PAIRS)
    qk_refs, v_refs = rest[:n_groups], rest[n_groups:2 * n_groups]
    h_scr, res_scr = rest[2 * n_groups:]
    j = pl.program_id(1)

    @pl.when(j == 0)
    def _():
        h_scr[...] = _rms_normalize(x_ref[...], g_ref[...]).astype(BF16)

    acc = jnp.dot(h_scr[...], w_ref[...], preferred_element_type=F32)

    def keep(res, first_lane):
        for c in range(res.shape[1] // LANES):
            res_scr[first_lane // LANES + c] = res[:, c * LANES:(c + 1) * LANES]

    @pl.when(j < 2 * n_groups)
    def _():
        for c in range(COL_TILE // NORM_CHUNK):
            sl = slice(c * NORM_CHUNK, (c + 1) * NORM_CHUNK)
            a = acc[:, sl]
            ss = jnp.dot((a * a).astype(BF16), bd_ref[...], preferred_element_type=F32)
            keep(a * lax.rsqrt(ss * (1.0 / HEAD_DIM) + RMS_EPS) * cs_ref[:, sl], c * NORM_CHUNK)

    @pl.when(j == 2 * n_groups)
    def _():
        keep(acc * cs_ref[...], 0)

    def write_classes(ref, dil):
        rows = res_scr.shape[1] // dil
        for c in range(res_scr.shape[0]):
            lanes = slice(c * LANES, (c + 1) * LANES)
            if dil == 1:
                ref[:, lanes] = res_scr[c].astype(ref.dtype)
            else:
                for r in range(dil):
                    ref[r, :, lanes] = res_scr[c, pl.ds(r, rows, stride=dil), :].astype(ref.dtype)

    for g, (_, dil) in enumerate(DILATED_PAIRS):
        pl.when(j // 2 == g)(functools.partial(write_classes, qk_refs[g], dil))
        pl.when(j == 2 * n_groups)(functools.partial(write_classes, v_refs[g], dil))


def _dilated_proj(x, gain, w, layer, col_scale, batch, seq):
    t, d = x.shape
    n = w.shape[2]
    n_groups = len(DILATED_PAIRS)
    tiles_per_seq = seq // DIL_ROW_TILE
    head_id = jnp.arange(NORM_CHUNK) // HEAD_DIM
    block_diag = (head_id[:, None] == head_id[None, :]).astype(BF16)
    out_shape, out_specs = [], []
    for kind in ("qk", "v"):
        for g, (_, dil) in enumerate(DILATED_PAIRS):
            rows = DIL_ROW_TILE // dil
            if kind == "qk":
                shape = (2, batch, dil, seq // dil, d)
                block = (None, None, dil, rows, d)
                index = lambda i, j, g=g: (jnp.clip(j - 2 * g, 0, 1), i // tiles_per_seq, 0, i % tiles_per_seq, 0)
            else:
                shape = (batch, dil, seq // dil, d)
                block = (None, dil, rows, d)
                index = lambda i, j: (i // tiles_per_seq, 0, i % tiles_per_seq, 0)
            if dil == 1:
                block = block[:-3] + (None,) + block[-2:]
            out_shape.append(jax.ShapeDtypeStruct(shape, BF16))
            out_specs.append(pl.BlockSpec(block, index))
    outs = pl.pallas_call(
        _dilated_proj_body,
        out_shape=tuple(out_shape),
        grid=(t // DIL_ROW_TILE, n // COL_TILE),
        in_specs=[
            pl.BlockSpec((DIL_ROW_TILE, d), lambda i, j: (i, 0)),
            pl.BlockSpec((1, d), lambda i, j: (0, 0)),
            pl.BlockSpec((None, d, COL_TILE), lambda i, j: (layer, 0, j)),
            pl.BlockSpec((1, COL_TILE), lambda i, j: (0, j)),
            pl.BlockSpec((NORM_CHUNK, NORM_CHUNK), lambda i, j: (0, 0)),
        ],
        out_specs=tuple(out_specs),
        scratch_shapes=[pltpu.VMEM((DIL_ROW_TILE, d), BF16),
                        pltpu.VMEM((COL_TILE // LANES, DIL_ROW_TILE, LANES), F32)],
        compiler_params=_params("parallel", "arbitrary"),
        name="dilated_proj",
    )(x, gain.reshape(1, d), w, col_scale.reshape(1, n).astype(F32), block_diag)
    qk = [o.reshape(2, t, d) for o in outs[:n_groups]]
    v = [o.reshape(t, d) for o in outs[n_groups:]]
    return qk, v


def _dilated_attention(qk, v, rel_bias, batch, seq):
    t = v[0].shape[0]
    n_groups = len(DILATED_PAIRS)
    bias_t = _dilated_bias(rel_bias)
    outs, stats = [], []
    for g, (_, dil) in enumerate(DILATED_PAIRS):
        sub_len = seq // dil
        o, s = _band_attention(qk[g], v[g], bias_t[g], batch, seq, sub_len)
        o = o.reshape(batch, dil, sub_len, D_MODEL).transpose(0, 2, 1, 3).reshape(t, D_MODEL)
        s = s.reshape(batch, HEAD_PAIRS, 2, dil, sub_len).swapaxes(3, 4)
        outs.append(o)
        stats.append(s.reshape(batch, N_HEADS, seq).transpose(0, 2, 1).reshape(t, N_HEADS))
    stats.append(jnp.zeros((t, LANES - n_groups * N_HEADS), F32))
    return _softmax_merge(outs, jnp.concatenate(stats, axis=1))


def _tile_heads(v):
    return jnp.tile(v.astype(F32), N_HEADS)


def _sb_mixer(x, gain, w_qkv, layer, batch, seq):
    ones = jnp.ones((D_MODEL,), F32)
    col_scale = jnp.concatenate([ones * QK_SCALE, ones, ones])
    proj = _norm_proj(x, gain, w_qkv, layer, col_scale, 0)
    return _sb_attention(proj, batch, seq)


def _dilated_mixer(x, gain, w_in, q_norm, k_norm, rel_bias, layer, batch, seq):
    scales = []
    for g in range(len(DILATED_PAIRS)):
        scales += [_tile_heads(q_norm[g]) * QK_SCALE, _tile_heads(k_norm[g])]
    scales.append(jnp.ones((D_MODEL,), F32))
    qk, v = _dilated_proj(x, gain, w_in, layer, jnp.concatenate(scales), batch, seq)
    return _dilated_attention(qk, v, rel_bias, batch, seq)


def _fox_mixer(x, gain, w_qkv, w_gate, b_f, q_norm, k_norm, layer, batch, seq):
    col_scale = jnp.concatenate([_tile_heads(q_norm) * QK_SCALE, _tile_heads(k_norm),
                                 jnp.ones((D_MODEL,), F32)])
    proj = _norm_proj(x, gain, w_qkv, layer, col_scale, 2)
    cum = _fox_cum_log_forget(x, gain, w_gate, b_f, batch, seq)
    return _fox_attention(proj, cum, batch, seq)


def kernel(x, sb_w_qkv, sb_w_o, dil_w_in, dil_q_norm, dil_k_norm, dil_w_o, fox_w_in, fox_b_f,
           fox_q_norm, fox_k_norm, fox_w_o, rel_bias, attn_norm, ffn_norm, mlp_w_gate, mlp_w_up,
           mlp_w_down, moe_router, moe_w_gate, moe_w_up, moe_w_down):
    batch, seq, d = x.shape
    depth = attn_norm.shape[0]
    assert d == D_MODEL and seq % max(ROW_TILE, DIL_ROW_TILE, ATT_Q_BLOCK, SB_Q_BLOCK) == 0
    assert all((seq // dil) % DIL_BLOCK == 0 for _, dil in DILATED_PAIRS)
    sb_w_qkv, sb_w_o, dil_w_in, dil_w_o, fox_w_o, mlp_w_gate, mlp_w_up, mlp_w_down = (
        w.astype(BF16) for w in (sb_w_qkv, sb_w_o, dil_w_in, dil_w_o, fox_w_o, mlp_w_gate, mlp_w_up,
                                 mlp_w_down))
    moe_w_gate, moe_w_up, moe_w_down = (w.astype(BF16) for w in (moe_w_gate, moe_w_up, moe_w_down))
    fox_w_qkv = fox_w_in[:, :, :3 * D_MODEL].astype(BF16)
    h = x.reshape(batch * seq, d)
    for i in range(depth):
        kind, j = i % 3, i // 3
        if kind == 0:
            mixed, w_o = _sb_mixer(h, attn_norm[i], sb_w_qkv, j, batch, seq), sb_w_o
        elif kind == 1:
            mixed, w_o = _dilated_mixer(h, attn_norm[i], dil_w_in, dil_q_norm[j], dil_k_norm[j], rel_bias,
                                        j, batch, seq), dil_w_o
        else:
            mixed, w_o = _fox_mixer(h, attn_norm[i], fox_w_qkv, fox_w_in[j, :, 3 * D_MODEL:], fox_b_f[j],
                                    fox_q_norm[j], fox_k_norm[j], j, batch, seq), fox_w_o
        f = i // 2
        if i % 2 == 0:
            h = _mix_ffn_residual(h, mixed, w_o, j, ffn_norm[i], mlp_w_gate, mlp_w_up, mlp_w_down, f)
        else:
            h = _mix_moe_residual(h, mixed, w_o, j, ffn_norm[i], moe_router[f], moe_w_gate, moe_w_up,
                                  moe_w_down, f)
    return h.reshape(batch, seq, d)
```

```python
import functools
import math

import jax
import jax.numpy as jnp
from jax import lax
from jax.experimental import pallas as pl
from jax.experimental.pallas import tpu as pltpu
from jax.experimental.pallas import tpu_sc as plsc

D_MODEL = 1024
N_HEADS = 16
HEAD_DIM = 64
LANES = 128
HEAD_PAIRS = D_MODEL // LANES
N_EXPERTS = 8
N_REL_BUCKETS = 32
REL_MAX_DISTANCE = 2048
DILATED_PAIRS = ((128, 1), (512, 4), (2048, 16))
DIL_SPAN = 128
RMS_EPS = 1e-6
NEG_INF = -1e30
SB_UNDERFLOW_LOG = -104.0
QK_SCALE = 1.0 / math.sqrt(HEAD_DIM)

ROW_TILE = 1024
COL_TILE = 1024
FF_TILE = 1792
SWIGLU_CHUNK = 256
MOE_ROW_TILE = 512
MOE_FF_TILE = 1792
SC_CORES = 2
SC_SUBCORES = 16
SC_CHUNK_ROWS = 64
NORM_CHUNK = 256
ATT_BLOCK = 256
ATT_Q_BLOCK = 512
ATT_K_PER_Q = ATT_Q_BLOCK // ATT_BLOCK
SB_Q_BLOCK = 2048
SB_SUB_BLOCKS = SB_Q_BLOCK // ATT_BLOCK
DIL_BLOCK = 128
DIL_ROW_TILE = 1024
VMEM_LIMIT = 56 * 1024 * 1024

F32 = jnp.float32
BF16 = jnp.bfloat16


def _params(*semantics):
    return pltpu.CompilerParams(dimension_semantics=semantics, vmem_limit_bytes=VMEM_LIMIT)


def _rms_normalize(x, gain):
    inv = lax.rsqrt(jnp.mean(x * x, axis=-1, keepdims=True) + RMS_EPS)
    return x * inv * gain


def _split_bf16(x, terms):
    parts = []
    for _ in range(terms):
        part = x.astype(BF16)
        parts.append(part)
        x = x - part.astype(F32)
    return parts


def _dot_split(a, b):
    a_hi, a_lo = _split_bf16(a, 2)
    b_hi, b_lo = _split_bf16(b, 2)
    return (jnp.dot(a_hi, b_hi, preferred_element_type=F32)
            + (jnp.dot(a_hi, b_lo, preferred_element_type=F32)
               + jnp.dot(a_lo, b_hi, preferred_element_type=F32)))


def _softplus(z):
    return jnp.maximum(z, 0.0) + jnp.log(1.0 + jnp.exp(-jnp.abs(z)))


def _norm_proj_body(x_ref, g_ref, w_ref, cs_ref, bd_ref, o_ref, h_scr, *, n_norm):
    j = pl.program_id(1)

    @pl.when(j == 0)
    def _():
        h_scr[...] = _rms_normalize(x_ref[...], g_ref[...]).astype(BF16)

    acc = jnp.dot(h_scr[...], w_ref[...], preferred_element_type=F32)

    def plain():
        o_ref[...] = (acc * cs_ref[...]).astype(o_ref.dtype)

    def head_normed():
        for c in range(COL_TILE // NORM_CHUNK):
            sl = slice(c * NORM_CHUNK, (c + 1) * NORM_CHUNK)
            a = acc[:, sl]
            ss = jnp.dot((a * a).astype(BF16), bd_ref[...], preferred_element_type=F32)
            inv = lax.rsqrt(ss * (1.0 / HEAD_DIM) + RMS_EPS)
            o_ref[:, sl] = (a * inv * cs_ref[:, sl]).astype(o_ref.dtype)

    if n_norm == 0:
        plain()
    else:
        pl.when(j < n_norm)(head_normed)
        pl.when(j >= n_norm)(plain)


def _norm_proj(x, gain, w, layer, col_scale, n_norm):
    t, d = x.shape
    n = w.shape[2]
    head_id = jnp.arange(NORM_CHUNK) // HEAD_DIM
    block_diag = (head_id[:, None] == head_id[None, :]).astype(BF16)
    return pl.pallas_call(
        functools.partial(_norm_proj_body, n_norm=n_norm),
        out_shape=jax.ShapeDtypeStruct((t, n), BF16),
        grid=(t // ROW_TILE, n // COL_TILE),
        in_specs=[
            pl.BlockSpec((ROW_TILE, d), lambda i, j: (i, 0)),
            pl.BlockSpec((1, d), lambda i, j: (0, 0)),
            pl.BlockSpec((None, d, COL_TILE), lambda i, j: (layer, 0, j)),
            pl.BlockSpec((1, COL_TILE), lambda i, j: (0, j)),
            pl.BlockSpec((NORM_CHUNK, NORM_CHUNK), lambda i, j: (0, 0)),
        ],
        out_specs=pl.BlockSpec((ROW_TILE, COL_TILE), lambda i, j: (i, j)),
        scratch_shapes=[pltpu.VMEM((ROW_TILE, d), BF16)],
        compiler_params=_params("parallel", "arbitrary"),
        name="norm_proj",
    )(x, gain.reshape(1, d), w, col_scale.reshape(1, n).astype(F32), block_diag)


def _swiglu_hidden(h, wg, wu):
    g = jnp.dot(h, wg, preferred_element_type=F32)
    u = jnp.dot(h, wu, preferred_element_type=F32)
    return g * (1.0 / (1.0 + jnp.exp(-g))) * u


def _swiglu_down(h, wg_ref, wu_ref, wd_ref):
    y = None
    for c in range(wg_ref.shape[1] // SWIGLU_CHUNK):
        cols = slice(c * SWIGLU_CHUNK, (c + 1) * SWIGLU_CHUNK)
        a = _swiglu_hidden(h, wg_ref[:, cols], wu_ref[:, cols]).astype(BF16)
        part = jnp.dot(a, wd_ref[cols, :], preferred_element_type=F32)
        y = part if y is None else y + part
    return y


def _ffn_body(x_ref, o_ref, wo_ref, g_ref, wg_ref, wu_ref, wd_ref, y_ref, h_scr):
    f = pl.program_id(1)

    @pl.when(f == 0)
    def _():
        x = x_ref[...] + jnp.dot(o_ref[...], wo_ref[...], preferred_element_type=F32)
        h_scr[...] = _rms_normalize(x, g_ref[...]).astype(BF16)
        y_ref[...] = x

    y_ref[...] += _swiglu_down(h_scr[...], wg_ref, wu_ref, wd_ref)


def _mix_ffn_residual(x, o, w_o, mix_layer, gain, w_gate, w_up, w_down, layer):
    t, d = x.shape
    ff = w_gate.shape[2]
    return pl.pallas_call(
        _ffn_body,
        out_shape=jax.ShapeDtypeStruct((t, d), F32),
        grid=(t // ROW_TILE, ff // FF_TILE),
        in_specs=[
            pl.BlockSpec((ROW_TILE, d), lambda i, f: (i, 0)),
            pl.BlockSpec((ROW_TILE, d), lambda i, f: (i, 0)),
            pl.BlockSpec((None, d, d), lambda i, f: (mix_layer, 0, 0)),
            pl.BlockSpec((1, d), lambda i, f: (0, 0)),
            pl.BlockSpec((None, d, FF_TILE), lambda i, f: (layer, 0, f)),
            pl.BlockSpec((None, d, FF_TILE), lambda i, f: (layer, 0, f)),
            pl.BlockSpec((None, FF_TILE, d), lambda i, f: (layer, f, 0)),
        ],
        out_specs=pl.BlockSpec((ROW_TILE, d), lambda i, f: (i, 0)),
        scratch_shapes=[pltpu.VMEM((ROW_TILE, d), BF16)],
        compiler_params=_params("parallel", "arbitrary"),
        name="ffn",
    )(x, o, w_o, gain.reshape(1, d), w_gate, w_up, w_down)


def _pack_bf16_pairs(x):
    half = x.shape[1] // 2
    bits = pltpu.bitcast(x.astype(BF16).astype(F32), jnp.int32)
    return bits[:, :half] | lax.shift_right_logical(bits[:, half:], jnp.int32(16))


def _unpack_bf16_pairs(p):
    left = pltpu.bitcast(p & jnp.int32(-65536), F32)
    right = pltpu.bitcast(lax.shift_left(p, jnp.int32(16)), F32)
    return jnp.concatenate([left, right], axis=1).astype(BF16)


def _router_body(x_ref, o_ref, wo_ref, g_ref, r_ref, tri_ref, gates_ref, rank_ref, h_ref, x1_ref, count_scr):
    @pl.when(pl.program_id(0) == 0)
    def _():
        count_scr[...] = jnp.zeros_like(count_scr)

    x1 = x_ref[...] + jnp.dot(o_ref[...], wo_ref[...], preferred_element_type=F32)
    x1_ref[...] = x1
    h = _rms_normalize(x1, g_ref[...])
    h_ref[...] = _pack_bf16_pairs(h)
    logits = _dot_split(h, r_ref[...])
    lane = lax.broadcasted_iota(jnp.int32, logits.shape, 1).astype(F32)
    logits = jnp.where(lane < N_EXPERTS, logits, -jnp.inf)
    m1 = jnp.max(logits, axis=-1, keepdims=True)
    i1 = jnp.min(jnp.where(logits == m1, lane, float(LANES)), axis=-1, keepdims=True)
    rest = jnp.where(lane == i1, -jnp.inf, logits)
    m2 = jnp.max(rest, axis=-1, keepdims=True)
    i2 = jnp.min(jnp.where(rest == m2, lane, float(LANES)), axis=-1, keepdims=True)
    e = jnp.exp(m2 - m1)
    g1 = 1.0 / (1.0 + e)
    gates = jnp.where(lane == i1, g1, 0.0) + jnp.where(lane == i2, e * g1, 0.0)
    gates_ref[...] = gates.T[:N_EXPERTS, :]
    chosen = jnp.where((lane == i1) | (lane == i2), 1.0, 0.0)
    inclusive = jnp.dot(tri_ref[...], chosen.astype(BF16), preferred_element_type=F32)
    rank = jnp.where(chosen > 0.0, inclusive - 1.0 + count_scr[...], -1.0)
    rank_ref[...] = rank.T[:N_EXPERTS, :]
    count_scr[...] += inclusive[ROW_TILE - 1:ROW_TILE, :]


def _mix_router(x, o, w_o, mix_layer, gain, router):
    t, d = x.shape
    r = jnp.zeros((d, LANES), F32).at[:, :N_EXPERTS].set(router)
    idx = jnp.arange(ROW_TILE)
    tri = (idx[:, None] >= idx[None, :]).astype(BF16)
    rows = pl.BlockSpec((ROW_TILE, d), lambda i: (i, 0))
    return pl.pallas_call(
        _router_body,
        out_shape=(jax.ShapeDtypeStruct((N_EXPERTS, t), F32), jax.ShapeDtypeStruct((N_EXPERTS, t), F32),
                   jax.ShapeDtypeStruct((t, d // 2), jnp.int32), jax.ShapeDtypeStruct((t, d), F32)),
        grid=(t // ROW_TILE,),
        in_specs=[
            rows,
            rows,
            pl.BlockSpec((None, d, d), lambda i: (mix_layer, 0, 0)),
            pl.BlockSpec((1, d), lambda i: (0, 0)),
            pl.BlockSpec((d, LANES), lambda i: (0, 0)),
            pl.BlockSpec((ROW_TILE, ROW_TILE), lambda i: (0, 0)),
        ],
        out_specs=(pl.BlockSpec((N_EXPERTS, ROW_TILE), lambda i: (0, i)),
                   pl.BlockSpec((N_EXPERTS, ROW_TILE), lambda i: (0, i)),
                   pl.BlockSpec((ROW_TILE, d // 2), lambda i: (i, 0)),
                   rows),
        scratch_shapes=[pltpu.VMEM((1, LANES), F32)],
        compiler_params=_params("arbitrary"),
        name="router",
    )(x, o, w_o, gain.reshape(1, d), r, tri)


def _sc_mesh():
    return plsc.VectorSubcoreMesh(core_axis_name="core", subcore_axis_name="subcore",
                                  num_cores=SC_CORES, num_subcores=SC_SUBCORES)


def _sc_worker_base(per_worker):
    return (lax.axis_index("subcore") * SC_CORES + lax.axis_index("core")) * per_worker


def _sc_row_gather(table, idx):
    width = table.shape[1]
    n = idx.shape[0]
    per_worker = n // (SC_CORES * SC_SUBCORES)
    n_chunks = per_worker // SC_CHUNK_ROWS
    assert n == n_chunks * SC_CHUNK_ROWS * SC_CORES * SC_SUBCORES and n_chunks % 2 == 0

    @functools.partial(
        pl.kernel, mesh=_sc_mesh(), out_type=jax.ShapeDtypeStruct((n, width), table.dtype),
        scratch_types=[pltpu.VMEM((SC_CHUNK_ROWS,), jnp.int32), pltpu.VMEM((SC_CHUNK_ROWS,), jnp.int32),
                       pltpu.VMEM((SC_CHUNK_ROWS, width), table.dtype),
                       pltpu.VMEM((SC_CHUNK_ROWS, width), table.dtype),
                       pltpu.SemaphoreType.DMA, pltpu.SemaphoreType.DMA],
        name="sc_row_gather")
    def gather(table_hbm, idx_hbm, out_hbm, idx_a, idx_b, rows_a, rows_b, sem_a, sem_b):
        base = _sc_worker_base(per_worker)
        bufs = ((idx_a, rows_a, sem_a), (idx_b, rows_b, sem_b))

        def rows_of(c):
            return pl.ds(pl.multiple_of(base + c * SC_CHUNK_ROWS, SC_CHUNK_ROWS), SC_CHUNK_ROWS)

        def fetch(c, buf):
            idx_v, rows_v, sem = buf
            pltpu.sync_copy(idx_hbm.at[rows_of(c)], idx_v)
            return pltpu.make_async_copy(table_hbm.at[idx_v], rows_v, sem)

        fetch(0, bufs[0]).start()

        @pl.loop(0, n_chunks, step=2)
        def _(c):
            for b in range(2):
                idx_v, rows_v, sem = bufs[b]
                pltpu.make_async_copy(table_hbm.at[idx_v], rows_v, sem).wait()

                @pl.when(c + b + 1 < n_chunks)
                def _():
                    fetch(c + b + 1, bufs[1 - b]).start()

                pltpu.sync_copy(rows_v, out_hbm.at[rows_of(c + b)])

    return gather(table, idx)


def _sc_row_scatter_pair(rows, idx_lo, idx_hi, n_out):
    n, width = rows.shape
    per_worker = n // (SC_CORES * SC_SUBCORES)
    n_chunks = per_worker // SC_CHUNK_ROWS
    assert n == n_chunks * SC_CHUNK_ROWS * SC_CORES * SC_SUBCORES and n_chunks % 2 == 0
    index_scratch = pltpu.VMEM((SC_CHUNK_ROWS,), jnp.int32)
    rows_scratch = pltpu.VMEM((SC_CHUNK_ROWS, width), rows.dtype)

    @functools.partial(
        pl.kernel, mesh=_sc_mesh(), out_type=jax.ShapeDtypeStruct((n_out, width), rows.dtype),
        scratch_types=[index_scratch, index_scratch, rows_scratch, rows_scratch,
                       pltpu.SemaphoreType.DMA, pltpu.SemaphoreType.DMA],
        name="sc_row_scatter")
    def scatter(rows_hbm, lo_hbm, hi_hbm, out_hbm, lo_v, hi_v, rows_a, rows_b, sem_a, sem_b):
        base = _sc_worker_base(per_worker)
        bufs = ((rows_a, sem_a), (rows_b, sem_b))

        def rows_of(c):
            return pl.ds(pl.multiple_of(base + c * SC_CHUNK_ROWS, SC_CHUNK_ROWS), SC_CHUNK_ROWS)

        def load(c, buf):
            rows_v, sem = buf
            return pltpu.make_async_copy(rows_hbm.at[rows_of(c)], rows_v, sem)

        load(0, bufs[0]).start()

        @pl.loop(0, n_chunks, step=2)
        def _(c):
            for b in range(2):
                rows_v, _ = bufs[b]
                load(c + b, bufs[b]).wait()

                @pl.when(c + b + 1 < n_chunks)
                def _():
                    load(c + b + 1, bufs[1 - b]).start()

                pltpu.sync_copy(lo_hbm.at[rows_of(c + b)], lo_v)
                pltpu.sync_copy(hi_hbm.at[rows_of(c + b)], hi_v)
                pltpu.sync_copy(rows_v, out_hbm.at[lo_v])
                pltpu.sync_copy(rows_v, out_hbm.at[hi_v])

    return scatter(rows, idx_lo, idx_hi)


def _expert_ffn_body(te_ref, nu_ref, nv_ref, h_ref, wg_ref, wu_ref, wd_ref, y_ref, acc_scr):
    i = pl.program_id(0)
    f = pl.program_id(1)

    @pl.when(i < nu_ref[0])
    def _():
        row = lax.broadcasted_iota(jnp.int32, h_ref.shape, 0)
        packed = jnp.where(row < nv_ref[i], h_ref[...], 0)
        y = _swiglu_down(_unpack_bf16_pairs(packed), wg_ref, wu_ref, wd_ref)

        @pl.when(f == 0)
        def _():
            acc_scr[...] = y

        @pl.when(f > 0)
        def _():
            acc_scr[...] += y

        @pl.when(f == pl.num_programs(1) - 1)
        def _():
            y_ref[...] = _pack_bf16_pairs(acc_scr[...])


def _expert_ffn(h_sorted, tile_expert, n_used, tile_valid, w_gate, w_up, w_down, layer):
    rows, half = h_sorted.shape
    d = 2 * half
    ff = w_gate.shape[3]
    n_f = ff // MOE_FF_TILE

    def row_map(i, f, te, nu, nv):
        return (jnp.minimum(i, nu[0] - 1), 0)

    def col_step(i, f, nu):
        return jnp.where(i < nu[0], f, n_f - 1)

    grid_spec = pltpu.PrefetchScalarGridSpec(
        num_scalar_prefetch=3,
        grid=(rows // MOE_ROW_TILE, n_f),
        in_specs=[
            pl.BlockSpec((MOE_ROW_TILE, half), row_map),
            pl.BlockSpec((None, None, d, MOE_FF_TILE),
                         lambda i, f, te, nu, nv: (layer, te[i], 0, col_step(i, f, nu))),
            pl.BlockSpec((None, None, d, MOE_FF_TILE),
                         lambda i, f, te, nu, nv: (layer, te[i], 0, col_step(i, f, nu))),
            pl.BlockSpec((None, None, MOE_FF_TILE, d),
                         lambda i, f, te, nu, nv: (layer, te[i], col_step(i, f, nu), 0)),
        ],
        out_specs=pl.BlockSpec((MOE_ROW_TILE, half), row_map),
        scratch_shapes=[pltpu.VMEM((MOE_ROW_TILE, d), F32)],
    )
    return pl.pallas_call(
        _expert_ffn_body,
        out_shape=jax.ShapeDtypeStruct((rows, half), jnp.int32),
        grid_spec=grid_spec,
        compiler_params=_params("arbitrary", "arbitrary"),
        name="expert_ffn",
    )(tile_expert, n_used, tile_valid, h_sorted, w_gate, w_up, w_down)


def _combine_body(x_ref, y_ref, g_ref, o_ref):
    out = x_ref[...]
    for s in range(2):
        column = jnp.broadcast_to(g_ref[s:s + 1, :], (LANES, g_ref.shape[1])).T
        gate = jnp.concatenate([column] * (out.shape[1] // LANES), axis=1)
        out = out + _unpack_bf16_pairs(y_ref[s]).astype(F32) * gate
    o_ref[...] = out


def _combine_residual(x, y_pairs, gates2):
    t, d = x.shape
    return pl.pallas_call(
        _combine_body,
        out_shape=jax.ShapeDtypeStruct((t, d), F32),
        grid=(t // ROW_TILE,),
        in_specs=[
            pl.BlockSpec((ROW_TILE, d), lambda i: (i, 0)),
            pl.BlockSpec((2, ROW_TILE, d // 2), lambda i: (0, i, 0)),
            pl.BlockSpec((2, ROW_TILE), lambda i: (0, i)),
        ],
        out_specs=pl.BlockSpec((ROW_TILE, d), lambda i: (i, 0)),
        compiler_params=_params("parallel"),
        name="moe_combine",
    )(x, y_pairs, gates2)


def _mix_moe_residual(x, o, w_o, mix_layer, gain, router, w_gate, w_up, w_down, layer):
    t, d = x.shape
    gates, rank, h_packed, x = _mix_router(x, o, w_o, mix_layer, gain, router)
    rank8 = rank.astype(jnp.int32)
    chosen = rank8 >= 0
    counts = jnp.sum(chosen, axis=1, dtype=jnp.int32)
    padded = (counts + MOE_ROW_TILE - 1) // MOE_ROW_TILE * MOE_ROW_TILE
    ends = jnp.cumsum(padded)
    starts = ends - padded
    pos = starts[:, None] + rank8
    max_rows = 2 * t + N_EXPERTS * MOE_ROW_TILE
    pos_lo = jnp.min(jnp.where(chosen, pos, max_rows), axis=0)
    pos_hi = jnp.max(jnp.where(chosen, pos, -1), axis=0)
    gates2 = jnp.stack([jnp.sum(jnp.where(chosen & (pos == pos_lo[None, :]), gates, 0.0), axis=0),
                        jnp.sum(jnp.where(chosen & (pos == pos_hi[None, :]), gates, 0.0), axis=0)])
    n_tiles = max_rows // MOE_ROW_TILE
    n_used = (ends[-1] // MOE_ROW_TILE).astype(jnp.int32)
    tile_start = jnp.minimum(jnp.arange(n_tiles, dtype=jnp.int32), n_used - 1) * MOE_ROW_TILE
    tile_expert = jnp.sum(tile_start[:, None] >= ends[None, :], axis=1, dtype=jnp.int32)
    tile_valid = jnp.clip((starts + counts)[tile_expert] - tile_start, 0, MOE_ROW_TILE).astype(jnp.int32)

    h_sorted = _sc_row_scatter_pair(h_packed, pos_lo, pos_hi, max_rows)
    y_sorted = _expert_ffn(h_sorted, tile_expert, n_used.reshape(1), tile_valid, w_gate, w_up, w_down, layer)
    y_pairs = _sc_row_gather(y_sorted, jnp.concatenate([pos_lo, pos_hi])).reshape(2, t, d // 2)
    return _combine_residual(x, y_pairs, gates2)


def _split_head_pair(q):
    is_first = lax.broadcasted_iota(jnp.int32, (1, LANES), 1) < HEAD_DIM
    zero = jnp.zeros_like(q)
    return jnp.where(is_first, q, zero), jnp.where(is_first, zero, q)


def _qk(q, k):
    return lax.dot_general(q, k, (((1,), (1,)), ((), ())), preferred_element_type=F32)


def _transpose_values(v_ref, vt_scr):
    for c in range(v_ref.shape[0] // ATT_BLOCK):
        rows = slice(c * ATT_BLOCK, (c + 1) * ATT_BLOCK)
        vt_scr[:, rows] = v_ref[rows, :].astype(F32).T.astype(vt_scr.dtype)


def _sb_body(q_ref, k_ref, v_ref, u_ref, o_ref, vt_scr, acc_scr, carry_scr, z_scr, keep_scr, sum_scr):
    qi = pl.program_id(2)

    @pl.when(qi == 0)
    def _():
        _transpose_values(v_ref, vt_scr)

    upper = u_ref[...]
    acc_scr[...] = jnp.zeros_like(acc_scr)
    carry_scr[...] = jnp.zeros_like(carry_scr)
    key = lax.broadcasted_iota(jnp.int32, (ATT_BLOCK, ATT_BLOCK), 0)
    query = lax.broadcasted_iota(jnp.int32, (ATT_BLOCK, ATT_BLOCK), 1)
    strict = key < query
    qs = [_split_head_pair(q_ref[j * ATT_BLOCK:(j + 1) * ATT_BLOCK, :]) for j in range(SB_SUB_BLOCKS)]

    def process(j, kb, keep):
        start = pl.multiple_of(kb * ATT_BLOCK, ATT_BLOCK)
        k = k_ref[pl.ds(start, ATT_BLOCK), :]
        for hh in range(2):
            z = _qk(k, qs[j][hh])
            log_beta = z - _softplus(z)
            log_keep = log_beta - z
            if keep is not None:
                log_keep = jnp.where(keep, log_keep, 0.0)
            remain = jnp.dot(upper, log_keep.astype(BF16), preferred_element_type=F32)
            w = jnp.exp(log_beta + remain + carry_scr[j, hh])
            if keep is not None:
                w = jnp.where(keep, w, 0.0)
            carry_scr[j, hh] += jnp.sum(log_keep, axis=0, keepdims=True)
            vt = vt_scr[pl.ds(hh * HEAD_DIM, HEAD_DIM), pl.ds(start, ATT_BLOCK)]
            acc_scr[j, hh] += jnp.dot(vt, w.astype(BF16), preferred_element_type=F32)

    units = []
    for j in range(SB_SUB_BLOCKS):
        g = qi * SB_SUB_BLOCKS + j
        has_previous = None if j > 0 else jnp.broadcast_to(g > 0, strict.shape)
        for kb, keep in ((g, strict), (jnp.maximum(g - 1, 0), has_previous)):
            for hh in range(2):
                units.append((j, hh, pl.multiple_of(kb * ATT_BLOCK, ATT_BLOCK), keep))

    for u, (j, hh, start, keep) in enumerate(units):
        z_scr[u] = _qk(k_ref[pl.ds(start, ATT_BLOCK), :], qs[j][hh])

    for u, (j, hh, start, keep) in enumerate(units):
        z = z_scr[u]
        log_beta = z - _softplus(z)
        log_keep = log_beta - z
        if keep is not None:
            log_keep = jnp.where(keep, log_keep, 0.0)
        z_scr[u] = log_beta
        keep_scr[u] = log_keep.astype(BF16)
        sum_scr[u] = jnp.sum(log_keep, axis=0, keepdims=True)

    for u in range(len(units)):
        z_scr[u] += jnp.dot(upper, keep_scr[u], preferred_element_type=F32)

    for u, (j, hh, start, keep) in enumerate(units):
        diagonal = u % 4 < 2
        log_w = z_scr[u] if diagonal else z_scr[u] + sum_scr[u - 2]
        w = jnp.exp(log_w)
        if keep is not None:
            w = jnp.where(keep, w, 0.0)
        keep_scr[u] = w.astype(BF16)

    for u, (j, hh, start, keep) in enumerate(units):
        vt = vt_scr[pl.ds(hh * HEAD_DIM, HEAD_DIM), pl.ds(start, ATT_BLOCK)]
        acc_scr[j, hh] += jnp.dot(vt, keep_scr[u], preferred_element_type=F32)
        carry_scr[j, hh] += sum_scr[u]

    def sweep_earlier():
        for j in range(SB_SUB_BLOCKS):
            def more(kb, j=j):
                return (kb >= 0) & (jnp.max(carry_scr[j]) > SB_UNDERFLOW_LOG)

            def step(kb, j=j):
                process(j, kb, None)
                return kb - 1

            lax.while_loop(more, step, qi * SB_SUB_BLOCKS + j - 2)

    pl.when(jnp.max(carry_scr[...]) > SB_UNDERFLOW_LOG)(sweep_earlier)

    for j in range(SB_SUB_BLOCKS):
        out_t = jnp.concatenate([acc_scr[j, 0], acc_scr[j, 1]], axis=0)
        o_ref[j * ATT_BLOCK:(j + 1) * ATT_BLOCK, :] = out_t.T.astype(o_ref.dtype)


def _sb_attention(proj, batch, seq):
    t = proj.shape[0]
    nq = seq // SB_Q_BLOCK
    idx = jnp.arange(ATT_BLOCK)
    upper = (idx[None, :] > idx[:, None]).astype(BF16)
    return pl.pallas_call(
        _sb_body,
        out_shape=jax.ShapeDtypeStruct((t, D_MODEL), BF16),
        grid=(batch, HEAD_PAIRS, nq),
        in_specs=[
            pl.BlockSpec((SB_Q_BLOCK, LANES), lambda b, p, i: (b * nq + i, p)),
            pl.BlockSpec((seq, LANES), lambda b, p, i: (b, HEAD_PAIRS + p)),
            pl.BlockSpec((seq, LANES), lambda b, p, i: (b, 2 * HEAD_PAIRS + p)),
            pl.BlockSpec((ATT_BLOCK, ATT_BLOCK), lambda b, p, i: (0, 0)),
        ],
        out_specs=pl.BlockSpec((SB_Q_BLOCK, LANES), lambda b, p, i: (b * nq + i, p)),
        scratch_shapes=[
            pltpu.VMEM((LANES, seq), BF16),
            pltpu.VMEM((SB_SUB_BLOCKS, 2, HEAD_DIM, ATT_BLOCK), F32),
            pltpu.VMEM((SB_SUB_BLOCKS, 2, 1, ATT_BLOCK), F32),
            pltpu.VMEM((4 * SB_SUB_BLOCKS, ATT_BLOCK, ATT_BLOCK), F32),
            pltpu.VMEM((4 * SB_SUB_BLOCKS, ATT_BLOCK, ATT_BLOCK), BF16),
            pltpu.VMEM((4 * SB_SUB_BLOCKS, 1, ATT_BLOCK), F32),
        ],
        compiler_params=_params("parallel", "parallel", "arbitrary"),
        name="sb_attention",
    )(proj, proj, proj, upper)


def _fox_gate_body(x_ref, g_ref, w_ref, b_ref, tri_ref, c_ref):
    h = _rms_normalize(x_ref[...], g_ref[...])
    logits = _dot_split(h, w_ref[...])
    log_f = -_softplus(-(logits + b_ref[...]))
    seq = log_f.shape[0]
    carry = jnp.zeros((1, LANES), F32)
    for blk in range(seq // ATT_BLOCK):
        rows = slice(blk * ATT_BLOCK, (blk + 1) * ATT_BLOCK)
        c = carry
        for part in _split_bf16(log_f[rows], 3):
            c = c + jnp.dot(tri_ref[...], part, preferred_element_type=F32)
        c_ref[rows, :] = c
        carry = c[ATT_BLOCK - 1:ATT_BLOCK, :]


def _fox_cum_log_forget(x, gain, w_gate, b_gate, batch, seq):
    t, d = x.shape
    w = jnp.zeros((d, LANES), F32).at[:, :N_HEADS].set(w_gate)
    b = jnp.zeros((1, LANES), F32).at[0, :N_HEADS].set(b_gate)
    idx = jnp.arange(ATT_BLOCK)
    tri = (idx[:, None] >= idx[None, :]).astype(BF16)
    return pl.pallas_call(
        _fox_gate_body,
        out_shape=jax.ShapeDtypeStruct((t, LANES), F32),
        grid=(batch,),
        in_specs=[
            pl.BlockSpec((seq, d), lambda i: (i, 0)),
            pl.BlockSpec((1, d), lambda i: (0, 0)),
            pl.BlockSpec((d, LANES), lambda i: (0, 0)),
            pl.BlockSpec((1, LANES), lambda i: (0, 0)),
            pl.BlockSpec((ATT_BLOCK, ATT_BLOCK), lambda i: (0, 0)),
        ],
        out_specs=pl.BlockSpec((seq, LANES), lambda i: (i, 0)),
        compiler_params=_params("parallel"),
        name="fox_gate",
    )(x, gain.reshape(1, d), w, b, tri)


def _fox_body(q_ref, k_ref, v_ref, cq_ref, ck_ref, o_ref, vt_scr, ckb_scr, a_scr, p_scr):
    qi = pl.program_id(2)

    @pl.when(qi == 0)
    def _():
        _transpose_values(v_ref, vt_scr)
        for hh in range(2):
            for c in range(ckb_scr.shape[1] // ATT_BLOCK):
                rows = slice(c * ATT_BLOCK, (c + 1) * ATT_BLOCK)
                ckb_scr[hh, rows, :] = jnp.broadcast_to(ck_ref[0, hh, :, rows], (LANES, ATT_BLOCK)).T

    qs = _split_head_pair(q_ref[...])
    key = lax.broadcasted_iota(jnp.int32, (ATT_BLOCK, ATT_Q_BLOCK), 0)
    query = lax.broadcasted_iota(jnp.int32, (ATT_BLOCK, ATT_Q_BLOCK), 1)
    row_shape = (1, ATT_Q_BLOCK)

    def scores(kb, tops, diagonal):
        start = kb * ATT_BLOCK
        k = k_ref[pl.ds(start, ATT_BLOCK), :]
        if diagonal:
            causal = key + (kb * ATT_BLOCK - qi * ATT_Q_BLOCK) <= query
        new_tops = []
        for hh in range(2):
            ck = ckb_scr[hh, pl.ds(start, ATT_BLOCK), :]
            a = _qk(k, qs[hh]) - jnp.concatenate([ck] * (ATT_Q_BLOCK // LANES), axis=1)
            if diagonal:
                a = jnp.where(causal, a, NEG_INF)
            a_scr[hh, kb] = a
            new_tops.append(jnp.maximum(tops[hh], jnp.max(a, axis=0, keepdims=True)))
        return tuple(new_tops)

    def sweep(n_blocks):
        tops = (jnp.full(row_shape, NEG_INF, F32),) * 2
        for kb in range(n_blocks):
            tops = scores(kb, tops, kb >= n_blocks - ATT_K_PER_Q)
        shifts = tuple(cq_ref[0, hh] - (tops[hh] + cq_ref[0, hh]) for hh in range(2))
        sums = [jnp.zeros(row_shape, F32)] * 2
        for kb in range(n_blocks):
            for hh in range(2):
                p = jnp.exp(a_scr[hh, kb] + shifts[hh])
                p_scr[hh, kb] = p.astype(BF16)
                sums[hh] = sums[hh] + jnp.sum(p, axis=0, keepdims=True)
        accs = [jnp.zeros((HEAD_DIM, ATT_Q_BLOCK), F32)] * 2
        for kb in range(n_blocks):
            for hh in range(2):
                vt = vt_scr[hh * HEAD_DIM:(hh + 1) * HEAD_DIM, kb * ATT_BLOCK:(kb + 1) * ATT_BLOCK]
                accs[hh] = accs[hh] + jnp.dot(vt, p_scr[hh, kb], preferred_element_type=F32)
        out_t = jnp.concatenate([accs[0] / sums[0], accs[1] / sums[1]], axis=0)
        o_ref[...] = out_t.T.astype(o_ref.dtype)

    for q in range(k_ref.shape[0] // ATT_Q_BLOCK):
        pl.when(qi == q)(functools.partial(sweep, (q + 1) * ATT_K_PER_Q))


def _fox_attention(proj, cum, batch, seq):
    t = proj.shape[0]
    nq = seq // ATT_Q_BLOCK
    cum_h = cum[:, :N_HEADS].reshape(batch, seq, N_HEADS).transpose(0, 2, 1)
    cum_rows = cum_h.reshape(batch, N_HEADS, 1, seq)
    return pl.pallas_call(
        _fox_body,
        out_shape=jax.ShapeDtypeStruct((t, D_MODEL), BF16),
        grid=(batch, HEAD_PAIRS, nq),
        in_specs=[
            pl.BlockSpec((ATT_Q_BLOCK, LANES), lambda b, p, i: (b * nq + i, p)),
            pl.BlockSpec((seq, LANES), lambda b, p, i: (b, HEAD_PAIRS + p)),
            pl.BlockSpec((seq, LANES), lambda b, p, i: (b, 2 * HEAD_PAIRS + p)),
            pl.BlockSpec((1, 2, 1, ATT_Q_BLOCK), lambda b, p, i: (b, p, 0, i)),
            pl.BlockSpec((1, 2, 1, seq), lambda b, p, i: (b, p, 0, 0)),
        ],
        out_specs=pl.BlockSpec((ATT_Q_BLOCK, LANES), lambda b, p, i: (b * nq + i, p)),
        scratch_shapes=[
            pltpu.VMEM((LANES, seq), BF16),
            pltpu.VMEM((2, seq, LANES), F32),
            pltpu.VMEM((2, seq // ATT_BLOCK, ATT_BLOCK, ATT_Q_BLOCK), F32),
            pltpu.VMEM((2, seq // ATT_BLOCK, ATT_BLOCK, ATT_Q_BLOCK), BF16),
        ],
        compiler_params=_params("parallel", "parallel", "arbitrary"),
        name="fox_attention",
    )(proj, proj, proj, cum_rows, cum_rows)


def _band_body(q_ref, k_ref, v_ref, bias_ref, o_ref, s_ref, vt_scr, logit_scr, p_scr, inv_scr, *, sub_len):
    seq = v_ref.shape[0]
    n_blocks = seq // DIL_BLOCK
    _transpose_values(v_ref, vt_scr)

    def key_rows(n):
        first = (n * DIL_BLOCK) % sub_len == 0
        return first, slice((n if first else n - 1) * DIL_BLOCK, (n + 1) * DIL_BLOCK)

    for n in range(n_blocks):
        first, k_rows = key_rows(n)
        qs = _split_head_pair(q_ref[n * DIL_BLOCK:(n + 1) * DIL_BLOCK, :])
        k = k_ref[k_rows, :]
        for hh in range(2):
            bias = bias_ref[hh, DIL_BLOCK:, :] if first else bias_ref[hh]
            logit_scr[2 * n + hh, :k.shape[0], :] = _qk(k, qs[hh]) + bias

    for n in range(n_blocks):
        first, k_rows = key_rows(n)
        n_keys = k_rows.stop - k_rows.start
        for hh in range(2):
            logits = logit_scr[2 * n + hh, :n_keys, :]
            m = jnp.max(logits, axis=0, keepdims=True)
            p = jnp.exp(logits - m)
            l = jnp.sum(p, axis=0, keepdims=True)
            p_scr[2 * n + hh, :n_keys, :] = p.astype(BF16)
            inv_scr[2 * n + hh] = 1.0 / l
            s_ref[0, 0, hh:hh + 1, n * DIL_BLOCK:(n + 1) * DIL_BLOCK] = m + jnp.log(l)

    for n in range(n_blocks):
        first, k_rows = key_rows(n)
        n_keys = k_rows.stop - k_rows.start
        outs = []
        for hh in range(2):
            vt = vt_scr[hh * HEAD_DIM:(hh + 1) * HEAD_DIM, k_rows]
            o = jnp.dot(vt, p_scr[2 * n + hh, :n_keys, :], preferred_element_type=F32)
            outs.append(o * inv_scr[2 * n + hh])
        o_ref[n * DIL_BLOCK:(n + 1) * DIL_BLOCK, :] = jnp.concatenate(outs, axis=0).T.astype(o_ref.dtype)


def _band_attention(qk, v, bias_t, batch, seq, sub_len):
    t = v.shape[0]
    return pl.pallas_call(
        functools.partial(_band_body, sub_len=sub_len),
        out_shape=(jax.ShapeDtypeStruct((t, D_MODEL), BF16),
                   jax.ShapeDtypeStruct((batch, HEAD_PAIRS, 2, seq), F32)),
        grid=(batch, HEAD_PAIRS),
        in_specs=[
            pl.BlockSpec((None, seq, LANES), lambda b, p: (0, b, p)),
            pl.BlockSpec((None, seq, LANES), lambda b, p: (1, b, p)),
            pl.BlockSpec((seq, LANES), lambda b, p: (b, p)),
            pl.BlockSpec((2, 2 * DIL_BLOCK, DIL_BLOCK), lambda b, p: (p, 0, 0)),
        ],
        out_specs=(pl.BlockSpec((seq, LANES), lambda b, p: (b, p)),
                   pl.BlockSpec((1, 1, 2, seq), lambda b, p: (b, p, 0, 0))),
        scratch_shapes=[pltpu.VMEM((LANES, seq), BF16),
                        pltpu.VMEM((2 * seq // DIL_BLOCK, 2 * DIL_BLOCK, DIL_BLOCK), F32),
                        pltpu.VMEM((2 * seq // DIL_BLOCK, 2 * DIL_BLOCK, DIL_BLOCK), BF16),
                        pltpu.VMEM((2 * seq // DIL_BLOCK, 1, DIL_BLOCK), F32)],
        compiler_params=_params("parallel", "parallel"),
        name="band_attention",
    )(qk, qk, v, bias_t)


def _softmax_merge_body(o1_ref, o2_ref, o3_ref, s_ref, e_ref, o_ref):
    s = s_ref[...]
    groups = [s, pltpu.roll(s, LANES - N_HEADS, axis=1), pltpu.roll(s, LANES - 2 * N_HEADS, axis=1)]
    top = jnp.maximum(jnp.maximum(groups[0], groups[1]), groups[2])
    weights = [jnp.exp(g - top) for g in groups]
    inv = 1.0 / (weights[0] + weights[1] + weights[2])
    out = jnp.zeros(o_ref.shape, F32)
    for w, part in zip(weights, (o1_ref, o2_ref, o3_ref)):
        spread = jnp.dot((w * inv).astype(BF16), e_ref[...], preferred_element_type=F32)
        out = out + spread * part[...].astype(F32)
    o_ref[...] = out.astype(o_ref.dtype)


def _softmax_merge(outs, s_all):
    t, d = outs[0].shape
    head_of = jnp.arange(d) // HEAD_DIM
    expand = (jnp.arange(LANES)[:, None] == head_of[None, :]).astype(BF16)
    rows = pl.BlockSpec((ROW_TILE, d), lambda i: (i, 0))
    return pl.pallas_call(
        _softmax_merge_body,
        out_shape=jax.ShapeDtypeStruct((t, d), BF16),
        grid=(t // ROW_TILE,),
        in_specs=[rows, rows, rows,
                  pl.BlockSpec((ROW_TILE, LANES), lambda i: (i, 0)),
                  pl.BlockSpec((LANES, d), lambda i: (0, 0))],
        out_specs=rows,
        compiler_params=_params("parallel"),
        name="softmax_merge",
    )(*outs, s_all, expand)


def _t5_causal_bucket(distance):
    max_exact = N_REL_BUCKETS // 2
    d = jnp.maximum(distance, 1).astype(F32)
    log_b = max_exact + (jnp.log(d / max_exact) / math.log(REL_MAX_DISTANCE / max_exact)
                         * (N_REL_BUCKETS - max_exact)).astype(jnp.int32)
    log_b = jnp.minimum(log_b, N_REL_BUCKETS - 1)
    return jnp.where(distance < max_exact, distance, log_b)


def _dilated_bias(rel_bias):
    kj = jnp.arange(2 * DIL_BLOCK, dtype=jnp.int32)
    qi = jnp.arange(DIL_BLOCK, dtype=jnp.int32)
    delta = qi[None, :] + DIL_BLOCK - kj[:, None]
    in_band = (delta >= 0) & (delta <= DIL_SPAN)
    buckets = jnp.stack([_t5_causal_bucket(jnp.maximum(delta, 0) * dil) for _, dil in DILATED_PAIRS])
    one_hot = (buckets[..., None] == jnp.arange(N_REL_BUCKETS)).astype(F32)
    bias = jnp.einsum("gkqb,bh->ghkq", one_hot, rel_bias.astype(F32), precision=lax.Precision.HIGHEST)
    return jnp.where(in_band[None, None], bias, NEG_INF)


def _dilated_proj_body(x_ref, g_ref, w_ref, cs_ref, bd_ref, *rest):
    n_groups = len(DILATED_PAIRS)
    qk_refs, v_refs = rest[:n_groups], rest[n_groups:2 * n_groups]
    h_scr, res_scr = rest[2 * n_groups:]
    j = pl.program_id(1)

    @pl.when(j == 0)
    def _():
        h_scr[...] = _rms_normalize(x_ref[...], g_ref[...]).astype(BF16)

    acc = jnp.dot(h_scr[...], w_ref[...], preferred_element_type=F32)

    def keep(res, first_lane):
        for c in range(res.shape[1] // LANES):
            res_scr[first_lane // LANES + c] = res[:, c * LANES:(c + 1) * LANES]

    @pl.when(j < 2 * n_groups)
    def _():
        for c in range(COL_TILE // NORM_CHUNK):
            sl = slice(c * NORM_CHUNK, (c + 1) * NORM_CHUNK)
            a = acc[:, sl]
            ss = jnp.dot((a * a).astype(BF16), bd_ref[...], preferred_element_type=F32)
            keep(a * lax.rsqrt(ss * (1.0 / HEAD_DIM) + RMS_EPS) * cs_ref[:, sl], c * NORM_CHUNK)

    @pl.when(j == 2 * n_groups)
    def _():
        keep(acc * cs_ref[...], 0)

    def write_classes(ref, dil):
        rows = res_scr.shape[1] // dil
        for c in range(res_scr.shape[0]):
            lanes = slice(c * LANES, (c + 1) * LANES)
            if dil == 1:
                ref[:, lanes] = res_scr[c].astype(ref.dtype)
            else:
                for r in range(dil):
                    ref[r, :, lanes] = res_scr[c, pl.ds(r, rows, stride=dil), :].astype(ref.dtype)

    for g, (_, dil) in enumerate(DILATED_PAIRS):
        pl.when(j // 2 == g)(functools.partial(write_classes, qk_refs[g], dil))
        pl.when(j == 2 * n_groups)(functools.partial(write_classes, v_refs[g], dil))


def _dilated_proj(x, gain, w, layer, col_scale, batch, seq):
    t, d = x.shape
    n = w.shape[2]
    n_groups = len(DILATED_PAIRS)
    tiles_per_seq = seq // DIL_ROW_TILE
    head_id = jnp.arange(NORM_CHUNK) // HEAD_DIM
    block_diag = (head_id[:, None] == head_id[None, :]).astype(BF16)
    out_shape, out_specs = [], []
    for kind in ("qk", "v"):
        for g, (_, dil) in enumerate(DILATED_PAIRS):
            rows = DIL_ROW_TILE // dil
            if kind == "qk":
                shape = (2, batch, dil, seq // dil, d)
                block = (None, None, dil, rows, d)
                index = lambda i, j, g=g: (jnp.clip(j - 2 * g, 0, 1), i // tiles_per_seq, 0, i % tiles_per_seq, 0)
            else:
                shape = (batch, dil, seq // dil, d)
                block = (None, dil, rows, d)
                index = lambda i, j: (i // tiles_per_seq, 0, i % tiles_per_seq, 0)
            if dil == 1:
                block = block[:-3] + (None,) + block[-2:]
            out_shape.append(jax.ShapeDtypeStruct(shape, BF16))
            out_specs.append(pl.BlockSpec(block, index))
    outs = pl.pallas_call(
        _dilated_proj_body,
        out_shape=tuple(out_shape),
        grid=(t // DIL_ROW_TILE, n // COL_TILE),
        in_specs=[
            pl.BlockSpec((DIL_ROW_TILE, d), lambda i, j: (i, 0)),
            pl.BlockSpec((1, d), lambda i, j: (0, 0)),
            pl.BlockSpec((None, d, COL_TILE), lambda i, j: (layer, 0, j)),
            pl.BlockSpec((1, COL_TILE), lambda i, j: (0, j)),
            pl.BlockSpec((NORM_CHUNK, NORM_CHUNK), lambda i, j: (0, 0)),
        ],
        out_specs=tuple(out_specs),
        scratch_shapes=[pltpu.VMEM((DIL_ROW_TILE, d), BF16),
                        pltpu.VMEM((COL_TILE // LANES, DIL_ROW_TILE, LANES), F32)],
        compiler_params=_params("parallel", "arbitrary"),
        name="dilated_proj",
    )(x, gain.reshape(1, d), w, col_scale.reshape(1, n).astype(F32), block_diag)
    qk = [o.reshape(2, t, d) for o in outs[:n_groups]]
    v = [o.reshape(t, d) for o in outs[n_groups:]]
    return qk, v


def _dilated_attention(qk, v, rel_bias, batch, seq):
    t = v[0].shape[0]
    n_groups = len(DILATED_PAIRS)
    bias_t = _dilated_bias(rel_bias)
    outs, stats = [], []
    for g, (_, dil) in enumerate(DILATED_PAIRS):
        sub_len = seq // dil
        o, s = _band_attention(qk[g], v[g], bias_t[g], batch, seq, sub_len)
        o = o.reshape(batch, dil, sub_len, D_MODEL).transpose(0, 2, 1, 3).reshape(t, D_MODEL)
        s = s.reshape(batch, HEAD_PAIRS, 2, dil, sub_len).swapaxes(3, 4)
        outs.append(o)
        stats.append(s.reshape(batch, N_HEADS, seq).transpose(0, 2, 1).reshape(t, N_HEADS))
    stats.append(jnp.zeros((t, LANES - n_groups * N_HEADS), F32))
    return _softmax_merge(outs, jnp.concatenate(stats, axis=1))


def _tile_heads(v):
    return jnp.tile(v.astype(F32), N_HEADS)


def _sb_mixer(x, gain, w_qkv, layer, batch, seq):
    ones = jnp.ones((D_MODEL,), F32)
    col_scale = jnp.concatenate([ones * QK_SCALE, ones, ones])
    proj = _norm_proj(x, gain, w_qkv, layer, col_scale, 0)
    return _sb_attention(proj, batch, seq)


def _dilated_mixer(x, gain, w_in, q_norm, k_norm, rel_bias, layer, batch, seq):
    scales = []
    for g in range(len(DILATED_PAIRS)):
        scales += [_tile_heads(q_norm[g]) * QK_SCALE, _tile_heads(k_norm[g])]
    scales.append(jnp.ones((D_MODEL,), F32))
    qk, v = _dilated_proj(x, gain, w_in, layer, jnp.concatenate(scales), batch, seq)
    return _dilated_attention(qk, v, rel_bias, batch, seq)


def _fox_mixer(x, gain, w_qkv, w_gate, b_f, q_norm, k_norm, layer, batch, seq):
    col_scale = jnp.concatenate([_tile_heads(q_norm) * QK_SCALE, _tile_heads(k_norm),
                                 jnp.ones((D_MODEL,), F32)])
    proj = _norm_proj(x, gain, w_qkv, layer, col_scale, 2)
    cum = _fox_cum_log_forget(x, gain, w_gate, b_f, batch, seq)
    return _fox_attention(proj, cum, batch, seq)


def kernel(x, sb_w_qkv, sb_w_o, dil_w_in, dil_q_norm, dil_k_norm, dil_w_o, fox_w_in, fox_b_f,
           fox_q_norm, fox_k_norm, fox_w_o, rel_bias, attn_norm, ffn_norm, mlp_w_gate, mlp_w_up,
           mlp_w_down, moe_router, moe_w_gate, moe_w_up, moe_w_down):
    batch, seq, d = x.shape
    depth = attn_norm.shape[0]
    assert d == D_MODEL and seq % max(ROW_TILE, DIL_ROW_TILE, ATT_Q_BLOCK, SB_Q_BLOCK) == 0
    assert all((seq // dil) % DIL_BLOCK == 0 for _, dil in DILATED_PAIRS)
    sb_w_qkv, sb_w_o, dil_w_in, dil_w_o, fox_w_o, mlp_w_gate, mlp_w_up, mlp_w_down = (
        w.astype(BF16) for w in (sb_w_qkv, sb_w_o, dil_w_in, dil_w_o, fox_w_o, mlp_w_gate, mlp_w_up,
                                 mlp_w_down))
    moe_w_gate, moe_w_up, moe_w_down = (w.astype(BF16) for w in (moe_w_gate, moe_w_up, moe_w_down))
    fox_w_qkv = fox_w_in[:, :, :3 * D_MODEL].astype(BF16)
    h = x.reshape(batch * seq, d)
    for i in range(depth):
        kind, j = i % 3, i // 3
        if kind == 0:
            mixed, w_o = _sb_mixer(h, attn_norm[i], sb_w_qkv, j, batch, seq), sb_w_o
        elif kind == 1:
            mixed, w_o = _dilated_mixer(h, attn_norm[i], dil_w_in, dil_q_norm[j], dil_k_norm[j], rel_bias,
                                        j, batch, seq), dil_w_o
        else:
            mixed, w_o = _fox_mixer(h, attn_norm[i], fox_w_qkv, fox_w_in[j, :, 3 * D_MODEL:], fox_b_f[j],
                                    fox_q_norm[j], fox_k_norm[j], j, batch, seq), fox_w_o
        f = i // 2
        if i % 2 == 0:
            h = _mix_ffn_residual(h, mixed, w_o, j, ffn_norm[i], mlp_w_gate, mlp_w_up, mlp_w_down, f)
        else:
            h = _mix_moe_residual(h, mixed, w_o, j, ffn_norm[i], moe_router[f], moe_w_gate, moe_w_up,
                                  moe_w_down, f)
    return h.reshape(batch, seq, d)
```

```python
import functools
import math

import jax
import jax.numpy as jnp
from jax import lax
from jax.experimental import pallas as pl
from jax.experimental.pallas import tpu as pltpu
from jax.experimental.pallas import tpu_sc as plsc

D_MODEL = 1024
N_HEADS = 16
HEAD_DIM = 64
LANES = 128
HEAD_PAIRS = D_MODEL // LANES
N_EXPERTS = 8
N_REL_BUCKETS = 32
REL_MAX_DISTANCE = 2048
DILATED_PAIRS = ((128, 1), (512, 4), (2048, 16))
DIL_SPAN = 128
RMS_EPS = 1e-6
NEG_INF = -1e30
SB_UNDERFLOW_LOG = -104.0
QK_SCALE = 1.0 / math.sqrt(HEAD_DIM)

ROW_TILE = 1024
COL_TILE = 1024
FF_TILE = 1792
SWIGLU_CHUNK = 256
MOE_ROW_TILE = 512
MOE_FF_TILE = 1792
SC_CORES = 2
SC_SUBCORES = 16
SC_CHUNK_ROWS = 64
NORM_CHUNK = 256
ATT_BLOCK = 256
ATT_Q_BLOCK = 512
ATT_K_PER_Q = ATT_Q_BLOCK // ATT_BLOCK
SB_Q_BLOCK = 2048
SB_SUB_BLOCKS = SB_Q_BLOCK // ATT_BLOCK
DIL_BLOCK = 128
DIL_ROW_TILE = 1024
VMEM_LIMIT = 56 * 1024 * 1024

F32 = jnp.float32
BF16 = jnp.bfloat16


def _params(*semantics):
    return pltpu.CompilerParams(dimension_semantics=semantics, vmem_limit_bytes=VMEM_LIMIT)


def _rms_normalize(x, gain):
    inv = lax.rsqrt(jnp.mean(x * x, axis=-1, keepdims=True) + RMS_EPS)
    return x * inv * gain


def _split_bf16(x, terms):
    parts = []
    for _ in range(terms):
        part = x.astype(BF16)
        parts.append(part)
        x = x - part.astype(F32)
    return parts


def _dot_split(a, b):
    a_hi, a_lo = _split_bf16(a, 2)
    b_hi, b_lo = _split_bf16(b, 2)
    return (jnp.dot(a_hi, b_hi, preferred_element_type=F32)
            + (jnp.dot(a_hi, b_lo, preferred_element_type=F32)
               + jnp.dot(a_lo, b_hi, preferred_element_type=F32)))


def _softplus(z):
    return jnp.maximum(z, 0.0) + jnp.log(1.0 + jnp.exp(-jnp.abs(z)))


def _norm_proj_body(x_ref, g_ref, w_ref, cs_ref, bd_ref, o_ref, h_scr, *, n_norm):
    j = pl.program_id(1)

    @pl.when(j == 0)
    def _():
        h_scr[...] = _rms_normalize(x_ref[...], g_ref[...]).astype(BF16)

    acc = jnp.dot(h_scr[...], w_ref[...], preferred_element_type=F32)

    def plain():
        o_ref[...] = (acc * cs_ref[...]).astype(o_ref.dtype)

    def head_normed():
        for c in range(COL_TILE // NORM_CHUNK):
            sl = slice(c * NORM_CHUNK, (c + 1) * NORM_CHUNK)
            a = acc[:, sl]
            ss = jnp.dot((a * a).astype(BF16), bd_ref[...], preferred_element_type=F32)
            inv = lax.rsqrt(ss * (1.0 / HEAD_DIM) + RMS_EPS)
            o_ref[:, sl] = (a * inv * cs_ref[:, sl]).astype(o_ref.dtype)

    if n_norm == 0:
        plain()
    else:
        pl.when(j < n_norm)(head_normed)
        pl.when(j >= n_norm)(plain)


def _norm_proj(x, gain, w, layer, col_scale, n_norm):
    t, d = x.shape
    n = w.shape[2]
    head_id = jnp.arange(NORM_CHUNK) // HEAD_DIM
    block_diag = (head_id[:, None] == head_id[None, :]).astype(BF16)
    return pl.pallas_call(
        functools.partial(_norm_proj_body, n_norm=n_norm),
        out_shape=jax.ShapeDtypeStruct((t, n), BF16),
        grid=(t // ROW_TILE, n // COL_TILE),
        in_specs=[
            pl.BlockSpec((ROW_TILE, d), lambda i, j: (i, 0)),
            pl.BlockSpec((1, d), lambda i, j: (0, 0)),
            pl.BlockSpec((None, d, COL_TILE), lambda i, j: (layer, 0, j)),
            pl.BlockSpec((1, COL_TILE), lambda i, j: (0, j)),
            pl.BlockSpec((NORM_CHUNK, NORM_CHUNK), lambda i, j: (0, 0)),
        ],
        out_specs=pl.BlockSpec((ROW_TILE, COL_TILE), lambda i, j: (i, j)),
        scratch_shapes=[pltpu.VMEM((ROW_TILE, d), BF16)],
        compiler_params=_params("parallel", "arbitrary"),
        name="norm_proj",
    )(x, gain.reshape(1, d), w, col_scale.reshape(1, n).astype(F32), block_diag)


def _swiglu_hidden(h, wg, wu):
    g = jnp.dot(h, wg, preferred_element_type=F32)
    u = jnp.dot(h, wu, preferred_element_type=F32)
    return g * (1.0 / (1.0 + jnp.exp(-g))) * u


def _swiglu_down(h, wg_ref, wu_ref, wd_ref):
    y = None
    for c in range(wg_ref.shape[1] // SWIGLU_CHUNK):
        cols = slice(c * SWIGLU_CHUNK, (c + 1) * SWIGLU_CHUNK)
        a = _swiglu_hidden(h, wg_ref[:, cols], wu_ref[:, cols]).astype(BF16)
        part = jnp.dot(a, wd_ref[cols, :], preferred_element_type=F32)
        y = part if y is None else y + part
    return y


def _ffn_body(x_ref, o_ref, wo_ref, g_ref, wg_ref, wu_ref, wd_ref, y_ref, h_scr):
    f = pl.program_id(1)

    @pl.when(f == 0)
    def _():
        x = x_ref[...] + jnp.dot(o_ref[...], wo_ref[...], preferred_element_type=F32)
        h_scr[...] = _rms_normalize(x, g_ref[...]).astype(BF16)
        y_ref[...] = x

    y_ref[...] += _swiglu_down(h_scr[...], wg_ref, wu_ref, wd_ref)


def _mix_ffn_residual(x, o, w_o, mix_layer, gain, w_gate, w_up, w_down, layer):
    t, d = x.shape
    ff = w_gate.shape[2]
    return pl.pallas_call(
        _ffn_body,
        out_shape=jax.ShapeDtypeStruct((t, d), F32),
        grid=(t // ROW_TILE, ff // FF_TILE),
        in_specs=[
            pl.BlockSpec((ROW_TILE, d), lambda i, f: (i, 0)),
            pl.BlockSpec((ROW_TILE, d), lambda i, f: (i, 0)),
            pl.BlockSpec((None, d, d), lambda i, f: (mix_layer, 0, 0)),
            pl.BlockSpec((1, d), lambda i, f: (0, 0)),
            pl.BlockSpec((None, d, FF_TILE), lambda i, f: (layer, 0, f)),
            pl.BlockSpec((None, d, FF_TILE), lambda i, f: (layer, 0, f)),
            pl.BlockSpec((None, FF_TILE, d), lambda i, f: (layer, f, 0)),
        ],
        out_specs=pl.BlockSpec((ROW_TILE, d), lambda i, f: (i, 0)),
        scratch_shapes=[pltpu.VMEM((ROW_TILE, d), BF16)],
        compiler_params=_params("parallel", "arbitrary"),
        name="ffn",
    )(x, o, w_o, gain.reshape(1, d), w_gate, w_up, w_down)


def _pack_bf16_pairs(x):
    half = x.shape[1] // 2
    bits = pltpu.bitcast(x.astype(BF16).astype(F32), jnp.int32)
    return bits[:, :half] | lax.shift_right_logical(bits[:, half:], jnp.int32(16))


def _unpack_bf16_pairs(p):
    left = pltpu.bitcast(p & jnp.int32(-65536), F32)
    right = pltpu.bitcast(lax.shift_left(p, jnp.int32(16)), F32)
    return jnp.concatenate([left, right], axis=1).astype(BF16)


def _router_body(x_ref, o_ref, wo_ref, g_ref, r_ref, tri_ref, gates_ref, rank_ref, h_ref, x1_ref, count_scr):
    @pl.when(pl.program_id(0) == 0)
    def _():
        count_scr[...] = jnp.zeros_like(count_scr)

    x1 = x_ref[...] + jnp.dot(o_ref[...], wo_ref[...], preferred_element_type=F32)
    x1_ref[...] = x1
    h = _rms_normalize(x1, g_ref[...])
    h_ref[...] = _pack_bf16_pairs(h)
    logits = _dot_split(h, r_ref[...])
    lane = lax.broadcasted_iota(jnp.int32, logits.shape, 1).astype(F32)
    logits = jnp.where(lane < N_EXPERTS, logits, -jnp.inf)
    m1 = jnp.max(logits, axis=-1, keepdims=True)
    i1 = jnp.min(jnp.where(logits == m1, lane, float(LANES)), axis=-1, keepdims=True)
    rest = jnp.where(lane == i1, -jnp.inf, logits)
    m2 = jnp.max(rest, axis=-1, keepdims=True)
    i2 = jnp.min(jnp.where(rest == m2, lane, float(LANES)), axis=-1, keepdims=True)
    e = jnp.exp(m2 - m1)
    g1 = 1.0 / (1.0 + e)
    gates = jnp.where(lane == i1, g1, 0.0) + jnp.where(lane == i2, e * g1, 0.0)
    gates_ref[...] = gates.T[:N_EXPERTS, :]
    chosen = jnp.where((lane == i1) | (lane == i2), 1.0, 0.0)
    inclusive = jnp.dot(tri_ref[...], chosen.astype(BF16), preferred_element_type=F32)
    rank = jnp.where(chosen > 0.0, inclusive - 1.0 + count_scr[...], -1.0)
    rank_ref[...] = rank.T[:N_EXPERTS, :]
    count_scr[...] += inclusive[ROW_TILE - 1:ROW_TILE, :]


def _mix_router(x, o, w_o, mix_layer, gain, router):
    t, d = x.shape
    r = jnp.zeros((d, LANES), F32).at[:, :N_EXPERTS].set(router)
    idx = jnp.arange(ROW_TILE)
    tri = (idx[:, None] >= idx[None, :]).astype(BF16)
    rows = pl.BlockSpec((ROW_TILE, d), lambda i: (i, 0))
    return pl.pallas_call(
        _router_body,
        out_shape=(jax.ShapeDtypeStruct((N_EXPERTS, t), F32), jax.ShapeDtypeStruct((N_EXPERTS, t), F32),
                   jax.ShapeDtypeStruct((t, d // 2), jnp.int32), jax.ShapeDtypeStruct((t, d), F32)),
        grid=(t // ROW_TILE,),
        in_specs=[
            rows,
            rows,
            pl.BlockSpec((None, d, d), lambda i: (mix_layer, 0, 0)),
            pl.BlockSpec((1, d), lambda i: (0, 0)),
            pl.BlockSpec((d, LANES), lambda i: (0, 0)),
            pl.BlockSpec((ROW_TILE, ROW_TILE), lambda i: (0, 0)),
        ],
        out_specs=(pl.BlockSpec((N_EXPERTS, ROW_TILE), lambda i: (0, i)),
                   pl.BlockSpec((N_EXPERTS, ROW_TILE), lambda i: (0, i)),
                   pl.BlockSpec((ROW_TILE, d // 2), lambda i: (i, 0)),
                   rows),
        scratch_shapes=[pltpu.VMEM((1, LANES), F32)],
        compiler_params=_params("arbitrary"),
        name="router",
    )(x, o, w_o, gain.reshape(1, d), r, tri)


def _sc_mesh():
    return plsc.VectorSubcoreMesh(core_axis_name="core", subcore_axis_name="subcore",
                                  num_cores=SC_CORES, num_subcores=SC_SUBCORES)


def _sc_worker_base(per_worker):
    return (lax.axis_index("subcore") * SC_CORES + lax.axis_index("core")) * per_worker


def _sc_row_gather(table, idx):
    width = table.shape[1]
    n = idx.shape[0]
    per_worker = n // (SC_CORES * SC_SUBCORES)
    n_chunks = per_worker // SC_CHUNK_ROWS
    assert n == n_chunks * SC_CHUNK_ROWS * SC_CORES * SC_SUBCORES and n_chunks % 2 == 0

    @functools.partial(
        pl.kernel, mesh=_sc_mesh(), out_type=jax.ShapeDtypeStruct((n, width), table.dtype),
        scratch_types=[pltpu.VMEM((SC_CHUNK_ROWS,), jnp.int32), pltpu.VMEM((SC_CHUNK_ROWS,), jnp.int32),
                       pltpu.VMEM((SC_CHUNK_ROWS, width), table.dtype),
                       pltpu.VMEM((SC_CHUNK_ROWS, width), table.dtype),
                       pltpu.SemaphoreType.DMA, pltpu.SemaphoreType.DMA],
        name="sc_row_gather")
    def gather(table_hbm, idx_hbm, out_hbm, idx_a, idx_b, rows_a, rows_b, sem_a, sem_b):
        base = _sc_worker_base(per_worker)
        bufs = ((idx_a, rows_a, sem_a), (idx_b, rows_b, sem_b))

        def rows_of(c):
            return pl.ds(pl.multiple_of(base + c * SC_CHUNK_ROWS, SC_CHUNK_ROWS), SC_CHUNK_ROWS)

        def fetch(c, buf):
            idx_v, rows_v, sem = buf
            pltpu.sync_copy(idx_hbm.at[rows_of(c)], idx_v)
            return pltpu.make_async_copy(table_hbm.at[idx_v], rows_v, sem)

        fetch(0, bufs[0]).start()

        @pl.loop(0, n_chunks, step=2)
        def _(c):
            for b in range(2):
                idx_v, rows_v, sem = bufs[b]
                pltpu.make_async_copy(table_hbm.at[idx_v], rows_v, sem).wait()

                @pl.when(c + b + 1 < n_chunks)
                def _():
                    fetch(c + b + 1, bufs[1 - b]).start()

                pltpu.sync_copy(rows_v, out_hbm.at[rows_of(c + b)])

    return gather(table, idx)


def _sc_row_scatter_pair(rows, idx_lo, idx_hi, n_out):
    n, width = rows.shape
    per_worker = n // (SC_CORES * SC_SUBCORES)
    n_chunks = per_worker // SC_CHUNK_ROWS
    assert n == n_chunks * SC_CHUNK_ROWS * SC_CORES * SC_SUBCORES and n_chunks % 2 == 0
    index_scratch = pltpu.VMEM((SC_CHUNK_ROWS,), jnp.int32)
    rows_scratch = pltpu.VMEM((SC_CHUNK_ROWS, width), rows.dtype)

    @functools.partial(
        pl.kernel, mesh=_sc_mesh(), out_type=jax.ShapeDtypeStruct((n_out, width), rows.dtype),
        scratch_types=[index_scratch, index_scratch, rows_scratch, rows_scratch,
                       pltpu.SemaphoreType.DMA, pltpu.SemaphoreType.DMA],
        name="sc_row_scatter")
    def scatter(rows_hbm, lo_hbm, hi_hbm, out_hbm, lo_v, hi_v, rows_a, rows_b, sem_a, sem_b):
        base = _sc_worker_base(per_worker)
        bufs = ((rows_a, sem_a), (rows_b, sem_b))

        def rows_of(c):
            return pl.ds(pl.multiple_of(base + c * SC_CHUNK_ROWS, SC_CHUNK_ROWS), SC_CHUNK_ROWS)

        def load(c, buf):
            rows_v, sem = buf
            return pltpu.make_async_copy(rows_hbm.at[rows_of(c)], rows_v, sem)

        load(0, bufs[0]).start()

        @pl.loop(0, n_chunks, step=2)
        def _(c):
            for b in range(2):
                rows_v, _ = bufs[b]
                load(c + b, bufs[b]).wait()

                @pl.when(c + b + 1 < n_chunks)
                def _():
                    load(c + b + 1, bufs[1 - b]).start()

                pltpu.sync_copy(lo_hbm.at[rows_of(c + b)], lo_v)
                pltpu.sync_copy(hi_hbm.at[rows_of(c + b)], hi_v)
                pltpu.sync_copy(rows_v, out_hbm.at[lo_v])
                pltpu.sync_copy(rows_v, out_hbm.at[hi_v])

    return scatter(rows, idx_lo, idx_hi)


def _expert_ffn_body(te_ref, nu_ref, nv_ref, h_ref, wg_ref, wu_ref, wd_ref, y_ref, acc_scr):
    i = pl.program_id(0)
    f = pl.program_id(1)

    @pl.when(i < nu_ref[0])
    def _():
        row = lax.broadcasted_iota(jnp.int32, h_ref.shape, 0)
        packed = jnp.where(row < nv_ref[i], h_ref[...], 0)
        y = _swiglu_down(_unpack_bf16_pairs(packed), wg_ref, wu_ref, wd_ref)

        @pl.when(f == 0)
        def _():
            acc_scr[...] = y

        @pl.when(f > 0)
        def _():
            acc_scr[...] += y

        @pl.when(f == pl.num_programs(1) - 1)
        def _():
            y_ref[...] = _pack_bf16_pairs(acc_scr[...])


def _expert_ffn(h_sorted, tile_expert, n_used, tile_valid, w_gate, w_up, w_down, layer):
    rows, half = h_sorted.shape
    d = 2 * half
    ff = w_gate.shape[3]
    n_f = ff // MOE_FF_TILE

    def row_map(i, f, te, nu, nv):
        return (jnp.minimum(i, nu[0] - 1), 0)

    def col_step(i, f, nu):
        return jnp.where(i < nu[0], f, n_f - 1)

    grid_spec = pltpu.PrefetchScalarGridSpec(
        num_scalar_prefetch=3,
        grid=(rows // MOE_ROW_TILE, n_f),
        in_specs=[
            pl.BlockSpec((MOE_ROW_TILE, half), row_map),
            pl.BlockSpec((None, None, d, MOE_FF_TILE),
                         lambda i, f, te, nu, nv: (layer, te[i], 0, col_step(i, f, nu))),
            pl.BlockSpec((None, None, d, MOE_FF_TILE),
                         lambda i, f, te, nu, nv: (layer, te[i], 0, col_step(i, f, nu))),
            pl.BlockSpec((None, None, MOE_FF_TILE, d),
                         lambda i, f, te, nu, nv: (layer, te[i], col_step(i, f, nu), 0)),
        ],
        out_specs=pl.BlockSpec((MOE_ROW_TILE, half), row_map),
        scratch_shapes=[pltpu.VMEM((MOE_ROW_TILE, d), F32)],
    )
    return pl.pallas_call(
        _expert_ffn_body,
        out_shape=jax.ShapeDtypeStruct((rows, half), jnp.int32),
        grid_spec=grid_spec,
        compiler_params=_params("arbitrary", "arbitrary"),
        name="expert_ffn",
    )(tile_expert, n_used, tile_valid, h_sorted, w_gate, w_up, w_down)


def _combine_body(x_ref, y_ref, g_ref, o_ref):
    out = x_ref[...]
    for s in range(2):
        column = jnp.broadcast_to(g_ref[s:s + 1, :], (LANES, g_ref.shape[1])).T
        gate = jnp.concatenate([column] * (out.shape[1] // LANES), axis=1)
        out = out + _unpack_bf16_pairs(y_ref[s]).astype(F32) * gate
    o_ref[...] = out


def _combine_residual(x, y_pairs, gates2):
    t, d = x.shape
    return pl.pallas_call(
        _combine_body,
        out_shape=jax.ShapeDtypeStruct((t, d), F32),
        grid=(t // ROW_TILE,),
        in_specs=[
            pl.BlockSpec((ROW_TILE, d), lambda i: (i, 0)),
            pl.BlockSpec((2, ROW_TILE, d // 2), lambda i: (0, i, 0)),
            pl.BlockSpec((2, ROW_TILE), lambda i: (0, i)),
        ],
        out_specs=pl.BlockSpec((ROW_TILE, d), lambda i: (i, 0)),
        compiler_params=_params("parallel"),
        name="moe_combine",
    )(x, y_pairs, gates2)


def _mix_moe_residual(x, o, w_o, mix_layer, gain, router, w_gate, w_up, w_down, layer):
    t, d = x.shape
    gates, rank, h_packed, x = _mix_router(x, o, w_o, mix_layer, gain, router)
    rank8 = rank.astype(jnp.int32)
    chosen = rank8 >= 0
    counts = jnp.sum(chosen, axis=1, dtype=jnp.int32)
    padded = (counts + MOE_ROW_TILE - 1) // MOE_ROW_TILE * MOE_ROW_TILE
    ends = jnp.cumsum(padded)
    starts = ends - padded
    pos = starts[:, None] + rank8
    max_rows = 2 * t + N_EXPERTS * MOE_ROW_TILE
    pos_lo = jnp.min(jnp.where(chosen, pos, max_rows), axis=0)
    pos_hi = jnp.max(jnp.where(chosen, pos, -1), axis=0)
    gates2 = jnp.stack([jnp.sum(jnp.where(chosen & (pos == pos_lo[None, :]), gates, 0.0), axis=0),
                        jnp.sum(jnp.where(chosen & (pos == pos_hi[None, :]), gates, 0.0), axis=0)])
    n_tiles = max_rows // MOE_ROW_TILE
    n_used = (ends[-1] // MOE_ROW_TILE).astype(jnp.int32)
    tile_start = jnp.minimum(jnp.arange(n_tiles, dtype=jnp.int32), n_used - 1) * MOE_ROW_TILE
    tile_expert = jnp.sum(tile_start[:, None] >= ends[None, :], axis=1, dtype=jnp.int32)
    tile_valid = jnp.clip((starts + counts)[tile_expert] - tile_start, 0, MOE_ROW_TILE).astype(jnp.int32)

    h_sorted = _sc_row_scatter_pair(h_packed, pos_lo, pos_hi, max_rows)
    y_sorted = _expert_ffn(h_sorted, tile_expert, n_used.reshape(1), tile_valid, w_gate, w_up, w_down, layer)
    y_pairs = _sc_row_gather(y_sorted, jnp.concatenate([pos_lo, pos_hi])).reshape(2, t, d // 2)
    return _combine_residual(x, y_pairs, gates2)


def _split_head_pair(q):
    is_first = lax.broadcasted_iota(jnp.int32, (1, LANES), 1) < HEAD_DIM
    zero = jnp.zeros_like(q)
    return jnp.where(is_first, q, zero), jnp.where(is_first, zero, q)


def _qk(q, k):
    return lax.dot_general(q, k, (((1,), (1,)), ((), ())), preferred_element_type=F32)


def _transpose_values(v_ref, vt_scr):
    for c in range(v_ref.shape[0] // ATT_BLOCK):
        rows = slice(c * ATT_BLOCK, (c + 1) * ATT_BLOCK)
        vt_scr[:, rows] = v_ref[rows, :].astype(F32).T.astype(vt_scr.dtype)


def _sb_body(q_ref, k_ref, v_ref, u_ref, o_ref, vt_scr, acc_scr, carry_scr, z_scr, keep_scr, sum_scr):
    qi = pl.program_id(2)

    @pl.when(qi == 0)
    def _():
        _transpose_values(v_ref, vt_scr)

    upper = u_ref[...]
    acc_scr[...] = jnp.zeros_like(acc_scr)
    carry_scr[...] = jnp.zeros_like(carry_scr)
    key = lax.broadcasted_iota(jnp.int32, (ATT_BLOCK, ATT_BLOCK), 0)
    query = lax.broadcasted_iota(jnp.int32, (ATT_BLOCK, ATT_BLOCK), 1)
    strict = key < query
    qs = [_split_head_pair(q_ref[j * ATT_BLOCK:(j + 1) * ATT_BLOCK, :]) for j in range(SB_SUB_BLOCKS)]

    def process(j, kb, keep):
        start = pl.multiple_of(kb * ATT_BLOCK, ATT_BLOCK)
        k = k_ref[pl.ds(start, ATT_BLOCK), :]
        for hh in range(2):
            z = _qk(k, qs[j][hh])
            log_beta = z - _softplus(z)
            log_keep = log_beta - z
            if keep is not None:
                log_keep = jnp.where(keep, log_keep, 0.0)
            remain = jnp.dot(upper, log_keep.astype(BF16), preferred_element_type=F32)
            w = jnp.exp(log_beta + remain + carry_scr[j, hh])
            if keep is not None:
                w = jnp.where(keep, w, 0.0)
            carry_scr[j, hh] += jnp.sum(log_keep, axis=0, keepdims=True)
            vt = vt_scr[pl.ds(hh * HEAD_DIM, HEAD_DIM), pl.ds(start, ATT_BLOCK)]
            acc_scr[j, hh] += jnp.dot(vt, w.astype(BF16), preferred_element_type=F32)

    units = []
    for j in range(SB_SUB_BLOCKS):
        g = qi * SB_SUB_BLOCKS + j
        has_previous = None if j > 0 else jnp.broadcast_to(g > 0, strict.shape)
        for kb, keep in ((g, strict), (jnp.maximum(g - 1, 0), has_previous)):
            for hh in range(2):
                units.append((j, hh, pl.multiple_of(kb * ATT_BLOCK, ATT_BLOCK), keep))

    for u, (j, hh, start, keep) in enumerate(units):
        z_scr[u] = _qk(k_ref[pl.ds(start, ATT_BLOCK), :], qs[j][hh])

    for u, (j, hh, start, keep) in enumerate(units):
        z = z_scr[u]
        log_beta = z - _softplus(z)
        log_keep = log_beta - z
        if keep is not None:
            log_keep = jnp.where(keep, log_keep, 0.0)
        z_scr[u] = log_beta
        keep_scr[u] = log_keep.astype(BF16)
        sum_scr[u] = jnp.sum(log_keep, axis=0, keepdims=True)

    for u in range(len(units)):
        z_scr[u] += jnp.dot(upper, keep_scr[u], preferred_element_type=F32)

    for u, (j, hh, start, keep) in enumerate(units):
        diagonal = u % 4 < 2
        log_w = z_scr[u] if diagonal else z_scr[u] + sum_scr[u - 2]
        w = jnp.exp(log_w)
        if keep is not None:
            w = jnp.where(keep, w, 0.0)
        keep_scr[u] = w.astype(BF16)

    for u, (j, hh, start, keep) in enumerate(units):
        vt = vt_scr[pl.ds(hh * HEAD_DIM, HEAD_DIM), pl.ds(start, ATT_BLOCK)]
        acc_scr[j, hh] += jnp.dot(vt, keep_scr[u], preferred_element_type=F32)
        carry_scr[j, hh] += sum_scr[u]

    def sweep_earlier():
        for j in range(SB_SUB_BLOCKS):
            def more(kb, j=j):
                return (kb >= 0) & (jnp.max(carry_scr[j]) > SB_UNDERFLOW_LOG)

            def step(kb, j=j):
                process(j, kb, None)
                return kb - 1

            lax.while_loop(more, step, qi * SB_SUB_BLOCKS + j - 2)

    pl.when(jnp.max(carry_scr[...]) > SB_UNDERFLOW_LOG)(sweep_earlier)

    for j in range(SB_SUB_BLOCKS):
        out_t = jnp.concatenate([acc_scr[j, 0], acc_scr[j, 1]], axis=0)
        o_ref[j * ATT_BLOCK:(j + 1) * ATT_BLOCK, :] = out_t.T.astype(o_ref.dtype)


def _sb_attention(proj, batch, seq):
    t = proj.shape[0]
    nq = seq // SB_Q_BLOCK
    idx = jnp.arange(ATT_BLOCK)
    upper = (idx[None, :] > idx[:, None]).astype(BF16)
    return pl.pallas_call(
        _sb_body,
        out_shape=jax.ShapeDtypeStruct((t, D_MODEL), BF16),
        grid=(batch, HEAD_PAIRS, nq),
        in_specs=[
            pl.BlockSpec((SB_Q_BLOCK, LANES), lambda b, p, i: (b * nq + i, p)),
            pl.BlockSpec((seq, LANES), lambda b, p, i: (b, HEAD_PAIRS + p)),
            pl.BlockSpec((seq, LANES), lambda b, p, i: (b, 2 * HEAD_PAIRS + p)),
            pl.BlockSpec((ATT_BLOCK, ATT_BLOCK), lambda b, p, i: (0, 0)),
        ],
        out_specs=pl.BlockSpec((SB_Q_BLOCK, LANES), lambda b, p, i: (b * nq + i, p)),
        scratch_shapes=[
            pltpu.VMEM((LANES, seq), BF16),
            pltpu.VMEM((SB_SUB_BLOCKS, 2, HEAD_DIM, ATT_BLOCK), F32),
            pltpu.VMEM((SB_SUB_BLOCKS, 2, 1, ATT_BLOCK), F32),
            pltpu.VMEM((4 * SB_SUB_BLOCKS, ATT_BLOCK, ATT_BLOCK), F32),
            pltpu.VMEM((4 * SB_SUB_BLOCKS, ATT_BLOCK, ATT_BLOCK), BF16),
            pltpu.VMEM((4 * SB_SUB_BLOCKS, 1, ATT_BLOCK), F32),
        ],
        compiler_params=_params("parallel", "parallel", "arbitrary"),
        name="sb_attention",
    )(proj, proj, proj, upper)


def _fox_gate_body(x_ref, g_ref, w_ref, b_ref, tri_ref, c_ref):
    h = _rms_normalize(x_ref[...], g_ref[...])
    logits = _dot_split(h, w_ref[...])
    log_f = -_softplus(-(logits + b_ref[...]))
    seq = log_f.shape[0]
    carry = jnp.zeros((1, LANES), F32)
    for blk in range(seq // ATT_BLOCK):
        rows = slice(blk * ATT_BLOCK, (blk + 1) * ATT_BLOCK)
        c = carry
        for part in _split_bf16(log_f[rows], 3):
            c = c + jnp.dot(tri_ref[...], part, preferred_element_type=F32)
        c_ref[rows, :] = c
        carry = c[ATT_BLOCK - 1:ATT_BLOCK, :]


def _fox_cum_log_forget(x, gain, w_gate, b_gate, batch, seq):
    t, d = x.shape
    w = jnp.zeros((d, LANES), F32).at[:, :N_HEADS].set(w_gate)
    b = jnp.zeros((1, LANES), F32).at[0, :N_HEADS].set(b_gate)
    idx = jnp.arange(ATT_BLOCK)
    tri = (idx[:, None] >= idx[None, :]).astype(BF16)
    return pl.pallas_call(
        _fox_gate_body,
        out_shape=jax.ShapeDtypeStruct((t, LANES), F32),
        grid=(batch,),
        in_specs=[
            pl.BlockSpec((seq, d), lambda i: (i, 0)),
            pl.BlockSpec((1, d), lambda i: (0, 0)),
            pl.BlockSpec((d, LANES), lambda i: (0, 0)),
            pl.BlockSpec((1, LANES), lambda i: (0, 0)),
            pl.BlockSpec((ATT_BLOCK, ATT_BLOCK), lambda i: (0, 0)),
        ],
        out_specs=pl.BlockSpec((seq, LANES), lambda i: (i, 0)),
        compiler_params=_params("parallel"),
        name="fox_gate",
    )(x, gain.reshape(1, d), w, b, tri)


def _fox_body(q_ref, k_ref, v_ref, cq_ref, ck_ref, o_ref, vt_scr, ckb_scr, a_scr, p_scr):
    _transpose_values(v_ref, vt_scr)
    for hh in range(2):
        for c in range(ckb_scr.shape[1] // ATT_BLOCK):
            rows = slice(c * ATT_BLOCK, (c + 1) * ATT_BLOCK)
            ckb_scr[hh, rows, :] = jnp.broadcast_to(ck_ref[0, hh, :, rows], (LANES, ATT_BLOCK)).T

    key = lax.broadcasted_iota(jnp.int32, (ATT_BLOCK, ATT_Q_BLOCK), 0)
    query = lax.broadcasted_iota(jnp.int32, (ATT_BLOCK, ATT_Q_BLOCK), 1)
    row_shape = (1, ATT_Q_BLOCK)

    def sweep(q, first_slot):
        n_blocks = (q + 1) * ATT_K_PER_Q
        q_rows = slice(q * ATT_Q_BLOCK, (q + 1) * ATT_Q_BLOCK)
        qs = _split_head_pair(q_ref[q_rows, :])
        tops = [jnp.full(row_shape, NEG_INF, F32)] * 2
        for kb in range(n_blocks):
            k_rows = slice(kb * ATT_BLOCK, (kb + 1) * ATT_BLOCK)
            k = k_ref[k_rows, :]
            for hh in range(2):
                a = _qk(k, qs[hh]) - jnp.concatenate([ckb_scr[hh, k_rows, :]] * (ATT_Q_BLOCK // LANES), axis=1)
                if kb >= n_blocks - ATT_K_PER_Q:
                    a = jnp.where(key + (kb * ATT_BLOCK - q * ATT_Q_BLOCK) <= query, a, NEG_INF)
                a_scr[hh, first_slot + kb] = a
                tops[hh] = jnp.maximum(tops[hh], jnp.max(a, axis=0, keepdims=True))
        shifts = [cq_ref[0, hh, :, q_rows] - (tops[hh] + cq_ref[0, hh, :, q_rows]) for hh in range(2)]
        sums = [jnp.zeros(row_shape, F32)] * 2
        for kb in range(n_blocks):
            for hh in range(2):
                p = jnp.exp(a_scr[hh, first_slot + kb] + shifts[hh])
                p_scr[hh, first_slot + kb] = p.astype(BF16)
                sums[hh] = sums[hh] + jnp.sum(p, axis=0, keepdims=True)
        accs = [jnp.zeros((HEAD_DIM, ATT_Q_BLOCK), F32)] * 2
        for kb in range(n_blocks):
            for hh in range(2):
                vt = vt_scr[hh * HEAD_DIM:(hh + 1) * HEAD_DIM, kb * ATT_BLOCK:(kb + 1) * ATT_BLOCK]
                accs[hh] = accs[hh] + jnp.dot(vt, p_scr[hh, first_slot + kb], preferred_element_type=F32)
        out_t = jnp.concatenate([accs[0] / sums[0], accs[1] / sums[1]], axis=0)
        o_ref[q_rows, :] = out_t.T.astype(o_ref.dtype)

    first_slot = 0
    for q in range(k_ref.shape[0] // ATT_Q_BLOCK):
        sweep(q, first_slot)
        first_slot += (q + 1) * ATT_K_PER_Q


def _fox_attention(proj, cum, batch, seq):
    t = proj.shape[0]
    nq = seq // ATT_Q_BLOCK
    cum_h = cum[:, :N_HEADS].reshape(batch, seq, N_HEADS).transpose(0, 2, 1)
    cum_rows = cum_h.reshape(batch, N_HEADS, 1, seq)
    n_slots = sum((q + 1) * ATT_K_PER_Q for q in range(nq))
    return pl.pallas_call(
        _fox_body,
        out_shape=jax.ShapeDtypeStruct((t, D_MODEL), BF16),
        grid=(batch, HEAD_PAIRS),
        in_specs=[
            pl.BlockSpec((seq, LANES), lambda b, p: (b, p)),
            pl.BlockSpec((seq, LANES), lambda b, p: (b, HEAD_PAIRS + p)),
            pl.BlockSpec((seq, LANES), lambda b, p: (b, 2 * HEAD_PAIRS + p)),
            pl.BlockSpec((1, 2, 1, seq), lambda b, p: (b, p, 0, 0)),
            pl.BlockSpec((1, 2, 1, seq), lambda b, p: (b, p, 0, 0)),
        ],
        out_specs=pl.BlockSpec((seq, LANES), lambda b, p: (b, p)),
        scratch_shapes=[
            pltpu.VMEM((LANES, seq), BF16),
            pltpu.VMEM((2, seq, LANES), F32),
            pltpu.VMEM((2, n_slots, ATT_BLOCK, ATT_Q_BLOCK), F32),
            pltpu.VMEM((2, n_slots, ATT_BLOCK, ATT_Q_BLOCK), BF16),
        ],
        compiler_params=_params("parallel", "parallel"),
        name="fox_attention",
    )(proj, proj, proj, cum_rows, cum_rows)


def _band_body(q_ref, k_ref, v_ref, bias_ref, o_ref, s_ref, vt_scr, logit_scr, p_scr, inv_scr, *, sub_len):
    seq = v_ref.shape[0]
    n_blocks = seq // DIL_BLOCK
    _transpose_values(v_ref, vt_scr)

    def key_rows(n):
        first = (n * DIL_BLOCK) % sub_len == 0
        return first, slice((n if first else n - 1) * DIL_BLOCK, (n + 1) * DIL_BLOCK)

    for n in range(n_blocks):
        first, k_rows = key_rows(n)
        qs = _split_head_pair(q_ref[n * DIL_BLOCK:(n + 1) * DIL_BLOCK, :])
        k = k_ref[k_rows, :]
        for hh in range(2):
            bias = bias_ref[hh, DIL_BLOCK:, :] if first else bias_ref[hh]
            logit_scr[2 * n + hh, :k.shape[0], :] = _qk(k, qs[hh]) + bias

    for n in range(n_blocks):
        first, k_rows = key_rows(n)
        n_keys = k_rows.stop - k_rows.start
        for hh in range(2):
            logits = logit_scr[2 * n + hh, :n_keys, :]
            m = jnp.max(logits, axis=0, keepdims=True)
            p = jnp.exp(logits - m)
            l = jnp.sum(p, axis=0, keepdims=True)
            p_scr[2 * n + hh, :n_keys, :] = p.astype(BF16)
            inv_scr[2 * n + hh] = 1.0 / l
            s_ref[0, 0, hh:hh + 1, n * DIL_BLOCK:(n + 1) * DIL_BLOCK] = m + jnp.log(l)

    for n in range(n_blocks):
        first, k_rows = key_rows(n)
        n_keys = k_rows.stop - k_rows.start
        outs = []
        for hh in range(2):
            vt = vt_scr[hh * HEAD_DIM:(hh + 1) * HEAD_DIM, k_rows]
            o = jnp.dot(vt, p_scr[2 * n + hh, :n_keys, :], preferred_element_type=F32)
            outs.append(o * inv_scr[2 * n + hh])
        o_ref[n * DIL_BLOCK:(n + 1) * DIL_BLOCK, :] = jnp.concatenate(outs, axis=0).T.astype(o_ref.dtype)


def _band_attention(qk, v, bias_t, batch, seq, sub_len):
    t = v.shape[0]
    return pl.pallas_call(
        functools.partial(_band_body, sub_len=sub_len),
        out_shape=(jax.ShapeDtypeStruct((t, D_MODEL), BF16),
                   jax.ShapeDtypeStruct((batch, HEAD_PAIRS, 2, seq), F32)),
        grid=(batch, HEAD_PAIRS),
        in_specs=[
            pl.BlockSpec((None, seq, LANES), lambda b, p: (0, b, p)),
            pl.BlockSpec((None, seq, LANES), lambda b, p: (1, b, p)),
            pl.BlockSpec((seq, LANES), lambda b, p: (b, p)),
            pl.BlockSpec((2, 2 * DIL_BLOCK, DIL_BLOCK), lambda b, p: (p, 0, 0)),
        ],
        out_specs=(pl.BlockSpec((seq, LANES), lambda b, p: (b, p)),
                   pl.BlockSpec((1, 1, 2, seq), lambda b, p: (b, p, 0, 0))),
        scratch_shapes=[pltpu.VMEM((LANES, seq), BF16),
                        pltpu.VMEM((2 * seq // DIL_BLOCK, 2 * DIL_BLOCK, DIL_BLOCK), F32),
                        pltpu.VMEM((2 * seq // DIL_BLOCK, 2 * DIL_BLOCK, DIL_BLOCK), BF16),
                        pltpu.VMEM((2 * seq // DIL_BLOCK, 1, DIL_BLOCK), F32)],
        compiler_params=_params("parallel", "parallel"),
        name="band_attention",
    )(qk, qk, v, bias_t)


def _softmax_merge_body(o1_ref, o2_ref, o3_ref, s_ref, e_ref, o_ref):
    s = s_ref[...]
    groups = [s, pltpu.roll(s, LANES - N_HEADS, axis=1), pltpu.roll(s, LANES - 2 * N_HEADS, axis=1)]
    top = jnp.maximum(jnp.maximum(groups[0], groups[1]), groups[2])
    weights = [jnp.exp(g - top) for g in groups]
    inv = 1.0 / (weights[0] + weights[1] + weights[2])
    out = jnp.zeros(o_ref.shape, F32)
    for w, part in zip(weights, (o1_ref, o2_ref, o3_ref)):
        spread = jnp.dot((w * inv).astype(BF16), e_ref[...], preferred_element_type=F32)
        out = out + spread * part[...].astype(F32)
    o_ref[...] = out.astype(o_ref.dtype)


def _softmax_merge(outs, s_all):
    t, d = outs[0].shape
    head_of = jnp.arange(d) // HEAD_DIM
    expand = (jnp.arange(LANES)[:, None] == head_of[None, :]).astype(BF16)
    rows = pl.BlockSpec((ROW_TILE, d), lambda i: (i, 0))
    return pl.pallas_call(
        _softmax_merge_body,
        out_shape=jax.ShapeDtypeStruct((t, d), BF16),
        grid=(t // ROW_TILE,),
        in_specs=[rows, rows, rows,
                  pl.BlockSpec((ROW_TILE, LANES), lambda i: (i, 0)),
                  pl.BlockSpec((LANES, d), lambda i: (0, 0))],
        out_specs=rows,
        compiler_params=_params("parallel"),
        name="softmax_merge",
    )(*outs, s_all, expand)


def _t5_causal_bucket(distance):
    max_exact = N_REL_BUCKETS // 2
    d = jnp.maximum(distance, 1).astype(F32)
    log_b = max_exact + (jnp.log(d / max_exact) / math.log(REL_MAX_DISTANCE / max_exact)
                         * (N_REL_BUCKETS - max_exact)).astype(jnp.int32)
    log_b = jnp.minimum(log_b, N_REL_BUCKETS - 1)
    return jnp.where(distance < max_exact, distance, log_b)


def _dilated_bias(rel_bias):
    kj = jnp.arange(2 * DIL_BLOCK, dtype=jnp.int32)
    qi = jnp.arange(DIL_BLOCK, dtype=jnp.int32)
    delta = qi[None, :] + DIL_BLOCK - kj[:, None]
    in_band = (delta >= 0) & (delta <= DIL_SPAN)
    buckets = jnp.stack([_t5_causal_bucket(jnp.maximum(delta, 0) * dil) for _, dil in DILATED_PAIRS])
    one_hot = (buckets[..., None] == jnp.arange(N_REL_BUCKETS)).astype(F32)
    bias = jnp.einsum("gkqb,bh->ghkq", one_hot, rel_bias.astype(F32), precision=lax.Precision.HIGHEST)
    return jnp.where(in_band[None, None], bias, NEG_INF)


def _dilated_proj_body(x_ref, g_ref, w_ref, cs_ref, bd_ref, *rest):
    n_groups = len(DILATED_PAIRS)
    qk_refs, v_refs = rest[:n_groups], rest[n_groups:2 * n_groups]
    h_scr, res_scr = rest[2 * n_groups:]
    j = pl.program_id(1)

    @pl.when(j == 0)
    def _():
        h_scr[...] = _rms_normalize(x_ref[...], g_ref[...]).astype(BF16)

    acc = jnp.dot(h_scr[...], w_ref[...], preferred_element_type=F32)

    def keep(res, first_lane):
        for c in range(res.shape[1] // LANES):
            res_scr[first_lane // LANES + c] = res[:, c * LANES:(c + 1) * LANES]

    @pl.when(j < 2 * n_groups)
    def _():
        for c in range(COL_TILE // NORM_CHUNK):
            sl = slice(c * NORM_CHUNK, (c + 1) * NORM_CHUNK)
            a = acc[:, sl]
            ss = jnp.dot((a * a).astype(BF16), bd_ref[...], preferred_element_type=F32)
            keep(a * lax.rsqrt(ss * (1.0 / HEAD_DIM) + RMS_EPS) * cs_ref[:, sl], c * NORM_CHUNK)

    @pl.when(j == 2 * n_groups)
    def _():
        keep(acc * cs_ref[...], 0)

    def write_classes(ref, dil):
        rows = res_scr.shape[1] // dil
        for c in range(res_scr.shape[0]):
            lanes = slice(c * LANES, (c + 1) * LANES)
            if dil == 1:
                ref[:, lanes] = res_scr[c].astype(ref.dtype)
            else:
                for r in range(dil):
                    ref[r, :, lanes] = res_scr[c, pl.ds(r, rows, stride=dil), :].astype(ref.dtype)

    for g, (_, dil) in enumerate(DILATED_PAIRS):
        pl.when(j // 2 == g)(functools.partial(write_classes, qk_refs[g], dil))
        pl.when(j == 2 * n_groups)(functools.partial(write_classes, v_refs[g], dil))


def _dilated_proj(x, gain, w, layer, col_scale, batch, seq):
    t, d = x.shape
    n = w.shape[2]
    n_groups = len(DILATED_PAIRS)
    tiles_per_seq = seq // DIL_ROW_TILE
    head_id = jnp.arange(NORM_CHUNK) // HEAD_DIM
    block_diag = (head_id[:, None] == head_id[None, :]).astype(BF16)
    out_shape, out_specs = [], []
    for kind in ("qk", "v"):
        for g, (_, dil) in enumerate(DILATED_PAIRS):
            rows = DIL_ROW_TILE // dil
            if kind == "qk":
                shape = (2, batch, dil, seq // dil, d)
                block = (None, None, dil, rows, d)
                index = lambda i, j, g=g: (jnp.clip(j - 2 * g, 0, 1), i // tiles_per_seq, 0, i % tiles_per_seq, 0)
            else:
                shape = (batch, dil, seq // dil, d)
                block = (None, dil, rows, d)
                index = lambda i, j: (i // tiles_per_seq, 0, i % tiles_per_seq, 0)
            if dil == 1:
                block = block[:-3] + (None,) + block[-2:]
            out_shape.append(jax.ShapeDtypeStruct(shape, BF16))
            out_specs.append(pl.BlockSpec(block, index))
    outs = pl.pallas_call(
        _dilated_proj_body,
        out_shape=tuple(out_shape),
        grid=(t // DIL_ROW_TILE, n // COL_TILE),
        in_specs=[
            pl.BlockSpec((DIL_ROW_TILE, d), lambda i, j: (i, 0)),
            pl.BlockSpec((1, d), lambda i, j: (0, 0)),
            pl.BlockSpec((None, d, COL_TILE), lambda i, j: (layer, 0, j)),
            pl.BlockSpec((1, COL_TILE), lambda i, j: (0, j)),
            pl.BlockSpec((NORM_CHUNK, NORM_CHUNK), lambda i, j: (0, 0)),
        ],
        out_specs=tuple(out_specs),
        scratch_shapes=[pltpu.VMEM((DIL_ROW_TILE, d), BF16),
                        pltpu.VMEM((COL_TILE // LANES, DIL_ROW_TILE, LANES), F32)],
        compiler_params=_params("parallel", "arbitrary"),
        name="dilated_proj",
    )(x, gain.reshape(1, d), w, col_scale.reshape(1, n).astype(F32), block_diag)
    qk = [o.reshape(2, t, d) for o in outs[:n_groups]]
    v = [o.reshape(t, d) for o in outs[n_groups:]]
    return qk, v


def _dilated_attention(qk, v, rel_bias, batch, seq):
    t = v[0].shape[0]
    n_groups = len(DILATED_PAIRS)
    bias_t = _dilated_bias(rel_bias)
    outs, stats = [], []
    for g, (_, dil) in enumerate(DILATED_PAIRS):
        sub_len = seq // dil
        o, s = _band_attention(qk[g], v[g], bias_t[g], batch, seq, sub_len)
        o = o.reshape(batch, dil, sub_len, D_MODEL).transpose(0, 2, 1, 3).reshape(t, D_MODEL)
        s = s.reshape(batch, HEAD_PAIRS, 2, dil, sub_len).swapaxes(3, 4)
        outs.append(o)
        stats.append(s.reshape(batch, N_HEADS, seq).transpose(0, 2, 1).reshape(t, N_HEADS))
    stats.append(jnp.zeros((t, LANES - n_groups * N_HEADS), F32))
    return _softmax_merge(outs, jnp.concatenate(stats, axis=1))


def _tile_heads(v):
    return jnp.tile(v.astype(F32), N_HEADS)


def _sb_mixer(x, gain, w_qkv, layer, batch, seq):
    ones = jnp.ones((D_MODEL,), F32)
    col_scale = jnp.concatenate([ones * QK_SCALE, ones, ones])
    proj = _norm_proj(x, gain, w_qkv, layer, col_scale, 0)
    return _sb_attention(proj, batch, seq)


def _dilated_mixer(x, gain, w_in, q_norm, k_norm, rel_bias, layer, batch, seq):
    scales = []
    for g in range(len(DILATED_PAIRS)):
        scales += [_tile_heads(q_norm[g]) * QK_SCALE, _tile_heads(k_norm[g])]
    scales.append(jnp.ones((D_MODEL,), F32))
    qk, v = _dilated_proj(x, gain, w_in, layer, jnp.concatenate(scales), batch, seq)
    return _dilated_attention(qk, v, rel_bias, batch, seq)


def _fox_mixer(x, gain, w_qkv, w_gate, b_f, q_norm, k_norm, layer, batch, seq):
    col_scale = jnp.concatenate([_tile_heads(q_norm) * QK_SCALE, _tile_heads(k_norm),
                                 jnp.ones((D_MODEL,), F32)])
    proj = _norm_proj(x, gain, w_qkv, layer, col_scale, 2)
    cum = _fox_cum_log_forget(x, gain, w_gate, b_f, batch, seq)
    return _fox_attention(proj, cum, batch, seq)


def kernel(x, sb_w_qkv, sb_w_o, dil_w_in, dil_q_norm, dil_k_norm, dil_w_o, fox_w_in, fox_b_f,
           fox_q_norm, fox_k_norm, fox_w_o, rel_bias, attn_norm, ffn_norm, mlp_w_gate, mlp_w_up,
           mlp_w_down, moe_router, moe_w_gate, moe_w_up, moe_w_down):
    batch, seq, d = x.shape
    depth = attn_norm.shape[0]
    assert d == D_MODEL and seq % max(ROW_TILE, DIL_ROW_TILE, ATT_Q_BLOCK, SB_Q_BLOCK) == 0
    assert all((seq // dil) % DIL_BLOCK == 0 for _, dil in DILATED_PAIRS)
    sb_w_qkv, sb_w_o, dil_w_in, dil_w_o, fox_w_o, mlp_w_gate, mlp_w_up, mlp_w_down = (
        w.astype(BF16) for w in (sb_w_qkv, sb_w_o, dil_w_in, dil_w_o, fox_w_o, mlp_w_gate, mlp_w_up,
                                 mlp_w_down))
    moe_w_gate, moe_w_up, moe_w_down = (w.astype(BF16) for w in (moe_w_gate, moe_w_up, moe_w_down))
    fox_w_qkv = fox_w_in[:, :, :3 * D_MODEL].astype(BF16)
    h = x.reshape(batch * seq, d)
    for i in range(depth):
        kind, j = i % 3, i // 3
        if kind == 0:
            mixed, w_o = _sb_mixer(h, attn_norm[i], sb_w_qkv, j, batch, seq), sb_w_o
        elif kind == 1:
            mixed, w_o = _dilated_mixer(h, attn_norm[i], dil_w_in, dil_q_norm[j], dil_k_norm[j], rel_bias,
                                        j, batch, seq), dil_w_o
        else:
            mixed, w_o = _fox_mixer(h, attn_norm[i], fox_w_qkv, fox_w_in[j, :, 3 * D_MODEL:], fox_b_f[j],
                                    fox_q_norm[j], fox_k_norm[j], j, batch, seq), fox_w_o
        f = i // 2
        if i % 2 == 0:
            h = _mix_ffn_residual(h, mixed, w_o, j, ffn_norm[i], mlp_w_gate, mlp_w_up, mlp_w_down, f)
        else:
            h = _mix_moe_residual(h, mixed, w_o, j, ffn_norm[i], moe_router[f], moe_w_gate, moe_w_up,
                                  moe_w_down, f)
    return h.reshape(batch, seq, d)
```

```python
import functools
import math

import jax
import jax.numpy as jnp
from jax import lax
from jax.experimental import pallas as pl
from jax.experimental.pallas import tpu as pltpu
from jax.experimental.pallas import tpu_sc as plsc

D_MODEL = 1024
N_HEADS = 16
HEAD_DIM = 64
LANES = 128
HEAD_PAIRS = D_MODEL // LANES
N_EXPERTS = 8
N_REL_BUCKETS = 32
REL_MAX_DISTANCE = 2048
DILATED_PAIRS = ((128, 1), (512, 4), (2048, 16))
DIL_SPAN = 128
RMS_EPS = 1e-6
NEG_INF = -1e30
SB_UNDERFLOW_LOG = -104.0
QK_SCALE = 1.0 / math.sqrt(HEAD_DIM)

ROW_TILE = 1024
COL_TILE = 1024
FF_TILE = 1792
SWIGLU_CHUNK = 256
MOE_ROW_TILE = 512
MOE_FF_TILE = 1792
SC_CORES = 2
SC_SUBCORES = 16
SC_CHUNK_ROWS = 64
NORM_CHUNK = 256
ATT_BLOCK = 256
ATT_Q_BLOCK = 512
ATT_K_PER_Q = ATT_Q_BLOCK // ATT_BLOCK
SB_Q_BLOCK = 2048
SB_SUB_BLOCKS = SB_Q_BLOCK // ATT_BLOCK
DIL_BLOCK = 128
DIL_ROW_TILE = 1024
VMEM_LIMIT = 56 * 1024 * 1024

F32 = jnp.float32
BF16 = jnp.bfloat16


def _params(*semantics):
    return pltpu.CompilerParams(dimension_semantics=semantics, vmem_limit_bytes=VMEM_LIMIT)


def _rms_normalize(x, gain):
    inv = lax.rsqrt(jnp.mean(x * x, axis=-1, keepdims=True) + RMS_EPS)
    return x * inv * gain


def _split_bf16(x, terms):
    parts = []
    for _ in range(terms):
        part = x.astype(BF16)
        parts.append(part)
        x = x - part.astype(F32)
    return parts


def _dot_split(a, b):
    a_hi, a_lo = _split_bf16(a, 2)
    b_hi, b_lo = _split_bf16(b, 2)
    return (jnp.dot(a_hi, b_hi, preferred_element_type=F32)
            + (jnp.dot(a_hi, b_lo, preferred_element_type=F32)
               + jnp.dot(a_lo, b_hi, preferred_element_type=F32)))


def _softplus(z):
    return jnp.maximum(z, 0.0) + jnp.log(1.0 + jnp.exp(-jnp.abs(z)))


def _norm_proj_body(x_ref, g_ref, w_ref, cs_ref, bd_ref, o_ref, h_scr, *, n_norm):
    j = pl.program_id(1)

    @pl.when(j == 0)
    def _():
        h_scr[...] = _rms_normalize(x_ref[...], g_ref[...]).astype(BF16)

    acc = jnp.dot(h_scr[...], w_ref[...], preferred_element_type=F32)

    def plain():
        o_ref[...] = (acc * cs_ref[...]).astype(o_ref.dtype)

    def head_normed():
        for c in range(COL_TILE // NORM_CHUNK):
            sl = slice(c * NORM_CHUNK, (c + 1) * NORM_CHUNK)
            a = acc[:, sl]
            ss = jnp.dot((a * a).astype(BF16), bd_ref[...], preferred_element_type=F32)
            inv = lax.rsqrt(ss * (1.0 / HEAD_DIM) + RMS_EPS)
            o_ref[:, sl] = (a * inv * cs_ref[:, sl]).astype(o_ref.dtype)

    if n_norm == 0:
        plain()
    else:
        pl.when(j < n_norm)(head_normed)
        pl.when(j >= n_norm)(plain)


def _norm_proj(x, gain, w, layer, col_scale, n_norm):
    t, d = x.shape
    n = w.shape[2]
    head_id = jnp.arange(NORM_CHUNK) // HEAD_DIM
    block_diag = (head_id[:, None] == head_id[None, :]).astype(BF16)
    return pl.pallas_call(
        functools.partial(_norm_proj_body, n_norm=n_norm),
        out_shape=jax.ShapeDtypeStruct((t, n), BF16),
        grid=(t // ROW_TILE, n // COL_TILE),
        in_specs=[
            pl.BlockSpec((ROW_TILE, d), lambda i, j: (i, 0)),
            pl.BlockSpec((1, d), lambda i, j: (0, 0)),
            pl.BlockSpec((None, d, COL_TILE), lambda i, j: (layer, 0, j)),
            pl.BlockSpec((1, COL_TILE), lambda i, j: (0, j)),
            pl.BlockSpec((NORM_CHUNK, NORM_CHUNK), lambda i, j: (0, 0)),
        ],
        out_specs=pl.BlockSpec((ROW_TILE, COL_TILE), lambda i, j: (i, j)),
        scratch_shapes=[pltpu.VMEM((ROW_TILE, d), BF16)],
        compiler_params=_params("parallel", "arbitrary"),
        name="norm_proj",
    )(x, gain.reshape(1, d), w, col_scale.reshape(1, n).astype(F32), block_diag)


def _swiglu_hidden(h, wg, wu):
    g = jnp.dot(h, wg, preferred_element_type=F32)
    u = jnp.dot(h, wu, preferred_element_type=F32)
    return g * (1.0 / (1.0 + jnp.exp(-g))) * u


def _swiglu_down(h, wg_ref, wu_ref, wd_ref):
    y = None
    for c in range(wg_ref.shape[1] // SWIGLU_CHUNK):
        cols = slice(c * SWIGLU_CHUNK, (c + 1) * SWIGLU_CHUNK)
        a = _swiglu_hidden(h, wg_ref[:, cols], wu_ref[:, cols]).astype(BF16)
        part = jnp.dot(a, wd_ref[cols, :], preferred_element_type=F32)
        y = part if y is None else y + part
    return y


def _ffn_body(x_ref, o_ref, wo_ref, g_ref, wg_ref, wu_ref, wd_ref, y_ref, h_scr):
    f = pl.program_id(1)

    @pl.when(f == 0)
    def _():
        x = x_ref[...] + jnp.dot(o_ref[...], wo_ref[...], preferred_element_type=F32)
        h_scr[...] = _rms_normalize(x, g_ref[...]).astype(BF16)
        y_ref[...] = x

    y_ref[...] += _swiglu_down(h_scr[...], wg_ref, wu_ref, wd_ref)


def _mix_ffn_residual(x, o, w_o, mix_layer, gain, w_gate, w_up, w_down, layer):
    t, d = x.shape
    ff = w_gate.shape[2]
    return pl.pallas_call(
        _ffn_body,
        out_shape=jax.ShapeDtypeStruct((t, d), F32),
        grid=(t // ROW_TILE, ff // FF_TILE),
        in_specs=[
            pl.BlockSpec((ROW_TILE, d), lambda i, f: (i, 0)),
            pl.BlockSpec((ROW_TILE, d), lambda i, f: (i, 0)),
            pl.BlockSpec((None, d, d), lambda i, f: (mix_layer, 0, 0)),
            pl.BlockSpec((1, d), lambda i, f: (0, 0)),
            pl.BlockSpec((None, d, FF_TILE), lambda i, f: (layer, 0, f)),
            pl.BlockSpec((None, d, FF_TILE), lambda i, f: (layer, 0, f)),
            pl.BlockSpec((None, FF_TILE, d), lambda i, f: (layer, f, 0)),
        ],
        out_specs=pl.BlockSpec((ROW_TILE, d), lambda i, f: (i, 0)),
        scratch_shapes=[pltpu.VMEM((ROW_TILE, d), BF16)],
        compiler_params=_params("parallel", "arbitrary"),
        name="ffn",
    )(x, o, w_o, gain.reshape(1, d), w_gate, w_up, w_down)


def _pack_bf16_pairs(x):
    half = x.shape[1] // 2
    bits = pltpu.bitcast(x.astype(BF16).astype(F32), jnp.int32)
    return bits[:, :half] | lax.shift_right_logical(bits[:, half:], jnp.int32(16))


def _unpack_bf16_pairs(p):
    left = pltpu.bitcast(p & jnp.int32(-65536), F32)
    right = pltpu.bitcast(lax.shift_left(p, jnp.int32(16)), F32)
    return jnp.concatenate([left, right], axis=1).astype(BF16)


def _router_body(x_ref, o_ref, wo_ref, g_ref, r_ref, tri_ref, gates_ref, rank_ref, h_ref, x1_ref, count_scr):
    @pl.when(pl.program_id(0) == 0)
    def _():
        count_scr[...] = jnp.zeros_like(count_scr)

    x1 = x_ref[...] + jnp.dot(o_ref[...], wo_ref[...], preferred_element_type=F32)
    x1_ref[...] = x1
    h = _rms_normalize(x1, g_ref[...])
    h_ref[...] = _pack_bf16_pairs(h)
    logits = _dot_split(h, r_ref[...])
    lane = lax.broadcasted_iota(jnp.int32, logits.shape, 1).astype(F32)
    logits = jnp.where(lane < N_EXPERTS, logits, -jnp.inf)
    m1 = jnp.max(logits, axis=-1, keepdims=True)
    i1 = jnp.min(jnp.where(logits == m1, lane, float(LANES)), axis=-1, keepdims=True)
    rest = jnp.where(lane == i1, -jnp.inf, logits)
    m2 = jnp.max(rest, axis=-1, keepdims=True)
    i2 = jnp.min(jnp.where(rest == m2, lane, float(LANES)), axis=-1, keepdims=True)
    e = jnp.exp(m2 - m1)
    g1 = 1.0 / (1.0 + e)
    gates = jnp.where(lane == i1, g1, 0.0) + jnp.where(lane == i2, e * g1, 0.0)
    gates_ref[...] = gates.T[:N_EXPERTS, :]
    chosen = jnp.where((lane == i1) | (lane == i2), 1.0, 0.0)
    inclusive = jnp.dot(tri_ref[...], chosen.astype(BF16), preferred_element_type=F32)
    rank = jnp.where(chosen > 0.0, inclusive - 1.0 + count_scr[...], -1.0)
    rank_ref[...] = rank.T[:N_EXPERTS, :]
    count_scr[...] += inclusive[ROW_TILE - 1:ROW_TILE, :]


def _mix_router(x, o, w_o, mix_layer, gain, router):
    t, d = x.shape
    r = jnp.zeros((d, LANES), F32).at[:, :N_EXPERTS].set(router)
    idx = jnp.arange(ROW_TILE)
    tri = (idx[:, None] >= idx[None, :]).astype(BF16)
    rows = pl.BlockSpec((ROW_TILE, d), lambda i: (i, 0))
    return pl.pallas_call(
        _router_body,
        out_shape=(jax.ShapeDtypeStruct((N_EXPERTS, t), F32), jax.ShapeDtypeStruct((N_EXPERTS, t), F32),
                   jax.ShapeDtypeStruct((t, d // 2), jnp.int32), jax.ShapeDtypeStruct((t, d), F32)),
        grid=(t // ROW_TILE,),
        in_specs=[
            rows,
            rows,
            pl.BlockSpec((None, d, d), lambda i: (mix_layer, 0, 0)),
            pl.BlockSpec((1, d), lambda i: (0, 0)),
            pl.BlockSpec((d, LANES), lambda i: (0, 0)),
            pl.BlockSpec((ROW_TILE, ROW_TILE), lambda i: (0, 0)),
        ],
        out_specs=(pl.BlockSpec((N_EXPERTS, ROW_TILE), lambda i: (0, i)),
                   pl.BlockSpec((N_EXPERTS, ROW_TILE), lambda i: (0, i)),
                   pl.BlockSpec((ROW_TILE, d // 2), lambda i: (i, 0)),
                   rows),
        scratch_shapes=[pltpu.VMEM((1, LANES), F32)],
        compiler_params=_params("arbitrary"),
        name="router",
    )(x, o, w_o, gain.reshape(1, d), r, tri)


def _sc_mesh():
    return plsc.VectorSubcoreMesh(core_axis_name="core", subcore_axis_name="subcore",
                                  num_cores=SC_CORES, num_subcores=SC_SUBCORES)


def _sc_worker_base(per_worker):
    return (lax.axis_index("subcore") * SC_CORES + lax.axis_index("core")) * per_worker


def _sc_row_gather(table, idx):
    width = table.shape[1]
    n = idx.shape[0]
    per_worker = n // (SC_CORES * SC_SUBCORES)
    n_chunks = per_worker // SC_CHUNK_ROWS
    assert n == n_chunks * SC_CHUNK_ROWS * SC_CORES * SC_SUBCORES and n_chunks % 2 == 0

    @functools.partial(
        pl.kernel, mesh=_sc_mesh(), out_type=jax.ShapeDtypeStruct((n, width), table.dtype),
        scratch_types=[pltpu.VMEM((SC_CHUNK_ROWS,), jnp.int32), pltpu.VMEM((SC_CHUNK_ROWS,), jnp.int32),
                       pltpu.VMEM((SC_CHUNK_ROWS, width), table.dtype),
                       pltpu.VMEM((SC_CHUNK_ROWS, width), table.dtype),
                       pltpu.SemaphoreType.DMA, pltpu.SemaphoreType.DMA],
        name="sc_row_gather")
    def gather(table_hbm, idx_hbm, out_hbm, idx_a, idx_b, rows_a, rows_b, sem_a, sem_b):
        base = _sc_worker_base(per_worker)
        bufs = ((idx_a, rows_a, sem_a), (idx_b, rows_b, sem_b))

        def rows_of(c):
            return pl.ds(pl.multiple_of(base + c * SC_CHUNK_ROWS, SC_CHUNK_ROWS), SC_CHUNK_ROWS)

        def fetch(c, buf):
            idx_v, rows_v, sem = buf
            pltpu.sync_copy(idx_hbm.at[rows_of(c)], idx_v)
            return pltpu.make_async_copy(table_hbm.at[idx_v], rows_v, sem)

        fetch(0, bufs[0]).start()

        @pl.loop(0, n_chunks, step=2)
        def _(c):
            for b in range(2):
                idx_v, rows_v, sem = bufs[b]
                pltpu.make_async_copy(table_hbm.at[idx_v], rows_v, sem).wait()

                @pl.when(c + b + 1 < n_chunks)
                def _():
                    fetch(c + b + 1, bufs[1 - b]).start()

                pltpu.sync_copy(rows_v, out_hbm.at[rows_of(c + b)])

    return gather(table, idx)


def _sc_row_scatter_pair(rows, idx_lo, idx_hi, n_out):
    n, width = rows.shape
    per_worker = n // (SC_CORES * SC_SUBCORES)
    n_chunks = per_worker // SC_CHUNK_ROWS
    assert n == n_chunks * SC_CHUNK_ROWS * SC_CORES * SC_SUBCORES and n_chunks % 2 == 0
    index_scratch = pltpu.VMEM((SC_CHUNK_ROWS,), jnp.int32)
    rows_scratch = pltpu.VMEM((SC_CHUNK_ROWS, width), rows.dtype)

    @functools.partial(
        pl.kernel, mesh=_sc_mesh(), out_type=jax.ShapeDtypeStruct((n_out, width), rows.dtype),
        scratch_types=[index_scratch, index_scratch, rows_scratch, rows_scratch,
                       pltpu.SemaphoreType.DMA, pltpu.SemaphoreType.DMA],
        name="sc_row_scatter")
    def scatter(rows_hbm, lo_hbm, hi_hbm, out_hbm, lo_v, hi_v, rows_a, rows_b, sem_a, sem_b):
        base = _sc_worker_base(per_worker)
        bufs = ((rows_a, sem_a), (rows_b, sem_b))

        def rows_of(c):
            return pl.ds(pl.multiple_of(base + c * SC_CHUNK_ROWS, SC_CHUNK_ROWS), SC_CHUNK_ROWS)

        def load(c, buf):
            rows_v, sem = buf
            return pltpu.make_async_copy(rows_hbm.at[rows_of(c)], rows_v, sem)

        load(0, bufs[0]).start()

        @pl.loop(0, n_chunks, step=2)
        def _(c):
            for b in range(2):
                rows_v, _ = bufs[b]
                load(c + b, bufs[b]).wait()

                @pl.when(c + b + 1 < n_chunks)
                def _():
                    load(c + b + 1, bufs[1 - b]).start()

                pltpu.sync_copy(lo_hbm.at[rows_of(c + b)], lo_v)
                pltpu.sync_copy(hi_hbm.at[rows_of(c + b)], hi_v)
                pltpu.sync_copy(rows_v, out_hbm.at[lo_v])
                pltpu.sync_copy(rows_v, out_hbm.at[hi_v])

    return scatter(rows, idx_lo, idx_hi)


def _expert_ffn_body(te_ref, nu_ref, nv_ref, h_ref, wg_ref, wu_ref, wd_ref, y_ref, acc_scr):
    i = pl.program_id(0)
    f = pl.program_id(1)

    @pl.when(i < nu_ref[0])
    def _():
        row = lax.broadcasted_iota(jnp.int32, h_ref.shape, 0)
        packed = jnp.where(row < nv_ref[i], h_ref[...], 0)
        y = _swiglu_down(_unpack_bf16_pairs(packed), wg_ref, wu_ref, wd_ref)

        @pl.when(f == 0)
        def _():
            acc_scr[...] = y

        @pl.when(f > 0)
        def _():
            acc_scr[...] += y

        @pl.when(f == pl.num_programs(1) - 1)
        def _():
            y_ref[...] = _pack_bf16_pairs(acc_scr[...])


def _expert_ffn(h_sorted, tile_expert, n_used, tile_valid, w_gate, w_up, w_down, layer):
    rows, half = h_sorted.shape
    d = 2 * half
    ff = w_gate.shape[3]
    n_f = ff // MOE_FF_TILE

    def row_map(i, f, te, nu, nv):
        return (jnp.minimum(i, nu[0] - 1), 0)

    def col_step(i, f, nu):
        return jnp.where(i < nu[0], f, n_f - 1)

    grid_spec = pltpu.PrefetchScalarGridSpec(
        num_scalar_prefetch=3,
        grid=(rows // MOE_ROW_TILE, n_f),
        in_specs=[
            pl.BlockSpec((MOE_ROW_TILE, half), row_map),
            pl.BlockSpec((None, None, d, MOE_FF_TILE),
                         lambda i, f, te, nu, nv: (layer, te[i], 0, col_step(i, f, nu))),
            pl.BlockSpec((None, None, d, MOE_FF_TILE),
                         lambda i, f, te, nu, nv: (layer, te[i], 0, col_step(i, f, nu))),
            pl.BlockSpec((None, None, MOE_FF_TILE, d),
                         lambda i, f, te, nu, nv: (layer, te[i], col_step(i, f, nu), 0)),
        ],
        out_specs=pl.BlockSpec((MOE_ROW_TILE, half), row_map),
        scratch_shapes=[pltpu.VMEM((MOE_ROW_TILE, d), F32)],
    )
    return pl.pallas_call(
        _expert_ffn_body,
        out_shape=jax.ShapeDtypeStruct((rows, half), jnp.int32),
        grid_spec=grid_spec,
        compiler_params=_params("arbitrary", "arbitrary"),
        name="expert_ffn",
    )(tile_expert, n_used, tile_valid, h_sorted, w_gate, w_up, w_down)


def _combine_body(x_ref, y_ref, g_ref, o_ref):
    out = x_ref[...]
    for s in range(2):
        column = jnp.broadcast_to(g_ref[s:s + 1, :], (LANES, g_ref.shape[1])).T
        gate = jnp.concatenate([column] * (out.shape[1] // LANES), axis=1)
        out = out + _unpack_bf16_pairs(y_ref[s]).astype(F32) * gate
    o_ref[...] = out


def _combine_residual(x, y_pairs, gates2):
    t, d = x.shape
    return pl.pallas_call(
        _combine_body,
        out_shape=jax.ShapeDtypeStruct((t, d), F32),
        grid=(t // ROW_TILE,),
        in_specs=[
            pl.BlockSpec((ROW_TILE, d), lambda i: (i, 0)),
            pl.BlockSpec((2, ROW_TILE, d // 2), lambda i: (0, i, 0)),
            pl.BlockSpec((2, ROW_TILE), lambda i: (0, i)),
        ],
        out_specs=pl.BlockSpec((ROW_TILE, d), lambda i: (i, 0)),
        compiler_params=_params("parallel"),
        name="moe_combine",
    )(x, y_pairs, gates2)


def _mix_moe_residual(x, o, w_o, mix_layer, gain, router, w_gate, w_up, w_down, layer):
    t, d = x.shape
    gates, rank, h_packed, x = _mix_router(x, o, w_o, mix_layer, gain, router)
    rank8 = rank.astype(jnp.int32)
    chosen = rank8 >= 0
    counts = jnp.sum(chosen, axis=1, dtype=jnp.int32)
    padded = (counts + MOE_ROW_TILE - 1) // MOE_ROW_TILE * MOE_ROW_TILE
    ends = jnp.cumsum(padded)
    starts = ends - padded
    pos = starts[:, None] + rank8
    max_rows = 2 * t + N_EXPERTS * MOE_ROW_TILE
    pos_lo = jnp.min(jnp.where(chosen, pos, max_rows), axis=0)
    pos_hi = jnp.max(jnp.where(chosen, pos, -1), axis=0)
    gates2 = jnp.stack([jnp.sum(jnp.where(chosen & (pos == pos_lo[None, :]), gates, 0.0), axis=0),
                        jnp.sum(jnp.where(chosen & (pos == pos_hi[None, :]), gates, 0.0), axis=0)])
    n_tiles = max_rows // MOE_ROW_TILE
    n_used = (ends[-1] // MOE_ROW_TILE).astype(jnp.int32)
    tile_start = jnp.minimum(jnp.arange(n_tiles, dtype=jnp.int32), n_used - 1) * MOE_ROW_TILE
    tile_expert = jnp.sum(tile_start[:, None] >= ends[None, :], axis=1, dtype=jnp.int32)
    tile_valid = jnp.clip((starts + counts)[tile_expert] - tile_start, 0, MOE_ROW_TILE).astype(jnp.int32)

    h_sorted = _sc_row_scatter_pair(h_packed, pos_lo, pos_hi, max_rows)
    y_sorted = _expert_ffn(h_sorted, tile_expert, n_used.reshape(1), tile_valid, w_gate, w_up, w_down, layer)
    y_pairs = _sc_row_gather(y_sorted, jnp.concatenate([pos_lo, pos_hi])).reshape(2, t, d // 2)
    return _combine_residual(x, y_pairs, gates2)


def _split_head_pair(q):
    is_first = lax.broadcasted_iota(jnp.int32, (1, LANES), 1) < HEAD_DIM
    zero = jnp.zeros_like(q)
    return jnp.where(is_first, q, zero), jnp.where(is_first, zero, q)


def _qk(q, k):
    return lax.dot_general(q, k, (((1,), (1,)), ((), ())), preferred_element_type=F32)


def _transpose_values(v_ref, vt_scr):
    for c in range(v_ref.shape[0] // ATT_BLOCK):
        rows = slice(c * ATT_BLOCK, (c + 1) * ATT_BLOCK)
        vt_scr[:, rows] = v_ref[rows, :].astype(F32).T.astype(vt_scr.dtype)


def _sb_body(q_ref, k_ref, v_ref, u_ref, o_ref, vt_scr, acc_scr, carry_scr, z_scr, keep_scr, sum_scr):
    qi = pl.program_id(2)

    @pl.when(qi == 0)
    def _():
        _transpose_values(v_ref, vt_scr)

    upper = u_ref[...]
    acc_scr[...] = jnp.zeros_like(acc_scr)
    carry_scr[...] = jnp.zeros_like(carry_scr)
    key = lax.broadcasted_iota(jnp.int32, (ATT_BLOCK, ATT_BLOCK), 0)
    query = lax.broadcasted_iota(jnp.int32, (ATT_BLOCK, ATT_BLOCK), 1)
    strict = key < query
    qs = [_split_head_pair(q_ref[j * ATT_BLOCK:(j + 1) * ATT_BLOCK, :]) for j in range(SB_SUB_BLOCKS)]

    def process(j, kb, keep):
        start = pl.multiple_of(kb * ATT_BLOCK, ATT_BLOCK)
        k = k_ref[pl.ds(start, ATT_BLOCK), :]
        for hh in range(2):
            z = _qk(k, qs[j][hh])
            log_beta = z - _softplus(z)
            log_keep = log_beta - z
            if keep is not None:
                log_keep = jnp.where(keep, log_keep, 0.0)
            remain = jnp.dot(upper, log_keep.astype(BF16), preferred_element_type=F32)
            w = jnp.exp(log_beta + remain + carry_scr[j, hh])
            if keep is not None:
                w = jnp.where(keep, w, 0.0)
            carry_scr[j, hh] += jnp.sum(log_keep, axis=0, keepdims=True)
            vt = vt_scr[pl.ds(hh * HEAD_DIM, HEAD_DIM), pl.ds(start, ATT_BLOCK)]
            acc_scr[j, hh] += jnp.dot(vt, w.astype(BF16), preferred_element_type=F32)

    units = []
    for j in range(SB_SUB_BLOCKS):
        g = qi * SB_SUB_BLOCKS + j
        has_previous = None if j > 0 else jnp.broadcast_to(g > 0, strict.shape)
        for kb, keep in ((g, strict), (jnp.maximum(g - 1, 0), has_previous)):
            for hh in range(2):
                units.append((j, hh, pl.multiple_of(kb * ATT_BLOCK, ATT_BLOCK), keep))

    for u, (j, hh, start, keep) in enumerate(units):
        z_scr[u] = _qk(k_ref[pl.ds(start, ATT_BLOCK), :], qs[j][hh])

    for u, (j, hh, start, keep) in enumerate(units):
        z = z_scr[u]
        log_beta = z - _softplus(z)
        log_keep = log_beta - z
        if keep is not None:
            log_keep = jnp.where(keep, log_keep, 0.0)
        z_scr[u] = log_beta
        keep_scr[u] = log_keep.astype(BF16)
        sum_scr[u] = jnp.sum(log_keep, axis=0, keepdims=True)

    for u in range(len(units)):
        z_scr[u] += jnp.dot(upper, keep_scr[u], preferred_element_type=F32)

    for u, (j, hh, start, keep) in enumerate(units):
        diagonal = u % 4 < 2
        log_w = z_scr[u] if diagonal else z_scr[u] + sum_scr[u - 2]
        w = jnp.exp(log_w)
        if keep is not None:
            w = jnp.where(keep, w, 0.0)
        keep_scr[u] = w.astype(BF16)

    for u, (j, hh, start, keep) in enumerate(units):
        vt = vt_scr[pl.ds(hh * HEAD_DIM, HEAD_DIM), pl.ds(start, ATT_BLOCK)]
        acc_scr[j, hh] += jnp.dot(vt, keep_scr[u], preferred_element_type=F32)
        carry_scr[j, hh] += sum_scr[u]

    def sweep_earlier():
        for j in range(SB_SUB_BLOCKS):
            def more(kb, j=j):
                return (kb >= 0) & (jnp.max(carry_scr[j]) > SB_UNDERFLOW_LOG)

            def step(kb, j=j):
                process(j, kb, None)
                return kb - 1

            lax.while_loop(more, step, qi * SB_SUB_BLOCKS + j - 2)

    pl.when(jnp.max(carry_scr[...]) > SB_UNDERFLOW_LOG)(sweep_earlier)

    for j in range(SB_SUB_BLOCKS):
        out_t = jnp.concatenate([acc_scr[j, 0], acc_scr[j, 1]], axis=0)
        o_ref[j * ATT_BLOCK:(j + 1) * ATT_BLOCK, :] = out_t.T.astype(o_ref.dtype)


def _sb_attention(proj, batch, seq):
    t = proj.shape[0]
    nq = seq // SB_Q_BLOCK
    idx = jnp.arange(ATT_BLOCK)
    upper = (idx[None, :] > idx[:, None]).astype(BF16)
    return pl.pallas_call(
        _sb_body,
        out_shape=jax.ShapeDtypeStruct((t, D_MODEL), BF16),
        grid=(batch, HEAD_PAIRS, nq),
        in_specs=[
            pl.BlockSpec((SB_Q_BLOCK, LANES), lambda b, p, i: (b * nq + i, p)),
            pl.BlockSpec((seq, LANES), lambda b, p, i: (b, HEAD_PAIRS + p)),
            pl.BlockSpec((seq, LANES), lambda b, p, i: (b, 2 * HEAD_PAIRS + p)),
            pl.BlockSpec((ATT_BLOCK, ATT_BLOCK), lambda b, p, i: (0, 0)),
        ],
        out_specs=pl.BlockSpec((SB_Q_BLOCK, LANES), lambda b, p, i: (b * nq + i, p)),
        scratch_shapes=[
            pltpu.VMEM((LANES, seq), BF16),
            pltpu.VMEM((SB_SUB_BLOCKS, 2, HEAD_DIM, ATT_BLOCK), F32),
            pltpu.VMEM((SB_SUB_BLOCKS, 2, 1, ATT_BLOCK), F32),
            pltpu.VMEM((4 * SB_SUB_BLOCKS, ATT_BLOCK, ATT_BLOCK), F32),
            pltpu.VMEM((4 * SB_SUB_BLOCKS, ATT_BLOCK, ATT_BLOCK), BF16),
            pltpu.VMEM((4 * SB_SUB_BLOCKS, 1, ATT_BLOCK), F32),
        ],
        compiler_params=_params("parallel", "parallel", "arbitrary"),
        name="sb_attention",
    )(proj, proj, proj, upper)


def _fox_gate_body(x_ref, g_ref, w_ref, b_ref, tri_ref, c_ref):
    h = _rms_normalize(x_ref[...], g_ref[...])
    logits = _dot_split(h, w_ref[...])
    log_f = -_softplus(-(logits + b_ref[...]))
    seq = log_f.shape[0]
    carry = jnp.zeros((1, LANES), F32)
    for blk in range(seq // ATT_BLOCK):
        rows = slice(blk * ATT_BLOCK, (blk + 1) * ATT_BLOCK)
        c = carry
        for part in _split_bf16(log_f[rows], 3):
            c = c + jnp.dot(tri_ref[...], part, preferred_element_type=F32)
        c_ref[rows, :] = c
        carry = c[ATT_BLOCK - 1:ATT_BLOCK, :]


def _fox_cum_log_forget(x, gain, w_gate, b_gate, batch, seq):
    t, d = x.shape
    w = jnp.zeros((d, LANES), F32).at[:, :N_HEADS].set(w_gate)
    b = jnp.zeros((1, LANES), F32).at[0, :N_HEADS].set(b_gate)
    idx = jnp.arange(ATT_BLOCK)
    tri = (idx[:, None] >= idx[None, :]).astype(BF16)
    return pl.pallas_call(
        _fox_gate_body,
        out_shape=jax.ShapeDtypeStruct((t, LANES), F32),
        grid=(batch,),
        in_specs=[
            pl.BlockSpec((seq, d), lambda i: (i, 0)),
            pl.BlockSpec((1, d), lambda i: (0, 0)),
            pl.BlockSpec((d, LANES), lambda i: (0, 0)),
            pl.BlockSpec((1, LANES), lambda i: (0, 0)),
            pl.BlockSpec((ATT_BLOCK, ATT_BLOCK), lambda i: (0, 0)),
        ],
        out_specs=pl.BlockSpec((seq, LANES), lambda i: (i, 0)),
        compiler_params=_params("parallel"),
        name="fox_gate",
    )(x, gain.reshape(1, d), w, b, tri)


def _fox_body(q_ref, k_ref, v_ref, cq_ref, ck_ref, o_ref, vt_scr, ckb_scr, a_scr, p_scr):
    _transpose_values(v_ref, vt_scr)
    for hh in range(2):
        for c in range(ckb_scr.shape[1] // ATT_BLOCK):
            rows = slice(c * ATT_BLOCK, (c + 1) * ATT_BLOCK)
            ckb_scr[hh, rows, :] = jnp.broadcast_to(ck_ref[0, hh, :, rows], (LANES, ATT_BLOCK)).T

    key = lax.broadcasted_iota(jnp.int32, (ATT_BLOCK, ATT_Q_BLOCK), 0)
    query = lax.broadcasted_iota(jnp.int32, (ATT_BLOCK, ATT_Q_BLOCK), 1)
    row_shape = (1, ATT_Q_BLOCK)

    def sweep(q, first_slot):
        n_blocks = (q + 1) * ATT_K_PER_Q
        q_rows = slice(q * ATT_Q_BLOCK, (q + 1) * ATT_Q_BLOCK)
        qs = _split_head_pair(q_ref[q_rows, :])
        tops = [jnp.full(row_shape, NEG_INF, F32)] * 2
        for kb in range(n_blocks):
            k_rows = slice(kb * ATT_BLOCK, (kb + 1) * ATT_BLOCK)
            k = k_ref[k_rows, :]
            for hh in range(2):
                a = _qk(k, qs[hh]) - jnp.concatenate([ckb_scr[hh, k_rows, :]] * (ATT_Q_BLOCK // LANES), axis=1)
                if kb >= n_blocks - ATT_K_PER_Q:
                    a = jnp.where(key + (kb * ATT_BLOCK - q * ATT_Q_BLOCK) <= query, a, NEG_INF)
                a_scr[hh, first_slot + kb] = a
                tops[hh] = jnp.maximum(tops[hh], jnp.max(a, axis=0, keepdims=True))
        shifts = [cq_ref[0, hh, :, q_rows] - (tops[hh] + cq_ref[0, hh, :, q_rows]) for hh in range(2)]
        sums = [jnp.zeros(row_shape, F32)] * 2
        for kb in range(n_blocks):
            for hh in range(2):
                p = jnp.exp(a_scr[hh, first_slot + kb] + shifts[hh])
                p_scr[hh, first_slot + kb] = p.astype(BF16)
                sums[hh] = sums[hh] + jnp.sum(p, axis=0, keepdims=True)
        accs = [jnp.zeros((HEAD_DIM, ATT_Q_BLOCK), F32)] * 2
        for kb in range(n_blocks):
            for hh in range(2):
                vt = vt_scr[hh * HEAD_DIM:(hh + 1) * HEAD_DIM, kb * ATT_BLOCK:(kb + 1) * ATT_BLOCK]
                accs[hh] = accs[hh] + jnp.dot(vt, p_scr[hh, first_slot + kb], preferred_element_type=F32)
        out_t = jnp.concatenate([accs[0] / sums[0], accs[1] / sums[1]], axis=0)
        o_ref[q_rows, :] = out_t.T.astype(o_ref.dtype)

    first_slot = 0
    for q in range(k_ref.shape[0] // ATT_Q_BLOCK):
        sweep(q, first_slot)
        first_slot += (q + 1) * ATT_K_PER_Q


def _fox_attention(proj, cum, batch, seq):
    t = proj.shape[0]
    nq = seq // ATT_Q_BLOCK
    cum_h = cum[:, :N_HEADS].reshape(batch, seq, N_HEADS).transpose(0, 2, 1)
    cum_rows = cum_h.reshape(batch, N_HEADS, 1, seq)
    n_slots = sum((q + 1) * ATT_K_PER_Q for q in range(nq))
    return pl.pallas_call(
        _fox_body,
        out_shape=jax.ShapeDtypeStruct((t, D_MODEL), BF16),
        grid=(batch, HEAD_PAIRS),
        in_specs=[
            pl.BlockSpec((seq, LANES), lambda b, p: (b, p)),
            pl.BlockSpec((seq, LANES), lambda b, p: (b, HEAD_PAIRS + p)),
            pl.BlockSpec((seq, LANES), lambda b, p: (b, 2 * HEAD_PAIRS + p)),
            pl.BlockSpec((1, 2, 1, seq), lambda b, p: (b, p, 0, 0)),
            pl.BlockSpec((1, 2, 1, seq), lambda b, p: (b, p, 0, 0)),
        ],
        out_specs=pl.BlockSpec((seq, LANES), lambda b, p: (b, p)),
        scratch_shapes=[
            pltpu.VMEM((LANES, seq), BF16),
            pltpu.VMEM((2, seq, LANES), F32),
            pltpu.VMEM((2, n_slots, ATT_BLOCK, ATT_Q_BLOCK), F32),
            pltpu.VMEM((2, n_slots, ATT_BLOCK, ATT_Q_BLOCK), BF16),
        ],
        compiler_params=_params("parallel", "parallel"),
        name="fox_attention",
    )(proj, proj, proj, cum_rows, cum_rows)


def _band_body(q_ref, k_ref, v_ref, bias_ref, o_ref, s_ref, vt_scr, logit_scr, p_scr, inv_scr, *, sub_len):
    seq = v_ref.shape[0]
    n_blocks = seq // DIL_BLOCK
    _transpose_values(v_ref, vt_scr)

    def key_rows(n):
        first = (n * DIL_BLOCK) % sub_len == 0
        return first, slice((n if first else n - 1) * DIL_BLOCK, (n + 1) * DIL_BLOCK)

    for n in range(n_blocks):
        first, k_rows = key_rows(n)
        qs = _split_head_pair(q_ref[n * DIL_BLOCK:(n + 1) * DIL_BLOCK, :])
        k = k_ref[k_rows, :]
        for hh in range(2):
            bias = bias_ref[hh, DIL_BLOCK:, :] if first else bias_ref[hh]
            logit_scr[2 * n + hh, :k.shape[0], :] = _qk(k, qs[hh]) + bias

    for n in range(n_blocks):
        first, k_rows = key_rows(n)
        n_keys = k_rows.stop - k_rows.start
        for hh in range(2):
            logits = logit_scr[2 * n + hh, :n_keys, :]
            m = jnp.max(logits, axis=0, keepdims=True)
            p = jnp.exp(logits - m)
            l = jnp.sum(p, axis=0, keepdims=True)
            p_scr[2 * n + hh, :n_keys, :] = p.astype(BF16)
            inv_scr[2 * n + hh] = 1.0 / l
            s_ref[0, 0, hh:hh + 1, n * DIL_BLOCK:(n + 1) * DIL_BLOCK] = m + jnp.log(l)

    for n in range(n_blocks):
        first, k_rows = key_rows(n)
        n_keys = k_rows.stop - k_rows.start
        outs = []
        for hh in range(2):
            vt = vt_scr[hh * HEAD_DIM:(hh + 1) * HEAD_DIM, k_rows]
            o = jnp.dot(vt, p_scr[2 * n + hh, :n_keys, :], preferred_element_type=F32)
            outs.append(o * inv_scr[2 * n + hh])
        o_ref[n * DIL_BLOCK:(n + 1) * DIL_BLOCK, :] = jnp.concatenate(outs, axis=0).T.astype(o_ref.dtype)


def _band_groups_body(*refs, sub_lens):
    n = len(sub_lens)
    ins, outs, scratch = refs[:4 * n], refs[4 * n:6 * n], refs[6 * n:]
    for g, sub_len in enumerate(sub_lens):
        _band_body(*ins[4 * g:4 * g + 4], *outs[2 * g:2 * g + 2], *scratch[4 * g:4 * g + 4], sub_len=sub_len)


def _band_attention(qk, v, bias_t, batch, seq, sub_lens):
    t = v[0].shape[0]
    n = len(sub_lens)
    in_specs, operands, out_shape, out_specs, scratch = [], [], [], [], []
    for g in range(n):
        in_specs += [
            pl.BlockSpec((None, seq, LANES), lambda b, p: (0, b, p)),
            pl.BlockSpec((None, seq, LANES), lambda b, p: (1, b, p)),
            pl.BlockSpec((seq, LANES), lambda b, p: (b, p)),
            pl.BlockSpec((None, 2, 2 * DIL_BLOCK, DIL_BLOCK), lambda b, p, g=g: (g, p, 0, 0)),
        ]
        operands += [qk[g], qk[g], v[g], bias_t]
        out_shape += [jax.ShapeDtypeStruct((t, D_MODEL), BF16),
                      jax.ShapeDtypeStruct((batch, HEAD_PAIRS, 2, seq), F32)]
        out_specs += [pl.BlockSpec((seq, LANES), lambda b, p: (b, p)),
                      pl.BlockSpec((1, 1, 2, seq), lambda b, p: (b, p, 0, 0))]
        scratch += [pltpu.VMEM((LANES, seq), BF16),
                    pltpu.VMEM((2 * seq // DIL_BLOCK, 2 * DIL_BLOCK, DIL_BLOCK), F32),
                    pltpu.VMEM((2 * seq // DIL_BLOCK, 2 * DIL_BLOCK, DIL_BLOCK), BF16),
                    pltpu.VMEM((2 * seq // DIL_BLOCK, 1, DIL_BLOCK), F32)]
    results = pl.pallas_call(
        functools.partial(_band_groups_body, sub_lens=tuple(sub_lens)),
        out_shape=tuple(out_shape),
        grid=(batch, HEAD_PAIRS),
        in_specs=in_specs,
        out_specs=tuple(out_specs),
        scratch_shapes=scratch,
        compiler_params=_params("parallel", "parallel"),
        name="band_attention",
    )(*operands)
    return [(results[2 * g], results[2 * g + 1]) for g in range(n)]


def _softmax_merge_body(o1_ref, o2_ref, o3_ref, s_ref, e_ref, o_ref):
    s = s_ref[...]
    groups = [s, pltpu.roll(s, LANES - N_HEADS, axis=1), pltpu.roll(s, LANES - 2 * N_HEADS, axis=1)]
    top = jnp.maximum(jnp.maximum(groups[0], groups[1]), groups[2])
    weights = [jnp.exp(g - top) for g in groups]
    inv = 1.0 / (weights[0] + weights[1] + weights[2])
    out = jnp.zeros(o_ref.shape, F32)
    for w, part in zip(weights, (o1_ref, o2_ref, o3_ref)):
        spread = jnp.dot((w * inv).astype(BF16), e_ref[...], preferred_element_type=F32)
        out = out + spread * part[...].astype(F32)
    o_ref[...] = out.astype(o_ref.dtype)


def _softmax_merge(outs, s_all):
    t, d = outs[0].shape
    head_of = jnp.arange(d) // HEAD_DIM
    expand = (jnp.arange(LANES)[:, None] == head_of[None, :]).astype(BF16)
    rows = pl.BlockSpec((ROW_TILE, d), lambda i: (i, 0))
    return pl.pallas_call(
        _softmax_merge_body,
        out_shape=jax.ShapeDtypeStruct((t, d), BF16),
        grid=(t // ROW_TILE,),
        in_specs=[rows, rows, rows,
                  pl.BlockSpec((ROW_TILE, LANES), lambda i: (i, 0)),
                  pl.BlockSpec((LANES, d), lambda i: (0, 0))],
        out_specs=rows,
        compiler_params=_params("parallel"),
        name="softmax_merge",
    )(*outs, s_all, expand)


def _t5_causal_bucket(distance):
    max_exact = N_REL_BUCKETS // 2
    d = jnp.maximum(distance, 1).astype(F32)
    log_b = max_exact + (jnp.log(d / max_exact) / math.log(REL_MAX_DISTANCE / max_exact)
                         * (N_REL_BUCKETS - max_exact)).astype(jnp.int32)
    log_b = jnp.minimum(log_b, N_REL_BUCKETS - 1)
    return jnp.where(distance < max_exact, distance, log_b)


def _dilated_bias(rel_bias):
    kj = jnp.arange(2 * DIL_BLOCK, dtype=jnp.int32)
    qi = jnp.arange(DIL_BLOCK, dtype=jnp.int32)
    delta = qi[None, :] + DIL_BLOCK - kj[:, None]
    in_band = (delta >= 0) & (delta <= DIL_SPAN)
    buckets = jnp.stack([_t5_causal_bucket(jnp.maximum(delta, 0) * dil) for _, dil in DILATED_PAIRS])
    one_hot = (buckets[..., None] == jnp.arange(N_REL_BUCKETS)).astype(F32)
    bias = jnp.einsum("gkqb,bh->ghkq", one_hot, rel_bias.astype(F32), precision=lax.Precision.HIGHEST)
    return jnp.where(in_band[None, None], bias, NEG_INF)


def _dilated_proj_body(x_ref, g_ref, w_ref, cs_ref, bd_ref, *rest):
    n_groups = len(DILATED_PAIRS)
    qk_refs, v_refs = rest[:n_groups], rest[n_groups:2 * n_groups]
    h_scr, res_scr = rest[2 * n_groups:]
    j = pl.program_id(1)

    @pl.when(j == 0)
    def _():
        h_scr[...] = _rms_normalize(x_ref[...], g_ref[...]).astype(BF16)

    acc = jnp.dot(h_scr[...], w_ref[...], preferred_element_type=F32)

    def keep(res, first_lane):
        for c in range(res.shape[1] // LANES):
            res_scr[first_lane // LANES + c] = res[:, c * LANES:(c + 1) * LANES]

    @pl.when(j < 2 * n_groups)
    def _():
        for c in range(COL_TILE // NORM_CHUNK):
            sl = slice(c * NORM_CHUNK, (c + 1) * NORM_CHUNK)
            a = acc[:, sl]
            ss = jnp.dot((a * a).astype(BF16), bd_ref[...], preferred_element_type=F32)
            keep(a * lax.rsqrt(ss * (1.0 / HEAD_DIM) + RMS_EPS) * cs_ref[:, sl], c * NORM_CHUNK)

    @pl.when(j == 2 * n_groups)
    def _():
        keep(acc * cs_ref[...], 0)

    def write_classes(ref, dil):
        rows = res_scr.shape[1] // dil
        for c in range(res_scr.shape[0]):
            lanes = slice(c * LANES, (c + 1) * LANES)
            if dil == 1:
                ref[:, lanes] = res_scr[c].astype(ref.dtype)
            else:
                for r in range(dil):
                    ref[r, :, lanes] = res_scr[c, pl.ds(r, rows, stride=dil), :].astype(ref.dtype)

    for g, (_, dil) in enumerate(DILATED_PAIRS):
        pl.when(j // 2 == g)(functools.partial(write_classes, qk_refs[g], dil))
        pl.when(j == 2 * n_groups)(functools.partial(write_classes, v_refs[g], dil))


def _dilated_proj(x, gain, w, layer, col_scale, batch, seq):
    t, d = x.shape
    n = w.shape[2]
    n_groups = len(DILATED_PAIRS)
    tiles_per_seq = seq // DIL_ROW_TILE
    head_id = jnp.arange(NORM_CHUNK) // HEAD_DIM
    block_diag = (head_id[:, None] == head_id[None, :]).astype(BF16)
    out_shape, out_specs = [], []
    for kind in ("qk", "v"):
        for g, (_, dil) in enumerate(DILATED_PAIRS):
            rows = DIL_ROW_TILE // dil
            if kind == "qk":
                shape = (2, batch, dil, seq // dil, d)
                block = (None, None, dil, rows, d)
                index = lambda i, j, g=g: (jnp.clip(j - 2 * g, 0, 1), i // tiles_per_seq, 0, i % tiles_per_seq, 0)
            else:
                shape = (batch, dil, seq // dil, d)
                block = (None, dil, rows, d)
                index = lambda i, j: (i // tiles_per_seq, 0, i % tiles_per_seq, 0)
            if dil == 1:
                block = block[:-3] + (None,) + block[-2:]
            out_shape.append(jax.ShapeDtypeStruct(shape, BF16))
            out_specs.append(pl.BlockSpec(block, index))
    outs = pl.pallas_call(
        _dilated_proj_body,
        out_shape=tuple(out_shape),
        grid=(t // DIL_ROW_TILE, n // COL_TILE),
        in_specs=[
            pl.BlockSpec((DIL_ROW_TILE, d), lambda i, j: (i, 0)),
            pl.BlockSpec((1, d), lambda i, j: (0, 0)),
            pl.BlockSpec((None, d, COL_TILE), lambda i, j: (layer, 0, j)),
            pl.BlockSpec((1, COL_TILE), lambda i, j: (0, j)),
            pl.BlockSpec((NORM_CHUNK, NORM_CHUNK), lambda i, j: (0, 0)),
        ],
        out_specs=tuple(out_specs),
        scratch_shapes=[pltpu.VMEM((DIL_ROW_TILE, d), BF16),
                        pltpu.VMEM((COL_TILE // LANES, DIL_ROW_TILE, LANES), F32)],
        compiler_params=_params("parallel", "arbitrary"),
        name="dilated_proj",
    )(x, gain.reshape(1, d), w, col_scale.reshape(1, n).astype(F32), block_diag)
    qk = [o.reshape(2, t, d) for o in outs[:n_groups]]
    v = [o.reshape(t, d) for o in outs[n_groups:]]
    return qk, v


def _dilated_attention(qk, v, rel_bias, batch, seq):
    t = v[0].shape[0]
    n_groups = len(DILATED_PAIRS)
    bias_t = _dilated_bias(rel_bias)
    outs, stats = [], []
    groups = _band_attention(qk, v, bias_t, batch, seq, [seq // dil for _, dil in DILATED_PAIRS])
    for (o, s), (_, dil) in zip(groups, DILATED_PAIRS):
        sub_len = seq // dil
        o = o.reshape(batch, dil, sub_len, D_MODEL).transpose(0, 2, 1, 3).reshape(t, D_MODEL)
        s = s.reshape(batch, HEAD_PAIRS, 2, dil, sub_len).swapaxes(3, 4)
        outs.append(o)
        stats.append(s.reshape(batch, N_HEADS, seq).transpose(0, 2, 1).reshape(t, N_HEADS))
    stats.append(jnp.zeros((t, LANES - n_groups * N_HEADS), F32))
    return _softmax_merge(outs, jnp.concatenate(stats, axis=1))


def _tile_heads(v):
    return jnp.tile(v.astype(F32), N_HEADS)


def _sb_mixer(x, gain, w_qkv, layer, batch, seq):
    ones = jnp.ones((D_MODEL,), F32)
    col_scale = jnp.concatenate([ones * QK_SCALE, ones, ones])
    proj = _norm_proj(x, gain, w_qkv, layer, col_scale, 0)
    return _sb_attention(proj, batch, seq)


def _dilated_mixer(x, gain, w_in, q_norm, k_norm, rel_bias, layer, batch, seq):
    scales = []
    for g in range(len(DILATED_PAIRS)):
        scales += [_tile_heads(q_norm[g]) * QK_SCALE, _tile_heads(k_norm[g])]
    scales.append(jnp.ones((D_MODEL,), F32))
    qk, v = _dilated_proj(x, gain, w_in, layer, jnp.concatenate(scales), batch, seq)
    return _dilated_attention(qk, v, rel_bias, batch, seq)


def _fox_mixer(x, gain, w_qkv, w_gate, b_f, q_norm, k_norm, layer, batch, seq):
    col_scale = jnp.concatenate([_tile_heads(q_norm) * QK_SCALE, _tile_heads(k_norm),
                                 jnp.ones((D_MODEL,), F32)])
    proj = _norm_proj(x, gain, w_qkv, layer, col_scale, 2)
    cum = _fox_cum_log_forget(x, gain, w_gate, b_f, batch, seq)
    return _fox_attention(proj, cum, batch, seq)


def kernel(x, sb_w_qkv, sb_w_o, dil_w_in, dil_q_norm, dil_k_norm, dil_w_o, fox_w_in, fox_b_f,
           fox_q_norm, fox_k_norm, fox_w_o, rel_bias, attn_norm, ffn_norm, mlp_w_gate, mlp_w_up,
           mlp_w_down, moe_router, moe_w_gate, moe_w_up, moe_w_down):
    batch, seq, d = x.shape
    depth = attn_norm.shape[0]
    assert d == D_MODEL and seq % max(ROW_TILE, DIL_ROW_TILE, ATT_Q_BLOCK, SB_Q_BLOCK) == 0
    assert all((seq // dil) % DIL_BLOCK == 0 for _, dil in DILATED_PAIRS)
    sb_w_qkv, sb_w_o, dil_w_in, dil_w_o, fox_w_o, mlp_w_gate, mlp_w_up, mlp_w_down = (
        w.astype(BF16) for w in (sb_w_qkv, sb_w_o, dil_w_in, dil_w_o, fox_w_o, mlp_w_gate, mlp_w_up,
                                 mlp_w_down))
    moe_w_gate, moe_w_up, moe_w_down = (w.astype(BF16) for w in (moe_w_gate, moe_w_up, moe_w_down))
    fox_w_qkv = fox_w_in[:, :, :3 * D_MODEL].astype(BF16)
    h = x.reshape(batch * seq, d)
    for i in range(depth):
        kind, j = i % 3, i // 3
        if kind == 0:
            mixed, w_o = _sb_mixer(h, attn_norm[i], sb_w_qkv, j, batch, seq), sb_w_o
        elif kind == 1:
            mixed, w_o = _dilated_mixer(h, attn_norm[i], dil_w_in, dil_q_norm[j], dil_k_norm[j], rel_bias,
                                        j, batch, seq), dil_w_o
        else:
            mixed, w_o = _fox_mixer(h, attn_norm[i], fox_w_qkv, fox_w_in[j, :, 3 * D_MODEL:], fox_b_f[j],
                                    fox_q_norm[j], fox_k_norm[j], j, batch, seq), fox_w_o
        f = i // 2
        if i % 2 == 0:
            h = _mix_ffn_residual(h, mixed, w_o, j, ffn_norm[i], mlp_w_gate, mlp_w_up, mlp_w_down, f)
        else:
            h = _mix_moe_residual(h, mixed, w_o, j, ffn_norm[i], moe_router[f], moe_w_gate, moe_w_up,
                                  moe_w_down, f)
    return h.reshape(batch, seq, d)
```

```python
import functools
import math

import jax
import jax.numpy as jnp
from jax import lax
from jax.experimental import pallas as pl
from jax.experimental.pallas import tpu as pltpu
from jax.experimental.pallas import tpu_sc as plsc

D_MODEL = 1024
N_HEADS = 16
HEAD_DIM = 64
LANES = 128
HEAD_PAIRS = D_MODEL // LANES
N_EXPERTS = 8
N_REL_BUCKETS = 32
REL_MAX_DISTANCE = 2048
DILATED_PAIRS = ((128, 1), (512, 4), (2048, 16))
DIL_SPAN = 128
RMS_EPS = 1e-6
NEG_INF = -1e30
SB_UNDERFLOW_LOG = -104.0
QK_SCALE = 1.0 / math.sqrt(HEAD_DIM)

ROW_TILE = 1024
COL_TILE = 1024
FF_TILE = 1792
SWIGLU_CHUNK = 256
MOE_ROW_TILE = 512
MOE_FF_TILE = 1792
SC_CORES = 2
SC_SUBCORES = 16
SC_CHUNK_ROWS = 64
NORM_CHUNK = 256
ATT_BLOCK = 256
ATT_Q_BLOCK = 512
ATT_K_PER_Q = ATT_Q_BLOCK // ATT_BLOCK
SB_Q_BLOCK = 2048
SB_SUB_BLOCKS = SB_Q_BLOCK // ATT_BLOCK
DIL_BLOCK = 128
DIL_ROW_TILE = 1024
VMEM_LIMIT = 56 * 1024 * 1024

F32 = jnp.float32
BF16 = jnp.bfloat16


def _params(*semantics):
    return pltpu.CompilerParams(dimension_semantics=semantics, vmem_limit_bytes=VMEM_LIMIT)


def _rms_normalize(x, gain):
    inv = lax.rsqrt(jnp.mean(x * x, axis=-1, keepdims=True) + RMS_EPS)
    return x * inv * gain


def _split_bf16(x, terms):
    parts = []
    for _ in range(terms):
        part = x.astype(BF16)
        parts.append(part)
        x = x - part.astype(F32)
    return parts


def _dot_split(a, b):
    a_hi, a_lo = _split_bf16(a, 2)
    b_hi, b_lo = _split_bf16(b, 2)
    return (jnp.dot(a_hi, b_hi, preferred_element_type=F32)
            + (jnp.dot(a_hi, b_lo, preferred_element_type=F32)
               + jnp.dot(a_lo, b_hi, preferred_element_type=F32)))


def _softplus(z):
    return jnp.maximum(z, 0.0) + jnp.log(1.0 + jnp.exp(-jnp.abs(z)))


def _norm_proj_body(x_ref, g_ref, w_ref, cs_ref, bd_ref, o_ref, h_scr, *, n_norm):
    j = pl.program_id(1)

    @pl.when(j == 0)
    def _():
        h_scr[...] = _rms_normalize(x_ref[...], g_ref[...]).astype(BF16)

    acc = jnp.dot(h_scr[...], w_ref[...], preferred_element_type=F32)

    def plain():
        o_ref[...] = (acc * cs_ref[...]).astype(o_ref.dtype)

    def head_normed():
        for c in range(COL_TILE // NORM_CHUNK):
            sl = slice(c * NORM_CHUNK, (c + 1) * NORM_CHUNK)
            a = acc[:, sl]
            ss = jnp.dot((a * a).astype(BF16), bd_ref[...], preferred_element_type=F32)
            inv = lax.rsqrt(ss * (1.0 / HEAD_DIM) + RMS_EPS)
            o_ref[:, sl] = (a * inv * cs_ref[:, sl]).astype(o_ref.dtype)

    if n_norm == 0:
        plain()
    else:
        pl.when(j < n_norm)(head_normed)
        pl.when(j >= n_norm)(plain)


def _norm_proj(x, gain, w, layer, col_scale, n_norm):
    t, d = x.shape
    n = w.shape[2]
    head_id = jnp.arange(NORM_CHUNK) // HEAD_DIM
    block_diag = (head_id[:, None] == head_id[None, :]).astype(BF16)
    return pl.pallas_call(
        functools.partial(_norm_proj_body, n_norm=n_norm),
        out_shape=jax.ShapeDtypeStruct((t, n), BF16),
        grid=(t // ROW_TILE, n // COL_TILE),
        in_specs=[
            pl.BlockSpec((ROW_TILE, d), lambda i, j: (i, 0)),
            pl.BlockSpec((1, d), lambda i, j: (0, 0)),
            pl.BlockSpec((None, d, COL_TILE), lambda i, j: (layer, 0, j)),
            pl.BlockSpec((1, COL_TILE), lambda i, j: (0, j)),
            pl.BlockSpec((NORM_CHUNK, NORM_CHUNK), lambda i, j: (0, 0)),
        ],
        out_specs=pl.BlockSpec((ROW_TILE, COL_TILE), lambda i, j: (i, j)),
        scratch_shapes=[pltpu.VMEM((ROW_TILE, d), BF16)],
        compiler_params=_params("parallel", "arbitrary"),
        name="norm_proj",
    )(x, gain.reshape(1, d), w, col_scale.reshape(1, n).astype(F32), block_diag)


def _swiglu_hidden(h, wg, wu):
    g = jnp.dot(h, wg, preferred_element_type=F32)
    u = jnp.dot(h, wu, preferred_element_type=F32)
    return g * (1.0 / (1.0 + jnp.exp(-g))) * u


def _swiglu_down(h, wg_ref, wu_ref, wd_ref):
    y = None
    for c in range(wg_ref.shape[1] // SWIGLU_CHUNK):
        cols = slice(c * SWIGLU_CHUNK, (c + 1) * SWIGLU_CHUNK)
        a = _swiglu_hidden(h, wg_ref[:, cols], wu_ref[:, cols]).astype(BF16)
        part = jnp.dot(a, wd_ref[cols, :], preferred_element_type=F32)
        y = part if y is None else y + part
    return y


def _ffn_body(x_ref, o_ref, wo_ref, g_ref, wg_ref, wu_ref, wd_ref, y_ref, h_scr):
    f = pl.program_id(1)

    @pl.when(f == 0)
    def _():
        x = x_ref[...] + jnp.dot(o_ref[...], wo_ref[...], preferred_element_type=F32)
        h_scr[...] = _rms_normalize(x, g_ref[...]).astype(BF16)
        y_ref[...] = x

    y_ref[...] += _swiglu_down(h_scr[...], wg_ref, wu_ref, wd_ref)


def _mix_ffn_residual(x, o, w_o, mix_layer, gain, w_gate, w_up, w_down, layer):
    t, d = x.shape
    ff = w_gate.shape[2]
    return pl.pallas_call(
        _ffn_body,
        out_shape=jax.ShapeDtypeStruct((t, d), F32),
        grid=(t // ROW_TILE, ff // FF_TILE),
        in_specs=[
            pl.BlockSpec((ROW_TILE, d), lambda i, f: (i, 0)),
            pl.BlockSpec((ROW_TILE, d), lambda i, f: (i, 0)),
            pl.BlockSpec((None, d, d), lambda i, f: (mix_layer, 0, 0)),
            pl.BlockSpec((1, d), lambda i, f: (0, 0)),
            pl.BlockSpec((None, d, FF_TILE), lambda i, f: (layer, 0, f)),
            pl.BlockSpec((None, d, FF_TILE), lambda i, f: (layer, 0, f)),
            pl.BlockSpec((None, FF_TILE, d), lambda i, f: (layer, f, 0)),
        ],
        out_specs=pl.BlockSpec((ROW_TILE, d), lambda i, f: (i, 0)),
        scratch_shapes=[pltpu.VMEM((ROW_TILE, d), BF16)],
        compiler_params=_params("parallel", "arbitrary"),
        name="ffn",
    )(x, o, w_o, gain.reshape(1, d), w_gate, w_up, w_down)


def _pack_bf16_pairs(x):
    half = x.shape[1] // 2
    bits = pltpu.bitcast(x.astype(BF16).astype(F32), jnp.int32)
    return bits[:, :half] | lax.shift_right_logical(bits[:, half:], jnp.int32(16))


def _unpack_bf16_pairs(p):
    left = pltpu.bitcast(p & jnp.int32(-65536), F32)
    right = pltpu.bitcast(lax.shift_left(p, jnp.int32(16)), F32)
    return jnp.concatenate([left, right], axis=1).astype(BF16)


def _router_body(x_ref, o_ref, wo_ref, g_ref, r_ref, tri_ref, gates_ref, rank_ref, h_ref, x1_ref, count_scr):
    @pl.when(pl.program_id(0) == 0)
    def _():
        count_scr[...] = jnp.zeros_like(count_scr)

    x1 = x_ref[...] + jnp.dot(o_ref[...], wo_ref[...], preferred_element_type=F32)
    x1_ref[...] = x1
    h = _rms_normalize(x1, g_ref[...])
    h_ref[...] = _pack_bf16_pairs(h)
    logits = _dot_split(h, r_ref[...])
    lane = lax.broadcasted_iota(jnp.int32, logits.shape, 1).astype(F32)
    logits = jnp.where(lane < N_EXPERTS, logits, -jnp.inf)
    m1 = jnp.max(logits, axis=-1, keepdims=True)
    i1 = jnp.min(jnp.where(logits == m1, lane, float(LANES)), axis=-1, keepdims=True)
    rest = jnp.where(lane == i1, -jnp.inf, logits)
    m2 = jnp.max(rest, axis=-1, keepdims=True)
    i2 = jnp.min(jnp.where(rest == m2, lane, float(LANES)), axis=-1, keepdims=True)
    e = jnp.exp(m2 - m1)
    g1 = 1.0 / (1.0 + e)
    gates = jnp.where(lane == i1, g1, 0.0) + jnp.where(lane == i2, e * g1, 0.0)
    gates_ref[...] = gates.T[:N_EXPERTS, :]
    chosen = jnp.where((lane == i1) | (lane == i2), 1.0, 0.0)
    inclusive = jnp.dot(tri_ref[...], chosen.astype(BF16), preferred_element_type=F32)
    rank = jnp.where(chosen > 0.0, inclusive - 1.0 + count_scr[...], -1.0)
    rank_ref[...] = rank.T[:N_EXPERTS, :]
    count_scr[...] += inclusive[ROW_TILE - 1:ROW_TILE, :]


def _mix_router(x, o, w_o, mix_layer, gain, router):
    t, d = x.shape
    r = jnp.zeros((d, LANES), F32).at[:, :N_EXPERTS].set(router)
    idx = jnp.arange(ROW_TILE)
    tri = (idx[:, None] >= idx[None, :]).astype(BF16)
    rows = pl.BlockSpec((ROW_TILE, d), lambda i: (i, 0))
    return pl.pallas_call(
        _router_body,
        out_shape=(jax.ShapeDtypeStruct((N_EXPERTS, t), F32), jax.ShapeDtypeStruct((N_EXPERTS, t), F32),
                   jax.ShapeDtypeStruct((t, d // 2), jnp.int32), jax.ShapeDtypeStruct((t, d), F32)),
        grid=(t // ROW_TILE,),
        in_specs=[
            rows,
            rows,
            pl.BlockSpec((None, d, d), lambda i: (mix_layer, 0, 0)),
            pl.BlockSpec((1, d), lambda i: (0, 0)),
            pl.BlockSpec((d, LANES), lambda i: (0, 0)),
            pl.BlockSpec((ROW_TILE, ROW_TILE), lambda i: (0, 0)),
        ],
        out_specs=(pl.BlockSpec((N_EXPERTS, ROW_TILE), lambda i: (0, i)),
                   pl.BlockSpec((N_EXPERTS, ROW_TILE), lambda i: (0, i)),
                   pl.BlockSpec((ROW_TILE, d // 2), lambda i: (i, 0)),
                   rows),
        scratch_shapes=[pltpu.VMEM((1, LANES), F32)],
        compiler_params=_params("arbitrary"),
        name="router",
    )(x, o, w_o, gain.reshape(1, d), r, tri)


def _sc_mesh():
    return plsc.VectorSubcoreMesh(core_axis_name="core", subcore_axis_name="subcore",
                                  num_cores=SC_CORES, num_subcores=SC_SUBCORES)


def _sc_worker_base(per_worker):
    return (lax.axis_index("subcore") * SC_CORES + lax.axis_index("core")) * per_worker


def _sc_row_gather(table, idx):
    width = table.shape[1]
    n = idx.shape[0]
    per_worker = n // (SC_CORES * SC_SUBCORES)
    n_chunks = per_worker // SC_CHUNK_ROWS
    assert n == n_chunks * SC_CHUNK_ROWS * SC_CORES * SC_SUBCORES and n_chunks % 2 == 0

    @functools.partial(
        pl.kernel, mesh=_sc_mesh(), out_type=jax.ShapeDtypeStruct((n, width), table.dtype),
        scratch_types=[pltpu.VMEM((SC_CHUNK_ROWS,), jnp.int32), pltpu.VMEM((SC_CHUNK_ROWS,), jnp.int32),
                       pltpu.VMEM((SC_CHUNK_ROWS, width), table.dtype),
                       pltpu.VMEM((SC_CHUNK_ROWS, width), table.dtype),
                       pltpu.SemaphoreType.DMA, pltpu.SemaphoreType.DMA],
        name="sc_row_gather")
    def gather(table_hbm, idx_hbm, out_hbm, idx_a, idx_b, rows_a, rows_b, sem_a, sem_b):
        base = _sc_worker_base(per_worker)
        bufs = ((idx_a, rows_a, sem_a), (idx_b, rows_b, sem_b))

        def rows_of(c):
            return pl.ds(pl.multiple_of(base + c * SC_CHUNK_ROWS, SC_CHUNK_ROWS), SC_CHUNK_ROWS)

        def fetch(c, buf):
            idx_v, rows_v, sem = buf
            pltpu.sync_copy(idx_hbm.at[rows_of(c)], idx_v)
            return pltpu.make_async_copy(table_hbm.at[idx_v], rows_v, sem)

        fetch(0, bufs[0]).start()

        @pl.loop(0, n_chunks, step=2)
        def _(c):
            for b in range(2):
                idx_v, rows_v, sem = bufs[b]
                pltpu.make_async_copy(table_hbm.at[idx_v], rows_v, sem).wait()

                @pl.when(c + b + 1 < n_chunks)
                def _():
                    fetch(c + b + 1, bufs[1 - b]).start()

                pltpu.sync_copy(rows_v, out_hbm.at[rows_of(c + b)])

    return gather(table, idx)


def _sc_row_scatter_pair(rows, idx_lo, idx_hi, n_out):
    n, width = rows.shape
    per_worker = n // (SC_CORES * SC_SUBCORES)
    n_chunks = per_worker // SC_CHUNK_ROWS
    assert n == n_chunks * SC_CHUNK_ROWS * SC_CORES * SC_SUBCORES and n_chunks % 2 == 0
    index_scratch = pltpu.VMEM((SC_CHUNK_ROWS,), jnp.int32)
    rows_scratch = pltpu.VMEM((SC_CHUNK_ROWS, width), rows.dtype)

    @functools.partial(
        pl.kernel, mesh=_sc_mesh(), out_type=jax.ShapeDtypeStruct((n_out, width), rows.dtype),
        scratch_types=[index_scratch, index_scratch, rows_scratch, rows_scratch,
                       pltpu.SemaphoreType.DMA, pltpu.SemaphoreType.DMA],
        name="sc_row_scatter")
    def scatter(rows_hbm, lo_hbm, hi_hbm, out_hbm, lo_v, hi_v, rows_a, rows_b, sem_a, sem_b):
        base = _sc_worker_base(per_worker)
        bufs = ((rows_a, sem_a), (rows_b, sem_b))

        def rows_of(c):
            return pl.ds(pl.multiple_of(base + c * SC_CHUNK_ROWS, SC_CHUNK_ROWS), SC_CHUNK_ROWS)

        def load(c, buf):
            rows_v, sem = buf
            return pltpu.make_async_copy(rows_hbm.at[rows_of(c)], rows_v, sem)

        load(0, bufs[0]).start()

        @pl.loop(0, n_chunks, step=2)
        def _(c):
            for b in range(2):
                rows_v, _ = bufs[b]
                load(c + b, bufs[b]).wait()

                @pl.when(c + b + 1 < n_chunks)
                def _():
                    load(c + b + 1, bufs[1 - b]).start()

                pltpu.sync_copy(lo_hbm.at[rows_of(c + b)], lo_v)
                pltpu.sync_copy(hi_hbm.at[rows_of(c + b)], hi_v)
                pltpu.sync_copy(rows_v, out_hbm.at[lo_v])
                pltpu.sync_copy(rows_v, out_hbm.at[hi_v])

    return scatter(rows, idx_lo, idx_hi)


def _expert_ffn_body(te_ref, nu_ref, nv_ref, h_ref, wg_ref, wu_ref, wd_ref, y_ref, acc_scr, *, n_f):
    i = pl.program_id(0)
    f = pl.program_id(1)

    @pl.when(i < nu_ref[0])
    def _():
        row = lax.broadcasted_iota(jnp.int32, h_ref.shape, 0)
        packed = jnp.where(row < nv_ref[i], h_ref[...], 0)
        y = _swiglu_down(_unpack_bf16_pairs(packed), wg_ref, wu_ref, wd_ref)
        if n_f == 1:
            y_ref[...] = _pack_bf16_pairs(y)
        else:
            @pl.when(f == 0)
            def _():
                acc_scr[...] = y

            @pl.when((f > 0) & (f < n_f - 1))
            def _():
                acc_scr[...] += y

            @pl.when(f == n_f - 1)
            def _():
                y_ref[...] = _pack_bf16_pairs(acc_scr[...] + y)


def _expert_ffn(h_sorted, tile_expert, n_used, tile_valid, w_gate, w_up, w_down, layer):
    rows, half = h_sorted.shape
    d = 2 * half
    ff = w_gate.shape[3]
    n_f = ff // MOE_FF_TILE

    def row_map(i, f, te, nu, nv):
        return (jnp.minimum(i, nu[0] - 1), 0)

    def col_step(i, f, nu):
        return jnp.where(i < nu[0], f, n_f - 1)

    grid_spec = pltpu.PrefetchScalarGridSpec(
        num_scalar_prefetch=3,
        grid=(rows // MOE_ROW_TILE, n_f),
        in_specs=[
            pl.BlockSpec((MOE_ROW_TILE, half), row_map),
            pl.BlockSpec((None, None, d, MOE_FF_TILE),
                         lambda i, f, te, nu, nv: (layer, te[i], 0, col_step(i, f, nu))),
            pl.BlockSpec((None, None, d, MOE_FF_TILE),
                         lambda i, f, te, nu, nv: (layer, te[i], 0, col_step(i, f, nu))),
            pl.BlockSpec((None, None, MOE_FF_TILE, d),
                         lambda i, f, te, nu, nv: (layer, te[i], col_step(i, f, nu), 0)),
        ],
        out_specs=pl.BlockSpec((MOE_ROW_TILE, half), row_map),
        scratch_shapes=[pltpu.VMEM((MOE_ROW_TILE, d), F32)],
    )
    return pl.pallas_call(
        functools.partial(_expert_ffn_body, n_f=n_f),
        out_shape=jax.ShapeDtypeStruct((rows, half), jnp.int32),
        grid_spec=grid_spec,
        compiler_params=_params("arbitrary", "arbitrary"),
        name="expert_ffn",
    )(tile_expert, n_used, tile_valid, h_sorted, w_gate, w_up, w_down)


def _combine_body(x_ref, y_ref, g_ref, o_ref):
    out = x_ref[...]
    for s in range(2):
        column = jnp.broadcast_to(g_ref[s:s + 1, :], (LANES, g_ref.shape[1])).T
        gate = jnp.concatenate([column] * (out.shape[1] // LANES), axis=1)
        out = out + _unpack_bf16_pairs(y_ref[s]).astype(F32) * gate
    o_ref[...] = out


def _combine_residual(x, y_pairs, gates2):
    t, d = x.shape
    return pl.pallas_call(
        _combine_body,
        out_shape=jax.ShapeDtypeStruct((t, d), F32),
        grid=(t // ROW_TILE,),
        in_specs=[
            pl.BlockSpec((ROW_TILE, d), lambda i: (i, 0)),
            pl.BlockSpec((2, ROW_TILE, d // 2), lambda i: (0, i, 0)),
            pl.BlockSpec((2, ROW_TILE), lambda i: (0, i)),
        ],
        out_specs=pl.BlockSpec((ROW_TILE, d), lambda i: (i, 0)),
        compiler_params=_params("parallel"),
        name="moe_combine",
    )(x, y_pairs, gates2)


def _mix_moe_residual(x, o, w_o, mix_layer, gain, router, w_gate, w_up, w_down, layer):
    t, d = x.shape
    gates, rank, h_packed, x = _mix_router(x, o, w_o, mix_layer, gain, router)
    rank8 = rank.astype(jnp.int32)
    chosen = rank8 >= 0
    counts = jnp.sum(chosen, axis=1, dtype=jnp.int32)
    padded = (counts + MOE_ROW_TILE - 1) // MOE_ROW_TILE * MOE_ROW_TILE
    ends = jnp.cumsum(padded)
    starts = ends - padded
    pos = starts[:, None] + rank8
    max_rows = 2 * t + N_EXPERTS * MOE_ROW_TILE
    pos_lo = jnp.min(jnp.where(chosen, pos, max_rows), axis=0)
    pos_hi = jnp.max(jnp.where(chosen, pos, -1), axis=0)
    gates2 = jnp.stack([jnp.sum(jnp.where(chosen & (pos == pos_lo[None, :]), gates, 0.0), axis=0),
                        jnp.sum(jnp.where(chosen & (pos == pos_hi[None, :]), gates, 0.0), axis=0)])
    n_tiles = max_rows // MOE_ROW_TILE
    n_used = (ends[-1] // MOE_ROW_TILE).astype(jnp.int32)
    tile_start = jnp.minimum(jnp.arange(n_tiles, dtype=jnp.int32), n_used - 1) * MOE_ROW_TILE
    tile_expert = jnp.sum(tile_start[:, None] >= ends[None, :], axis=1, dtype=jnp.int32)
    tile_valid = jnp.clip((starts + counts)[tile_expert] - tile_start, 0, MOE_ROW_TILE).astype(jnp.int32)

    h_sorted = _sc_row_scatter_pair(h_packed, pos_lo, pos_hi, max_rows)
    y_sorted = _expert_ffn(h_sorted, tile_expert, n_used.reshape(1), tile_valid, w_gate, w_up, w_down, layer)
    y_pairs = _sc_row_gather(y_sorted, jnp.concatenate([pos_lo, pos_hi])).reshape(2, t, d // 2)
    return _combine_residual(x, y_pairs, gates2)


def _split_head_pair(q):
    is_first = lax.broadcasted_iota(jnp.int32, (1, LANES), 1) < HEAD_DIM
    zero = jnp.zeros_like(q)
    return jnp.where(is_first, q, zero), jnp.where(is_first, zero, q)


def _qk(q, k):
    return lax.dot_general(q, k, (((1,), (1,)), ((), ())), preferred_element_type=F32)


def _transpose_values(v_ref, vt_scr):
    for c in range(v_ref.shape[0] // ATT_BLOCK):
        rows = slice(c * ATT_BLOCK, (c + 1) * ATT_BLOCK)
        vt_scr[:, rows] = v_ref[rows, :].astype(F32).T.astype(vt_scr.dtype)


def _sb_body(q_ref, k_ref, v_ref, u_ref, o_ref, vt_scr, acc_scr, carry_scr, z_scr, keep_scr, sum_scr):
    qi = pl.program_id(2)

    @pl.when(qi == 0)
    def _():
        _transpose_values(v_ref, vt_scr)

    upper = u_ref[...]
    acc_scr[...] = jnp.zeros_like(acc_scr)
    carry_scr[...] = jnp.zeros_like(carry_scr)
    key = lax.broadcasted_iota(jnp.int32, (ATT_BLOCK, ATT_BLOCK), 0)
    query = lax.broadcasted_iota(jnp.int32, (ATT_BLOCK, ATT_BLOCK), 1)
    strict = key < query
    qs = [_split_head_pair(q_ref[j * ATT_BLOCK:(j + 1) * ATT_BLOCK, :]) for j in range(SB_SUB_BLOCKS)]

    def process(j, kb, keep):
        start = pl.multiple_of(kb * ATT_BLOCK, ATT_BLOCK)
        k = k_ref[pl.ds(start, ATT_BLOCK), :]
        for hh in range(2):
            z = _qk(k, qs[j][hh])
            log_beta = z - _softplus(z)
            log_keep = log_beta - z
            if keep is not None:
                log_keep = jnp.where(keep, log_keep, 0.0)
            remain = jnp.dot(upper, log_keep.astype(BF16), preferred_element_type=F32)
            w = jnp.exp(log_beta + remain + carry_scr[j, hh])
            if keep is not None:
                w = jnp.where(keep, w, 0.0)
            carry_scr[j, hh] += jnp.sum(log_keep, axis=0, keepdims=True)
            vt = vt_scr[pl.ds(hh * HEAD_DIM, HEAD_DIM), pl.ds(start, ATT_BLOCK)]
            acc_scr[j, hh] += jnp.dot(vt, w.astype(BF16), preferred_element_type=F32)

    units = []
    for j in range(SB_SUB_BLOCKS):
        g = qi * SB_SUB_BLOCKS + j
        has_previous = None if j > 0 else jnp.broadcast_to(g > 0, strict.shape)
        for kb, keep in ((g, strict), (jnp.maximum(g - 1, 0), has_previous)):
            for hh in range(2):
                units.append((j, hh, pl.multiple_of(kb * ATT_BLOCK, ATT_BLOCK), keep))

    for u, (j, hh, start, keep) in enumerate(units):
        z_scr[u] = _qk(k_ref[pl.ds(start, ATT_BLOCK), :], qs[j][hh])

    for u, (j, hh, start, keep) in enumerate(units):
        z = z_scr[u]
        log_beta = z - _softplus(z)
        log_keep = log_beta - z
        if keep is not None:
            log_keep = jnp.where(keep, log_keep, 0.0)
        z_scr[u] = log_beta
        keep_scr[u] = log_keep.astype(BF16)
        sum_scr[u] = jnp.sum(log_keep, axis=0, keepdims=True)

    for u in range(len(units)):
        z_scr[u] += jnp.dot(upper, keep_scr[u], preferred_element_type=F32)

    for u, (j, hh, start, keep) in enumerate(units):
        diagonal = u % 4 < 2
        log_w = z_scr[u] if diagonal else z_scr[u] + sum_scr[u - 2]
        w = jnp.exp(log_w)
        if keep is not None:
            w = jnp.where(keep, w, 0.0)
        keep_scr[u] = w.astype(BF16)

    for u, (j, hh, start, keep) in enumerate(units):
        vt = vt_scr[pl.ds(hh * HEAD_DIM, HEAD_DIM), pl.ds(start, ATT_BLOCK)]
        acc_scr[j, hh] += jnp.dot(vt, keep_scr[u], preferred_element_type=F32)
        carry_scr[j, hh] += sum_scr[u]

    def sweep_earlier():
        for j in range(SB_SUB_BLOCKS):
            def more(kb, j=j):
                return (kb >= 0) & (jnp.max(carry_scr[j]) > SB_UNDERFLOW_LOG)

            def step(kb, j=j):
                process(j, kb, None)
                return kb - 1

            lax.while_loop(more, step, qi * SB_SUB_BLOCKS + j - 2)

    pl.when(jnp.max(carry_scr[...]) > SB_UNDERFLOW_LOG)(sweep_earlier)

    for j in range(SB_SUB_BLOCKS):
        out_t = jnp.concatenate([acc_scr[j, 0], acc_scr[j, 1]], axis=0)
        o_ref[j * ATT_BLOCK:(j + 1) * ATT_BLOCK, :] = out_t.T.astype(o_ref.dtype)


def _sb_attention(proj, batch, seq):
    t = proj.shape[0]
    nq = seq // SB_Q_BLOCK
    idx = jnp.arange(ATT_BLOCK)
    upper = (idx[None, :] > idx[:, None]).astype(BF16)
    return pl.pallas_call(
        _sb_body,
        out_shape=jax.ShapeDtypeStruct((t, D_MODEL), BF16),
        grid=(batch, HEAD_PAIRS, nq),
        in_specs=[
            pl.BlockSpec((SB_Q_BLOCK, LANES), lambda b, p, i: (b * nq + i, p)),
            pl.BlockSpec((seq, LANES), lambda b, p, i: (b, HEAD_PAIRS + p)),
            pl.BlockSpec((seq, LANES), lambda b, p, i: (b, 2 * HEAD_PAIRS + p)),
            pl.BlockSpec((ATT_BLOCK, ATT_BLOCK), lambda b, p, i: (0, 0)),
        ],
        out_specs=pl.BlockSpec((SB_Q_BLOCK, LANES), lambda b, p, i: (b * nq + i, p)),
        scratch_shapes=[
            pltpu.VMEM((LANES, seq), BF16),
            pltpu.VMEM((SB_SUB_BLOCKS, 2, HEAD_DIM, ATT_BLOCK), F32),
            pltpu.VMEM((SB_SUB_BLOCKS, 2, 1, ATT_BLOCK), F32),
            pltpu.VMEM((4 * SB_SUB_BLOCKS, ATT_BLOCK, ATT_BLOCK), F32),
            pltpu.VMEM((4 * SB_SUB_BLOCKS, ATT_BLOCK, ATT_BLOCK), BF16),
            pltpu.VMEM((4 * SB_SUB_BLOCKS, 1, ATT_BLOCK), F32),
        ],
        compiler_params=_params("parallel", "parallel", "arbitrary"),
        name="sb_attention",
    )(proj, proj, proj, upper)


def _fox_gate_body(x_ref, g_ref, w_ref, b_ref, tri_ref, c_ref):
    h = _rms_normalize(x_ref[...], g_ref[...])
    logits = _dot_split(h, w_ref[...])
    log_f = -_softplus(-(logits + b_ref[...]))
    seq = log_f.shape[0]
    carry = jnp.zeros((1, LANES), F32)
    for blk in range(seq // ATT_BLOCK):
        rows = slice(blk * ATT_BLOCK, (blk + 1) * ATT_BLOCK)
        c = carry
        for part in _split_bf16(log_f[rows], 3):
            c = c + jnp.dot(tri_ref[...], part, preferred_element_type=F32)
        c_ref[rows, :] = c
        carry = c[ATT_BLOCK - 1:ATT_BLOCK, :]


def _fox_cum_log_forget(x, gain, w_gate, b_gate, batch, seq):
    t, d = x.shape
    w = jnp.zeros((d, LANES), F32).at[:, :N_HEADS].set(w_gate)
    b = jnp.zeros((1, LANES), F32).at[0, :N_HEADS].set(b_gate)
    idx = jnp.arange(ATT_BLOCK)
    tri = (idx[:, None] >= idx[None, :]).astype(BF16)
    return pl.pallas_call(
        _fox_gate_body,
        out_shape=jax.ShapeDtypeStruct((t, LANES), F32),
        grid=(batch,),
        in_specs=[
            pl.BlockSpec((seq, d), lambda i: (i, 0)),
            pl.BlockSpec((1, d), lambda i: (0, 0)),
            pl.BlockSpec((d, LANES), lambda i: (0, 0)),
            pl.BlockSpec((1, LANES), lambda i: (0, 0)),
            pl.BlockSpec((ATT_BLOCK, ATT_BLOCK), lambda i: (0, 0)),
        ],
        out_specs=pl.BlockSpec((seq, LANES), lambda i: (i, 0)),
        compiler_params=_params("parallel"),
        name="fox_gate",
    )(x, gain.reshape(1, d), w, b, tri)


def _fox_body(q_ref, k_ref, v_ref, cq_ref, ck_ref, o_ref, vt_scr, ckb_scr, a_scr, p_scr):
    _transpose_values(v_ref, vt_scr)
    for hh in range(2):
        for c in range(ckb_scr.shape[1] // ATT_BLOCK):
            rows = slice(c * ATT_BLOCK, (c + 1) * ATT_BLOCK)
            ckb_scr[hh, rows, :] = jnp.broadcast_to(ck_ref[0, hh, :, rows], (LANES, ATT_BLOCK)).T

    key = lax.broadcasted_iota(jnp.int32, (ATT_BLOCK, ATT_Q_BLOCK), 0)
    query = lax.broadcasted_iota(jnp.int32, (ATT_BLOCK, ATT_Q_BLOCK), 1)
    row_shape = (1, ATT_Q_BLOCK)

    def sweep(q, first_slot):
        n_blocks = (q + 1) * ATT_K_PER_Q
        q_rows = slice(q * ATT_Q_BLOCK, (q + 1) * ATT_Q_BLOCK)
        qs = _split_head_pair(q_ref[q_rows, :])
        tops = [jnp.full(row_shape, NEG_INF, F32)] * 2
        for kb in range(n_blocks):
            k_rows = slice(kb * ATT_BLOCK, (kb + 1) * ATT_BLOCK)
            k = k_ref[k_rows, :]
            for hh in range(2):
                a = _qk(k, qs[hh]) - jnp.concatenate([ckb_scr[hh, k_rows, :]] * (ATT_Q_BLOCK // LANES), axis=1)
                if kb >= n_blocks - ATT_K_PER_Q:
                    a = jnp.where(key + (kb * ATT_BLOCK - q * ATT_Q_BLOCK) <= query, a, NEG_INF)
                a_scr[hh, first_slot + kb] = a
                tops[hh] = jnp.maximum(tops[hh], jnp.max(a, axis=0, keepdims=True))
        shifts = [cq_ref[0, hh, :, q_rows] - (tops[hh] + cq_ref[0, hh, :, q_rows]) for hh in range(2)]
        sums = [jnp.zeros(row_shape, F32)] * 2
        for kb in range(n_blocks):
            for hh in range(2):
                p = jnp.exp(a_scr[hh, first_slot + kb] + shifts[hh])
                p_scr[hh, first_slot + kb] = p.astype(BF16)
                sums[hh] = sums[hh] + jnp.sum(p, axis=0, keepdims=True)
        accs = [jnp.zeros((HEAD_DIM, ATT_Q_BLOCK), F32)] * 2
        for kb in range(n_blocks):
            for hh in range(2):
                vt = vt_scr[hh * HEAD_DIM:(hh + 1) * HEAD_DIM, kb * ATT_BLOCK:(kb + 1) * ATT_BLOCK]
                accs[hh] = accs[hh] + jnp.dot(vt, p_scr[hh, first_slot + kb], preferred_element_type=F32)
        out_t = jnp.concatenate([accs[0] / sums[0], accs[1] / sums[1]], axis=0)
        o_ref[q_rows, :] = out_t.T.astype(o_ref.dtype)

    first_slot = 0
    for q in range(k_ref.shape[0] // ATT_Q_BLOCK):
        sweep(q, first_slot)
        first_slot += (q + 1) * ATT_K_PER_Q


def _fox_attention(proj, cum, batch, seq):
    t = proj.shape[0]
    nq = seq // ATT_Q_BLOCK
    cum_h = cum[:, :N_HEADS].reshape(batch, seq, N_HEADS).transpose(0, 2, 1)
    cum_rows = cum_h.reshape(batch, N_HEADS, 1, seq)
    n_slots = sum((q + 1) * ATT_K_PER_Q for q in range(nq))
    return pl.pallas_call(
        _fox_body,
        out_shape=jax.ShapeDtypeStruct((t, D_MODEL), BF16),
        grid=(batch, HEAD_PAIRS),
        in_specs=[
            pl.BlockSpec((seq, LANES), lambda b, p: (b, p)),
            pl.BlockSpec((seq, LANES), lambda b, p: (b, HEAD_PAIRS + p)),
            pl.BlockSpec((seq, LANES), lambda b, p: (b, 2 * HEAD_PAIRS + p)),
            pl.BlockSpec((1, 2, 1, seq), lambda b, p: (b, p, 0, 0)),
            pl.BlockSpec((1, 2, 1, seq), lambda b, p: (b, p, 0, 0)),
        ],
        out_specs=pl.BlockSpec((seq, LANES), lambda b, p: (b, p)),
        scratch_shapes=[
            pltpu.VMEM((LANES, seq), BF16),
            pltpu.VMEM((2, seq, LANES), F32),
            pltpu.VMEM((2, n_slots, ATT_BLOCK, ATT_Q_BLOCK), F32),
            pltpu.VMEM((2, n_slots, ATT_BLOCK, ATT_Q_BLOCK), BF16),
        ],
        compiler_params=_params("parallel", "parallel"),
        name="fox_attention",
    )(proj, proj, proj, cum_rows, cum_rows)


def _band_body(q_ref, k_ref, v_ref, bias_ref, o_ref, s_ref, vt_scr, logit_scr, p_scr, inv_scr, *, sub_len):
    seq = v_ref.shape[0]
    n_blocks = seq // DIL_BLOCK
    _transpose_values(v_ref, vt_scr)

    def key_rows(n):
        first = (n * DIL_BLOCK) % sub_len == 0
        return first, slice((n if first else n - 1) * DIL_BLOCK, (n + 1) * DIL_BLOCK)

    for n in range(n_blocks):
        first, k_rows = key_rows(n)
        qs = _split_head_pair(q_ref[n * DIL_BLOCK:(n + 1) * DIL_BLOCK, :])
        k = k_ref[k_rows, :]
        for hh in range(2):
            bias = bias_ref[hh, DIL_BLOCK:, :] if first else bias_ref[hh]
            logit_scr[2 * n + hh, :k.shape[0], :] = _qk(k, qs[hh]) + bias

    for n in range(n_blocks):
        first, k_rows = key_rows(n)
        n_keys = k_rows.stop - k_rows.start
        for hh in range(2):
            logits = logit_scr[2 * n + hh, :n_keys, :]
            m = jnp.max(logits, axis=0, keepdims=True)
            p = jnp.exp(logits - m)
            l = jnp.sum(p, axis=0, keepdims=True)
            p_scr[2 * n + hh, :n_keys, :] = p.astype(BF16)
            inv_scr[2 * n + hh] = 1.0 / l
            s_ref[0, 0, hh:hh + 1, n * DIL_BLOCK:(n + 1) * DIL_BLOCK] = m + jnp.log(l)

    for n in range(n_blocks):
        first, k_rows = key_rows(n)
        n_keys = k_rows.stop - k_rows.start
        outs = []
        for hh in range(2):
            vt = vt_scr[hh * HEAD_DIM:(hh + 1) * HEAD_DIM, k_rows]
            o = jnp.dot(vt, p_scr[2 * n + hh, :n_keys, :], preferred_element_type=F32)
            outs.append(o * inv_scr[2 * n + hh])
        o_ref[n * DIL_BLOCK:(n + 1) * DIL_BLOCK, :] = jnp.concatenate(outs, axis=0).T.astype(o_ref.dtype)


def _band_groups_body(*refs, sub_lens):
    n = len(sub_lens)
    ins, outs, scratch = refs[:4 * n], refs[4 * n:6 * n], refs[6 * n:]
    for g, sub_len in enumerate(sub_lens):
        _band_body(*ins[4 * g:4 * g + 4], *outs[2 * g:2 * g + 2], *scratch[4 * g:4 * g + 4], sub_len=sub_len)


def _band_attention(qk, v, bias_t, batch, seq, sub_lens):
    t = v[0].shape[0]
    n = len(sub_lens)
    in_specs, operands, out_shape, out_specs, scratch = [], [], [], [], []
    for g in range(n):
        in_specs += [
            pl.BlockSpec((None, seq, LANES), lambda b, p: (0, b, p)),
            pl.BlockSpec((None, seq, LANES), lambda b, p: (1, b, p)),
            pl.BlockSpec((seq, LANES), lambda b, p: (b, p)),
            pl.BlockSpec((None, 2, 2 * DIL_BLOCK, DIL_BLOCK), lambda b, p, g=g: (g, p, 0, 0)),
        ]
        operands += [qk[g], qk[g], v[g], bias_t]
        out_shape += [jax.ShapeDtypeStruct((t, D_MODEL), BF16),
                      jax.ShapeDtypeStruct((batch, HEAD_PAIRS, 2, seq), F32)]
        out_specs += [pl.BlockSpec((seq, LANES), lambda b, p: (b, p)),
                      pl.BlockSpec((1, 1, 2, seq), lambda b, p: (b, p, 0, 0))]
        scratch += [pltpu.VMEM((LANES, seq), BF16),
                    pltpu.VMEM((2 * seq // DIL_BLOCK, 2 * DIL_BLOCK, DIL_BLOCK), F32),
                    pltpu.VMEM((2 * seq // DIL_BLOCK, 2 * DIL_BLOCK, DIL_BLOCK), BF16),
                    pltpu.VMEM((2 * seq // DIL_BLOCK, 1, DIL_BLOCK), F32)]
    results = pl.pallas_call(
        functools.partial(_band_groups_body, sub_lens=tuple(sub_lens)),
        out_shape=tuple(out_shape),
        grid=(batch, HEAD_PAIRS),
        in_specs=in_specs,
        out_specs=tuple(out_specs),
        scratch_shapes=scratch,
        compiler_params=_params("parallel", "parallel"),
        name="band_attention",
    )(*operands)
    return [(results[2 * g], results[2 * g + 1]) for g in range(n)]


def _softmax_merge_body(o1_ref, o2_ref, o3_ref, s_ref, e_ref, o_ref):
    s = s_ref[...]
    groups = [s, pltpu.roll(s, LANES - N_HEADS, axis=1), pltpu.roll(s, LANES - 2 * N_HEADS, axis=1)]
    top = jnp.maximum(jnp.maximum(groups[0], groups[1]), groups[2])
    weights = [jnp.exp(g - top) for g in groups]
    inv = 1.0 / (weights[0] + weights[1] + weights[2])
    out = jnp.zeros(o_ref.shape, F32)
    for w, part in zip(weights, (o1_ref, o2_ref, o3_ref)):
        spread = jnp.dot((w * inv).astype(BF16), e_ref[...], preferred_element_type=F32)
        out = out + spread * part[...].astype(F32)
    o_ref[...] = out.astype(o_ref.dtype)


def _softmax_merge(outs, s_all):
    t, d = outs[0].shape
    head_of = jnp.arange(d) // HEAD_DIM
    expand = (jnp.arange(LANES)[:, None] == head_of[None, :]).astype(BF16)
    rows = pl.BlockSpec((ROW_TILE, d), lambda i: (i, 0))
    return pl.pallas_call(
        _softmax_merge_body,
        out_shape=jax.ShapeDtypeStruct((t, d), BF16),
        grid=(t // ROW_TILE,),
        in_specs=[rows, rows, rows,
                  pl.BlockSpec((ROW_TILE, LANES), lambda i: (i, 0)),
                  pl.BlockSpec((LANES, d), lambda i: (0, 0))],
        out_specs=rows,
        compiler_params=_params("parallel"),
        name="softmax_merge",
    )(*outs, s_all, expand)


def _t5_causal_bucket(distance):
    max_exact = N_REL_BUCKETS // 2
    d = jnp.maximum(distance, 1).astype(F32)
    log_b = max_exact + (jnp.log(d / max_exact) / math.log(REL_MAX_DISTANCE / max_exact)
                         * (N_REL_BUCKETS - max_exact)).astype(jnp.int32)
    log_b = jnp.minimum(log_b, N_REL_BUCKETS - 1)
    return jnp.where(distance < max_exact, distance, log_b)


def _dilated_bias(rel_bias):
    kj = jnp.arange(2 * DIL_BLOCK, dtype=jnp.int32)
    qi = jnp.arange(DIL_BLOCK, dtype=jnp.int32)
    delta = qi[None, :] + DIL_BLOCK - kj[:, None]
    in_band = (delta >= 0) & (delta <= DIL_SPAN)
    buckets = jnp.stack([_t5_causal_bucket(jnp.maximum(delta, 0) * dil) for _, dil in DILATED_PAIRS])
    one_hot = (buckets[..., None] == jnp.arange(N_REL_BUCKETS)).astype(F32)
    bias = jnp.einsum("gkqb,bh->ghkq", one_hot, rel_bias.astype(F32), precision=lax.Precision.HIGHEST)
    return jnp.where(in_band[None, None], bias, NEG_INF)


def _dilated_proj_body(x_ref, g_ref, w_ref, cs_ref, bd_ref, *rest):
    n_groups = len(DILATED_PAIRS)
    qk_refs, v_refs = rest[:n_groups], rest[n_groups:2 * n_groups]
    h_scr, res_scr = rest[2 * n_groups:]
    j = pl.program_id(1)

    @pl.when(j == 0)
    def _():
        h_scr[...] = _rms_normalize(x_ref[...], g_ref[...]).astype(BF16)

    acc = jnp.dot(h_scr[...], w_ref[...], preferred_element_type=F32)

    def keep(res, first_lane):
        for c in range(res.shape[1] // LANES):
            res_scr[first_lane // LANES + c] = res[:, c * LANES:(c + 1) * LANES]

    @pl.when(j < 2 * n_groups)
    def _():
        for c in range(COL_TILE // NORM_CHUNK):
            sl = slice(c * NORM_CHUNK, (c + 1) * NORM_CHUNK)
            a = acc[:, sl]
            ss = jnp.dot((a * a).astype(BF16), bd_ref[...], preferred_element_type=F32)
            keep(a * lax.rsqrt(ss * (1.0 / HEAD_DIM) + RMS_EPS) * cs_ref[:, sl], c * NORM_CHUNK)

    @pl.when(j == 2 * n_groups)
    def _():
        keep(acc * cs_ref[...], 0)

    def write_classes(ref, dil):
        rows = res_scr.shape[1] // dil
        for c in range(res_scr.shape[0]):
            lanes = slice(c * LANES, (c + 1) * LANES)
            if dil == 1:
                ref[:, lanes] = res_scr[c].astype(ref.dtype)
            else:
                for r in range(dil):
                    ref[r, :, lanes] = res_scr[c, pl.ds(r, rows, stride=dil), :].astype(ref.dtype)

    for g, (_, dil) in enumerate(DILATED_PAIRS):
        pl.when(j // 2 == g)(functools.partial(write_classes, qk_refs[g], dil))
        pl.when(j == 2 * n_groups)(functools.partial(write_classes, v_refs[g], dil))


def _dilated_proj(x, gain, w, layer, col_scale, batch, seq):
    t, d = x.shape
    n = w.shape[2]
    n_groups = len(DILATED_PAIRS)
    tiles_per_seq = seq // DIL_ROW_TILE
    head_id = jnp.arange(NORM_CHUNK) // HEAD_DIM
    block_diag = (head_id[:, None] == head_id[None, :]).astype(BF16)
    out_shape, out_specs = [], []
    for kind in ("qk", "v"):
        for g, (_, dil) in enumerate(DILATED_PAIRS):
            rows = DIL_ROW_TILE // dil
            if kind == "qk":
                shape = (2, batch, dil, seq // dil, d)
                block = (None, None, dil, rows, d)
                index = lambda i, j, g=g: (jnp.clip(j - 2 * g, 0, 1), i // tiles_per_seq, 0, i % tiles_per_seq, 0)
            else:
                shape = (batch, dil, seq // dil, d)
                block = (None, dil, rows, d)
                index = lambda i, j: (i // tiles_per_seq, 0, i % tiles_per_seq, 0)
            if dil == 1:
                block = block[:-3] + (None,) + block[-2:]
            out_shape.append(jax.ShapeDtypeStruct(shape, BF16))
            out_specs.append(pl.BlockSpec(block, index))
    outs = pl.pallas_call(
        _dilated_proj_body,
        out_shape=tuple(out_shape),
        grid=(t // DIL_ROW_TILE, n // COL_TILE),
        in_specs=[
            pl.BlockSpec((DIL_ROW_TILE, d), lambda i, j: (i, 0)),
            pl.BlockSpec((1, d), lambda i, j: (0, 0)),
            pl.BlockSpec((None, d, COL_TILE), lambda i, j: (layer, 0, j)),
            pl.BlockSpec((1, COL_TILE), lambda i, j: (0, j)),
            pl.BlockSpec((NORM_CHUNK, NORM_CHUNK), lambda i, j: (0, 0)),
        ],
        out_specs=tuple(out_specs),
        scratch_shapes=[pltpu.VMEM((DIL_ROW_TILE, d), BF16),
                        pltpu.VMEM((COL_TILE // LANES, DIL_ROW_TILE, LANES), F32)],
        compiler_params=_params("parallel", "arbitrary"),
        name="dilated_proj",
    )(x, gain.reshape(1, d), w, col_scale.reshape(1, n).astype(F32), block_diag)
    qk = [o.reshape(2, t, d) for o in outs[:n_groups]]
    v = [o.reshape(t, d) for o in outs[n_groups:]]
    return qk, v


def _dilated_attention(qk, v, rel_bias, batch, seq):
    t = v[0].shape[0]
    n_groups = len(DILATED_PAIRS)
    bias_t = _dilated_bias(rel_bias)
    outs, stats = [], []
    groups = _band_attention(qk, v, bias_t, batch, seq, [seq // dil for _, dil in DILATED_PAIRS])
    for (o, s), (_, dil) in zip(groups, DILATED_PAIRS):
        sub_len = seq // dil
        o = o.reshape(batch, dil, sub_len, D_MODEL).transpose(0, 2, 1, 3).reshape(t, D_MODEL)
        s = s.reshape(batch, HEAD_PAIRS, 2, dil, sub_len).swapaxes(3, 4)
        outs.append(o)
        stats.append(s.reshape(batch, N_HEADS, seq).transpose(0, 2, 1).reshape(t, N_HEADS))
    stats.append(jnp.zeros((t, LANES - n_groups * N_HEADS), F32))
    return _softmax_merge(outs, jnp.concatenate(stats, axis=1))


def _tile_heads(v):
    return jnp.tile(v.astype(F32), N_HEADS)


def _sb_mixer(x, gain, w_qkv, layer, batch, seq):
    ones = jnp.ones((D_MODEL,), F32)
    col_scale = jnp.concatenate([ones * QK_SCALE, ones, ones])
    proj = _norm_proj(x, gain, w_qkv, layer, col_scale, 0)
    return _sb_attention(proj, batch, seq)


def _dilated_mixer(x, gain, w_in, q_norm, k_norm, rel_bias, layer, batch, seq):
    scales = []
    for g in range(len(DILATED_PAIRS)):
        scales += [_tile_heads(q_norm[g]) * QK_SCALE, _tile_heads(k_norm[g])]
    scales.append(jnp.ones((D_MODEL,), F32))
    qk, v = _dilated_proj(x, gain, w_in, layer, jnp.concatenate(scales), batch, seq)
    return _dilated_attention(qk, v, rel_bias, batch, seq)


def _fox_mixer(x, gain, w_qkv, w_gate, b_f, q_norm, k_norm, layer, batch, seq):
    col_scale = jnp.concatenate([_tile_heads(q_norm) * QK_SCALE, _tile_heads(k_norm),
                                 jnp.ones((D_MODEL,), F32)])
    proj = _norm_proj(x, gain, w_qkv, layer, col_scale, 2)
    cum = _fox_cum_log_forget(x, gain, w_gate, b_f, batch, seq)
    return _fox_attention(proj, cum, batch, seq)


def kernel(x, sb_w_qkv, sb_w_o, dil_w_in, dil_q_norm, dil_k_norm, dil_w_o, fox_w_in, fox_b_f,
           fox_q_norm, fox_k_norm, fox_w_o, rel_bias, attn_norm, ffn_norm, mlp_w_gate, mlp_w_up,
           mlp_w_down, moe_router, moe_w_gate, moe_w_up, moe_w_down):
    batch, seq, d = x.shape
    depth = attn_norm.shape[0]
    assert d == D_MODEL and seq % max(ROW_TILE, DIL_ROW_TILE, ATT_Q_BLOCK, SB_Q_BLOCK) == 0
    assert all((seq // dil) % DIL_BLOCK == 0 for _, dil in DILATED_PAIRS)
    sb_w_qkv, sb_w_o, dil_w_in, dil_w_o, fox_w_o, mlp_w_gate, mlp_w_up, mlp_w_down = (
        w.astype(BF16) for w in (sb_w_qkv, sb_w_o, dil_w_in, dil_w_o, fox_w_o, mlp_w_gate, mlp_w_up,
                                 mlp_w_down))
    moe_w_gate, moe_w_up, moe_w_down = (w.astype(BF16) for w in (moe_w_gate, moe_w_up, moe_w_down))
    fox_w_qkv = fox_w_in[:, :, :3 * D_MODEL].astype(BF16)
    h = x.reshape(batch * seq, d)
    for i in range(depth):
        kind, j = i % 3, i // 3
        if kind == 0:
            mixed, w_o = _sb_mixer(h, attn_norm[i], sb_w_qkv, j, batch, seq), sb_w_o
        elif kind == 1:
            mixed, w_o = _dilated_mixer(h, attn_norm[i], dil_w_in, dil_q_norm[j], dil_k_norm[j], rel_bias,
                                        j, batch, seq), dil_w_o
        else:
            mixed, w_o = _fox_mixer(h, attn_norm[i], fox_w_qkv, fox_w_in[j, :, 3 * D_MODEL:], fox_b_f[j],
                                    fox_q_norm[j], fox_k_norm[j], j, batch, seq), fox_w_o
        f = i // 2
        if i % 2 == 0:
            h = _mix_ffn_residual(h, mixed, w_o, j, ffn_norm[i], mlp_w_gate, mlp_w_up, mlp_w_down, f)
        else:
            h = _mix_moe_residual(h, mixed, w_o, j, ffn_norm[i], moe_router[f], moe_w_gate, moe_w_up,
                                  moe_w_down, f)
    return h.reshape(batch, seq, d)
```
